```python
import math
import jax, jax.numpy as jnp
from jax import lax
import numpy as np

D_MODEL = 1024
BATCH = 16
SEQ = 2048
DEPTH = 4

HEAD_DIM = 64
NSA_HEADS = 6
NSA_KV_GROUPS = 2
NSA_REP = NSA_HEADS // NSA_KV_GROUPS
N_BRANCH = 3
CMP_LEN = 32
CMP_STRIDE = 16
CMP_HIDDEN = 2 * HEAD_DIM
SEL_BLOCK = 64
SEL_TOP_N = 16
WINDOW = 512
NSA_Q_BLOCK = 32
MLA_HEADS = 5
MLA_Q_RANK = 256
MLA_KV_RANK = 128
MLA_NOPE_DIM = 64
MLA_ROPE_DIM = 32
MLA_V_DIM = 64
ATTN_Q_BLOCK = 128
RET_HEADS = 5
RET_CHUNK = 128
ROPE_THETA = 500000.0
PARTIAL_ROPE_DIM = HEAD_DIM // 4
RET_THETA = 10000.0
D_FF = 4 * D_MODEL
NORM_EPS = 1e-6
NEG_INF = -1e30
FORCE_SCORE = 1e9

NSA_Q_W = NSA_HEADS * HEAD_DIM
NSA_KV_W = NSA_KV_GROUPS * HEAD_DIM
NSA_GATE_W = NSA_HEADS * N_BRANCH
MLA_OUT_W = MLA_HEADS * MLA_V_DIM
RET_W = RET_HEADS * HEAD_DIM
MIX_WIDTH = NSA_Q_W + MLA_OUT_W + RET_W
IN_SIZES = (NSA_Q_W, NSA_KV_W, NSA_KV_W, NSA_KV_W, NSA_KV_W, NSA_KV_W, NSA_KV_W, NSA_GATE_W,
            MLA_Q_RANK, MLA_KV_RANK, MLA_ROPE_DIM, RET_W, RET_W, RET_W, RET_W)
N_IN = sum(IN_SIZES)

kernel_name = 'hybrid_nsa_mla_retention'

F32 = jnp.float32


def in_offsets():
    return [int(o) for o in np.cumsum(IN_SIZES)[:-1]]


def rms_norm(x, gain):
    xf = x.astype(F32)
    y = xf * lax.rsqrt(jnp.mean(xf * xf, axis=-1, keepdims=True) + NORM_EPS) * gain.astype(F32)
    return y.astype(x.dtype)


def masked_softmax(s, mask):
    s = jnp.where(mask, s.astype(F32), NEG_INF)
    m = jnp.max(s, axis=-1, keepdims=True)
    p = jnp.exp(s - m) * mask
    return p / jnp.maximum(jnp.sum(p, axis=-1, keepdims=True), 1e-30)


def rope_tables(positions, dim, theta):
    inv = 1.0 / (theta ** (jnp.arange(0, dim, 2, dtype=F32) / dim))
    ang = positions.astype(F32)[..., None] * inv
    return jnp.cos(ang), jnp.sin(ang)


def apply_rope(x, cos, sin):
    half = x.shape[-1] // 2
    x1 = x[..., :half].astype(F32)
    x2 = x[..., half:].astype(F32)
    c = cos[:, :, None, :]
    s = sin[:, :, None, :]
    return jnp.concatenate([x1 * c - x2 * s, x2 * c + x1 * s], axis=-1).astype(x.dtype)


def partial_rope(x, cos, sin):
    return jnp.concatenate([apply_rope(x[..., :PARTIAL_ROPE_DIM], cos, sin), x[..., PARTIAL_ROPE_DIM:]], axis=-1)


def selection_overlap(n_cmp, n_sel):
    cs = np.arange(n_cmp) * CMP_STRIDE
    ce = cs + CMP_LEN
    ss = np.arange(n_sel) * SEL_BLOCK
    se = ss + SEL_BLOCK
    ov = np.clip(np.minimum(ce[:, None], se[None, :]) - np.maximum(cs[:, None], ss[None, :]), 0, None) / CMP_LEN
    return jnp.asarray(ov.astype(np.float32))


def compress_blocks(tok, tok_idx, pos_emb, w1, w2):
    b, _, g, dh = tok.shape
    blk = tok[:, tok_idx] + pos_emb[None, None, :, None, :]
    flat = blk.transpose(0, 1, 3, 2, 4).reshape(b, tok_idx.shape[0], g, CMP_LEN * dh)
    return jax.nn.gelu(flat @ w1) @ w2


def nsa_mixer(q, k_cmp, v_cmp, k_slc, v_slc, k_win, v_win, gates, pos_k, w1_k, w2_k, pos_v, w1_v, w2_v):
    b, s, _, dh = q.shape
    g, r = NSA_KV_GROUPS, NSA_REP
    n_cmp = (s - CMP_LEN) // CMP_STRIDE + 1
    n_sel = s // SEL_BLOCK
    top_n = min(SEL_TOP_N, n_sel)
    scale = dh ** -0.5
    qb_len = NSA_Q_BLOCK

    tok_idx = np.arange(n_cmp)[:, None] * CMP_STRIDE + np.arange(CMP_LEN)[None, :]
    kc = compress_blocks(k_cmp, tok_idx, pos_k, w1_k, w2_k)
    vc = compress_blocks(v_cmp, tok_idx, pos_v, w1_v, w2_v)
    cmp_end = jnp.arange(n_cmp) * CMP_STRIDE + CMP_LEN - 1
    overlap = selection_overlap(n_cmp, n_sel)

    ks_blk = k_slc.reshape(b, n_sel, SEL_BLOCK, g, dh).transpose(0, 3, 1, 2, 4)
    vs_blk = v_slc.reshape(b, n_sel, SEL_BLOCK, g, dh).transpose(0, 3, 1, 2, 4)
    kw_pad = jnp.pad(k_win, ((0, 0), (WINDOW, 0), (0, 0), (0, 0)))
    vw_pad = jnp.pad(v_win, ((0, 0), (WINDOW, 0), (0, 0), (0, 0)))

    n_qb = s // qb_len
    q_blocks = q.reshape(b, n_qb, qb_len, g, r, dh).transpose(1, 0, 2, 3, 4, 5)
    gate_blocks = jax.nn.sigmoid(gates.astype(F32)).reshape(b, n_qb, qb_len, g, r, N_BRANCH).transpose(1, 0, 2, 3, 4, 5)
    b_idx = jnp.arange(b)[:, None, None, None]
    g_idx = jnp.arange(g)[None, :, None, None]
    sel_ids = jnp.arange(n_sel)[None, :]

    def block(args):
        qb, gb, i = args
        q0 = i * qb_len
        t = q0 + jnp.arange(qb_len)
        s_c = jnp.einsum('bqgrd,bngd->bgrqn', qb, kc) * scale
        p_c = masked_softmax(s_c, cmp_end[None, :] <= t[:, None])
        o_c = jnp.einsum('bgrqn,bngd->bqgrd', p_c, vc)
        imp = jnp.einsum('bgrqn,nm->bgqm', p_c, overlap)
        cur = (t // SEL_BLOCK)[:, None]
        valid = sel_ids <= cur
        forced = (sel_ids == 0) | (sel_ids == cur) | (sel_ids == cur - 1)
        imp = jnp.where(valid & forced, FORCE_SCORE, imp)
        imp = jnp.where(valid, imp, NEG_INF)
        _, idx = lax.top_k(imp, top_n)
        k_sel = ks_blk[b_idx, g_idx, idx]
        v_sel = vs_blk[b_idx, g_idx, idx]
        key_pos = idx[..., None] * SEL_BLOCK + jnp.arange(SEL_BLOCK)
        mask_s = (key_pos <= t[None, None, :, None, None]).reshape(b, g, 1, qb_len, top_n * SEL_BLOCK)
        s_s = jnp.einsum('bqgrd,bgqjsd->bgrqjs', qb, k_sel).reshape(b, g, r, qb_len, top_n * SEL_BLOCK) * scale
        p_s = masked_softmax(s_s, mask_s).reshape(b, g, r, qb_len, top_n, SEL_BLOCK)
        o_s = jnp.einsum('bgrqjs,bgqjsd->bqgrd', p_s, v_sel)
        kw = lax.dynamic_slice_in_dim(kw_pad, q0, qb_len + WINDOW, axis=1)
        vw = lax.dynamic_slice_in_dim(vw_pad, q0, qb_len + WINDOW, axis=1)
        kp = (q0 - WINDOW + jnp.arange(qb_len + WINDOW))[None, :]
        mask_w = (kp <= t[:, None]) & (kp > t[:, None] - WINDOW) & (kp >= 0)
        s_w = jnp.einsum('bqgrd,bkgd->bgrqk', qb, kw) * scale
        p_w = masked_softmax(s_w, mask_w)
        o_w = jnp.einsum('bgrqk,bkgd->bqgrd', p_w, vw)
        return gb[..., 0:1] * o_c + gb[..., 1:2] * o_s + gb[..., 2:3] * o_w

    out = lax.map(block, (q_blocks, gate_blocks, jnp.arange(n_qb)))
    return out.transpose(1, 0, 2, 3, 4, 5).reshape(b, s, NSA_HEADS * dh)


def causal_block_attention(q, k, v, scale):
    b, s, h, dq = q.shape
    n_qb = s // ATTN_Q_BLOCK
    q_blocks = q.reshape(b, n_qb, ATTN_Q_BLOCK, h, dq).transpose(1, 0, 2, 3, 4)
    kp = jnp.arange(s)[None, :]

    def block(args):
        qi, i = args
        t = (i * ATTN_Q_BLOCK + jnp.arange(ATTN_Q_BLOCK))[:, None]
        sc = jnp.einsum('bqhd,bkhd->bhqk', qi, k) * scale
        p = masked_softmax(sc, kp <= t)
        return jnp.einsum('bhqk,bkhd->bqhd', p, v)

    out = lax.map(block, (q_blocks, jnp.arange(n_qb)))
    return out.transpose(1, 0, 2, 3, 4).reshape(b, s, h, v.shape[-1])


def mla_mixer(c_q, c_kv, k_pe, q_norm, w_uq, kv_norm, w_ukv, cos, sin):
    b, s, _ = c_q.shape
    q = (rms_norm(c_q, q_norm) @ w_uq).reshape(b, s, MLA_HEADS, MLA_NOPE_DIM + MLA_ROPE_DIM)
    q_nope, q_pe = q[..., :MLA_NOPE_DIM], apply_rope(q[..., MLA_NOPE_DIM:], cos, sin)
    kv = (rms_norm(c_kv, kv_norm) @ w_ukv).reshape(b, s, MLA_HEADS, MLA_NOPE_DIM + MLA_V_DIM)
    k_nope, v = kv[..., :MLA_NOPE_DIM], kv[..., MLA_NOPE_DIM:]
    k_pe = apply_rope(k_pe[:, :, None, :], cos, sin)
    q_full = jnp.concatenate([q_nope, q_pe], axis=-1)
    k_full = jnp.concatenate([k_nope, jnp.broadcast_to(k_pe, (b, s, MLA_HEADS, MLA_ROPE_DIM)).astype(k_nope.dtype)], axis=-1)
    o = causal_block_attention(q_full, k_full, v, (MLA_NOPE_DIM + MLA_ROPE_DIM) ** -0.5)
    return o.reshape(b, s, MLA_OUT_W)


def chunkwise_retention(q, k, v):
    b, s, h, d = q.shape
    n = s // RET_CHUNK
    log_g = jnp.log(1.0 - 2.0 ** (-5.0 - jnp.arange(h, dtype=F32)))
    i = jnp.arange(RET_CHUNK, dtype=F32)
    diff = i[:, None] - i[None, :]
    intra = jnp.where(diff >= 0, jnp.exp(jnp.maximum(diff, 0.0)[None] * log_g[:, None, None]), 0.0)
    read_decay = jnp.exp((i + 1.0)[None, :] * log_g[:, None])[None, :, :, None]
    write_decay = jnp.exp((RET_CHUNK - 1.0 - i)[None, :] * log_g[:, None])[None, :, :, None]
    chunk_decay = jnp.exp(RET_CHUNK * log_g)[None, :, None, None]

    def to_chunks(a):
        return a.reshape(b, n, RET_CHUNK, h, d).transpose(1, 0, 3, 2, 4)

    def step(state, xs):
        qc, kc, vc = xs
        sc = jnp.einsum('bhid,bhjd->bhij', qc, kc) * intra
        o = jnp.einsum('bhij,bhjd->bhid', sc, vc) + jnp.einsum('bhid,bhde->bhie', qc, state) * read_decay
        state = state * chunk_decay + jnp.einsum('bhjd,bhje->bhde', kc * write_decay, vc)
        return state, o

    state0 = jnp.zeros((b, h, d, d), F32)
    _, outs = lax.scan(step, state0, (to_chunks(q), to_chunks(k), to_chunks(v)))
    return outs.transpose(1, 0, 3, 2, 4).reshape(b, s, h, d)


def retention_mixer(q, k, v, gate, gn_gain, cos, sin):
    b, s, _ = q.shape
    h, d = RET_HEADS, HEAD_DIM
    q = apply_rope(q.reshape(b, s, h, d), cos, sin).astype(F32)
    k = apply_rope(k.reshape(b, s, h, d), cos, sin).astype(F32) * (d ** -0.5)
    v = v.reshape(b, s, h, d).astype(F32)
    o = chunkwise_retention(q, k, v)
    mu = jnp.mean(o, axis=-1, keepdims=True)
    var = jnp.mean(jnp.square(o - mu), axis=-1, keepdims=True)
    o = (o - mu) * lax.rsqrt(var + NORM_EPS) * gn_gain.astype(F32)
    return jax.nn.silu(gate.astype(F32)) * o.reshape(b, s, h * d)


def setup_inputs(seed: int = 0) -> dict:
    key = jax.random.key(seed)
    ks = jax.random.split(key, 24)

    def nrm(k, shape, scale):
        return scale * jax.random.normal(k, shape, F32)

    def gain(k, shape):
        return 1.0 + 0.02 * jax.random.normal(k, shape, F32)

    offs = jax.random.randint(ks[1], (BATCH, 1), 0, 4096)
    positions = (jnp.arange(SEQ, dtype=jnp.int32)[None, :] + offs).astype(jnp.int32)
    return {
        'x': nrm(ks[0], (BATCH, SEQ, D_MODEL), 1.0),
        'positions': positions,
        'ln1_gain': gain(ks[2], (DEPTH, D_MODEL)),
        'w_in': nrm(ks[3], (DEPTH, D_MODEL, N_IN), D_MODEL ** -0.5),
        'cmp_pos_k': nrm(ks[4], (DEPTH, CMP_LEN, HEAD_DIM), 0.1),
        'cmp_w1_k': nrm(ks[5], (DEPTH, CMP_LEN * HEAD_DIM, CMP_HIDDEN), (CMP_LEN * HEAD_DIM) ** -0.5),
        'cmp_w2_k': nrm(ks[6], (DEPTH, CMP_HIDDEN, HEAD_DIM), CMP_HIDDEN ** -0.5),
        'cmp_pos_v': nrm(ks[7], (DEPTH, CMP_LEN, HEAD_DIM), 0.1),
        'cmp_w1_v': nrm(ks[8], (DEPTH, CMP_LEN * HEAD_DIM, CMP_HIDDEN), (CMP_LEN * HEAD_DIM) ** -0.5),
        'cmp_w2_v': nrm(ks[9], (DEPTH, CMP_HIDDEN, HEAD_DIM), CMP_HIDDEN ** -0.5),
        'mla_q_norm': gain(ks[10], (DEPTH, MLA_Q_RANK)),
        'mla_w_uq': nrm(ks[11], (DEPTH, MLA_Q_RANK, MLA_HEADS * (MLA_NOPE_DIM + MLA_ROPE_DIM)), MLA_Q_RANK ** -0.5),
        'mla_kv_norm': gain(ks[12], (DEPTH, MLA_KV_RANK)),
        'mla_w_ukv': nrm(ks[13], (DEPTH, MLA_KV_RANK, MLA_HEADS * (MLA_NOPE_DIM + MLA_V_DIM)), MLA_KV_RANK ** -0.5),
        'ret_gn_gain': gain(ks[14], (DEPTH, RET_HEADS, HEAD_DIM)),
        'w_out': nrm(ks[15], (DEPTH, MIX_WIDTH, D_MODEL), MIX_WIDTH ** -0.5),
        'ln2_gain': gain(ks[16], (DEPTH, D_MODEL)),
        'w_up': nrm(ks[17], (DEPTH, D_MODEL, D_FF), D_MODEL ** -0.5),
        'w_down': nrm(ks[18], (DEPTH, D_FF, D_MODEL), D_FF ** -0.5),
        'final_gain': gain(ks[19], (D_MODEL,)),
    }


def reference(x, positions, ln1_gain, w_in, cmp_pos_k, cmp_w1_k, cmp_w2_k, cmp_pos_v, cmp_w1_v, cmp_w2_v,
              mla_q_norm, mla_w_uq, mla_kv_norm, mla_w_ukv, ret_gn_gain, w_out, ln2_gain, w_up, w_down,
              final_gain):
    b, s, _ = x.shape
    cos_p, sin_p = rope_tables(positions, PARTIAL_ROPE_DIM, ROPE_THETA)
    cos_m, sin_m = rope_tables(positions, MLA_ROPE_DIM, ROPE_THETA)
    cos_r, sin_r = rope_tables(positions, HEAD_DIM, RET_THETA)
    offsets = in_offsets()
    for l in range(DEPTH):
        h = rms_norm(x, ln1_gain[l])
        proj = h @ w_in[l]
        (nsa_q, k_cmp, v_cmp, k_slc, v_slc, k_win, v_win, nsa_gate,
         mla_cq, mla_ckv, mla_kpe, ret_q, ret_k, ret_v, ret_gate) = jnp.split(proj, offsets, axis=-1)

        def kv_heads(a):
            return a.reshape(b, s, NSA_KV_GROUPS, HEAD_DIM)

        o_nsa = nsa_mixer(
            partial_rope(nsa_q.reshape(b, s, NSA_HEADS, HEAD_DIM), cos_p, sin_p),
            partial_rope(kv_heads(k_cmp), cos_p, sin_p), kv_heads(v_cmp),
            partial_rope(kv_heads(k_slc), cos_p, sin_p), kv_heads(v_slc),
            partial_rope(kv_heads(k_win), cos_p, sin_p), kv_heads(v_win),
            nsa_gate.reshape(b, s, NSA_HEADS, N_BRANCH),
            cmp_pos_k[l], cmp_w1_k[l], cmp_w2_k[l], cmp_pos_v[l], cmp_w1_v[l], cmp_w2_v[l])
        o_mla = mla_mixer(mla_cq, mla_ckv, mla_kpe, mla_q_norm[l], mla_w_uq[l], mla_kv_norm[l], mla_w_ukv[l],
                          cos_m, sin_m)
        o_ret = retention_mixer(ret_q, ret_k, ret_v, ret_gate, ret_gn_gain[l], cos_r, sin_r)
        mixed = jnp.concatenate([o_nsa.astype(x.dtype), o_mla.astype(x.dtype), o_ret.astype(x.dtype)], axis=-1)
        x = x + mixed @ w_out[l]
        h = rms_norm(x, ln2_gain[l])
        x = x + jnp.square(jax.nn.relu(h @ w_up[l])) @ w_down[l]
    return rms_norm(x, final_gain)
```

```python
import functools

import numpy as np
import jax
import jax.numpy as jnp
from jax import lax
from jax.experimental import pallas as pl
from jax.experimental.pallas import tpu as pltpu

F32 = jnp.float32
BF16 = jnp.bfloat16

D_MODEL = 1024
HEAD_DIM = 64
NSA_HEADS = 6
NSA_KV_GROUPS = 2
NSA_REP = NSA_HEADS // NSA_KV_GROUPS
N_BRANCH = 3
CMP_LEN = 32
CMP_STRIDE = 16
CMP_HIDDEN = 2 * HEAD_DIM
SEL_BLOCK = 64
SEL_TOP_N = 16
WINDOW = 512
MLA_HEADS = 5
MLA_Q_RANK = 256
MLA_KV_RANK = 128
MLA_NOPE_DIM = 64
MLA_ROPE_DIM = 32
MLA_V_DIM = 64
RET_HEADS = 5
RET_CHUNK = 128
ROPE_THETA = 500000.0
PARTIAL_ROPE_DIM = HEAD_DIM // 4
RET_THETA = 10000.0
D_FF = 4 * D_MODEL
NORM_EPS = 1e-6
NEG_INF = -1e30
FORCE_SCORE = 1e9

NSA_Q_W = NSA_HEADS * HEAD_DIM
NSA_KV_W = NSA_KV_GROUPS * HEAD_DIM
NSA_GATE_W = NSA_HEADS * N_BRANCH
RET_W = RET_HEADS * HEAD_DIM
IN_SIZES = (NSA_Q_W, NSA_KV_W, NSA_KV_W, NSA_KV_W, NSA_KV_W, NSA_KV_W, NSA_KV_W, NSA_GATE_W,
            MLA_Q_RANK, MLA_KV_RANK, MLA_ROPE_DIM, RET_W, RET_W, RET_W, RET_W)

LANES = 128
HALF = LANES // 2
VMEM_LIMIT = 56 * 1024 * 1024

NSA_TILES = 7
MLA_IN_TILES = 4
RET_PAIRS = 3
RET_TILES = 3 * RET_PAIRS
IN_TILES = NSA_TILES + 2 + 1 + MLA_IN_TILES + RET_TILES + RET_PAIRS
N_PAD = IN_TILES * LANES

IN_TM = 256
MLP_TM = 512
MLA_TM = 512
ATT_T = 256
MLP_FF_CHUNK = 1024


def _nn(a, b):
    return jnp.dot(a, b, preferred_element_type=F32)


def _nt(a, b):
    return lax.dot_general(a, b, (((1,), (1,)), ((), ())), preferred_element_type=F32)


def _tn(a, b):
    return lax.dot_general(a, b, (((0,), (0,)), ((), ())), preferred_element_type=F32)


def _rms(x, gain):
    return x * lax.rsqrt(jnp.mean(x * x, axis=-1, keepdims=True) + NORM_EPS) * gain


def _rope(val, tab, half):
    cos = tab[:, 0:LANES]
    sin_a = tab[:, LANES:2 * LANES]
    sin_b = tab[:, 2 * LANES:3 * LANES]
    return (val * cos + pltpu.roll(val, LANES - half, 1) * sin_a
            + pltpu.roll(val, half, 1) * sin_b)


def _lane_lo(shape):
    return lax.broadcasted_iota(jnp.int32, shape, len(shape) - 1) < HALF


def _inproj_kernel(x_ref, g_ref, w_ref, tn_ref, tm_ref, tr_ref,
                   nsa_ref, kcmp_ref, vcmp_ref, gate_ref, mla_ref, ret_ref, rgate_ref):
    h = _rms(x_ref[...], g_ref[...]).astype(BF16)
    tab_n = tn_ref[...]
    tab_r = tr_ref[...]

    def tile(a, i):
        return a[:, i * LANES:(i + 1) * LANES]

    c0 = 0
    a = _nn(h, w_ref[:, c0:c0 + NSA_TILES * LANES])
    for i in range(NSA_TILES):
        v = tile(a, i)
        if i < 5:
            v = _rope(v, tab_n, PARTIAL_ROPE_DIM // 2)
        nsa_ref[:, i * LANES:(i + 1) * LANES] = v.astype(BF16)
    c0 += NSA_TILES * LANES
    a = _nn(h, w_ref[:, c0:c0 + 3 * LANES])
    kcmp_ref[...] = _rope(tile(a, 0), tab_n, PARTIAL_ROPE_DIM // 2)
    vcmp_ref[...] = tile(a, 1)
    gate_ref[...] = tile(a, 2)
    c0 += 3 * LANES
    a = _nn(h, w_ref[:, c0:c0 + MLA_IN_TILES * LANES])
    mla_ref[:, 0:3 * LANES] = a[:, 0:3 * LANES]
    mla_ref[:, 3 * LANES:4 * LANES] = _rope(tile(a, 3), tm_ref[...], MLA_ROPE_DIM // 2)
    c0 += MLA_IN_TILES * LANES
    a = _nn(h, w_ref[:, c0:c0 + RET_TILES * LANES])
    for i in range(RET_TILES):
        v = tile(a, i)
        if i < 2 * RET_PAIRS:
            v = _rope(v, tab_r, HEAD_DIM // 2)
        ret_ref[:, i * LANES:(i + 1) * LANES] = v.astype(BF16)
    c0 += RET_TILES * LANES
    rgate_ref[...] = _nn(h, w_ref[:, c0:c0 + RET_PAIRS * LANES])


def _in_proj(x2, gain, w, tab_n, tab_m, tab_r):
    t = x2.shape[0]
    row = lambda i: (i, 0)
    const = lambda i: (0, 0)
    out_shapes = (
        jax.ShapeDtypeStruct((t, NSA_TILES * LANES), BF16),
        jax.ShapeDtypeStruct((t, LANES), F32),
        jax.ShapeDtypeStruct((t, LANES), F32),
        jax.ShapeDtypeStruct((t, LANES), F32),
        jax.ShapeDtypeStruct((t, MLA_IN_TILES * LANES), F32),
        jax.ShapeDtypeStruct((t, RET_TILES * LANES), BF16),
        jax.ShapeDtypeStruct((t, RET_PAIRS * LANES), F32),
    )
    return pl.pallas_call(
        _inproj_kernel,
        grid=(t // IN_TM,),
        in_specs=[
            pl.BlockSpec((IN_TM, D_MODEL), row),
            pl.BlockSpec((1, D_MODEL), const),
            pl.BlockSpec((D_MODEL, N_PAD), const),
            pl.BlockSpec((IN_TM, 3 * LANES), row),
            pl.BlockSpec((IN_TM, 3 * LANES), row),
            pl.BlockSpec((IN_TM, 3 * LANES), row),
        ],
        out_specs=tuple(pl.BlockSpec((IN_TM, s.shape[1]), row) for s in out_shapes),
        out_shape=out_shapes,
        compiler_params=pltpu.CompilerParams(
            dimension_semantics=("parallel",), vmem_limit_bytes=VMEM_LIMIT),
        name="in_proj",
    )(x2, gain, w, tab_n, tab_m, tab_r)


def _compress_kernel(k_ref, v_ref, pos_ref, wlo_ref, whi_ref, w2_ref, kc_ref, vc_ref):
    n_blk = k_ref.shape[1]
    for i, (src, dst) in enumerate(((k_ref, kc_ref), (v_ref, vc_ref))):
        tok = src[0]
        lo = _nn((tok + pos_ref[i, 0:1, :]).astype(BF16), wlo_ref[i])
        hi = _nn((tok + pos_ref[i, 1:2, :]).astype(BF16), whi_ref[i])
        hid = lo + pltpu.roll(hi, n_blk - 1, 0)
        dst[0] = _nn(jax.nn.gelu(hid).astype(BF16), w2_ref[i]).astype(BF16)


def _compress(k16, v16, pos, wlo, whi, w2):
    b, n_blk, width = k16.shape
    return pl.pallas_call(
        _compress_kernel,
        grid=(b,),
        in_specs=[
            pl.BlockSpec((1, n_blk, width), lambda i: (i, 0, 0)),
            pl.BlockSpec((1, n_blk, width), lambda i: (i, 0, 0)),
            pl.BlockSpec(pos.shape, lambda i: (0, 0, 0)),
            pl.BlockSpec(wlo.shape, lambda i: (0, 0, 0)),
            pl.BlockSpec(whi.shape, lambda i: (0, 0, 0)),
            pl.BlockSpec(w2.shape, lambda i: (0, 0, 0)),
        ],
        out_specs=(pl.BlockSpec((1, n_blk, LANES), lambda i: (i, 0, 0)),
                   pl.BlockSpec((1, n_blk, LANES), lambda i: (i, 0, 0))),
        out_shape=(jax.ShapeDtypeStruct((b, n_blk, LANES), BF16),
                   jax.ShapeDtypeStruct((b, n_blk, LANES), BF16)),
        compiler_params=pltpu.CompilerParams(
            dimension_semantics=("parallel",), vmem_limit_bytes=VMEM_LIMIT),
        name="nsa_compress",
    )(k16, v16, pos, wlo, whi, w2)


def _softmax_step(s, v, m, l, acc):
    m_new = jnp.maximum(m, jnp.max(s, axis=-1, keepdims=True))
    alpha = jnp.exp(m - m_new)
    p = jnp.exp(s - m_new)
    l = alpha * l + jnp.sum(p, axis=-1, keepdims=True)
    acc = alpha * acc + _nn(p.astype(BF16), v)
    return m_new, l, acc


def _softmax_init(rows):
    return (jnp.full((rows, 1), NEG_INF, F32), jnp.zeros((rows, 1), F32),
            jnp.zeros((rows, LANES), F32))


def _softmax_finish(l, acc):
    return acc * (1.0 / jnp.maximum(l, 1e-30))


def _nsa_kernel(q_ref, gate_ref, kc_ref, vc_ref, ks_ref, vs_ref, kw_ref, vw_ref,
                ovl_ref, e_ref, o_ref, acc_ref):
    tq = ATT_T
    qi = pl.program_id(1)
    q0 = pl.multiple_of(qi * tq, tq)
    lo1 = _lane_lo((1, LANES))
    group_lanes = (lo1, jnp.logical_not(lo1))
    gate = jax.nn.sigmoid(gate_ref[...])
    acc_ref[...] = jnp.zeros_like(acc_ref)

    def q_head(h):
        j, g = h % NSA_REP, h // NSA_REP
        tile = q_ref[:, j * LANES:(j + 1) * LANES]
        return jnp.where(group_lanes[g], tile, jnp.zeros_like(tile))

    def emit(h, branch, o):
        j, g = h % NSA_REP, h // NSA_REP
        col = h * N_BRANCH + branch
        acc_ref[:, j * LANES:(j + 1) * LANES] += jnp.where(group_lanes[g], gate[:, col:col + 1] * o, 0.0)

    n_cmp_pad = kc_ref.shape[1]
    kc = kc_ref[0]
    vc = vc_ref[0]
    n_i = lax.broadcasted_iota(jnp.int32, (n_cmp_pad, tq), 0)
    t_l = q0 + lax.broadcasted_iota(jnp.int32, (n_cmp_pad, tq), 1)
    cmask = (n_i * CMP_STRIDE + (CMP_LEN - 1)) <= t_l
    cmask_f = cmask.astype(F32)
    psum = [None] * NSA_KV_GROUPS
    for h in range(NSA_HEADS):
        g = h // NSA_REP
        s_t = jnp.where(cmask, _nt(kc, q_head(h)), NEG_INF)
        m = jnp.max(s_t, axis=0, keepdims=True)
        p = jnp.exp(s_t - m) * cmask_f
        p = p * (1.0 / jnp.maximum(jnp.sum(p, axis=0, keepdims=True), 1e-30))
        psum[g] = p if psum[g] is None else psum[g] + p
        emit(h, 0, _tn(p.astype(BF16), vc))

    n_sel = ovl_ref.shape[0]
    m_i = lax.broadcasted_iota(jnp.int32, (n_sel, tq), 0)
    cur = jnp.right_shift(q0 + lax.broadcasted_iota(jnp.int32, (n_sel, tq), 1), SEL_BLOCK.bit_length() - 1)
    valid = m_i <= cur
    forced = (m_i == 0) | (m_i == cur) | (m_i == cur - 1)
    ovl = ovl_ref[...]
    sel_bias = []
    for g in range(NSA_KV_GROUPS):
        p_hi = psum[g].astype(BF16)
        p_lo = (psum[g] - p_hi.astype(F32)).astype(BF16)
        imp = _nn(ovl, p_hi) + _nn(ovl, p_lo)
        imp = jnp.where(valid & forced, FORCE_SCORE, imp)
        imp = jnp.where(valid, imp, NEG_INF)
        rank = jnp.zeros((n_sel, tq), jnp.int32)
        for mp in range(n_sel):
            row = imp[mp:mp + 1, :]
            beats = (row > imp) | ((row == imp) & (m_i > mp))
            rank = rank + beats.astype(jnp.int32)
        sel_bias.append(jnp.where(rank < SEL_TOP_N, 0.0, NEG_INF).astype(BF16))

    row_i = lax.broadcasted_iota(jnp.int32, (tq, tq), 0)
    col_i = lax.broadcasted_iota(jnp.int32, (tq, tq), 1)
    causal = col_i <= row_i

    for h in range(NSA_HEADS):
        g = h // NSA_REP
        qh = q_head(h)
        s = _nt(qh, ks_ref[pl.ds(q0, tq), :]) + _tn(sel_bias[g], e_ref[qi])
        state = _softmax_step(jnp.where(causal, s, NEG_INF), vs_ref[pl.ds(q0, tq), :],
                              *_softmax_init(tq))

        def sel_body(c, st, qh=qh, g=g):
            k0 = pl.multiple_of(c * tq, tq)
            s = _nt(qh, ks_ref[pl.ds(k0, tq), :]) + _tn(sel_bias[g], e_ref[c])
            return _softmax_step(s, vs_ref[pl.ds(k0, tq), :], *st)

        _, l, acc = lax.fori_loop(0, qi, sel_body, state)
        emit(h, 1, _softmax_finish(l, acc))

        s = jnp.where(causal, _nt(qh, kw_ref[pl.ds(q0, tq), :]), NEG_INF)
        state = _softmax_step(s, vw_ref[pl.ds(q0, tq), :], *_softmax_init(tq))
        for d in range(1, WINDOW // tq + 1):
            c = jnp.maximum(qi - d, 0)
            k0 = pl.multiple_of(c * tq, tq)
            in_band = (col_i - d * tq > row_i - WINDOW) & (qi >= d)
            s = jnp.where(in_band, _nt(qh, kw_ref[pl.ds(k0, tq), :]), NEG_INF)
            state = _softmax_step(s, vw_ref[pl.ds(k0, tq), :], *state)
        emit(h, 2, _softmax_finish(state[1], state[2]))

    o_ref[...] = acc_ref[...].astype(BF16)


def _nsa_attention(nsa, gate, kc, vc, ovl_t, e_blocks, batch, seq):
    tq = ATT_T
    nq = seq // tq
    n_blk = kc.shape[1]
    qrow = lambda b, i: (b * nq + i, 0)
    return pl.pallas_call(
        _nsa_kernel,
        grid=(batch, nq),
        in_specs=[
            pl.BlockSpec((tq, NSA_REP * LANES), qrow),
            pl.BlockSpec((tq, LANES), qrow),
            pl.BlockSpec((1, n_blk, LANES), lambda b, i: (b, 0, 0)),
            pl.BlockSpec((1, n_blk, LANES), lambda b, i: (b, 0, 0)),
            pl.BlockSpec((seq, LANES), lambda b, i: (b, 3)),
            pl.BlockSpec((seq, LANES), lambda b, i: (b, 5)),
            pl.BlockSpec((seq, LANES), lambda b, i: (b, 4)),
            pl.BlockSpec((seq, LANES), lambda b, i: (b, 6)),
            pl.BlockSpec(ovl_t.shape, lambda b, i: (0, 0)),
            pl.BlockSpec(e_blocks.shape, lambda b, i: (0, 0, 0)),
        ],
        out_specs=pl.BlockSpec((tq, NSA_REP * LANES), qrow),
        out_shape=jax.ShapeDtypeStruct((batch * seq, NSA_REP * LANES), BF16),
        scratch_shapes=[pltpu.VMEM((tq, NSA_REP * LANES), F32)],
        compiler_params=pltpu.CompilerParams(
            dimension_semantics=("parallel", "arbitrary"), vmem_limit_bytes=VMEM_LIMIT),
        name="nsa_attention",
    )(nsa, gate, kc, vc, nsa, nsa, nsa, nsa, ovl_t, e_blocks)


def _mla_up_kernel(in_ref, qn_ref, kvn_ref, wq_ref, wk_ref, wv_ref, tab_ref, q_ref, k_ref, v_ref):
    scale = (MLA_NOPE_DIM + MLA_ROPE_DIM) ** -0.5
    cq = _rms(in_ref[:, 0:MLA_Q_RANK], qn_ref[...]).astype(BF16)
    ckv = _rms(in_ref[:, MLA_Q_RANK:MLA_Q_RANK + MLA_KV_RANK], kvn_ref[...]).astype(BF16)
    k_pe = in_ref[:, 3 * LANES:4 * LANES]
    tab = tab_ref[...]
    q = _nn(cq, wq_ref[...])
    k = _nn(ckv, wk_ref[...])
    for h in range(MLA_HEADS):
        sl = slice(h * LANES, (h + 1) * LANES)
        q_ref[:, sl] = (_rope(q[:, sl], tab, MLA_ROPE_DIM // 2) * scale).astype(BF16)
        k_ref[:, sl] = (k[:, sl] + k_pe).astype(BF16)
    v_ref[...] = _nn(ckv, wv_ref[...]).astype(BF16)


def _mla_up(mla_in, q_norm, kv_norm, wq, wk, wv, tab_m):
    t = mla_in.shape[0]
    row = lambda i: (i, 0)
    const = lambda i: (0, 0)
    return pl.pallas_call(
        _mla_up_kernel,
        grid=(t // MLA_TM,),
        in_specs=[
            pl.BlockSpec((MLA_TM, MLA_IN_TILES * LANES), row),
            pl.BlockSpec((1, MLA_Q_RANK), const),
            pl.BlockSpec((1, MLA_KV_RANK), const),
            pl.BlockSpec(wq.shape, const),
            pl.BlockSpec(wk.shape, const),
            pl.BlockSpec(wv.shape, const),
            pl.BlockSpec((MLA_TM, 3 * LANES), row),
        ],
        out_specs=(pl.BlockSpec((MLA_TM, MLA_HEADS * LANES), row),
                   pl.BlockSpec((MLA_TM, MLA_HEADS * LANES), row),
                   pl.BlockSpec((MLA_TM, RET_PAIRS * LANES), row)),
        out_shape=(jax.ShapeDtypeStruct((t, MLA_HEADS * LANES), BF16),
                   jax.ShapeDtypeStruct((t, MLA_HEADS * LANES), BF16),
                   jax.ShapeDtypeStruct((t, RET_PAIRS * LANES), BF16)),
        compiler_params=pltpu.CompilerParams(
            dimension_semantics=("parallel",), vmem_limit_bytes=VMEM_LIMIT),
        name="mla_up",
    )(mla_in, q_norm, kv_norm, wq, wk, wv, tab_m)


def _mla_attn_kernel(q_ref, k_ref, v_ref, o_ref):
    tq = ATT_T
    qi = pl.program_id(1)
    q0 = pl.multiple_of(qi * tq, tq)
    row_i = lax.broadcasted_iota(jnp.int32, (tq, tq), 0)
    col_i = lax.broadcasted_iota(jnp.int32, (tq, tq), 1)
    causal = col_i <= row_i
    lo = _lane_lo((1, LANES))
    outs = []
    for h in range(MLA_HEADS):
        hs = slice(h * LANES, (h + 1) * LANES)
        vs = slice((h // 2) * LANES, (h // 2 + 1) * LANES)
        qh = q_ref[:, hs]
        s = jnp.where(causal, _nt(qh, k_ref[pl.ds(q0, tq), hs]), NEG_INF)
        state = _softmax_step(s, v_ref[pl.ds(q0, tq), vs], *_softmax_init(tq))

        def body(c, st, qh=qh, hs=hs, vs=vs):
            k0 = pl.multiple_of(c * tq, tq)
            return _softmax_step(_nt(qh, k_ref[pl.ds(k0, tq), hs]), v_ref[pl.ds(k0, tq), vs], *st)

        _, l, acc = lax.fori_loop(0, qi, body, state)
        outs.append(_softmax_finish(l, acc))
    outs.append(jnp.zeros_like(outs[0]))
    for j in range(RET_PAIRS):
        o_ref[:, j * LANES:(j + 1) * LANES] = jnp.where(lo, outs[2 * j], outs[2 * j + 1]).astype(BF16)


def _mla_attention(q, k, v, batch, seq):
    tq = ATT_T
    nq = seq // tq
    return pl.pallas_call(
        _mla_attn_kernel,
        grid=(batch, nq),
        in_specs=[
            pl.BlockSpec((tq, MLA_HEADS * LANES), lambda b, i: (b * nq + i, 0)),
            pl.BlockSpec((seq, MLA_HEADS * LANES), lambda b, i: (b, 0)),
            pl.BlockSpec((seq, RET_PAIRS * LANES), lambda b, i: (b, 0)),
        ],
        out_specs=pl.BlockSpec((tq, RET_PAIRS * LANES), lambda b, i: (b * nq + i, 0)),
        out_shape=jax.ShapeDtypeStruct((batch * seq, RET_PAIRS * LANES), BF16),
        compiler_params=pltpu.CompilerParams(
            dimension_semantics=("parallel", "arbitrary"), vmem_limit_bytes=VMEM_LIMIT),
        name="mla_attention",
    )(q, k, v)


def _retention_kernel(q_ref, k_ref, v_ref, gate_ref, gn_ref, intra_ref, rd_ref, wd_ref, cd_ref, o_ref):
    c_len = RET_CHUNK
    n_chunks = q_ref.shape[0] // c_len
    lo = _lane_lo((1, LANES))
    hi = jnp.logical_not(lo)
    blockdiag = (lax.broadcasted_iota(jnp.int32, (LANES, LANES), 0) < HALF) == _lane_lo((LANES, LANES))
    intra_a = intra_ref[0, 0]
    intra_b = intra_ref[0, 1]
    read_decay = rd_ref[0]
    write_decay = wd_ref[0]
    chunk_decay = cd_ref[0]
    gn = gn_ref[0]

    def half_mean(x):
        s_lo = jnp.sum(jnp.where(lo, x, 0.0), axis=-1, keepdims=True)
        s_hi = jnp.sum(jnp.where(hi, x, 0.0), axis=-1, keepdims=True)
        return jnp.where(lo, s_lo, s_hi) * (1.0 / HEAD_DIM)

    def body(c, state):
        r0 = pl.multiple_of(c * c_len, c_len)
        qc = q_ref[pl.ds(r0, c_len), :]
        kc = k_ref[pl.ds(r0, c_len), :]
        vc = v_ref[pl.ds(r0, c_len), :]
        zero = jnp.zeros_like(qc)
        sa = _nt(jnp.where(lo, qc, zero), kc) * intra_a
        sb = _nt(jnp.where(hi, qc, zero), kc) * intra_b
        o = jnp.where(lo, _nn(sa.astype(BF16), vc), _nn(sb.astype(BF16), vc))
        o = o + _nn(qc, jnp.where(blockdiag, state, 0.0).astype(BF16)) * read_decay
        state = state * chunk_decay + _tn((kc.astype(F32) * write_decay).astype(BF16), vc)
        mu = half_mean(o)
        d = o - mu
        y = d * lax.rsqrt(half_mean(d * d) + NORM_EPS) * gn
        o_ref[pl.ds(r0, c_len), :] = (jax.nn.silu(gate_ref[pl.ds(r0, c_len), :]) * y).astype(BF16)
        return state

    lax.fori_loop(0, n_chunks, body, jnp.zeros((LANES, LANES), F32))


def _retention(ret, rgate, gn, intra, rd, wd, cd, batch, seq):
    pair_const3 = lambda b, j: (j, 0, 0)
    return pl.pallas_call(
        _retention_kernel,
        grid=(batch, RET_PAIRS),
        in_specs=[
            pl.BlockSpec((seq, LANES), lambda b, j: (b, j)),
            pl.BlockSpec((seq, LANES), lambda b, j: (b, RET_PAIRS + j)),
            pl.BlockSpec((seq, LANES), lambda b, j: (b, 2 * RET_PAIRS + j)),
            pl.BlockSpec((seq, LANES), lambda b, j: (b, j)),
            pl.BlockSpec((1, 1, LANES), pair_const3),
            pl.BlockSpec((1, 2, RET_CHUNK, RET_CHUNK), lambda b, j: (j, 0, 0, 0)),
            pl.BlockSpec((1, RET_CHUNK, LANES), pair_const3),
            pl.BlockSpec((1, RET_CHUNK, LANES), pair_const3),
            pl.BlockSpec((1, 1, LANES), pair_const3),
        ],
        out_specs=pl.BlockSpec((seq, LANES), lambda b, j: (b, j)),
        out_shape=jax.ShapeDtypeStruct((batch * seq, RET_PAIRS * LANES), BF16),
        compiler_params=pltpu.CompilerParams(
            dimension_semantics=("parallel", "arbitrary"), vmem_limit_bytes=VMEM_LIMIT),
        name="retention",
    )(ret, ret, ret, rgate, gn, intra, rd, wd, cd)


def _out_mlp_kernel(x_ref, nsa_ref, mla_ref, ret_ref, wo_ref, g2_ref, wu_ref, wd_ref, gf_ref, o_ref,
                    *, final_norm):
    w = RET_PAIRS * LANES
    mixed = (_nn(nsa_ref[...], wo_ref[0:w, :]) + _nn(mla_ref[...], wo_ref[w:2 * w, :])
             + _nn(ret_ref[...], wo_ref[2 * w:3 * w, :]))
    x = x_ref[...] + mixed
    h = _rms(x, g2_ref[...]).astype(BF16)
    y = x
    for c in range(D_FF // MLP_FF_CHUNK):
        sl = slice(c * MLP_FF_CHUNK, (c + 1) * MLP_FF_CHUNK)
        u = jnp.maximum(_nn(h, wu_ref[:, sl]), 0.0)
        y = y + _nn((u * u).astype(BF16), wd_ref[sl, :])
    if final_norm:
        y = _rms(y, gf_ref[...])
    o_ref[...] = y


def _out_mlp(x2, o_nsa, o_mla, o_ret, wo, g2, wu, wd, gf, final_norm):
    t = x2.shape[0]
    w = RET_PAIRS * LANES
    row = lambda i: (i, 0)
    const = lambda i: (0, 0)
    resident = dict(pipeline_mode=pl.Buffered(1))
    return pl.pallas_call(
        functools.partial(_out_mlp_kernel, final_norm=final_norm),
        grid=(t // MLP_TM,),
        in_specs=[
            pl.BlockSpec((MLP_TM, D_MODEL), row),
            pl.BlockSpec((MLP_TM, w), row),
            pl.BlockSpec((MLP_TM, w), row),
            pl.BlockSpec((MLP_TM, w), row),
            pl.BlockSpec(wo.shape, const, **resident),
            pl.BlockSpec((1, D_MODEL), const),
            pl.BlockSpec(wu.shape, const, **resident),
            pl.BlockSpec(wd.shape, const, **resident),
            pl.BlockSpec((1, D_MODEL), const),
        ],
        out_specs=pl.BlockSpec((MLP_TM, D_MODEL), row),
        out_shape=jax.ShapeDtypeStruct((t, D_MODEL), F32),
        compiler_params=pltpu.CompilerParams(
            dimension_semantics=("parallel",), vmem_limit_bytes=VMEM_LIMIT),
        name="out_mlp",
    )(x2, o_nsa, o_mla, o_ret, wo, g2, wu, wd, gf)


def _rope_table(positions, dim, theta, period, base):
    half = dim // 2
    inv = 1.0 / (theta ** (jnp.arange(0, dim, 2, dtype=F32) / dim))
    ang = positions.reshape(-1).astype(F32)[:, None] * inv
    cos, sin = jnp.cos(ang), jnp.sin(ang)
    rel = (np.arange(LANES) - base) % period
    first = (rel < half) & (np.arange(LANES) >= base)
    second = (rel >= half) & (rel < dim) & (np.arange(LANES) >= base)
    idx = np.where(first, rel, np.where(second, rel - half, 0))
    cos_l = jnp.where(first | second, cos[:, idx], 1.0)
    sin_l = sin[:, idx]
    return jnp.concatenate([cos_l, jnp.where(first, -sin_l, 0.0), jnp.where(second, sin_l, 0.0)], axis=1)


def _pad_cols(w, n):
    return jnp.pad(w, ((0, 0), (0, 0), (0, n - w.shape[-1])))


def _in_weight(w_in):
    offs = np.cumsum((0,) + IN_SIZES)
    seg = [w_in[:, :, offs[i]:offs[i + 1]] for i in range(len(IN_SIZES))]
    (nsa_q, k_cmp, v_cmp, k_slc, v_slc, k_win, v_win, gate,
     cq, ckv, kpe, ret_q, ret_k, ret_v, ret_g) = seg
    scale = HEAD_DIM ** -0.5
    qh = [nsa_q[:, :, h * HEAD_DIM:(h + 1) * HEAD_DIM] * scale for h in range(NSA_HEADS)]
    tiles = [jnp.concatenate([qh[j], qh[j + NSA_REP]], axis=-1) for j in range(NSA_REP)]
    tiles += [k_slc, k_win, v_slc, v_win, k_cmp, v_cmp, _pad_cols(gate, LANES)]
    tiles += [cq, ckv, jnp.pad(kpe, ((0, 0), (0, 0), (HALF, LANES - HALF - MLA_ROPE_DIM)))]
    ret_w = RET_PAIRS * LANES
    tiles += [_pad_cols(ret_q, ret_w), _pad_cols(ret_k * scale, ret_w), _pad_cols(ret_v, ret_w),
              _pad_cols(ret_g, ret_w)]
    return jnp.concatenate(tiles, axis=-1).astype(BF16)


def _compress_weights(pos, w1, w2):
    nl = pos.shape[0]
    g, dh, hid = NSA_KV_GROUPS, HEAD_DIM, CMP_HIDDEN
    p = pos.reshape(nl, 2, CMP_STRIDE, 1, dh)
    p = jnp.broadcast_to(p, (nl, 2, CMP_STRIDE, g, dh)).reshape(nl, 2, CMP_STRIDE * g * dh)
    w = w1.reshape(nl, 2, CMP_STRIDE, dh, hid)
    eye = jnp.eye(g, dtype=w1.dtype)
    w = jnp.einsum('lhrdc,gk->lhrgdkc', w, eye).reshape(nl, 2, CMP_STRIDE * g * dh, g * hid)
    w2b = jnp.einsum('lcd,gk->lgckd', w2, eye).reshape(nl, g * hid, g * dh)
    return p, w[:, 0].astype(BF16), w[:, 1].astype(BF16), w2b.astype(BF16)


def _mla_weights(w_uq, w_ukv):
    nl = w_uq.shape[0]
    dq = MLA_NOPE_DIM + MLA_ROPE_DIM
    wq = w_uq.reshape(nl, MLA_Q_RANK, MLA_HEADS, dq)
    wq = jnp.pad(wq, ((0, 0), (0, 0), (0, 0), (0, LANES - dq))).reshape(nl, MLA_Q_RANK, MLA_HEADS * LANES)
    wkv = w_ukv.reshape(nl, MLA_KV_RANK, MLA_HEADS, MLA_NOPE_DIM + MLA_V_DIM)
    wk = jnp.pad(wkv[..., :MLA_NOPE_DIM], ((0, 0), (0, 0), (0, 0), (0, LANES - MLA_NOPE_DIM)))
    wk = wk.reshape(nl, MLA_KV_RANK, MLA_HEADS * LANES)
    wv = wkv[..., MLA_NOPE_DIM:].reshape(nl, MLA_KV_RANK, MLA_HEADS * MLA_V_DIM)
    wv = _pad_cols(wv, RET_PAIRS * LANES)
    return wq.astype(BF16), wk.astype(BF16), wv.astype(BF16)


def _out_weight(w_out):
    nl = w_out.shape[0]
    pad_rows = lambda w: jnp.pad(w, ((0, 0), (0, RET_PAIRS * LANES - w.shape[1]), (0, 0)))
    nsa = w_out[:, :NSA_Q_W].reshape(nl, NSA_HEADS, HEAD_DIM, D_MODEL)
    order = [h for j in range(NSA_REP) for h in (j, j + NSA_REP)]
    nsa = nsa[:, order].reshape(nl, NSA_Q_W, D_MODEL)
    mla = pad_rows(w_out[:, NSA_Q_W:NSA_Q_W + MLA_HEADS * MLA_V_DIM])
    ret = pad_rows(w_out[:, NSA_Q_W + MLA_HEADS * MLA_V_DIM:])
    return jnp.concatenate([nsa, mla, ret], axis=1).astype(BF16)


def _retention_tables(gn_gain):
    nh = 2 * RET_PAIRS
    log_g = jnp.log(1.0 - 2.0 ** (-5.0 - jnp.arange(nh, dtype=F32)))
    i = jnp.arange(RET_CHUNK, dtype=F32)
    diff = i[:, None] - i[None, :]
    intra = jnp.where(diff >= 0, jnp.exp(jnp.maximum(diff, 0.0)[None] * log_g[:, None, None]), 0.0)
    read_decay = jnp.exp((i + 1.0)[None, :] * log_g[:, None])
    write_decay = jnp.exp((RET_CHUNK - 1.0 - i)[None, :] * log_g[:, None])
    chunk_decay = jnp.exp(RET_CHUNK * log_g)

    def lanes(t):
        t = t.reshape(RET_PAIRS, 2, -1)
        return jnp.repeat(t.transpose(0, 2, 1), HALF, axis=-1)

    gn = jnp.pad(gn_gain, ((0, 0), (0, nh - RET_HEADS), (0, 0)))
    gn = gn.reshape(gn.shape[0], RET_PAIRS, 1, LANES)
    return (gn, intra.reshape(RET_PAIRS, 2, RET_CHUNK, RET_CHUNK), lanes(read_decay), lanes(write_decay),
            lanes(chunk_decay[:, None]))


def _selection_constants(seq):
    n_cmp = (seq - CMP_LEN) // CMP_STRIDE + 1
    n_cmp_pad = seq // CMP_STRIDE
    n_sel = seq // SEL_BLOCK
    cs = np.arange(n_cmp) * CMP_STRIDE
    ss = np.arange(n_sel) * SEL_BLOCK
    ov = np.clip(np.minimum(cs[:, None] + CMP_LEN, ss[None, :] + SEL_BLOCK)
                 - np.maximum(cs[:, None], ss[None, :]), 0, None) / CMP_LEN
    ovl_t = np.zeros((n_sel, n_cmp_pad), np.float32)
    ovl_t[:, :n_cmp] = ov.T
    key_block = np.arange(seq) // SEL_BLOCK
    e = (np.arange(n_sel)[:, None] == key_block[None, :]).astype(np.float32)
    e = e.reshape(n_sel, seq // ATT_T, ATT_T).transpose(1, 0, 2)
    return jnp.asarray(ovl_t, BF16), jnp.asarray(e, BF16)


def kernel(x, positions, ln1_gain, w_in, cmp_pos_k, cmp_w1_k, cmp_w2_k, cmp_pos_v, cmp_w1_v, cmp_w2_v,
           mla_q_norm, mla_w_uq, mla_kv_norm, mla_w_ukv, ret_gn_gain, w_out, ln2_gain, w_up, w_down,
           final_gain):
    batch, seq, _ = x.shape
    depth = w_in.shape[0]
    t = batch * seq

    tab_n = _rope_table(positions, PARTIAL_ROPE_DIM, ROPE_THETA, HEAD_DIM, 0)
    tab_m = _rope_table(positions, MLA_ROPE_DIM, ROPE_THETA, LANES, HALF)
    tab_r = _rope_table(positions, HEAD_DIM, RET_THETA, HEAD_DIM, 0)

    w_in_p = _in_weight(w_in)
    pos_k, wlo_k, whi_k, w2_k = _compress_weights(cmp_pos_k, cmp_w1_k, cmp_w2_k)
    pos_v, wlo_v, whi_v, w2_v = _compress_weights(cmp_pos_v, cmp_w1_v, cmp_w2_v)
    cmp_pos = jnp.stack([pos_k, pos_v], axis=1)
    cmp_wlo = jnp.stack([wlo_k, wlo_v], axis=1)
    cmp_whi = jnp.stack([whi_k, whi_v], axis=1)
    cmp_w2 = jnp.stack([w2_k, w2_v], axis=1)
    wq, wk, wv = _mla_weights(mla_w_uq, mla_w_ukv)
    wo = _out_weight(w_out)
    wu = w_up.astype(BF16)
    wd = w_down.astype(BF16)
    gn, intra, rd, wdec, cd = _retention_tables(ret_gn_gain)
    ovl_t, e_blocks = _selection_constants(seq)
    gf = final_gain.reshape(1, D_MODEL)

    x2 = x.reshape(t, D_MODEL)
    rows16 = seq // CMP_STRIDE
    for l in range(depth):
        nsa, k_cmp, v_cmp, gate, mla_in, ret, rgate = _in_proj(
            x2, ln1_gain[l].reshape(1, D_MODEL), w_in_p[l], tab_n, tab_m, tab_r)
        kc, vc = _compress(k_cmp.reshape(batch, rows16, CMP_STRIDE * LANES),
                           v_cmp.reshape(batch, rows16, CMP_STRIDE * LANES),
                           cmp_pos[l], cmp_wlo[l], cmp_whi[l], cmp_w2[l])
        o_nsa = _nsa_attention(nsa, gate, kc, vc, ovl_t, e_blocks, batch, seq)
        q_m, k_m, v_m = _mla_up(mla_in, mla_q_norm[l].reshape(1, -1), mla_kv_norm[l].reshape(1, -1),
                                wq[l], wk[l], wv[l], tab_m)
        o_mla = _mla_attention(q_m, k_m, v_m, batch, seq)
        o_ret = _retention(ret, rgate, gn[l], intra, rd, wdec, cd, batch, seq)
        x2 = _out_mlp(x2, o_nsa, o_mla, o_ret, wo[l], ln2_gain[l].reshape(1, D_MODEL), wu[l], wd[l], gf,
                      final_norm=(l == depth - 1))
    return x2.reshape(batch, seq, D_MODEL)
```

```python
import functools
import math

import numpy as np
import jax
import jax.numpy as jnp
from jax import lax
from jax.experimental import pallas as pl
from jax.experimental.pallas import tpu as pltpu

F32 = jnp.float32
BF16 = jnp.bfloat16

D_MODEL = 1024
HEAD_DIM = 64
NSA_HEADS = 6
NSA_KV_GROUPS = 2
NSA_REP = NSA_HEADS // NSA_KV_GROUPS
N_BRANCH = 3
CMP_LEN = 32
CMP_STRIDE = 16
CMP_HIDDEN = 2 * HEAD_DIM
SEL_BLOCK = 64
SEL_TOP_N = 16
WINDOW = 512
MLA_HEADS = 5
MLA_Q_RANK = 256
MLA_KV_RANK = 128
MLA_NOPE_DIM = 64
MLA_ROPE_DIM = 32
MLA_V_DIM = 64
RET_HEADS = 5
RET_CHUNK = 128
ROPE_THETA = 500000.0
PARTIAL_ROPE_DIM = HEAD_DIM // 4
RET_THETA = 10000.0
D_FF = 4 * D_MODEL
NORM_EPS = 1e-6
NEG_INF = -1e30
FORCE_SCORE = 1e9
LOG2E = math.log2(math.e)

NSA_Q_W = NSA_HEADS * HEAD_DIM
NSA_KV_W = NSA_KV_GROUPS * HEAD_DIM
NSA_GATE_W = NSA_HEADS * N_BRANCH
RET_W = RET_HEADS * HEAD_DIM
IN_SIZES = (NSA_Q_W, NSA_KV_W, NSA_KV_W, NSA_KV_W, NSA_KV_W, NSA_KV_W, NSA_KV_W, NSA_GATE_W,
            MLA_Q_RANK, MLA_KV_RANK, MLA_ROPE_DIM, RET_W, RET_W, RET_W, RET_W)

LANES = 128
HALF = LANES // 2
VMEM_LIMIT = 56 * 1024 * 1024

NSA_TILES = 5
GATE_ROWS = 32
NSA_T_ROWS = 2 * LANES + GATE_ROWS
MLA_IN_TILES = 4
RET_PAIRS = 3
RET_TILES = 3 * RET_PAIRS
IN_TILES = NSA_TILES + 2 + MLA_IN_TILES + RET_TILES + RET_PAIRS
N_PAD = IN_TILES * LANES
MLA_V_ROWS = MLA_HEADS * MLA_V_DIM

IN_TM = 256
MLP_TM = 512
MLA_TM = 512
ATT_T = 512
MLP_FF_CHUNK = 1024


def _nn(a, b):
    return jnp.dot(a, b, preferred_element_type=F32)


def _nt(a, b):
    return lax.dot_general(a, b, (((1,), (1,)), ((), ())), preferred_element_type=F32)


def _rms(x, gain):
    return x * lax.rsqrt(jnp.mean(x * x, axis=-1, keepdims=True) + NORM_EPS) * gain


def _rope(val, tab, half):
    cos = tab[:, 0:LANES]
    sin_a = tab[:, LANES:2 * LANES]
    sin_b = tab[:, 2 * LANES:3 * LANES]
    return (val * cos + pltpu.roll(val, LANES - half, 1) * sin_a
            + pltpu.roll(val, half, 1) * sin_b)


def _lane_lo(shape):
    return lax.broadcasted_iota(jnp.int32, shape, len(shape) - 1) < HALF


def _inproj_kernel(x_ref, g_ref, w_ref, wt_ref, tn_ref, tm_ref, tr_ref,
                   nsa_ref, nsat_ref, gatet_ref, kcmp_ref, vcmp_ref, mla_ref, ret_ref, rgate_ref):
    h = _rms(x_ref[...], g_ref[...]).astype(BF16)
    tab_n = tn_ref[...]
    tab_r = tr_ref[...]

    def tile(a, i):
        return a[:, i * LANES:(i + 1) * LANES]

    c0 = 0
    a = _nn(h, w_ref[:, c0:c0 + NSA_TILES * LANES])
    for i in range(NSA_TILES):
        v = _rope(tile(a, i), tab_n, PARTIAL_ROPE_DIM // 2)
        if i < NSA_REP:
            v = v * LOG2E
        nsa_ref[:, i * LANES:(i + 1) * LANES] = v.astype(BF16)
    c0 += NSA_TILES * LANES
    a = _nn(h, w_ref[:, c0:c0 + 2 * LANES])
    kcmp_ref[...] = _rope(tile(a, 0), tab_n, PARTIAL_ROPE_DIM // 2)
    vcmp_ref[...] = tile(a, 1)
    c0 += 2 * LANES
    a = _nn(h, w_ref[:, c0:c0 + MLA_IN_TILES * LANES])
    mla_ref[:, 0:3 * LANES] = a[:, 0:3 * LANES]
    mla_ref[:, 3 * LANES:4 * LANES] = _rope(tile(a, 3), tm_ref[...], MLA_ROPE_DIM // 2)
    c0 += MLA_IN_TILES * LANES
    a = _nn(h, w_ref[:, c0:c0 + RET_TILES * LANES])
    for i in range(RET_TILES):
        v = tile(a, i)
        if i < 2 * RET_PAIRS:
            v = _rope(v, tab_r, HEAD_DIM // 2)
        ret_ref[:, i * LANES:(i + 1) * LANES] = v.astype(BF16)
    c0 += RET_TILES * LANES
    rgate_ref[...] = _nn(h, w_ref[:, c0:c0 + RET_PAIRS * LANES])
    at = _nt(wt_ref[...], h)
    nsat_ref[...] = at[0:2 * LANES, :].astype(BF16)
    gatet_ref[...] = at[2 * LANES:NSA_T_ROWS, :]


def _in_proj(x2, gain, w, wt, tab_n, tab_m, tab_r):
    t = x2.shape[0]
    row = lambda i: (i, 0)
    col = lambda i: (0, i)
    const = lambda i: (0, 0)
    out_shapes = (
        jax.ShapeDtypeStruct((t, NSA_TILES * LANES), BF16),
        jax.ShapeDtypeStruct((2 * LANES, t), BF16),
        jax.ShapeDtypeStruct((GATE_ROWS, t), F32),
        jax.ShapeDtypeStruct((t, LANES), F32),
        jax.ShapeDtypeStruct((t, LANES), F32),
        jax.ShapeDtypeStruct((t, MLA_IN_TILES * LANES), F32),
        jax.ShapeDtypeStruct((t, RET_TILES * LANES), BF16),
        jax.ShapeDtypeStruct((t, RET_PAIRS * LANES), F32),
    )
    out_specs = tuple(
        pl.BlockSpec((s.shape[0], IN_TM), col) if s.shape[1] == t else pl.BlockSpec((IN_TM, s.shape[1]), row)
        for s in out_shapes)
    return pl.pallas_call(
        _inproj_kernel,
        grid=(t // IN_TM,),
        in_specs=[
            pl.BlockSpec((IN_TM, D_MODEL), row),
            pl.BlockSpec((1, D_MODEL), const),
            pl.BlockSpec((D_MODEL, N_PAD), const),
            pl.BlockSpec((NSA_T_ROWS, D_MODEL), const),
            pl.BlockSpec((IN_TM, 3 * LANES), row),
            pl.BlockSpec((IN_TM, 3 * LANES), row),
            pl.BlockSpec((IN_TM, 3 * LANES), row),
        ],
        out_specs=out_specs,
        out_shape=out_shapes,
        compiler_params=pltpu.CompilerParams(
            dimension_semantics=("parallel",), vmem_limit_bytes=VMEM_LIMIT),
        name="in_proj",
    )(x2, gain, w, wt, tab_n, tab_m, tab_r)


def _compress_kernel(k_ref, v_ref, pos_ref, wlo_ref, whi_ref, w2k_ref, w2vt_ref, kc_ref, vct_ref):
    n_blk = k_ref.shape[1]

    def hidden(src, i):
        tok = src[0]
        lo = _nn((tok + pos_ref[i, 0:1, :]).astype(BF16), wlo_ref[i])
        hi = _nn((tok + pos_ref[i, 1:2, :]).astype(BF16), whi_ref[i])
        return jax.nn.gelu(lo + pltpu.roll(hi, n_blk - 1, 0)).astype(BF16)

    kc_ref[0] = _nn(hidden(k_ref, 0), w2k_ref[...]).astype(BF16)
    vct_ref[0] = _nt(w2vt_ref[...], hidden(v_ref, 1)).astype(BF16)


def _compress(k16, v16, pos, wlo, whi, w2k, w2vt):
    b, n_blk, width = k16.shape
    const3 = lambda i: (0, 0, 0)
    return pl.pallas_call(
        _compress_kernel,
        grid=(b,),
        in_specs=[
            pl.BlockSpec((1, n_blk, width), lambda i: (i, 0, 0)),
            pl.BlockSpec((1, n_blk, width), lambda i: (i, 0, 0)),
            pl.BlockSpec(pos.shape, const3),
            pl.BlockSpec(wlo.shape, const3),
            pl.BlockSpec(whi.shape, const3),
            pl.BlockSpec(w2k.shape, lambda i: (0, 0)),
            pl.BlockSpec(w2vt.shape, lambda i: (0, 0)),
        ],
        out_specs=(pl.BlockSpec((1, n_blk, LANES), lambda i: (i, 0, 0)),
                   pl.BlockSpec((1, LANES, n_blk), lambda i: (i, 0, 0))),
        out_shape=(jax.ShapeDtypeStruct((b, n_blk, LANES), BF16),
                   jax.ShapeDtypeStruct((b, LANES, n_blk), BF16)),
        compiler_params=pltpu.CompilerParams(
            dimension_semantics=("parallel",), vmem_limit_bytes=VMEM_LIMIT),
        name="nsa_compress",
    )(k16, v16, pos, wlo, whi, w2k, w2vt)


def _softmax_step(s_t, v_t, m, l, acc):
    m_new = jnp.maximum(m, jnp.max(s_t, axis=0, keepdims=True))
    alpha = jnp.exp2(m - m_new)
    p = jnp.exp2(s_t - m_new)
    l = alpha * l + jnp.sum(p, axis=0, keepdims=True)
    acc = alpha * acc + _nn(v_t, p.astype(BF16))
    return m_new, l, acc


def _softmax_init(cols):
    return (jnp.full((1, cols), NEG_INF, F32), jnp.zeros((1, cols), F32),
            jnp.zeros((HEAD_DIM, cols), F32))


def _softmax_finish(l, acc):
    return acc * (1.0 / jnp.maximum(l, 1e-30))


def _nsa_kernel(q_ref, gatet_ref, kc_ref, vct_ref, ks_ref, kw_ref, vst_ref, vwt_ref, ovl_ref,
                o_ref, out_t_ref, selb_ref):
    tq = ATT_T
    qi = pl.program_id(1)
    q0 = pl.multiple_of(qi * tq, tq)
    lo1 = _lane_lo((1, LANES))
    group_lanes = (lo1, jnp.logical_not(lo1))
    gate = jax.nn.sigmoid(gatet_ref[...])
    heads = range(NSA_HEADS)

    def q_head(h):
        j, g = h % NSA_REP, h // NSA_REP
        tile = q_ref[:, j * LANES:(j + 1) * LANES]
        return jnp.where(group_lanes[g], tile, jnp.zeros_like(tile))

    def v_rows(ref, h, k0, n):
        g = h // NSA_REP
        return ref[g * HEAD_DIM:(g + 1) * HEAD_DIM, pl.ds(k0, n)]

    def emit(h, branch, o_t):
        j, g = h % NSA_REP, h // NSA_REP
        r0 = j * LANES + g * HEAD_DIM
        val = gate[h * N_BRANCH + branch:h * N_BRANCH + branch + 1, :] * o_t
        if branch == 0:
            out_t_ref[r0:r0 + HEAD_DIM, :] = val
        else:
            out_t_ref[r0:r0 + HEAD_DIM, :] += val

    qm = [q_head(h) for h in heads]

    n_cmp_pad = kc_ref.shape[1]
    kc = kc_ref[0]
    vct = vct_ref[0]
    n_i = lax.broadcasted_iota(jnp.int32, (n_cmp_pad, tq), 0)
    t_l = q0 + lax.broadcasted_iota(jnp.int32, (n_cmp_pad, tq), 1)
    cmask = (n_i * CMP_STRIDE + (CMP_LEN - 1)) <= t_l
    cmask_f = cmask.astype(F32)
    psum = [None] * NSA_KV_GROUPS
    for h in heads:
        g = h // NSA_REP
        s_t = jnp.where(cmask, _nt(kc, qm[h]), NEG_INF)
        m = jnp.max(s_t, axis=0, keepdims=True)
        p = jnp.exp2(s_t - m) * cmask_f
        p = p * (1.0 / jnp.maximum(jnp.sum(p, axis=0, keepdims=True), 1e-30))
        psum[g] = p if psum[g] is None else psum[g] + p
        emit(h, 0, _nn(vct[g * HEAD_DIM:(g + 1) * HEAD_DIM, :], p.astype(BF16)))

    n_sel = ovl_ref.shape[0]
    m_i = lax.broadcasted_iota(jnp.int32, (n_sel, tq), 0)
    cur = jnp.right_shift(q0 + lax.broadcasted_iota(jnp.int32, (n_sel, tq), 1), SEL_BLOCK.bit_length() - 1)
    valid = m_i <= cur
    forced = (m_i == 0) | (m_i == cur) | (m_i == cur - 1)
    ovl = ovl_ref[...]
    for g in range(NSA_KV_GROUPS):
        p_hi = psum[g].astype(BF16)
        p_lo = (psum[g] - p_hi.astype(F32)).astype(BF16)
        imp = _nn(ovl, p_hi) + _nn(ovl, p_lo)
        imp = jnp.where(valid & forced, FORCE_SCORE, imp)
        imp = jnp.where(valid, imp, NEG_INF)
        rank = jnp.zeros((n_sel, tq), jnp.int32)
        for mp in range(n_sel):
            row = imp[mp:mp + 1, :]
            beats = (row > imp) | ((row == imp) & (m_i > mp))
            rank = rank + beats.astype(jnp.int32)
        selb_ref[g] = jnp.where(rank < SEL_TOP_N, 0.0, NEG_INF)

    key_i = lax.broadcasted_iota(jnp.int32, (tq, tq), 0)
    qry_i = lax.broadcasted_iota(jnp.int32, (tq, tq), 1)
    causal = key_i <= qry_i
    blocks_per_chunk = tq // SEL_BLOCK

    def sel_bias(g, c):
        rows = [jnp.broadcast_to(selb_ref[g, pl.ds(c * blocks_per_chunk + i, 1), :], (SEL_BLOCK, tq))
                for i in range(blocks_per_chunk)]
        return jnp.concatenate(rows, axis=0)

    def sel_chunk(c, k0, states, mask):
        k = ks_ref[pl.ds(k0, tq), :]
        bias = [sel_bias(g, c) for g in range(NSA_KV_GROUPS)]
        new = []
        for h in heads:
            s_t = _nt(k, qm[h]) + bias[h // NSA_REP]
            if mask is not None:
                s_t = jnp.where(mask, s_t, NEG_INF)
            new.append(_softmax_step(s_t, v_rows(vst_ref, h, k0, tq), *states[h]))
        return tuple(new)

    states = sel_chunk(qi, q0, tuple(_softmax_init(tq) for _ in heads), causal)
    states = lax.fori_loop(
        0, qi, lambda c, st: sel_chunk(c, pl.multiple_of(c * tq, tq), st, None), states)
    for h in heads:
        emit(h, 1, _softmax_finish(states[h][1], states[h][2]))

    def win_chunk(k0, states, mask):
        k = kw_ref[pl.ds(k0, tq), :]
        return tuple(
            _softmax_step(jnp.where(mask, _nt(k, qm[h]), NEG_INF), v_rows(vwt_ref, h, k0, tq), *states[h])
            for h in heads)

    states = win_chunk(q0, tuple(_softmax_init(tq) for _ in heads), causal)
    for d in range(1, WINDOW // tq + 1):
        k0 = pl.multiple_of(jnp.maximum(qi - d, 0) * tq, tq)
        in_band = (key_i - d * tq > qry_i - WINDOW) & (qi >= d)
        states = win_chunk(k0, states, in_band)
    for h in heads:
        emit(h, 2, _softmax_finish(states[h][1], states[h][2]))

    for j in range(NSA_REP):
        o_ref[:, j * LANES:(j + 1) * LANES] = out_t_ref[j * LANES:(j + 1) * LANES, :].T.astype(BF16)


def _nsa_attention(nsa, nsa_t, gate_t, kc, vct, ovl_t, batch, seq):
    tq = ATT_T
    nq = seq // tq
    n_blk = kc.shape[1]
    qrow = lambda b, i: (b * nq + i, 0)
    return pl.pallas_call(
        _nsa_kernel,
        grid=(batch, nq),
        in_specs=[
            pl.BlockSpec((tq, NSA_REP * LANES), qrow),
            pl.BlockSpec((GATE_ROWS, tq), lambda b, i: (0, b * nq + i)),
            pl.BlockSpec((1, n_blk, LANES), lambda b, i: (b, 0, 0)),
            pl.BlockSpec((1, LANES, n_blk), lambda b, i: (b, 0, 0)),
            pl.BlockSpec((seq, LANES), lambda b, i: (b, 3)),
            pl.BlockSpec((seq, LANES), lambda b, i: (b, 4)),
            pl.BlockSpec((LANES, seq), lambda b, i: (0, b)),
            pl.BlockSpec((LANES, seq), lambda b, i: (1, b)),
            pl.BlockSpec(ovl_t.shape, lambda b, i: (0, 0)),
        ],
        out_specs=pl.BlockSpec((tq, NSA_REP * LANES), qrow),
        out_shape=jax.ShapeDtypeStruct((batch * seq, NSA_REP * LANES), BF16),
        scratch_shapes=[pltpu.VMEM((NSA_REP * LANES, tq), F32),
                        pltpu.VMEM((NSA_KV_GROUPS, seq // SEL_BLOCK, tq), F32)],
        compiler_params=pltpu.CompilerParams(
            dimension_semantics=("parallel", "arbitrary"), vmem_limit_bytes=VMEM_LIMIT),
        name="nsa_attention",
    )(nsa, gate_t, kc, vct, nsa, nsa, nsa_t, nsa_t, ovl_t)


def _mla_up_kernel(in_ref, qn_ref, kvn_ref, wq_ref, wk_ref, wvt_ref, tab_ref, q_ref, k_ref, vt_ref):
    scale = (MLA_NOPE_DIM + MLA_ROPE_DIM) ** -0.5 * LOG2E
    cq = _rms(in_ref[:, 0:MLA_Q_RANK], qn_ref[...]).astype(BF16)
    ckv = _rms(in_ref[:, MLA_Q_RANK:MLA_Q_RANK + MLA_KV_RANK], kvn_ref[...]).astype(BF16)
    k_pe = in_ref[:, 3 * LANES:4 * LANES]
    tab = tab_ref[...]
    q = _nn(cq, wq_ref[...])
    k = _nn(ckv, wk_ref[...])
    for h in range(MLA_HEADS):
        sl = slice(h * LANES, (h + 1) * LANES)
        q_ref[:, sl] = (_rope(q[:, sl], tab, MLA_ROPE_DIM // 2) * scale).astype(BF16)
        k_ref[:, sl] = (k[:, sl] + k_pe).astype(BF16)
    vt_ref[...] = _nt(wvt_ref[...], ckv).astype(BF16)


def _mla_up(mla_in, q_norm, kv_norm, wq, wk, wvt, tab_m):
    t = mla_in.shape[0]
    row = lambda i: (i, 0)
    const = lambda i: (0, 0)
    return pl.pallas_call(
        _mla_up_kernel,
        grid=(t // MLA_TM,),
        in_specs=[
            pl.BlockSpec((MLA_TM, MLA_IN_TILES * LANES), row),
            pl.BlockSpec((1, MLA_Q_RANK), const),
            pl.BlockSpec((1, MLA_KV_RANK), const),
            pl.BlockSpec(wq.shape, const),
            pl.BlockSpec(wk.shape, const),
            pl.BlockSpec(wvt.shape, const),
            pl.BlockSpec((MLA_TM, 3 * LANES), row),
        ],
        out_specs=(pl.BlockSpec((MLA_TM, MLA_HEADS * LANES), row),
                   pl.BlockSpec((MLA_TM, MLA_HEADS * LANES), row),
                   pl.BlockSpec((MLA_V_ROWS, MLA_TM), lambda i: (0, i))),
        out_shape=(jax.ShapeDtypeStruct((t, MLA_HEADS * LANES), BF16),
                   jax.ShapeDtypeStruct((t, MLA_HEADS * LANES), BF16),
                   jax.ShapeDtypeStruct((MLA_V_ROWS, t), BF16)),
        compiler_params=pltpu.CompilerParams(
            dimension_semantics=("parallel",), vmem_limit_bytes=VMEM_LIMIT),
        name="mla_up",
    )(mla_in, q_norm, kv_norm, wq, wk, wvt, tab_m)


def _mla_attn_kernel(q_ref, k_ref, vt_ref, o_ref, out_t_ref):
    tq = ATT_T
    qi = pl.program_id(1)
    q0 = pl.multiple_of(qi * tq, tq)
    key_i = lax.broadcasted_iota(jnp.int32, (tq, tq), 0)
    qry_i = lax.broadcasted_iota(jnp.int32, (tq, tq), 1)
    causal = key_i <= qry_i
    heads = range(MLA_HEADS)
    qh = [q_ref[:, h * LANES:(h + 1) * LANES] for h in heads]

    def chunk(k0, states, mask):
        new = []
        for h in heads:
            s_t = _nt(k_ref[pl.ds(k0, tq), h * LANES:(h + 1) * LANES], qh[h])
            if mask is not None:
                s_t = jnp.where(mask, s_t, NEG_INF)
            v_t = vt_ref[h * MLA_V_DIM:(h + 1) * MLA_V_DIM, pl.ds(k0, tq)]
            new.append(_softmax_step(s_t, v_t, *states[h]))
        return tuple(new)

    states = chunk(q0, tuple(_softmax_init(tq) for _ in heads), causal)
    states = lax.fori_loop(0, qi, lambda c, st: chunk(pl.multiple_of(c * tq, tq), st, None), states)
    for h in heads:
        out_t_ref[h * MLA_V_DIM:(h + 1) * MLA_V_DIM, :] = _softmax_finish(states[h][1], states[h][2])
    out_t_ref[MLA_V_ROWS:, :] = jnp.zeros((RET_PAIRS * LANES - MLA_V_ROWS, tq), F32)
    for j in range(RET_PAIRS):
        o_ref[:, j * LANES:(j + 1) * LANES] = out_t_ref[j * LANES:(j + 1) * LANES, :].T.astype(BF16)


def _mla_attention(q, k, vt, batch, seq):
    tq = ATT_T
    nq = seq // tq
    return pl.pallas_call(
        _mla_attn_kernel,
        grid=(batch, nq),
        in_specs=[
            pl.BlockSpec((tq, MLA_HEADS * LANES), lambda b, i: (b * nq + i, 0)),
            pl.BlockSpec((seq, MLA_HEADS * LANES), lambda b, i: (b, 0)),
            pl.BlockSpec((MLA_V_ROWS, seq), lambda b, i: (0, b)),
        ],
        out_specs=pl.BlockSpec((tq, RET_PAIRS * LANES), lambda b, i: (b * nq + i, 0)),
        out_shape=jax.ShapeDtypeStruct((batch * seq, RET_PAIRS * LANES), BF16),
        scratch_shapes=[pltpu.VMEM((RET_PAIRS * LANES, tq), F32)],
        compiler_params=pltpu.CompilerParams(
            dimension_semantics=("parallel", "arbitrary"), vmem_limit_bytes=VMEM_LIMIT),
        name="mla_attention",
    )(q, k, vt)


def _tn(a, b):
    return lax.dot_general(a, b, (((0,), (0,)), ((), ())), preferred_element_type=F32)


def _retention_kernel(q_ref, k_ref, v_ref, gate_ref, gn_ref, intra_ref, rd_ref, wd_ref, cd_ref, o_ref):
    c_len = RET_CHUNK
    n_chunks = q_ref.shape[0] // c_len
    lo = _lane_lo((1, LANES))
    hi = jnp.logical_not(lo)
    blockdiag = (lax.broadcasted_iota(jnp.int32, (LANES, LANES), 0) < HALF) == _lane_lo((LANES, LANES))
    intra_a = intra_ref[0, 0]
    intra_b = intra_ref[0, 1]
    read_decay = rd_ref[0]
    write_decay = wd_ref[0]
    chunk_decay = cd_ref[0]
    gn = gn_ref[0]

    def half_mean(x):
        s_lo = jnp.sum(jnp.where(lo, x, 0.0), axis=-1, keepdims=True)
        s_hi = jnp.sum(jnp.where(hi, x, 0.0), axis=-1, keepdims=True)
        return jnp.where(lo, s_lo, s_hi) * (1.0 / HEAD_DIM)

    def body(c, state):
        r0 = pl.multiple_of(c * c_len, c_len)
        qc = q_ref[pl.ds(r0, c_len), :]
        kc = k_ref[pl.ds(r0, c_len), :]
        vc = v_ref[pl.ds(r0, c_len), :]
        zero = jnp.zeros_like(qc)
        sa = _nt(jnp.where(lo, qc, zero), kc) * intra_a
        sb = _nt(jnp.where(hi, qc, zero), kc) * intra_b
        o = jnp.where(lo, _nn(sa.astype(BF16), vc), _nn(sb.astype(BF16), vc))
        o = o + _nn(qc, jnp.where(blockdiag, state, 0.0).astype(BF16)) * read_decay
        state = state * chunk_decay + _tn((kc.astype(F32) * write_decay).astype(BF16), vc)
        mu = half_mean(o)
        d = o - mu
        y = d * lax.rsqrt(half_mean(d * d) + NORM_EPS) * gn
        o_ref[pl.ds(r0, c_len), :] = (jax.nn.silu(gate_ref[pl.ds(r0, c_len), :]) * y).astype(BF16)
        return state

    lax.fori_loop(0, n_chunks, body, jnp.zeros((LANES, LANES), F32))


def _retention(ret, rgate, gn, intra, rd, wd, cd, batch, seq):
    pair_const3 = lambda b, j: (j, 0, 0)
    return pl.pallas_call(
        _retention_kernel,
        grid=(batch, RET_PAIRS),
        in_specs=[
            pl.BlockSpec((seq, LANES), lambda b, j: (b, j)),
            pl.BlockSpec((seq, LANES), lambda b, j: (b, RET_PAIRS + j)),
            pl.BlockSpec((seq, LANES), lambda b, j: (b, 2 * RET_PAIRS + j)),
            pl.BlockSpec((seq, LANES), lambda b, j: (b, j)),
            pl.BlockSpec((1, 1, LANES), pair_const3),
            pl.BlockSpec((1, 2, RET_CHUNK, RET_CHUNK), lambda b, j: (j, 0, 0, 0)),
            pl.BlockSpec((1, RET_CHUNK, LANES), pair_const3),
            pl.BlockSpec((1, RET_CHUNK, LANES), pair_const3),
            pl.BlockSpec((1, 1, LANES), pair_const3),
        ],
        out_specs=pl.BlockSpec((seq, LANES), lambda b, j: (b, j)),
        out_shape=jax.ShapeDtypeStruct((batch * seq, RET_PAIRS * LANES), BF16),
        compiler_params=pltpu.CompilerParams(
            dimension_semantics=("parallel", "arbitrary"), vmem_limit_bytes=VMEM_LIMIT),
        name="retention",
    )(ret, ret, ret, rgate, gn, intra, rd, wd, cd)


def _out_mlp_kernel(x_ref, nsa_ref, mla_ref, ret_ref, wo_ref, g2_ref, wu_ref, wd_ref, gf_ref, o_ref,
                    *, final_norm):
    w = RET_PAIRS * LANES
    mixed = (_nn(nsa_ref[...], wo_ref[0:w, :]) + _nn(mla_ref[...], wo_ref[w:2 * w, :])
             + _nn(ret_ref[...], wo_ref[2 * w:3 * w, :]))
    x = x_ref[...] + mixed
    h = _rms(x, g2_ref[...]).astype(BF16)
    y = x
    for c in range(D_FF // MLP_FF_CHUNK):
        sl = slice(c * MLP_FF_CHUNK, (c + 1) * MLP_FF_CHUNK)
        u = jnp.maximum(_nn(h, wu_ref[:, sl]), 0.0)
        y = y + _nn((u * u).astype(BF16), wd_ref[sl, :])
    if final_norm:
        y = _rms(y, gf_ref[...])
    o_ref[...] = y


def _out_mlp(x2, o_nsa, o_mla, o_ret, wo, g2, wu, wd, gf, final_norm):
    t = x2.shape[0]
    w = RET_PAIRS * LANES
    row = lambda i: (i, 0)
    const = lambda i: (0, 0)
    resident = dict(pipeline_mode=pl.Buffered(1))
    return pl.pallas_call(
        functools.partial(_out_mlp_kernel, final_norm=final_norm),
        grid=(t // MLP_TM,),
        in_specs=[
            pl.BlockSpec((MLP_TM, D_MODEL), row),
            pl.BlockSpec((MLP_TM, w), row),
            pl.BlockSpec((MLP_TM, w), row),
            pl.BlockSpec((MLP_TM, w), row),
            pl.BlockSpec(wo.shape, const, **resident),
            pl.BlockSpec((1, D_MODEL), const),
            pl.BlockSpec(wu.shape, const, **resident),
            pl.BlockSpec(wd.shape, const, **resident),
            pl.BlockSpec((1, D_MODEL), const),
        ],
        out_specs=pl.BlockSpec((MLP_TM, D_MODEL), row),
        out_shape=jax.ShapeDtypeStruct((t, D_MODEL), F32),
        compiler_params=pltpu.CompilerParams(
            dimension_semantics=("parallel",), vmem_limit_bytes=VMEM_LIMIT),
        name="out_mlp",
    )(x2, o_nsa, o_mla, o_ret, wo, g2, wu, wd, gf)


def _rope_table(positions, dim, theta, period, base):
    half = dim // 2
    inv = 1.0 / (theta ** (jnp.arange(0, dim, 2, dtype=F32) / dim))
    ang = positions.reshape(-1).astype(F32)[:, None] * inv
    cos, sin = jnp.cos(ang), jnp.sin(ang)
    rel = (np.arange(LANES) - base) % period
    first = (rel < half) & (np.arange(LANES) >= base)
    second = (rel >= half) & (rel < dim) & (np.arange(LANES) >= base)
    idx = np.where(first, rel, np.where(second, rel - half, 0))
    cos_l = jnp.where(first | second, cos[:, idx], 1.0)
    sin_l = sin[:, idx]
    return jnp.concatenate([cos_l, jnp.where(first, -sin_l, 0.0), jnp.where(second, sin_l, 0.0)], axis=1)


def _pad_cols(w, n):
    return jnp.pad(w, ((0, 0), (0, 0), (0, n - w.shape[-1])))


def _in_weight(w_in):
    offs = np.cumsum((0,) + IN_SIZES)
    seg = [w_in[:, :, offs[i]:offs[i + 1]] for i in range(len(IN_SIZES))]
    (nsa_q, k_cmp, v_cmp, k_slc, v_slc, k_win, v_win, gate,
     cq, ckv, kpe, ret_q, ret_k, ret_v, ret_g) = seg
    scale = HEAD_DIM ** -0.5
    qh = [nsa_q[:, :, h * HEAD_DIM:(h + 1) * HEAD_DIM] * scale for h in range(NSA_HEADS)]
    tiles = [jnp.concatenate([qh[j], qh[j + NSA_REP]], axis=-1) for j in range(NSA_REP)]
    tiles += [k_slc, k_win, k_cmp, v_cmp]
    tiles += [cq, ckv, jnp.pad(kpe, ((0, 0), (0, 0), (HALF, LANES - HALF - MLA_ROPE_DIM)))]
    ret_w = RET_PAIRS * LANES
    tiles += [_pad_cols(ret_q, ret_w), _pad_cols(ret_k * scale, ret_w), _pad_cols(ret_v, ret_w),
              _pad_cols(ret_g, ret_w)]
    w_t = jnp.concatenate([v_slc, v_win, _pad_cols(gate, GATE_ROWS)], axis=-1).transpose(0, 2, 1)
    return jnp.concatenate(tiles, axis=-1).astype(BF16), w_t.astype(BF16)


def _compress_weights(pos, w1, w2):
    nl = pos.shape[0]
    g, dh, hid = NSA_KV_GROUPS, HEAD_DIM, CMP_HIDDEN
    p = pos.reshape(nl, 2, CMP_STRIDE, 1, dh)
    p = jnp.broadcast_to(p, (nl, 2, CMP_STRIDE, g, dh)).reshape(nl, 2, CMP_STRIDE * g * dh)
    w = w1.reshape(nl, 2, CMP_STRIDE, dh, hid)
    eye = jnp.eye(g, dtype=w1.dtype)
    w = jnp.einsum('lhrdc,gk->lhrgdkc', w, eye).reshape(nl, 2, CMP_STRIDE * g * dh, g * hid)
    w2b = jnp.einsum('lcd,gk->lgckd', w2, eye).reshape(nl, g * hid, g * dh)
    return p, w[:, 0].astype(BF16), w[:, 1].astype(BF16), w2b.astype(BF16)


def _mla_weights(w_uq, w_ukv):
    nl = w_uq.shape[0]
    dq = MLA_NOPE_DIM + MLA_ROPE_DIM
    wq = w_uq.reshape(nl, MLA_Q_RANK, MLA_HEADS, dq)
    wq = jnp.pad(wq, ((0, 0), (0, 0), (0, 0), (0, LANES - dq))).reshape(nl, MLA_Q_RANK, MLA_HEADS * LANES)
    wkv = w_ukv.reshape(nl, MLA_KV_RANK, MLA_HEADS, MLA_NOPE_DIM + MLA_V_DIM)
    wk = jnp.pad(wkv[..., :MLA_NOPE_DIM], ((0, 0), (0, 0), (0, 0), (0, LANES - MLA_NOPE_DIM)))
    wk = wk.reshape(nl, MLA_KV_RANK, MLA_HEADS * LANES)
    wvt = wkv[..., MLA_NOPE_DIM:].reshape(nl, MLA_KV_RANK, MLA_V_ROWS).transpose(0, 2, 1)
    return wq.astype(BF16), wk.astype(BF16), wvt.astype(BF16)


def _out_weight(w_out):
    nl = w_out.shape[0]
    pad_rows = lambda w: jnp.pad(w, ((0, 0), (0, RET_PAIRS * LANES - w.shape[1]), (0, 0)))
    nsa = w_out[:, :NSA_Q_W].reshape(nl, NSA_HEADS, HEAD_DIM, D_MODEL)
    order = [h for j in range(NSA_REP) for h in (j, j + NSA_REP)]
    nsa = nsa[:, order].reshape(nl, NSA_Q_W, D_MODEL)
    mla = pad_rows(w_out[:, NSA_Q_W:NSA_Q_W + MLA_HEADS * MLA_V_DIM])
    ret = pad_rows(w_out[:, NSA_Q_W + MLA_HEADS * MLA_V_DIM:])
    return jnp.concatenate([nsa, mla, ret], axis=1).astype(BF16)


def _retention_tables(gn_gain):
    nh = 2 * RET_PAIRS
    log_g = jnp.log(1.0 - 2.0 ** (-5.0 - jnp.arange(nh, dtype=F32)))
    i = jnp.arange(RET_CHUNK, dtype=F32)
    diff = i[:, None] - i[None, :]
    intra = jnp.where(diff >= 0, jnp.exp(jnp.maximum(diff, 0.0)[None] * log_g[:, None, None]), 0.0)
    read_decay = jnp.exp((i + 1.0)[None, :] * log_g[:, None])
    write_decay = jnp.exp((RET_CHUNK - 1.0 - i)[None, :] * log_g[:, None])
    chunk_decay = jnp.exp(RET_CHUNK * log_g)

    def lanes(t):
        t = t.reshape(RET_PAIRS, 2, -1)
        return jnp.repeat(t.transpose(0, 2, 1), HALF, axis=-1)

    gn = jnp.pad(gn_gain, ((0, 0), (0, nh - RET_HEADS), (0, 0)))
    gn = gn.reshape(gn.shape[0], RET_PAIRS, 1, LANES)
    return (gn, intra.reshape(RET_PAIRS, 2, RET_CHUNK, RET_CHUNK), lanes(read_decay), lanes(write_decay),
            lanes(chunk_decay[:, None]))


def _selection_overlap(seq):
    n_cmp = (seq - CMP_LEN) // CMP_STRIDE + 1
    n_sel = seq // SEL_BLOCK
    cs = np.arange(n_cmp) * CMP_STRIDE
    ss = np.arange(n_sel) * SEL_BLOCK
    ov = np.clip(np.minimum(cs[:, None] + CMP_LEN, ss[None, :] + SEL_BLOCK)
                 - np.maximum(cs[:, None], ss[None, :]), 0, None) / CMP_LEN
    ovl_t = np.zeros((n_sel, seq // CMP_STRIDE), np.float32)
    ovl_t[:, :n_cmp] = ov.T
    return jnp.asarray(ovl_t, BF16)


def kernel(x, positions, ln1_gain, w_in, cmp_pos_k, cmp_w1_k, cmp_w2_k, cmp_pos_v, cmp_w1_v, cmp_w2_v,
           mla_q_norm, mla_w_uq, mla_kv_norm, mla_w_ukv, ret_gn_gain, w_out, ln2_gain, w_up, w_down,
           final_gain):
    batch, seq, _ = x.shape
    depth = w_in.shape[0]
    t = batch * seq

    tab_n = _rope_table(positions, PARTIAL_ROPE_DIM, ROPE_THETA, HEAD_DIM, 0)
    tab_m = _rope_table(positions, MLA_ROPE_DIM, ROPE_THETA, LANES, HALF)
    tab_r = _rope_table(positions, HEAD_DIM, RET_THETA, HEAD_DIM, 0)

    w_in_p, w_in_t = _in_weight(w_in)
    pos_k, wlo_k, whi_k, w2_k = _compress_weights(cmp_pos_k, cmp_w1_k, cmp_w2_k)
    pos_v, wlo_v, whi_v, w2_v = _compress_weights(cmp_pos_v, cmp_w1_v, cmp_w2_v)
    cmp_pos = jnp.stack([pos_k, pos_v], axis=1)
    cmp_wlo = jnp.stack([wlo_k, wlo_v], axis=1)
    cmp_whi = jnp.stack([whi_k, whi_v], axis=1)
    w2_vt = w2_v.transpose(0, 2, 1)
    wq, wk, wvt = _mla_weights(mla_w_uq, mla_w_ukv)
    wo = _out_weight(w_out)
    wu = w_up.astype(BF16)
    wd = w_down.astype(BF16)
    gn, intra, rd, wdec, cd = _retention_tables(ret_gn_gain)
    ovl_t = _selection_overlap(seq)
    gf = final_gain.reshape(1, D_MODEL)

    x2 = x.reshape(t, D_MODEL)
    rows16 = seq // CMP_STRIDE
    for l in range(depth):
        nsa, nsa_t, gate_t, k_cmp, v_cmp, mla_in, ret, rgate = _in_proj(
            x2, ln1_gain[l].reshape(1, D_MODEL), w_in_p[l], w_in_t[l], tab_n, tab_m, tab_r)
        kc, vct = _compress(k_cmp.reshape(batch, rows16, CMP_STRIDE * LANES),
                            v_cmp.reshape(batch, rows16, CMP_STRIDE * LANES),
                            cmp_pos[l], cmp_wlo[l], cmp_whi[l], w2_k[l], w2_vt[l])
        o_nsa = _nsa_attention(nsa, nsa_t, gate_t, kc, vct, ovl_t, batch, seq)
        q_m, k_m, vt_m = _mla_up(mla_in, mla_q_norm[l].reshape(1, -1), mla_kv_norm[l].reshape(1, -1),
                                 wq[l], wk[l], wvt[l], tab_m)
        o_mla = _mla_attention(q_m, k_m, vt_m, batch, seq)
        o_ret = _retention(ret, rgate, gn[l], intra, rd, wdec, cd, batch, seq)
        x2 = _out_mlp(x2, o_nsa, o_mla, o_ret, wo[l], ln2_gain[l].reshape(1, D_MODEL), wu[l], wd[l], gf,
                      final_norm=(l == depth - 1))
    return x2.reshape(batch, seq, D_MODEL)
```

```python
import functools
import math

import numpy as np
import jax
import jax.numpy as jnp
from jax import lax
from jax.experimental import pallas as pl
from jax.experimental.pallas import tpu as pltpu

F32 = jnp.float32
BF16 = jnp.bfloat16

D_MODEL = 1024
HEAD_DIM = 64
NSA_HEADS = 6
NSA_KV_GROUPS = 2
NSA_REP = NSA_HEADS // NSA_KV_GROUPS
N_BRANCH = 3
CMP_LEN = 32
CMP_STRIDE = 16
CMP_HIDDEN = 2 * HEAD_DIM
SEL_BLOCK = 64
SEL_TOP_N = 16
WINDOW = 512
MLA_HEADS = 5
MLA_Q_RANK = 256
MLA_KV_RANK = 128
MLA_NOPE_DIM = 64
MLA_ROPE_DIM = 32
MLA_V_DIM = 64
RET_HEADS = 5
RET_CHUNK = 128
ROPE_THETA = 500000.0
PARTIAL_ROPE_DIM = HEAD_DIM // 4
RET_THETA = 10000.0
D_FF = 4 * D_MODEL
NORM_EPS = 1e-6
NEG_INF = -1e30
FORCE_SCORE = 1e9
LOG2E = math.log2(math.e)

NSA_Q_W = NSA_HEADS * HEAD_DIM
NSA_KV_W = NSA_KV_GROUPS * HEAD_DIM
NSA_GATE_W = NSA_HEADS * N_BRANCH
RET_W = RET_HEADS * HEAD_DIM
IN_SIZES = (NSA_Q_W, NSA_KV_W, NSA_KV_W, NSA_KV_W, NSA_KV_W, NSA_KV_W, NSA_KV_W, NSA_GATE_W,
            MLA_Q_RANK, MLA_KV_RANK, MLA_ROPE_DIM, RET_W, RET_W, RET_W, RET_W)

LANES = 128
HALF = LANES // 2
VMEM_LIMIT = 56 * 1024 * 1024

NSA_TILES = 5
GATE_ROWS = 32
NSA_T_ROWS = 2 * LANES + GATE_ROWS
MLA_IN_TILES = 4
RET_PAIRS = 3
RET_TILES = 3 * RET_PAIRS
IN_TILES = NSA_TILES + 2 + MLA_IN_TILES + RET_TILES + RET_PAIRS
N_PAD = IN_TILES * LANES
MLA_V_ROWS = MLA_HEADS * MLA_V_DIM

IN_TM = 256
MLP_TM = 512
MLA_TM = 512
ATT_T = 512
MLP_FF_CHUNK = 1024


def _nn(a, b):
    return jnp.dot(a, b, preferred_element_type=F32)


def _nt(a, b):
    return lax.dot_general(a, b, (((1,), (1,)), ((), ())), preferred_element_type=F32)


def _rms(x, gain):
    return x * lax.rsqrt(jnp.mean(x * x, axis=-1, keepdims=True) + NORM_EPS) * gain


def _rope(val, tab, half):
    cos = tab[:, 0:LANES]
    sin_a = tab[:, LANES:2 * LANES]
    sin_b = tab[:, 2 * LANES:3 * LANES]
    return (val * cos + pltpu.roll(val, LANES - half, 1) * sin_a
            + pltpu.roll(val, half, 1) * sin_b)


def _lane_lo(shape):
    return lax.broadcasted_iota(jnp.int32, shape, len(shape) - 1) < HALF


def _inproj_kernel(x_ref, g_ref, w_ref, wt_ref, tn_ref, tm_ref, tr_ref,
                   nsa_ref, nsat_ref, gatet_ref, kcmp_ref, vcmp_ref, mla_ref, ret_ref, rgate_ref):
    h = _rms(x_ref[...], g_ref[...]).astype(BF16)
    tab_n = tn_ref[...]
    tab_r = tr_ref[...]

    def tile(a, i):
        return a[:, i * LANES:(i + 1) * LANES]

    c0 = 0
    a = _nn(h, w_ref[:, c0:c0 + NSA_TILES * LANES])
    for i in range(NSA_TILES):
        v = _rope(tile(a, i), tab_n, PARTIAL_ROPE_DIM // 2)
        if i < NSA_REP:
            v = v * LOG2E
        nsa_ref[:, i * LANES:(i + 1) * LANES] = v.astype(BF16)
    c0 += NSA_TILES * LANES
    a = _nn(h, w_ref[:, c0:c0 + 2 * LANES])
    kcmp_ref[...] = _rope(tile(a, 0), tab_n, PARTIAL_ROPE_DIM // 2)
    vcmp_ref[...] = tile(a, 1)
    c0 += 2 * LANES
    a = _nn(h, w_ref[:, c0:c0 + MLA_IN_TILES * LANES])
    mla_ref[:, 0:3 * LANES] = a[:, 0:3 * LANES]
    mla_ref[:, 3 * LANES:4 * LANES] = _rope(tile(a, 3), tm_ref[...], MLA_ROPE_DIM // 2)
    c0 += MLA_IN_TILES * LANES
    a = _nn(h, w_ref[:, c0:c0 + RET_TILES * LANES])
    for i in range(RET_TILES):
        v = tile(a, i)
        if i < 2 * RET_PAIRS:
            v = _rope(v, tab_r, HEAD_DIM // 2)
        ret_ref[:, i * LANES:(i + 1) * LANES] = v.astype(BF16)
    c0 += RET_TILES * LANES
    rgate_ref[...] = _nn(h, w_ref[:, c0:c0 + RET_PAIRS * LANES])
    at = _nt(wt_ref[...], h)
    nsat_ref[...] = at[0:2 * LANES, :].astype(BF16)
    gatet_ref[...] = at[2 * LANES:NSA_T_ROWS, :]


def _in_proj(x2, gain, w, wt, tab_n, tab_m, tab_r):
    t = x2.shape[0]
    row = lambda i: (i, 0)
    col = lambda i: (0, i)
    const = lambda i: (0, 0)
    out_shapes = (
        jax.ShapeDtypeStruct((t, NSA_TILES * LANES), BF16),
        jax.ShapeDtypeStruct((2 * LANES, t), BF16),
        jax.ShapeDtypeStruct((GATE_ROWS, t), F32),
        jax.ShapeDtypeStruct((t, LANES), F32),
        jax.ShapeDtypeStruct((t, LANES), F32),
        jax.ShapeDtypeStruct((t, MLA_IN_TILES * LANES), F32),
        jax.ShapeDtypeStruct((t, RET_TILES * LANES), BF16),
        jax.ShapeDtypeStruct((t, RET_PAIRS * LANES), F32),
    )
    out_specs = tuple(
        pl.BlockSpec((s.shape[0], IN_TM), col) if s.shape[1] == t else pl.BlockSpec((IN_TM, s.shape[1]), row)
        for s in out_shapes)
    return pl.pallas_call(
        _inproj_kernel,
        grid=(t // IN_TM,),
        in_specs=[
            pl.BlockSpec((IN_TM, D_MODEL), row),
            pl.BlockSpec((1, D_MODEL), const),
            pl.BlockSpec((D_MODEL, N_PAD), const),
            pl.BlockSpec((NSA_T_ROWS, D_MODEL), const),
            pl.BlockSpec((IN_TM, 3 * LANES), row),
            pl.BlockSpec((IN_TM, 3 * LANES), row),
            pl.BlockSpec((IN_TM, 3 * LANES), row),
        ],
        out_specs=out_specs,
        out_shape=out_shapes,
        compiler_params=pltpu.CompilerParams(
            dimension_semantics=("parallel",), vmem_limit_bytes=VMEM_LIMIT),
        name="in_proj",
    )(x2, gain, w, wt, tab_n, tab_m, tab_r)


def _compress_kernel(k_ref, v_ref, pos_ref, wlo_ref, whi_ref, w2k_ref, w2vt_ref, kc_ref, vct_ref):
    n_blk = k_ref.shape[1]

    def hidden(src, i):
        tok = src[0]
        lo = _nn((tok + pos_ref[i, 0:1, :]).astype(BF16), wlo_ref[i])
        hi = _nn((tok + pos_ref[i, 1:2, :]).astype(BF16), whi_ref[i])
        return jax.nn.gelu(lo + pltpu.roll(hi, n_blk - 1, 0)).astype(BF16)

    kc_ref[0] = _nn(hidden(k_ref, 0), w2k_ref[...]).astype(BF16)
    vct_ref[0] = _nt(w2vt_ref[...], hidden(v_ref, 1)).astype(BF16)


def _compress(k16, v16, pos, wlo, whi, w2k, w2vt):
    b, n_blk, width = k16.shape
    const3 = lambda i: (0, 0, 0)
    return pl.pallas_call(
        _compress_kernel,
        grid=(b,),
        in_specs=[
            pl.BlockSpec((1, n_blk, width), lambda i: (i, 0, 0)),
            pl.BlockSpec((1, n_blk, width), lambda i: (i, 0, 0)),
            pl.BlockSpec(pos.shape, const3),
            pl.BlockSpec(wlo.shape, const3),
            pl.BlockSpec(whi.shape, const3),
            pl.BlockSpec(w2k.shape, lambda i: (0, 0)),
            pl.BlockSpec(w2vt.shape, lambda i: (0, 0)),
        ],
        out_specs=(pl.BlockSpec((1, n_blk, LANES), lambda i: (i, 0, 0)),
                   pl.BlockSpec((1, LANES, n_blk), lambda i: (i, 0, 0))),
        out_shape=(jax.ShapeDtypeStruct((b, n_blk, LANES), BF16),
                   jax.ShapeDtypeStruct((b, LANES, n_blk), BF16)),
        compiler_params=pltpu.CompilerParams(
            dimension_semantics=("parallel",), vmem_limit_bytes=VMEM_LIMIT),
        name="nsa_compress",
    )(k16, v16, pos, wlo, whi, w2k, w2vt)


SCORE_LOOKAHEAD = 2


def _softmax_steps(score_fns, v_ts, states):
    n = len(score_fns)
    s_ts = [score_fns[h]() if h < SCORE_LOOKAHEAD else None for h in range(n)]
    out = []
    for h in range(n):
        if h + SCORE_LOOKAHEAD < n:
            s_ts[h + SCORE_LOOKAHEAD] = score_fns[h + SCORE_LOOKAHEAD]()
        m_old, l_old, acc_old = states[h]
        m = jnp.maximum(m_old, jnp.max(s_ts[h], axis=0, keepdims=True))
        p = jnp.exp2(s_ts[h] - m)
        s_ts[h] = None
        alpha = jnp.exp2(m_old - m)
        l = alpha * l_old + jnp.sum(p, axis=0, keepdims=True)
        out.append((m, l, alpha * acc_old + _nn(v_ts[h], p.astype(BF16))))
    return tuple(out)


def _softmax_init(cols):
    return (jnp.full((1, cols), NEG_INF, F32), jnp.zeros((1, cols), F32),
            jnp.zeros((HEAD_DIM, cols), F32))


def _softmax_finish(l, acc):
    return acc * (1.0 / jnp.maximum(l, 1e-30))


def _nsa_kernel(q_ref, gatet_ref, kc_ref, vct_ref, ks_ref, kw_ref, vst_ref, vwt_ref, ovl_ref,
                o_ref, out_t_ref, selb_ref):
    tq = ATT_T
    qi = pl.program_id(1)
    q0 = pl.multiple_of(qi * tq, tq)
    lo1 = _lane_lo((1, LANES))
    group_lanes = (lo1, jnp.logical_not(lo1))
    gate = jax.nn.sigmoid(gatet_ref[...])
    heads = range(NSA_HEADS)

    def q_head(h):
        j, g = h % NSA_REP, h // NSA_REP
        tile = q_ref[:, j * LANES:(j + 1) * LANES]
        return jnp.where(group_lanes[g], tile, jnp.zeros_like(tile))

    def v_rows(ref, h, k0, n):
        g = h // NSA_REP
        return ref[g * HEAD_DIM:(g + 1) * HEAD_DIM, pl.ds(k0, n)]

    def emit(h, branch, o_t):
        j, g = h % NSA_REP, h // NSA_REP
        r0 = j * LANES + g * HEAD_DIM
        val = gate[h * N_BRANCH + branch:h * N_BRANCH + branch + 1, :] * o_t
        if branch == 0:
            out_t_ref[r0:r0 + HEAD_DIM, :] = val
        else:
            out_t_ref[r0:r0 + HEAD_DIM, :] += val

    qm = [q_head(h) for h in heads]

    n_cmp_pad = kc_ref.shape[1]
    kc = kc_ref[0]
    vct = vct_ref[0]
    n_i = lax.broadcasted_iota(jnp.int32, (n_cmp_pad, tq), 0)
    t_l = q0 + lax.broadcasted_iota(jnp.int32, (n_cmp_pad, tq), 1)
    cmask = (n_i * CMP_STRIDE + (CMP_LEN - 1)) <= t_l
    cmask_f = cmask.astype(F32)
    psum = [None] * NSA_KV_GROUPS
    for h in heads:
        g = h // NSA_REP
        s_t = jnp.where(cmask, _nt(kc, qm[h]), NEG_INF)
        m = jnp.max(s_t, axis=0, keepdims=True)
        p = jnp.exp2(s_t - m) * cmask_f
        p = p * (1.0 / jnp.maximum(jnp.sum(p, axis=0, keepdims=True), 1e-30))
        psum[g] = p if psum[g] is None else psum[g] + p
        emit(h, 0, _nn(vct[g * HEAD_DIM:(g + 1) * HEAD_DIM, :], p.astype(BF16)))

    n_sel = ovl_ref.shape[0]
    m_i = lax.broadcasted_iota(jnp.int32, (n_sel, tq), 0)
    cur = jnp.right_shift(q0 + lax.broadcasted_iota(jnp.int32, (n_sel, tq), 1), SEL_BLOCK.bit_length() - 1)
    valid = m_i <= cur
    forced = (m_i == 0) | (m_i == cur) | (m_i == cur - 1)
    ovl = ovl_ref[...]
    for g in range(NSA_KV_GROUPS):
        p_hi = psum[g].astype(BF16)
        p_lo = (psum[g] - p_hi.astype(F32)).astype(BF16)
        imp = _nn(ovl, p_hi) + _nn(ovl, p_lo)
        imp = jnp.where(valid & forced, FORCE_SCORE, imp)
        imp = jnp.where(valid, imp, NEG_INF)
        rank = jnp.zeros((n_sel, tq), jnp.int32)
        for mp in range(n_sel):
            row = imp[mp:mp + 1, :]
            beats = (row > imp) | ((row == imp) & (m_i > mp))
            rank = rank + beats.astype(jnp.int32)
        selb_ref[g] = jnp.where(rank < SEL_TOP_N, 0.0, NEG_INF)

    key_i = lax.broadcasted_iota(jnp.int32, (tq, tq), 0)
    qry_i = lax.broadcasted_iota(jnp.int32, (tq, tq), 1)
    causal = key_i <= qry_i
    blocks_per_chunk = tq // SEL_BLOCK

    def sel_bias(g, c):
        rows = [jnp.broadcast_to(selb_ref[g, pl.ds(c * blocks_per_chunk + i, 1), :], (SEL_BLOCK, tq))
                for i in range(blocks_per_chunk)]
        return jnp.concatenate(rows, axis=0)

    def sel_chunk(c, k0, states, mask):
        k = ks_ref[pl.ds(k0, tq), :]
        bias = [sel_bias(g, c) for g in range(NSA_KV_GROUPS)]

        def score_fn(h):
            s_t = _nt(k, qm[h]) + bias[h // NSA_REP]
            return s_t if mask is None else jnp.where(mask, s_t, NEG_INF)

        return _softmax_steps([functools.partial(score_fn, h) for h in heads],
                              [v_rows(vst_ref, h, k0, tq) for h in heads], states)

    states = sel_chunk(qi, q0, tuple(_softmax_init(tq) for _ in heads), causal)
    states = lax.fori_loop(
        0, qi, lambda c, st: sel_chunk(c, pl.multiple_of(c * tq, tq), st, None), states)
    for h in heads:
        emit(h, 1, _softmax_finish(states[h][1], states[h][2]))

    def win_chunk(k0, states, mask):
        k = kw_ref[pl.ds(k0, tq), :]
        score_fns = [lambda h=h: jnp.where(mask, _nt(k, qm[h]), NEG_INF) for h in heads]
        return _softmax_steps(score_fns, [v_rows(vwt_ref, h, k0, tq) for h in heads], states)

    states = win_chunk(q0, tuple(_softmax_init(tq) for _ in heads), causal)
    for d in range(1, WINDOW // tq + 1):
        k0 = pl.multiple_of(jnp.maximum(qi - d, 0) * tq, tq)
        in_band = (key_i - d * tq > qry_i - WINDOW) & (qi >= d)
        states = win_chunk(k0, states, in_band)
    for h in heads:
        emit(h, 2, _softmax_finish(states[h][1], states[h][2]))

    for j in range(NSA_REP):
        o_ref[:, j * LANES:(j + 1) * LANES] = out_t_ref[j * LANES:(j + 1) * LANES, :].T.astype(BF16)


def _nsa_attention(nsa, nsa_t, gate_t, kc, vct, ovl_t, batch, seq):
    tq = ATT_T
    nq = seq // tq
    n_blk = kc.shape[1]
    qrow = lambda b, i: (b * nq + i, 0)
    return pl.pallas_call(
        _nsa_kernel,
        grid=(batch, nq),
        in_specs=[
            pl.BlockSpec((tq, NSA_REP * LANES), qrow),
            pl.BlockSpec((GATE_ROWS, tq), lambda b, i: (0, b * nq + i)),
            pl.BlockSpec((1, n_blk, LANES), lambda b, i: (b, 0, 0)),
            pl.BlockSpec((1, LANES, n_blk), lambda b, i: (b, 0, 0)),
            pl.BlockSpec((seq, LANES), lambda b, i: (b, 3)),
            pl.BlockSpec((seq, LANES), lambda b, i: (b, 4)),
            pl.BlockSpec((LANES, seq), lambda b, i: (0, b)),
            pl.BlockSpec((LANES, seq), lambda b, i: (1, b)),
            pl.BlockSpec(ovl_t.shape, lambda b, i: (0, 0)),
        ],
        out_specs=pl.BlockSpec((tq, NSA_REP * LANES), qrow),
        out_shape=jax.ShapeDtypeStruct((batch * seq, NSA_REP * LANES), BF16),
        scratch_shapes=[pltpu.VMEM((NSA_REP * LANES, tq), F32),
                        pltpu.VMEM((NSA_KV_GROUPS, seq // SEL_BLOCK, tq), F32)],
        compiler_params=pltpu.CompilerParams(
            dimension_semantics=("parallel", "arbitrary"), vmem_limit_bytes=VMEM_LIMIT),
        name="nsa_attention",
    )(nsa, gate_t, kc, vct, nsa, nsa, nsa_t, nsa_t, ovl_t)


def _mla_up_kernel(in_ref, qn_ref, kvn_ref, wq_ref, wk_ref, wvt_ref, tab_ref, q_ref, k_ref, vt_ref):
    scale = (MLA_NOPE_DIM + MLA_ROPE_DIM) ** -0.5 * LOG2E
    cq = _rms(in_ref[:, 0:MLA_Q_RANK], qn_ref[...]).astype(BF16)
    ckv = _rms(in_ref[:, MLA_Q_RANK:MLA_Q_RANK + MLA_KV_RANK], kvn_ref[...]).astype(BF16)
    k_pe = in_ref[:, 3 * LANES:4 * LANES]
    tab = tab_ref[...]
    q = _nn(cq, wq_ref[...])
    k = _nn(ckv, wk_ref[...])
    for h in range(MLA_HEADS):
        sl = slice(h * LANES, (h + 1) * LANES)
        q_ref[:, sl] = (_rope(q[:, sl], tab, MLA_ROPE_DIM // 2) * scale).astype(BF16)
        k_ref[:, sl] = (k[:, sl] + k_pe).astype(BF16)
    vt_ref[...] = _nt(wvt_ref[...], ckv).astype(BF16)


def _mla_up(mla_in, q_norm, kv_norm, wq, wk, wvt, tab_m):
    t = mla_in.shape[0]
    row = lambda i: (i, 0)
    const = lambda i: (0, 0)
    return pl.pallas_call(
        _mla_up_kernel,
        grid=(t // MLA_TM,),
        in_specs=[
            pl.BlockSpec((MLA_TM, MLA_IN_TILES * LANES), row),
            pl.BlockSpec((1, MLA_Q_RANK), const),
            pl.BlockSpec((1, MLA_KV_RANK), const),
            pl.BlockSpec(wq.shape, const),
            pl.BlockSpec(wk.shape, const),
            pl.BlockSpec(wvt.shape, const),
            pl.BlockSpec((MLA_TM, 3 * LANES), row),
        ],
        out_specs=(pl.BlockSpec((MLA_TM, MLA_HEADS * LANES), row),
                   pl.BlockSpec((MLA_TM, MLA_HEADS * LANES), row),
                   pl.BlockSpec((MLA_V_ROWS, MLA_TM), lambda i: (0, i))),
        out_shape=(jax.ShapeDtypeStruct((t, MLA_HEADS * LANES), BF16),
                   jax.ShapeDtypeStruct((t, MLA_HEADS * LANES), BF16),
                   jax.ShapeDtypeStruct((MLA_V_ROWS, t), BF16)),
        compiler_params=pltpu.CompilerParams(
            dimension_semantics=("parallel",), vmem_limit_bytes=VMEM_LIMIT),
        name="mla_up",
    )(mla_in, q_norm, kv_norm, wq, wk, wvt, tab_m)


def _mla_attn_kernel(q_ref, k_ref, vt_ref, o_ref, out_t_ref):
    tq = ATT_T
    qi = pl.program_id(1)
    q0 = pl.multiple_of(qi * tq, tq)
    key_i = lax.broadcasted_iota(jnp.int32, (tq, tq), 0)
    qry_i = lax.broadcasted_iota(jnp.int32, (tq, tq), 1)
    causal = key_i <= qry_i
    heads = range(MLA_HEADS)
    qh = [q_ref[:, h * LANES:(h + 1) * LANES] for h in heads]

    def chunk(k0, states, mask):
        def score_fn(h):
            s_t = _nt(k_ref[pl.ds(k0, tq), h * LANES:(h + 1) * LANES], qh[h])
            return s_t if mask is None else jnp.where(mask, s_t, NEG_INF)

        v_ts = [vt_ref[h * MLA_V_DIM:(h + 1) * MLA_V_DIM, pl.ds(k0, tq)] for h in heads]
        return _softmax_steps([functools.partial(score_fn, h) for h in heads], v_ts, states)

    states = chunk(q0, tuple(_softmax_init(tq) for _ in heads), causal)
    states = lax.fori_loop(0, qi, lambda c, st: chunk(pl.multiple_of(c * tq, tq), st, None), states)
    for h in heads:
        out_t_ref[h * MLA_V_DIM:(h + 1) * MLA_V_DIM, :] = _softmax_finish(states[h][1], states[h][2])
    out_t_ref[MLA_V_ROWS:, :] = jnp.zeros((RET_PAIRS * LANES - MLA_V_ROWS, tq), F32)
    for j in range(RET_PAIRS):
        o_ref[:, j * LANES:(j + 1) * LANES] = out_t_ref[j * LANES:(j + 1) * LANES, :].T.astype(BF16)


def _mla_attention(q, k, vt, batch, seq):
    tq = ATT_T
    nq = seq // tq
    return pl.pallas_call(
        _mla_attn_kernel,
        grid=(batch, nq),
        in_specs=[
            pl.BlockSpec((tq, MLA_HEADS * LANES), lambda b, i: (b * nq + i, 0)),
            pl.BlockSpec((seq, MLA_HEADS * LANES), lambda b, i: (b, 0)),
            pl.BlockSpec((MLA_V_ROWS, seq), lambda b, i: (0, b)),
        ],
        out_specs=pl.BlockSpec((tq, RET_PAIRS * LANES), lambda b, i: (b * nq + i, 0)),
        out_shape=jax.ShapeDtypeStruct((batch * seq, RET_PAIRS * LANES), BF16),
        scratch_shapes=[pltpu.VMEM((RET_PAIRS * LANES, tq), F32)],
        compiler_params=pltpu.CompilerParams(
            dimension_semantics=("parallel", "arbitrary"), vmem_limit_bytes=VMEM_LIMIT),
        name="mla_attention",
    )(q, k, vt)


def _tn(a, b):
    return lax.dot_general(a, b, (((0,), (0,)), ((), ())), preferred_element_type=F32)


def _retention_kernel(q_ref, k_ref, v_ref, gate_ref, gn_ref, intra_ref, rd_ref, wd_ref, cd_ref, o_ref):
    c_len = RET_CHUNK
    n_chunks = q_ref.shape[0] // c_len
    lo = _lane_lo((1, LANES))
    hi = jnp.logical_not(lo)
    blockdiag = (lax.broadcasted_iota(jnp.int32, (LANES, LANES), 0) < HALF) == _lane_lo((LANES, LANES))
    intra_a = intra_ref[0, 0]
    intra_b = intra_ref[0, 1]
    read_decay = rd_ref[0]
    write_decay = wd_ref[0]
    chunk_decay = cd_ref[0]
    gn = gn_ref[0]

    def half_mean(x):
        s_lo = jnp.sum(jnp.where(lo, x, 0.0), axis=-1, keepdims=True)
        s_hi = jnp.sum(jnp.where(hi, x, 0.0), axis=-1, keepdims=True)
        return jnp.where(lo, s_lo, s_hi) * (1.0 / HEAD_DIM)

    def body(c, state):
        r0 = pl.multiple_of(c * c_len, c_len)
        qc = q_ref[pl.ds(r0, c_len), :]
        kc = k_ref[pl.ds(r0, c_len), :]
        vc = v_ref[pl.ds(r0, c_len), :]
        zero = jnp.zeros_like(qc)
        sa = _nt(jnp.where(lo, qc, zero), kc) * intra_a
        sb = _nt(jnp.where(hi, qc, zero), kc) * intra_b
        o = jnp.where(lo, _nn(sa.astype(BF16), vc), _nn(sb.astype(BF16), vc))
        o = o + _nn(qc, jnp.where(blockdiag, state, 0.0).astype(BF16)) * read_decay
        state = state * chunk_decay + _tn((kc.astype(F32) * write_decay).astype(BF16), vc)
        mu = half_mean(o)
        d = o - mu
        y = d * lax.rsqrt(half_mean(d * d) + NORM_EPS) * gn
        o_ref[pl.ds(r0, c_len), :] = (jax.nn.silu(gate_ref[pl.ds(r0, c_len), :]) * y).astype(BF16)
        return state

    lax.fori_loop(0, n_chunks, body, jnp.zeros((LANES, LANES), F32))


def _retention(ret, rgate, gn, intra, rd, wd, cd, batch, seq):
    pair_const3 = lambda b, j: (j, 0, 0)
    return pl.pallas_call(
        _retention_kernel,
        grid=(batch, RET_PAIRS),
        in_specs=[
            pl.BlockSpec((seq, LANES), lambda b, j: (b, j)),
            pl.BlockSpec((seq, LANES), lambda b, j: (b, RET_PAIRS + j)),
            pl.BlockSpec((seq, LANES), lambda b, j: (b, 2 * RET_PAIRS + j)),
            pl.BlockSpec((seq, LANES), lambda b, j: (b, j)),
            pl.BlockSpec((1, 1, LANES), pair_const3),
            pl.BlockSpec((1, 2, RET_CHUNK, RET_CHUNK), lambda b, j: (j, 0, 0, 0)),
            pl.BlockSpec((1, RET_CHUNK, LANES), pair_const3),
            pl.BlockSpec((1, RET_CHUNK, LANES), pair_const3),
            pl.BlockSpec((1, 1, LANES), pair_const3),
        ],
        out_specs=pl.BlockSpec((seq, LANES), lambda b, j: (b, j)),
        out_shape=jax.ShapeDtypeStruct((batch * seq, RET_PAIRS * LANES), BF16),
        compiler_params=pltpu.CompilerParams(
            dimension_semantics=("parallel", "arbitrary"), vmem_limit_bytes=VMEM_LIMIT),
        name="retention",
    )(ret, ret, ret, rgate, gn, intra, rd, wd, cd)


def _out_mlp_kernel(x_ref, nsa_ref, mla_ref, ret_ref, wo_ref, g2_ref, wu_ref, wd_ref, gf_ref, o_ref,
                    *, final_norm):
    w = RET_PAIRS * LANES
    mixed = (_nn(nsa_ref[...], wo_ref[0:w, :]) + _nn(mla_ref[...], wo_ref[w:2 * w, :])
             + _nn(ret_ref[...], wo_ref[2 * w:3 * w, :]))
    x = x_ref[...] + mixed
    h = _rms(x, g2_ref[...]).astype(BF16)
    y = x
    for c in range(D_FF // MLP_FF_CHUNK):
        sl = slice(c * MLP_FF_CHUNK, (c + 1) * MLP_FF_CHUNK)
        u = jnp.maximum(_nn(h, wu_ref[:, sl]), 0.0)
        y = y + _nn((u * u).astype(BF16), wd_ref[sl, :])
    if final_norm:
        y = _rms(y, gf_ref[...])
    o_ref[...] = y


def _out_mlp(x2, o_nsa, o_mla, o_ret, wo, g2, wu, wd, gf, final_norm):
    t = x2.shape[0]
    w = RET_PAIRS * LANES
    row = lambda i: (i, 0)
    const = lambda i: (0, 0)
    resident = dict(pipeline_mode=pl.Buffered(1))
    return pl.pallas_call(
        functools.partial(_out_mlp_kernel, final_norm=final_norm),
        grid=(t // MLP_TM,),
        in_specs=[
            pl.BlockSpec((MLP_TM, D_MODEL), row),
            pl.BlockSpec((MLP_TM, w), row),
            pl.BlockSpec((MLP_TM, w), row),
            pl.BlockSpec((MLP_TM, w), row),
            pl.BlockSpec(wo.shape, const, **resident),
            pl.BlockSpec((1, D_MODEL), const),
            pl.BlockSpec(wu.shape, const, **resident),
            pl.BlockSpec(wd.shape, const, **resident),
            pl.BlockSpec((1, D_MODEL), const),
        ],
        out_specs=pl.BlockSpec((MLP_TM, D_MODEL), row),
        out_shape=jax.ShapeDtypeStruct((t, D_MODEL), F32),
        compiler_params=pltpu.CompilerParams(
            dimension_semantics=("parallel",), vmem_limit_bytes=VMEM_LIMIT),
        name="out_mlp",
    )(x2, o_nsa, o_mla, o_ret, wo, g2, wu, wd, gf)


def _rope_table(positions, dim, theta, period, base):
    half = dim // 2
    inv = 1.0 / (theta ** (jnp.arange(0, dim, 2, dtype=F32) / dim))
    ang = positions.reshape(-1).astype(F32)[:, None] * inv
    cos, sin = jnp.cos(ang), jnp.sin(ang)
    rel = (np.arange(LANES) - base) % period
    first = (rel < half) & (np.arange(LANES) >= base)
    second = (rel >= half) & (rel < dim) & (np.arange(LANES) >= base)
    idx = np.where(first, rel, np.where(second, rel - half, 0))
    cos_l = jnp.where(first | second, cos[:, idx], 1.0)
    sin_l = sin[:, idx]
    return jnp.concatenate([cos_l, jnp.where(first, -sin_l, 0.0), jnp.where(second, sin_l, 0.0)], axis=1)


def _pad_cols(w, n):
    return jnp.pad(w, ((0, 0), (0, 0), (0, n - w.shape[-1])))


def _in_weight(w_in):
    offs = np.cumsum((0,) + IN_SIZES)
    seg = [w_in[:, :, offs[i]:offs[i + 1]] for i in range(len(IN_SIZES))]
    (nsa_q, k_cmp, v_cmp, k_slc, v_slc, k_win, v_win, gate,
     cq, ckv, kpe, ret_q, ret_k, ret_v, ret_g) = seg
    scale = HEAD_DIM ** -0.5
    qh = [nsa_q[:, :, h * HEAD_DIM:(h + 1) * HEAD_DIM] * scale for h in range(NSA_HEADS)]
    tiles = [jnp.concatenate([qh[j], qh[j + NSA_REP]], axis=-1) for j in range(NSA_REP)]
    tiles += [k_slc, k_win, k_cmp, v_cmp]
    tiles += [cq, ckv, jnp.pad(kpe, ((0, 0), (0, 0), (HALF, LANES - HALF - MLA_ROPE_DIM)))]
    ret_w = RET_PAIRS * LANES
    tiles += [_pad_cols(ret_q, ret_w), _pad_cols(ret_k * scale, ret_w), _pad_cols(ret_v, ret_w),
              _pad_cols(ret_g, ret_w)]
    w_t = jnp.concatenate([v_slc, v_win, _pad_cols(gate, GATE_ROWS)], axis=-1).transpose(0, 2, 1)
    return jnp.concatenate(tiles, axis=-1).astype(BF16), w_t.astype(BF16)


def _compress_weights(pos, w1, w2):
    nl = pos.shape[0]
    g, dh, hid = NSA_KV_GROUPS, HEAD_DIM, CMP_HIDDEN
    p = pos.reshape(nl, 2, CMP_STRIDE, 1, dh)
    p = jnp.broadcast_to(p, (nl, 2, CMP_STRIDE, g, dh)).reshape(nl, 2, CMP_STRIDE * g * dh)
    w = w1.reshape(nl, 2, CMP_STRIDE, dh, hid)
    eye = jnp.eye(g, dtype=w1.dtype)
    w = jnp.einsum('lhrdc,gk->lhrgdkc', w, eye).reshape(nl, 2, CMP_STRIDE * g * dh, g * hid)
    w2b = jnp.einsum('lcd,gk->lgckd', w2, eye).reshape(nl, g * hid, g * dh)
    return p, w[:, 0].astype(BF16), w[:, 1].astype(BF16), w2b.astype(BF16)


def _mla_weights(w_uq, w_ukv):
    nl = w_uq.shape[0]
    dq = MLA_NOPE_DIM + MLA_ROPE_DIM
    wq = w_uq.reshape(nl, MLA_Q_RANK, MLA_HEADS, dq)
    wq = jnp.pad(wq, ((0, 0), (0, 0), (0, 0), (0, LANES - dq))).reshape(nl, MLA_Q_RANK, MLA_HEADS * LANES)
    wkv = w_ukv.reshape(nl, MLA_KV_RANK, MLA_HEADS, MLA_NOPE_DIM + MLA_V_DIM)
    wk = jnp.pad(wkv[..., :MLA_NOPE_DIM], ((0, 0), (0, 0), (0, 0), (0, LANES - MLA_NOPE_DIM)))
    wk = wk.reshape(nl, MLA_KV_RANK, MLA_HEADS * LANES)
    wvt = wkv[..., MLA_NOPE_DIM:].reshape(nl, MLA_KV_RANK, MLA_V_ROWS).transpose(0, 2, 1)
    return wq.astype(BF16), wk.astype(BF16), wvt.astype(BF16)


def _out_weight(w_out):
    nl = w_out.shape[0]
    pad_rows = lambda w: jnp.pad(w, ((0, 0), (0, RET_PAIRS * LANES - w.shape[1]), (0, 0)))
    nsa = w_out[:, :NSA_Q_W].reshape(nl, NSA_HEADS, HEAD_DIM, D_MODEL)
    order = [h for j in range(NSA_REP) for h in (j, j + NSA_REP)]
    nsa = nsa[:, order].reshape(nl, NSA_Q_W, D_MODEL)
    mla = pad_rows(w_out[:, NSA_Q_W:NSA_Q_W + MLA_HEADS * MLA_V_DIM])
    ret = pad_rows(w_out[:, NSA_Q_W + MLA_HEADS * MLA_V_DIM:])
    return jnp.concatenate([nsa, mla, ret], axis=1).astype(BF16)


def _retention_tables(gn_gain):
    nh = 2 * RET_PAIRS
    log_g = jnp.log(1.0 - 2.0 ** (-5.0 - jnp.arange(nh, dtype=F32)))
    i = jnp.arange(RET_CHUNK, dtype=F32)
    diff = i[:, None] - i[None, :]
    intra = jnp.where(diff >= 0, jnp.exp(jnp.maximum(diff, 0.0)[None] * log_g[:, None, None]), 0.0)
    read_decay = jnp.exp((i + 1.0)[None, :] * log_g[:, None])
    write_decay = jnp.exp((RET_CHUNK - 1.0 - i)[None, :] * log_g[:, None])
    chunk_decay = jnp.exp(RET_CHUNK * log_g)

    def lanes(t):
        t = t.reshape(RET_PAIRS, 2, -1)
        return jnp.repeat(t.transpose(0, 2, 1), HALF, axis=-1)

    gn = jnp.pad(gn_gain, ((0, 0), (0, nh - RET_HEADS), (0, 0)))
    gn = gn.reshape(gn.shape[0], RET_PAIRS, 1, LANES)
    return (gn, intra.reshape(RET_PAIRS, 2, RET_CHUNK, RET_CHUNK), lanes(read_decay), lanes(write_decay),
            lanes(chunk_decay[:, None]))


def _selection_overlap(seq):
    n_cmp = (seq - CMP_LEN) // CMP_STRIDE + 1
    n_sel = seq // SEL_BLOCK
    cs = np.arange(n_cmp) * CMP_STRIDE
    ss = np.arange(n_sel) * SEL_BLOCK
    ov = np.clip(np.minimum(cs[:, None] + CMP_LEN, ss[None, :] + SEL_BLOCK)
                 - np.maximum(cs[:, None], ss[None, :]), 0, None) / CMP_LEN
    ovl_t = np.zeros((n_sel, seq // CMP_STRIDE), np.float32)
    ovl_t[:, :n_cmp] = ov.T
    return jnp.asarray(ovl_t, BF16)


def kernel(x, positions, ln1_gain, w_in, cmp_pos_k, cmp_w1_k, cmp_w2_k, cmp_pos_v, cmp_w1_v, cmp_w2_v,
           mla_q_norm, mla_w_uq, mla_kv_norm, mla_w_ukv, ret_gn_gain, w_out, ln2_gain, w_up, w_down,
           final_gain):
    batch, seq, _ = x.shape
    depth = w_in.shape[0]
    t = batch * seq

    tab_n = _rope_table(positions, PARTIAL_ROPE_DIM, ROPE_THETA, HEAD_DIM, 0)
    tab_m = _rope_table(positions, MLA_ROPE_DIM, ROPE_THETA, LANES, HALF)
    tab_r = _rope_table(positions, HEAD_DIM, RET_THETA, HEAD_DIM, 0)

    w_in_p, w_in_t = _in_weight(w_in)
    pos_k, wlo_k, whi_k, w2_k = _compress_weights(cmp_pos_k, cmp_w1_k, cmp_w2_k)
    pos_v, wlo_v, whi_v, w2_v = _compress_weights(cmp_pos_v, cmp_w1_v, cmp_w2_v)
    cmp_pos = jnp.stack([pos_k, pos_v], axis=1)
    cmp_wlo = jnp.stack([wlo_k, wlo_v], axis=1)
    cmp_whi = jnp.stack([whi_k, whi_v], axis=1)
    w2_vt = w2_v.transpose(0, 2, 1)
    wq, wk, wvt = _mla_weights(mla_w_uq, mla_w_ukv)
    wo = _out_weight(w_out)
    wu = w_up.astype(BF16)
    wd = w_down.astype(BF16)
    gn, intra, rd, wdec, cd = _retention_tables(ret_gn_gain)
    ovl_t = _selection_overlap(seq)
    gf = final_gain.reshape(1, D_MODEL)

    x2 = x.reshape(t, D_MODEL)
    rows16 = seq // CMP_STRIDE
    for l in range(depth):
        nsa, nsa_t, gate_t, k_cmp, v_cmp, mla_in, ret, rgate = _in_proj(
            x2, ln1_gain[l].reshape(1, D_MODEL), w_in_p[l], w_in_t[l], tab_n, tab_m, tab_r)
        kc, vct = _compress(k_cmp.reshape(batch, rows16, CMP_STRIDE * LANES),
                            v_cmp.reshape(batch, rows16, CMP_STRIDE * LANES),
                            cmp_pos[l], cmp_wlo[l], cmp_whi[l], w2_k[l], w2_vt[l])
        o_nsa = _nsa_attention(nsa, nsa_t, gate_t, kc, vct, ovl_t, batch, seq)
        q_m, k_m, vt_m = _mla_up(mla_in, mla_q_norm[l].reshape(1, -1), mla_kv_norm[l].reshape(1, -1),
                                 wq[l], wk[l], wvt[l], tab_m)
        o_mla = _mla_attention(q_m, k_m, vt_m, batch, seq)
        o_ret = _retention(ret, rgate, gn[l], intra, rd, wdec, cd, batch, seq)
        x2 = _out_mlp(x2, o_nsa, o_mla, o_ret, wo[l], ln2_gain[l].reshape(1, D_MODEL), wu[l], wd[l], gf,
                      final_norm=(l == depth - 1))
    return x2.reshape(batch, seq, D_MODEL)
```

```python
import functools
import math

import numpy as np
import jax
import jax.numpy as jnp
from jax import lax
from jax.experimental import pallas as pl
from jax.experimental.pallas import tpu as pltpu

F32 = jnp.float32
BF16 = jnp.bfloat16

D_MODEL = 1024
HEAD_DIM = 64
NSA_HEADS = 6
NSA_KV_GROUPS = 2
NSA_REP = NSA_HEADS // NSA_KV_GROUPS
N_BRANCH = 3
CMP_LEN = 32
CMP_STRIDE = 16
CMP_HIDDEN = 2 * HEAD_DIM
SEL_BLOCK = 64
SEL_TOP_N = 16
WINDOW = 512
MLA_HEADS = 5
MLA_Q_RANK = 256
MLA_KV_RANK = 128
MLA_NOPE_DIM = 64
MLA_ROPE_DIM = 32
MLA_V_DIM = 64
RET_HEADS = 5
RET_CHUNK = 128
ROPE_THETA = 500000.0
PARTIAL_ROPE_DIM = HEAD_DIM // 4
RET_THETA = 10000.0
D_FF = 4 * D_MODEL
NORM_EPS = 1e-6
NEG_INF = -1e30
FORCE_SCORE = 1e9
LOG2E = math.log2(math.e)

NSA_Q_W = NSA_HEADS * HEAD_DIM
NSA_KV_W = NSA_KV_GROUPS * HEAD_DIM
NSA_GATE_W = NSA_HEADS * N_BRANCH
RET_W = RET_HEADS * HEAD_DIM
IN_SIZES = (NSA_Q_W, NSA_KV_W, NSA_KV_W, NSA_KV_W, NSA_KV_W, NSA_KV_W, NSA_KV_W, NSA_GATE_W,
            MLA_Q_RANK, MLA_KV_RANK, MLA_ROPE_DIM, RET_W, RET_W, RET_W, RET_W)

LANES = 128
HALF = LANES // 2
VMEM_LIMIT = 56 * 1024 * 1024

NSA_TILES = 5
GATE_ROWS = 32
NSA_T_ROWS = 2 * LANES + GATE_ROWS
MLA_IN_TILES = 4
RET_PAIRS = 3
RET_TILES = 3 * RET_PAIRS
IN_TILES = NSA_TILES + 2 + MLA_IN_TILES + RET_TILES + RET_PAIRS
N_PAD = IN_TILES * LANES
MLA_V_ROWS = MLA_HEADS * MLA_V_DIM

IN_TM = 256
MLP_TM = 512
MLA_TM = 512
ATT_T = 512
MLP_FF_CHUNK = 1024


def _nn(a, b):
    return jnp.dot(a, b, preferred_element_type=F32)


def _nt(a, b):
    return lax.dot_general(a, b, (((1,), (1,)), ((), ())), preferred_element_type=F32)


def _rms(x, gain):
    return x * lax.rsqrt(jnp.mean(x * x, axis=-1, keepdims=True) + NORM_EPS) * gain


def _rope(val, tab, half):
    cos = tab[:, 0:LANES]
    sin_a = tab[:, LANES:2 * LANES]
    sin_b = tab[:, 2 * LANES:3 * LANES]
    return (val * cos + pltpu.roll(val, LANES - half, 1) * sin_a
            + pltpu.roll(val, half, 1) * sin_b)


def _lane_lo(shape):
    return lax.broadcasted_iota(jnp.int32, shape, len(shape) - 1) < HALF


def _inproj_kernel(x_ref, g_ref, w_ref, wt_ref, tn_ref, tm_ref, tr_ref,
                   nsa_ref, nsat_ref, gatet_ref, kcmp_ref, vcmp_ref, mla_ref, ret_ref, rgate_ref):
    h = _rms(x_ref[...], g_ref[...]).astype(BF16)
    tab_n = tn_ref[...]
    tab_r = tr_ref[...]

    def tile(a, i):
        return a[:, i * LANES:(i + 1) * LANES]

    c0 = 0
    a = _nn(h, w_ref[:, c0:c0 + NSA_TILES * LANES])
    for i in range(NSA_TILES):
        v = _rope(tile(a, i), tab_n, PARTIAL_ROPE_DIM // 2)
        if i < NSA_REP:
            v = v * LOG2E
        nsa_ref[:, i * LANES:(i + 1) * LANES] = v.astype(BF16)
    c0 += NSA_TILES * LANES
    a = _nn(h, w_ref[:, c0:c0 + 2 * LANES])
    kcmp_ref[...] = _rope(tile(a, 0), tab_n, PARTIAL_ROPE_DIM // 2)
    vcmp_ref[...] = tile(a, 1)
    c0 += 2 * LANES
    a = _nn(h, w_ref[:, c0:c0 + MLA_IN_TILES * LANES])
    mla_ref[:, 0:3 * LANES] = a[:, 0:3 * LANES]
    mla_ref[:, 3 * LANES:4 * LANES] = _rope(tile(a, 3), tm_ref[...], MLA_ROPE_DIM // 2)
    c0 += MLA_IN_TILES * LANES
    a = _nn(h, w_ref[:, c0:c0 + RET_TILES * LANES])
    for i in range(RET_TILES):
        v = tile(a, i)
        if i < 2 * RET_PAIRS:
            v = _rope(v, tab_r, HEAD_DIM // 2)
        ret_ref[:, i * LANES:(i + 1) * LANES] = v.astype(BF16)
    c0 += RET_TILES * LANES
    rgate_ref[...] = _nn(h, w_ref[:, c0:c0 + RET_PAIRS * LANES])
    at = _nt(wt_ref[...], h)
    nsat_ref[...] = at[0:2 * LANES, :].astype(BF16)
    gatet_ref[...] = at[2 * LANES:NSA_T_ROWS, :]


def _in_proj(x2, gain, w, wt, tab_n, tab_m, tab_r):
    t = x2.shape[0]
    row = lambda i: (i, 0)
    col = lambda i: (0, i)
    const = lambda i: (0, 0)
    out_shapes = (
        jax.ShapeDtypeStruct((t, NSA_TILES * LANES), BF16),
        jax.ShapeDtypeStruct((2 * LANES, t), BF16),
        jax.ShapeDtypeStruct((GATE_ROWS, t), F32),
        jax.ShapeDtypeStruct((t, LANES), F32),
        jax.ShapeDtypeStruct((t, LANES), F32),
        jax.ShapeDtypeStruct((t, MLA_IN_TILES * LANES), F32),
        jax.ShapeDtypeStruct((t, RET_TILES * LANES), BF16),
        jax.ShapeDtypeStruct((t, RET_PAIRS * LANES), F32),
    )
    out_specs = tuple(
        pl.BlockSpec((s.shape[0], IN_TM), col) if s.shape[1] == t else pl.BlockSpec((IN_TM, s.shape[1]), row)
        for s in out_shapes)
    return pl.pallas_call(
        _inproj_kernel,
        grid=(t // IN_TM,),
        in_specs=[
            pl.BlockSpec((IN_TM, D_MODEL), row),
            pl.BlockSpec((1, D_MODEL), const),
            pl.BlockSpec((D_MODEL, N_PAD), const),
            pl.BlockSpec((NSA_T_ROWS, D_MODEL), const),
            pl.BlockSpec((IN_TM, 3 * LANES), row),
            pl.BlockSpec((IN_TM, 3 * LANES), row),
            pl.BlockSpec((IN_TM, 3 * LANES), row),
        ],
        out_specs=out_specs,
        out_shape=out_shapes,
        compiler_params=pltpu.CompilerParams(
            dimension_semantics=("parallel",), vmem_limit_bytes=VMEM_LIMIT),
        name="in_proj",
    )(x2, gain, w, wt, tab_n, tab_m, tab_r)


def _compress_kernel(k_ref, v_ref, pos_ref, w1_ref, w2k_ref, w2vt_ref, kc_ref, vct_ref):
    n_blk = k_ref.shape[0] // CMP_STRIDE

    def hidden(src, i):
        lo = jnp.zeros((n_blk, NSA_KV_GROUPS * CMP_HIDDEN), F32)
        hi = jnp.zeros((n_blk, NSA_KV_GROUPS * CMP_HIDDEN), F32)
        for r in range(CMP_STRIDE):
            tok = src[pl.ds(r, n_blk, stride=CMP_STRIDE), :]
            lo = lo + _nn((tok + pos_ref[i, r:r + 1, :]).astype(BF16), w1_ref[i, r])
            r2 = CMP_STRIDE + r
            hi = hi + _nn((tok + pos_ref[i, r2:r2 + 1, :]).astype(BF16), w1_ref[i, r2])
        return jax.nn.gelu(lo + pltpu.roll(hi, n_blk - 1, 0)).astype(BF16)

    kc_ref[0] = _nn(hidden(k_ref, 0), w2k_ref[...]).astype(BF16)
    vct_ref[0] = _nt(w2vt_ref[...], hidden(v_ref, 1)).astype(BF16)


def _compress(k_cmp, v_cmp, pos, w1, w2k, w2vt, batch, seq):
    b = batch
    n_blk = seq // CMP_STRIDE
    return pl.pallas_call(
        _compress_kernel,
        grid=(b,),
        in_specs=[
            pl.BlockSpec((seq, LANES), lambda i: (i, 0)),
            pl.BlockSpec((seq, LANES), lambda i: (i, 0)),
            pl.BlockSpec(pos.shape, lambda i: (0, 0, 0)),
            pl.BlockSpec(w1.shape, lambda i: (0, 0, 0, 0)),
            pl.BlockSpec(w2k.shape, lambda i: (0, 0)),
            pl.BlockSpec(w2vt.shape, lambda i: (0, 0)),
        ],
        out_specs=(pl.BlockSpec((1, n_blk, LANES), lambda i: (i, 0, 0)),
                   pl.BlockSpec((1, LANES, n_blk), lambda i: (i, 0, 0))),
        out_shape=(jax.ShapeDtypeStruct((b, n_blk, LANES), BF16),
                   jax.ShapeDtypeStruct((b, LANES, n_blk), BF16)),
        compiler_params=pltpu.CompilerParams(
            dimension_semantics=("parallel",), vmem_limit_bytes=VMEM_LIMIT),
        name="nsa_compress",
    )(k_cmp, v_cmp, pos, w1, w2k, w2vt)


SCORE_LOOKAHEAD = 2


def _softmax_steps(score_fns, v_ts, states):
    n = len(score_fns)
    s_ts = [score_fns[h]() if h < SCORE_LOOKAHEAD else None for h in range(n)]
    out = []
    for h in range(n):
        if h + SCORE_LOOKAHEAD < n:
            s_ts[h + SCORE_LOOKAHEAD] = score_fns[h + SCORE_LOOKAHEAD]()
        m_old, l_old, acc_old = states[h]
        m = jnp.maximum(m_old, jnp.max(s_ts[h], axis=0, keepdims=True))
        p = jnp.exp2(s_ts[h] - m)
        s_ts[h] = None
        alpha = jnp.exp2(m_old - m)
        l = alpha * l_old + jnp.sum(p, axis=0, keepdims=True)
        out.append((m, l, alpha * acc_old + _nn(v_ts[h], p.astype(BF16))))
    return tuple(out)


def _softmax_init(cols):
    return (jnp.full((1, cols), NEG_INF, F32), jnp.zeros((1, cols), F32),
            jnp.zeros((HEAD_DIM, cols), F32))


def _softmax_finish(l, acc):
    return acc * (1.0 / jnp.maximum(l, 1e-30))


def _nsa_kernel(q_ref, gatet_ref, kc_ref, vct_ref, ks_ref, kw_ref, vst_ref, vwt_ref, ovl_ref,
                o_ref, out_t_ref, selb_ref):
    tq = ATT_T
    qi = pl.program_id(1)
    q0 = pl.multiple_of(qi * tq, tq)
    lo1 = _lane_lo((1, LANES))
    group_lanes = (lo1, jnp.logical_not(lo1))
    gate = jax.nn.sigmoid(gatet_ref[...])
    heads = range(NSA_HEADS)

    def q_head(h):
        j, g = h % NSA_REP, h // NSA_REP
        tile = q_ref[:, j * LANES:(j + 1) * LANES]
        return jnp.where(group_lanes[g], tile, jnp.zeros_like(tile))

    def v_rows(ref, h, k0, n):
        g = h // NSA_REP
        return ref[g * HEAD_DIM:(g + 1) * HEAD_DIM, pl.ds(k0, n)]

    def emit(h, branch, o_t):
        j, g = h % NSA_REP, h // NSA_REP
        r0 = j * LANES + g * HEAD_DIM
        val = gate[h * N_BRANCH + branch:h * N_BRANCH + branch + 1, :] * o_t
        if branch == 0:
            out_t_ref[r0:r0 + HEAD_DIM, :] = val
        else:
            out_t_ref[r0:r0 + HEAD_DIM, :] += val

    qm = [q_head(h) for h in heads]

    n_cmp_pad = kc_ref.shape[1]
    kc = kc_ref[0]
    vct = vct_ref[0]
    n_i = lax.broadcasted_iota(jnp.int32, (n_cmp_pad, tq), 0)
    t_l = q0 + lax.broadcasted_iota(jnp.int32, (n_cmp_pad, tq), 1)
    cmask = (n_i * CMP_STRIDE + (CMP_LEN - 1)) <= t_l
    cmask_f = cmask.astype(F32)
    psum = [None] * NSA_KV_GROUPS
    for h in heads:
        g = h // NSA_REP
        s_t = jnp.where(cmask, _nt(kc, qm[h]), NEG_INF)
        m = jnp.max(s_t, axis=0, keepdims=True)
        p = jnp.exp2(s_t - m) * cmask_f
        p = p * (1.0 / jnp.maximum(jnp.sum(p, axis=0, keepdims=True), 1e-30))
        psum[g] = p if psum[g] is None else psum[g] + p
        emit(h, 0, _nn(vct[g * HEAD_DIM:(g + 1) * HEAD_DIM, :], p.astype(BF16)))

    n_sel = ovl_ref.shape[0]
    m_i = lax.broadcasted_iota(jnp.int32, (n_sel, tq), 0)
    cur = jnp.right_shift(q0 + lax.broadcasted_iota(jnp.int32, (n_sel, tq), 1), SEL_BLOCK.bit_length() - 1)
    valid = m_i <= cur
    forced = (m_i == 0) | (m_i == cur) | (m_i == cur - 1)
    ovl = ovl_ref[...]
    for g in range(NSA_KV_GROUPS):
        p_hi = psum[g].astype(BF16)
        p_lo = (psum[g] - p_hi.astype(F32)).astype(BF16)
        imp = _nn(ovl, p_hi) + _nn(ovl, p_lo)
        imp = jnp.where(valid & forced, FORCE_SCORE, imp)
        imp = jnp.where(valid, imp, NEG_INF)
        rank = jnp.zeros((n_sel, tq), jnp.int32)
        for mp in range(n_sel):
            row = imp[mp:mp + 1, :]
            beats = (row > imp) | ((row == imp) & (m_i > mp))
            rank = rank + beats.astype(jnp.int32)
        selb_ref[g] = jnp.where(rank < SEL_TOP_N, 0.0, NEG_INF)

    key_i = lax.broadcasted_iota(jnp.int32, (tq, tq), 0)
    qry_i = lax.broadcasted_iota(jnp.int32, (tq, tq), 1)
    causal = key_i <= qry_i
    blocks_per_chunk = tq // SEL_BLOCK

    def sel_bias(g, c):
        rows = [jnp.broadcast_to(selb_ref[g, pl.ds(c * blocks_per_chunk + i, 1), :], (SEL_BLOCK, tq))
                for i in range(blocks_per_chunk)]
        return jnp.concatenate(rows, axis=0)

    def sel_chunk(c, k0, states, mask):
        k = ks_ref[pl.ds(k0, tq), :]
        bias = [sel_bias(g, c) for g in range(NSA_KV_GROUPS)]

        def score_fn(h):
            s_t = _nt(k, qm[h]) + bias[h // NSA_REP]
            return s_t if mask is None else jnp.where(mask, s_t, NEG_INF)

        return _softmax_steps([functools.partial(score_fn, h) for h in heads],
                              [v_rows(vst_ref, h, k0, tq) for h in heads], states)

    states = sel_chunk(qi, q0, tuple(_softmax_init(tq) for _ in heads), causal)
    states = lax.fori_loop(
        0, qi, lambda c, st: sel_chunk(c, pl.multiple_of(c * tq, tq), st, None), states)
    for h in heads:
        emit(h, 1, _softmax_finish(states[h][1], states[h][2]))

    def win_chunk(k0, states, mask):
        k = kw_ref[pl.ds(k0, tq), :]
        score_fns = [lambda h=h: jnp.where(mask, _nt(k, qm[h]), NEG_INF) for h in heads]
        return _softmax_steps(score_fns, [v_rows(vwt_ref, h, k0, tq) for h in heads], states)

    states = win_chunk(q0, tuple(_softmax_init(tq) for _ in heads), causal)
    for d in range(1, WINDOW // tq + 1):
        k0 = pl.multiple_of(jnp.maximum(qi - d, 0) * tq, tq)
        in_band = (key_i - d * tq > qry_i - WINDOW) & (qi >= d)
        states = win_chunk(k0, states, in_band)
    for h in heads:
        emit(h, 2, _softmax_finish(states[h][1], states[h][2]))

    for j in range(NSA_REP):
        o_ref[:, j * LANES:(j + 1) * LANES] = out_t_ref[j * LANES:(j + 1) * LANES, :].T.astype(BF16)


def _nsa_attention(nsa, nsa_t, gate_t, kc, vct, ovl_t, batch, seq):
    tq = ATT_T
    nq = seq // tq
    n_blk = kc.shape[1]
    qrow = lambda b, i: (b * nq + i, 0)
    return pl.pallas_call(
        _nsa_kernel,
        grid=(batch, nq),
        in_specs=[
            pl.BlockSpec((tq, NSA_REP * LANES), qrow),
            pl.BlockSpec((GATE_ROWS, tq), lambda b, i: (0, b * nq + i)),
            pl.BlockSpec((1, n_blk, LANES), lambda b, i: (b, 0, 0)),
            pl.BlockSpec((1, LANES, n_blk), lambda b, i: (b, 0, 0)),
            pl.BlockSpec((seq, LANES), lambda b, i: (b, 3)),
            pl.BlockSpec((seq, LANES), lambda b, i: (b, 4)),
            pl.BlockSpec((LANES, seq), lambda b, i: (0, b)),
            pl.BlockSpec((LANES, seq), lambda b, i: (1, b)),
            pl.BlockSpec(ovl_t.shape, lambda b, i: (0, 0)),
        ],
        out_specs=pl.BlockSpec((tq, NSA_REP * LANES), qrow),
        out_shape=jax.ShapeDtypeStruct((batch * seq, NSA_REP * LANES), BF16),
        scratch_shapes=[pltpu.VMEM((NSA_REP * LANES, tq), F32),
                        pltpu.VMEM((NSA_KV_GROUPS, seq // SEL_BLOCK, tq), F32)],
        compiler_params=pltpu.CompilerParams(
            dimension_semantics=("parallel", "arbitrary"), vmem_limit_bytes=VMEM_LIMIT),
        name="nsa_attention",
    )(nsa, gate_t, kc, vct, nsa, nsa, nsa_t, nsa_t, ovl_t)


def _mla_up_kernel(in_ref, qn_ref, kvn_ref, wq_ref, wk_ref, wvt_ref, tab_ref, q_ref, k_ref, vt_ref):
    scale = (MLA_NOPE_DIM + MLA_ROPE_DIM) ** -0.5 * LOG2E
    cq = _rms(in_ref[:, 0:MLA_Q_RANK], qn_ref[...]).astype(BF16)
    ckv = _rms(in_ref[:, MLA_Q_RANK:MLA_Q_RANK + MLA_KV_RANK], kvn_ref[...]).astype(BF16)
    k_pe = in_ref[:, 3 * LANES:4 * LANES]
    tab = tab_ref[...]
    q = _nn(cq, wq_ref[...])
    k = _nn(ckv, wk_ref[...])
    for h in range(MLA_HEADS):
        sl = slice(h * LANES, (h + 1) * LANES)
        q_ref[:, sl] = (_rope(q[:, sl], tab, MLA_ROPE_DIM // 2) * scale).astype(BF16)
        k_ref[:, sl] = (k[:, sl] + k_pe).astype(BF16)
    vt_ref[...] = _nt(wvt_ref[...], ckv).astype(BF16)


def _mla_up(mla_in, q_norm, kv_norm, wq, wk, wvt, tab_m):
    t = mla_in.shape[0]
    row = lambda i: (i, 0)
    const = lambda i: (0, 0)
    return pl.pallas_call(
        _mla_up_kernel,
        grid=(t // MLA_TM,),
        in_specs=[
            pl.BlockSpec((MLA_TM, MLA_IN_TILES * LANES), row),
            pl.BlockSpec((1, MLA_Q_RANK), const),
            pl.BlockSpec((1, MLA_KV_RANK), const),
            pl.BlockSpec(wq.shape, const),
            pl.BlockSpec(wk.shape, const),
            pl.BlockSpec(wvt.shape, const),
            pl.BlockSpec((MLA_TM, 3 * LANES), row),
        ],
        out_specs=(pl.BlockSpec((MLA_TM, MLA_HEADS * LANES), row),
                   pl.BlockSpec((MLA_TM, MLA_HEADS * LANES), row),
                   pl.BlockSpec((MLA_V_ROWS, MLA_TM), lambda i: (0, i))),
        out_shape=(jax.ShapeDtypeStruct((t, MLA_HEADS * LANES), BF16),
                   jax.ShapeDtypeStruct((t, MLA_HEADS * LANES), BF16),
                   jax.ShapeDtypeStruct((MLA_V_ROWS, t), BF16)),
        compiler_params=pltpu.CompilerParams(
            dimension_semantics=("parallel",), vmem_limit_bytes=VMEM_LIMIT),
        name="mla_up",
    )(mla_in, q_norm, kv_norm, wq, wk, wvt, tab_m)


def _mla_attn_kernel(q_ref, k_ref, vt_ref, o_ref, out_t_ref):
    tq = ATT_T
    qi = pl.program_id(1)
    q0 = pl.multiple_of(qi * tq, tq)
    key_i = lax.broadcasted_iota(jnp.int32, (tq, tq), 0)
    qry_i = lax.broadcasted_iota(jnp.int32, (tq, tq), 1)
    causal = key_i <= qry_i
    heads = range(MLA_HEADS)
    qh = [q_ref[:, h * LANES:(h + 1) * LANES] for h in heads]

    def chunk(k0, states, mask):
        def score_fn(h):
            s_t = _nt(k_ref[pl.ds(k0, tq), h * LANES:(h + 1) * LANES], qh[h])
            return s_t if mask is None else jnp.where(mask, s_t, NEG_INF)

        v_ts = [vt_ref[h * MLA_V_DIM:(h + 1) * MLA_V_DIM, pl.ds(k0, tq)] for h in heads]
        return _softmax_steps([functools.partial(score_fn, h) for h in heads], v_ts, states)

    states = chunk(q0, tuple(_softmax_init(tq) for _ in heads), causal)
    states = lax.fori_loop(0, qi, lambda c, st: chunk(pl.multiple_of(c * tq, tq), st, None), states)
    for h in heads:
        out_t_ref[h * MLA_V_DIM:(h + 1) * MLA_V_DIM, :] = _softmax_finish(states[h][1], states[h][2])
    out_t_ref[MLA_V_ROWS:, :] = jnp.zeros((RET_PAIRS * LANES - MLA_V_ROWS, tq), F32)
    for j in range(RET_PAIRS):
        o_ref[:, j * LANES:(j + 1) * LANES] = out_t_ref[j * LANES:(j + 1) * LANES, :].T.astype(BF16)


def _mla_attention(q, k, vt, batch, seq):
    tq = ATT_T
    nq = seq // tq
    return pl.pallas_call(
        _mla_attn_kernel,
        grid=(batch, nq),
        in_specs=[
            pl.BlockSpec((tq, MLA_HEADS * LANES), lambda b, i: (b * nq + i, 0)),
            pl.BlockSpec((seq, MLA_HEADS * LANES), lambda b, i: (b, 0)),
            pl.BlockSpec((MLA_V_ROWS, seq), lambda b, i: (0, b)),
        ],
        out_specs=pl.BlockSpec((tq, RET_PAIRS * LANES), lambda b, i: (b * nq + i, 0)),
        out_shape=jax.ShapeDtypeStruct((batch * seq, RET_PAIRS * LANES), BF16),
        scratch_shapes=[pltpu.VMEM((RET_PAIRS * LANES, tq), F32)],
        compiler_params=pltpu.CompilerParams(
            dimension_semantics=("parallel", "arbitrary"), vmem_limit_bytes=VMEM_LIMIT),
        name="mla_attention",
    )(q, k, vt)


def _tn(a, b):
    return lax.dot_general(a, b, (((0,), (0,)), ((), ())), preferred_element_type=F32)


RET_UNROLL = 4


def _retention_kernel(q_ref, k_ref, v_ref, gate_ref, gn_ref, intra_ref, rd_ref, wd_ref, cd_ref, o_ref,
                      kv_ref, st_ref):
    c_len = RET_CHUNK
    n_chunks = q_ref.shape[0] // c_len
    lo = _lane_lo((1, LANES))
    hi = jnp.logical_not(lo)
    blockdiag = (lax.broadcasted_iota(jnp.int32, (LANES, LANES), 0) < HALF) == _lane_lo((LANES, LANES))
    intra_a = intra_ref[0, 0]
    intra_b = intra_ref[0, 1]
    read_decay = rd_ref[0]
    write_decay = wd_ref[0]
    chunk_decay = cd_ref[0]
    gn = gn_ref[0]

    def half_mean(x):
        s_lo = jnp.sum(jnp.where(lo, x, 0.0), axis=-1, keepdims=True)
        s_hi = jnp.sum(jnp.where(hi, x, 0.0), axis=-1, keepdims=True)
        return jnp.where(lo, s_lo, s_hi) * (1.0 / HEAD_DIM)

    def kv_body(c, carry):
        r0 = pl.multiple_of(c * c_len, c_len)
        kc = k_ref[pl.ds(r0, c_len), :]
        kv_ref[c] = _tn((kc.astype(F32) * write_decay).astype(BF16), v_ref[pl.ds(r0, c_len), :])
        return carry

    lax.fori_loop(0, n_chunks, kv_body, 0, unroll=RET_UNROLL)

    state = jnp.zeros((LANES, LANES), F32)
    for c in range(n_chunks):
        st_ref[c] = jnp.where(blockdiag, state, 0.0).astype(BF16)
        state = state * chunk_decay + kv_ref[c]

    def out_body(c, carry):
        r0 = pl.multiple_of(c * c_len, c_len)
        qc = q_ref[pl.ds(r0, c_len), :]
        kc = k_ref[pl.ds(r0, c_len), :]
        vc = v_ref[pl.ds(r0, c_len), :]
        zero = jnp.zeros_like(qc)
        sa = _nt(jnp.where(lo, qc, zero), kc) * intra_a
        sb = _nt(jnp.where(hi, qc, zero), kc) * intra_b
        o = jnp.where(lo, _nn(sa.astype(BF16), vc), _nn(sb.astype(BF16), vc))
        o = o + _nn(qc, st_ref[c]) * read_decay
        mu = half_mean(o)
        d = o - mu
        y = d * lax.rsqrt(half_mean(d * d) + NORM_EPS) * gn
        o_ref[pl.ds(r0, c_len), :] = (jax.nn.silu(gate_ref[pl.ds(r0, c_len), :]) * y).astype(BF16)
        return carry

    lax.fori_loop(0, n_chunks, out_body, 0, unroll=RET_UNROLL)


def _retention(ret, rgate, gn, intra, rd, wd, cd, batch, seq):
    pair_const3 = lambda b, j: (j, 0, 0)
    return pl.pallas_call(
        _retention_kernel,
        grid=(batch, RET_PAIRS),
        in_specs=[
            pl.BlockSpec((seq, LANES), lambda b, j: (b, j)),
            pl.BlockSpec((seq, LANES), lambda b, j: (b, RET_PAIRS + j)),
            pl.BlockSpec((seq, LANES), lambda b, j: (b, 2 * RET_PAIRS + j)),
            pl.BlockSpec((seq, LANES), lambda b, j: (b, j)),
            pl.BlockSpec((1, 1, LANES), pair_const3),
            pl.BlockSpec((1, 2, RET_CHUNK, RET_CHUNK), lambda b, j: (j, 0, 0, 0)),
            pl.BlockSpec((1, RET_CHUNK, LANES), pair_const3),
            pl.BlockSpec((1, RET_CHUNK, LANES), pair_const3),
            pl.BlockSpec((1, 1, LANES), pair_const3),
        ],
        out_specs=pl.BlockSpec((seq, LANES), lambda b, j: (b, j)),
        out_shape=jax.ShapeDtypeStruct((batch * seq, RET_PAIRS * LANES), BF16),
        scratch_shapes=[pltpu.VMEM((seq // RET_CHUNK, LANES, LANES), F32),
                        pltpu.VMEM((seq // RET_CHUNK, LANES, LANES), BF16)],
        compiler_params=pltpu.CompilerParams(
            dimension_semantics=("parallel", "arbitrary"), vmem_limit_bytes=VMEM_LIMIT),
        name="retention",
    )(ret, ret, ret, rgate, gn, intra, rd, wd, cd)


def _out_mlp_kernel(x_ref, nsa_ref, mla_ref, ret_ref, wo_ref, g2_ref, wu_ref, wd_ref, gf_ref, o_ref,
                    *, final_norm):
    w = RET_PAIRS * LANES
    mixed = (_nn(nsa_ref[...], wo_ref[0:w, :]) + _nn(mla_ref[...], wo_ref[w:2 * w, :])
             + _nn(ret_ref[...], wo_ref[2 * w:3 * w, :]))
    x = x_ref[...] + mixed
    h = _rms(x, g2_ref[...]).astype(BF16)
    y = x
    for c in range(D_FF // MLP_FF_CHUNK):
        sl = slice(c * MLP_FF_CHUNK, (c + 1) * MLP_FF_CHUNK)
        u = jnp.maximum(_nn(h, wu_ref[:, sl]), 0.0)
        y = y + _nn((u * u).astype(BF16), wd_ref[sl, :])
    if final_norm:
        y = _rms(y, gf_ref[...])
    o_ref[...] = y


def _out_mlp(x2, o_nsa, o_mla, o_ret, wo, g2, wu, wd, gf, final_norm):
    t = x2.shape[0]
    w = RET_PAIRS * LANES
    row = lambda i: (i, 0)
    const = lambda i: (0, 0)
    resident = dict(pipeline_mode=pl.Buffered(1))
    return pl.pallas_call(
        functools.partial(_out_mlp_kernel, final_norm=final_norm),
        grid=(t // MLP_TM,),
        in_specs=[
            pl.BlockSpec((MLP_TM, D_MODEL), row),
            pl.BlockSpec((MLP_TM, w), row),
            pl.BlockSpec((MLP_TM, w), row),
            pl.BlockSpec((MLP_TM, w), row),
            pl.BlockSpec(wo.shape, const, **resident),
            pl.BlockSpec((1, D_MODEL), const),
            pl.BlockSpec(wu.shape, const, **resident),
            pl.BlockSpec(wd.shape, const, **resident),
            pl.BlockSpec((1, D_MODEL), const),
        ],
        out_specs=pl.BlockSpec((MLP_TM, D_MODEL), row),
        out_shape=jax.ShapeDtypeStruct((t, D_MODEL), F32),
        compiler_params=pltpu.CompilerParams(
            dimension_semantics=("parallel",), vmem_limit_bytes=VMEM_LIMIT),
        name="out_mlp",
    )(x2, o_nsa, o_mla, o_ret, wo, g2, wu, wd, gf)


def _rope_table(positions, dim, theta, period, base):
    half = dim // 2
    inv = 1.0 / (theta ** (jnp.arange(0, dim, 2, dtype=F32) / dim))
    ang = positions.reshape(-1).astype(F32)[:, None] * inv
    cos, sin = jnp.cos(ang), jnp.sin(ang)
    zero = jnp.zeros_like(sin)

    def lanes(first, second, fill):
        unit = jnp.pad(jnp.concatenate([first, second], axis=1), ((0, 0), (base, period - base - dim)),
                       constant_values=fill)
        return jnp.tile(unit, (1, LANES // period))

    return jnp.concatenate([lanes(cos, cos, 1.0), lanes(-sin, zero, 0.0), lanes(zero, sin, 0.0)], axis=1)


def _pad_cols(w, n):
    return jnp.pad(w, ((0, 0), (0, 0), (0, n - w.shape[-1])))


def _in_weight(w_in):
    offs = np.cumsum((0,) + IN_SIZES)
    seg = [w_in[:, :, offs[i]:offs[i + 1]] for i in range(len(IN_SIZES))]
    (nsa_q, k_cmp, v_cmp, k_slc, v_slc, k_win, v_win, gate,
     cq, ckv, kpe, ret_q, ret_k, ret_v, ret_g) = seg
    scale = HEAD_DIM ** -0.5
    qh = [nsa_q[:, :, h * HEAD_DIM:(h + 1) * HEAD_DIM] * scale for h in range(NSA_HEADS)]
    tiles = [jnp.concatenate([qh[j], qh[j + NSA_REP]], axis=-1) for j in range(NSA_REP)]
    tiles += [k_slc, k_win, k_cmp, v_cmp]
    tiles += [cq, ckv, jnp.pad(kpe, ((0, 0), (0, 0), (HALF, LANES - HALF - MLA_ROPE_DIM)))]
    ret_w = RET_PAIRS * LANES
    tiles += [_pad_cols(ret_q, ret_w), _pad_cols(ret_k * scale, ret_w), _pad_cols(ret_v, ret_w),
              _pad_cols(ret_g, ret_w)]
    w_t = jnp.concatenate([v_slc, v_win, _pad_cols(gate, GATE_ROWS)], axis=-1).transpose(0, 2, 1)
    return jnp.concatenate(tiles, axis=-1).astype(BF16), w_t.astype(BF16)


def _compress_weights(pos, w1, w2):
    nl = pos.shape[0]
    g, dh, hid = NSA_KV_GROUPS, HEAD_DIM, CMP_HIDDEN
    p = jnp.tile(pos, (1, 1, g))
    eye = jnp.eye(g, dtype=w1.dtype)
    w = w1.reshape(nl, CMP_LEN, dh, hid)
    w = jnp.einsum('ltdc,gk->ltgdkc', w, eye).reshape(nl, CMP_LEN, g * dh, g * hid)
    w2b = jnp.einsum('lcd,gk->lgckd', w2, eye).reshape(nl, g * hid, g * dh)
    return p, w.astype(BF16), w2b.astype(BF16)


def _mla_weights(w_uq, w_ukv):
    nl = w_uq.shape[0]
    dq = MLA_NOPE_DIM + MLA_ROPE_DIM
    wq = w_uq.reshape(nl, MLA_Q_RANK, MLA_HEADS, dq)
    wq = jnp.pad(wq, ((0, 0), (0, 0), (0, 0), (0, LANES - dq))).reshape(nl, MLA_Q_RANK, MLA_HEADS * LANES)
    wkv = w_ukv.reshape(nl, MLA_KV_RANK, MLA_HEADS, MLA_NOPE_DIM + MLA_V_DIM)
    wk = jnp.pad(wkv[..., :MLA_NOPE_DIM], ((0, 0), (0, 0), (0, 0), (0, LANES - MLA_NOPE_DIM)))
    wk = wk.reshape(nl, MLA_KV_RANK, MLA_HEADS * LANES)
    wvt = wkv[..., MLA_NOPE_DIM:].reshape(nl, MLA_KV_RANK, MLA_V_ROWS).transpose(0, 2, 1)
    return wq.astype(BF16), wk.astype(BF16), wvt.astype(BF16)


def _out_weight(w_out):
    nl = w_out.shape[0]
    pad_rows = lambda w: jnp.pad(w, ((0, 0), (0, RET_PAIRS * LANES - w.shape[1]), (0, 0)))
    nsa = w_out[:, :NSA_Q_W].reshape(nl, NSA_HEADS, HEAD_DIM, D_MODEL)
    order = [h for j in range(NSA_REP) for h in (j, j + NSA_REP)]
    nsa = nsa[:, order].reshape(nl, NSA_Q_W, D_MODEL)
    mla = pad_rows(w_out[:, NSA_Q_W:NSA_Q_W + MLA_HEADS * MLA_V_DIM])
    ret = pad_rows(w_out[:, NSA_Q_W + MLA_HEADS * MLA_V_DIM:])
    return jnp.concatenate([nsa, mla, ret], axis=1).astype(BF16)


def _retention_tables(gn_gain):
    nh = 2 * RET_PAIRS
    log_g = jnp.log(1.0 - 2.0 ** (-5.0 - jnp.arange(nh, dtype=F32)))
    i = jnp.arange(RET_CHUNK, dtype=F32)
    diff = i[:, None] - i[None, :]
    intra = jnp.where(diff >= 0, jnp.exp(jnp.maximum(diff, 0.0)[None] * log_g[:, None, None]), 0.0)
    read_decay = jnp.exp((i + 1.0)[None, :] * log_g[:, None])
    write_decay = jnp.exp((RET_CHUNK - 1.0 - i)[None, :] * log_g[:, None])
    chunk_decay = jnp.exp(RET_CHUNK * log_g)

    def lanes(t):
        t = t.reshape(RET_PAIRS, 2, -1)
        return jnp.repeat(t.transpose(0, 2, 1), HALF, axis=-1)

    gn = jnp.pad(gn_gain, ((0, 0), (0, nh - RET_HEADS), (0, 0)))
    gn = gn.reshape(gn.shape[0], RET_PAIRS, 1, LANES)
    return (gn, intra.reshape(RET_PAIRS, 2, RET_CHUNK, RET_CHUNK), lanes(read_decay), lanes(write_decay),
            lanes(chunk_decay[:, None]))


def _selection_overlap(seq):
    n_cmp = (seq - CMP_LEN) // CMP_STRIDE + 1
    n_sel = seq // SEL_BLOCK
    cs = np.arange(n_cmp) * CMP_STRIDE
    ss = np.arange(n_sel) * SEL_BLOCK
    ov = np.clip(np.minimum(cs[:, None] + CMP_LEN, ss[None, :] + SEL_BLOCK)
                 - np.maximum(cs[:, None], ss[None, :]), 0, None) / CMP_LEN
    ovl_t = np.zeros((n_sel, seq // CMP_STRIDE), np.float32)
    ovl_t[:, :n_cmp] = ov.T
    return jnp.asarray(ovl_t, BF16)


def kernel(x, positions, ln1_gain, w_in, cmp_pos_k, cmp_w1_k, cmp_w2_k, cmp_pos_v, cmp_w1_v, cmp_w2_v,
           mla_q_norm, mla_w_uq, mla_kv_norm, mla_w_ukv, ret_gn_gain, w_out, ln2_gain, w_up, w_down,
           final_gain):
    batch, seq, _ = x.shape
    depth = w_in.shape[0]
    t = batch * seq

    tab_n = _rope_table(positions, PARTIAL_ROPE_DIM, ROPE_THETA, HEAD_DIM, 0)
    tab_m = _rope_table(positions, MLA_ROPE_DIM, ROPE_THETA, LANES, HALF)
    tab_r = _rope_table(positions, HEAD_DIM, RET_THETA, HEAD_DIM, 0)

    w_in_p, w_in_t = _in_weight(w_in)
    pos_k, w1_k, w2_k = _compress_weights(cmp_pos_k, cmp_w1_k, cmp_w2_k)
    pos_v, w1_v, w2_v = _compress_weights(cmp_pos_v, cmp_w1_v, cmp_w2_v)
    cmp_pos = jnp.stack([pos_k, pos_v], axis=1)
    cmp_w1 = jnp.stack([w1_k, w1_v], axis=1)
    w2_vt = w2_v.transpose(0, 2, 1)
    wq, wk, wvt = _mla_weights(mla_w_uq, mla_w_ukv)
    wo = _out_weight(w_out)
    wu = w_up.astype(BF16)
    wd = w_down.astype(BF16)
    gn, intra, rd, wdec, cd = _retention_tables(ret_gn_gain)
    ovl_t = _selection_overlap(seq)
    gf = final_gain.reshape(1, D_MODEL)

    x2 = x.reshape(t, D_MODEL)
    for l in range(depth):
        nsa, nsa_t, gate_t, k_cmp, v_cmp, mla_in, ret, rgate = _in_proj(
            x2, ln1_gain[l].reshape(1, D_MODEL), w_in_p[l], w_in_t[l], tab_n, tab_m, tab_r)
        kc, vct = _compress(k_cmp, v_cmp, cmp_pos[l], cmp_w1[l], w2_k[l], w2_vt[l], batch, seq)
        o_nsa = _nsa_attention(nsa, nsa_t, gate_t, kc, vct, ovl_t, batch, seq)
        q_m, k_m, vt_m = _mla_up(mla_in, mla_q_norm[l].reshape(1, -1), mla_kv_norm[l].reshape(1, -1),
                                 wq[l], wk[l], wvt[l], tab_m)
        o_mla = _mla_attention(q_m, k_m, vt_m, batch, seq)
        o_ret = _retention(ret, rgate, gn[l], intra, rd, wdec, cd, batch, seq)
        x2 = _out_mlp(x2, o_nsa, o_mla, o_ret, wo[l], ln2_gain[l].reshape(1, D_MODEL), wu[l], wd[l], gf,
                      final_norm=(l == depth - 1))
    return x2.reshape(batch, seq, D_MODEL)
```

```python
import functools
import math

import numpy as np
import jax
import jax.numpy as jnp
from jax import lax
from jax.experimental import pallas as pl
from jax.experimental.pallas import tpu as pltpu

F32 = jnp.float32
BF16 = jnp.bfloat16

D_MODEL = 1024
HEAD_DIM = 64
NSA_HEADS = 6
NSA_KV_GROUPS = 2
NSA_REP = NSA_HEADS // NSA_KV_GROUPS
N_BRANCH = 3
CMP_LEN = 32
CMP_STRIDE = 16
CMP_HIDDEN = 2 * HEAD_DIM
SEL_BLOCK = 64
SEL_TOP_N = 16
WINDOW = 512
MLA_HEADS = 5
MLA_Q_RANK = 256
MLA_KV_RANK = 128
MLA_NOPE_DIM = 64
MLA_ROPE_DIM = 32
MLA_V_DIM = 64
RET_HEADS = 5
RET_CHUNK = 128
ROPE_THETA = 500000.0
PARTIAL_ROPE_DIM = HEAD_DIM // 4
RET_THETA = 10000.0
D_FF = 4 * D_MODEL
NORM_EPS = 1e-6
NEG_INF = -1e30
FORCE_SCORE = 1e9
LOG2E = math.log2(math.e)

NSA_Q_W = NSA_HEADS * HEAD_DIM
NSA_KV_W = NSA_KV_GROUPS * HEAD_DIM
NSA_GATE_W = NSA_HEADS * N_BRANCH
RET_W = RET_HEADS * HEAD_DIM
IN_SIZES = (NSA_Q_W, NSA_KV_W, NSA_KV_W, NSA_KV_W, NSA_KV_W, NSA_KV_W, NSA_KV_W, NSA_GATE_W,
            MLA_Q_RANK, MLA_KV_RANK, MLA_ROPE_DIM, RET_W, RET_W, RET_W, RET_W)

LANES = 128
HALF = LANES // 2
VMEM_LIMIT = 56 * 1024 * 1024

NSA_TILES = 5
GATE_ROWS = 32
NSA_T_ROWS = 2 * LANES + GATE_ROWS
MLA_IN_TILES = 4
RET_PAIRS = 3
RET_TILES = 3 * RET_PAIRS
IN_TILES = NSA_TILES + 2 + MLA_IN_TILES + RET_TILES + RET_PAIRS
N_PAD = IN_TILES * LANES
MLA_V_ROWS = MLA_HEADS * MLA_V_DIM

IN_TM = 512
MLP_TM = 512
MLA_TM = 512
ATT_T = 512
MLP_FF_CHUNK = 1024


def _nn(a, b):
    return jnp.dot(a, b, preferred_element_type=F32)


def _nt(a, b):
    return lax.dot_general(a, b, (((1,), (1,)), ((), ())), preferred_element_type=F32)


def _rms(x, gain):
    return x * lax.rsqrt(jnp.mean(x * x, axis=-1, keepdims=True) + NORM_EPS) * gain


def _rope(val, tab, half):
    cos = tab[:, 0:LANES]
    sin_a = tab[:, LANES:2 * LANES]
    sin_b = tab[:, 2 * LANES:3 * LANES]
    return (val * cos + pltpu.roll(val, LANES - half, 1) * sin_a
            + pltpu.roll(val, half, 1) * sin_b)


def _lane_lo(shape):
    return lax.broadcasted_iota(jnp.int32, shape, len(shape) - 1) < HALF


def _inproj_kernel(x_ref, g_ref, w_ref, wt_ref, tn_ref, tm_ref, tr_ref,
                   nsa_ref, nsat_ref, gatet_ref, kcmp_ref, vcmp_ref, mla_ref, ret_ref, rgate_ref):
    h = _rms(x_ref[...], g_ref[...]).astype(BF16)
    tab_n = tn_ref[...]
    tab_r = tr_ref[...]

    def tile(a, i):
        return a[:, i * LANES:(i + 1) * LANES]

    c0 = 0
    a = _nn(h, w_ref[:, c0:c0 + NSA_TILES * LANES])
    for i in range(NSA_TILES):
        v = _rope(tile(a, i), tab_n, PARTIAL_ROPE_DIM // 2)
        if i < NSA_REP:
            v = v * LOG2E
        nsa_ref[:, i * LANES:(i + 1) * LANES] = v.astype(BF16)
    c0 += NSA_TILES * LANES
    a = _nn(h, w_ref[:, c0:c0 + 2 * LANES])
    kcmp_ref[...] = _rope(tile(a, 0), tab_n, PARTIAL_ROPE_DIM // 2)
    vcmp_ref[...] = tile(a, 1)
    c0 += 2 * LANES
    a = _nn(h, w_ref[:, c0:c0 + MLA_IN_TILES * LANES])
    mla_ref[:, 0:3 * LANES] = a[:, 0:3 * LANES]
    mla_ref[:, 3 * LANES:4 * LANES] = _rope(tile(a, 3), tm_ref[...], MLA_ROPE_DIM // 2)
    c0 += MLA_IN_TILES * LANES
    a = _nn(h, w_ref[:, c0:c0 + RET_TILES * LANES])
    for i in range(RET_TILES):
        v = tile(a, i)
        if i < 2 * RET_PAIRS:
            v = _rope(v, tab_r, HEAD_DIM // 2)
        ret_ref[:, i * LANES:(i + 1) * LANES] = v.astype(BF16)
    c0 += RET_TILES * LANES
    rgate_ref[...] = _nn(h, w_ref[:, c0:c0 + RET_PAIRS * LANES])
    at = _nt(wt_ref[...], h)
    nsat_ref[...] = at[0:2 * LANES, :].astype(BF16)
    gatet_ref[...] = at[2 * LANES:NSA_T_ROWS, :]


def _layer_spec(w, layer, **kwargs):
    zeros = (0,) * (w.ndim - 1)
    return pl.BlockSpec((None,) + w.shape[1:], lambda *_: (layer,) + zeros, **kwargs)


def _in_proj(x2, gain, w, wt, layer, tab_n, tab_m, tab_r):
    t = x2.shape[0]
    row = lambda i: (i, 0)
    col = lambda i: (0, i)
    const = lambda i: (0, 0)
    out_shapes = (
        jax.ShapeDtypeStruct((t, NSA_TILES * LANES), BF16),
        jax.ShapeDtypeStruct((2 * LANES, t), BF16),
        jax.ShapeDtypeStruct((GATE_ROWS, t), F32),
        jax.ShapeDtypeStruct((t, LANES), F32),
        jax.ShapeDtypeStruct((t, LANES), F32),
        jax.ShapeDtypeStruct((t, MLA_IN_TILES * LANES), F32),
        jax.ShapeDtypeStruct((t, RET_TILES * LANES), BF16),
        jax.ShapeDtypeStruct((t, RET_PAIRS * LANES), F32),
    )
    out_specs = tuple(
        pl.BlockSpec((s.shape[0], IN_TM), col) if s.shape[1] == t else pl.BlockSpec((IN_TM, s.shape[1]), row)
        for s in out_shapes)
    return pl.pallas_call(
        _inproj_kernel,
        grid=(t // IN_TM,),
        in_specs=[
            pl.BlockSpec((IN_TM, D_MODEL), row),
            pl.BlockSpec((1, D_MODEL), const),
            _layer_spec(w, layer),
            _layer_spec(wt, layer),
            pl.BlockSpec((IN_TM, 3 * LANES), row),
            pl.BlockSpec((IN_TM, 3 * LANES), row),
            pl.BlockSpec((IN_TM, 3 * LANES), row),
        ],
        out_specs=out_specs,
        out_shape=out_shapes,
        compiler_params=pltpu.CompilerParams(
            dimension_semantics=("parallel",), vmem_limit_bytes=VMEM_LIMIT),
        name="in_proj",
    )(x2, gain, w, wt, tab_n, tab_m, tab_r)


def _compress_kernel(k_ref, v_ref, pos_ref, w1_ref, w2k_ref, w2vt_ref, kc_ref, vct_ref):
    n_blk = k_ref.shape[0] // CMP_STRIDE

    def hidden(src, i):
        lo = jnp.zeros((n_blk, NSA_KV_GROUPS * CMP_HIDDEN), F32)
        hi = jnp.zeros((n_blk, NSA_KV_GROUPS * CMP_HIDDEN), F32)
        for r in range(CMP_STRIDE):
            tok = src[pl.ds(r, n_blk, stride=CMP_STRIDE), :]
            lo = lo + _nn((tok + pos_ref[i, r:r + 1, :]).astype(BF16), w1_ref[i, r])
            r2 = CMP_STRIDE + r
            hi = hi + _nn((tok + pos_ref[i, r2:r2 + 1, :]).astype(BF16), w1_ref[i, r2])
        return jax.nn.gelu(lo + pltpu.roll(hi, n_blk - 1, 0)).astype(BF16)

    kc_ref[0] = _nn(hidden(k_ref, 0), w2k_ref[...]).astype(BF16)
    vct_ref[0] = _nt(w2vt_ref[...], hidden(v_ref, 1)).astype(BF16)


def _compress(k_cmp, v_cmp, pos, w1, w2k, w2vt, layer, batch, seq):
    b = batch
    n_blk = seq // CMP_STRIDE
    return pl.pallas_call(
        _compress_kernel,
        grid=(b,),
        in_specs=[
            pl.BlockSpec((seq, LANES), lambda i: (i, 0)),
            pl.BlockSpec((seq, LANES), lambda i: (i, 0)),
            pl.BlockSpec(pos.shape, lambda i: (0, 0, 0)),
            _layer_spec(w1, layer),
            pl.BlockSpec(w2k.shape, lambda i: (0, 0)),
            pl.BlockSpec(w2vt.shape, lambda i: (0, 0)),
        ],
        out_specs=(pl.BlockSpec((1, n_blk, LANES), lambda i: (i, 0, 0)),
                   pl.BlockSpec((1, LANES, n_blk), lambda i: (i, 0, 0))),
        out_shape=(jax.ShapeDtypeStruct((b, n_blk, LANES), BF16),
                   jax.ShapeDtypeStruct((b, LANES, n_blk), BF16)),
        compiler_params=pltpu.CompilerParams(
            dimension_semantics=("parallel",), vmem_limit_bytes=VMEM_LIMIT),
        name="nsa_compress",
    )(k_cmp, v_cmp, pos, w1, w2k, w2vt)


SCORE_LOOKAHEAD = 2


def _softmax_steps(score_fns, v_ts, states):
    n = len(score_fns)
    s_ts = [score_fns[h]() if h < SCORE_LOOKAHEAD else None for h in range(n)]
    out = []
    for h in range(n):
        if h + SCORE_LOOKAHEAD < n:
            s_ts[h + SCORE_LOOKAHEAD] = score_fns[h + SCORE_LOOKAHEAD]()
        m_old, l_old, acc_old = states[h]
        m = jnp.maximum(m_old, jnp.max(s_ts[h], axis=0, keepdims=True))
        p = jnp.exp2(s_ts[h] - m)
        s_ts[h] = None
        alpha = jnp.exp2(m_old - m)
        l = alpha * l_old + jnp.sum(p, axis=0, keepdims=True)
        out.append((m, l, alpha * acc_old + _nn(v_ts[h], p.astype(BF16))))
    return tuple(out)


def _softmax_init(cols):
    return (jnp.full((1, cols), NEG_INF, F32), jnp.zeros((1, cols), F32),
            jnp.zeros((HEAD_DIM, cols), F32))


def _softmax_finish(l, acc):
    return acc * (1.0 / jnp.maximum(l, 1e-30))


def _nsa_kernel(q_ref, gatet_ref, kc_ref, vct_ref, ks_ref, kw_ref, vst_ref, vwt_ref, ovl_ref,
                o_ref, out_t_ref, selb_ref):
    tq = ATT_T
    qi = pl.program_id(1)
    q0 = pl.multiple_of(qi * tq, tq)
    lo1 = _lane_lo((1, LANES))
    group_lanes = (lo1, jnp.logical_not(lo1))
    gate = jax.nn.sigmoid(gatet_ref[...])
    heads = range(NSA_HEADS)

    def q_head(h):
        j, g = h % NSA_REP, h // NSA_REP
        tile = q_ref[:, j * LANES:(j + 1) * LANES]
        return jnp.where(group_lanes[g], tile, jnp.zeros_like(tile))

    def v_rows(ref, h, k0, n):
        g = h // NSA_REP
        return ref[g * HEAD_DIM:(g + 1) * HEAD_DIM, pl.ds(k0, n)]

    def emit(h, branch, qs, nq, o_t):
        j, g = h % NSA_REP, h // NSA_REP
        r0 = j * LANES + g * HEAD_DIM
        row = h * N_BRANCH + branch
        val = gate[row:row + 1, qs:qs + nq] * o_t
        if branch == 0:
            out_t_ref[r0:r0 + HEAD_DIM, qs:qs + nq] = val
        else:
            out_t_ref[r0:r0 + HEAD_DIM, qs:qs + nq] += val

    qm = [q_head(h) for h in heads]

    n_cmp_pad = kc_ref.shape[1]
    kc = kc_ref[0]
    vct = vct_ref[0]
    n_i = lax.broadcasted_iota(jnp.int32, (n_cmp_pad, tq), 0)
    t_l = q0 + lax.broadcasted_iota(jnp.int32, (n_cmp_pad, tq), 1)
    cmask = (n_i * CMP_STRIDE + (CMP_LEN - 1)) <= t_l
    cmask_f = cmask.astype(F32)
    psum = [None] * NSA_KV_GROUPS
    for h in heads:
        g = h // NSA_REP
        s_t = jnp.where(cmask, _nt(kc, qm[h]), NEG_INF)
        m = jnp.max(s_t, axis=0, keepdims=True)
        p = jnp.exp2(s_t - m) * cmask_f
        p = p * (1.0 / jnp.maximum(jnp.sum(p, axis=0, keepdims=True), 1e-30))
        psum[g] = p if psum[g] is None else psum[g] + p
        emit(h, 0, 0, tq, _nn(vct[g * HEAD_DIM:(g + 1) * HEAD_DIM, :], p.astype(BF16)))

    n_sel = ovl_ref.shape[0]
    m_i = lax.broadcasted_iota(jnp.int32, (n_sel, tq), 0)
    cur = jnp.right_shift(q0 + lax.broadcasted_iota(jnp.int32, (n_sel, tq), 1), SEL_BLOCK.bit_length() - 1)
    valid = m_i <= cur
    forced = (m_i == 0) | (m_i == cur) | (m_i == cur - 1)
    ovl = ovl_ref[...]
    for g in range(NSA_KV_GROUPS):
        p_hi = psum[g].astype(BF16)
        p_lo = (psum[g] - p_hi.astype(F32)).astype(BF16)
        imp = _nn(ovl, p_hi) + _nn(ovl, p_lo)
        imp = jnp.where(valid & forced, FORCE_SCORE, imp)
        imp = jnp.where(valid, imp, NEG_INF)
        rank = jnp.zeros((n_sel, tq), jnp.int32)
        for mp in range(n_sel):
            row = imp[mp:mp + 1, :]
            beats = (row > imp) | ((row == imp) & (m_i > mp))
            rank = rank + beats.astype(jnp.int32)
        selb_ref[g] = jnp.where(rank < SEL_TOP_N, 0.0, NEG_INF)

    blocks_per_chunk = tq // SEL_BLOCK

    def sel_bias(g, c, n_keys, qs, nq):
        rows = [jnp.broadcast_to(selb_ref[g, pl.ds(c * blocks_per_chunk + i, 1), qs:qs + nq], (SEL_BLOCK, nq))
                for i in range(n_keys // SEL_BLOCK)]
        return jnp.concatenate(rows, axis=0)

    def sel_chunk(c, k0, n_keys, qs, nq, states, causal):
        k = ks_ref[pl.ds(k0, n_keys), :]
        bias = [sel_bias(g, c, n_keys, qs, nq) for g in range(NSA_KV_GROUPS)]
        if causal:
            mask = (lax.broadcasted_iota(jnp.int32, (n_keys, nq), 0)
                    <= qs + lax.broadcasted_iota(jnp.int32, (n_keys, nq), 1))

        def score_fn(h):
            s_t = _nt(k, qm[h][qs:qs + nq]) + bias[h // NSA_REP]
            return jnp.where(mask, s_t, NEG_INF) if causal else s_t

        return _softmax_steps([functools.partial(score_fn, h) for h in heads],
                              [v_rows(vst_ref, h, k0, n_keys) for h in heads], states)

    states = sel_chunk(qi, q0, tq, 0, tq, tuple(_softmax_init(tq) for _ in heads), True)
    states = lax.fori_loop(
        0, qi, lambda c, st: sel_chunk(c, pl.multiple_of(c * tq, tq), tq, 0, tq, st, False), states)
    for h in heads:
        emit(h, 1, 0, tq, _softmax_finish(states[h][1], states[h][2]))

    key_i = lax.broadcasted_iota(jnp.int32, (tq, tq), 0)
    qry_i = lax.broadcasted_iota(jnp.int32, (tq, tq), 1)

    def win_chunk(k0, states, mask):
        k = kw_ref[pl.ds(k0, tq), :]
        score_fns = [lambda h=h: jnp.where(mask, _nt(k, qm[h]), NEG_INF) for h in heads]
        return _softmax_steps(score_fns, [v_rows(vwt_ref, h, k0, tq) for h in heads], states)

    states = win_chunk(q0, tuple(_softmax_init(tq) for _ in heads), key_i <= qry_i)
    for d in range(1, WINDOW // tq + 1):
        k0 = pl.multiple_of(jnp.maximum(qi - d, 0) * tq, tq)
        in_band = (key_i - d * tq > qry_i - WINDOW) & (qi >= d)
        states = win_chunk(k0, states, in_band)
    for h in heads:
        emit(h, 2, 0, tq, _softmax_finish(states[h][1], states[h][2]))

    for j in range(NSA_REP):
        o_ref[:, j * LANES:(j + 1) * LANES] = out_t_ref[j * LANES:(j + 1) * LANES, :].T.astype(BF16)


def _nsa_attention(nsa, nsa_t, gate_t, kc, vct, ovl_t, batch, seq):
    tq = ATT_T
    nq = seq // tq
    n_blk = kc.shape[1]
    qrow = lambda b, i: (b * nq + i, 0)
    return pl.pallas_call(
        _nsa_kernel,
        grid=(batch, nq),
        in_specs=[
            pl.BlockSpec((tq, NSA_REP * LANES), qrow),
            pl.BlockSpec((GATE_ROWS, tq), lambda b, i: (0, b * nq + i)),
            pl.BlockSpec((1, n_blk, LANES), lambda b, i: (b, 0, 0)),
            pl.BlockSpec((1, LANES, n_blk), lambda b, i: (b, 0, 0)),
            pl.BlockSpec((seq, LANES), lambda b, i: (b, 3)),
            pl.BlockSpec((seq, LANES), lambda b, i: (b, 4)),
            pl.BlockSpec((LANES, seq), lambda b, i: (0, b)),
            pl.BlockSpec((LANES, seq), lambda b, i: (1, b)),
            pl.BlockSpec(ovl_t.shape, lambda b, i: (0, 0)),
        ],
        out_specs=pl.BlockSpec((tq, NSA_REP * LANES), qrow),
        out_shape=jax.ShapeDtypeStruct((batch * seq, NSA_REP * LANES), BF16),
        scratch_shapes=[pltpu.VMEM((NSA_REP * LANES, tq), F32),
                        pltpu.VMEM((NSA_KV_GROUPS, seq // SEL_BLOCK, tq), F32)],
        compiler_params=pltpu.CompilerParams(
            dimension_semantics=("parallel", "arbitrary"), vmem_limit_bytes=VMEM_LIMIT),
        name="nsa_attention",
    )(nsa, gate_t, kc, vct, nsa, nsa, nsa_t, nsa_t, ovl_t)


def _mla_up_kernel(in_ref, qn_ref, kvn_ref, wq_ref, wk_ref, wvt_ref, tab_ref, q_ref, k_ref, vt_ref):
    scale = (MLA_NOPE_DIM + MLA_ROPE_DIM) ** -0.5 * LOG2E
    cq = _rms(in_ref[:, 0:MLA_Q_RANK], qn_ref[...]).astype(BF16)
    ckv = _rms(in_ref[:, MLA_Q_RANK:MLA_Q_RANK + MLA_KV_RANK], kvn_ref[...]).astype(BF16)
    k_pe = in_ref[:, 3 * LANES:4 * LANES]
    tab = tab_ref[...]
    q = _nn(cq, wq_ref[...])
    k = _nn(ckv, wk_ref[...])
    for h in range(MLA_HEADS):
        sl = slice(h * LANES, (h + 1) * LANES)
        q_ref[:, sl] = (_rope(q[:, sl], tab, MLA_ROPE_DIM // 2) * scale).astype(BF16)
        k_ref[:, sl] = (k[:, sl] + k_pe).astype(BF16)
    vt_ref[...] = _nt(wvt_ref[...], ckv).astype(BF16)


def _mla_up(mla_in, q_norm, kv_norm, wq, wk, wvt, layer, tab_m):
    t = mla_in.shape[0]
    row = lambda i: (i, 0)
    const = lambda i: (0, 0)
    return pl.pallas_call(
        _mla_up_kernel,
        grid=(t // MLA_TM,),
        in_specs=[
            pl.BlockSpec((MLA_TM, MLA_IN_TILES * LANES), row),
            pl.BlockSpec((1, MLA_Q_RANK), const),
            pl.BlockSpec((1, MLA_KV_RANK), const),
            _layer_spec(wq, layer),
            _layer_spec(wk, layer),
            _layer_spec(wvt, layer),
            pl.BlockSpec((MLA_TM, 3 * LANES), row),
        ],
        out_specs=(pl.BlockSpec((MLA_TM, MLA_HEADS * LANES), row),
                   pl.BlockSpec((MLA_TM, MLA_HEADS * LANES), row),
                   pl.BlockSpec((MLA_V_ROWS, MLA_TM), lambda i: (0, i))),
        out_shape=(jax.ShapeDtypeStruct((t, MLA_HEADS * LANES), BF16),
                   jax.ShapeDtypeStruct((t, MLA_HEADS * LANES), BF16),
                   jax.ShapeDtypeStruct((MLA_V_ROWS, t), BF16)),
        compiler_params=pltpu.CompilerParams(
            dimension_semantics=("parallel",), vmem_limit_bytes=VMEM_LIMIT),
        name="mla_up",
    )(mla_in, q_norm, kv_norm, wq, wk, wvt, tab_m)


def _mla_attn_kernel(q_ref, k_ref, vt_ref, o_ref, out_t_ref):
    tq = ATT_T
    qi = pl.program_id(1)
    q0 = pl.multiple_of(qi * tq, tq)
    heads = range(MLA_HEADS)
    qh = [q_ref[:, h * LANES:(h + 1) * LANES] for h in heads]

    def chunk(k0, n_keys, qs, nq, states, causal):
        if causal:
            mask = (lax.broadcasted_iota(jnp.int32, (n_keys, nq), 0)
                    <= qs + lax.broadcasted_iota(jnp.int32, (n_keys, nq), 1))

        def score_fn(h):
            s_t = _nt(k_ref[pl.ds(k0, n_keys), h * LANES:(h + 1) * LANES], qh[h][qs:qs + nq])
            return jnp.where(mask, s_t, NEG_INF) if causal else s_t

        v_ts = [vt_ref[h * MLA_V_DIM:(h + 1) * MLA_V_DIM, pl.ds(k0, n_keys)] for h in heads]
        return _softmax_steps([functools.partial(score_fn, h) for h in heads], v_ts, states)

    states = chunk(q0, tq, 0, tq, tuple(_softmax_init(tq) for _ in heads), True)
    states = lax.fori_loop(
        0, qi, lambda c, st: chunk(pl.multiple_of(c * tq, tq), tq, 0, tq, st, False), states)
    for h in heads:
        out_t_ref[h * MLA_V_DIM:(h + 1) * MLA_V_DIM, :] = _softmax_finish(states[h][1], states[h][2])
    out_t_ref[MLA_V_ROWS:, :] = jnp.zeros((RET_PAIRS * LANES - MLA_V_ROWS, tq), F32)
    for j in range(RET_PAIRS):
        o_ref[:, j * LANES:(j + 1) * LANES] = out_t_ref[j * LANES:(j + 1) * LANES, :].T.astype(BF16)


def _mla_attention(q, k, vt, batch, seq):
    tq = ATT_T
    nq = seq // tq
    return pl.pallas_call(
        _mla_attn_kernel,
        grid=(batch, nq),
        in_specs=[
            pl.BlockSpec((tq, MLA_HEADS * LANES), lambda b, i: (b * nq + i, 0)),
            pl.BlockSpec((seq, MLA_HEADS * LANES), lambda b, i: (b, 0)),
            pl.BlockSpec((MLA_V_ROWS, seq), lambda b, i: (0, b)),
        ],
        out_specs=pl.BlockSpec((tq, RET_PAIRS * LANES), lambda b, i: (b * nq + i, 0)),
        out_shape=jax.ShapeDtypeStruct((batch * seq, RET_PAIRS * LANES), BF16),
        scratch_shapes=[pltpu.VMEM((RET_PAIRS * LANES, tq), F32)],
        compiler_params=pltpu.CompilerParams(
            dimension_semantics=("parallel", "arbitrary"), vmem_limit_bytes=VMEM_LIMIT),
        name="mla_attention",
    )(q, k, vt)


def _tn(a, b):
    return lax.dot_general(a, b, (((0,), (0,)), ((), ())), preferred_element_type=F32)


RET_UNROLL = 16


def _retention_kernel(q_ref, k_ref, v_ref, gate_ref, gn_ref, intra_ref, rd_ref, wd_ref, cd_ref, o_ref,
                      kv_ref, st_ref):
    c_len = RET_CHUNK
    n_chunks = q_ref.shape[0] // c_len
    lo = _lane_lo((1, LANES))
    hi = jnp.logical_not(lo)
    blockdiag = (lax.broadcasted_iota(jnp.int32, (LANES, LANES), 0) < HALF) == _lane_lo((LANES, LANES))
    intra_a = intra_ref[0, 0]
    intra_b = intra_ref[0, 1]
    read_decay = rd_ref[0]
    write_decay = wd_ref[0]
    chunk_decay = cd_ref[0]
    gn = gn_ref[0]

    def half_mean(x):
        s_lo = jnp.sum(jnp.where(lo, x, 0.0), axis=-1, keepdims=True)
        s_hi = jnp.sum(jnp.where(hi, x, 0.0), axis=-1, keepdims=True)
        return jnp.where(lo, s_lo, s_hi) * (1.0 / HEAD_DIM)

    def kv_body(c, carry):
        r0 = pl.multiple_of(c * c_len, c_len)
        kc = k_ref[pl.ds(r0, c_len), :]
        kv_ref[c] = _tn((kc.astype(F32) * write_decay).astype(BF16), v_ref[pl.ds(r0, c_len), :])
        return carry

    lax.fori_loop(0, n_chunks, kv_body, 0, unroll=RET_UNROLL)

    state = jnp.zeros((LANES, LANES), F32)
    for c in range(n_chunks):
        st_ref[c] = jnp.where(blockdiag, state, 0.0).astype(BF16)
        state = state * chunk_decay + kv_ref[c]

    def out_body(step, carry):
        cs = [step * RET_UNROLL + u for u in range(RET_UNROLL)]
        rows = [pl.ds(pl.multiple_of(c * c_len, c_len), c_len) for c in cs]
        qs = [q_ref[r, :] for r in rows]
        ks = [k_ref[r, :] for r in rows]
        vs = [v_ref[r, :] for r in rows]
        zero = jnp.zeros_like(qs[0])
        sa = [(_nt(jnp.where(lo, q, zero), k) * intra_a).astype(BF16) for q, k in zip(qs, ks)]
        sb = [(_nt(jnp.where(hi, q, zero), k) * intra_b).astype(BF16) for q, k in zip(qs, ks)]
        cross = [_nn(q, st_ref[c]) * read_decay for q, c in zip(qs, cs)]
        os_ = [jnp.where(lo, _nn(a, v), _nn(b, v)) + x for a, b, v, x in zip(sa, sb, vs, cross)]
        ds_ = [o - half_mean(o) for o in os_]
        ys = [d * lax.rsqrt(half_mean(d * d) + NORM_EPS) * gn for d in ds_]
        for r, y in zip(rows, ys):
            o_ref[r, :] = (jax.nn.silu(gate_ref[r, :]) * y).astype(BF16)
        return carry

    lax.fori_loop(0, n_chunks // RET_UNROLL, out_body, 0)


def _retention(ret, rgate, gn, intra, rd, wd, cd, batch, seq):
    pair_const3 = lambda b, j: (j, 0, 0)
    return pl.pallas_call(
        _retention_kernel,
        grid=(batch, RET_PAIRS),
        in_specs=[
            pl.BlockSpec((seq, LANES), lambda b, j: (b, j)),
            pl.BlockSpec((seq, LANES), lambda b, j: (b, RET_PAIRS + j)),
            pl.BlockSpec((seq, LANES), lambda b, j: (b, 2 * RET_PAIRS + j)),
            pl.BlockSpec((seq, LANES), lambda b, j: (b, j)),
            pl.BlockSpec((1, 1, LANES), pair_const3),
            pl.BlockSpec((1, 2, RET_CHUNK, RET_CHUNK), lambda b, j: (j, 0, 0, 0)),
            pl.BlockSpec((1, RET_CHUNK, LANES), pair_const3),
            pl.BlockSpec((1, RET_CHUNK, LANES), pair_const3),
            pl.BlockSpec((1, 1, LANES), pair_const3),
        ],
        out_specs=pl.BlockSpec((seq, LANES), lambda b, j: (b, j)),
        out_shape=jax.ShapeDtypeStruct((batch * seq, RET_PAIRS * LANES), BF16),
        scratch_shapes=[pltpu.VMEM((seq // RET_CHUNK, LANES, LANES), F32),
                        pltpu.VMEM((seq // RET_CHUNK, LANES, LANES), BF16)],
        compiler_params=pltpu.CompilerParams(
            dimension_semantics=("parallel", "arbitrary"), vmem_limit_bytes=VMEM_LIMIT),
        name="retention",
    )(ret, ret, ret, rgate, gn, intra, rd, wd, cd)


def _out_mlp_kernel(x_ref, nsa_ref, mla_ref, ret_ref, wo_ref, g2_ref, wu_ref, wd_ref, gf_ref, o_ref,
                    *, final_norm):
    w = RET_PAIRS * LANES
    mixed = (_nn(nsa_ref[...], wo_ref[0:w, :]) + _nn(mla_ref[...], wo_ref[w:2 * w, :])
             + _nn(ret_ref[...], wo_ref[2 * w:3 * w, :]))
    x = x_ref[...] + mixed
    h = _rms(x, g2_ref[...]).astype(BF16)
    y = x
    for c in range(D_FF // MLP_FF_CHUNK):
        sl = slice(c * MLP_FF_CHUNK, (c + 1) * MLP_FF_CHUNK)
        u = jnp.maximum(_nn(h, wu_ref[:, sl]), 0.0)
        y = y + _nn((u * u).astype(BF16), wd_ref[sl, :])
    if final_norm:
        y = _rms(y, gf_ref[...])
    o_ref[...] = y


def _out_mlp(x2, o_nsa, o_mla, o_ret, wo, g2, wu, wd, layer, gf, final_norm):
    t = x2.shape[0]
    w = RET_PAIRS * LANES
    row = lambda i: (i, 0)
    const = lambda i: (0, 0)
    resident = dict(pipeline_mode=pl.Buffered(1))
    return pl.pallas_call(
        functools.partial(_out_mlp_kernel, final_norm=final_norm),
        grid=(t // MLP_TM,),
        in_specs=[
            pl.BlockSpec((MLP_TM, D_MODEL), row),
            pl.BlockSpec((MLP_TM, w), row),
            pl.BlockSpec((MLP_TM, w), row),
            pl.BlockSpec((MLP_TM, w), row),
            _layer_spec(wo, layer, **resident),
            pl.BlockSpec((1, D_MODEL), const),
            _layer_spec(wu, layer, **resident),
            _layer_spec(wd, layer, **resident),
            pl.BlockSpec((1, D_MODEL), const),
        ],
        out_specs=pl.BlockSpec((MLP_TM, D_MODEL), row),
        out_shape=jax.ShapeDtypeStruct((t, D_MODEL), F32),
        compiler_params=pltpu.CompilerParams(
            dimension_semantics=("parallel",), vmem_limit_bytes=VMEM_LIMIT),
        name="out_mlp",
    )(x2, o_nsa, o_mla, o_ret, wo, g2, wu, wd, gf)


def _rope_table(positions, dim, theta, period, base):
    half = dim // 2
    inv = 1.0 / (theta ** (jnp.arange(0, dim, 2, dtype=F32) / dim))
    ang = positions.reshape(-1).astype(F32)[:, None] * inv
    cos_sin = jnp.concatenate([jnp.cos(ang), jnp.sin(ang)], axis=1)
    lane = np.arange(LANES)
    rel = (lane - base) % period
    first = (lane >= base) & (rel < half)
    second = (lane >= base) & (rel >= half) & (rel < dim)
    idx = np.where(first, rel, np.where(second, rel - half, 0))
    place = np.zeros((dim, 3 * LANES), np.float32)
    fill = np.zeros((3 * LANES,), np.float32)
    for l in range(LANES):
        if first[l] or second[l]:
            place[idx[l], l] = 1.0
        else:
            fill[l] = 1.0
        if first[l]:
            place[half + idx[l], LANES + l] = -1.0
        if second[l]:
            place[half + idx[l], 2 * LANES + l] = 1.0
    return jnp.dot(cos_sin, place, precision=lax.Precision.HIGHEST) + fill


def _pad_cols(w, n):
    return jnp.pad(w, ((0, 0), (0, 0), (0, n - w.shape[-1])))


def _in_weight(w_in):
    offs = np.cumsum((0,) + IN_SIZES)
    seg = [w_in[:, :, offs[i]:offs[i + 1]] for i in range(len(IN_SIZES))]
    (nsa_q, k_cmp, v_cmp, k_slc, v_slc, k_win, v_win, gate,
     cq, ckv, kpe, ret_q, ret_k, ret_v, ret_g) = seg
    scale = HEAD_DIM ** -0.5
    qh = [nsa_q[:, :, h * HEAD_DIM:(h + 1) * HEAD_DIM] * scale for h in range(NSA_HEADS)]
    tiles = [jnp.concatenate([qh[j], qh[j + NSA_REP]], axis=-1) for j in range(NSA_REP)]
    tiles += [k_slc, k_win, k_cmp, v_cmp]
    tiles += [cq, ckv, jnp.pad(kpe, ((0, 0), (0, 0), (HALF, LANES - HALF - MLA_ROPE_DIM)))]
    ret_w = RET_PAIRS * LANES
    tiles += [_pad_cols(ret_q, ret_w), _pad_cols(ret_k * scale, ret_w), _pad_cols(ret_v, ret_w),
              _pad_cols(ret_g, ret_w)]
    w_t = jnp.concatenate([v_slc, v_win, _pad_cols(gate, GATE_ROWS)], axis=-1).transpose(0, 2, 1)
    return jnp.concatenate(tiles, axis=-1).astype(BF16), w_t.astype(BF16)


def _compress_weights(pos, w1, w2):
    nl = pos.shape[0]
    g, dh, hid = NSA_KV_GROUPS, HEAD_DIM, CMP_HIDDEN
    p = jnp.tile(pos, (1, 1, g))
    eye = jnp.eye(g, dtype=w1.dtype)
    w = w1.reshape(nl, CMP_LEN, dh, hid)
    w = jnp.einsum('ltdc,gk->ltgdkc', w, eye).reshape(nl, CMP_LEN, g * dh, g * hid)
    w2b = jnp.einsum('lcd,gk->lgckd', w2, eye).reshape(nl, g * hid, g * dh)
    return p, w.astype(BF16), w2b.astype(BF16)


def _mla_weights(w_uq, w_ukv):
    nl = w_uq.shape[0]
    dq = MLA_NOPE_DIM + MLA_ROPE_DIM
    wq = w_uq.reshape(nl, MLA_Q_RANK, MLA_HEADS, dq)
    wq = jnp.pad(wq, ((0, 0), (0, 0), (0, 0), (0, LANES - dq))).reshape(nl, MLA_Q_RANK, MLA_HEADS * LANES)
    wkv = w_ukv.reshape(nl, MLA_KV_RANK, MLA_HEADS, MLA_NOPE_DIM + MLA_V_DIM)
    wk = jnp.pad(wkv[..., :MLA_NOPE_DIM], ((0, 0), (0, 0), (0, 0), (0, LANES - MLA_NOPE_DIM)))
    wk = wk.reshape(nl, MLA_KV_RANK, MLA_HEADS * LANES)
    wvt = wkv[..., MLA_NOPE_DIM:].reshape(nl, MLA_KV_RANK, MLA_V_ROWS).transpose(0, 2, 1)
    return wq.astype(BF16), wk.astype(BF16), wvt.astype(BF16)


def _out_weight(w_out):
    nl = w_out.shape[0]
    pad_rows = lambda w: jnp.pad(w, ((0, 0), (0, RET_PAIRS * LANES - w.shape[1]), (0, 0)))
    nsa = w_out[:, :NSA_Q_W].reshape(nl, NSA_HEADS, HEAD_DIM, D_MODEL)
    order = [h for j in range(NSA_REP) for h in (j, j + NSA_REP)]
    nsa = nsa[:, order].reshape(nl, NSA_Q_W, D_MODEL)
    mla = pad_rows(w_out[:, NSA_Q_W:NSA_Q_W + MLA_HEADS * MLA_V_DIM])
    ret = pad_rows(w_out[:, NSA_Q_W + MLA_HEADS * MLA_V_DIM:])
    return jnp.concatenate([nsa, mla, ret], axis=1).astype(BF16)


def _retention_tables(gn_gain):
    nh = 2 * RET_PAIRS
    log_g = jnp.log(1.0 - 2.0 ** (-5.0 - jnp.arange(nh, dtype=F32)))
    i = jnp.arange(RET_CHUNK, dtype=F32)
    diff = i[:, None] - i[None, :]
    intra = jnp.where(diff >= 0, jnp.exp(jnp.maximum(diff, 0.0)[None] * log_g[:, None, None]), 0.0)
    read_decay = jnp.exp((i + 1.0)[None, :] * log_g[:, None])
    write_decay = jnp.exp((RET_CHUNK - 1.0 - i)[None, :] * log_g[:, None])
    chunk_decay = jnp.exp(RET_CHUNK * log_g)

    def lanes(t):
        t = t.reshape(RET_PAIRS, 2, -1)
        return jnp.repeat(t.transpose(0, 2, 1), HALF, axis=-1)

    gn = jnp.pad(gn_gain, ((0, 0), (0, nh - RET_HEADS), (0, 0)))
    gn = gn.reshape(gn.shape[0], RET_PAIRS, 1, LANES)
    return (gn, intra.reshape(RET_PAIRS, 2, RET_CHUNK, RET_CHUNK), lanes(read_decay), lanes(write_decay),
            lanes(chunk_decay[:, None]))


def _selection_overlap(seq):
    n_cmp = (seq - CMP_LEN) // CMP_STRIDE + 1
    n_sel = seq // SEL_BLOCK
    cs = np.arange(n_cmp) * CMP_STRIDE
    ss = np.arange(n_sel) * SEL_BLOCK
    ov = np.clip(np.minimum(cs[:, None] + CMP_LEN, ss[None, :] + SEL_BLOCK)
                 - np.maximum(cs[:, None], ss[None, :]), 0, None) / CMP_LEN
    ovl_t = np.zeros((n_sel, seq // CMP_STRIDE), np.float32)
    ovl_t[:, :n_cmp] = ov.T
    return jnp.asarray(ovl_t, BF16)


def kernel(x, positions, ln1_gain, w_in, cmp_pos_k, cmp_w1_k, cmp_w2_k, cmp_pos_v, cmp_w1_v, cmp_w2_v,
           mla_q_norm, mla_w_uq, mla_kv_norm, mla_w_ukv, ret_gn_gain, w_out, ln2_gain, w_up, w_down,
           final_gain):
    batch, seq, _ = x.shape
    depth = w_in.shape[0]
    t = batch * seq

    tab_n = _rope_table(positions, PARTIAL_ROPE_DIM, ROPE_THETA, HEAD_DIM, 0)
    tab_m = _rope_table(positions, MLA_ROPE_DIM, ROPE_THETA, LANES, HALF)
    tab_r = _rope_table(positions, HEAD_DIM, RET_THETA, HEAD_DIM, 0)

    w_in_p, w_in_t = _in_weight(w_in)
    pos_k, w1_k, w2_k = _compress_weights(cmp_pos_k, cmp_w1_k, cmp_w2_k)
    pos_v, w1_v, w2_v = _compress_weights(cmp_pos_v, cmp_w1_v, cmp_w2_v)
    cmp_pos = jnp.stack([pos_k, pos_v], axis=1)
    cmp_w1 = jnp.stack([w1_k, w1_v], axis=1)
    w2_vt = w2_v.transpose(0, 2, 1)
    wq, wk, wvt = _mla_weights(mla_w_uq, mla_w_ukv)
    wo = _out_weight(w_out)
    wu = w_up.astype(BF16)
    wd = w_down.astype(BF16)
    gn, intra, rd, wdec, cd = _retention_tables(ret_gn_gain)
    ovl_t = _selection_overlap(seq)
    gf = final_gain.reshape(1, D_MODEL)

    x2 = x.reshape(t, D_MODEL)
    for l in range(depth):
        nsa, nsa_t, gate_t, k_cmp, v_cmp, mla_in, ret, rgate = _in_proj(
            x2, ln1_gain[l].reshape(1, D_MODEL), w_in_p, w_in_t, l, tab_n, tab_m, tab_r)
        kc, vct = _compress(k_cmp, v_cmp, cmp_pos[l], cmp_w1, w2_k[l], w2_vt[l], l, batch, seq)
        o_nsa = _nsa_attention(nsa, nsa_t, gate_t, kc, vct, ovl_t, batch, seq)
        q_m, k_m, vt_m = _mla_up(mla_in, mla_q_norm[l].reshape(1, -1), mla_kv_norm[l].reshape(1, -1),
                                 wq, wk, wvt, l, tab_m)
        o_mla = _mla_attention(q_m, k_m, vt_m, batch, seq)
        o_ret = _retention(ret, rgate, gn[l], intra, rd, wdec, cd, batch, seq)
        x2 = _out_mlp(x2, o_nsa, o_mla, o_ret, wo, ln2_gain[l].reshape(1, D_MODEL), wu, wd, l, gf,
                      final_norm=(l == depth - 1))
    return x2.reshape(batch, seq, D_MODEL)
```

```python
import functools
import math

import numpy as np
import jax
import jax.numpy as jnp
from jax import lax
from jax.experimental import pallas as pl
from jax.experimental.pallas import tpu as pltpu

F32 = jnp.float32
BF16 = jnp.bfloat16

D_MODEL = 1024
HEAD_DIM = 64
NSA_HEADS = 6
NSA_KV_GROUPS = 2
NSA_REP = NSA_HEADS // NSA_KV_GROUPS
N_BRANCH = 3
CMP_LEN = 32
CMP_STRIDE = 16
CMP_HIDDEN = 2 * HEAD_DIM
SEL_BLOCK = 64
SEL_TOP_N = 16
WINDOW = 512
MLA_HEADS = 5
MLA_Q_RANK = 256
MLA_KV_RANK = 128
MLA_NOPE_DIM = 64
MLA_ROPE_DIM = 32
MLA_V_DIM = 64
RET_HEADS = 5
RET_CHUNK = 128
ROPE_THETA = 500000.0
PARTIAL_ROPE_DIM = HEAD_DIM // 4
RET_THETA = 10000.0
D_FF = 4 * D_MODEL
NORM_EPS = 1e-6
NEG_INF = -1e30
FORCE_SCORE = 1e9
LOG2E = math.log2(math.e)

NSA_Q_W = NSA_HEADS * HEAD_DIM
NSA_KV_W = NSA_KV_GROUPS * HEAD_DIM
NSA_GATE_W = NSA_HEADS * N_BRANCH
RET_W = RET_HEADS * HEAD_DIM
IN_SIZES = (NSA_Q_W, NSA_KV_W, NSA_KV_W, NSA_KV_W, NSA_KV_W, NSA_KV_W, NSA_KV_W, NSA_GATE_W,
            MLA_Q_RANK, MLA_KV_RANK, MLA_ROPE_DIM, RET_W, RET_W, RET_W, RET_W)

LANES = 128
HALF = LANES // 2
VMEM_LIMIT = 56 * 1024 * 1024

NSA_TILES = 5
GATE_ROWS = 32
NSA_T_ROWS = 2 * LANES + GATE_ROWS
MLA_IN_TILES = 4
RET_PAIRS = 3
RET_TILES = 3 * RET_PAIRS
IN_TILES = NSA_TILES + 2 + MLA_IN_TILES + RET_TILES + RET_PAIRS
N_PAD = IN_TILES * LANES
MLA_V_ROWS = MLA_HEADS * MLA_V_DIM

IN_TM = 512
MLP_TM = 512
MLA_TM = 512
ATT_T = 512
MLP_FF_CHUNK = 1024


def _nn(a, b):
    return jnp.dot(a, b, preferred_element_type=F32)


def _nt(a, b):
    return lax.dot_general(a, b, (((1,), (1,)), ((), ())), preferred_element_type=F32)


def _rms(x, gain):
    return x * lax.rsqrt(jnp.mean(x * x, axis=-1, keepdims=True) + NORM_EPS) * gain


def _rope(val, tab, half):
    cos = tab[:, 0:LANES]
    sin_a = tab[:, LANES:2 * LANES]
    sin_b = tab[:, 2 * LANES:3 * LANES]
    return (val * cos + pltpu.roll(val, LANES - half, 1) * sin_a
            + pltpu.roll(val, half, 1) * sin_b)


def _lane_lo(shape):
    return lax.broadcasted_iota(jnp.int32, shape, len(shape) - 1) < HALF


def _inproj_kernel(x_ref, g_ref, w_ref, wt_ref, tn_ref, tm_ref, tr_ref,
                   nsa_ref, nsat_ref, gatet_ref, kcmp_ref, vcmp_ref, mla_ref, ret_ref, rgate_ref):
    h = _rms(x_ref[...], g_ref[...]).astype(BF16)
    tab_n = tn_ref[...]
    tab_r = tr_ref[...]

    def tile(a, i):
        return a[:, i * LANES:(i + 1) * LANES]

    c0 = 0
    a = _nn(h, w_ref[:, c0:c0 + NSA_TILES * LANES])
    for i in range(NSA_TILES):
        v = _rope(tile(a, i), tab_n, PARTIAL_ROPE_DIM // 2)
        if i < NSA_REP:
            v = v * LOG2E
        nsa_ref[:, i * LANES:(i + 1) * LANES] = v.astype(BF16)
    c0 += NSA_TILES * LANES
    a = _nn(h, w_ref[:, c0:c0 + 2 * LANES])
    kcmp_ref[...] = _rope(tile(a, 0), tab_n, PARTIAL_ROPE_DIM // 2)
    vcmp_ref[...] = tile(a, 1)
    c0 += 2 * LANES
    a = _nn(h, w_ref[:, c0:c0 + MLA_IN_TILES * LANES])
    mla_ref[:, 0:3 * LANES] = a[:, 0:3 * LANES]
    mla_ref[:, 3 * LANES:4 * LANES] = _rope(tile(a, 3), tm_ref[...], MLA_ROPE_DIM // 2)
    c0 += MLA_IN_TILES * LANES
    a = _nn(h, w_ref[:, c0:c0 + RET_TILES * LANES])
    for i in range(RET_TILES):
        v = tile(a, i)
        if i < 2 * RET_PAIRS:
            v = _rope(v, tab_r, HEAD_DIM // 2)
        ret_ref[:, i * LANES:(i + 1) * LANES] = v.astype(BF16)
    c0 += RET_TILES * LANES
    rgate_ref[...] = _nn(h, w_ref[:, c0:c0 + RET_PAIRS * LANES])
    at = _nt(wt_ref[...], h)
    nsat_ref[...] = at[0:2 * LANES, :].astype(BF16)
    gatet_ref[...] = at[2 * LANES:NSA_T_ROWS, :]


def _layer_spec(w, layer, **kwargs):
    zeros = (0,) * (w.ndim - 1)
    return pl.BlockSpec((None,) + w.shape[1:], lambda *_: (layer,) + zeros, **kwargs)


def _in_proj(x2, gain, w, wt, layer, tab_n, tab_m, tab_r):
    t = x2.shape[0]
    row = lambda i: (i, 0)
    col = lambda i: (0, i)
    const = lambda i: (0, 0)
    out_shapes = (
        jax.ShapeDtypeStruct((t, NSA_TILES * LANES), BF16),
        jax.ShapeDtypeStruct((2 * LANES, t), BF16),
        jax.ShapeDtypeStruct((GATE_ROWS, t), F32),
        jax.ShapeDtypeStruct((t, LANES), F32),
        jax.ShapeDtypeStruct((t, LANES), F32),
        jax.ShapeDtypeStruct((t, MLA_IN_TILES * LANES), F32),
        jax.ShapeDtypeStruct((t, RET_TILES * LANES), BF16),
        jax.ShapeDtypeStruct((t, RET_PAIRS * LANES), F32),
    )
    out_specs = tuple(
        pl.BlockSpec((s.shape[0], IN_TM), col) if s.shape[1] == t else pl.BlockSpec((IN_TM, s.shape[1]), row)
        for s in out_shapes)
    return pl.pallas_call(
        _inproj_kernel,
        grid=(t // IN_TM,),
        in_specs=[
            pl.BlockSpec((IN_TM, D_MODEL), row),
            pl.BlockSpec((1, D_MODEL), const),
            _layer_spec(w, layer),
            _layer_spec(wt, layer),
            pl.BlockSpec((IN_TM, 3 * LANES), row),
            pl.BlockSpec((IN_TM, 3 * LANES), row),
            pl.BlockSpec((IN_TM, 3 * LANES), row),
        ],
        out_specs=out_specs,
        out_shape=out_shapes,
        compiler_params=pltpu.CompilerParams(
            dimension_semantics=("parallel",), vmem_limit_bytes=VMEM_LIMIT),
        name="in_proj",
    )(x2, gain, w, wt, tab_n, tab_m, tab_r)


def _compress_kernel(k_ref, v_ref, pos_ref, w1_ref, w2k_ref, w2vt_ref, kc_ref, vct_ref):
    n_blk = k_ref.shape[0] // CMP_STRIDE

    def hidden(src, i):
        lo = jnp.zeros((n_blk, NSA_KV_GROUPS * CMP_HIDDEN), F32)
        hi = jnp.zeros((n_blk, NSA_KV_GROUPS * CMP_HIDDEN), F32)
        for r in range(CMP_STRIDE):
            tok = src[pl.ds(r, n_blk, stride=CMP_STRIDE), :]
            lo = lo + _nn((tok + pos_ref[i, r:r + 1, :]).astype(BF16), w1_ref[i, r])
            r2 = CMP_STRIDE + r
            hi = hi + _nn((tok + pos_ref[i, r2:r2 + 1, :]).astype(BF16), w1_ref[i, r2])
        return jax.nn.gelu(lo + pltpu.roll(hi, n_blk - 1, 0)).astype(BF16)

    kc_ref[0] = _nn(hidden(k_ref, 0), w2k_ref[...]).astype(BF16)
    vct_ref[0] = _nt(w2vt_ref[...], hidden(v_ref, 1)).astype(BF16)


def _compress(k_cmp, v_cmp, pos, w1, w2k, w2vt, layer, batch, seq):
    b = batch
    n_blk = seq // CMP_STRIDE
    return pl.pallas_call(
        _compress_kernel,
        grid=(b,),
        in_specs=[
            pl.BlockSpec((seq, LANES), lambda i: (i, 0)),
            pl.BlockSpec((seq, LANES), lambda i: (i, 0)),
            pl.BlockSpec(pos.shape, lambda i: (0, 0, 0)),
            _layer_spec(w1, layer),
            pl.BlockSpec(w2k.shape, lambda i: (0, 0)),
            pl.BlockSpec(w2vt.shape, lambda i: (0, 0)),
        ],
        out_specs=(pl.BlockSpec((1, n_blk, LANES), lambda i: (i, 0, 0)),
                   pl.BlockSpec((1, LANES, n_blk), lambda i: (i, 0, 0))),
        out_shape=(jax.ShapeDtypeStruct((b, n_blk, LANES), BF16),
                   jax.ShapeDtypeStruct((b, LANES, n_blk), BF16)),
        compiler_params=pltpu.CompilerParams(
            dimension_semantics=("parallel",), vmem_limit_bytes=VMEM_LIMIT),
        name="nsa_compress",
    )(k_cmp, v_cmp, pos, w1, w2k, w2vt)


SCORE_LOOKAHEAD = 2
ONES_ROWS = 16


def _softmax_steps(score_fns, v_ts, states):
    n = len(score_fns)
    s_ts = [score_fns[h]() if h < SCORE_LOOKAHEAD else None for h in range(n)]
    out = []
    for h in range(n):
        if h + SCORE_LOOKAHEAD < n:
            s_ts[h + SCORE_LOOKAHEAD] = score_fns[h + SCORE_LOOKAHEAD]()
        m_old, acc_old = states[h]
        m = jnp.maximum(m_old, jnp.max(s_ts[h], axis=0, keepdims=True))
        p = jnp.exp2(s_ts[h] - m).astype(BF16)
        s_ts[h] = None
        v_ext = jnp.concatenate([v_ts[h], jnp.ones((ONES_ROWS, v_ts[h].shape[1]), BF16)], axis=0)
        out.append((m, jnp.exp2(m_old - m) * acc_old + _nn(v_ext, p)))
    return tuple(out)


def _softmax_init(cols):
    return jnp.full((1, cols), NEG_INF, F32), jnp.zeros((HEAD_DIM + ONES_ROWS, cols), F32)


def _softmax_finish(state):
    acc = state[1]
    return acc[0:HEAD_DIM] * (1.0 / jnp.maximum(acc[HEAD_DIM:HEAD_DIM + 1], 1e-30))


def _nsa_kernel(q_ref, gatet_ref, kc_ref, vct_ref, ks_ref, kw_ref, vst_ref, vwt_ref, ovl_ref,
                o_ref, out_t_ref, selb_ref):
    tq = ATT_T
    qi = pl.program_id(1)
    q0 = pl.multiple_of(qi * tq, tq)
    lo1 = _lane_lo((1, LANES))
    group_lanes = (lo1, jnp.logical_not(lo1))
    gate = jax.nn.sigmoid(gatet_ref[...])
    heads = range(NSA_HEADS)

    def q_head(h):
        j, g = h % NSA_REP, h // NSA_REP
        tile = q_ref[:, j * LANES:(j + 1) * LANES]
        return jnp.where(group_lanes[g], tile, jnp.zeros_like(tile))

    def v_rows(ref, h, k0, n):
        g = h // NSA_REP
        return ref[g * HEAD_DIM:(g + 1) * HEAD_DIM, pl.ds(k0, n)]

    def emit(h, branch, qs, nq, o_t):
        j, g = h % NSA_REP, h // NSA_REP
        r0 = j * LANES + g * HEAD_DIM
        row = h * N_BRANCH + branch
        val = gate[row:row + 1, qs:qs + nq] * o_t
        if branch == 0:
            out_t_ref[r0:r0 + HEAD_DIM, qs:qs + nq] = val
        else:
            out_t_ref[r0:r0 + HEAD_DIM, qs:qs + nq] += val

    qm = [q_head(h) for h in heads]

    n_cmp_pad = kc_ref.shape[1]
    kc = kc_ref[0]
    vct = vct_ref[0]
    n_i = lax.broadcasted_iota(jnp.int32, (n_cmp_pad, tq), 0)
    t_l = q0 + lax.broadcasted_iota(jnp.int32, (n_cmp_pad, tq), 1)
    cmask = (n_i * CMP_STRIDE + (CMP_LEN - 1)) <= t_l
    cmask_f = cmask.astype(F32)
    psum = [None] * NSA_KV_GROUPS
    for h in heads:
        g = h // NSA_REP
        s_t = jnp.where(cmask, _nt(kc, qm[h]), NEG_INF)
        m = jnp.max(s_t, axis=0, keepdims=True)
        p = jnp.exp2(s_t - m) * cmask_f
        p = p * (1.0 / jnp.maximum(jnp.sum(p, axis=0, keepdims=True), 1e-30))
        psum[g] = p if psum[g] is None else psum[g] + p
        emit(h, 0, 0, tq, _nn(vct[g * HEAD_DIM:(g + 1) * HEAD_DIM, :], p.astype(BF16)))

    n_sel = ovl_ref.shape[0]
    m_i = lax.broadcasted_iota(jnp.int32, (n_sel, tq), 0)
    cur = jnp.right_shift(q0 + lax.broadcasted_iota(jnp.int32, (n_sel, tq), 1), SEL_BLOCK.bit_length() - 1)
    valid = m_i <= cur
    forced = (m_i == 0) | (m_i == cur) | (m_i == cur - 1)
    ovl = ovl_ref[...]
    for g in range(NSA_KV_GROUPS):
        p_hi = psum[g].astype(BF16)
        p_lo = (psum[g] - p_hi.astype(F32)).astype(BF16)
        imp = _nn(ovl, p_hi) + _nn(ovl, p_lo)
        imp = jnp.where(valid & forced, FORCE_SCORE, imp)
        imp = jnp.where(valid, imp, NEG_INF)
        rank = jnp.zeros((n_sel, tq), jnp.int32)
        for mp in range(n_sel):
            row = imp[mp:mp + 1, :]
            beats = (row > imp) | ((row == imp) & (m_i > mp))
            rank = rank + beats.astype(jnp.int32)
        selb_ref[g] = jnp.where(rank < SEL_TOP_N, 0.0, NEG_INF)

    blocks_per_chunk = tq // SEL_BLOCK

    def sel_bias(g, c, n_keys, qs, nq):
        rows = [jnp.broadcast_to(selb_ref[g, pl.ds(c * blocks_per_chunk + i, 1), qs:qs + nq], (SEL_BLOCK, nq))
                for i in range(n_keys // SEL_BLOCK)]
        return jnp.concatenate(rows, axis=0)

    def sel_chunk(c, k0, n_keys, qs, nq, states, causal):
        k = ks_ref[pl.ds(k0, n_keys), :]
        bias = [sel_bias(g, c, n_keys, qs, nq) for g in range(NSA_KV_GROUPS)]
        if causal:
            mask = (lax.broadcasted_iota(jnp.int32, (n_keys, nq), 0)
                    <= qs + lax.broadcasted_iota(jnp.int32, (n_keys, nq), 1))

        def score_fn(h):
            s_t = _nt(k, qm[h][qs:qs + nq]) + bias[h // NSA_REP]
            return jnp.where(mask, s_t, NEG_INF) if causal else s_t

        return _softmax_steps([functools.partial(score_fn, h) for h in heads],
                              [v_rows(vst_ref, h, k0, n_keys) for h in heads], states)

    states = sel_chunk(qi, q0, tq, 0, tq, tuple(_softmax_init(tq) for _ in heads), True)
    states = lax.fori_loop(
        0, qi, lambda c, st: sel_chunk(c, pl.multiple_of(c * tq, tq), tq, 0, tq, st, False), states)
    for h in heads:
        emit(h, 1, 0, tq, _softmax_finish(states[h]))

    key_i = lax.broadcasted_iota(jnp.int32, (tq, tq), 0)
    qry_i = lax.broadcasted_iota(jnp.int32, (tq, tq), 1)

    def win_chunk(k0, states, mask):
        k = kw_ref[pl.ds(k0, tq), :]
        score_fns = [lambda h=h: jnp.where(mask, _nt(k, qm[h]), NEG_INF) for h in heads]
        return _softmax_steps(score_fns, [v_rows(vwt_ref, h, k0, tq) for h in heads], states)

    states = win_chunk(q0, tuple(_softmax_init(tq) for _ in heads), key_i <= qry_i)
    for d in range(1, WINDOW // tq + 1):
        k0 = pl.multiple_of(jnp.maximum(qi - d, 0) * tq, tq)
        in_band = (key_i - d * tq > qry_i - WINDOW) & (qi >= d)
        states = win_chunk(k0, states, in_band)
    for h in heads:
        emit(h, 2, 0, tq, _softmax_finish(states[h]))

    for j in range(NSA_REP):
        o_ref[:, j * LANES:(j + 1) * LANES] = out_t_ref[j * LANES:(j + 1) * LANES, :].T.astype(BF16)


def _nsa_attention(nsa, nsa_t, gate_t, kc, vct, ovl_t, batch, seq):
    tq = ATT_T
    nq = seq // tq
    n_blk = kc.shape[1]
    qrow = lambda b, i: (b * nq + i, 0)
    return pl.pallas_call(
        _nsa_kernel,
        grid=(batch, nq),
        in_specs=[
            pl.BlockSpec((tq, NSA_REP * LANES), qrow),
            pl.BlockSpec((GATE_ROWS, tq), lambda b, i: (0, b * nq + i)),
            pl.BlockSpec((1, n_blk, LANES), lambda b, i: (b, 0, 0)),
            pl.BlockSpec((1, LANES, n_blk), lambda b, i: (b, 0, 0)),
            pl.BlockSpec((seq, LANES), lambda b, i: (b, 3)),
            pl.BlockSpec((seq, LANES), lambda b, i: (b, 4)),
            pl.BlockSpec((LANES, seq), lambda b, i: (0, b)),
            pl.BlockSpec((LANES, seq), lambda b, i: (1, b)),
            pl.BlockSpec(ovl_t.shape, lambda b, i: (0, 0)),
        ],
        out_specs=pl.BlockSpec((tq, NSA_REP * LANES), qrow),
        out_shape=jax.ShapeDtypeStruct((batch * seq, NSA_REP * LANES), BF16),
        scratch_shapes=[pltpu.VMEM((NSA_REP * LANES, tq), F32),
                        pltpu.VMEM((NSA_KV_GROUPS, seq // SEL_BLOCK, tq), F32)],
        compiler_params=pltpu.CompilerParams(
            dimension_semantics=("parallel", "arbitrary"), vmem_limit_bytes=VMEM_LIMIT),
        name="nsa_attention",
    )(nsa, gate_t, kc, vct, nsa, nsa, nsa_t, nsa_t, ovl_t)


def _mla_up_kernel(in_ref, qn_ref, kvn_ref, wq_ref, wk_ref, wvt_ref, tab_ref, q_ref, k_ref, vt_ref):
    scale = (MLA_NOPE_DIM + MLA_ROPE_DIM) ** -0.5 * LOG2E
    cq = _rms(in_ref[:, 0:MLA_Q_RANK], qn_ref[...]).astype(BF16)
    ckv = _rms(in_ref[:, MLA_Q_RANK:MLA_Q_RANK + MLA_KV_RANK], kvn_ref[...]).astype(BF16)
    k_pe = in_ref[:, 3 * LANES:4 * LANES]
    tab = tab_ref[...]
    q = _nn(cq, wq_ref[...])
    k = _nn(ckv, wk_ref[...])
    for h in range(MLA_HEADS):
        sl = slice(h * LANES, (h + 1) * LANES)
        q_ref[:, sl] = (_rope(q[:, sl], tab, MLA_ROPE_DIM // 2) * scale).astype(BF16)
        k_ref[:, sl] = (k[:, sl] + k_pe).astype(BF16)
    vt_ref[...] = _nt(wvt_ref[...], ckv).astype(BF16)


def _mla_up(mla_in, q_norm, kv_norm, wq, wk, wvt, layer, tab_m):
    t = mla_in.shape[0]
    row = lambda i: (i, 0)
    const = lambda i: (0, 0)
    return pl.pallas_call(
        _mla_up_kernel,
        grid=(t // MLA_TM,),
        in_specs=[
            pl.BlockSpec((MLA_TM, MLA_IN_TILES * LANES), row),
            pl.BlockSpec((1, MLA_Q_RANK), const),
            pl.BlockSpec((1, MLA_KV_RANK), const),
            _layer_spec(wq, layer),
            _layer_spec(wk, layer),
            _layer_spec(wvt, layer),
            pl.BlockSpec((MLA_TM, 3 * LANES), row),
        ],
        out_specs=(pl.BlockSpec((MLA_TM, MLA_HEADS * LANES), row),
                   pl.BlockSpec((MLA_TM, MLA_HEADS * LANES), row),
                   pl.BlockSpec((MLA_V_ROWS, MLA_TM), lambda i: (0, i))),
        out_shape=(jax.ShapeDtypeStruct((t, MLA_HEADS * LANES), BF16),
                   jax.ShapeDtypeStruct((t, MLA_HEADS * LANES), BF16),
                   jax.ShapeDtypeStruct((MLA_V_ROWS, t), BF16)),
        compiler_params=pltpu.CompilerParams(
            dimension_semantics=("parallel",), vmem_limit_bytes=VMEM_LIMIT),
        name="mla_up",
    )(mla_in, q_norm, kv_norm, wq, wk, wvt, tab_m)


def _mla_attn_kernel(q_ref, k_ref, vt_ref, o_ref, out_t_ref):
    tq = ATT_T
    qi = pl.program_id(1)
    q0 = pl.multiple_of(qi * tq, tq)
    heads = range(MLA_HEADS)
    qh = [q_ref[:, h * LANES:(h + 1) * LANES] for h in heads]

    def chunk(k0, n_keys, qs, nq, states, causal):
        if causal:
            mask = (lax.broadcasted_iota(jnp.int32, (n_keys, nq), 0)
                    <= qs + lax.broadcasted_iota(jnp.int32, (n_keys, nq), 1))

        def score_fn(h):
            s_t = _nt(k_ref[pl.ds(k0, n_keys), h * LANES:(h + 1) * LANES], qh[h][qs:qs + nq])
            return jnp.where(mask, s_t, NEG_INF) if causal else s_t

        v_ts = [vt_ref[h * MLA_V_DIM:(h + 1) * MLA_V_DIM, pl.ds(k0, n_keys)] for h in heads]
        return _softmax_steps([functools.partial(score_fn, h) for h in heads], v_ts, states)

    states = chunk(q0, tq, 0, tq, tuple(_softmax_init(tq) for _ in heads), True)
    states = lax.fori_loop(
        0, qi, lambda c, st: chunk(pl.multiple_of(c * tq, tq), tq, 0, tq, st, False), states)
    for h in heads:
        out_t_ref[h * MLA_V_DIM:(h + 1) * MLA_V_DIM, :] = _softmax_finish(states[h])
    out_t_ref[MLA_V_ROWS:, :] = jnp.zeros((RET_PAIRS * LANES - MLA_V_ROWS, tq), F32)
    for j in range(RET_PAIRS):
        o_ref[:, j * LANES:(j + 1) * LANES] = out_t_ref[j * LANES:(j + 1) * LANES, :].T.astype(BF16)


def _mla_attention(q, k, vt, batch, seq):
    tq = ATT_T
    nq = seq // tq
    return pl.pallas_call(
        _mla_attn_kernel,
        grid=(batch, nq),
        in_specs=[
            pl.BlockSpec((tq, MLA_HEADS * LANES), lambda b, i: (b * nq + i, 0)),
            pl.BlockSpec((seq, MLA_HEADS * LANES), lambda b, i: (b, 0)),
            pl.BlockSpec((MLA_V_ROWS, seq), lambda b, i: (0, b)),
        ],
        out_specs=pl.BlockSpec((tq, RET_PAIRS * LANES), lambda b, i: (b * nq + i, 0)),
        out_shape=jax.ShapeDtypeStruct((batch * seq, RET_PAIRS * LANES), BF16),
        scratch_shapes=[pltpu.VMEM((RET_PAIRS * LANES, tq), F32)],
        compiler_params=pltpu.CompilerParams(
            dimension_semantics=("parallel", "arbitrary"), vmem_limit_bytes=VMEM_LIMIT),
        name="mla_attention",
    )(q, k, vt)


def _tn(a, b):
    return lax.dot_general(a, b, (((0,), (0,)), ((), ())), preferred_element_type=F32)


RET_UNROLL = 16


def _retention_kernel(q_ref, k_ref, v_ref, gate_ref, gn_ref, intra_ref, rd_ref, wd_ref, cd_ref, o_ref,
                      kv_ref, st_ref):
    c_len = RET_CHUNK
    n_chunks = q_ref.shape[0] // c_len
    lo = _lane_lo((1, LANES))
    hi = jnp.logical_not(lo)
    blockdiag = (lax.broadcasted_iota(jnp.int32, (LANES, LANES), 0) < HALF) == _lane_lo((LANES, LANES))
    intra_a = intra_ref[0, 0]
    intra_b = intra_ref[0, 1]
    read_decay = rd_ref[0]
    write_decay = wd_ref[0]
    chunk_decay = cd_ref[0]
    gn = gn_ref[0]

    def half_mean(x):
        s_lo = jnp.sum(jnp.where(lo, x, 0.0), axis=-1, keepdims=True)
        s_hi = jnp.sum(jnp.where(hi, x, 0.0), axis=-1, keepdims=True)
        return jnp.where(lo, s_lo, s_hi) * (1.0 / HEAD_DIM)

    def kv_body(c, carry):
        r0 = pl.multiple_of(c * c_len, c_len)
        kc = k_ref[pl.ds(r0, c_len), :]
        kv_ref[c] = _tn((kc.astype(F32) * write_decay).astype(BF16), v_ref[pl.ds(r0, c_len), :])
        return carry

    lax.fori_loop(0, n_chunks, kv_body, 0, unroll=RET_UNROLL)

    state = jnp.zeros((LANES, LANES), F32)
    for c in range(n_chunks):
        st_ref[c] = jnp.where(blockdiag, state, 0.0).astype(BF16)
        state = state * chunk_decay + kv_ref[c]

    def out_body(step, carry):
        cs = [step * RET_UNROLL + u for u in range(RET_UNROLL)]
        rows = [pl.ds(pl.multiple_of(c * c_len, c_len), c_len) for c in cs]
        qs = [q_ref[r, :] for r in rows]
        ks = [k_ref[r, :] for r in rows]
        vs = [v_ref[r, :] for r in rows]
        zero = jnp.zeros_like(qs[0])
        sa = [(_nt(jnp.where(lo, q, zero), k) * intra_a).astype(BF16) for q, k in zip(qs, ks)]
        sb = [(_nt(jnp.where(hi, q, zero), k) * intra_b).astype(BF16) for q, k in zip(qs, ks)]
        cross = [_nn(q, st_ref[c]) * read_decay for q, c in zip(qs, cs)]
        os_ = [jnp.where(lo, _nn(a, v), _nn(b, v)) + x for a, b, v, x in zip(sa, sb, vs, cross)]
        ds_ = [o - half_mean(o) for o in os_]
        ys = [d * lax.rsqrt(half_mean(d * d) + NORM_EPS) * gn for d in ds_]
        for r, y in zip(rows, ys):
            o_ref[r, :] = (jax.nn.silu(gate_ref[r, :]) * y).astype(BF16)
        return carry

    lax.fori_loop(0, n_chunks // RET_UNROLL, out_body, 0)


def _retention(ret, rgate, gn, intra, rd, wd, cd, batch, seq):
    pair_const3 = lambda b, j: (j, 0, 0)
    return pl.pallas_call(
        _retention_kernel,
        grid=(batch, RET_PAIRS),
        in_specs=[
            pl.BlockSpec((seq, LANES), lambda b, j: (b, j)),
            pl.BlockSpec((seq, LANES), lambda b, j: (b, RET_PAIRS + j)),
            pl.BlockSpec((seq, LANES), lambda b, j: (b, 2 * RET_PAIRS + j)),
            pl.BlockSpec((seq, LANES), lambda b, j: (b, j)),
            pl.BlockSpec((1, 1, LANES), pair_const3),
            pl.BlockSpec((1, 2, RET_CHUNK, RET_CHUNK), lambda b, j: (j, 0, 0, 0)),
            pl.BlockSpec((1, RET_CHUNK, LANES), pair_const3),
            pl.BlockSpec((1, RET_CHUNK, LANES), pair_const3),
            pl.BlockSpec((1, 1, LANES), pair_const3),
        ],
        out_specs=pl.BlockSpec((seq, LANES), lambda b, j: (b, j)),
        out_shape=jax.ShapeDtypeStruct((batch * seq, RET_PAIRS * LANES), BF16),
        scratch_shapes=[pltpu.VMEM((seq // RET_CHUNK, LANES, LANES), F32),
                        pltpu.VMEM((seq // RET_CHUNK, LANES, LANES), BF16)],
        compiler_params=pltpu.CompilerParams(
            dimension_semantics=("parallel", "arbitrary"), vmem_limit_bytes=VMEM_LIMIT),
        name="retention",
    )(ret, ret, ret, rgate, gn, intra, rd, wd, cd)


def _out_mlp_kernel(x_ref, nsa_ref, mla_ref, ret_ref, wo_ref, g2_ref, wu_ref, wd_ref, gf_ref, o_ref,
                    *, final_norm):
    w = RET_PAIRS * LANES
    mixed = (_nn(nsa_ref[...], wo_ref[0:w, :]) + _nn(mla_ref[...], wo_ref[w:2 * w, :])
             + _nn(ret_ref[...], wo_ref[2 * w:3 * w, :]))
    x = x_ref[...] + mixed
    h = _rms(x, g2_ref[...]).astype(BF16)
    y = x
    for c in range(D_FF // MLP_FF_CHUNK):
        sl = slice(c * MLP_FF_CHUNK, (c + 1) * MLP_FF_CHUNK)
        u = jnp.maximum(_nn(h, wu_ref[:, sl]), 0.0)
        y = y + _nn((u * u).astype(BF16), wd_ref[sl, :])
    if final_norm:
        y = _rms(y, gf_ref[...])
    o_ref[...] = y


def _out_mlp(x2, o_nsa, o_mla, o_ret, wo, g2, wu, wd, layer, gf, final_norm):
    t = x2.shape[0]
    w = RET_PAIRS * LANES
    row = lambda i: (i, 0)
    const = lambda i: (0, 0)
    resident = dict(pipeline_mode=pl.Buffered(1))
    return pl.pallas_call(
        functools.partial(_out_mlp_kernel, final_norm=final_norm),
        grid=(t // MLP_TM,),
        in_specs=[
            pl.BlockSpec((MLP_TM, D_MODEL), row),
            pl.BlockSpec((MLP_TM, w), row),
            pl.BlockSpec((MLP_TM, w), row),
            pl.BlockSpec((MLP_TM, w), row),
            _layer_spec(wo, layer, **resident),
            pl.BlockSpec((1, D_MODEL), const),
            _layer_spec(wu, layer, **resident),
            _layer_spec(wd, layer, **resident),
            pl.BlockSpec((1, D_MODEL), const),
        ],
        out_specs=pl.BlockSpec((MLP_TM, D_MODEL), row),
        out_shape=jax.ShapeDtypeStruct((t, D_MODEL), F32),
        compiler_params=pltpu.CompilerParams(
            dimension_semantics=("parallel",), vmem_limit_bytes=VMEM_LIMIT),
        name="out_mlp",
    )(x2, o_nsa, o_mla, o_ret, wo, g2, wu, wd, gf)


ROPE_KINDS = (
    (PARTIAL_ROPE_DIM, ROPE_THETA, HEAD_DIM, 0),
    (MLA_ROPE_DIM, ROPE_THETA, LANES, HALF),
    (HEAD_DIM, RET_THETA, HEAD_DIM, 0),
)
ROPE_TM = 1024


def _rope_placement():
    place = np.zeros((LANES, 3 * LANES * len(ROPE_KINDS)), np.float32)
    fill = np.zeros((1, 3 * LANES * len(ROPE_KINDS)), np.float32)
    row0 = 0
    for kind, (dim, _, period, base) in enumerate(ROPE_KINDS):
        half = dim // 2
        col0 = kind * 3 * LANES
        for lane in range(LANES):
            rel = (lane - base) % period
            first = lane >= base and rel < half
            second = lane >= base and half <= rel < dim
            if first or second:
                angle = rel if first else rel - half
                place[row0 + angle, col0 + lane] = 1.0
                place[row0 + half + angle, col0 + (1 if first else 2) * LANES + lane] = -1.0 if first else 1.0
            else:
                fill[0, col0 + lane] = 1.0
        row0 += dim
    assert row0 <= LANES
    return jnp.asarray(place, BF16), jnp.asarray(fill)


def _rope_kernel(cs_ref, place_ref, fill_ref, *out_refs):
    x = cs_ref[...]
    x1 = x.astype(BF16)
    r1 = x - x1.astype(F32)
    x2 = r1.astype(BF16)
    x3 = (r1 - x2.astype(F32)).astype(BF16)
    place = place_ref[...]
    tab = _nn(x1, place) + _nn(x2, place) + _nn(x3, place) + fill_ref[...]
    for i, ref in enumerate(out_refs):
        ref[...] = tab[:, i * 3 * LANES:(i + 1) * 3 * LANES]


def _rope_tables(positions):
    pos = positions.reshape(-1).astype(F32)[:, None]
    parts = []
    for dim, theta, _, _ in ROPE_KINDS:
        inv = 1.0 / (theta ** (jnp.arange(0, dim, 2, dtype=F32) / dim))
        ang = pos * inv
        parts += [jnp.cos(ang), jnp.sin(ang)]
    compact = jnp.concatenate(parts, axis=1)
    compact = jnp.pad(compact, ((0, 0), (0, LANES - compact.shape[1])))
    place, fill = _rope_placement()
    t = compact.shape[0]
    tab_shape = jax.ShapeDtypeStruct((t, 3 * LANES), F32)
    return pl.pallas_call(
        _rope_kernel,
        grid=(t // ROPE_TM,),
        in_specs=[pl.BlockSpec((ROPE_TM, LANES), lambda i: (i, 0)),
                  pl.BlockSpec(place.shape, lambda i: (0, 0)),
                  pl.BlockSpec(fill.shape, lambda i: (0, 0))],
        out_specs=tuple(pl.BlockSpec((ROPE_TM, 3 * LANES), lambda i: (i, 0)) for _ in ROPE_KINDS),
        out_shape=tuple(tab_shape for _ in ROPE_KINDS),
        compiler_params=pltpu.CompilerParams(
            dimension_semantics=("parallel",), vmem_limit_bytes=VMEM_LIMIT),
        name="rope_tables",
    )(compact, place, fill)


def _pad_cols(w, n):
    return jnp.pad(w, ((0, 0), (0, 0), (0, n - w.shape[-1])))


def _in_weight(w_in):
    offs = np.cumsum((0,) + IN_SIZES)
    seg = [w_in[:, :, offs[i]:offs[i + 1]] for i in range(len(IN_SIZES))]
    (nsa_q, k_cmp, v_cmp, k_slc, v_slc, k_win, v_win, gate,
     cq, ckv, kpe, ret_q, ret_k, ret_v, ret_g) = seg
    scale = HEAD_DIM ** -0.5
    qh = [nsa_q[:, :, h * HEAD_DIM:(h + 1) * HEAD_DIM] * scale for h in range(NSA_HEADS)]
    tiles = [jnp.concatenate([qh[j], qh[j + NSA_REP]], axis=-1) for j in range(NSA_REP)]
    tiles += [k_slc, k_win, k_cmp, v_cmp]
    tiles += [cq, ckv, jnp.pad(kpe, ((0, 0), (0, 0), (HALF, LANES - HALF - MLA_ROPE_DIM)))]
    ret_w = RET_PAIRS * LANES
    tiles += [_pad_cols(ret_q, ret_w), _pad_cols(ret_k * scale, ret_w), _pad_cols(ret_v, ret_w),
              _pad_cols(ret_g, ret_w)]
    w_t = jnp.concatenate([v_slc, v_win, _pad_cols(gate, GATE_ROWS)], axis=-1).transpose(0, 2, 1)
    return jnp.concatenate(tiles, axis=-1).astype(BF16), w_t.astype(BF16)


def _compress_weights(pos, w1, w2):
    nl = pos.shape[0]
    g, dh, hid = NSA_KV_GROUPS, HEAD_DIM, CMP_HIDDEN
    p = jnp.tile(pos, (1, 1, g))
    eye = jnp.eye(g, dtype=w1.dtype)
    w = w1.reshape(nl, CMP_LEN, dh, hid)
    w = jnp.einsum('ltdc,gk->ltgdkc', w, eye).reshape(nl, CMP_LEN, g * dh, g * hid)
    w2b = jnp.einsum('lcd,gk->lgckd', w2, eye).reshape(nl, g * hid, g * dh)
    return p, w.astype(BF16), w2b.astype(BF16)


def _mla_weights(w_uq, w_ukv):
    nl = w_uq.shape[0]
    dq = MLA_NOPE_DIM + MLA_ROPE_DIM
    wq = w_uq.reshape(nl, MLA_Q_RANK, MLA_HEADS, dq)
    wq = jnp.pad(wq, ((0, 0), (0, 0), (0, 0), (0, LANES - dq))).reshape(nl, MLA_Q_RANK, MLA_HEADS * LANES)
    wkv = w_ukv.reshape(nl, MLA_KV_RANK, MLA_HEADS, MLA_NOPE_DIM + MLA_V_DIM)
    wk = jnp.pad(wkv[..., :MLA_NOPE_DIM], ((0, 0), (0, 0), (0, 0), (0, LANES - MLA_NOPE_DIM)))
    wk = wk.reshape(nl, MLA_KV_RANK, MLA_HEADS * LANES)
    wvt = wkv[..., MLA_NOPE_DIM:].reshape(nl, MLA_KV_RANK, MLA_V_ROWS).transpose(0, 2, 1)
    return wq.astype(BF16), wk.astype(BF16), wvt.astype(BF16)


def _out_weight(w_out):
    nl = w_out.shape[0]
    pad_rows = lambda w: jnp.pad(w, ((0, 0), (0, RET_PAIRS * LANES - w.shape[1]), (0, 0)))
    nsa = w_out[:, :NSA_Q_W].reshape(nl, NSA_HEADS, HEAD_DIM, D_MODEL)
    order = [h for j in range(NSA_REP) for h in (j, j + NSA_REP)]
    nsa = nsa[:, order].reshape(nl, NSA_Q_W, D_MODEL)
    mla = pad_rows(w_out[:, NSA_Q_W:NSA_Q_W + MLA_HEADS * MLA_V_DIM])
    ret = pad_rows(w_out[:, NSA_Q_W + MLA_HEADS * MLA_V_DIM:])
    return jnp.concatenate([nsa, mla, ret], axis=1).astype(BF16)


def _retention_tables(gn_gain):
    nh = 2 * RET_PAIRS
    log_g = jnp.log(1.0 - 2.0 ** (-5.0 - jnp.arange(nh, dtype=F32)))
    i = jnp.arange(RET_CHUNK, dtype=F32)
    diff = i[:, None] - i[None, :]
    intra = jnp.where(diff >= 0, jnp.exp(jnp.maximum(diff, 0.0)[None] * log_g[:, None, None]), 0.0)
    read_decay = jnp.exp((i + 1.0)[None, :] * log_g[:, None])
    write_decay = jnp.exp((RET_CHUNK - 1.0 - i)[None, :] * log_g[:, None])
    chunk_decay = jnp.exp(RET_CHUNK * log_g)

    def lanes(t):
        t = t.reshape(RET_PAIRS, 2, -1)
        return jnp.repeat(t.transpose(0, 2, 1), HALF, axis=-1)

    gn = jnp.pad(gn_gain, ((0, 0), (0, nh - RET_HEADS), (0, 0)))
    gn = gn.reshape(gn.shape[0], RET_PAIRS, 1, LANES)
    return (gn, intra.reshape(RET_PAIRS, 2, RET_CHUNK, RET_CHUNK), lanes(read_decay), lanes(write_decay),
            lanes(chunk_decay[:, None]))


def _selection_overlap(seq):
    n_cmp = (seq - CMP_LEN) // CMP_STRIDE + 1
    n_sel = seq // SEL_BLOCK
    cs = np.arange(n_cmp) * CMP_STRIDE
    ss = np.arange(n_sel) * SEL_BLOCK
    ov = np.clip(np.minimum(cs[:, None] + CMP_LEN, ss[None, :] + SEL_BLOCK)
                 - np.maximum(cs[:, None], ss[None, :]), 0, None) / CMP_LEN
    ovl_t = np.zeros((n_sel, seq // CMP_STRIDE), np.float32)
    ovl_t[:, :n_cmp] = ov.T
    return jnp.asarray(ovl_t, BF16)


def kernel(x, positions, ln1_gain, w_in, cmp_pos_k, cmp_w1_k, cmp_w2_k, cmp_pos_v, cmp_w1_v, cmp_w2_v,
           mla_q_norm, mla_w_uq, mla_kv_norm, mla_w_ukv, ret_gn_gain, w_out, ln2_gain, w_up, w_down,
           final_gain):
    batch, seq, _ = x.shape
    depth = w_in.shape[0]
    t = batch * seq

    tab_n, tab_m, tab_r = _rope_tables(positions)

    w_in_p, w_in_t = _in_weight(w_in)
    pos_k, w1_k, w2_k = _compress_weights(cmp_pos_k, cmp_w1_k, cmp_w2_k)
    pos_v, w1_v, w2_v = _compress_weights(cmp_pos_v, cmp_w1_v, cmp_w2_v)
    cmp_pos = jnp.stack([pos_k, pos_v], axis=1)
    cmp_w1 = jnp.stack([w1_k, w1_v], axis=1)
    w2_vt = w2_v.transpose(0, 2, 1)
    wq, wk, wvt = _mla_weights(mla_w_uq, mla_w_ukv)
    wo = _out_weight(w_out)
    wu = w_up.astype(BF16)
    wd = w_down.astype(BF16)
    gn, intra, rd, wdec, cd = _retention_tables(ret_gn_gain)
    ovl_t = _selection_overlap(seq)
    gf = final_gain.reshape(1, D_MODEL)

    x2 = x.reshape(t, D_MODEL)
    for l in range(depth):
        nsa, nsa_t, gate_t, k_cmp, v_cmp, mla_in, ret, rgate = _in_proj(
            x2, ln1_gain[l].reshape(1, D_MODEL), w_in_p, w_in_t, l, tab_n, tab_m, tab_r)
        kc, vct = _compress(k_cmp, v_cmp, cmp_pos[l], cmp_w1, w2_k[l], w2_vt[l], l, batch, seq)
        o_nsa = _nsa_attention(nsa, nsa_t, gate_t, kc, vct, ovl_t, batch, seq)
        q_m, k_m, vt_m = _mla_up(mla_in, mla_q_norm[l].reshape(1, -1), mla_kv_norm[l].reshape(1, -1),
                                 wq, wk, wvt, l, tab_m)
        o_mla = _mla_attention(q_m, k_m, vt_m, batch, seq)
        o_ret = _retention(ret, rgate, gn[l], intra, rd, wdec, cd, batch, seq)
        x2 = _out_mlp(x2, o_nsa, o_mla, o_ret, wo, ln2_gain[l].reshape(1, D_MODEL), wu, wd, l, gf,
                      final_norm=(l == depth - 1))
    return x2.reshape(batch, seq, D_MODEL)
```

```python
import functools
import math

import numpy as np
import jax
import jax.numpy as jnp
from jax import lax
from jax.experimental import pallas as pl
from jax.experimental.pallas import tpu as pltpu

F32 = jnp.float32
BF16 = jnp.bfloat16

D_MODEL = 1024
HEAD_DIM = 64
NSA_HEADS = 6
NSA_KV_GROUPS = 2
NSA_REP = NSA_HEADS // NSA_KV_GROUPS
N_BRANCH = 3
CMP_LEN = 32
CMP_STRIDE = 16
CMP_HIDDEN = 2 * HEAD_DIM
SEL_BLOCK = 64
SEL_TOP_N = 16
WINDOW = 512
MLA_HEADS = 5
MLA_Q_RANK = 256
MLA_KV_RANK = 128
MLA_NOPE_DIM = 64
MLA_ROPE_DIM = 32
MLA_V_DIM = 64
RET_HEADS = 5
RET_CHUNK = 128
ROPE_THETA = 500000.0
PARTIAL_ROPE_DIM = HEAD_DIM // 4
RET_THETA = 10000.0
D_FF = 4 * D_MODEL
NORM_EPS = 1e-6
NEG_INF = -1e30
FORCE_SCORE = 1e9
LOG2E = math.log2(math.e)

NSA_Q_W = NSA_HEADS * HEAD_DIM
NSA_KV_W = NSA_KV_GROUPS * HEAD_DIM
NSA_GATE_W = NSA_HEADS * N_BRANCH
RET_W = RET_HEADS * HEAD_DIM
IN_SIZES = (NSA_Q_W, NSA_KV_W, NSA_KV_W, NSA_KV_W, NSA_KV_W, NSA_KV_W, NSA_KV_W, NSA_GATE_W,
            MLA_Q_RANK, MLA_KV_RANK, MLA_ROPE_DIM, RET_W, RET_W, RET_W, RET_W)

LANES = 128
HALF = LANES // 2
VMEM_LIMIT = 56 * 1024 * 1024

NSA_TILES = 5
GATE_ROWS = 32
NSA_T_ROWS = 2 * LANES + GATE_ROWS
MLA_IN_TILES = 4
RET_PAIRS = 3
RET_TILES = 3 * RET_PAIRS
IN_TILES = NSA_TILES + 2 + MLA_IN_TILES + RET_TILES + RET_PAIRS
N_PAD = IN_TILES * LANES
MLA_V_ROWS = MLA_HEADS * MLA_V_DIM

IN_TM = 512
MLP_TM = 512
MLA_TM = 512
ATT_T = 512
MLP_FF_CHUNK = 1024


def _nn(a, b):
    return jnp.dot(a, b, preferred_element_type=F32)


def _nt(a, b):
    return lax.dot_general(a, b, (((1,), (1,)), ((), ())), preferred_element_type=F32)


def _rms(x, gain):
    return x * lax.rsqrt(jnp.mean(x * x, axis=-1, keepdims=True) + NORM_EPS) * gain


def _rope(val, tab, half):
    cos = tab[:, 0:LANES]
    sin_a = tab[:, LANES:2 * LANES]
    sin_b = tab[:, 2 * LANES:3 * LANES]
    return (val * cos + pltpu.roll(val, LANES - half, 1) * sin_a
            + pltpu.roll(val, half, 1) * sin_b)


def _lane_lo(shape):
    return lax.broadcasted_iota(jnp.int32, shape, len(shape) - 1) < HALF


def _inproj_kernel(x_ref, g_ref, w_ref, wt_ref, tn_ref, tm_ref, tr_ref,
                   nsa_ref, nsat_ref, gatet_ref, kcmp_ref, vcmp_ref, mla_ref, ret_ref, rgate_ref):
    h = _rms(x_ref[...], g_ref[...]).astype(BF16)
    tab_n = tn_ref[...]
    tab_r = tr_ref[...]

    def tile(a, i):
        return a[:, i * LANES:(i + 1) * LANES]

    c0 = 0
    a = _nn(h, w_ref[:, c0:c0 + NSA_TILES * LANES])
    for i in range(NSA_TILES):
        v = _rope(tile(a, i), tab_n, PARTIAL_ROPE_DIM // 2)
        if i < NSA_REP:
            v = v * LOG2E
        nsa_ref[:, i * LANES:(i + 1) * LANES] = v.astype(BF16)
    c0 += NSA_TILES * LANES
    a = _nn(h, w_ref[:, c0:c0 + 2 * LANES])
    kcmp_ref[...] = _rope(tile(a, 0), tab_n, PARTIAL_ROPE_DIM // 2)
    vcmp_ref[...] = tile(a, 1)
    c0 += 2 * LANES
    a = _nn(h, w_ref[:, c0:c0 + MLA_IN_TILES * LANES])
    mla_ref[:, 0:3 * LANES] = a[:, 0:3 * LANES]
    mla_ref[:, 3 * LANES:4 * LANES] = _rope(tile(a, 3), tm_ref[...], MLA_ROPE_DIM // 2)
    c0 += MLA_IN_TILES * LANES
    a = _nn(h, w_ref[:, c0:c0 + RET_TILES * LANES])
    for i in range(RET_TILES):
        v = tile(a, i)
        if i < 2 * RET_PAIRS:
            v = _rope(v, tab_r, HEAD_DIM // 2)
        ret_ref[:, i * LANES:(i + 1) * LANES] = v.astype(BF16)
    c0 += RET_TILES * LANES
    rgate_ref[...] = _nn(h, w_ref[:, c0:c0 + RET_PAIRS * LANES])
    at = _nt(wt_ref[...], h)
    nsat_ref[...] = at[0:2 * LANES, :].astype(BF16)
    gatet_ref[...] = at[2 * LANES:NSA_T_ROWS, :]


def _layer_spec(w, layer, **kwargs):
    zeros = (0,) * (w.ndim - 1)
    return pl.BlockSpec((None,) + w.shape[1:], lambda *_: (layer,) + zeros, **kwargs)


def _in_proj(x2, gain, w, wt, layer, tab_n, tab_m, tab_r):
    t = x2.shape[0]
    row = lambda i: (i, 0)
    col = lambda i: (0, i)
    const = lambda i: (0, 0)
    out_shapes = (
        jax.ShapeDtypeStruct((t, NSA_TILES * LANES), BF16),
        jax.ShapeDtypeStruct((2 * LANES, t), BF16),
        jax.ShapeDtypeStruct((GATE_ROWS, t), F32),
        jax.ShapeDtypeStruct((t, LANES), F32),
        jax.ShapeDtypeStruct((t, LANES), F32),
        jax.ShapeDtypeStruct((t, MLA_IN_TILES * LANES), F32),
        jax.ShapeDtypeStruct((t, RET_TILES * LANES), BF16),
        jax.ShapeDtypeStruct((t, RET_PAIRS * LANES), F32),
    )
    out_specs = tuple(
        pl.BlockSpec((s.shape[0], IN_TM), col) if s.shape[1] == t else pl.BlockSpec((IN_TM, s.shape[1]), row)
        for s in out_shapes)
    return pl.pallas_call(
        _inproj_kernel,
        grid=(t // IN_TM,),
        in_specs=[
            pl.BlockSpec((IN_TM, D_MODEL), row),
            pl.BlockSpec((1, D_MODEL), const),
            _layer_spec(w, layer),
            _layer_spec(wt, layer),
            pl.BlockSpec((IN_TM, 3 * LANES), row),
            pl.BlockSpec((IN_TM, 3 * LANES), row),
            pl.BlockSpec((IN_TM, 3 * LANES), row),
        ],
        out_specs=out_specs,
        out_shape=out_shapes,
        compiler_params=pltpu.CompilerParams(
            dimension_semantics=("parallel",), vmem_limit_bytes=VMEM_LIMIT),
        name="in_proj",
    )(x2, gain, w, wt, tab_n, tab_m, tab_r)


def _compress_kernel(k_ref, v_ref, pos_ref, w1_ref, w2k_ref, w2vt_ref, kc_ref, vct_ref):
    n_blk = k_ref.shape[0] // CMP_STRIDE

    def hidden(src, i):
        lo = jnp.zeros((n_blk, NSA_KV_GROUPS * CMP_HIDDEN), F32)
        hi = jnp.zeros((n_blk, NSA_KV_GROUPS * CMP_HIDDEN), F32)
        for r in range(CMP_STRIDE):
            tok = src[pl.ds(r, n_blk, stride=CMP_STRIDE), :]
            lo = lo + _nn((tok + pos_ref[i, r:r + 1, :]).astype(BF16), w1_ref[i, r])
            r2 = CMP_STRIDE + r
            hi = hi + _nn((tok + pos_ref[i, r2:r2 + 1, :]).astype(BF16), w1_ref[i, r2])
        return jax.nn.gelu(lo + pltpu.roll(hi, n_blk - 1, 0)).astype(BF16)

    kc_ref[0] = _nn(hidden(k_ref, 0), w2k_ref[...]).astype(BF16)
    vct_ref[0] = _nt(w2vt_ref[...], hidden(v_ref, 1)).astype(BF16)


def _compress(k_cmp, v_cmp, pos, w1, w2k, w2vt, layer, batch, seq):
    b = batch
    n_blk = seq // CMP_STRIDE
    return pl.pallas_call(
        _compress_kernel,
        grid=(b,),
        in_specs=[
            pl.BlockSpec((seq, LANES), lambda i: (i, 0)),
            pl.BlockSpec((seq, LANES), lambda i: (i, 0)),
            pl.BlockSpec(pos.shape, lambda i: (0, 0, 0)),
            _layer_spec(w1, layer),
            pl.BlockSpec(w2k.shape, lambda i: (0, 0)),
            pl.BlockSpec(w2vt.shape, lambda i: (0, 0)),
        ],
        out_specs=(pl.BlockSpec((1, n_blk, LANES), lambda i: (i, 0, 0)),
                   pl.BlockSpec((1, LANES, n_blk), lambda i: (i, 0, 0))),
        out_shape=(jax.ShapeDtypeStruct((b, n_blk, LANES), BF16),
                   jax.ShapeDtypeStruct((b, LANES, n_blk), BF16)),
        compiler_params=pltpu.CompilerParams(
            dimension_semantics=("parallel",), vmem_limit_bytes=VMEM_LIMIT),
        name="nsa_compress",
    )(k_cmp, v_cmp, pos, w1, w2k, w2vt)


SCORE_LOOKAHEAD = 2
ONES_ROWS = 16


def _softmax_steps(score_fns, v_ts, states):
    n = len(score_fns)
    s_ts = [score_fns[h]() if h < SCORE_LOOKAHEAD else None for h in range(n)]
    out = []
    for h in range(n):
        if h + SCORE_LOOKAHEAD < n:
            s_ts[h + SCORE_LOOKAHEAD] = score_fns[h + SCORE_LOOKAHEAD]()
        m_old, acc_old = states[h]
        m = jnp.maximum(m_old, jnp.max(s_ts[h], axis=0, keepdims=True))
        p = jnp.exp2(s_ts[h] - m).astype(BF16)
        s_ts[h] = None
        v_ext = jnp.concatenate([v_ts[h], jnp.ones((ONES_ROWS, v_ts[h].shape[1]), BF16)], axis=0)
        out.append((m, jnp.exp2(m_old - m) * acc_old + _nn(v_ext, p)))
    return tuple(out)


def _softmax_init(cols):
    return jnp.full((1, cols), NEG_INF, F32), jnp.zeros((HEAD_DIM + ONES_ROWS, cols), F32)


def _softmax_finish(state):
    acc = state[1]
    return acc[0:HEAD_DIM] * (1.0 / jnp.maximum(acc[HEAD_DIM:HEAD_DIM + 1], 1e-30))


def _nsa_kernel(q_ref, gatet_ref, kc_ref, vct_ref, ks_ref, kw_ref, vst_ref, vwt_ref, ovl_ref,
                o_ref, out_t_ref, selb_ref):
    tq = ATT_T
    qi = pl.program_id(1)
    q0 = pl.multiple_of(qi * tq, tq)
    lo1 = _lane_lo((1, LANES))
    group_lanes = (lo1, jnp.logical_not(lo1))
    gate = jax.nn.sigmoid(gatet_ref[...])
    heads = range(NSA_HEADS)

    def q_head(h):
        j, g = h % NSA_REP, h // NSA_REP
        tile = q_ref[:, j * LANES:(j + 1) * LANES]
        return jnp.where(group_lanes[g], tile, jnp.zeros_like(tile))

    def v_rows(ref, h, k0, n):
        g = h // NSA_REP
        return ref[g * HEAD_DIM:(g + 1) * HEAD_DIM, pl.ds(k0, n)]

    def emit(h, branch, qs, nq, o_t):
        j, g = h % NSA_REP, h // NSA_REP
        r0 = j * LANES + g * HEAD_DIM
        row = h * N_BRANCH + branch
        val = gate[row:row + 1, qs:qs + nq] * o_t
        if branch == 0:
            out_t_ref[r0:r0 + HEAD_DIM, qs:qs + nq] = val
        else:
            out_t_ref[r0:r0 + HEAD_DIM, qs:qs + nq] += val

    qm = [q_head(h) for h in heads]

    n_cmp_pad = kc_ref.shape[1]
    kc = kc_ref[0]
    vct = vct_ref[0]
    n_i = lax.broadcasted_iota(jnp.int32, (n_cmp_pad, tq), 0)
    t_l = q0 + lax.broadcasted_iota(jnp.int32, (n_cmp_pad, tq), 1)
    cmask = (n_i * CMP_STRIDE + (CMP_LEN - 1)) <= t_l
    cmask_f = cmask.astype(F32)
    psum = [None] * NSA_KV_GROUPS
    for h in heads:
        g = h // NSA_REP
        s_t = jnp.where(cmask, _nt(kc, qm[h]), NEG_INF)
        m = jnp.max(s_t, axis=0, keepdims=True)
        p = jnp.exp2(s_t - m) * cmask_f
        p = p * (1.0 / jnp.maximum(jnp.sum(p, axis=0, keepdims=True), 1e-30))
        psum[g] = p if psum[g] is None else psum[g] + p
        emit(h, 0, 0, tq, _nn(vct[g * HEAD_DIM:(g + 1) * HEAD_DIM, :], p.astype(BF16)))

    n_sel = ovl_ref.shape[0]
    m_i = lax.broadcasted_iota(jnp.int32, (n_sel, tq), 0)
    cur = jnp.right_shift(q0 + lax.broadcasted_iota(jnp.int32, (n_sel, tq), 1), SEL_BLOCK.bit_length() - 1)
    valid = m_i <= cur
    forced = (m_i == 0) | (m_i == cur) | (m_i == cur - 1)
    ovl = ovl_ref[...]
    for g in range(NSA_KV_GROUPS):
        p_hi = psum[g].astype(BF16)
        p_lo = (psum[g] - p_hi.astype(F32)).astype(BF16)
        imp = _nn(ovl, p_hi) + _nn(ovl, p_lo)
        imp = jnp.where(valid & forced, FORCE_SCORE, imp)
        imp = jnp.where(valid, imp, NEG_INF)
        rank = jnp.zeros((n_sel, tq), jnp.int32)
        for mp in range(n_sel):
            row = imp[mp:mp + 1, :]
            beats = (row > imp) | ((row == imp) & (m_i > mp))
            rank = rank + beats.astype(jnp.int32)
        selb_ref[g] = jnp.where(rank < SEL_TOP_N, 0.0, NEG_INF)

    blocks_per_chunk = tq // SEL_BLOCK

    def sel_bias(g, c, n_keys, qs, nq):
        rows = [jnp.broadcast_to(selb_ref[g, pl.ds(c * blocks_per_chunk + i, 1), qs:qs + nq], (SEL_BLOCK, nq))
                for i in range(n_keys // SEL_BLOCK)]
        return jnp.concatenate(rows, axis=0)

    def sel_chunk(c, k0, n_keys, qs, nq, states, causal):
        k = ks_ref[pl.ds(k0, n_keys), :]
        bias = [sel_bias(g, c, n_keys, qs, nq) for g in range(NSA_KV_GROUPS)]
        if causal:
            mask = (lax.broadcasted_iota(jnp.int32, (n_keys, nq), 0)
                    <= qs + lax.broadcasted_iota(jnp.int32, (n_keys, nq), 1))

        def score_fn(h):
            s_t = _nt(k, qm[h][qs:qs + nq]) + bias[h // NSA_REP]
            return jnp.where(mask, s_t, NEG_INF) if causal else s_t

        return _softmax_steps([functools.partial(score_fn, h) for h in heads],
                              [v_rows(vst_ref, h, k0, n_keys) for h in heads], states)

    states = sel_chunk(qi, q0, tq, 0, tq, tuple(_softmax_init(tq) for _ in heads), True)
    states = lax.fori_loop(
        0, qi, lambda c, st: sel_chunk(c, pl.multiple_of(c * tq, tq), tq, 0, tq, st, False), states)
    for h in heads:
        emit(h, 1, 0, tq, _softmax_finish(states[h]))

    key_i = lax.broadcasted_iota(jnp.int32, (tq, tq), 0)
    qry_i = lax.broadcasted_iota(jnp.int32, (tq, tq), 1)

    def win_chunk(k0, states, mask):
        k = kw_ref[pl.ds(k0, tq), :]
        score_fns = [lambda h=h: jnp.where(mask, _nt(k, qm[h]), NEG_INF) for h in heads]
        return _softmax_steps(score_fns, [v_rows(vwt_ref, h, k0, tq) for h in heads], states)

    states = win_chunk(q0, tuple(_softmax_init(tq) for _ in heads), key_i <= qry_i)
    for d in range(1, WINDOW // tq + 1):
        k0 = pl.multiple_of(jnp.maximum(qi - d, 0) * tq, tq)
        in_band = (key_i - d * tq > qry_i - WINDOW) & (qi >= d)
        states = win_chunk(k0, states, in_band)
    for h in heads:
        emit(h, 2, 0, tq, _softmax_finish(states[h]))

    for j in range(NSA_REP):
        o_ref[:, j * LANES:(j + 1) * LANES] = out_t_ref[j * LANES:(j + 1) * LANES, :].T.astype(BF16)


def _nsa_attention(nsa, nsa_t, gate_t, kc, vct, ovl_t, batch, seq):
    tq = ATT_T
    nq = seq // tq
    n_blk = kc.shape[1]
    qrow = lambda b, i: (b * nq + i, 0)
    return pl.pallas_call(
        _nsa_kernel,
        grid=(batch, nq),
        in_specs=[
            pl.BlockSpec((tq, NSA_REP * LANES), qrow),
            pl.BlockSpec((GATE_ROWS, tq), lambda b, i: (0, b * nq + i)),
            pl.BlockSpec((1, n_blk, LANES), lambda b, i: (b, 0, 0)),
            pl.BlockSpec((1, LANES, n_blk), lambda b, i: (b, 0, 0)),
            pl.BlockSpec((seq, LANES), lambda b, i: (b, 3)),
            pl.BlockSpec((seq, LANES), lambda b, i: (b, 4)),
            pl.BlockSpec((LANES, seq), lambda b, i: (0, b)),
            pl.BlockSpec((LANES, seq), lambda b, i: (1, b)),
            pl.BlockSpec(ovl_t.shape, lambda b, i: (0, 0)),
        ],
        out_specs=pl.BlockSpec((tq, NSA_REP * LANES), qrow),
        out_shape=jax.ShapeDtypeStruct((batch * seq, NSA_REP * LANES), BF16),
        scratch_shapes=[pltpu.VMEM((NSA_REP * LANES, tq), F32),
                        pltpu.VMEM((NSA_KV_GROUPS, seq // SEL_BLOCK, tq), F32)],
        compiler_params=pltpu.CompilerParams(
            dimension_semantics=("parallel", "arbitrary"), vmem_limit_bytes=VMEM_LIMIT),
        name="nsa_attention",
    )(nsa, gate_t, kc, vct, nsa, nsa, nsa_t, nsa_t, ovl_t)


def _mla_up_kernel(in_ref, qn_ref, kvn_ref, wq_ref, wk_ref, wvt_ref, tab_ref, q_ref, k_ref, vt_ref):
    scale = (MLA_NOPE_DIM + MLA_ROPE_DIM) ** -0.5 * LOG2E
    cq = _rms(in_ref[:, 0:MLA_Q_RANK], qn_ref[...]).astype(BF16)
    ckv = _rms(in_ref[:, MLA_Q_RANK:MLA_Q_RANK + MLA_KV_RANK], kvn_ref[...]).astype(BF16)
    k_pe = in_ref[:, 3 * LANES:4 * LANES]
    cos = tab_ref[:, 0:LANES] * scale
    sin = (tab_ref[:, 2 * LANES:3 * LANES] - tab_ref[:, LANES:2 * LANES]) * scale
    q = _nn(cq, wq_ref[...])
    k = _nn(ckv, wk_ref[...])
    w = MLA_HEADS * LANES
    for h in range(MLA_HEADS):
        sl = slice(h * LANES, (h + 1) * LANES)
        rot = slice(w + h * LANES, w + (h + 1) * LANES)
        q_ref[:, sl] = (q[:, sl] * cos + q[:, rot] * sin).astype(BF16)
        k_ref[:, sl] = (k[:, sl] + k_pe).astype(BF16)
    vt_ref[...] = _nt(wvt_ref[...], ckv).astype(BF16)


def _mla_up(mla_in, q_norm, kv_norm, wq, wk, wvt, layer, tab_m):
    t = mla_in.shape[0]
    row = lambda i: (i, 0)
    const = lambda i: (0, 0)
    return pl.pallas_call(
        _mla_up_kernel,
        grid=(t // MLA_TM,),
        in_specs=[
            pl.BlockSpec((MLA_TM, MLA_IN_TILES * LANES), row),
            pl.BlockSpec((1, MLA_Q_RANK), const),
            pl.BlockSpec((1, MLA_KV_RANK), const),
            _layer_spec(wq, layer),
            _layer_spec(wk, layer),
            _layer_spec(wvt, layer),
            pl.BlockSpec((MLA_TM, 3 * LANES), row),
        ],
        out_specs=(pl.BlockSpec((MLA_TM, MLA_HEADS * LANES), row),
                   pl.BlockSpec((MLA_TM, MLA_HEADS * LANES), row),
                   pl.BlockSpec((MLA_V_ROWS, MLA_TM), lambda i: (0, i))),
        out_shape=(jax.ShapeDtypeStruct((t, MLA_HEADS * LANES), BF16),
                   jax.ShapeDtypeStruct((t, MLA_HEADS * LANES), BF16),
                   jax.ShapeDtypeStruct((MLA_V_ROWS, t), BF16)),
        compiler_params=pltpu.CompilerParams(
            dimension_semantics=("parallel",), vmem_limit_bytes=VMEM_LIMIT),
        name="mla_up",
    )(mla_in, q_norm, kv_norm, wq, wk, wvt, tab_m)


def _mla_attn_kernel(q_ref, k_ref, vt_ref, o_ref, out_t_ref):
    tq = ATT_T
    qi = pl.program_id(1)
    q0 = pl.multiple_of(qi * tq, tq)
    heads = range(MLA_HEADS)
    qh = [q_ref[:, h * LANES:(h + 1) * LANES] for h in heads]

    def chunk(k0, n_keys, qs, nq, states, causal):
        if causal:
            mask = (lax.broadcasted_iota(jnp.int32, (n_keys, nq), 0)
                    <= qs + lax.broadcasted_iota(jnp.int32, (n_keys, nq), 1))

        def score_fn(h):
            s_t = _nt(k_ref[pl.ds(k0, n_keys), h * LANES:(h + 1) * LANES], qh[h][qs:qs + nq])
            return jnp.where(mask, s_t, NEG_INF) if causal else s_t

        v_ts = [vt_ref[h * MLA_V_DIM:(h + 1) * MLA_V_DIM, pl.ds(k0, n_keys)] for h in heads]
        return _softmax_steps([functools.partial(score_fn, h) for h in heads], v_ts, states)

    states = chunk(q0, tq, 0, tq, tuple(_softmax_init(tq) for _ in heads), True)
    states = lax.fori_loop(
        0, qi, lambda c, st: chunk(pl.multiple_of(c * tq, tq), tq, 0, tq, st, False), states)
    for h in heads:
        out_t_ref[h * MLA_V_DIM:(h + 1) * MLA_V_DIM, :] = _softmax_finish(states[h])
    out_t_ref[MLA_V_ROWS:, :] = jnp.zeros((RET_PAIRS * LANES - MLA_V_ROWS, tq), F32)
    for j in range(RET_PAIRS):
        o_ref[:, j * LANES:(j + 1) * LANES] = out_t_ref[j * LANES:(j + 1) * LANES, :].T.astype(BF16)


def _mla_attention(q, k, vt, batch, seq):
    tq = ATT_T
    nq = seq // tq
    return pl.pallas_call(
        _mla_attn_kernel,
        grid=(batch, nq),
        in_specs=[
            pl.BlockSpec((tq, MLA_HEADS * LANES), lambda b, i: (b * nq + i, 0)),
            pl.BlockSpec((seq, MLA_HEADS * LANES), lambda b, i: (b, 0)),
            pl.BlockSpec((MLA_V_ROWS, seq), lambda b, i: (0, b)),
        ],
        out_specs=pl.BlockSpec((tq, RET_PAIRS * LANES), lambda b, i: (b * nq + i, 0)),
        out_shape=jax.ShapeDtypeStruct((batch * seq, RET_PAIRS * LANES), BF16),
        scratch_shapes=[pltpu.VMEM((RET_PAIRS * LANES, tq), F32)],
        compiler_params=pltpu.CompilerParams(
            dimension_semantics=("parallel", "arbitrary"), vmem_limit_bytes=VMEM_LIMIT),
        name="mla_attention",
    )(q, k, vt)


def _tn(a, b):
    return lax.dot_general(a, b, (((0,), (0,)), ((), ())), preferred_element_type=F32)


RET_UNROLL = 16


def _retention_kernel(q_ref, k_ref, v_ref, gate_ref, gn_ref, intra_ref, rd_ref, wd_ref, cd_ref, o_ref,
                      kv_ref, st_ref):
    c_len = RET_CHUNK
    n_chunks = q_ref.shape[0] // c_len
    lo = _lane_lo((1, LANES))
    hi = jnp.logical_not(lo)
    blockdiag = (lax.broadcasted_iota(jnp.int32, (LANES, LANES), 0) < HALF) == _lane_lo((LANES, LANES))
    intra_a = intra_ref[0, 0]
    intra_b = intra_ref[0, 1]
    read_decay = rd_ref[0]
    write_decay = wd_ref[0]
    chunk_decay = cd_ref[0]
    gn = gn_ref[0]

    def half_mean(x):
        s_lo = jnp.sum(jnp.where(lo, x, 0.0), axis=-1, keepdims=True)
        s_hi = jnp.sum(jnp.where(hi, x, 0.0), axis=-1, keepdims=True)
        return jnp.where(lo, s_lo, s_hi) * (1.0 / HEAD_DIM)

    def kv_body(c, carry):
        r0 = pl.multiple_of(c * c_len, c_len)
        kc = k_ref[pl.ds(r0, c_len), :]
        kv_ref[c] = _tn((kc.astype(F32) * write_decay).astype(BF16), v_ref[pl.ds(r0, c_len), :])
        return carry

    lax.fori_loop(0, n_chunks, kv_body, 0, unroll=RET_UNROLL)

    state = jnp.zeros((LANES, LANES), F32)
    for c in range(n_chunks):
        st_ref[c] = jnp.where(blockdiag, state, 0.0).astype(BF16)
        state = state * chunk_decay + kv_ref[c]

    def out_body(step, carry):
        cs = [step * RET_UNROLL + u for u in range(RET_UNROLL)]
        rows = [pl.ds(pl.multiple_of(c * c_len, c_len), c_len) for c in cs]
        qs = [q_ref[r, :] for r in rows]
        ks = [k_ref[r, :] for r in rows]
        vs = [v_ref[r, :] for r in rows]
        zero = jnp.zeros_like(qs[0])
        sa = [(_nt(jnp.where(lo, q, zero), k) * intra_a).astype(BF16) for q, k in zip(qs, ks)]
        sb = [(_nt(jnp.where(hi, q, zero), k) * intra_b).astype(BF16) for q, k in zip(qs, ks)]
        cross = [_nn(q, st_ref[c]) * read_decay for q, c in zip(qs, cs)]
        os_ = [jnp.where(lo, _nn(a, v), _nn(b, v)) + x for a, b, v, x in zip(sa, sb, vs, cross)]
        ds_ = [o - half_mean(o) for o in os_]
        ys = [d * lax.rsqrt(half_mean(d * d) + NORM_EPS) * gn for d in ds_]
        for r, y in zip(rows, ys):
            o_ref[r, :] = (jax.nn.silu(gate_ref[r, :]) * y).astype(BF16)
        return carry

    lax.fori_loop(0, n_chunks // RET_UNROLL, out_body, 0)


def _retention(ret, rgate, gn, intra, rd, wd, cd, batch, seq):
    pair_const3 = lambda b, j: (j, 0, 0)
    return pl.pallas_call(
        _retention_kernel,
        grid=(batch, RET_PAIRS),
        in_specs=[
            pl.BlockSpec((seq, LANES), lambda b, j: (b, j)),
            pl.BlockSpec((seq, LANES), lambda b, j: (b, RET_PAIRS + j)),
            pl.BlockSpec((seq, LANES), lambda b, j: (b, 2 * RET_PAIRS + j)),
            pl.BlockSpec((seq, LANES), lambda b, j: (b, j)),
            pl.BlockSpec((1, 1, LANES), pair_const3),
            pl.BlockSpec((1, 2, RET_CHUNK, RET_CHUNK), lambda b, j: (j, 0, 0, 0)),
            pl.BlockSpec((1, RET_CHUNK, LANES), pair_const3),
            pl.BlockSpec((1, RET_CHUNK, LANES), pair_const3),
            pl.BlockSpec((1, 1, LANES), pair_const3),
        ],
        out_specs=pl.BlockSpec((seq, LANES), lambda b, j: (b, j)),
        out_shape=jax.ShapeDtypeStruct((batch * seq, RET_PAIRS * LANES), BF16),
        scratch_shapes=[pltpu.VMEM((seq // RET_CHUNK, LANES, LANES), F32),
                        pltpu.VMEM((seq // RET_CHUNK, LANES, LANES), BF16)],
        compiler_params=pltpu.CompilerParams(
            dimension_semantics=("parallel", "arbitrary"), vmem_limit_bytes=VMEM_LIMIT),
        name="retention",
    )(ret, ret, ret, rgate, gn, intra, rd, wd, cd)


def _out_mlp_kernel(x_ref, nsa_ref, mla_ref, ret_ref, wo_ref, g2_ref, wu_ref, wd_ref, gf_ref, o_ref,
                    *, final_norm):
    w = RET_PAIRS * LANES
    mixed = (_nn(nsa_ref[...], wo_ref[0:w, :]) + _nn(mla_ref[...], wo_ref[w:2 * w, :])
             + _nn(ret_ref[...], wo_ref[2 * w:3 * w, :]))
    x = x_ref[...] + mixed
    h = _rms(x, g2_ref[...]).astype(BF16)
    y = x
    for c in range(D_FF // MLP_FF_CHUNK):
        sl = slice(c * MLP_FF_CHUNK, (c + 1) * MLP_FF_CHUNK)
        u = jnp.maximum(_nn(h, wu_ref[:, sl]), 0.0)
        y = y + _nn((u * u).astype(BF16), wd_ref[sl, :])
    if final_norm:
        y = _rms(y, gf_ref[...])
    o_ref[...] = y


def _out_mlp(x2, o_nsa, o_mla, o_ret, wo, g2, wu, wd, layer, gf, final_norm):
    t = x2.shape[0]
    w = RET_PAIRS * LANES
    row = lambda i: (i, 0)
    const = lambda i: (0, 0)
    resident = dict(pipeline_mode=pl.Buffered(1))
    return pl.pallas_call(
        functools.partial(_out_mlp_kernel, final_norm=final_norm),
        grid=(t // MLP_TM,),
        in_specs=[
            pl.BlockSpec((MLP_TM, D_MODEL), row),
            pl.BlockSpec((MLP_TM, w), row),
            pl.BlockSpec((MLP_TM, w), row),
            pl.BlockSpec((MLP_TM, w), row),
            _layer_spec(wo, layer, **resident),
            pl.BlockSpec((1, D_MODEL), const),
            _layer_spec(wu, layer, **resident),
            _layer_spec(wd, layer, **resident),
            pl.BlockSpec((1, D_MODEL), const),
        ],
        out_specs=pl.BlockSpec((MLP_TM, D_MODEL), row),
        out_shape=jax.ShapeDtypeStruct((t, D_MODEL), F32),
        compiler_params=pltpu.CompilerParams(
            dimension_semantics=("parallel",), vmem_limit_bytes=VMEM_LIMIT),
        name="out_mlp",
    )(x2, o_nsa, o_mla, o_ret, wo, g2, wu, wd, gf)


ROPE_KINDS = (
    (PARTIAL_ROPE_DIM, ROPE_THETA, HEAD_DIM, 0),
    (MLA_ROPE_DIM, ROPE_THETA, LANES, HALF),
    (HEAD_DIM, RET_THETA, HEAD_DIM, 0),
)
ROPE_TM = 1024


def _rope_placement():
    n_angles = sum(dim // 2 for dim, _, _, _ in ROPE_KINDS)
    assert 2 * n_angles <= LANES
    place = np.zeros((LANES, 3 * LANES * len(ROPE_KINDS)), np.float32)
    fill = np.zeros((1, 3 * LANES * len(ROPE_KINDS)), np.float32)
    row0 = 0
    for kind, (dim, _, period, base) in enumerate(ROPE_KINDS):
        half = dim // 2
        col0 = kind * 3 * LANES
        for lane in range(LANES):
            rel = (lane - base) % period
            first = lane >= base and rel < half
            second = lane >= base and half <= rel < dim
            if first or second:
                angle = rel if first else rel - half
                place[row0 + angle, col0 + lane] = 1.0
                place[n_angles + row0 + angle, col0 + (1 if first else 2) * LANES + lane] = -1.0 if first else 1.0
            else:
                fill[0, col0 + lane] = 1.0
        row0 += half
    return jnp.asarray(place, BF16), jnp.asarray(fill)


def _rope_kernel(cs_ref, place_ref, fill_ref, *out_refs):
    x = cs_ref[...]
    x1 = x.astype(BF16)
    r1 = x - x1.astype(F32)
    x2 = r1.astype(BF16)
    x3 = (r1 - x2.astype(F32)).astype(BF16)
    place = place_ref[...]
    tab = _tn(x1, place) + _tn(x2, place) + _tn(x3, place) + fill_ref[...]
    for i, ref in enumerate(out_refs):
        ref[...] = tab[:, i * 3 * LANES:(i + 1) * 3 * LANES]


def _rope_tables(positions):
    inv = jnp.concatenate([1.0 / (theta ** (jnp.arange(0, dim, 2, dtype=F32) / dim))
                           for dim, theta, _, _ in ROPE_KINDS])
    ang = inv[:, None] * positions.reshape(-1).astype(F32)[None, :]
    compact = jnp.concatenate([jnp.cos(ang), jnp.sin(ang)], axis=0)
    compact = jnp.pad(compact, ((0, LANES - compact.shape[0]), (0, 0)))
    place, fill = _rope_placement()
    t = compact.shape[1]
    tab_shape = jax.ShapeDtypeStruct((t, 3 * LANES), F32)
    return pl.pallas_call(
        _rope_kernel,
        grid=(t // ROPE_TM,),
        in_specs=[pl.BlockSpec((LANES, ROPE_TM), lambda i: (0, i)),
                  pl.BlockSpec(place.shape, lambda i: (0, 0)),
                  pl.BlockSpec(fill.shape, lambda i: (0, 0))],
        out_specs=tuple(pl.BlockSpec((ROPE_TM, 3 * LANES), lambda i: (i, 0)) for _ in ROPE_KINDS),
        out_shape=tuple(tab_shape for _ in ROPE_KINDS),
        compiler_params=pltpu.CompilerParams(
            dimension_semantics=("parallel",), vmem_limit_bytes=VMEM_LIMIT),
        name="rope_tables",
    )(compact, place, fill)


def _pad_cols(w, n):
    return jnp.pad(w, ((0, 0), (0, 0), (0, n - w.shape[-1])))


def _in_weight(w_in):
    offs = np.cumsum((0,) + IN_SIZES)
    seg = [w_in[:, :, offs[i]:offs[i + 1]] for i in range(len(IN_SIZES))]
    (nsa_q, k_cmp, v_cmp, k_slc, v_slc, k_win, v_win, gate,
     cq, ckv, kpe, ret_q, ret_k, ret_v, ret_g) = seg
    scale = HEAD_DIM ** -0.5
    qh = [nsa_q[:, :, h * HEAD_DIM:(h + 1) * HEAD_DIM] * scale for h in range(NSA_HEADS)]
    tiles = [jnp.concatenate([qh[j], qh[j + NSA_REP]], axis=-1) for j in range(NSA_REP)]
    tiles += [k_slc, k_win, k_cmp, v_cmp]
    tiles += [cq, ckv, jnp.pad(kpe, ((0, 0), (0, 0), (HALF, LANES - HALF - MLA_ROPE_DIM)))]
    ret_w = RET_PAIRS * LANES
    tiles += [_pad_cols(ret_q, ret_w), _pad_cols(ret_k * scale, ret_w), _pad_cols(ret_v, ret_w),
              _pad_cols(ret_g, ret_w)]
    w_t = lax.optimization_barrier(
        jnp.concatenate([v_slc, v_win, _pad_cols(gate, GATE_ROWS)], axis=-1).astype(BF16))
    return jnp.concatenate(tiles, axis=-1).astype(BF16), w_t.transpose(0, 2, 1)


def _compress_weights(pos, w1, w2):
    nl = pos.shape[0]
    g, dh, hid = NSA_KV_GROUPS, HEAD_DIM, CMP_HIDDEN
    p = jnp.tile(pos, (1, 1, g))
    eye = jnp.eye(g, dtype=w1.dtype)
    w = w1.reshape(nl, CMP_LEN, dh, hid)
    w = jnp.einsum('ltdc,gk->ltgdkc', w, eye).reshape(nl, CMP_LEN, g * dh, g * hid)
    w2b = jnp.einsum('lcd,gk->lgckd', w2, eye).reshape(nl, g * hid, g * dh)
    return p, w.astype(BF16), w2b.astype(BF16)


def _mla_weights(w_uq, w_ukv):
    nl = w_uq.shape[0]
    dq = MLA_NOPE_DIM + MLA_ROPE_DIM
    wq = w_uq.reshape(nl, MLA_Q_RANK, MLA_HEADS, dq)
    pe = wq[..., MLA_NOPE_DIM:]
    half = MLA_ROPE_DIM // 2
    rot = jnp.concatenate([jnp.zeros_like(wq[..., :MLA_NOPE_DIM]), -pe[..., half:], pe[..., :half]], axis=-1)
    pad = lambda w: jnp.pad(w, ((0, 0), (0, 0), (0, 0), (0, LANES - dq))).reshape(
        nl, MLA_Q_RANK, MLA_HEADS * LANES)
    wq = jnp.concatenate([pad(wq), pad(rot)], axis=-1)
    wkv = w_ukv.reshape(nl, MLA_KV_RANK, MLA_HEADS, MLA_NOPE_DIM + MLA_V_DIM)
    wk = jnp.pad(wkv[..., :MLA_NOPE_DIM], ((0, 0), (0, 0), (0, 0), (0, LANES - MLA_NOPE_DIM)))
    wk = wk.reshape(nl, MLA_KV_RANK, MLA_HEADS * LANES)
    wvt = wkv[..., MLA_NOPE_DIM:].reshape(nl, MLA_KV_RANK, MLA_V_ROWS).transpose(0, 2, 1)
    return wq.astype(BF16), wk.astype(BF16), wvt.astype(BF16)


def _out_weight(w_out):
    nl = w_out.shape[0]
    pad_rows = lambda w: jnp.pad(w, ((0, 0), (0, RET_PAIRS * LANES - w.shape[1]), (0, 0)))
    nsa = w_out[:, :NSA_Q_W].reshape(nl, NSA_HEADS, HEAD_DIM, D_MODEL)
    order = [h for j in range(NSA_REP) for h in (j, j + NSA_REP)]
    nsa = nsa[:, order].reshape(nl, NSA_Q_W, D_MODEL)
    mla = pad_rows(w_out[:, NSA_Q_W:NSA_Q_W + MLA_HEADS * MLA_V_DIM])
    ret = pad_rows(w_out[:, NSA_Q_W + MLA_HEADS * MLA_V_DIM:])
    return jnp.concatenate([nsa, mla, ret], axis=1).astype(BF16)


def _retention_tables(gn_gain):
    nh = 2 * RET_PAIRS
    log_g = jnp.log(1.0 - 2.0 ** (-5.0 - jnp.arange(nh, dtype=F32)))
    i = jnp.arange(RET_CHUNK, dtype=F32)
    diff = i[:, None] - i[None, :]
    intra = jnp.where(diff >= 0, jnp.exp(jnp.maximum(diff, 0.0)[None] * log_g[:, None, None]), 0.0)
    read_decay = jnp.exp((i + 1.0)[None, :] * log_g[:, None])
    write_decay = jnp.exp((RET_CHUNK - 1.0 - i)[None, :] * log_g[:, None])
    chunk_decay = jnp.exp(RET_CHUNK * log_g)

    def lanes(t):
        t = t.reshape(RET_PAIRS, 2, -1)
        return jnp.repeat(t.transpose(0, 2, 1), HALF, axis=-1)

    gn = jnp.pad(gn_gain, ((0, 0), (0, nh - RET_HEADS), (0, 0)))
    gn = gn.reshape(gn.shape[0], RET_PAIRS, 1, LANES)
    return (gn, intra.reshape(RET_PAIRS, 2, RET_CHUNK, RET_CHUNK), lanes(read_decay), lanes(write_decay),
            lanes(chunk_decay[:, None]))


def _selection_overlap(seq):
    n_cmp = (seq - CMP_LEN) // CMP_STRIDE + 1
    n_sel = seq // SEL_BLOCK
    cs = np.arange(n_cmp) * CMP_STRIDE
    ss = np.arange(n_sel) * SEL_BLOCK
    ov = np.clip(np.minimum(cs[:, None] + CMP_LEN, ss[None, :] + SEL_BLOCK)
                 - np.maximum(cs[:, None], ss[None, :]), 0, None) / CMP_LEN
    ovl_t = np.zeros((n_sel, seq // CMP_STRIDE), np.float32)
    ovl_t[:, :n_cmp] = ov.T
    return jnp.asarray(ovl_t, BF16)


def kernel(x, positions, ln1_gain, w_in, cmp_pos_k, cmp_w1_k, cmp_w2_k, cmp_pos_v, cmp_w1_v, cmp_w2_v,
           mla_q_norm, mla_w_uq, mla_kv_norm, mla_w_ukv, ret_gn_gain, w_out, ln2_gain, w_up, w_down,
           final_gain):
    batch, seq, _ = x.shape
    depth = w_in.shape[0]
    t = batch * seq

    tab_n, tab_m, tab_r = _rope_tables(positions)

    w_in_p, w_in_t = _in_weight(w_in)
    pos_k, w1_k, w2_k = _compress_weights(cmp_pos_k, cmp_w1_k, cmp_w2_k)
    pos_v, w1_v, w2_v = _compress_weights(cmp_pos_v, cmp_w1_v, cmp_w2_v)
    cmp_pos = jnp.stack([pos_k, pos_v], axis=1)
    cmp_w1 = jnp.stack([w1_k, w1_v], axis=1)
    w2_vt = w2_v.transpose(0, 2, 1)
    wq, wk, wvt = _mla_weights(mla_w_uq, mla_w_ukv)
    wo = _out_weight(w_out)
    wu = w_up.astype(BF16)
    wd = w_down.astype(BF16)
    gn, intra, rd, wdec, cd = _retention_tables(ret_gn_gain)
    ovl_t = _selection_overlap(seq)
    gf = final_gain.reshape(1, D_MODEL)

    x2 = x.reshape(t, D_MODEL)
    for l in range(depth):
        nsa, nsa_t, gate_t, k_cmp, v_cmp, mla_in, ret, rgate = _in_proj(
            x2, ln1_gain[l].reshape(1, D_MODEL), w_in_p, w_in_t, l, tab_n, tab_m, tab_r)
        kc, vct = _compress(k_cmp, v_cmp, cmp_pos[l], cmp_w1, w2_k[l], w2_vt[l], l, batch, seq)
        o_nsa = _nsa_attention(nsa, nsa_t, gate_t, kc, vct, ovl_t, batch, seq)
        q_m, k_m, vt_m = _mla_up(mla_in, mla_q_norm[l].reshape(1, -1), mla_kv_norm[l].reshape(1, -1),
                                 wq, wk, wvt, l, tab_m)
        o_mla = _mla_attention(q_m, k_m, vt_m, batch, seq)
        o_ret = _retention(ret, rgate, gn[l], intra, rd, wdec, cd, batch, seq)
        x2 = _out_mlp(x2, o_nsa, o_mla, o_ret, wo, ln2_gain[l].reshape(1, D_MODEL), wu, wd, l, gf,
                      final_norm=(l == depth - 1))
    return x2.reshape(batch, seq, D_MODEL)
```

```python
import functools
import math

import numpy as np
import jax
import jax.numpy as jnp
from jax import lax
from jax.experimental import pallas as pl
from jax.experimental.pallas import tpu as pltpu

F32 = jnp.float32
BF16 = jnp.bfloat16

D_MODEL = 1024
HEAD_DIM = 64
NSA_HEADS = 6
NSA_KV_GROUPS = 2
NSA_REP = NSA_HEADS // NSA_KV_GROUPS
N_BRANCH = 3
CMP_LEN = 32
CMP_STRIDE = 16
CMP_HIDDEN = 2 * HEAD_DIM
SEL_BLOCK = 64
SEL_TOP_N = 16
WINDOW = 512
MLA_HEADS = 5
MLA_Q_RANK = 256
MLA_KV_RANK = 128
MLA_NOPE_DIM = 64
MLA_ROPE_DIM = 32
MLA_V_DIM = 64
RET_HEADS = 5
RET_CHUNK = 128
ROPE_THETA = 500000.0
PARTIAL_ROPE_DIM = HEAD_DIM // 4
RET_THETA = 10000.0
D_FF = 4 * D_MODEL
NORM_EPS = 1e-6
NEG_INF = -1e30
FORCE_SCORE = 1e9
LOG2E = math.log2(math.e)

NSA_Q_W = NSA_HEADS * HEAD_DIM
NSA_KV_W = NSA_KV_GROUPS * HEAD_DIM
NSA_GATE_W = NSA_HEADS * N_BRANCH
RET_W = RET_HEADS * HEAD_DIM
IN_SIZES = (NSA_Q_W, NSA_KV_W, NSA_KV_W, NSA_KV_W, NSA_KV_W, NSA_KV_W, NSA_KV_W, NSA_GATE_W,
            MLA_Q_RANK, MLA_KV_RANK, MLA_ROPE_DIM, RET_W, RET_W, RET_W, RET_W)

LANES = 128
HALF = LANES // 2
VMEM_LIMIT = 56 * 1024 * 1024

NSA_TILES = 5
GATE_ROWS = 32
NSA_T_ROWS = 2 * LANES + GATE_ROWS
MLA_IN_TILES = 4
RET_PAIRS = 3
RET_TILES = 3 * RET_PAIRS
IN_TILES = NSA_TILES + 2 + MLA_IN_TILES + RET_TILES + RET_PAIRS
N_PAD = IN_TILES * LANES
MLA_V_ROWS = MLA_HEADS * MLA_V_DIM

IN_TM = 512
MLP_TM = 512
MLA_TM = 512
ATT_T = 512
MLP_FF_CHUNK = 1024


def _nn(a, b):
    return jnp.dot(a, b, preferred_element_type=F32)


def _nt(a, b):
    return lax.dot_general(a, b, (((1,), (1,)), ((), ())), preferred_element_type=F32)


def _rms(x, gain):
    return x * lax.rsqrt(jnp.mean(x * x, axis=-1, keepdims=True) + NORM_EPS) * gain


def _rope(val, tab, half):
    cos = tab[:, 0:LANES]
    sin_a = tab[:, LANES:2 * LANES]
    sin_b = tab[:, 2 * LANES:3 * LANES]
    return (val * cos + pltpu.roll(val, LANES - half, 1) * sin_a
            + pltpu.roll(val, half, 1) * sin_b)


def _lane_lo(shape):
    return lax.broadcasted_iota(jnp.int32, shape, len(shape) - 1) < HALF


def _inproj_kernel(x_ref, g_ref, w_ref, wt_ref, tn_ref, tm_ref, tr_ref,
                   nsa_ref, nsat_ref, gatet_ref, kcmp_ref, vcmp_ref, mla_ref, ret_ref, rgate_ref):
    h = _rms(x_ref[...], g_ref[...]).astype(BF16)
    tab_n = tn_ref[...]
    tab_r = tr_ref[...]

    def tile(a, i):
        return a[:, i * LANES:(i + 1) * LANES]

    c0 = 0
    a = _nn(h, w_ref[:, c0:c0 + NSA_TILES * LANES])
    for i in range(NSA_TILES):
        v = _rope(tile(a, i), tab_n, PARTIAL_ROPE_DIM // 2)
        if i < NSA_REP:
            v = v * LOG2E
        nsa_ref[:, i * LANES:(i + 1) * LANES] = v.astype(BF16)
    c0 += NSA_TILES * LANES
    a = _nn(h, w_ref[:, c0:c0 + 2 * LANES])
    kcmp_ref[...] = _rope(tile(a, 0), tab_n, PARTIAL_ROPE_DIM // 2)
    vcmp_ref[...] = tile(a, 1)
    c0 += 2 * LANES
    a = _nn(h, w_ref[:, c0:c0 + MLA_IN_TILES * LANES])
    mla_ref[:, 0:3 * LANES] = a[:, 0:3 * LANES]
    mla_ref[:, 3 * LANES:4 * LANES] = _rope(tile(a, 3), tm_ref[...], MLA_ROPE_DIM // 2)
    c0 += MLA_IN_TILES * LANES
    a = _nn(h, w_ref[:, c0:c0 + RET_TILES * LANES])
    for i in range(RET_TILES):
        v = tile(a, i)
        if i < 2 * RET_PAIRS:
            v = _rope(v, tab_r, HEAD_DIM // 2)
        ret_ref[:, i * LANES:(i + 1) * LANES] = v.astype(BF16)
    c0 += RET_TILES * LANES
    rgate_ref[...] = _nn(h, w_ref[:, c0:c0 + RET_PAIRS * LANES])
    at = _nt(wt_ref[...], h)
    nsat_ref[...] = at[0:2 * LANES, :].astype(BF16)
    gatet_ref[...] = at[2 * LANES:NSA_T_ROWS, :]


def _layer_spec(w, layer, **kwargs):
    zeros = (0,) * (w.ndim - 1)
    return pl.BlockSpec((None,) + w.shape[1:], lambda *_: (layer,) + zeros, **kwargs)


def _in_proj(x2, gain, w, wt, layer, tab_n, tab_m, tab_r):
    t = x2.shape[0]
    row = lambda i: (i, 0)
    col = lambda i: (0, i)
    const = lambda i: (0, 0)
    out_shapes = (
        jax.ShapeDtypeStruct((t, NSA_TILES * LANES), BF16),
        jax.ShapeDtypeStruct((2 * LANES, t), BF16),
        jax.ShapeDtypeStruct((GATE_ROWS, t), F32),
        jax.ShapeDtypeStruct((t, LANES), F32),
        jax.ShapeDtypeStruct((t, LANES), F32),
        jax.ShapeDtypeStruct((t, MLA_IN_TILES * LANES), F32),
        jax.ShapeDtypeStruct((t, RET_TILES * LANES), BF16),
        jax.ShapeDtypeStruct((t, RET_PAIRS * LANES), F32),
    )
    out_specs = tuple(
        pl.BlockSpec((s.shape[0], IN_TM), col) if s.shape[1] == t else pl.BlockSpec((IN_TM, s.shape[1]), row)
        for s in out_shapes)
    return pl.pallas_call(
        _inproj_kernel,
        grid=(t // IN_TM,),
        in_specs=[
            pl.BlockSpec((IN_TM, D_MODEL), row),
            pl.BlockSpec((1, D_MODEL), const),
            _layer_spec(w, layer),
            _layer_spec(wt, layer),
            pl.BlockSpec((IN_TM, 3 * LANES), row),
            pl.BlockSpec((IN_TM, 3 * LANES), row),
            pl.BlockSpec((IN_TM, 3 * LANES), row),
        ],
        out_specs=out_specs,
        out_shape=out_shapes,
        compiler_params=pltpu.CompilerParams(
            dimension_semantics=("parallel",), vmem_limit_bytes=VMEM_LIMIT),
        name="in_proj",
    )(x2, gain, w, wt, tab_n, tab_m, tab_r)


def _compress_kernel(k_ref, v_ref, pos_ref, w1_ref, w2k_ref, w2vt_ref, kc_ref, vct_ref):
    n_blk = k_ref.shape[0] // CMP_STRIDE

    def hidden(src, i):
        lo = jnp.zeros((n_blk, NSA_KV_GROUPS * CMP_HIDDEN), F32)
        hi = jnp.zeros((n_blk, NSA_KV_GROUPS * CMP_HIDDEN), F32)
        for r in range(CMP_STRIDE):
            tok = src[pl.ds(r, n_blk, stride=CMP_STRIDE), :]
            lo = lo + _nn((tok + pos_ref[i, r:r + 1, :]).astype(BF16), w1_ref[i, r])
            r2 = CMP_STRIDE + r
            hi = hi + _nn((tok + pos_ref[i, r2:r2 + 1, :]).astype(BF16), w1_ref[i, r2])
        return jax.nn.gelu(lo + pltpu.roll(hi, n_blk - 1, 0)).astype(BF16)

    kc_ref[0] = _nn(hidden(k_ref, 0), w2k_ref[...]).astype(BF16)
    vct_ref[0] = _nt(w2vt_ref[...], hidden(v_ref, 1)).astype(BF16)


def _compress(k_cmp, v_cmp, pos, w1, w2k, w2vt, layer, batch, seq):
    b = batch
    n_blk = seq // CMP_STRIDE
    return pl.pallas_call(
        _compress_kernel,
        grid=(b,),
        in_specs=[
            pl.BlockSpec((seq, LANES), lambda i: (i, 0)),
            pl.BlockSpec((seq, LANES), lambda i: (i, 0)),
            pl.BlockSpec(pos.shape, lambda i: (0, 0, 0)),
            _layer_spec(w1, layer),
            pl.BlockSpec(w2k.shape, lambda i: (0, 0)),
            pl.BlockSpec(w2vt.shape, lambda i: (0, 0)),
        ],
        out_specs=(pl.BlockSpec((1, n_blk, LANES), lambda i: (i, 0, 0)),
                   pl.BlockSpec((1, LANES, n_blk), lambda i: (i, 0, 0))),
        out_shape=(jax.ShapeDtypeStruct((b, n_blk, LANES), BF16),
                   jax.ShapeDtypeStruct((b, LANES, n_blk), BF16)),
        compiler_params=pltpu.CompilerParams(
            dimension_semantics=("parallel",), vmem_limit_bytes=VMEM_LIMIT),
        name="nsa_compress",
    )(k_cmp, v_cmp, pos, w1, w2k, w2vt)


SCORE_LOOKAHEAD = 2
ONES_ROWS = 16


def _softmax_steps(score_fns, v_ts, states):
    n = len(score_fns)
    s_ts = [score_fns[h]() if h < SCORE_LOOKAHEAD else None for h in range(n)]
    out = []
    for h in range(n):
        if h + SCORE_LOOKAHEAD < n:
            s_ts[h + SCORE_LOOKAHEAD] = score_fns[h + SCORE_LOOKAHEAD]()
        m_old, acc_old = states[h]
        m = jnp.maximum(m_old, jnp.max(s_ts[h], axis=0, keepdims=True))
        p = jnp.exp2(s_ts[h] - m).astype(BF16)
        s_ts[h] = None
        v_ext = jnp.concatenate([v_ts[h], jnp.ones((ONES_ROWS, v_ts[h].shape[1]), BF16)], axis=0)
        out.append((m, jnp.exp2(m_old - m) * acc_old + _nn(v_ext, p)))
    return tuple(out)


def _softmax_init(cols):
    return jnp.full((1, cols), NEG_INF, F32), jnp.zeros((HEAD_DIM + ONES_ROWS, cols), F32)


def _softmax_finish(state):
    acc = state[1]
    return acc[0:HEAD_DIM] * (1.0 / jnp.maximum(acc[HEAD_DIM:HEAD_DIM + 1], 1e-30))


def _nsa_kernel(q_ref, gatet_ref, kc_ref, vct_ref, ks_ref, kw_ref, vst_ref, vwt_ref, ovl_ref,
                o_ref, out_t_ref, selb_ref):
    tq = ATT_T
    qi = pl.program_id(1)
    q0 = pl.multiple_of(qi * tq, tq)
    lo1 = _lane_lo((1, LANES))
    group_lanes = (lo1, jnp.logical_not(lo1))
    gate = jax.nn.sigmoid(gatet_ref[...])
    heads = range(NSA_HEADS)

    def q_head(h):
        j, g = h % NSA_REP, h // NSA_REP
        tile = q_ref[:, j * LANES:(j + 1) * LANES]
        return jnp.where(group_lanes[g], tile, jnp.zeros_like(tile))

    def v_rows(ref, h, k0, n):
        g = h // NSA_REP
        return ref[g * HEAD_DIM:(g + 1) * HEAD_DIM, pl.ds(k0, n)]

    def emit(h, branch, qs, nq, o_t):
        j, g = h % NSA_REP, h // NSA_REP
        r0 = j * LANES + g * HEAD_DIM
        row = h * N_BRANCH + branch
        val = gate[row:row + 1, qs:qs + nq] * o_t
        if branch == 0:
            out_t_ref[r0:r0 + HEAD_DIM, qs:qs + nq] = val
        else:
            out_t_ref[r0:r0 + HEAD_DIM, qs:qs + nq] += val

    qm = [q_head(h) for h in heads]

    n_cmp_pad = kc_ref.shape[1]
    kc = kc_ref[0]
    vct = vct_ref[0]
    n_i = lax.broadcasted_iota(jnp.int32, (n_cmp_pad, tq), 0)
    t_l = q0 + lax.broadcasted_iota(jnp.int32, (n_cmp_pad, tq), 1)
    cmask = (n_i * CMP_STRIDE + (CMP_LEN - 1)) <= t_l
    cmask_f = cmask.astype(F32)
    s_cs = [jnp.where(cmask, _nt(kc, qm[h]), NEG_INF) for h in heads]
    p_cs = [jnp.exp2(s_t - jnp.max(s_t, axis=0, keepdims=True)) * cmask_f for s_t in s_cs]
    p_cs = [p * (1.0 / jnp.maximum(jnp.sum(p, axis=0, keepdims=True), 1e-30)) for p in p_cs]
    for h in heads:
        g = h // NSA_REP
        emit(h, 0, 0, tq, _nn(vct[g * HEAD_DIM:(g + 1) * HEAD_DIM, :], p_cs[h].astype(BF16)))

    n_sel = ovl_ref.shape[0]
    needs_rank = q0 + tq > SEL_TOP_N * SEL_BLOCK

    @pl.when(jnp.logical_not(needs_rank))
    def _():
        selb_ref[...] = jnp.zeros_like(selb_ref)

    @pl.when(needs_rank)
    def _():
        m_i = lax.broadcasted_iota(jnp.int32, (n_sel, tq), 0)
        cur = jnp.right_shift(q0 + lax.broadcasted_iota(jnp.int32, (n_sel, tq), 1),
                              SEL_BLOCK.bit_length() - 1)
        valid = m_i <= cur
        forced = (m_i == 0) | (m_i == cur) | (m_i == cur - 1)
        ovl = ovl_ref[...]
        sub = 8
        m_loc = lax.broadcasted_iota(jnp.int32, (sub, tq), 0)
        for g in range(NSA_KV_GROUPS):
            psum = p_cs[g * NSA_REP]
            for h in range(g * NSA_REP + 1, (g + 1) * NSA_REP):
                psum = psum + p_cs[h]
            p_hi = psum.astype(BF16)
            p_lo = (psum - p_hi.astype(F32)).astype(BF16)
            imp = _nn(ovl, p_hi) + _nn(ovl, p_lo)
            imp = jnp.where(valid & forced, FORCE_SCORE, imp)
            imp = jnp.where(valid, imp, NEG_INF)
            parts = [imp[i:i + sub] for i in range(0, n_sel, sub)]
            ranks = [jnp.zeros((sub, tq), jnp.int32) for _ in parts]
            for mp in range(n_sel):
                row = imp[mp:mp + 1, :]
                for i, part in enumerate(parts):
                    if i * sub + sub - 1 <= mp:
                        beats = row > part
                    elif i * sub > mp:
                        beats = row >= part
                    else:
                        beats = (row > part) | ((row == part) & (m_loc + i * sub > mp))
                    ranks[i] = ranks[i] + beats.astype(jnp.int32)
            rank = jnp.concatenate(ranks, axis=0)
            selb_ref[g] = jnp.where(rank < SEL_TOP_N, 0.0, NEG_INF)

    blocks_per_chunk = tq // SEL_BLOCK

    def sel_bias(g, c, n_keys, qs, nq):
        rows = [jnp.broadcast_to(selb_ref[g, pl.ds(c * blocks_per_chunk + i, 1), qs:qs + nq], (SEL_BLOCK, nq))
                for i in range(n_keys // SEL_BLOCK)]
        return jnp.concatenate(rows, axis=0)

    def sel_chunk(c, k0, n_keys, qs, nq, states, causal):
        k = ks_ref[pl.ds(k0, n_keys), :]
        bias = [sel_bias(g, c, n_keys, qs, nq) for g in range(NSA_KV_GROUPS)]
        if causal:
            mask = (lax.broadcasted_iota(jnp.int32, (n_keys, nq), 0)
                    <= qs + lax.broadcasted_iota(jnp.int32, (n_keys, nq), 1))

        def score_fn(h):
            s_t = _nt(k, qm[h][qs:qs + nq]) + bias[h // NSA_REP]
            return jnp.where(mask, s_t, NEG_INF) if causal else s_t

        return _softmax_steps([functools.partial(score_fn, h) for h in heads],
                              [v_rows(vst_ref, h, k0, n_keys) for h in heads], states)

    states = sel_chunk(qi, q0, tq, 0, tq, tuple(_softmax_init(tq) for _ in heads), True)
    states = lax.fori_loop(
        0, qi, lambda c, st: sel_chunk(c, pl.multiple_of(c * tq, tq), tq, 0, tq, st, False), states)
    for h in heads:
        emit(h, 1, 0, tq, _softmax_finish(states[h]))

    key_i = lax.broadcasted_iota(jnp.int32, (tq, tq), 0)
    qry_i = lax.broadcasted_iota(jnp.int32, (tq, tq), 1)

    def win_chunk(k0, states, mask):
        k = kw_ref[pl.ds(k0, tq), :]
        score_fns = [lambda h=h: jnp.where(mask, _nt(k, qm[h]), NEG_INF) for h in heads]
        return _softmax_steps(score_fns, [v_rows(vwt_ref, h, k0, tq) for h in heads], states)

    states = win_chunk(q0, tuple(_softmax_init(tq) for _ in heads), key_i <= qry_i)
    for d in range(1, WINDOW // tq + 1):
        k0 = pl.multiple_of(jnp.maximum(qi - d, 0) * tq, tq)
        in_band = (key_i - d * tq > qry_i - WINDOW) & (qi >= d)
        states = win_chunk(k0, states, in_band)
    for h in heads:
        emit(h, 2, 0, tq, _softmax_finish(states[h]))

    for j in range(NSA_REP):
        o_ref[:, j * LANES:(j + 1) * LANES] = out_t_ref[j * LANES:(j + 1) * LANES, :].T.astype(BF16)


def _nsa_attention(nsa, nsa_t, gate_t, kc, vct, ovl_t, batch, seq):
    tq = ATT_T
    nq = seq // tq
    n_blk = kc.shape[1]
    qrow = lambda b, i: (b * nq + i, 0)
    return pl.pallas_call(
        _nsa_kernel,
        grid=(batch, nq),
        in_specs=[
            pl.BlockSpec((tq, NSA_REP * LANES), qrow),
            pl.BlockSpec((GATE_ROWS, tq), lambda b, i: (0, b * nq + i)),
            pl.BlockSpec((1, n_blk, LANES), lambda b, i: (b, 0, 0)),
            pl.BlockSpec((1, LANES, n_blk), lambda b, i: (b, 0, 0)),
            pl.BlockSpec((seq, LANES), lambda b, i: (b, 3)),
            pl.BlockSpec((seq, LANES), lambda b, i: (b, 4)),
            pl.BlockSpec((LANES, seq), lambda b, i: (0, b)),
            pl.BlockSpec((LANES, seq), lambda b, i: (1, b)),
            pl.BlockSpec(ovl_t.shape, lambda b, i: (0, 0)),
        ],
        out_specs=pl.BlockSpec((tq, NSA_REP * LANES), qrow),
        out_shape=jax.ShapeDtypeStruct((batch * seq, NSA_REP * LANES), BF16),
        scratch_shapes=[pltpu.VMEM((NSA_REP * LANES, tq), F32),
                        pltpu.VMEM((NSA_KV_GROUPS, seq // SEL_BLOCK, tq), F32)],
        compiler_params=pltpu.CompilerParams(
            dimension_semantics=("parallel", "arbitrary"), vmem_limit_bytes=VMEM_LIMIT),
        name="nsa_attention",
    )(nsa, gate_t, kc, vct, nsa, nsa, nsa_t, nsa_t, ovl_t)


def _mla_up_kernel(in_ref, qn_ref, kvn_ref, wq_ref, wk_ref, wvt_ref, tab_ref, q_ref, k_ref, vt_ref):
    scale = (MLA_NOPE_DIM + MLA_ROPE_DIM) ** -0.5 * LOG2E
    cq = _rms(in_ref[:, 0:MLA_Q_RANK], qn_ref[...]).astype(BF16)
    ckv = _rms(in_ref[:, MLA_Q_RANK:MLA_Q_RANK + MLA_KV_RANK], kvn_ref[...]).astype(BF16)
    k_pe = in_ref[:, 3 * LANES:4 * LANES]
    cos = tab_ref[:, 0:LANES] * scale
    sin = (tab_ref[:, 2 * LANES:3 * LANES] - tab_ref[:, LANES:2 * LANES]) * scale
    q = _nn(cq, wq_ref[...])
    k = _nn(ckv, wk_ref[...])
    w = MLA_HEADS * LANES
    for h in range(MLA_HEADS):
        sl = slice(h * LANES, (h + 1) * LANES)
        rot = slice(w + h * LANES, w + (h + 1) * LANES)
        q_ref[:, sl] = (q[:, sl] * cos + q[:, rot] * sin).astype(BF16)
        k_ref[:, sl] = (k[:, sl] + k_pe).astype(BF16)
    vt_ref[...] = _nt(wvt_ref[...], ckv).astype(BF16)


def _mla_up(mla_in, q_norm, kv_norm, wq, wk, wvt, layer, tab_m):
    t = mla_in.shape[0]
    row = lambda i: (i, 0)
    const = lambda i: (0, 0)
    return pl.pallas_call(
        _mla_up_kernel,
        grid=(t // MLA_TM,),
        in_specs=[
            pl.BlockSpec((MLA_TM, MLA_IN_TILES * LANES), row),
            pl.BlockSpec((1, MLA_Q_RANK), const),
            pl.BlockSpec((1, MLA_KV_RANK), const),
            _layer_spec(wq, layer),
            _layer_spec(wk, layer),
            _layer_spec(wvt, layer),
            pl.BlockSpec((MLA_TM, 3 * LANES), row),
        ],
        out_specs=(pl.BlockSpec((MLA_TM, MLA_HEADS * LANES), row),
                   pl.BlockSpec((MLA_TM, MLA_HEADS * LANES), row),
                   pl.BlockSpec((MLA_V_ROWS, MLA_TM), lambda i: (0, i))),
        out_shape=(jax.ShapeDtypeStruct((t, MLA_HEADS * LANES), BF16),
                   jax.ShapeDtypeStruct((t, MLA_HEADS * LANES), BF16),
                   jax.ShapeDtypeStruct((MLA_V_ROWS, t), BF16)),
        compiler_params=pltpu.CompilerParams(
            dimension_semantics=("parallel",), vmem_limit_bytes=VMEM_LIMIT),
        name="mla_up",
    )(mla_in, q_norm, kv_norm, wq, wk, wvt, tab_m)


def _mla_attn_kernel(q_ref, k_ref, vt_ref, o_ref, out_t_ref):
    tq = ATT_T
    qi = pl.program_id(1)
    q0 = pl.multiple_of(qi * tq, tq)
    heads = range(MLA_HEADS)
    qh = [q_ref[:, h * LANES:(h + 1) * LANES] for h in heads]

    def chunk(k0, n_keys, qs, nq, states, causal):
        if causal:
            mask = (lax.broadcasted_iota(jnp.int32, (n_keys, nq), 0)
                    <= qs + lax.broadcasted_iota(jnp.int32, (n_keys, nq), 1))

        def score_fn(h):
            s_t = _nt(k_ref[pl.ds(k0, n_keys), h * LANES:(h + 1) * LANES], qh[h][qs:qs + nq])
            return jnp.where(mask, s_t, NEG_INF) if causal else s_t

        v_ts = [vt_ref[h * MLA_V_DIM:(h + 1) * MLA_V_DIM, pl.ds(k0, n_keys)] for h in heads]
        return _softmax_steps([functools.partial(score_fn, h) for h in heads], v_ts, states)

    states = chunk(q0, tq, 0, tq, tuple(_softmax_init(tq) for _ in heads), True)
    states = lax.fori_loop(
        0, qi, lambda c, st: chunk(pl.multiple_of(c * tq, tq), tq, 0, tq, st, False), states)
    for h in heads:
        out_t_ref[h * MLA_V_DIM:(h + 1) * MLA_V_DIM, :] = _softmax_finish(states[h])
    out_t_ref[MLA_V_ROWS:, :] = jnp.zeros((RET_PAIRS * LANES - MLA_V_ROWS, tq), F32)
    for j in range(RET_PAIRS):
        o_ref[:, j * LANES:(j + 1) * LANES] = out_t_ref[j * LANES:(j + 1) * LANES, :].T.astype(BF16)


def _mla_attention(q, k, vt, batch, seq):
    tq = ATT_T
    nq = seq // tq
    return pl.pallas_call(
        _mla_attn_kernel,
        grid=(batch, nq),
        in_specs=[
            pl.BlockSpec((tq, MLA_HEADS * LANES), lambda b, i: (b * nq + i, 0)),
            pl.BlockSpec((seq, MLA_HEADS * LANES), lambda b, i: (b, 0)),
            pl.BlockSpec((MLA_V_ROWS, seq), lambda b, i: (0, b)),
        ],
        out_specs=pl.BlockSpec((tq, RET_PAIRS * LANES), lambda b, i: (b * nq + i, 0)),
        out_shape=jax.ShapeDtypeStruct((batch * seq, RET_PAIRS * LANES), BF16),
        scratch_shapes=[pltpu.VMEM((RET_PAIRS * LANES, tq), F32)],
        compiler_params=pltpu.CompilerParams(
            dimension_semantics=("parallel", "arbitrary"), vmem_limit_bytes=VMEM_LIMIT),
        name="mla_attention",
    )(q, k, vt)


def _tn(a, b):
    return lax.dot_general(a, b, (((0,), (0,)), ((), ())), preferred_element_type=F32)


RET_UNROLL = 16


def _retention_kernel(q_ref, k_ref, v_ref, gate_ref, gn_ref, intra_ref, rd_ref, wd_ref, cd_ref, o_ref,
                      kv_ref, st_ref):
    c_len = RET_CHUNK
    n_chunks = q_ref.shape[0] // c_len
    lo = _lane_lo((1, LANES))
    hi = jnp.logical_not(lo)
    blockdiag = (lax.broadcasted_iota(jnp.int32, (LANES, LANES), 0) < HALF) == _lane_lo((LANES, LANES))
    intra_a = intra_ref[0, 0]
    intra_b = intra_ref[0, 1]
    read_decay = rd_ref[0]
    write_decay = wd_ref[0]
    chunk_decay = cd_ref[0]
    gn = gn_ref[0]

    def half_mean(x):
        s_lo = jnp.sum(jnp.where(lo, x, 0.0), axis=-1, keepdims=True)
        s_hi = jnp.sum(jnp.where(hi, x, 0.0), axis=-1, keepdims=True)
        return jnp.where(lo, s_lo, s_hi) * (1.0 / HEAD_DIM)

    def kv_body(c, carry):
        r0 = pl.multiple_of(c * c_len, c_len)
        kc = k_ref[pl.ds(r0, c_len), :]
        kv_ref[c] = _tn((kc.astype(F32) * write_decay).astype(BF16), v_ref[pl.ds(r0, c_len), :])
        return carry

    lax.fori_loop(0, n_chunks, kv_body, 0, unroll=RET_UNROLL)

    state = jnp.zeros((LANES, LANES), F32)
    for c in range(n_chunks):
        st_ref[c] = jnp.where(blockdiag, state, 0.0).astype(BF16)
        state = state * chunk_decay + kv_ref[c]

    def out_body(step, carry):
        cs = [step * RET_UNROLL + u for u in range(RET_UNROLL)]
        rows = [pl.ds(pl.multiple_of(c * c_len, c_len), c_len) for c in cs]
        qs = [q_ref[r, :] for r in rows]
        ks = [k_ref[r, :] for r in rows]
        vs = [v_ref[r, :] for r in rows]
        zero = jnp.zeros_like(qs[0])
        sa = [(_nt(jnp.where(lo, q, zero), k) * intra_a).astype(BF16) for q, k in zip(qs, ks)]
        sb = [(_nt(jnp.where(hi, q, zero), k) * intra_b).astype(BF16) for q, k in zip(qs, ks)]
        cross = [_nn(q, st_ref[c]) * read_decay for q, c in zip(qs, cs)]
        os_ = [jnp.where(lo, _nn(a, v), _nn(b, v)) + x for a, b, v, x in zip(sa, sb, vs, cross)]
        ds_ = [o - half_mean(o) for o in os_]
        ys = [d * lax.rsqrt(half_mean(d * d) + NORM_EPS) * gn for d in ds_]
        for r, y in zip(rows, ys):
            o_ref[r, :] = (jax.nn.silu(gate_ref[r, :]) * y).astype(BF16)
        return carry

    lax.fori_loop(0, n_chunks // RET_UNROLL, out_body, 0)


def _retention(ret, rgate, gn, intra, rd, wd, cd, batch, seq):
    pair_const3 = lambda b, j: (j, 0, 0)
    return pl.pallas_call(
        _retention_kernel,
        grid=(batch, RET_PAIRS),
        in_specs=[
            pl.BlockSpec((seq, LANES), lambda b, j: (b, j)),
            pl.BlockSpec((seq, LANES), lambda b, j: (b, RET_PAIRS + j)),
            pl.BlockSpec((seq, LANES), lambda b, j: (b, 2 * RET_PAIRS + j)),
            pl.BlockSpec((seq, LANES), lambda b, j: (b, j)),
            pl.BlockSpec((1, 1, LANES), pair_const3),
            pl.BlockSpec((1, 2, RET_CHUNK, RET_CHUNK), lambda b, j: (j, 0, 0, 0)),
            pl.BlockSpec((1, RET_CHUNK, LANES), pair_const3),
            pl.BlockSpec((1, RET_CHUNK, LANES), pair_const3),
            pl.BlockSpec((1, 1, LANES), pair_const3),
        ],
        out_specs=pl.BlockSpec((seq, LANES), lambda b, j: (b, j)),
        out_shape=jax.ShapeDtypeStruct((batch * seq, RET_PAIRS * LANES), BF16),
        scratch_shapes=[pltpu.VMEM((seq // RET_CHUNK, LANES, LANES), F32),
                        pltpu.VMEM((seq // RET_CHUNK, LANES, LANES), BF16)],
        compiler_params=pltpu.CompilerParams(
            dimension_semantics=("parallel", "arbitrary"), vmem_limit_bytes=VMEM_LIMIT),
        name="retention",
    )(ret, ret, ret, rgate, gn, intra, rd, wd, cd)


def _out_mlp_kernel(x_ref, nsa_ref, mla_ref, ret_ref, wo_ref, g2_ref, wu_ref, wd_ref, gf_ref, o_ref,
                    *, final_norm):
    w = RET_PAIRS * LANES
    mixed = (_nn(nsa_ref[...], wo_ref[0:w, :]) + _nn(mla_ref[...], wo_ref[w:2 * w, :])
             + _nn(ret_ref[...], wo_ref[2 * w:3 * w, :]))
    x = x_ref[...] + mixed
    h = _rms(x, g2_ref[...]).astype(BF16)
    y = x
    for c in range(D_FF // MLP_FF_CHUNK):
        sl = slice(c * MLP_FF_CHUNK, (c + 1) * MLP_FF_CHUNK)
        u = jnp.maximum(_nn(h, wu_ref[:, sl]), 0.0)
        y = y + _nn((u * u).astype(BF16), wd_ref[sl, :])
    if final_norm:
        y = _rms(y, gf_ref[...])
    o_ref[...] = y


def _out_mlp(x2, o_nsa, o_mla, o_ret, wo, g2, wu, wd, layer, gf, final_norm):
    t = x2.shape[0]
    w = RET_PAIRS * LANES
    row = lambda i: (i, 0)
    const = lambda i: (0, 0)
    resident = dict(pipeline_mode=pl.Buffered(1))
    return pl.pallas_call(
        functools.partial(_out_mlp_kernel, final_norm=final_norm),
        grid=(t // MLP_TM,),
        in_specs=[
            pl.BlockSpec((MLP_TM, D_MODEL), row),
            pl.BlockSpec((MLP_TM, w), row),
            pl.BlockSpec((MLP_TM, w), row),
            pl.BlockSpec((MLP_TM, w), row),
            _layer_spec(wo, layer, **resident),
            pl.BlockSpec((1, D_MODEL), const),
            _layer_spec(wu, layer, **resident),
            _layer_spec(wd, layer, **resident),
            pl.BlockSpec((1, D_MODEL), const),
        ],
        out_specs=pl.BlockSpec((MLP_TM, D_MODEL), row),
        out_shape=jax.ShapeDtypeStruct((t, D_MODEL), F32),
        compiler_params=pltpu.CompilerParams(
            dimension_semantics=("parallel",), vmem_limit_bytes=VMEM_LIMIT),
        name="out_mlp",
    )(x2, o_nsa, o_mla, o_ret, wo, g2, wu, wd, gf)


ROPE_KINDS = (
    (PARTIAL_ROPE_DIM, ROPE_THETA, HEAD_DIM, 0),
    (MLA_ROPE_DIM, ROPE_THETA, LANES, HALF),
    (HEAD_DIM, RET_THETA, HEAD_DIM, 0),
)
ROPE_TM = 1024


def _rope_placement():
    n_angles = sum(dim // 2 for dim, _, _, _ in ROPE_KINDS)
    assert 2 * n_angles <= LANES
    place = np.zeros((LANES, 3 * LANES * len(ROPE_KINDS)), np.float32)
    fill = np.zeros((1, 3 * LANES * len(ROPE_KINDS)), np.float32)
    row0 = 0
    for kind, (dim, _, period, base) in enumerate(ROPE_KINDS):
        half = dim // 2
        col0 = kind * 3 * LANES
        for lane in range(LANES):
            rel = (lane - base) % period
            first = lane >= base and rel < half
            second = lane >= base and half <= rel < dim
            if first or second:
                angle = rel if first else rel - half
                place[row0 + angle, col0 + lane] = 1.0
                place[n_angles + row0 + angle, col0 + (1 if first else 2) * LANES + lane] = -1.0 if first else 1.0
            else:
                fill[0, col0 + lane] = 1.0
        row0 += half
    return jnp.asarray(place, BF16), jnp.asarray(fill)


def _rope_kernel(cs_ref, place_ref, fill_ref, *out_refs):
    x = cs_ref[...]
    x1 = x.astype(BF16)
    r1 = x - x1.astype(F32)
    x2 = r1.astype(BF16)
    x3 = (r1 - x2.astype(F32)).astype(BF16)
    place = place_ref[...]
    tab = _tn(x1, place) + _tn(x2, place) + _tn(x3, place) + fill_ref[...]
    for i, ref in enumerate(out_refs):
        ref[...] = tab[:, i * 3 * LANES:(i + 1) * 3 * LANES]


def _rope_tables(positions):
    inv = jnp.concatenate([1.0 / (theta ** (jnp.arange(0, dim, 2, dtype=F32) / dim))
                           for dim, theta, _, _ in ROPE_KINDS])
    ang = inv[:, None] * positions.reshape(-1).astype(F32)[None, :]
    compact = jnp.concatenate([jnp.cos(ang), jnp.sin(ang)], axis=0)
    compact = jnp.pad(compact, ((0, LANES - compact.shape[0]), (0, 0)))
    place, fill = _rope_placement()
    t = compact.shape[1]
    tab_shape = jax.ShapeDtypeStruct((t, 3 * LANES), F32)
    return pl.pallas_call(
        _rope_kernel,
        grid=(t // ROPE_TM,),
        in_specs=[pl.BlockSpec((LANES, ROPE_TM), lambda i: (0, i)),
                  pl.BlockSpec(place.shape, lambda i: (0, 0)),
                  pl.BlockSpec(fill.shape, lambda i: (0, 0))],
        out_specs=tuple(pl.BlockSpec((ROPE_TM, 3 * LANES), lambda i: (i, 0)) for _ in ROPE_KINDS),
        out_shape=tuple(tab_shape for _ in ROPE_KINDS),
        compiler_params=pltpu.CompilerParams(
            dimension_semantics=("parallel",), vmem_limit_bytes=VMEM_LIMIT),
        name="rope_tables",
    )(compact, place, fill)


def _pad_cols(w, n):
    return jnp.pad(w, ((0, 0), (0, 0), (0, n - w.shape[-1])))


def _in_weight(w_in):
    offs = np.cumsum((0,) + IN_SIZES)
    seg = [w_in[:, :, offs[i]:offs[i + 1]] for i in range(len(IN_SIZES))]
    (nsa_q, k_cmp, v_cmp, k_slc, v_slc, k_win, v_win, gate,
     cq, ckv, kpe, ret_q, ret_k, ret_v, ret_g) = seg
    scale = HEAD_DIM ** -0.5
    qh = [nsa_q[:, :, h * HEAD_DIM:(h + 1) * HEAD_DIM] * scale for h in range(NSA_HEADS)]
    tiles = [jnp.concatenate([qh[j], qh[j + NSA_REP]], axis=-1) for j in range(NSA_REP)]
    tiles += [k_slc, k_win, k_cmp, v_cmp]
    tiles += [cq, ckv, jnp.pad(kpe, ((0, 0), (0, 0), (HALF, LANES - HALF - MLA_ROPE_DIM)))]
    ret_w = RET_PAIRS * LANES
    tiles += [_pad_cols(ret_q, ret_w), _pad_cols(ret_k * scale, ret_w), _pad_cols(ret_v, ret_w),
              _pad_cols(ret_g, ret_w)]
    w_t = lax.optimization_barrier(
        jnp.concatenate([v_slc, v_win, _pad_cols(gate, GATE_ROWS)], axis=-1).astype(BF16))
    return jnp.concatenate(tiles, axis=-1).astype(BF16), w_t.transpose(0, 2, 1)


def _compress_weights(pos, w1, w2):
    nl = pos.shape[0]
    g, dh, hid = NSA_KV_GROUPS, HEAD_DIM, CMP_HIDDEN
    p = jnp.tile(pos, (1, 1, g))
    eye = jnp.eye(g, dtype=w1.dtype)
    w = w1.reshape(nl, CMP_LEN, dh, hid)
    w = jnp.einsum('ltdc,gk->ltgdkc', w, eye).reshape(nl, CMP_LEN, g * dh, g * hid)
    w2b = jnp.einsum('lcd,gk->lgckd', w2, eye).reshape(nl, g * hid, g * dh)
    return p, w.astype(BF16), w2b.astype(BF16)


def _mla_weights(w_uq, w_ukv):
    nl = w_uq.shape[0]
    dq = MLA_NOPE_DIM + MLA_ROPE_DIM
    wq = w_uq.reshape(nl, MLA_Q_RANK, MLA_HEADS, dq)
    pe = wq[..., MLA_NOPE_DIM:]
    half = MLA_ROPE_DIM // 2
    rot = jnp.concatenate([jnp.zeros_like(wq[..., :MLA_NOPE_DIM]), -pe[..., half:], pe[..., :half]], axis=-1)
    pad = lambda w: jnp.pad(w, ((0, 0), (0, 0), (0, 0), (0, LANES - dq))).reshape(
        nl, MLA_Q_RANK, MLA_HEADS * LANES)
    wq = jnp.concatenate([pad(wq), pad(rot)], axis=-1)
    wkv = w_ukv.reshape(nl, MLA_KV_RANK, MLA_HEADS, MLA_NOPE_DIM + MLA_V_DIM)
    wk = jnp.pad(wkv[..., :MLA_NOPE_DIM], ((0, 0), (0, 0), (0, 0), (0, LANES - MLA_NOPE_DIM)))
    wk = wk.reshape(nl, MLA_KV_RANK, MLA_HEADS * LANES)
    wvt = wkv[..., MLA_NOPE_DIM:].reshape(nl, MLA_KV_RANK, MLA_V_ROWS).transpose(0, 2, 1)
    return wq.astype(BF16), wk.astype(BF16), wvt.astype(BF16)


def _out_weight(w_out):
    nl = w_out.shape[0]
    pad_rows = lambda w: jnp.pad(w, ((0, 0), (0, RET_PAIRS * LANES - w.shape[1]), (0, 0)))
    nsa = w_out[:, :NSA_Q_W].reshape(nl, NSA_HEADS, HEAD_DIM, D_MODEL)
    order = [h for j in range(NSA_REP) for h in (j, j + NSA_REP)]
    nsa = nsa[:, order].reshape(nl, NSA_Q_W, D_MODEL)
    mla = pad_rows(w_out[:, NSA_Q_W:NSA_Q_W + MLA_HEADS * MLA_V_DIM])
    ret = pad_rows(w_out[:, NSA_Q_W + MLA_HEADS * MLA_V_DIM:])
    return jnp.concatenate([nsa, mla, ret], axis=1).astype(BF16)


def _retention_tables(gn_gain):
    nh = 2 * RET_PAIRS
    log_g = jnp.log(1.0 - 2.0 ** (-5.0 - jnp.arange(nh, dtype=F32)))
    i = jnp.arange(RET_CHUNK, dtype=F32)
    diff = i[:, None] - i[None, :]
    intra = jnp.where(diff >= 0, jnp.exp(jnp.maximum(diff, 0.0)[None] * log_g[:, None, None]), 0.0)
    read_decay = jnp.exp((i + 1.0)[None, :] * log_g[:, None])
    write_decay = jnp.exp((RET_CHUNK - 1.0 - i)[None, :] * log_g[:, None])
    chunk_decay = jnp.exp(RET_CHUNK * log_g)

    def lanes(t):
        t = t.reshape(RET_PAIRS, 2, -1)
        return jnp.repeat(t.transpose(0, 2, 1), HALF, axis=-1)

    gn = jnp.pad(gn_gain, ((0, 0), (0, nh - RET_HEADS), (0, 0)))
    gn = gn.reshape(gn.shape[0], RET_PAIRS, 1, LANES)
    return (gn, intra.reshape(RET_PAIRS, 2, RET_CHUNK, RET_CHUNK), lanes(read_decay), lanes(write_decay),
            lanes(chunk_decay[:, None]))


def _selection_overlap(seq):
    n_cmp = (seq - CMP_LEN) // CMP_STRIDE + 1
    n_sel = seq // SEL_BLOCK
    cs = np.arange(n_cmp) * CMP_STRIDE
    ss = np.arange(n_sel) * SEL_BLOCK
    ov = np.clip(np.minimum(cs[:, None] + CMP_LEN, ss[None, :] + SEL_BLOCK)
                 - np.maximum(cs[:, None], ss[None, :]), 0, None) / CMP_LEN
    ovl_t = np.zeros((n_sel, seq // CMP_STRIDE), np.float32)
    ovl_t[:, :n_cmp] = ov.T
    return jnp.asarray(ovl_t, BF16)


def kernel(x, positions, ln1_gain, w_in, cmp_pos_k, cmp_w1_k, cmp_w2_k, cmp_pos_v, cmp_w1_v, cmp_w2_v,
           mla_q_norm, mla_w_uq, mla_kv_norm, mla_w_ukv, ret_gn_gain, w_out, ln2_gain, w_up, w_down,
           final_gain):
    batch, seq, _ = x.shape
    depth = w_in.shape[0]
    t = batch * seq

    tab_n, tab_m, tab_r = _rope_tables(positions)

    w_in_p, w_in_t = _in_weight(w_in)
    pos_k, w1_k, w2_k = _compress_weights(cmp_pos_k, cmp_w1_k, cmp_w2_k)
    pos_v, w1_v, w2_v = _compress_weights(cmp_pos_v, cmp_w1_v, cmp_w2_v)
    cmp_pos = jnp.stack([pos_k, pos_v], axis=1)
    cmp_w1 = jnp.stack([w1_k, w1_v], axis=1)
    w2_vt = w2_v.transpose(0, 2, 1)
    wq, wk, wvt = _mla_weights(mla_w_uq, mla_w_ukv)
    wo = _out_weight(w_out)
    wu = w_up.astype(BF16)
    wd = w_down.astype(BF16)
    gn, intra, rd, wdec, cd = _retention_tables(ret_gn_gain)
    ovl_t = _selection_overlap(seq)
    gf = final_gain.reshape(1, D_MODEL)

    x2 = x.reshape(t, D_MODEL)
    for l in range(depth):
        nsa, nsa_t, gate_t, k_cmp, v_cmp, mla_in, ret, rgate = _in_proj(
            x2, ln1_gain[l].reshape(1, D_MODEL), w_in_p, w_in_t, l, tab_n, tab_m, tab_r)
        kc, vct = _compress(k_cmp, v_cmp, cmp_pos[l], cmp_w1, w2_k[l], w2_vt[l], l, batch, seq)
        o_nsa = _nsa_attention(nsa, nsa_t, gate_t, kc, vct, ovl_t, batch, seq)
        q_m, k_m, vt_m = _mla_up(mla_in, mla_q_norm[l].reshape(1, -1), mla_kv_norm[l].reshape(1, -1),
                                 wq, wk, wvt, l, tab_m)
        o_mla = _mla_attention(q_m, k_m, vt_m, batch, seq)
        o_ret = _retention(ret, rgate, gn[l], intra, rd, wdec, cd, batch, seq)
        x2 = _out_mlp(x2, o_nsa, o_mla, o_ret, wo, ln2_gain[l].reshape(1, D_MODEL), wu, wd, l, gf,
                      final_norm=(l == depth - 1))
    return x2.reshape(batch, seq, D_MODEL)
```

```python
import functools
import math

import numpy as np
import jax
import jax.numpy as jnp
from jax import lax
from jax.experimental import pallas as pl
from jax.experimental.pallas import tpu as pltpu

F32 = jnp.float32
BF16 = jnp.bfloat16

D_MODEL = 1024
HEAD_DIM = 64
NSA_HEADS = 6
NSA_KV_GROUPS = 2
NSA_REP = NSA_HEADS // NSA_KV_GROUPS
N_BRANCH = 3
CMP_LEN = 32
CMP_STRIDE = 16
CMP_HIDDEN = 2 * HEAD_DIM
SEL_BLOCK = 64
SEL_TOP_N = 16
WINDOW = 512
MLA_HEADS = 5
MLA_Q_RANK = 256
MLA_KV_RANK = 128
MLA_NOPE_DIM = 64
MLA_ROPE_DIM = 32
MLA_V_DIM = 64
RET_HEADS = 5
RET_CHUNK = 128
ROPE_THETA = 500000.0
PARTIAL_ROPE_DIM = HEAD_DIM // 4
RET_THETA = 10000.0
D_FF = 4 * D_MODEL
NORM_EPS = 1e-6
NEG_INF = -1e30
FORCE_SCORE = 1e9
LOG2E = math.log2(math.e)

NSA_Q_W = NSA_HEADS * HEAD_DIM
NSA_KV_W = NSA_KV_GROUPS * HEAD_DIM
NSA_GATE_W = NSA_HEADS * N_BRANCH
RET_W = RET_HEADS * HEAD_DIM
IN_SIZES = (NSA_Q_W, NSA_KV_W, NSA_KV_W, NSA_KV_W, NSA_KV_W, NSA_KV_W, NSA_KV_W, NSA_GATE_W,
            MLA_Q_RANK, MLA_KV_RANK, MLA_ROPE_DIM, RET_W, RET_W, RET_W, RET_W)

LANES = 128
HALF = LANES // 2
VMEM_LIMIT = 56 * 1024 * 1024

NSA_TILES = 5
GATE_ROWS = 32
NSA_T_ROWS = 2 * LANES + GATE_ROWS
MLA_IN_TILES = 4
RET_PAIRS = 3
RET_TILES = 3 * RET_PAIRS
IN_TILES = NSA_TILES + 2 + MLA_IN_TILES + RET_TILES + RET_PAIRS
N_PAD = IN_TILES * LANES
MLA_V_ROWS = MLA_HEADS * MLA_V_DIM

IN_TM = 512
MLP_TM = 512
ATT_T = 512
MLP_FF_CHUNK = 1024


def _nn(a, b):
    return jnp.dot(a, b, preferred_element_type=F32)


def _nt(a, b):
    return lax.dot_general(a, b, (((1,), (1,)), ((), ())), preferred_element_type=F32)


def _rms(x, gain):
    return x * lax.rsqrt(jnp.mean(x * x, axis=-1, keepdims=True) + NORM_EPS) * gain


def _rope(val, tab, half):
    cos = tab[:, 0:LANES]
    sin_a = tab[:, LANES:2 * LANES]
    sin_b = tab[:, 2 * LANES:3 * LANES]
    return (val * cos + pltpu.roll(val, LANES - half, 1) * sin_a
            + pltpu.roll(val, half, 1) * sin_b)


def _lane_lo(shape):
    return lax.broadcasted_iota(jnp.int32, shape, len(shape) - 1) < HALF


def _mla_up(c, qn_ref, kvn_ref, wq_ref, wk_ref, wvt_ref, tm_ref, q_ref, k_ref, vt_ref):
    scale = (MLA_NOPE_DIM + MLA_ROPE_DIM) ** -0.5 * LOG2E
    cq = _rms(c[:, 0:MLA_Q_RANK], qn_ref[...]).astype(BF16)
    ckv = _rms(c[:, MLA_Q_RANK:MLA_Q_RANK + MLA_KV_RANK], kvn_ref[...]).astype(BF16)
    k_pe = _rope(c[:, 3 * LANES:4 * LANES], tm_ref[...], MLA_ROPE_DIM // 2)
    cos = tm_ref[:, 0:LANES] * scale
    sin = (tm_ref[:, 2 * LANES:3 * LANES] - tm_ref[:, LANES:2 * LANES]) * scale
    q = _nn(cq, wq_ref[...])
    k = _nn(ckv, wk_ref[...])
    w = MLA_HEADS * LANES
    for hd in range(MLA_HEADS):
        sl = slice(hd * LANES, (hd + 1) * LANES)
        rot = slice(w + hd * LANES, w + (hd + 1) * LANES)
        q_ref[:, sl] = (q[:, sl] * cos + q[:, rot] * sin).astype(BF16)
        k_ref[:, sl] = (k[:, sl] + k_pe).astype(BF16)
    vt_ref[...] = _nt(wvt_ref[...], ckv).astype(BF16)


def _inproj_kernel(x_ref, g_ref, w_ref, wt_ref, tn_ref, tm_ref, tr_ref, qn_ref, kvn_ref, wq_ref, wk_ref,
                   wvt_ref, nsa_ref, nsat_ref, gatet_ref, kcmp_ref, vcmp_ref, ret_ref, rgate_ref,
                   mq_ref, mk_ref, mvt_ref):
    h = _rms(x_ref[...], g_ref[...]).astype(BF16)
    tab_n = tn_ref[...]
    tab_r = tr_ref[...]

    def tile(a, i):
        return a[:, i * LANES:(i + 1) * LANES]

    c0 = 0
    a = _nn(h, w_ref[:, c0:c0 + NSA_TILES * LANES])
    for i in range(NSA_TILES):
        v = _rope(tile(a, i), tab_n, PARTIAL_ROPE_DIM // 2)
        if i < NSA_REP:
            v = v * LOG2E
        nsa_ref[:, i * LANES:(i + 1) * LANES] = v.astype(BF16)
    c0 += NSA_TILES * LANES
    a = _nn(h, w_ref[:, c0:c0 + 2 * LANES])
    kcmp_ref[...] = _rope(tile(a, 0), tab_n, PARTIAL_ROPE_DIM // 2)
    vcmp_ref[...] = tile(a, 1)
    c0 += 2 * LANES
    _mla_up(_nn(h, w_ref[:, c0:c0 + MLA_IN_TILES * LANES]), qn_ref, kvn_ref, wq_ref, wk_ref, wvt_ref, tm_ref,
            mq_ref, mk_ref, mvt_ref)
    c0 += MLA_IN_TILES * LANES
    a = _nn(h, w_ref[:, c0:c0 + RET_TILES * LANES])
    for i in range(RET_TILES):
        v = tile(a, i)
        if i < 2 * RET_PAIRS:
            v = _rope(v, tab_r, HEAD_DIM // 2)
        ret_ref[:, i * LANES:(i + 1) * LANES] = v.astype(BF16)
    c0 += RET_TILES * LANES
    rgate_ref[...] = _nn(h, w_ref[:, c0:c0 + RET_PAIRS * LANES])
    at = _nt(wt_ref[...], h)
    nsat_ref[...] = at[0:2 * LANES, :].astype(BF16)
    gatet_ref[...] = at[2 * LANES:NSA_T_ROWS, :]


def _layer_spec(w, layer, **kwargs):
    zeros = (0,) * (w.ndim - 1)
    return pl.BlockSpec((None,) + w.shape[1:], lambda *_: (layer,) + zeros, **kwargs)


def _in_proj(x2, gain, w, wt, q_norm, kv_norm, wq, wk, wvt, layer, tab_n, tab_m, tab_r):
    t = x2.shape[0]
    row = lambda i: (i, 0)
    col = lambda i: (0, i)
    const = lambda i: (0, 0)
    out_shapes = (
        jax.ShapeDtypeStruct((t, NSA_TILES * LANES), BF16),
        jax.ShapeDtypeStruct((2 * LANES, t), BF16),
        jax.ShapeDtypeStruct((GATE_ROWS, t), F32),
        jax.ShapeDtypeStruct((t, LANES), F32),
        jax.ShapeDtypeStruct((t, LANES), F32),
        jax.ShapeDtypeStruct((t, RET_TILES * LANES), BF16),
        jax.ShapeDtypeStruct((t, RET_PAIRS * LANES), F32),
        jax.ShapeDtypeStruct((t, MLA_HEADS * LANES), BF16),
        jax.ShapeDtypeStruct((t, MLA_HEADS * LANES), BF16),
        jax.ShapeDtypeStruct((MLA_V_ROWS, t), BF16),
    )
    out_specs = tuple(
        pl.BlockSpec((s.shape[0], IN_TM), col) if s.shape[1] == t else pl.BlockSpec((IN_TM, s.shape[1]), row)
        for s in out_shapes)
    return pl.pallas_call(
        _inproj_kernel,
        grid=(t // IN_TM,),
        in_specs=[
            pl.BlockSpec((IN_TM, D_MODEL), row),
            pl.BlockSpec((1, D_MODEL), const),
            _layer_spec(w, layer),
            _layer_spec(wt, layer),
            pl.BlockSpec((IN_TM, 3 * LANES), row),
            pl.BlockSpec((IN_TM, 3 * LANES), row),
            pl.BlockSpec((IN_TM, 3 * LANES), row),
            pl.BlockSpec((1, MLA_Q_RANK), const),
            pl.BlockSpec((1, MLA_KV_RANK), const),
            _layer_spec(wq, layer),
            _layer_spec(wk, layer),
            _layer_spec(wvt, layer),
        ],
        out_specs=out_specs,
        out_shape=out_shapes,
        compiler_params=pltpu.CompilerParams(
            dimension_semantics=("parallel",), vmem_limit_bytes=VMEM_LIMIT),
        name="in_proj",
    )(x2, gain, w, wt, tab_n, tab_m, tab_r, q_norm, kv_norm, wq, wk, wvt)


def _compress_kernel(k_ref, v_ref, pos_ref, w1_ref, w2k_ref, w2vt_ref, kc_ref, vct_ref):
    n_blk = k_ref.shape[0] // CMP_STRIDE

    def hidden(src, i):
        lo = jnp.zeros((n_blk, NSA_KV_GROUPS * CMP_HIDDEN), F32)
        hi = jnp.zeros((n_blk, NSA_KV_GROUPS * CMP_HIDDEN), F32)
        for r in range(CMP_STRIDE):
            tok = src[pl.ds(r, n_blk, stride=CMP_STRIDE), :]
            lo = lo + _nn((tok + pos_ref[i, r:r + 1, :]).astype(BF16), w1_ref[i, r])
            r2 = CMP_STRIDE + r
            hi = hi + _nn((tok + pos_ref[i, r2:r2 + 1, :]).astype(BF16), w1_ref[i, r2])
        return jax.nn.gelu(lo + pltpu.roll(hi, n_blk - 1, 0)).astype(BF16)

    kc_ref[0] = _nn(hidden(k_ref, 0), w2k_ref[...]).astype(BF16)
    vct_ref[0] = _nt(w2vt_ref[...], hidden(v_ref, 1)).astype(BF16)


def _compress(k_cmp, v_cmp, pos, w1, w2k, w2vt, layer, batch, seq):
    b = batch
    n_blk = seq // CMP_STRIDE
    return pl.pallas_call(
        _compress_kernel,
        grid=(b,),
        in_specs=[
            pl.BlockSpec((seq, LANES), lambda i: (i, 0)),
            pl.BlockSpec((seq, LANES), lambda i: (i, 0)),
            pl.BlockSpec(pos.shape, lambda i: (0, 0, 0)),
            _layer_spec(w1, layer),
            pl.BlockSpec(w2k.shape, lambda i: (0, 0)),
            pl.BlockSpec(w2vt.shape, lambda i: (0, 0)),
        ],
        out_specs=(pl.BlockSpec((1, n_blk, LANES), lambda i: (i, 0, 0)),
                   pl.BlockSpec((1, LANES, n_blk), lambda i: (i, 0, 0))),
        out_shape=(jax.ShapeDtypeStruct((b, n_blk, LANES), BF16),
                   jax.ShapeDtypeStruct((b, LANES, n_blk), BF16)),
        compiler_params=pltpu.CompilerParams(
            dimension_semantics=("parallel",), vmem_limit_bytes=VMEM_LIMIT),
        name="nsa_compress",
    )(k_cmp, v_cmp, pos, w1, w2k, w2vt)


SCORE_LOOKAHEAD = 2
ONES_ROWS = 16


def _softmax_steps(score_fns, v_ts, states):
    n = len(score_fns)
    s_ts = [score_fns[h]() if h < SCORE_LOOKAHEAD else None for h in range(n)]
    out = []
    for h in range(n):
        if h + SCORE_LOOKAHEAD < n:
            s_ts[h + SCORE_LOOKAHEAD] = score_fns[h + SCORE_LOOKAHEAD]()
        m_old, acc_old = states[h]
        m = jnp.maximum(m_old, jnp.max(s_ts[h], axis=0, keepdims=True))
        p = jnp.exp2(s_ts[h] - m).astype(BF16)
        s_ts[h] = None
        v_ext = jnp.concatenate([v_ts[h], jnp.ones((ONES_ROWS, v_ts[h].shape[1]), BF16)], axis=0)
        out.append((m, jnp.exp2(m_old - m) * acc_old + _nn(v_ext, p)))
    return tuple(out)


def _softmax_init(cols):
    return jnp.full((1, cols), NEG_INF, F32), jnp.zeros((HEAD_DIM + ONES_ROWS, cols), F32)


def _softmax_finish(state):
    acc = state[1]
    return acc[0:HEAD_DIM] * (1.0 / jnp.maximum(acc[HEAD_DIM:HEAD_DIM + 1], 1e-30))


def _nsa_kernel(q_ref, gatet_ref, kc_ref, vct_ref, ks_ref, kw_ref, vst_ref, vwt_ref, ovl_ref,
                o_ref, out_t_ref, selb_ref):
    tq = ATT_T
    qi = pl.program_id(1)
    q0 = pl.multiple_of(qi * tq, tq)
    lo1 = _lane_lo((1, LANES))
    group_lanes = (lo1, jnp.logical_not(lo1))
    gate = jax.nn.sigmoid(gatet_ref[...])
    heads = range(NSA_HEADS)

    def q_head(h):
        j, g = h % NSA_REP, h // NSA_REP
        tile = q_ref[:, j * LANES:(j + 1) * LANES]
        return jnp.where(group_lanes[g], tile, jnp.zeros_like(tile))

    def v_rows(ref, h, k0, n):
        g = h // NSA_REP
        return ref[g * HEAD_DIM:(g + 1) * HEAD_DIM, pl.ds(k0, n)]

    def emit(h, branch, qs, nq, o_t):
        j, g = h % NSA_REP, h // NSA_REP
        r0 = j * LANES + g * HEAD_DIM
        row = h * N_BRANCH + branch
        val = gate[row:row + 1, qs:qs + nq] * o_t
        if branch == 0:
            out_t_ref[r0:r0 + HEAD_DIM, qs:qs + nq] = val
        else:
            out_t_ref[r0:r0 + HEAD_DIM, qs:qs + nq] += val

    qm = [q_head(h) for h in heads]

    n_cmp_pad = kc_ref.shape[1]
    kc = kc_ref[0]
    vct = vct_ref[0]
    n_i = lax.broadcasted_iota(jnp.int32, (n_cmp_pad, tq), 0)
    t_l = q0 + lax.broadcasted_iota(jnp.int32, (n_cmp_pad, tq), 1)
    cmask = (n_i * CMP_STRIDE + (CMP_LEN - 1)) <= t_l
    cmask_f = cmask.astype(F32)
    s_cs = [jnp.where(cmask, _nt(kc, qm[h]), NEG_INF) for h in heads]
    p_cs = [jnp.exp2(s_t - jnp.max(s_t, axis=0, keepdims=True)) * cmask_f for s_t in s_cs]
    p_cs = [p * (1.0 / jnp.maximum(jnp.sum(p, axis=0, keepdims=True), 1e-30)) for p in p_cs]
    for h in heads:
        g = h // NSA_REP
        emit(h, 0, 0, tq, _nn(vct[g * HEAD_DIM:(g + 1) * HEAD_DIM, :], p_cs[h].astype(BF16)))

    n_sel = ovl_ref.shape[0]
    needs_rank = q0 + tq > SEL_TOP_N * SEL_BLOCK

    @pl.when(jnp.logical_not(needs_rank))
    def _():
        selb_ref[...] = jnp.zeros_like(selb_ref)

    @pl.when(needs_rank)
    def _():
        m_i = lax.broadcasted_iota(jnp.int32, (n_sel, tq), 0)
        cur = jnp.right_shift(q0 + lax.broadcasted_iota(jnp.int32, (n_sel, tq), 1),
                              SEL_BLOCK.bit_length() - 1)
        valid = m_i <= cur
        forced = (m_i == 0) | (m_i == cur) | (m_i == cur - 1)
        ovl = ovl_ref[...]
        sub = 8
        m_loc = lax.broadcasted_iota(jnp.int32, (sub, tq), 0)
        for g in range(NSA_KV_GROUPS):
            psum = p_cs[g * NSA_REP]
            for h in range(g * NSA_REP + 1, (g + 1) * NSA_REP):
                psum = psum + p_cs[h]
            p_hi = psum.astype(BF16)
            p_lo = (psum - p_hi.astype(F32)).astype(BF16)
            imp = _nn(ovl, p_hi) + _nn(ovl, p_lo)
            imp = jnp.where(valid & forced, FORCE_SCORE, imp)
            imp = jnp.where(valid, imp, NEG_INF)
            parts = [imp[i:i + sub] for i in range(0, n_sel, sub)]
            ranks = [jnp.zeros((sub, tq), jnp.int32) for _ in parts]
            for mp in range(n_sel):
                row = imp[mp:mp + 1, :]
                for i, part in enumerate(parts):
                    if i * sub + sub - 1 <= mp:
                        beats = row > part
                    elif i * sub > mp:
                        beats = row >= part
                    else:
                        beats = (row > part) | ((row == part) & (m_loc + i * sub > mp))
                    ranks[i] = ranks[i] + beats.astype(jnp.int32)
            rank = jnp.concatenate(ranks, axis=0)
            selb_ref[g] = jnp.where(rank < SEL_TOP_N, 0.0, NEG_INF)

    blocks_per_chunk = tq // SEL_BLOCK

    def sel_bias(g, c, n_keys, qs, nq):
        rows = [jnp.broadcast_to(selb_ref[g, pl.ds(c * blocks_per_chunk + i, 1), qs:qs + nq], (SEL_BLOCK, nq))
                for i in range(n_keys // SEL_BLOCK)]
        return jnp.concatenate(rows, axis=0)

    def sel_chunk(c, k0, n_keys, qs, nq, states, causal):
        k = ks_ref[pl.ds(k0, n_keys), :]
        bias = [sel_bias(g, c, n_keys, qs, nq) for g in range(NSA_KV_GROUPS)]
        if causal:
            mask = (lax.broadcasted_iota(jnp.int32, (n_keys, nq), 0)
                    <= qs + lax.broadcasted_iota(jnp.int32, (n_keys, nq), 1))

        def score_fn(h):
            s_t = _nt(k, qm[h][qs:qs + nq]) + bias[h // NSA_REP]
            return jnp.where(mask, s_t, NEG_INF) if causal else s_t

        return _softmax_steps([functools.partial(score_fn, h) for h in heads],
                              [v_rows(vst_ref, h, k0, n_keys) for h in heads], states)

    states = sel_chunk(qi, q0, tq, 0, tq, tuple(_softmax_init(tq) for _ in heads), True)
    states = lax.fori_loop(
        0, qi, lambda c, st: sel_chunk(c, pl.multiple_of(c * tq, tq), tq, 0, tq, st, False), states)
    for h in heads:
        emit(h, 1, 0, tq, _softmax_finish(states[h]))

    key_i = lax.broadcasted_iota(jnp.int32, (tq, tq), 0)
    qry_i = lax.broadcasted_iota(jnp.int32, (tq, tq), 1)

    def win_chunk(k0, states, mask):
        k = kw_ref[pl.ds(k0, tq), :]
        score_fns = [lambda h=h: jnp.where(mask, _nt(k, qm[h]), NEG_INF) for h in heads]
        return _softmax_steps(score_fns, [v_rows(vwt_ref, h, k0, tq) for h in heads], states)

    states = win_chunk(q0, tuple(_softmax_init(tq) for _ in heads), key_i <= qry_i)
    for d in range(1, WINDOW // tq + 1):
        k0 = pl.multiple_of(jnp.maximum(qi - d, 0) * tq, tq)
        in_band = (key_i - d * tq > qry_i - WINDOW) & (qi >= d)
        states = win_chunk(k0, states, in_band)
    for h in heads:
        emit(h, 2, 0, tq, _softmax_finish(states[h]))

    for j in range(NSA_REP):
        o_ref[:, j * LANES:(j + 1) * LANES] = out_t_ref[j * LANES:(j + 1) * LANES, :].T.astype(BF16)


def _nsa_attention(nsa, nsa_t, gate_t, kc, vct, ovl_t, batch, seq):
    tq = ATT_T
    nq = seq // tq
    n_blk = kc.shape[1]
    qrow = lambda b, i: (b * nq + i, 0)
    return pl.pallas_call(
        _nsa_kernel,
        grid=(batch, nq),
        in_specs=[
            pl.BlockSpec((tq, NSA_REP * LANES), qrow),
            pl.BlockSpec((GATE_ROWS, tq), lambda b, i: (0, b * nq + i)),
            pl.BlockSpec((1, n_blk, LANES), lambda b, i: (b, 0, 0)),
            pl.BlockSpec((1, LANES, n_blk), lambda b, i: (b, 0, 0)),
            pl.BlockSpec((seq, LANES), lambda b, i: (b, 3)),
            pl.BlockSpec((seq, LANES), lambda b, i: (b, 4)),
            pl.BlockSpec((LANES, seq), lambda b, i: (0, b)),
            pl.BlockSpec((LANES, seq), lambda b, i: (1, b)),
            pl.BlockSpec(ovl_t.shape, lambda b, i: (0, 0)),
        ],
        out_specs=pl.BlockSpec((tq, NSA_REP * LANES), qrow),
        out_shape=jax.ShapeDtypeStruct((batch * seq, NSA_REP * LANES), BF16),
        scratch_shapes=[pltpu.VMEM((NSA_REP * LANES, tq), F32),
                        pltpu.VMEM((NSA_KV_GROUPS, seq // SEL_BLOCK, tq), F32)],
        compiler_params=pltpu.CompilerParams(
            dimension_semantics=("parallel", "arbitrary"), vmem_limit_bytes=VMEM_LIMIT),
        name="nsa_attention",
    )(nsa, gate_t, kc, vct, nsa, nsa, nsa_t, nsa_t, ovl_t)


def _mla_attn_kernel(q_ref, k_ref, vt_ref, o_ref, out_t_ref):
    tq = ATT_T
    qi = pl.program_id(1)
    q0 = pl.multiple_of(qi * tq, tq)
    heads = range(MLA_HEADS)
    qh = [q_ref[:, h * LANES:(h + 1) * LANES] for h in heads]

    def chunk(k0, n_keys, qs, nq, states, causal):
        if causal:
            mask = (lax.broadcasted_iota(jnp.int32, (n_keys, nq), 0)
                    <= qs + lax.broadcasted_iota(jnp.int32, (n_keys, nq), 1))

        def score_fn(h):
            s_t = _nt(k_ref[pl.ds(k0, n_keys), h * LANES:(h + 1) * LANES], qh[h][qs:qs + nq])
            return jnp.where(mask, s_t, NEG_INF) if causal else s_t

        v_ts = [vt_ref[h * MLA_V_DIM:(h + 1) * MLA_V_DIM, pl.ds(k0, n_keys)] for h in heads]
        return _softmax_steps([functools.partial(score_fn, h) for h in heads], v_ts, states)

    states = chunk(q0, tq, 0, tq, tuple(_softmax_init(tq) for _ in heads), True)
    states = lax.fori_loop(
        0, qi, lambda c, st: chunk(pl.multiple_of(c * tq, tq), tq, 0, tq, st, False), states)
    for h in heads:
        out_t_ref[h * MLA_V_DIM:(h + 1) * MLA_V_DIM, :] = _softmax_finish(states[h])
    out_t_ref[MLA_V_ROWS:, :] = jnp.zeros((RET_PAIRS * LANES - MLA_V_ROWS, tq), F32)
    for j in range(RET_PAIRS):
        o_ref[:, j * LANES:(j + 1) * LANES] = out_t_ref[j * LANES:(j + 1) * LANES, :].T.astype(BF16)


def _mla_attention(q, k, vt, batch, seq):
    tq = ATT_T
    nq = seq // tq
    return pl.pallas_call(
        _mla_attn_kernel,
        grid=(batch, nq),
        in_specs=[
            pl.BlockSpec((tq, MLA_HEADS * LANES), lambda b, i: (b * nq + i, 0)),
            pl.BlockSpec((seq, MLA_HEADS * LANES), lambda b, i: (b, 0)),
            pl.BlockSpec((MLA_V_ROWS, seq), lambda b, i: (0, b)),
        ],
        out_specs=pl.BlockSpec((tq, RET_PAIRS * LANES), lambda b, i: (b * nq + i, 0)),
        out_shape=jax.ShapeDtypeStruct((batch * seq, RET_PAIRS * LANES), BF16),
        scratch_shapes=[pltpu.VMEM((RET_PAIRS * LANES, tq), F32)],
        compiler_params=pltpu.CompilerParams(
            dimension_semantics=("parallel", "arbitrary"), vmem_limit_bytes=VMEM_LIMIT),
        name="mla_attention",
    )(q, k, vt)


def _tn(a, b):
    return lax.dot_general(a, b, (((0,), (0,)), ((), ())), preferred_element_type=F32)


RET_UNROLL = 16


def _retention_kernel(q_ref, k_ref, v_ref, gate_ref, gn_ref, intra_ref, rd_ref, wd_ref, cd_ref, o_ref,
                      kv_ref, st_ref):
    c_len = RET_CHUNK
    n_chunks = q_ref.shape[0] // c_len
    lo = _lane_lo((1, LANES))
    hi = jnp.logical_not(lo)
    blockdiag = (lax.broadcasted_iota(jnp.int32, (LANES, LANES), 0) < HALF) == _lane_lo((LANES, LANES))
    intra_a = intra_ref[0, 0]
    intra_b = intra_ref[0, 1]
    read_decay = rd_ref[0]
    write_decay = wd_ref[0]
    chunk_decay = cd_ref[0]
    gn = gn_ref[0]

    def half_mean(x):
        s_lo = jnp.sum(jnp.where(lo, x, 0.0), axis=-1, keepdims=True)
        s_hi = jnp.sum(jnp.where(hi, x, 0.0), axis=-1, keepdims=True)
        return jnp.where(lo, s_lo, s_hi) * (1.0 / HEAD_DIM)

    def kv_body(c, carry):
        r0 = pl.multiple_of(c * c_len, c_len)
        kc = k_ref[pl.ds(r0, c_len), :]
        kv_ref[c] = _tn((kc.astype(F32) * write_decay).astype(BF16), v_ref[pl.ds(r0, c_len), :])
        return carry

    lax.fori_loop(0, n_chunks, kv_body, 0, unroll=RET_UNROLL)

    state = jnp.zeros((LANES, LANES), F32)
    for c in range(n_chunks):
        st_ref[c] = jnp.where(blockdiag, state, 0.0).astype(BF16)
        state = state * chunk_decay + kv_ref[c]

    def out_body(step, carry):
        cs = [step * RET_UNROLL + u for u in range(RET_UNROLL)]
        rows = [pl.ds(pl.multiple_of(c * c_len, c_len), c_len) for c in cs]
        qs = [q_ref[r, :] for r in rows]
        ks = [k_ref[r, :] for r in rows]
        vs = [v_ref[r, :] for r in rows]
        zero = jnp.zeros_like(qs[0])
        sa = [(_nt(jnp.where(lo, q, zero), k) * intra_a).astype(BF16) for q, k in zip(qs, ks)]
        sb = [(_nt(jnp.where(hi, q, zero), k) * intra_b).astype(BF16) for q, k in zip(qs, ks)]
        cross = [_nn(q, st_ref[c]) * read_decay for q, c in zip(qs, cs)]
        os_ = [jnp.where(lo, _nn(a, v), _nn(b, v)) + x for a, b, v, x in zip(sa, sb, vs, cross)]
        ds_ = [o - half_mean(o) for o in os_]
        ys = [d * lax.rsqrt(half_mean(d * d) + NORM_EPS) * gn for d in ds_]
        for r, y in zip(rows, ys):
            o_ref[r, :] = (jax.nn.silu(gate_ref[r, :]) * y).astype(BF16)
        return carry

    lax.fori_loop(0, n_chunks // RET_UNROLL, out_body, 0)


def _retention(ret, rgate, gn, intra, rd, wd, cd, batch, seq):
    pair_const3 = lambda b, j: (j, 0, 0)
    return pl.pallas_call(
        _retention_kernel,
        grid=(batch, RET_PAIRS),
        in_specs=[
            pl.BlockSpec((seq, LANES), lambda b, j: (b, j)),
            pl.BlockSpec((seq, LANES), lambda b, j: (b, RET_PAIRS + j)),
            pl.BlockSpec((seq, LANES), lambda b, j: (b, 2 * RET_PAIRS + j)),
            pl.BlockSpec((seq, LANES), lambda b, j: (b, j)),
            pl.BlockSpec((1, 1, LANES), pair_const3),
            pl.BlockSpec((1, 2, RET_CHUNK, RET_CHUNK), lambda b, j: (j, 0, 0, 0)),
            pl.BlockSpec((1, RET_CHUNK, LANES), pair_const3),
            pl.BlockSpec((1, RET_CHUNK, LANES), pair_const3),
            pl.BlockSpec((1, 1, LANES), pair_const3),
        ],
        out_specs=pl.BlockSpec((seq, LANES), lambda b, j: (b, j)),
        out_shape=jax.ShapeDtypeStruct((batch * seq, RET_PAIRS * LANES), BF16),
        scratch_shapes=[pltpu.VMEM((seq // RET_CHUNK, LANES, LANES), F32),
                        pltpu.VMEM((seq // RET_CHUNK, LANES, LANES), BF16)],
        compiler_params=pltpu.CompilerParams(
            dimension_semantics=("parallel", "arbitrary"), vmem_limit_bytes=VMEM_LIMIT),
        name="retention",
    )(ret, ret, ret, rgate, gn, intra, rd, wd, cd)


def _out_mlp_kernel(x_ref, nsa_ref, mla_ref, ret_ref, wo_ref, g2_ref, wu_ref, wd_ref, gf_ref, o_ref,
                    *, final_norm):
    w = RET_PAIRS * LANES
    mixed = (_nn(nsa_ref[...], wo_ref[0:w, :]) + _nn(mla_ref[...], wo_ref[w:2 * w, :])
             + _nn(ret_ref[...], wo_ref[2 * w:3 * w, :]))
    x = x_ref[...] + mixed
    h = _rms(x, g2_ref[...]).astype(BF16)
    y = x
    for c in range(D_FF // MLP_FF_CHUNK):
        sl = slice(c * MLP_FF_CHUNK, (c + 1) * MLP_FF_CHUNK)
        u = jnp.maximum(_nn(h, wu_ref[:, sl]), 0.0)
        y = y + _nn((u * u).astype(BF16), wd_ref[sl, :])
    if final_norm:
        y = _rms(y, gf_ref[...])
    o_ref[...] = y


def _out_mlp(x2, o_nsa, o_mla, o_ret, wo, g2, wu, wd, layer, gf, final_norm):
    t = x2.shape[0]
    w = RET_PAIRS * LANES
    row = lambda i: (i, 0)
    const = lambda i: (0, 0)
    resident = dict(pipeline_mode=pl.Buffered(1))
    return pl.pallas_call(
        functools.partial(_out_mlp_kernel, final_norm=final_norm),
        grid=(t // MLP_TM,),
        in_specs=[
            pl.BlockSpec((MLP_TM, D_MODEL), row),
            pl.BlockSpec((MLP_TM, w), row),
            pl.BlockSpec((MLP_TM, w), row),
            pl.BlockSpec((MLP_TM, w), row),
            _layer_spec(wo, layer, **resident),
            pl.BlockSpec((1, D_MODEL), const),
            _layer_spec(wu, layer, **resident),
            _layer_spec(wd, layer, **resident),
            pl.BlockSpec((1, D_MODEL), const),
        ],
        out_specs=pl.BlockSpec((MLP_TM, D_MODEL), row),
        out_shape=jax.ShapeDtypeStruct((t, D_MODEL), F32),
        compiler_params=pltpu.CompilerParams(
            dimension_semantics=("parallel",), vmem_limit_bytes=VMEM_LIMIT),
        name="out_mlp",
    )(x2, o_nsa, o_mla, o_ret, wo, g2, wu, wd, gf)


ROPE_KINDS = (
    (PARTIAL_ROPE_DIM, ROPE_THETA, HEAD_DIM, 0),
    (MLA_ROPE_DIM, ROPE_THETA, LANES, HALF),
    (HEAD_DIM, RET_THETA, HEAD_DIM, 0),
)
ROPE_TM = 1024


def _rope_placement():
    n_angles = sum(dim // 2 for dim, _, _, _ in ROPE_KINDS)
    assert 2 * n_angles <= LANES
    place = np.zeros((LANES, 3 * LANES * len(ROPE_KINDS)), np.float32)
    fill = np.zeros((1, 3 * LANES * len(ROPE_KINDS)), np.float32)
    row0 = 0
    for kind, (dim, _, period, base) in enumerate(ROPE_KINDS):
        half = dim // 2
        col0 = kind * 3 * LANES
        for lane in range(LANES):
            rel = (lane - base) % period
            first = lane >= base and rel < half
            second = lane >= base and half <= rel < dim
            if first or second:
                angle = rel if first else rel - half
                place[row0 + angle, col0 + lane] = 1.0
                place[n_angles + row0 + angle, col0 + (1 if first else 2) * LANES + lane] = -1.0 if first else 1.0
            else:
                fill[0, col0 + lane] = 1.0
        row0 += half
    return jnp.asarray(place, BF16), jnp.asarray(fill)


def _rope_kernel(cs_ref, place_ref, fill_ref, *out_refs):
    x = cs_ref[...]
    x1 = x.astype(BF16)
    r1 = x - x1.astype(F32)
    x2 = r1.astype(BF16)
    x3 = (r1 - x2.astype(F32)).astype(BF16)
    place = place_ref[...]
    tab = _tn(x1, place) + _tn(x2, place) + _tn(x3, place) + fill_ref[...]
    for i, ref in enumerate(out_refs):
        ref[...] = tab[:, i * 3 * LANES:(i + 1) * 3 * LANES]


def _rope_tables(positions):
    inv = jnp.concatenate([1.0 / (theta ** (jnp.arange(0, dim, 2, dtype=F32) / dim))
                           for dim, theta, _, _ in ROPE_KINDS])
    ang = inv[:, None] * positions.reshape(-1).astype(F32)[None, :]
    compact = jnp.concatenate([jnp.cos(ang), jnp.sin(ang)], axis=0)
    compact = jnp.pad(compact, ((0, LANES - compact.shape[0]), (0, 0)))
    place, fill = _rope_placement()
    t = compact.shape[1]
    tab_shape = jax.ShapeDtypeStruct((t, 3 * LANES), F32)
    return pl.pallas_call(
        _rope_kernel,
        grid=(t // ROPE_TM,),
        in_specs=[pl.BlockSpec((LANES, ROPE_TM), lambda i: (0, i)),
                  pl.BlockSpec(place.shape, lambda i: (0, 0)),
                  pl.BlockSpec(fill.shape, lambda i: (0, 0))],
        out_specs=tuple(pl.BlockSpec((ROPE_TM, 3 * LANES), lambda i: (i, 0)) for _ in ROPE_KINDS),
        out_shape=tuple(tab_shape for _ in ROPE_KINDS),
        compiler_params=pltpu.CompilerParams(
            dimension_semantics=("parallel",), vmem_limit_bytes=VMEM_LIMIT),
        name="rope_tables",
    )(compact, place, fill)


def _pad_cols(w, n):
    return jnp.pad(w, ((0, 0), (0, 0), (0, n - w.shape[-1])))


def _in_weight(w_in):
    offs = np.cumsum((0,) + IN_SIZES)
    seg = [w_in[:, :, offs[i]:offs[i + 1]] for i in range(len(IN_SIZES))]
    (nsa_q, k_cmp, v_cmp, k_slc, v_slc, k_win, v_win, gate,
     cq, ckv, kpe, ret_q, ret_k, ret_v, ret_g) = seg
    scale = HEAD_DIM ** -0.5
    qh = [nsa_q[:, :, h * HEAD_DIM:(h + 1) * HEAD_DIM] * scale for h in range(NSA_HEADS)]
    tiles = [jnp.concatenate([qh[j], qh[j + NSA_REP]], axis=-1) for j in range(NSA_REP)]
    tiles += [k_slc, k_win, k_cmp, v_cmp]
    tiles += [cq, ckv, jnp.pad(kpe, ((0, 0), (0, 0), (HALF, LANES - HALF - MLA_ROPE_DIM)))]
    ret_w = RET_PAIRS * LANES
    tiles += [_pad_cols(ret_q, ret_w), _pad_cols(ret_k * scale, ret_w), _pad_cols(ret_v, ret_w),
              _pad_cols(ret_g, ret_w)]
    w_t = lax.optimization_barrier(
        jnp.concatenate([v_slc, v_win, _pad_cols(gate, GATE_ROWS)], axis=-1).astype(BF16))
    return jnp.concatenate(tiles, axis=-1).astype(BF16), w_t.transpose(0, 2, 1)


def _compress_weights(pos, w1, w2):
    nl = pos.shape[0]
    g, dh, hid = NSA_KV_GROUPS, HEAD_DIM, CMP_HIDDEN
    p = jnp.tile(pos, (1, 1, g))
    eye = jnp.eye(g, dtype=w1.dtype)
    w = w1.reshape(nl, CMP_LEN, dh, hid)
    w = jnp.einsum('ltdc,gk->ltgdkc', w, eye).reshape(nl, CMP_LEN, g * dh, g * hid)
    w2b = jnp.einsum('lcd,gk->lgckd', w2, eye).reshape(nl, g * hid, g * dh)
    return p, w.astype(BF16), w2b.astype(BF16)


def _mla_weights(w_uq, w_ukv):
    nl = w_uq.shape[0]
    dq = MLA_NOPE_DIM + MLA_ROPE_DIM
    wq = w_uq.reshape(nl, MLA_Q_RANK, MLA_HEADS, dq)
    pe = wq[..., MLA_NOPE_DIM:]
    half = MLA_ROPE_DIM // 2
    rot = jnp.concatenate([jnp.zeros_like(wq[..., :MLA_NOPE_DIM]), -pe[..., half:], pe[..., :half]], axis=-1)
    pad = lambda w: jnp.pad(w, ((0, 0), (0, 0), (0, 0), (0, LANES - dq))).reshape(
        nl, MLA_Q_RANK, MLA_HEADS * LANES)
    wq = jnp.concatenate([pad(wq), pad(rot)], axis=-1)
    wkv = w_ukv.reshape(nl, MLA_KV_RANK, MLA_HEADS, MLA_NOPE_DIM + MLA_V_DIM)
    wk = jnp.pad(wkv[..., :MLA_NOPE_DIM], ((0, 0), (0, 0), (0, 0), (0, LANES - MLA_NOPE_DIM)))
    wk = wk.reshape(nl, MLA_KV_RANK, MLA_HEADS * LANES)
    wvt = wkv[..., MLA_NOPE_DIM:].reshape(nl, MLA_KV_RANK, MLA_V_ROWS).transpose(0, 2, 1)
    return wq.astype(BF16), wk.astype(BF16), wvt.astype(BF16)


def _out_weight(w_out):
    nl = w_out.shape[0]
    pad_rows = lambda w: jnp.pad(w, ((0, 0), (0, RET_PAIRS * LANES - w.shape[1]), (0, 0)))
    nsa = w_out[:, :NSA_Q_W].reshape(nl, NSA_HEADS, HEAD_DIM, D_MODEL)
    order = [h for j in range(NSA_REP) for h in (j, j + NSA_REP)]
    nsa = nsa[:, order].reshape(nl, NSA_Q_W, D_MODEL)
    mla = pad_rows(w_out[:, NSA_Q_W:NSA_Q_W + MLA_HEADS * MLA_V_DIM])
    ret = pad_rows(w_out[:, NSA_Q_W + MLA_HEADS * MLA_V_DIM:])
    return jnp.concatenate([nsa, mla, ret], axis=1).astype(BF16)


def _retention_tables(gn_gain):
    nh = 2 * RET_PAIRS
    log_g = jnp.log(1.0 - 2.0 ** (-5.0 - jnp.arange(nh, dtype=F32)))
    i = jnp.arange(RET_CHUNK, dtype=F32)
    diff = i[:, None] - i[None, :]
    intra = jnp.where(diff >= 0, jnp.exp(jnp.maximum(diff, 0.0)[None] * log_g[:, None, None]), 0.0)
    read_decay = jnp.exp((i + 1.0)[None, :] * log_g[:, None])
    write_decay = jnp.exp((RET_CHUNK - 1.0 - i)[None, :] * log_g[:, None])
    chunk_decay = jnp.exp(RET_CHUNK * log_g)

    def lanes(t):
        t = t.reshape(RET_PAIRS, 2, -1)
        return jnp.repeat(t.transpose(0, 2, 1), HALF, axis=-1)

    gn = jnp.pad(gn_gain, ((0, 0), (0, nh - RET_HEADS), (0, 0)))
    gn = gn.reshape(gn.shape[0], RET_PAIRS, 1, LANES)
    return (gn, intra.reshape(RET_PAIRS, 2, RET_CHUNK, RET_CHUNK), lanes(read_decay), lanes(write_decay),
            lanes(chunk_decay[:, None]))


def _selection_overlap(seq):
    n_cmp = (seq - CMP_LEN) // CMP_STRIDE + 1
    n_sel = seq // SEL_BLOCK
    cs = np.arange(n_cmp) * CMP_STRIDE
    ss = np.arange(n_sel) * SEL_BLOCK
    ov = np.clip(np.minimum(cs[:, None] + CMP_LEN, ss[None, :] + SEL_BLOCK)
                 - np.maximum(cs[:, None], ss[None, :]), 0, None) / CMP_LEN
    ovl_t = np.zeros((n_sel, seq // CMP_STRIDE), np.float32)
    ovl_t[:, :n_cmp] = ov.T
    return jnp.asarray(ovl_t, BF16)


def kernel(x, positions, ln1_gain, w_in, cmp_pos_k, cmp_w1_k, cmp_w2_k, cmp_pos_v, cmp_w1_v, cmp_w2_v,
           mla_q_norm, mla_w_uq, mla_kv_norm, mla_w_ukv, ret_gn_gain, w_out, ln2_gain, w_up, w_down,
           final_gain):
    batch, seq, _ = x.shape
    depth = w_in.shape[0]
    t = batch * seq

    tab_n, tab_m, tab_r = _rope_tables(positions)

    w_in_p, w_in_t = _in_weight(w_in)
    pos_k, w1_k, w2_k = _compress_weights(cmp_pos_k, cmp_w1_k, cmp_w2_k)
    pos_v, w1_v, w2_v = _compress_weights(cmp_pos_v, cmp_w1_v, cmp_w2_v)
    cmp_pos = jnp.stack([pos_k, pos_v], axis=1)
    cmp_w1 = jnp.stack([w1_k, w1_v], axis=1)
    w2_vt = w2_v.transpose(0, 2, 1)
    wq, wk, wvt = _mla_weights(mla_w_uq, mla_w_ukv)
    wo = _out_weight(w_out)
    wu = w_up.astype(BF16)
    wd = w_down.astype(BF16)
    gn, intra, rd, wdec, cd = _retention_tables(ret_gn_gain)
    ovl_t = _selection_overlap(seq)
    gf = final_gain.reshape(1, D_MODEL)

    x2 = x.reshape(t, D_MODEL)
    for l in range(depth):
        nsa, nsa_t, gate_t, k_cmp, v_cmp, ret, rgate, q_m, k_m, vt_m = _in_proj(
            x2, ln1_gain[l].reshape(1, D_MODEL), w_in_p, w_in_t, mla_q_norm[l].reshape(1, -1),
            mla_kv_norm[l].reshape(1, -1), wq, wk, wvt, l, tab_n, tab_m, tab_r)
        kc, vct = _compress(k_cmp, v_cmp, cmp_pos[l], cmp_w1, w2_k[l], w2_vt[l], l, batch, seq)
        o_nsa = _nsa_attention(nsa, nsa_t, gate_t, kc, vct, ovl_t, batch, seq)
        o_mla = _mla_attention(q_m, k_m, vt_m, batch, seq)
        o_ret = _retention(ret, rgate, gn[l], intra, rd, wdec, cd, batch, seq)
        x2 = _out_mlp(x2, o_nsa, o_mla, o_ret, wo, ln2_gain[l].reshape(1, D_MODEL), wu, wd, l, gf,
                      final_norm=(l == depth - 1))
    return x2.reshape(batch, seq, D_MODEL)
```

```python
import functools
import math

import numpy as np
import jax
import jax.numpy as jnp
from jax import lax
from jax.experimental import pallas as pl
from jax.experimental.pallas import tpu as pltpu

F32 = jnp.float32
BF16 = jnp.bfloat16

D_MODEL = 1024
HEAD_DIM = 64
NSA_HEADS = 6
NSA_KV_GROUPS = 2
NSA_REP = NSA_HEADS // NSA_KV_GROUPS
N_BRANCH = 3
CMP_LEN = 32
CMP_STRIDE = 16
CMP_HIDDEN = 2 * HEAD_DIM
SEL_BLOCK = 64
SEL_TOP_N = 16
WINDOW = 512
MLA_HEADS = 5
MLA_Q_RANK = 256
MLA_KV_RANK = 128
MLA_NOPE_DIM = 64
MLA_ROPE_DIM = 32
MLA_V_DIM = 64
RET_HEADS = 5
RET_CHUNK = 128
ROPE_THETA = 500000.0
PARTIAL_ROPE_DIM = HEAD_DIM // 4
RET_THETA = 10000.0
D_FF = 4 * D_MODEL
NORM_EPS = 1e-6
NEG_INF = -1e30
FORCE_SCORE = 1e9
LOG2E = math.log2(math.e)

NSA_Q_W = NSA_HEADS * HEAD_DIM
NSA_KV_W = NSA_KV_GROUPS * HEAD_DIM
NSA_GATE_W = NSA_HEADS * N_BRANCH
RET_W = RET_HEADS * HEAD_DIM
IN_SIZES = (NSA_Q_W, NSA_KV_W, NSA_KV_W, NSA_KV_W, NSA_KV_W, NSA_KV_W, NSA_KV_W, NSA_GATE_W,
            MLA_Q_RANK, MLA_KV_RANK, MLA_ROPE_DIM, RET_W, RET_W, RET_W, RET_W)

LANES = 128
HALF = LANES // 2
VMEM_LIMIT = 56 * 1024 * 1024

NSA_TILES = 5
GATE_ROWS = 32
NSA_T_ROWS = 2 * LANES + GATE_ROWS
MLA_IN_TILES = 4
RET_PAIRS = 3
RET_TILES = 3 * RET_PAIRS
IN_TILES = NSA_TILES + 2 + MLA_IN_TILES + RET_TILES + RET_PAIRS
N_PAD = IN_TILES * LANES
MLA_V_ROWS = MLA_HEADS * MLA_V_DIM

IN_TM = 512
MLP_TM = 512
ATT_T = 512
MLP_FF_CHUNK = 1024


def _nn(a, b):
    return jnp.dot(a, b, preferred_element_type=F32)


def _nt(a, b):
    return lax.dot_general(a, b, (((1,), (1,)), ((), ())), preferred_element_type=F32)


def _rms(x, gain):
    return x * lax.rsqrt(jnp.mean(x * x, axis=-1, keepdims=True) + NORM_EPS) * gain


def _rope(val, tab, half):
    cos = tab[:, 0:LANES]
    sin_a = tab[:, LANES:2 * LANES]
    sin_b = tab[:, 2 * LANES:3 * LANES]
    return (val * cos + pltpu.roll(val, LANES - half, 1) * sin_a
            + pltpu.roll(val, half, 1) * sin_b)


def _lane_lo(shape):
    return lax.broadcasted_iota(jnp.int32, shape, len(shape) - 1) < HALF


def _mla_up(c, qn_ref, kvn_ref, wq_ref, wk_ref, wvt_ref, tm_ref, q_ref, k_ref, vt_ref):
    scale = (MLA_NOPE_DIM + MLA_ROPE_DIM) ** -0.5 * LOG2E
    cq = _rms(c[:, 0:MLA_Q_RANK], qn_ref[...]).astype(BF16)
    ckv = _rms(c[:, MLA_Q_RANK:MLA_Q_RANK + MLA_KV_RANK], kvn_ref[...]).astype(BF16)
    k_pe = _rope(c[:, 3 * LANES:4 * LANES], tm_ref[...], MLA_ROPE_DIM // 2)
    cos = tm_ref[:, 0:LANES] * scale
    sin = (tm_ref[:, 2 * LANES:3 * LANES] - tm_ref[:, LANES:2 * LANES]) * scale
    q = _nn(cq, wq_ref[...])
    k = _nn(ckv, wk_ref[...])
    w = MLA_HEADS * LANES
    for hd in range(MLA_HEADS):
        sl = slice(hd * LANES, (hd + 1) * LANES)
        rot = slice(w + hd * LANES, w + (hd + 1) * LANES)
        q_ref[:, sl] = (q[:, sl] * cos + q[:, rot] * sin).astype(BF16)
        k_ref[:, sl] = (k[:, sl] + k_pe).astype(BF16)
    vt_ref[...] = _nt(wvt_ref[...], ckv).astype(BF16)


def _inproj_kernel(x_ref, g_ref, w_ref, wt_ref, tn_ref, tm_ref, tr_ref, qn_ref, kvn_ref, wq_ref, wk_ref,
                   wvt_ref, nsa_ref, nsat_ref, gatet_ref, kcmp_ref, vcmp_ref, ret_ref, rgate_ref,
                   mq_ref, mk_ref, mvt_ref):
    h = _rms(x_ref[...], g_ref[...]).astype(BF16)
    tab_n = tn_ref[...]
    tab_r = tr_ref[...]

    def tile(a, i):
        return a[:, i * LANES:(i + 1) * LANES]

    c0 = 0
    a = _nn(h, w_ref[:, c0:c0 + NSA_TILES * LANES])
    for i in range(NSA_TILES):
        v = _rope(tile(a, i), tab_n, PARTIAL_ROPE_DIM // 2)
        if i < NSA_REP:
            v = v * LOG2E
        nsa_ref[:, i * LANES:(i + 1) * LANES] = v.astype(BF16)
    c0 += NSA_TILES * LANES
    a = _nn(h, w_ref[:, c0:c0 + 2 * LANES])
    kcmp_ref[...] = _rope(tile(a, 0), tab_n, PARTIAL_ROPE_DIM // 2)
    vcmp_ref[...] = tile(a, 1)
    c0 += 2 * LANES
    _mla_up(_nn(h, w_ref[:, c0:c0 + MLA_IN_TILES * LANES]), qn_ref, kvn_ref, wq_ref, wk_ref, wvt_ref, tm_ref,
            mq_ref, mk_ref, mvt_ref)
    c0 += MLA_IN_TILES * LANES
    a = _nn(h, w_ref[:, c0:c0 + RET_TILES * LANES])
    for i in range(RET_TILES):
        v = tile(a, i)
        if i < 2 * RET_PAIRS:
            v = _rope(v, tab_r, HEAD_DIM // 2)
        ret_ref[:, i * LANES:(i + 1) * LANES] = v.astype(BF16)
    c0 += RET_TILES * LANES
    rgate_ref[...] = _nn(h, w_ref[:, c0:c0 + RET_PAIRS * LANES])
    at = _nt(wt_ref[...], h)
    nsat_ref[...] = at[0:2 * LANES, :].astype(BF16)
    gatet_ref[...] = at[2 * LANES:NSA_T_ROWS, :]


def _layer_spec(w, layer, **kwargs):
    zeros = (0,) * (w.ndim - 1)
    return pl.BlockSpec((None,) + w.shape[1:], lambda *_: (layer,) + zeros, **kwargs)


def _in_proj(x2, gain, w, wt, q_norm, kv_norm, wq, wk, wvt, layer, tab_n, tab_m, tab_r):
    t = x2.shape[0]
    row = lambda i: (i, 0)
    col = lambda i: (0, i)
    const = lambda i: (0, 0)
    out_shapes = (
        jax.ShapeDtypeStruct((t, NSA_TILES * LANES), BF16),
        jax.ShapeDtypeStruct((2 * LANES, t), BF16),
        jax.ShapeDtypeStruct((GATE_ROWS, t), F32),
        jax.ShapeDtypeStruct((t, LANES), F32),
        jax.ShapeDtypeStruct((t, LANES), F32),
        jax.ShapeDtypeStruct((t, RET_TILES * LANES), BF16),
        jax.ShapeDtypeStruct((t, RET_PAIRS * LANES), F32),
        jax.ShapeDtypeStruct((t, MLA_HEADS * LANES), BF16),
        jax.ShapeDtypeStruct((t, MLA_HEADS * LANES), BF16),
        jax.ShapeDtypeStruct((MLA_V_ROWS, t), BF16),
    )
    out_specs = tuple(
        pl.BlockSpec((s.shape[0], IN_TM), col) if s.shape[1] == t else pl.BlockSpec((IN_TM, s.shape[1]), row)
        for s in out_shapes)
    return pl.pallas_call(
        _inproj_kernel,
        grid=(t // IN_TM,),
        in_specs=[
            pl.BlockSpec((IN_TM, D_MODEL), row),
            pl.BlockSpec((1, D_MODEL), const),
            _layer_spec(w, layer),
            _layer_spec(wt, layer),
            pl.BlockSpec((IN_TM, 3 * LANES), row),
            pl.BlockSpec((IN_TM, 3 * LANES), row),
            pl.BlockSpec((IN_TM, 3 * LANES), row),
            pl.BlockSpec((1, MLA_Q_RANK), const),
            pl.BlockSpec((1, MLA_KV_RANK), const),
            _layer_spec(wq, layer),
            _layer_spec(wk, layer),
            _layer_spec(wvt, layer),
        ],
        out_specs=out_specs,
        out_shape=out_shapes,
        compiler_params=pltpu.CompilerParams(
            dimension_semantics=("parallel",), vmem_limit_bytes=VMEM_LIMIT),
        name="in_proj",
    )(x2, gain, w, wt, tab_n, tab_m, tab_r, q_norm, kv_norm, wq, wk, wvt)


def _compress_kernel(k_ref, v_ref, pos_ref, w1_ref, w2k_ref, w2vt_ref, kc_ref, vct_ref):
    n_blk = k_ref.shape[0] // CMP_STRIDE

    def hidden(src, i):
        lo = jnp.zeros((n_blk, NSA_KV_GROUPS * CMP_HIDDEN), F32)
        hi = jnp.zeros((n_blk, NSA_KV_GROUPS * CMP_HIDDEN), F32)
        for r in range(CMP_STRIDE):
            tok = src[pl.ds(r, n_blk, stride=CMP_STRIDE), :]
            lo = lo + _nn((tok + pos_ref[i, r:r + 1, :]).astype(BF16), w1_ref[i, r])
            r2 = CMP_STRIDE + r
            hi = hi + _nn((tok + pos_ref[i, r2:r2 + 1, :]).astype(BF16), w1_ref[i, r2])
        return jax.nn.gelu(lo + pltpu.roll(hi, n_blk - 1, 0)).astype(BF16)

    kc_ref[0] = _nn(hidden(k_ref, 0), w2k_ref[...]).astype(BF16)
    vct_ref[0] = _nt(w2vt_ref[...], hidden(v_ref, 1)).astype(BF16)


def _compress(k_cmp, v_cmp, pos, w1, w2k, w2vt, layer, batch, seq):
    b = batch
    n_blk = seq // CMP_STRIDE
    return pl.pallas_call(
        _compress_kernel,
        grid=(b,),
        in_specs=[
            pl.BlockSpec((seq, LANES), lambda i: (i, 0)),
            pl.BlockSpec((seq, LANES), lambda i: (i, 0)),
            pl.BlockSpec(pos.shape, lambda i: (0, 0, 0)),
            _layer_spec(w1, layer),
            pl.BlockSpec(w2k.shape, lambda i: (0, 0)),
            pl.BlockSpec(w2vt.shape, lambda i: (0, 0)),
        ],
        out_specs=(pl.BlockSpec((1, n_blk, LANES), lambda i: (i, 0, 0)),
                   pl.BlockSpec((1, LANES, n_blk), lambda i: (i, 0, 0))),
        out_shape=(jax.ShapeDtypeStruct((b, n_blk, LANES), BF16),
                   jax.ShapeDtypeStruct((b, LANES, n_blk), BF16)),
        compiler_params=pltpu.CompilerParams(
            dimension_semantics=("parallel",), vmem_limit_bytes=VMEM_LIMIT),
        name="nsa_compress",
    )(k_cmp, v_cmp, pos, w1, w2k, w2vt)


SCORE_LOOKAHEAD = 2
ONES_ROWS = 16


def _softmax_steps(score_fns, v_ts, states):
    n = len(score_fns)
    s_ts = [score_fns[h]() if h < SCORE_LOOKAHEAD else None for h in range(n)]
    out = []
    for h in range(n):
        if h + SCORE_LOOKAHEAD < n:
            s_ts[h + SCORE_LOOKAHEAD] = score_fns[h + SCORE_LOOKAHEAD]()
        m_old, acc_old = states[h]
        m = jnp.maximum(m_old, jnp.max(s_ts[h], axis=0, keepdims=True))
        p = jnp.exp2(s_ts[h] - m).astype(BF16)
        s_ts[h] = None
        v_ext = jnp.concatenate([v_ts[h], jnp.ones((ONES_ROWS, v_ts[h].shape[1]), BF16)], axis=0)
        out.append((m, jnp.exp2(m_old - m) * acc_old + _nn(v_ext, p)))
    return tuple(out)


def _softmax_init(cols):
    return jnp.full((1, cols), NEG_INF, F32), jnp.zeros((HEAD_DIM + ONES_ROWS, cols), F32)


def _softmax_finish(state):
    acc = state[1]
    return acc[0:HEAD_DIM] * (1.0 / jnp.maximum(acc[HEAD_DIM:HEAD_DIM + 1], 1e-30))


def _nsa_kernel(q_ref, gatet_ref, kc_ref, vct_ref, ks_ref, kw_ref, vst_ref, vwt_ref, ovl_ref,
                o_ref, out_t_ref, selb_ref):
    tq = ATT_T
    qi = pl.program_id(1)
    q0 = pl.multiple_of(qi * tq, tq)
    lo1 = _lane_lo((1, LANES))
    group_lanes = (lo1, jnp.logical_not(lo1))
    gate = jax.nn.sigmoid(gatet_ref[...])
    heads = range(NSA_HEADS)

    def q_head(h):
        j, g = h % NSA_REP, h // NSA_REP
        tile = q_ref[:, j * LANES:(j + 1) * LANES]
        return jnp.where(group_lanes[g], tile, jnp.zeros_like(tile))

    def v_rows(ref, h, k0, n):
        g = h // NSA_REP
        return ref[g * HEAD_DIM:(g + 1) * HEAD_DIM, pl.ds(k0, n)]

    def emit(h, branch, qs, nq, o_t):
        j, g = h % NSA_REP, h // NSA_REP
        r0 = j * LANES + g * HEAD_DIM
        row = h * N_BRANCH + branch
        val = gate[row:row + 1, qs:qs + nq] * o_t
        if branch == 0:
            out_t_ref[r0:r0 + HEAD_DIM, qs:qs + nq] = val
        else:
            out_t_ref[r0:r0 + HEAD_DIM, qs:qs + nq] += val

    qm = [q_head(h) for h in heads]

    n_cmp_pad = kc_ref.shape[1]
    kc = kc_ref[0]
    vct = vct_ref[0]
    n_i = lax.broadcasted_iota(jnp.int32, (n_cmp_pad, tq), 0)
    t_l = q0 + lax.broadcasted_iota(jnp.int32, (n_cmp_pad, tq), 1)
    cmask = (n_i * CMP_STRIDE + (CMP_LEN - 1)) <= t_l
    cmask_f = cmask.astype(F32)
    s_cs = [jnp.where(cmask, _nt(kc, qm[h]), NEG_INF) for h in heads]
    p_cs = [jnp.exp2(s_t - jnp.max(s_t, axis=0, keepdims=True)) * cmask_f for s_t in s_cs]
    p_cs = [p * (1.0 / jnp.maximum(jnp.sum(p, axis=0, keepdims=True), 1e-30)) for p in p_cs]
    for h in heads:
        g = h // NSA_REP
        emit(h, 0, 0, tq, _nn(vct[g * HEAD_DIM:(g + 1) * HEAD_DIM, :], p_cs[h].astype(BF16)))

    n_sel = ovl_ref.shape[0]
    needs_rank = q0 + tq > SEL_TOP_N * SEL_BLOCK

    @pl.when(jnp.logical_not(needs_rank))
    def _():
        selb_ref[...] = jnp.zeros_like(selb_ref)

    @pl.when(needs_rank)
    def _():
        m_i = lax.broadcasted_iota(jnp.int32, (n_sel, tq), 0)
        cur = jnp.right_shift(q0 + lax.broadcasted_iota(jnp.int32, (n_sel, tq), 1),
                              SEL_BLOCK.bit_length() - 1)
        valid = m_i <= cur
        forced = (m_i == 0) | (m_i == cur) | (m_i == cur - 1)
        ovl = ovl_ref[...]
        sub = 8
        m_loc = lax.broadcasted_iota(jnp.int32, (sub, tq), 0)
        for g in range(NSA_KV_GROUPS):
            psum = p_cs[g * NSA_REP]
            for h in range(g * NSA_REP + 1, (g + 1) * NSA_REP):
                psum = psum + p_cs[h]
            p_hi = psum.astype(BF16)
            p_lo = (psum - p_hi.astype(F32)).astype(BF16)
            imp = _nn(ovl, p_hi) + _nn(ovl, p_lo)
            imp = jnp.where(valid & forced, FORCE_SCORE, imp)
            imp = jnp.where(valid, imp, NEG_INF)
            parts = [imp[i:i + sub] for i in range(0, n_sel, sub)]
            ranks = [jnp.zeros((sub, tq), jnp.int32) for _ in parts]
            for mp in range(n_sel):
                row = imp[mp:mp + 1, :]
                for i, part in enumerate(parts):
                    if i * sub + sub - 1 <= mp:
                        beats = row > part
                    elif i * sub > mp:
                        beats = row >= part
                    else:
                        beats = (row > part) | ((row == part) & (m_loc + i * sub > mp))
                    ranks[i] = ranks[i] + beats.astype(jnp.int32)
            rank = jnp.concatenate(ranks, axis=0)
            selb_ref[g] = jnp.where(rank < SEL_TOP_N, 0.0, NEG_INF)

    blocks_per_chunk = tq // SEL_BLOCK

    def sel_bias(g, c, n_keys, qs, nq):
        rows = [jnp.broadcast_to(selb_ref[g, pl.ds(c * blocks_per_chunk + i, 1), qs:qs + nq], (SEL_BLOCK, nq))
                for i in range(n_keys // SEL_BLOCK)]
        return jnp.concatenate(rows, axis=0)

    def sel_chunk(c, k0, n_keys, qs, nq, states, causal):
        k = ks_ref[pl.ds(k0, n_keys), :]
        bias = [sel_bias(g, c, n_keys, qs, nq) for g in range(NSA_KV_GROUPS)]
        if causal:
            mask = (lax.broadcasted_iota(jnp.int32, (n_keys, nq), 0)
                    <= qs + lax.broadcasted_iota(jnp.int32, (n_keys, nq), 1))

        def score_fn(h):
            s_t = _nt(k, qm[h][qs:qs + nq]) + bias[h // NSA_REP]
            return jnp.where(mask, s_t, NEG_INF) if causal else s_t

        return _softmax_steps([functools.partial(score_fn, h) for h in heads],
                              [v_rows(vst_ref, h, k0, n_keys) for h in heads], states)

    states = sel_chunk(qi, q0, tq, 0, tq, tuple(_softmax_init(tq) for _ in heads), True)
    states = lax.fori_loop(
        0, qi, lambda c, st: sel_chunk(c, pl.multiple_of(c * tq, tq), tq, 0, tq, st, False), states)
    for h in heads:
        emit(h, 1, 0, tq, _softmax_finish(states[h]))

    key_i = lax.broadcasted_iota(jnp.int32, (tq, tq), 0)
    qry_i = lax.broadcasted_iota(jnp.int32, (tq, tq), 1)

    def win_chunk(k0, states, mask):
        k = kw_ref[pl.ds(k0, tq), :]
        score_fns = [lambda h=h: jnp.where(mask, _nt(k, qm[h]), NEG_INF) for h in heads]
        return _softmax_steps(score_fns, [v_rows(vwt_ref, h, k0, tq) for h in heads], states)

    states = win_chunk(q0, tuple(_softmax_init(tq) for _ in heads), key_i <= qry_i)
    for d in range(1, WINDOW // tq + 1):
        k0 = pl.multiple_of(jnp.maximum(qi - d, 0) * tq, tq)
        in_band = (key_i - d * tq > qry_i - WINDOW) & (qi >= d)
        states = win_chunk(k0, states, in_band)
    for h in heads:
        emit(h, 2, 0, tq, _softmax_finish(states[h]))

    for j in range(NSA_REP):
        o_ref[:, j * LANES:(j + 1) * LANES] = out_t_ref[j * LANES:(j + 1) * LANES, :].T.astype(BF16)


def _nsa_attention(nsa, nsa_t, gate_t, kc, vct, ovl_t, batch, seq):
    tq = ATT_T
    nq = seq // tq
    n_blk = kc.shape[1]
    qrow = lambda b, i: (b * nq + i, 0)
    return pl.pallas_call(
        _nsa_kernel,
        grid=(batch, nq),
        in_specs=[
            pl.BlockSpec((tq, NSA_REP * LANES), qrow),
            pl.BlockSpec((GATE_ROWS, tq), lambda b, i: (0, b * nq + i)),
            pl.BlockSpec((1, n_blk, LANES), lambda b, i: (b, 0, 0)),
            pl.BlockSpec((1, LANES, n_blk), lambda b, i: (b, 0, 0)),
            pl.BlockSpec((seq, LANES), lambda b, i: (b, 3)),
            pl.BlockSpec((seq, LANES), lambda b, i: (b, 4)),
            pl.BlockSpec((LANES, seq), lambda b, i: (0, b)),
            pl.BlockSpec((LANES, seq), lambda b, i: (1, b)),
            pl.BlockSpec(ovl_t.shape, lambda b, i: (0, 0)),
        ],
        out_specs=pl.BlockSpec((tq, NSA_REP * LANES), qrow),
        out_shape=jax.ShapeDtypeStruct((batch * seq, NSA_REP * LANES), BF16),
        scratch_shapes=[pltpu.VMEM((NSA_REP * LANES, tq), F32),
                        pltpu.VMEM((NSA_KV_GROUPS, seq // SEL_BLOCK, tq), F32)],
        compiler_params=pltpu.CompilerParams(
            dimension_semantics=("parallel", "arbitrary"), vmem_limit_bytes=VMEM_LIMIT),
        name="nsa_attention",
    )(nsa, gate_t, kc, vct, nsa, nsa, nsa_t, nsa_t, ovl_t)


def _mla_attn_kernel(q_ref, k_ref, vt_ref, o_ref, out_t_ref):
    tq = ATT_T
    qi = pl.program_id(1)
    q0 = pl.multiple_of(qi * tq, tq)
    heads = range(MLA_HEADS)
    qh = [q_ref[:, h * LANES:(h + 1) * LANES] for h in heads]

    def chunk(k0, n_keys, qs, nq, states, causal):
        if causal:
            mask = (lax.broadcasted_iota(jnp.int32, (n_keys, nq), 0)
                    <= qs + lax.broadcasted_iota(jnp.int32, (n_keys, nq), 1))

        def score_fn(h):
            s_t = _nt(k_ref[pl.ds(k0, n_keys), h * LANES:(h + 1) * LANES], qh[h][qs:qs + nq])
            return jnp.where(mask, s_t, NEG_INF) if causal else s_t

        v_ts = [vt_ref[h * MLA_V_DIM:(h + 1) * MLA_V_DIM, pl.ds(k0, n_keys)] for h in heads]
        return _softmax_steps([functools.partial(score_fn, h) for h in heads], v_ts, states)

    states = chunk(q0, tq, 0, tq, tuple(_softmax_init(tq) for _ in heads), True)
    states = lax.fori_loop(
        0, qi, lambda c, st: chunk(pl.multiple_of(c * tq, tq), tq, 0, tq, st, False), states)
    for h in heads:
        out_t_ref[h * MLA_V_DIM:(h + 1) * MLA_V_DIM, :] = _softmax_finish(states[h])
    out_t_ref[MLA_V_ROWS:, :] = jnp.zeros((RET_PAIRS * LANES - MLA_V_ROWS, tq), F32)
    for j in range(RET_PAIRS):
        o_ref[:, j * LANES:(j + 1) * LANES] = out_t_ref[j * LANES:(j + 1) * LANES, :].T.astype(BF16)


def _mla_attention(q, k, vt, batch, seq):
    tq = ATT_T
    nq = seq // tq
    return pl.pallas_call(
        _mla_attn_kernel,
        grid=(batch, nq),
        in_specs=[
            pl.BlockSpec((tq, MLA_HEADS * LANES), lambda b, i: (b * nq + i, 0)),
            pl.BlockSpec((seq, MLA_HEADS * LANES), lambda b, i: (b, 0)),
            pl.BlockSpec((MLA_V_ROWS, seq), lambda b, i: (0, b)),
        ],
        out_specs=pl.BlockSpec((tq, RET_PAIRS * LANES), lambda b, i: (b * nq + i, 0)),
        out_shape=jax.ShapeDtypeStruct((batch * seq, RET_PAIRS * LANES), BF16),
        scratch_shapes=[pltpu.VMEM((RET_PAIRS * LANES, tq), F32)],
        compiler_params=pltpu.CompilerParams(
            dimension_semantics=("parallel", "arbitrary"), vmem_limit_bytes=VMEM_LIMIT),
        name="mla_attention",
    )(q, k, vt)


def _tn(a, b):
    return lax.dot_general(a, b, (((0,), (0,)), ((), ())), preferred_element_type=F32)


RET_UNROLL = 16


def _retention_kernel(q_ref, k_ref, v_ref, gate_ref, gn_ref, intra_ref, rd_ref, wd_ref, cd_ref, o_ref,
                      kv_ref, st_ref):
    c_len = RET_CHUNK
    n_chunks = q_ref.shape[0] // c_len
    lo = _lane_lo((1, LANES))
    hi = jnp.logical_not(lo)
    blockdiag = (lax.broadcasted_iota(jnp.int32, (LANES, LANES), 0) < HALF) == _lane_lo((LANES, LANES))
    intra_a = intra_ref[0, 0]
    intra_b = intra_ref[0, 1]
    read_decay = rd_ref[0]
    write_decay = wd_ref[0]
    chunk_decay = cd_ref[0]
    gn = gn_ref[0]

    averager = jnp.where(blockdiag, 1.0 / HEAD_DIM, 0.0).astype(BF16)

    def half_mean(x):
        x_hi = x.astype(BF16)
        x_lo = (x - x_hi.astype(F32)).astype(BF16)
        return _nn(x_hi, averager) + _nn(x_lo, averager)

    def kv_body(c, carry):
        r0 = pl.multiple_of(c * c_len, c_len)
        kc = k_ref[pl.ds(r0, c_len), :]
        kv_ref[c] = _tn((kc.astype(F32) * write_decay).astype(BF16), v_ref[pl.ds(r0, c_len), :])
        return carry

    lax.fori_loop(0, n_chunks, kv_body, 0, unroll=RET_UNROLL)

    state = jnp.zeros((LANES, LANES), F32)
    for c in range(n_chunks):
        st_ref[c] = jnp.where(blockdiag, state, 0.0).astype(BF16)
        state = state * chunk_decay + kv_ref[c]

    def out_body(step, carry):
        cs = [step * RET_UNROLL + u for u in range(RET_UNROLL)]
        rows = [pl.ds(pl.multiple_of(c * c_len, c_len), c_len) for c in cs]
        qs = [q_ref[r, :] for r in rows]
        ks = [k_ref[r, :] for r in rows]
        vs = [v_ref[r, :] for r in rows]
        zero = jnp.zeros_like(qs[0])
        sa = [(_nt(jnp.where(lo, q, zero), k) * intra_a).astype(BF16) for q, k in zip(qs, ks)]
        sb = [(_nt(jnp.where(hi, q, zero), k) * intra_b).astype(BF16) for q, k in zip(qs, ks)]
        cross = [_nn(q, st_ref[c]) * read_decay for q, c in zip(qs, cs)]
        os_ = [jnp.where(lo, _nn(a, v), _nn(b, v)) + x for a, b, v, x in zip(sa, sb, vs, cross)]
        ds_ = [o - half_mean(o) for o in os_]
        ys = [d * lax.rsqrt(half_mean(d * d) + NORM_EPS) * gn for d in ds_]
        for r, y in zip(rows, ys):
            o_ref[r, :] = (jax.nn.silu(gate_ref[r, :]) * y).astype(BF16)
        return carry

    lax.fori_loop(0, n_chunks // RET_UNROLL, out_body, 0)


def _retention(ret, rgate, gn, intra, rd, wd, cd, batch, seq):
    pair_const3 = lambda b, j: (j, 0, 0)
    return pl.pallas_call(
        _retention_kernel,
        grid=(batch, RET_PAIRS),
        in_specs=[
            pl.BlockSpec((seq, LANES), lambda b, j: (b, j)),
            pl.BlockSpec((seq, LANES), lambda b, j: (b, RET_PAIRS + j)),
            pl.BlockSpec((seq, LANES), lambda b, j: (b, 2 * RET_PAIRS + j)),
            pl.BlockSpec((seq, LANES), lambda b, j: (b, j)),
            pl.BlockSpec((1, 1, LANES), pair_const3),
            pl.BlockSpec((1, 2, RET_CHUNK, RET_CHUNK), lambda b, j: (j, 0, 0, 0)),
            pl.BlockSpec((1, RET_CHUNK, LANES), pair_const3),
            pl.BlockSpec((1, RET_CHUNK, LANES), pair_const3),
            pl.BlockSpec((1, 1, LANES), pair_const3),
        ],
        out_specs=pl.BlockSpec((seq, LANES), lambda b, j: (b, j)),
        out_shape=jax.ShapeDtypeStruct((batch * seq, RET_PAIRS * LANES), BF16),
        scratch_shapes=[pltpu.VMEM((seq // RET_CHUNK, LANES, LANES), F32),
                        pltpu.VMEM((seq // RET_CHUNK, LANES, LANES), BF16)],
        compiler_params=pltpu.CompilerParams(
            dimension_semantics=("parallel", "arbitrary"), vmem_limit_bytes=VMEM_LIMIT),
        name="retention",
    )(ret, ret, ret, rgate, gn, intra, rd, wd, cd)


def _out_mlp_kernel(x_ref, nsa_ref, mla_ref, ret_ref, wo_ref, g2_ref, wu_ref, wd_ref, gf_ref, o_ref,
                    *, final_norm):
    w = RET_PAIRS * LANES
    mixed = (_nn(nsa_ref[...], wo_ref[0:w, :]) + _nn(mla_ref[...], wo_ref[w:2 * w, :])
             + _nn(ret_ref[...], wo_ref[2 * w:3 * w, :]))
    x = x_ref[...] + mixed
    h = _rms(x, g2_ref[...]).astype(BF16)
    y = x
    for c in range(D_FF // MLP_FF_CHUNK):
        sl = slice(c * MLP_FF_CHUNK, (c + 1) * MLP_FF_CHUNK)
        u = jnp.maximum(_nn(h, wu_ref[:, sl]), 0.0)
        y = y + _nn((u * u).astype(BF16), wd_ref[sl, :])
    if final_norm:
        y = _rms(y, gf_ref[...])
    o_ref[...] = y


def _out_mlp(x2, o_nsa, o_mla, o_ret, wo, g2, wu, wd, layer, gf, final_norm):
    t = x2.shape[0]
    w = RET_PAIRS * LANES
    row = lambda i: (i, 0)
    const = lambda i: (0, 0)
    resident = dict(pipeline_mode=pl.Buffered(1))
    return pl.pallas_call(
        functools.partial(_out_mlp_kernel, final_norm=final_norm),
        grid=(t // MLP_TM,),
        in_specs=[
            pl.BlockSpec((MLP_TM, D_MODEL), row),
            pl.BlockSpec((MLP_TM, w), row),
            pl.BlockSpec((MLP_TM, w), row),
            pl.BlockSpec((MLP_TM, w), row),
            _layer_spec(wo, layer, **resident),
            pl.BlockSpec((1, D_MODEL), const),
            _layer_spec(wu, layer, **resident),
            _layer_spec(wd, layer, **resident),
            pl.BlockSpec((1, D_MODEL), const),
        ],
        out_specs=pl.BlockSpec((MLP_TM, D_MODEL), row),
        out_shape=jax.ShapeDtypeStruct((t, D_MODEL), F32),
        compiler_params=pltpu.CompilerParams(
            dimension_semantics=("parallel",), vmem_limit_bytes=VMEM_LIMIT),
        name="out_mlp",
    )(x2, o_nsa, o_mla, o_ret, wo, g2, wu, wd, gf)


ROPE_KINDS = (
    (PARTIAL_ROPE_DIM, ROPE_THETA, HEAD_DIM, 0),
    (MLA_ROPE_DIM, ROPE_THETA, LANES, HALF),
    (HEAD_DIM, RET_THETA, HEAD_DIM, 0),
)
ROPE_TM = 1024


def _rope_placement():
    n_angles = sum(dim // 2 for dim, _, _, _ in ROPE_KINDS)
    assert 2 * n_angles <= LANES
    place = np.zeros((LANES, 3 * LANES * len(ROPE_KINDS)), np.float32)
    fill = np.zeros((1, 3 * LANES * len(ROPE_KINDS)), np.float32)
    row0 = 0
    for kind, (dim, _, period, base) in enumerate(ROPE_KINDS):
        half = dim // 2
        col0 = kind * 3 * LANES
        for lane in range(LANES):
            rel = (lane - base) % period
            first = lane >= base and rel < half
            second = lane >= base and half <= rel < dim
            if first or second:
                angle = rel if first else rel - half
                place[row0 + angle, col0 + lane] = 1.0
                place[n_angles + row0 + angle, col0 + (1 if first else 2) * LANES + lane] = -1.0 if first else 1.0
            else:
                fill[0, col0 + lane] = 1.0
        row0 += half
    return jnp.asarray(place, BF16), jnp.asarray(fill)


def _rope_kernel(cs_ref, place_ref, fill_ref, *out_refs):
    x = cs_ref[...]
    x1 = x.astype(BF16)
    r1 = x - x1.astype(F32)
    x2 = r1.astype(BF16)
    x3 = (r1 - x2.astype(F32)).astype(BF16)
    place = place_ref[...]
    tab = _tn(x1, place) + _tn(x2, place) + _tn(x3, place) + fill_ref[...]
    for i, ref in enumerate(out_refs):
        ref[...] = tab[:, i * 3 * LANES:(i + 1) * 3 * LANES]


def _rope_tables(positions):
    inv = jnp.concatenate([1.0 / (theta ** (jnp.arange(0, dim, 2, dtype=F32) / dim))
                           for dim, theta, _, _ in ROPE_KINDS])
    ang = inv[:, None] * positions.reshape(-1).astype(F32)[None, :]
    compact = jnp.concatenate([jnp.cos(ang), jnp.sin(ang)], axis=0)
    compact = jnp.pad(compact, ((0, LANES - compact.shape[0]), (0, 0)))
    place, fill = _rope_placement()
    t = compact.shape[1]
    tab_shape = jax.ShapeDtypeStruct((t, 3 * LANES), F32)
    return pl.pallas_call(
        _rope_kernel,
        grid=(t // ROPE_TM,),
        in_specs=[pl.BlockSpec((LANES, ROPE_TM), lambda i: (0, i)),
                  pl.BlockSpec(place.shape, lambda i: (0, 0)),
                  pl.BlockSpec(fill.shape, lambda i: (0, 0))],
        out_specs=tuple(pl.BlockSpec((ROPE_TM, 3 * LANES), lambda i: (i, 0)) for _ in ROPE_KINDS),
        out_shape=tuple(tab_shape for _ in ROPE_KINDS),
        compiler_params=pltpu.CompilerParams(
            dimension_semantics=("parallel",), vmem_limit_bytes=VMEM_LIMIT),
        name="rope_tables",
    )(compact, place, fill)


def _pad_cols(w, n):
    return jnp.pad(w, ((0, 0), (0, 0), (0, n - w.shape[-1])))


def _in_weight(w_in):
    offs = np.cumsum((0,) + IN_SIZES)
    seg = [w_in[:, :, offs[i]:offs[i + 1]] for i in range(len(IN_SIZES))]
    (nsa_q, k_cmp, v_cmp, k_slc, v_slc, k_win, v_win, gate,
     cq, ckv, kpe, ret_q, ret_k, ret_v, ret_g) = seg
    scale = HEAD_DIM ** -0.5
    qh = [nsa_q[:, :, h * HEAD_DIM:(h + 1) * HEAD_DIM] * scale for h in range(NSA_HEADS)]
    tiles = [jnp.concatenate([qh[j], qh[j + NSA_REP]], axis=-1) for j in range(NSA_REP)]
    tiles += [k_slc, k_win, k_cmp, v_cmp]
    tiles += [cq, ckv, jnp.pad(kpe, ((0, 0), (0, 0), (HALF, LANES - HALF - MLA_ROPE_DIM)))]
    ret_w = RET_PAIRS * LANES
    tiles += [_pad_cols(ret_q, ret_w), _pad_cols(ret_k * scale, ret_w), _pad_cols(ret_v, ret_w),
              _pad_cols(ret_g, ret_w)]
    w_t = lax.optimization_barrier(
        jnp.concatenate([v_slc, v_win, _pad_cols(gate, GATE_ROWS)], axis=-1).astype(BF16))
    return jnp.concatenate(tiles, axis=-1).astype(BF16), w_t.transpose(0, 2, 1)


def _compress_weights(pos, w1, w2):
    nl = pos.shape[0]
    g, dh, hid = NSA_KV_GROUPS, HEAD_DIM, CMP_HIDDEN
    assert g == 2
    p = jnp.tile(pos, (1, 1, g))

    def block_diag2(w):
        lead = ((0, 0),) * (w.ndim - 1)
        return jnp.concatenate([jnp.pad(w, lead + ((0, w.shape[-1]),)),
                                jnp.pad(w, lead + ((w.shape[-1], 0),))], axis=-2)

    w = block_diag2(w1.astype(BF16).reshape(nl, CMP_LEN, dh, hid))
    return p, w, block_diag2(w2.astype(BF16))


def _mla_weights(w_uq, w_ukv):
    nl = w_uq.shape[0]
    dq = MLA_NOPE_DIM + MLA_ROPE_DIM
    wq = w_uq.reshape(nl, MLA_Q_RANK, MLA_HEADS, dq)
    pe = wq[..., MLA_NOPE_DIM:]
    half = MLA_ROPE_DIM // 2
    rot = jnp.concatenate([jnp.zeros_like(wq[..., :MLA_NOPE_DIM]), -pe[..., half:], pe[..., :half]], axis=-1)
    pad = lambda w: jnp.pad(w, ((0, 0), (0, 0), (0, 0), (0, LANES - dq))).reshape(
        nl, MLA_Q_RANK, MLA_HEADS * LANES)
    wq = jnp.concatenate([pad(wq), pad(rot)], axis=-1)
    wkv = w_ukv.reshape(nl, MLA_KV_RANK, MLA_HEADS, MLA_NOPE_DIM + MLA_V_DIM)
    wk = jnp.pad(wkv[..., :MLA_NOPE_DIM], ((0, 0), (0, 0), (0, 0), (0, LANES - MLA_NOPE_DIM)))
    wk = wk.reshape(nl, MLA_KV_RANK, MLA_HEADS * LANES)
    wvt = wkv[..., MLA_NOPE_DIM:].reshape(nl, MLA_KV_RANK, MLA_V_ROWS).transpose(0, 2, 1)
    return wq.astype(BF16), wk.astype(BF16), wvt.astype(BF16)


def _out_weight(w_out):
    nl = w_out.shape[0]
    pad_rows = lambda w: jnp.pad(w, ((0, 0), (0, RET_PAIRS * LANES - w.shape[1]), (0, 0)))
    nsa = w_out[:, :NSA_Q_W].reshape(nl, NSA_HEADS, HEAD_DIM, D_MODEL)
    order = [h for j in range(NSA_REP) for h in (j, j + NSA_REP)]
    nsa = nsa[:, order].reshape(nl, NSA_Q_W, D_MODEL)
    mla = pad_rows(w_out[:, NSA_Q_W:NSA_Q_W + MLA_HEADS * MLA_V_DIM])
    ret = pad_rows(w_out[:, NSA_Q_W + MLA_HEADS * MLA_V_DIM:])
    return jnp.concatenate([nsa, mla, ret], axis=1).astype(BF16)


def _retention_tables(gn_gain):
    nh = 2 * RET_PAIRS
    log_g = jnp.log(1.0 - 2.0 ** (-5.0 - jnp.arange(nh, dtype=F32)))
    i = jnp.arange(RET_CHUNK, dtype=F32)
    diff = i[:, None] - i[None, :]
    intra = jnp.where(diff >= 0, jnp.exp(jnp.maximum(diff, 0.0)[None] * log_g[:, None, None]), 0.0)
    read_decay = jnp.exp((i + 1.0)[None, :] * log_g[:, None])
    write_decay = jnp.exp((RET_CHUNK - 1.0 - i)[None, :] * log_g[:, None])
    chunk_decay = jnp.exp(RET_CHUNK * log_g)

    def lanes(t):
        t = t.reshape(RET_PAIRS, 2, -1)
        return jnp.repeat(t.transpose(0, 2, 1), HALF, axis=-1)

    gn = jnp.pad(gn_gain, ((0, 0), (0, nh - RET_HEADS), (0, 0)))
    gn = gn.reshape(gn.shape[0], RET_PAIRS, 1, LANES)
    return (gn, intra.reshape(RET_PAIRS, 2, RET_CHUNK, RET_CHUNK), lanes(read_decay), lanes(write_decay),
            lanes(chunk_decay[:, None]))


def _selection_overlap(seq):
    n_cmp = (seq - CMP_LEN) // CMP_STRIDE + 1
    n_sel = seq // SEL_BLOCK
    cs = np.arange(n_cmp) * CMP_STRIDE
    ss = np.arange(n_sel) * SEL_BLOCK
    ov = np.clip(np.minimum(cs[:, None] + CMP_LEN, ss[None, :] + SEL_BLOCK)
                 - np.maximum(cs[:, None], ss[None, :]), 0, None) / CMP_LEN
    ovl_t = np.zeros((n_sel, seq // CMP_STRIDE), np.float32)
    ovl_t[:, :n_cmp] = ov.T
    return jnp.asarray(ovl_t, BF16)


def kernel(x, positions, ln1_gain, w_in, cmp_pos_k, cmp_w1_k, cmp_w2_k, cmp_pos_v, cmp_w1_v, cmp_w2_v,
           mla_q_norm, mla_w_uq, mla_kv_norm, mla_w_ukv, ret_gn_gain, w_out, ln2_gain, w_up, w_down,
           final_gain):
    batch, seq, _ = x.shape
    depth = w_in.shape[0]
    t = batch * seq

    tab_n, tab_m, tab_r = _rope_tables(positions)

    w_in_p, w_in_t = _in_weight(w_in)
    pos_k, w1_k, w2_k = _compress_weights(cmp_pos_k, cmp_w1_k, cmp_w2_k)
    pos_v, w1_v, w2_v = _compress_weights(cmp_pos_v, cmp_w1_v, cmp_w2_v)
    cmp_pos = jnp.stack([pos_k, pos_v], axis=1)
    cmp_w1 = jnp.stack([w1_k, w1_v], axis=1)
    w2_vt = w2_v.transpose(0, 2, 1)
    wq, wk, wvt = _mla_weights(mla_w_uq, mla_w_ukv)
    wo = _out_weight(w_out)
    wu = w_up.astype(BF16)
    wd = w_down.astype(BF16)
    gn, intra, rd, wdec, cd = _retention_tables(ret_gn_gain)
    ovl_t = _selection_overlap(seq)
    gf = final_gain.reshape(1, D_MODEL)

    x2 = x.reshape(t, D_MODEL)
    for l in range(depth):
        nsa, nsa_t, gate_t, k_cmp, v_cmp, ret, rgate, q_m, k_m, vt_m = _in_proj(
            x2, ln1_gain[l].reshape(1, D_MODEL), w_in_p, w_in_t, mla_q_norm[l].reshape(1, -1),
            mla_kv_norm[l].reshape(1, -1), wq, wk, wvt, l, tab_n, tab_m, tab_r)
        kc, vct = _compress(k_cmp, v_cmp, cmp_pos[l], cmp_w1, w2_k[l], w2_vt[l], l, batch, seq)
        o_nsa = _nsa_attention(nsa, nsa_t, gate_t, kc, vct, ovl_t, batch, seq)
        o_mla = _mla_attention(q_m, k_m, vt_m, batch, seq)
        o_ret = _retention(ret, rgate, gn[l], intra, rd, wdec, cd, batch, seq)
        x2 = _out_mlp(x2, o_nsa, o_mla, o_ret, wo, ln2_gain[l].reshape(1, D_MODEL), wu, wd, l, gf,
                      final_norm=(l == depth - 1))
    return x2.reshape(batch, seq, D_MODEL)
```

```python
import functools
import math

import numpy as np
import jax
import jax.numpy as jnp
from jax import lax
from jax.experimental import pallas as pl
from jax.experimental.pallas import tpu as pltpu

F32 = jnp.float32
BF16 = jnp.bfloat16

D_MODEL = 1024
HEAD_DIM = 64
NSA_HEADS = 6
NSA_KV_GROUPS = 2
NSA_REP = NSA_HEADS // NSA_KV_GROUPS
N_BRANCH = 3
CMP_LEN = 32
CMP_STRIDE = 16
CMP_HIDDEN = 2 * HEAD_DIM
SEL_BLOCK = 64
SEL_TOP_N = 16
WINDOW = 512
MLA_HEADS = 5
MLA_Q_RANK = 256
MLA_KV_RANK = 128
MLA_NOPE_DIM = 64
MLA_ROPE_DIM = 32
MLA_V_DIM = 64
RET_HEADS = 5
RET_CHUNK = 128
ROPE_THETA = 500000.0
PARTIAL_ROPE_DIM = HEAD_DIM // 4
RET_THETA = 10000.0
D_FF = 4 * D_MODEL
NORM_EPS = 1e-6
NEG_INF = -1e30
FORCE_SCORE = 1e9
LOG2E = math.log2(math.e)

NSA_Q_W = NSA_HEADS * HEAD_DIM
NSA_KV_W = NSA_KV_GROUPS * HEAD_DIM
NSA_GATE_W = NSA_HEADS * N_BRANCH
RET_W = RET_HEADS * HEAD_DIM
IN_SIZES = (NSA_Q_W, NSA_KV_W, NSA_KV_W, NSA_KV_W, NSA_KV_W, NSA_KV_W, NSA_KV_W, NSA_GATE_W,
            MLA_Q_RANK, MLA_KV_RANK, MLA_ROPE_DIM, RET_W, RET_W, RET_W, RET_W)

LANES = 128
HALF = LANES // 2
VMEM_LIMIT = 56 * 1024 * 1024

NSA_TILES = 5
GATE_ROWS = 32
NSA_T_ROWS = 2 * LANES + GATE_ROWS
MLA_IN_TILES = 4
RET_PAIRS = 3
RET_TILES = 3 * RET_PAIRS
IN_TILES = NSA_TILES + 2 + MLA_IN_TILES + RET_TILES + RET_PAIRS
N_PAD = IN_TILES * LANES
MLA_V_ROWS = MLA_HEADS * MLA_V_DIM

IN_TM = 512
MLP_TM = 512
ATT_T = 512
MLP_FF_CHUNK = 1024


def _nn(a, b):
    return jnp.dot(a, b, preferred_element_type=F32)


def _nt(a, b):
    return lax.dot_general(a, b, (((1,), (1,)), ((), ())), preferred_element_type=F32)


def _rms(x, gain):
    return x * lax.rsqrt(jnp.mean(x * x, axis=-1, keepdims=True) + NORM_EPS) * gain


def _rope(val, tab, half):
    cos = tab[:, 0:LANES]
    sin_a = tab[:, LANES:2 * LANES]
    sin_b = tab[:, 2 * LANES:3 * LANES]
    return (val * cos + pltpu.roll(val, LANES - half, 1) * sin_a
            + pltpu.roll(val, half, 1) * sin_b)


def _lane_lo(shape):
    return lax.broadcasted_iota(jnp.int32, shape, len(shape) - 1) < HALF


def _mla_up(c, qn_ref, kvn_ref, wq_ref, wk_ref, wvt_ref, tm_ref, q_ref, k_ref, vt_ref):
    scale = (MLA_NOPE_DIM + MLA_ROPE_DIM) ** -0.5 * LOG2E
    cq = _rms(c[:, 0:MLA_Q_RANK], qn_ref[...]).astype(BF16)
    ckv = _rms(c[:, MLA_Q_RANK:MLA_Q_RANK + MLA_KV_RANK], kvn_ref[...]).astype(BF16)
    k_pe = _rope(c[:, 3 * LANES:4 * LANES], tm_ref[...], MLA_ROPE_DIM // 2)
    cos = tm_ref[:, 0:LANES] * scale
    sin = (tm_ref[:, 2 * LANES:3 * LANES] - tm_ref[:, LANES:2 * LANES]) * scale
    q = _nn(cq, wq_ref[...])
    k = _nn(ckv, wk_ref[...])
    w = MLA_HEADS * LANES
    for hd in range(MLA_HEADS):
        sl = slice(hd * LANES, (hd + 1) * LANES)
        rot = slice(w + hd * LANES, w + (hd + 1) * LANES)
        q_ref[:, sl] = (q[:, sl] * cos + q[:, rot] * sin).astype(BF16)
        k_ref[:, sl] = (k[:, sl] + k_pe).astype(BF16)
    vt_ref[...] = _nt(wvt_ref[...], ckv).astype(BF16)


def _inproj_kernel(x_ref, g_ref, w_ref, wt_ref, tn_ref, tm_ref, tr_ref, qn_ref, kvn_ref, wq_ref, wk_ref,
                   wvt_ref, nsa_ref, nsat_ref, gatet_ref, kcmp_ref, vcmp_ref, ret_ref, rgate_ref,
                   mq_ref, mk_ref, mvt_ref):
    h = _rms(x_ref[...], g_ref[...]).astype(BF16)
    tab_n = tn_ref[...]
    tab_r = tr_ref[...]

    def tile(a, i):
        return a[:, i * LANES:(i + 1) * LANES]

    c0 = 0
    a = _nn(h, w_ref[:, c0:c0 + NSA_TILES * LANES])
    for i in range(NSA_TILES):
        v = _rope(tile(a, i), tab_n, PARTIAL_ROPE_DIM // 2)
        if i < NSA_REP:
            v = v * LOG2E
        nsa_ref[:, i * LANES:(i + 1) * LANES] = v.astype(BF16)
    c0 += NSA_TILES * LANES
    a = _nn(h, w_ref[:, c0:c0 + 2 * LANES])
    kcmp_ref[...] = _rope(tile(a, 0), tab_n, PARTIAL_ROPE_DIM // 2)
    vcmp_ref[...] = tile(a, 1)
    c0 += 2 * LANES
    _mla_up(_nn(h, w_ref[:, c0:c0 + MLA_IN_TILES * LANES]), qn_ref, kvn_ref, wq_ref, wk_ref, wvt_ref, tm_ref,
            mq_ref, mk_ref, mvt_ref)
    c0 += MLA_IN_TILES * LANES
    a = _nn(h, w_ref[:, c0:c0 + RET_TILES * LANES])
    for i in range(RET_TILES):
        v = tile(a, i)
        if i < 2 * RET_PAIRS:
            v = _rope(v, tab_r, HEAD_DIM // 2)
        ret_ref[:, i * LANES:(i + 1) * LANES] = v.astype(BF16)
    c0 += RET_TILES * LANES
    rgate_ref[...] = _nn(h, w_ref[:, c0:c0 + RET_PAIRS * LANES])
    at = _nt(wt_ref[...], h)
    nsat_ref[...] = at[0:2 * LANES, :].astype(BF16)
    gatet_ref[...] = at[2 * LANES:NSA_T_ROWS, :]


def _layer_spec(w, layer, **kwargs):
    zeros = (0,) * (w.ndim - 1)
    return pl.BlockSpec((None,) + w.shape[1:], lambda *_: (layer,) + zeros, **kwargs)


def _in_proj(x2, gain, w, wt, q_norm, kv_norm, wq, wk, wvt, layer, tab_n, tab_m, tab_r):
    t = x2.shape[0]
    row = lambda i: (i, 0)
    col = lambda i: (0, i)
    const = lambda i: (0, 0)
    out_shapes = (
        jax.ShapeDtypeStruct((t, NSA_TILES * LANES), BF16),
        jax.ShapeDtypeStruct((2 * LANES, t), BF16),
        jax.ShapeDtypeStruct((GATE_ROWS, t), F32),
        jax.ShapeDtypeStruct((t, LANES), F32),
        jax.ShapeDtypeStruct((t, LANES), F32),
        jax.ShapeDtypeStruct((t, RET_TILES * LANES), BF16),
        jax.ShapeDtypeStruct((t, RET_PAIRS * LANES), F32),
        jax.ShapeDtypeStruct((t, MLA_HEADS * LANES), BF16),
        jax.ShapeDtypeStruct((t, MLA_HEADS * LANES), BF16),
        jax.ShapeDtypeStruct((MLA_V_ROWS, t), BF16),
    )
    out_specs = tuple(
        pl.BlockSpec((s.shape[0], IN_TM), col) if s.shape[1] == t else pl.BlockSpec((IN_TM, s.shape[1]), row)
        for s in out_shapes)
    return pl.pallas_call(
        _inproj_kernel,
        grid=(t // IN_TM,),
        in_specs=[
            pl.BlockSpec((IN_TM, D_MODEL), row),
            pl.BlockSpec((1, D_MODEL), const),
            _layer_spec(w, layer),
            _layer_spec(wt, layer),
            pl.BlockSpec((IN_TM, 3 * LANES), row),
            pl.BlockSpec((IN_TM, 3 * LANES), row),
            pl.BlockSpec((IN_TM, 3 * LANES), row),
            pl.BlockSpec((1, MLA_Q_RANK), const),
            pl.BlockSpec((1, MLA_KV_RANK), const),
            _layer_spec(wq, layer),
            _layer_spec(wk, layer),
            _layer_spec(wvt, layer),
        ],
        out_specs=out_specs,
        out_shape=out_shapes,
        compiler_params=pltpu.CompilerParams(
            dimension_semantics=("parallel",), vmem_limit_bytes=VMEM_LIMIT),
        name="in_proj",
    )(x2, gain, w, wt, tab_n, tab_m, tab_r, q_norm, kv_norm, wq, wk, wvt)


def _compress_kernel(k_ref, v_ref, pos_ref, w1_ref, w2k_ref, w2vt_ref, kc_ref, vct_ref):
    n_blk = k_ref.shape[0] // CMP_STRIDE

    def hidden(src, i):
        lo = jnp.zeros((n_blk, NSA_KV_GROUPS * CMP_HIDDEN), F32)
        hi = jnp.zeros((n_blk, NSA_KV_GROUPS * CMP_HIDDEN), F32)
        for r in range(CMP_STRIDE):
            tok = src[pl.ds(r, n_blk, stride=CMP_STRIDE), :]
            lo = lo + _nn((tok + pos_ref[i, r:r + 1, :]).astype(BF16), w1_ref[i, r])
            r2 = CMP_STRIDE + r
            hi = hi + _nn((tok + pos_ref[i, r2:r2 + 1, :]).astype(BF16), w1_ref[i, r2])
        return jax.nn.gelu(lo + pltpu.roll(hi, n_blk - 1, 0)).astype(BF16)

    kc_ref[0] = _nn(hidden(k_ref, 0), w2k_ref[...]).astype(BF16)
    vct_ref[0] = _nt(w2vt_ref[...], hidden(v_ref, 1)).astype(BF16)


def _compress(k_cmp, v_cmp, pos, w1, w2k, w2vt, layer, batch, seq):
    b = batch
    n_blk = seq // CMP_STRIDE
    return pl.pallas_call(
        _compress_kernel,
        grid=(b,),
        in_specs=[
            pl.BlockSpec((seq, LANES), lambda i: (i, 0)),
            pl.BlockSpec((seq, LANES), lambda i: (i, 0)),
            pl.BlockSpec(pos.shape, lambda i: (0, 0, 0)),
            _layer_spec(w1, layer),
            pl.BlockSpec(w2k.shape, lambda i: (0, 0)),
            pl.BlockSpec(w2vt.shape, lambda i: (0, 0)),
        ],
        out_specs=(pl.BlockSpec((1, n_blk, LANES), lambda i: (i, 0, 0)),
                   pl.BlockSpec((1, LANES, n_blk), lambda i: (i, 0, 0))),
        out_shape=(jax.ShapeDtypeStruct((b, n_blk, LANES), BF16),
                   jax.ShapeDtypeStruct((b, LANES, n_blk), BF16)),
        compiler_params=pltpu.CompilerParams(
            dimension_semantics=("parallel",), vmem_limit_bytes=VMEM_LIMIT),
        name="nsa_compress",
    )(k_cmp, v_cmp, pos, w1, w2k, w2vt)


SCORE_LOOKAHEAD = 2
ONES_ROWS = 16


def _tri_scores(k, q, dead_upper, mask, bias=None):
    half = k.shape[0] // 2
    dead = jnp.full((half, half), NEG_INF, F32)
    if dead_upper:
        top = _nt(k[:half], q)
        bottom = jnp.concatenate([dead, _nt(k[half:], q[half:])], axis=1)
    else:
        top = jnp.concatenate([_nt(k[:half], q[:half]), dead], axis=1)
        bottom = _nt(k[half:], q)
    s_t = jnp.concatenate([top, bottom], axis=0)
    return jnp.where(mask, s_t if bias is None else s_t + bias, NEG_INF)


def _tri_pv(dead_upper, v_ext, p):
    half = p.shape[0] // 2
    if dead_upper:
        full = _nn(v_ext[:, :half], p[:half])
        part = _nn(v_ext[:, half:], p[half:, half:])
        return jnp.concatenate([full[:, :half], full[:, half:] + part], axis=1)
    part = _nn(v_ext[:, :half], p[:half, :half])
    full = _nn(v_ext[:, half:], p[half:])
    return jnp.concatenate([full[:, :half] + part, full[:, half:]], axis=1)


def _softmax_steps(score_fns, v_ts, states, pv_fn=_nn):
    n = len(score_fns)
    s_ts = [score_fns[h]() if h < SCORE_LOOKAHEAD else None for h in range(n)]
    out = []
    for h in range(n):
        if h + SCORE_LOOKAHEAD < n:
            s_ts[h + SCORE_LOOKAHEAD] = score_fns[h + SCORE_LOOKAHEAD]()
        m_old, acc_old = states[h]
        m = jnp.maximum(m_old, jnp.max(s_ts[h], axis=0, keepdims=True))
        p = jnp.exp2(s_ts[h] - m).astype(BF16)
        s_ts[h] = None
        v_ext = jnp.concatenate([v_ts[h], jnp.ones((ONES_ROWS, v_ts[h].shape[1]), BF16)], axis=0)
        out.append((m, jnp.exp2(m_old - m) * acc_old + pv_fn(v_ext, p)))
    return tuple(out)


def _softmax_init(cols):
    return jnp.full((1, cols), NEG_INF, F32), jnp.zeros((HEAD_DIM + ONES_ROWS, cols), F32)


def _softmax_finish(state):
    acc = state[1]
    return acc[0:HEAD_DIM] * (1.0 / jnp.maximum(acc[HEAD_DIM:HEAD_DIM + 1], 1e-30))


def _nsa_kernel(q_ref, gatet_ref, kc_ref, vct_ref, ks_ref, kw_ref, vst_ref, vwt_ref, ovl_ref,
                o_ref, out_t_ref, selb_ref):
    tq = ATT_T
    qi = pl.program_id(1)
    q0 = pl.multiple_of(qi * tq, tq)
    lo1 = _lane_lo((1, LANES))
    group_lanes = (lo1, jnp.logical_not(lo1))
    gate = jax.nn.sigmoid(gatet_ref[...])
    heads = range(NSA_HEADS)

    def q_head(h):
        j, g = h % NSA_REP, h // NSA_REP
        tile = q_ref[:, j * LANES:(j + 1) * LANES]
        return jnp.where(group_lanes[g], tile, jnp.zeros_like(tile))

    def v_rows(ref, h, k0, n):
        g = h // NSA_REP
        return ref[g * HEAD_DIM:(g + 1) * HEAD_DIM, pl.ds(k0, n)]

    def emit(h, branch, qs, nq, o_t):
        j, g = h % NSA_REP, h // NSA_REP
        r0 = j * LANES + g * HEAD_DIM
        row = h * N_BRANCH + branch
        val = gate[row:row + 1, qs:qs + nq] * o_t
        if branch == 0:
            out_t_ref[r0:r0 + HEAD_DIM, qs:qs + nq] = val
        else:
            out_t_ref[r0:r0 + HEAD_DIM, qs:qs + nq] += val

    qm = [q_head(h) for h in heads]

    n_cmp_pad = kc_ref.shape[1]
    kc = kc_ref[0]
    vct = vct_ref[0]
    n_i = lax.broadcasted_iota(jnp.int32, (n_cmp_pad, tq), 0)
    t_l = q0 + lax.broadcasted_iota(jnp.int32, (n_cmp_pad, tq), 1)
    cmask = (n_i * CMP_STRIDE + (CMP_LEN - 1)) <= t_l
    cmask_f = cmask.astype(F32)
    s_cs = [jnp.where(cmask, _nt(kc, qm[h]), NEG_INF) for h in heads]
    p_cs = [jnp.exp2(s_t - jnp.max(s_t, axis=0, keepdims=True)) * cmask_f for s_t in s_cs]
    p_cs = [p * (1.0 / jnp.maximum(jnp.sum(p, axis=0, keepdims=True), 1e-30)) for p in p_cs]
    for h in heads:
        g = h // NSA_REP
        emit(h, 0, 0, tq, _nn(vct[g * HEAD_DIM:(g + 1) * HEAD_DIM, :], p_cs[h].astype(BF16)))

    n_sel = ovl_ref.shape[0]
    needs_rank = q0 + tq > SEL_TOP_N * SEL_BLOCK

    @pl.when(jnp.logical_not(needs_rank))
    def _():
        selb_ref[...] = jnp.zeros_like(selb_ref)

    @pl.when(needs_rank)
    def _():
        m_i = lax.broadcasted_iota(jnp.int32, (n_sel, tq), 0)
        cur = jnp.right_shift(q0 + lax.broadcasted_iota(jnp.int32, (n_sel, tq), 1),
                              SEL_BLOCK.bit_length() - 1)
        valid = m_i <= cur
        forced = (m_i == 0) | (m_i == cur) | (m_i == cur - 1)
        ovl = ovl_ref[...]
        sub = 8
        m_loc = lax.broadcasted_iota(jnp.int32, (sub, tq), 0)
        for g in range(NSA_KV_GROUPS):
            psum = p_cs[g * NSA_REP]
            for h in range(g * NSA_REP + 1, (g + 1) * NSA_REP):
                psum = psum + p_cs[h]
            p_hi = psum.astype(BF16)
            p_lo = (psum - p_hi.astype(F32)).astype(BF16)
            imp = _nn(ovl, p_hi) + _nn(ovl, p_lo)
            imp = jnp.where(valid & forced, FORCE_SCORE, imp)
            imp = jnp.where(valid, imp, NEG_INF)
            parts = [imp[i:i + sub] for i in range(0, n_sel, sub)]
            ranks = [jnp.zeros((sub, tq), jnp.int32) for _ in parts]
            for mp in range(n_sel):
                row = imp[mp:mp + 1, :]
                for i, part in enumerate(parts):
                    if i * sub + sub - 1 <= mp:
                        beats = row > part
                    elif i * sub > mp:
                        beats = row >= part
                    else:
                        beats = (row > part) | ((row == part) & (m_loc + i * sub > mp))
                    ranks[i] = ranks[i] + beats.astype(jnp.int32)
            rank = jnp.concatenate(ranks, axis=0)
            selb_ref[g] = jnp.where(rank < SEL_TOP_N, 0.0, NEG_INF)

    blocks_per_chunk = tq // SEL_BLOCK

    def sel_bias(g, c, n_keys, qs, nq):
        rows = [jnp.broadcast_to(selb_ref[g, pl.ds(c * blocks_per_chunk + i, 1), qs:qs + nq], (SEL_BLOCK, nq))
                for i in range(n_keys // SEL_BLOCK)]
        return jnp.concatenate(rows, axis=0)

    key_i = lax.broadcasted_iota(jnp.int32, (tq, tq), 0)
    qry_i = lax.broadcasted_iota(jnp.int32, (tq, tq), 1)
    causal_mask = key_i <= qry_i

    def sel_chunk(c, k0, states, causal):
        k = ks_ref[pl.ds(k0, tq), :]
        bias = [sel_bias(g, c, tq, 0, tq) for g in range(NSA_KV_GROUPS)]

        def score_fn(h):
            if causal:
                return _tri_scores(k, qm[h], True, causal_mask, bias[h // NSA_REP])
            return _nt(k, qm[h]) + bias[h // NSA_REP]

        return _softmax_steps([functools.partial(score_fn, h) for h in heads],
                              [v_rows(vst_ref, h, k0, tq) for h in heads], states,
                              functools.partial(_tri_pv, True) if causal else _nn)

    states = sel_chunk(qi, q0, tuple(_softmax_init(tq) for _ in heads), True)
    states = lax.fori_loop(
        0, qi, lambda c, st: sel_chunk(c, pl.multiple_of(c * tq, tq), st, False), states)
    for h in heads:
        emit(h, 1, 0, tq, _softmax_finish(states[h]))

    def win_chunk(k0, states, mask, dead_upper):
        k = kw_ref[pl.ds(k0, tq), :]
        if dead_upper is None:
            score_fns = [lambda h=h: jnp.where(mask, _nt(k, qm[h]), NEG_INF) for h in heads]
            pv_fn = _nn
        else:
            score_fns = [lambda h=h: _tri_scores(k, qm[h], dead_upper, mask) for h in heads]
            pv_fn = functools.partial(_tri_pv, dead_upper)
        return _softmax_steps(score_fns, [v_rows(vwt_ref, h, k0, tq) for h in heads], states, pv_fn)

    states = win_chunk(q0, tuple(_softmax_init(tq) for _ in heads), causal_mask, True)
    for d in range(1, WINDOW // tq + 1):
        k0 = pl.multiple_of(jnp.maximum(qi - d, 0) * tq, tq)
        in_band = (key_i - d * tq > qry_i - WINDOW) & (qi >= d)
        states = win_chunk(k0, states, in_band, False if d * tq == WINDOW else None)
    for h in heads:
        emit(h, 2, 0, tq, _softmax_finish(states[h]))

    for j in range(NSA_REP):
        o_ref[:, j * LANES:(j + 1) * LANES] = out_t_ref[j * LANES:(j + 1) * LANES, :].T.astype(BF16)


def _nsa_attention(nsa, nsa_t, gate_t, kc, vct, ovl_t, batch, seq):
    tq = ATT_T
    nq = seq // tq
    n_blk = kc.shape[1]
    qrow = lambda b, i: (b * nq + i, 0)
    return pl.pallas_call(
        _nsa_kernel,
        grid=(batch, nq),
        in_specs=[
            pl.BlockSpec((tq, NSA_REP * LANES), qrow),
            pl.BlockSpec((GATE_ROWS, tq), lambda b, i: (0, b * nq + i)),
            pl.BlockSpec((1, n_blk, LANES), lambda b, i: (b, 0, 0)),
            pl.BlockSpec((1, LANES, n_blk), lambda b, i: (b, 0, 0)),
            pl.BlockSpec((seq, LANES), lambda b, i: (b, 3)),
            pl.BlockSpec((seq, LANES), lambda b, i: (b, 4)),
            pl.BlockSpec((LANES, seq), lambda b, i: (0, b)),
            pl.BlockSpec((LANES, seq), lambda b, i: (1, b)),
            pl.BlockSpec(ovl_t.shape, lambda b, i: (0, 0)),
        ],
        out_specs=pl.BlockSpec((tq, NSA_REP * LANES), qrow),
        out_shape=jax.ShapeDtypeStruct((batch * seq, NSA_REP * LANES), BF16),
        scratch_shapes=[pltpu.VMEM((NSA_REP * LANES, tq), F32),
                        pltpu.VMEM((NSA_KV_GROUPS, seq // SEL_BLOCK, tq), F32)],
        compiler_params=pltpu.CompilerParams(
            dimension_semantics=("parallel", "arbitrary"), vmem_limit_bytes=VMEM_LIMIT),
        name="nsa_attention",
    )(nsa, gate_t, kc, vct, nsa, nsa, nsa_t, nsa_t, ovl_t)


def _mla_attn_kernel(q_ref, k_ref, vt_ref, o_ref, out_t_ref):
    tq = ATT_T
    qi = pl.program_id(1)
    q0 = pl.multiple_of(qi * tq, tq)
    heads = range(MLA_HEADS)
    qh = [q_ref[:, h * LANES:(h + 1) * LANES] for h in heads]

    def chunk(k0, states, causal):
        if causal:
            mask = (lax.broadcasted_iota(jnp.int32, (tq, tq), 0)
                    <= lax.broadcasted_iota(jnp.int32, (tq, tq), 1))

        def score_fn(h):
            k = k_ref[pl.ds(k0, tq), h * LANES:(h + 1) * LANES]
            return _tri_scores(k, qh[h], True, mask) if causal else _nt(k, qh[h])

        v_ts = [vt_ref[h * MLA_V_DIM:(h + 1) * MLA_V_DIM, pl.ds(k0, tq)] for h in heads]
        return _softmax_steps([functools.partial(score_fn, h) for h in heads], v_ts, states,
                              functools.partial(_tri_pv, True) if causal else _nn)

    states = chunk(q0, tuple(_softmax_init(tq) for _ in heads), True)
    states = lax.fori_loop(0, qi, lambda c, st: chunk(pl.multiple_of(c * tq, tq), st, False), states)
    for h in heads:
        out_t_ref[h * MLA_V_DIM:(h + 1) * MLA_V_DIM, :] = _softmax_finish(states[h])
    out_t_ref[MLA_V_ROWS:, :] = jnp.zeros((RET_PAIRS * LANES - MLA_V_ROWS, tq), F32)
    for j in range(RET_PAIRS):
        o_ref[:, j * LANES:(j + 1) * LANES] = out_t_ref[j * LANES:(j + 1) * LANES, :].T.astype(BF16)


def _mla_attention(q, k, vt, batch, seq):
    tq = ATT_T
    nq = seq // tq
    return pl.pallas_call(
        _mla_attn_kernel,
        grid=(batch, nq),
        in_specs=[
            pl.BlockSpec((tq, MLA_HEADS * LANES), lambda b, i: (b * nq + i, 0)),
            pl.BlockSpec((seq, MLA_HEADS * LANES), lambda b, i: (b, 0)),
            pl.BlockSpec((MLA_V_ROWS, seq), lambda b, i: (0, b)),
        ],
        out_specs=pl.BlockSpec((tq, RET_PAIRS * LANES), lambda b, i: (b * nq + i, 0)),
        out_shape=jax.ShapeDtypeStruct((batch * seq, RET_PAIRS * LANES), BF16),
        scratch_shapes=[pltpu.VMEM((RET_PAIRS * LANES, tq), F32)],
        compiler_params=pltpu.CompilerParams(
            dimension_semantics=("parallel", "arbitrary"), vmem_limit_bytes=VMEM_LIMIT),
        name="mla_attention",
    )(q, k, vt)


def _tn(a, b):
    return lax.dot_general(a, b, (((0,), (0,)), ((), ())), preferred_element_type=F32)


RET_UNROLL = 16


def _retention_kernel(q_ref, k_ref, v_ref, gate_ref, gn_ref, intra_ref, rd_ref, wd_ref, cd_ref, o_ref,
                      kv_ref, st_ref):
    c_len = RET_CHUNK
    n_chunks = q_ref.shape[0] // c_len
    lo = _lane_lo((1, LANES))
    hi = jnp.logical_not(lo)
    blockdiag = (lax.broadcasted_iota(jnp.int32, (LANES, LANES), 0) < HALF) == _lane_lo((LANES, LANES))
    intra_a = intra_ref[0, 0]
    intra_b = intra_ref[0, 1]
    read_decay = rd_ref[0]
    write_decay = wd_ref[0]
    chunk_decay = cd_ref[0]
    gn = gn_ref[0]

    averager = jnp.where(blockdiag, 1.0 / HEAD_DIM, 0.0).astype(BF16)

    def half_mean(x):
        x_hi = x.astype(BF16)
        x_lo = (x - x_hi.astype(F32)).astype(BF16)
        return _nn(x_hi, averager) + _nn(x_lo, averager)

    def kv_body(c, carry):
        r0 = pl.multiple_of(c * c_len, c_len)
        kc = k_ref[pl.ds(r0, c_len), :]
        kv_ref[c] = _tn((kc.astype(F32) * write_decay).astype(BF16), v_ref[pl.ds(r0, c_len), :])
        return carry

    lax.fori_loop(0, n_chunks, kv_body, 0, unroll=RET_UNROLL)

    state = jnp.zeros((LANES, LANES), F32)
    for c in range(n_chunks):
        st_ref[c] = jnp.where(blockdiag, state, 0.0).astype(BF16)
        state = state * chunk_decay + kv_ref[c]

    def out_body(step, carry):
        cs = [step * RET_UNROLL + u for u in range(RET_UNROLL)]
        rows = [pl.ds(pl.multiple_of(c * c_len, c_len), c_len) for c in cs]
        qs = [q_ref[r, :] for r in rows]
        ks = [k_ref[r, :] for r in rows]
        vs = [v_ref[r, :] for r in rows]
        zero = jnp.zeros_like(qs[0])
        sa = [(_nt(jnp.where(lo, q, zero), k) * intra_a).astype(BF16) for q, k in zip(qs, ks)]
        sb = [(_nt(jnp.where(hi, q, zero), k) * intra_b).astype(BF16) for q, k in zip(qs, ks)]
        cross = [_nn(q, st_ref[c]) * read_decay for q, c in zip(qs, cs)]
        os_ = [jnp.where(lo, _nn(a, v), _nn(b, v)) + x for a, b, v, x in zip(sa, sb, vs, cross)]
        ds_ = [o - half_mean(o) for o in os_]
        ys = [d * lax.rsqrt(half_mean(d * d) + NORM_EPS) * gn for d in ds_]
        for r, y in zip(rows, ys):
            o_ref[r, :] = (jax.nn.silu(gate_ref[r, :]) * y).astype(BF16)
        return carry

    lax.fori_loop(0, n_chunks // RET_UNROLL, out_body, 0)


def _retention(ret, rgate, gn, intra, rd, wd, cd, batch, seq):
    pair_const3 = lambda b, j: (j, 0, 0)
    return pl.pallas_call(
        _retention_kernel,
        grid=(batch, RET_PAIRS),
        in_specs=[
            pl.BlockSpec((seq, LANES), lambda b, j: (b, j)),
            pl.BlockSpec((seq, LANES), lambda b, j: (b, RET_PAIRS + j)),
            pl.BlockSpec((seq, LANES), lambda b, j: (b, 2 * RET_PAIRS + j)),
            pl.BlockSpec((seq, LANES), lambda b, j: (b, j)),
            pl.BlockSpec((1, 1, LANES), pair_const3),
            pl.BlockSpec((1, 2, RET_CHUNK, RET_CHUNK), lambda b, j: (j, 0, 0, 0)),
            pl.BlockSpec((1, RET_CHUNK, LANES), pair_const3),
            pl.BlockSpec((1, RET_CHUNK, LANES), pair_const3),
            pl.BlockSpec((1, 1, LANES), pair_const3),
        ],
        out_specs=pl.BlockSpec((seq, LANES), lambda b, j: (b, j)),
        out_shape=jax.ShapeDtypeStruct((batch * seq, RET_PAIRS * LANES), BF16),
        scratch_shapes=[pltpu.VMEM((seq // RET_CHUNK, LANES, LANES), F32),
                        pltpu.VMEM((seq // RET_CHUNK, LANES, LANES), BF16)],
        compiler_params=pltpu.CompilerParams(
            dimension_semantics=("parallel", "arbitrary"), vmem_limit_bytes=VMEM_LIMIT),
        name="retention",
    )(ret, ret, ret, rgate, gn, intra, rd, wd, cd)


def _out_mlp_kernel(x_ref, nsa_ref, mla_ref, ret_ref, wo_ref, g2_ref, wu_ref, wd_ref, gf_ref, o_ref,
                    *, final_norm):
    w = RET_PAIRS * LANES
    mixed = (_nn(nsa_ref[...], wo_ref[0:w, :]) + _nn(mla_ref[...], wo_ref[w:2 * w, :])
             + _nn(ret_ref[...], wo_ref[2 * w:3 * w, :]))
    x = x_ref[...] + mixed
    h = _rms(x, g2_ref[...]).astype(BF16)
    y = x
    for c in range(D_FF // MLP_FF_CHUNK):
        sl = slice(c * MLP_FF_CHUNK, (c + 1) * MLP_FF_CHUNK)
        u = jnp.maximum(_nn(h, wu_ref[:, sl]), 0.0)
        y = y + _nn((u * u).astype(BF16), wd_ref[sl, :])
    if final_norm:
        y = _rms(y, gf_ref[...])
    o_ref[...] = y


def _out_mlp(x2, o_nsa, o_mla, o_ret, wo, g2, wu, wd, layer, gf, final_norm):
    t = x2.shape[0]
    w = RET_PAIRS * LANES
    row = lambda i: (i, 0)
    const = lambda i: (0, 0)
    resident = dict(pipeline_mode=pl.Buffered(1))
    return pl.pallas_call(
        functools.partial(_out_mlp_kernel, final_norm=final_norm),
        grid=(t // MLP_TM,),
        in_specs=[
            pl.BlockSpec((MLP_TM, D_MODEL), row),
            pl.BlockSpec((MLP_TM, w), row),
            pl.BlockSpec((MLP_TM, w), row),
            pl.BlockSpec((MLP_TM, w), row),
            _layer_spec(wo, layer, **resident),
            pl.BlockSpec((1, D_MODEL), const),
            _layer_spec(wu, layer, **resident),
            _layer_spec(wd, layer, **resident),
            pl.BlockSpec((1, D_MODEL), const),
        ],
        out_specs=pl.BlockSpec((MLP_TM, D_MODEL), row),
        out_shape=jax.ShapeDtypeStruct((t, D_MODEL), F32),
        compiler_params=pltpu.CompilerParams(
            dimension_semantics=("parallel",), vmem_limit_bytes=VMEM_LIMIT),
        name="out_mlp",
    )(x2, o_nsa, o_mla, o_ret, wo, g2, wu, wd, gf)


ROPE_KINDS = (
    (PARTIAL_ROPE_DIM, ROPE_THETA, HEAD_DIM, 0),
    (MLA_ROPE_DIM, ROPE_THETA, LANES, HALF),
    (HEAD_DIM, RET_THETA, HEAD_DIM, 0),
)
ROPE_TM = 1024


def _rope_placement():
    n_angles = sum(dim // 2 for dim, _, _, _ in ROPE_KINDS)
    assert 2 * n_angles <= LANES
    place = np.zeros((LANES, 3 * LANES * len(ROPE_KINDS)), np.float32)
    fill = np.zeros((1, 3 * LANES * len(ROPE_KINDS)), np.float32)
    row0 = 0
    for kind, (dim, _, period, base) in enumerate(ROPE_KINDS):
        half = dim // 2
        col0 = kind * 3 * LANES
        for lane in range(LANES):
            rel = (lane - base) % period
            first = lane >= base and rel < half
            second = lane >= base and half <= rel < dim
            if first or second:
                angle = rel if first else rel - half
                place[row0 + angle, col0 + lane] = 1.0
                place[n_angles + row0 + angle, col0 + (1 if first else 2) * LANES + lane] = -1.0 if first else 1.0
            else:
                fill[0, col0 + lane] = 1.0
        row0 += half
    return jnp.asarray(place, BF16), jnp.asarray(fill)


def _rope_kernel(cs_ref, place_ref, fill_ref, *out_refs):
    x = cs_ref[...]
    x1 = x.astype(BF16)
    r1 = x - x1.astype(F32)
    x2 = r1.astype(BF16)
    x3 = (r1 - x2.astype(F32)).astype(BF16)
    place = place_ref[...]
    tab = _tn(x1, place) + _tn(x2, place) + _tn(x3, place) + fill_ref[...]
    for i, ref in enumerate(out_refs):
        ref[...] = tab[:, i * 3 * LANES:(i + 1) * 3 * LANES]


def _rope_tables(positions):
    inv = jnp.concatenate([1.0 / (theta ** (jnp.arange(0, dim, 2, dtype=F32) / dim))
                           for dim, theta, _, _ in ROPE_KINDS])
    ang = inv[:, None] * positions.reshape(-1).astype(F32)[None, :]
    compact = jnp.concatenate([jnp.cos(ang), jnp.sin(ang)], axis=0)
    compact = jnp.pad(compact, ((0, LANES - compact.shape[0]), (0, 0)))
    place, fill = _rope_placement()
    t = compact.shape[1]
    tab_shape = jax.ShapeDtypeStruct((t, 3 * LANES), F32)
    return pl.pallas_call(
        _rope_kernel,
        grid=(t // ROPE_TM,),
        in_specs=[pl.BlockSpec((LANES, ROPE_TM), lambda i: (0, i)),
                  pl.BlockSpec(place.shape, lambda i: (0, 0)),
                  pl.BlockSpec(fill.shape, lambda i: (0, 0))],
        out_specs=tuple(pl.BlockSpec((ROPE_TM, 3 * LANES), lambda i: (i, 0)) for _ in ROPE_KINDS),
        out_shape=tuple(tab_shape for _ in ROPE_KINDS),
        compiler_params=pltpu.CompilerParams(
            dimension_semantics=("parallel",), vmem_limit_bytes=VMEM_LIMIT),
        name="rope_tables",
    )(compact, place, fill)


def _pad_cols(w, n):
    return jnp.pad(w, ((0, 0), (0, 0), (0, n - w.shape[-1])))


def _in_weight(w_in):
    offs = np.cumsum((0,) + IN_SIZES)
    seg = [w_in[:, :, offs[i]:offs[i + 1]] for i in range(len(IN_SIZES))]
    (nsa_q, k_cmp, v_cmp, k_slc, v_slc, k_win, v_win, gate,
     cq, ckv, kpe, ret_q, ret_k, ret_v, ret_g) = seg
    scale = HEAD_DIM ** -0.5
    qh = [nsa_q[:, :, h * HEAD_DIM:(h + 1) * HEAD_DIM] * scale for h in range(NSA_HEADS)]
    tiles = [jnp.concatenate([qh[j], qh[j + NSA_REP]], axis=-1) for j in range(NSA_REP)]
    tiles += [k_slc, k_win, k_cmp, v_cmp]
    tiles += [cq, ckv, jnp.pad(kpe, ((0, 0), (0, 0), (HALF, LANES - HALF - MLA_ROPE_DIM)))]
    ret_w = RET_PAIRS * LANES
    tiles += [_pad_cols(ret_q, ret_w), _pad_cols(ret_k * scale, ret_w), _pad_cols(ret_v, ret_w),
              _pad_cols(ret_g, ret_w)]
    w_t = lax.optimization_barrier(
        jnp.concatenate([v_slc, v_win, _pad_cols(gate, GATE_ROWS)], axis=-1).astype(BF16))
    return jnp.concatenate(tiles, axis=-1).astype(BF16), w_t.transpose(0, 2, 1)


def _compress_weights(pos, w1, w2):
    nl = pos.shape[0]
    g, dh, hid = NSA_KV_GROUPS, HEAD_DIM, CMP_HIDDEN
    assert g == 2
    p = jnp.tile(pos, (1, 1, g))

    def block_diag2(w):
        lead = ((0, 0),) * (w.ndim - 1)
        return jnp.concatenate([jnp.pad(w, lead + ((0, w.shape[-1]),)),
                                jnp.pad(w, lead + ((w.shape[-1], 0),))], axis=-2)

    w = block_diag2(w1.astype(BF16).reshape(nl, CMP_LEN, dh, hid))
    return p, w, block_diag2(w2.astype(BF16))


def _mla_weights(w_uq, w_ukv):
    nl = w_uq.shape[0]
    dq = MLA_NOPE_DIM + MLA_ROPE_DIM
    wq = w_uq.reshape(nl, MLA_Q_RANK, MLA_HEADS, dq)
    pe = wq[..., MLA_NOPE_DIM:]
    half = MLA_ROPE_DIM // 2
    rot = jnp.concatenate([jnp.zeros_like(wq[..., :MLA_NOPE_DIM]), -pe[..., half:], pe[..., :half]], axis=-1)
    pad = lambda w: jnp.pad(w, ((0, 0), (0, 0), (0, 0), (0, LANES - dq))).reshape(
        nl, MLA_Q_RANK, MLA_HEADS * LANES)
    wq = jnp.concatenate([pad(wq), pad(rot)], axis=-1)
    wkv = w_ukv.reshape(nl, MLA_KV_RANK, MLA_HEADS, MLA_NOPE_DIM + MLA_V_DIM)
    wk = jnp.pad(wkv[..., :MLA_NOPE_DIM], ((0, 0), (0, 0), (0, 0), (0, LANES - MLA_NOPE_DIM)))
    wk = wk.reshape(nl, MLA_KV_RANK, MLA_HEADS * LANES)
    wvt = wkv[..., MLA_NOPE_DIM:].reshape(nl, MLA_KV_RANK, MLA_V_ROWS).transpose(0, 2, 1)
    return wq.astype(BF16), wk.astype(BF16), wvt.astype(BF16)


def _out_weight(w_out):
    nl = w_out.shape[0]
    pad_rows = lambda w: jnp.pad(w, ((0, 0), (0, RET_PAIRS * LANES - w.shape[1]), (0, 0)))
    nsa = w_out[:, :NSA_Q_W].reshape(nl, NSA_HEADS, HEAD_DIM, D_MODEL)
    order = [h for j in range(NSA_REP) for h in (j, j + NSA_REP)]
    nsa = nsa[:, order].reshape(nl, NSA_Q_W, D_MODEL)
    mla = pad_rows(w_out[:, NSA_Q_W:NSA_Q_W + MLA_HEADS * MLA_V_DIM])
    ret = pad_rows(w_out[:, NSA_Q_W + MLA_HEADS * MLA_V_DIM:])
    return jnp.concatenate([nsa, mla, ret], axis=1).astype(BF16)


def _retention_tables(gn_gain):
    nh = 2 * RET_PAIRS
    log_g = jnp.log(1.0 - 2.0 ** (-5.0 - jnp.arange(nh, dtype=F32)))
    i = jnp.arange(RET_CHUNK, dtype=F32)
    diff = i[:, None] - i[None, :]
    intra = jnp.where(diff >= 0, jnp.exp(jnp.maximum(diff, 0.0)[None] * log_g[:, None, None]), 0.0)
    read_decay = jnp.exp((i + 1.0)[None, :] * log_g[:, None])
    write_decay = jnp.exp((RET_CHUNK - 1.0 - i)[None, :] * log_g[:, None])
    chunk_decay = jnp.exp(RET_CHUNK * log_g)

    def lanes(t):
        t = t.reshape(RET_PAIRS, 2, -1)
        return jnp.repeat(t.transpose(0, 2, 1), HALF, axis=-1)

    gn = jnp.pad(gn_gain, ((0, 0), (0, nh - RET_HEADS), (0, 0)))
    gn = gn.reshape(gn.shape[0], RET_PAIRS, 1, LANES)
    return (gn, intra.reshape(RET_PAIRS, 2, RET_CHUNK, RET_CHUNK), lanes(read_decay), lanes(write_decay),
            lanes(chunk_decay[:, None]))


def _selection_overlap(seq):
    n_cmp = (seq - CMP_LEN) // CMP_STRIDE + 1
    n_sel = seq // SEL_BLOCK
    cs = np.arange(n_cmp) * CMP_STRIDE
    ss = np.arange(n_sel) * SEL_BLOCK
    ov = np.clip(np.minimum(cs[:, None] + CMP_LEN, ss[None, :] + SEL_BLOCK)
                 - np.maximum(cs[:, None], ss[None, :]), 0, None) / CMP_LEN
    ovl_t = np.zeros((n_sel, seq // CMP_STRIDE), np.float32)
    ovl_t[:, :n_cmp] = ov.T
    return jnp.asarray(ovl_t, BF16)


def kernel(x, positions, ln1_gain, w_in, cmp_pos_k, cmp_w1_k, cmp_w2_k, cmp_pos_v, cmp_w1_v, cmp_w2_v,
           mla_q_norm, mla_w_uq, mla_kv_norm, mla_w_ukv, ret_gn_gain, w_out, ln2_gain, w_up, w_down,
           final_gain):
    batch, seq, _ = x.shape
    depth = w_in.shape[0]
    t = batch * seq

    tab_n, tab_m, tab_r = _rope_tables(positions)

    w_in_p, w_in_t = _in_weight(w_in)
    pos_k, w1_k, w2_k = _compress_weights(cmp_pos_k, cmp_w1_k, cmp_w2_k)
    pos_v, w1_v, w2_v = _compress_weights(cmp_pos_v, cmp_w1_v, cmp_w2_v)
    cmp_pos = jnp.stack([pos_k, pos_v], axis=1)
    cmp_w1 = jnp.stack([w1_k, w1_v], axis=1)
    w2_vt = w2_v.transpose(0, 2, 1)
    wq, wk, wvt = _mla_weights(mla_w_uq, mla_w_ukv)
    wo = _out_weight(w_out)
    wu = w_up.astype(BF16)
    wd = w_down.astype(BF16)
    gn, intra, rd, wdec, cd = _retention_tables(ret_gn_gain)
    ovl_t = _selection_overlap(seq)
    gf = final_gain.reshape(1, D_MODEL)

    x2 = x.reshape(t, D_MODEL)
    for l in range(depth):
        nsa, nsa_t, gate_t, k_cmp, v_cmp, ret, rgate, q_m, k_m, vt_m = _in_proj(
            x2, ln1_gain[l].reshape(1, D_MODEL), w_in_p, w_in_t, mla_q_norm[l].reshape(1, -1),
            mla_kv_norm[l].reshape(1, -1), wq, wk, wvt, l, tab_n, tab_m, tab_r)
        kc, vct = _compress(k_cmp, v_cmp, cmp_pos[l], cmp_w1, w2_k[l], w2_vt[l], l, batch, seq)
        o_nsa = _nsa_attention(nsa, nsa_t, gate_t, kc, vct, ovl_t, batch, seq)
        o_mla = _mla_attention(q_m, k_m, vt_m, batch, seq)
        o_ret = _retention(ret, rgate, gn[l], intra, rd, wdec, cd, batch, seq)
        x2 = _out_mlp(x2, o_nsa, o_mla, o_ret, wo, ln2_gain[l].reshape(1, D_MODEL), wu, wd, l, gf,
                      final_norm=(l == depth - 1))
    return x2.reshape(batch, seq, D_MODEL)
```

```python
import functools
import math

import numpy as np
import jax
import jax.numpy as jnp
from jax import lax
from jax.experimental import pallas as pl
from jax.experimental.pallas import tpu as pltpu

F32 = jnp.float32
BF16 = jnp.bfloat16

D_MODEL = 1024
HEAD_DIM = 64
NSA_HEADS = 6
NSA_KV_GROUPS = 2
NSA_REP = NSA_HEADS // NSA_KV_GROUPS
N_BRANCH = 3
CMP_LEN = 32
CMP_STRIDE = 16
CMP_HIDDEN = 2 * HEAD_DIM
SEL_BLOCK = 64
SEL_TOP_N = 16
WINDOW = 512
MLA_HEADS = 5
MLA_Q_RANK = 256
MLA_KV_RANK = 128
MLA_NOPE_DIM = 64
MLA_ROPE_DIM = 32
MLA_V_DIM = 64
RET_HEADS = 5
RET_CHUNK = 128
ROPE_THETA = 500000.0
PARTIAL_ROPE_DIM = HEAD_DIM // 4
RET_THETA = 10000.0
D_FF = 4 * D_MODEL
NORM_EPS = 1e-6
NEG_INF = -1e30
FORCE_SCORE = 1e9
LOG2E = math.log2(math.e)

NSA_Q_W = NSA_HEADS * HEAD_DIM
NSA_KV_W = NSA_KV_GROUPS * HEAD_DIM
NSA_GATE_W = NSA_HEADS * N_BRANCH
RET_W = RET_HEADS * HEAD_DIM
IN_SIZES = (NSA_Q_W, NSA_KV_W, NSA_KV_W, NSA_KV_W, NSA_KV_W, NSA_KV_W, NSA_KV_W, NSA_GATE_W,
            MLA_Q_RANK, MLA_KV_RANK, MLA_ROPE_DIM, RET_W, RET_W, RET_W, RET_W)

LANES = 128
HALF = LANES // 2
SUBLANES = 8
VMEM_LIMIT = 56 * 1024 * 1024

NSA_TILES = 5
GATE_ROWS = 32
NSA_T_ROWS = 2 * LANES + GATE_ROWS
MLA_IN_TILES = 4
RET_PAIRS = 3
RET_TILES = 3 * RET_PAIRS
IN_TILES = NSA_TILES + 2 + MLA_IN_TILES + RET_TILES + RET_PAIRS
N_PAD = IN_TILES * LANES
MLA_V_ROWS = MLA_HEADS * MLA_V_DIM

IN_TM = 512
IN_CHUNK_TILES = 8
MLP_TM = 512
ATT_T = 512
MLP_FF_CHUNK = 1024


def _nn(a, b):
    return jnp.dot(a, b, preferred_element_type=F32)


def _nt(a, b):
    return lax.dot_general(a, b, (((1,), (1,)), ((), ())), preferred_element_type=F32)


def _rms(x, gain):
    return x * lax.rsqrt(jnp.mean(x * x, axis=-1, keepdims=True) + NORM_EPS) * gain


def _rope(val, tab, half):
    cos = tab[:, 0:LANES]
    sin_a = tab[:, LANES:2 * LANES]
    sin_b = tab[:, 2 * LANES:3 * LANES]
    return (val * cos + pltpu.roll(val, LANES - half, 1) * sin_a
            + pltpu.roll(val, half, 1) * sin_b)


def _lane_lo(shape):
    return lax.broadcasted_iota(jnp.int32, shape, len(shape) - 1) < HALF


def _mla_up(c, qn_ref, kvn_ref, wq_ref, wk_ref, wvt_ref, tm_ref, q_ref, k_ref, vt_ref):
    scale = (MLA_NOPE_DIM + MLA_ROPE_DIM) ** -0.5 * LOG2E
    cq = _rms(c[:, 0:MLA_Q_RANK], qn_ref[...]).astype(BF16)
    ckv = _rms(c[:, MLA_Q_RANK:MLA_Q_RANK + MLA_KV_RANK], kvn_ref[...]).astype(BF16)
    k_pe = _rope(c[:, 3 * LANES:4 * LANES], tm_ref[...], MLA_ROPE_DIM // 2)
    cos = tm_ref[:, 0:LANES] * scale
    sin = (tm_ref[:, 2 * LANES:3 * LANES] - tm_ref[:, LANES:2 * LANES]) * scale
    q = _nn(cq, wq_ref[...])
    k = _nn(ckv, wk_ref[...])
    w = MLA_HEADS * LANES
    for hd in range(MLA_HEADS):
        sl = slice(hd * LANES, (hd + 1) * LANES)
        rot = slice(w + hd * LANES, w + (hd + 1) * LANES)
        q_ref[:, sl] = (q[:, sl] * cos + q[:, rot] * sin).astype(BF16)
        k_ref[:, sl] = (k[:, sl] + k_pe).astype(BF16)
    vt_ref[...] = _nt(wvt_ref[...], ckv).astype(BF16)


def _inproj_kernel(x_ref, g_ref, w_ref, wt_ref, tn_ref, tm_ref, tr_ref, qn_ref, kvn_ref, wq_ref, wk_ref,
                   wvt_ref, nsa_ref, nsat_ref, gatet_ref, kcmp_ref, vcmp_ref, ret_ref, rgate_ref,
                   mq_ref, mk_ref, mvt_ref):
    h = _rms(x_ref[...], g_ref[...]).astype(BF16)
    tab_n = tn_ref[...]
    tab_r = tr_ref[...]

    chunks = [_nn(h, w_ref[:, c:min(c + IN_CHUNK_TILES * LANES, N_PAD)])
              for c in range(0, N_PAD, IN_CHUNK_TILES * LANES)]

    def tiles(first, n):
        cols = [chunks[i // IN_CHUNK_TILES][:, (i % IN_CHUNK_TILES) * LANES:(i % IN_CHUNK_TILES + 1) * LANES]
                for i in range(first, first + n)]
        return cols[0] if n == 1 else jnp.concatenate(cols, axis=1)

    t0 = 0
    for i in range(NSA_TILES):
        v = _rope(tiles(t0 + i, 1), tab_n, PARTIAL_ROPE_DIM // 2)
        if i < NSA_REP:
            v = v * LOG2E
        nsa_ref[:, i * LANES:(i + 1) * LANES] = v.astype(BF16)
    t0 += NSA_TILES
    kcmp_ref[...] = _rope(tiles(t0, 1), tab_n, PARTIAL_ROPE_DIM // 2)
    vcmp_ref[...] = tiles(t0 + 1, 1)
    t0 += 2
    _mla_up(tiles(t0, MLA_IN_TILES), qn_ref, kvn_ref, wq_ref, wk_ref, wvt_ref, tm_ref, mq_ref, mk_ref, mvt_ref)
    t0 += MLA_IN_TILES
    for i in range(RET_TILES):
        v = tiles(t0 + i, 1)
        if i < 2 * RET_PAIRS:
            v = _rope(v, tab_r, HEAD_DIM // 2)
        ret_ref[:, i * LANES:(i + 1) * LANES] = v.astype(BF16)
    t0 += RET_TILES
    rgate_ref[...] = tiles(t0, RET_PAIRS)
    at = _nt(wt_ref[...], h)
    nsat_ref[...] = at[0:2 * LANES, :].astype(BF16)
    gatet_ref[...] = at[2 * LANES:NSA_T_ROWS, :]


def _layer_spec(w, layer, **kwargs):
    zeros = (0,) * (w.ndim - 1)
    return pl.BlockSpec((None,) + w.shape[1:], lambda *_: (layer,) + zeros, **kwargs)


def _in_proj(x2, gain, w, wt, q_norm, kv_norm, wq, wk, wvt, layer, tab_n, tab_m, tab_r):
    t = x2.shape[0]
    row = lambda i: (i, 0)
    col = lambda i: (0, i)
    const = lambda i: (0, 0)
    out_shapes = (
        jax.ShapeDtypeStruct((t, NSA_TILES * LANES), BF16),
        jax.ShapeDtypeStruct((2 * LANES, t), BF16),
        jax.ShapeDtypeStruct((GATE_ROWS, t), F32),
        jax.ShapeDtypeStruct((t, LANES), F32),
        jax.ShapeDtypeStruct((t, LANES), F32),
        jax.ShapeDtypeStruct((t, RET_TILES * LANES), BF16),
        jax.ShapeDtypeStruct((t, RET_PAIRS * LANES), F32),
        jax.ShapeDtypeStruct((t, MLA_HEADS * LANES), BF16),
        jax.ShapeDtypeStruct((t, MLA_HEADS * LANES), BF16),
        jax.ShapeDtypeStruct((MLA_V_ROWS, t), BF16),
    )
    out_specs = tuple(
        pl.BlockSpec((s.shape[0], IN_TM), col) if s.shape[1] == t else pl.BlockSpec((IN_TM, s.shape[1]), row)
        for s in out_shapes)
    return pl.pallas_call(
        _inproj_kernel,
        grid=(t // IN_TM,),
        in_specs=[
            pl.BlockSpec((IN_TM, D_MODEL), row),
            pl.BlockSpec((1, D_MODEL), const),
            _layer_spec(w, layer),
            _layer_spec(wt, layer),
            pl.BlockSpec((IN_TM, 3 * LANES), row),
            pl.BlockSpec((IN_TM, 3 * LANES), row),
            pl.BlockSpec((IN_TM, 3 * LANES), row),
            pl.BlockSpec((1, MLA_Q_RANK), const),
            pl.BlockSpec((1, MLA_KV_RANK), const),
            _layer_spec(wq, layer),
            _layer_spec(wk, layer),
            _layer_spec(wvt, layer),
        ],
        out_specs=out_specs,
        out_shape=out_shapes,
        compiler_params=pltpu.CompilerParams(
            dimension_semantics=("parallel",), vmem_limit_bytes=VMEM_LIMIT),
        name="in_proj",
    )(x2, gain, w, wt, tab_n, tab_m, tab_r, q_norm, kv_norm, wq, wk, wvt)


def _compress_kernel(k_ref, v_ref, pos_ref, w1_ref, w2k_ref, w2vt_ref, kc_ref, vct_ref):
    n_blk = k_ref.shape[0] // CMP_STRIDE

    def hidden(src, i):
        lo = jnp.zeros((n_blk, NSA_KV_GROUPS * CMP_HIDDEN), F32)
        hi = jnp.zeros((n_blk, NSA_KV_GROUPS * CMP_HIDDEN), F32)
        for r in range(CMP_STRIDE):
            tok = src[pl.ds(r, n_blk, stride=CMP_STRIDE), :]
            lo = lo + _nn((tok + pos_ref[i, r:r + 1, :]).astype(BF16), w1_ref[i, r])
            r2 = CMP_STRIDE + r
            hi = hi + _nn((tok + pos_ref[i, r2:r2 + 1, :]).astype(BF16), w1_ref[i, r2])
        return jax.nn.gelu(lo + pltpu.roll(hi, n_blk - 1, 0)).astype(BF16)

    kc_ref[0] = _nn(hidden(k_ref, 0), w2k_ref[...]).astype(BF16)
    vct_ref[0] = _nt(w2vt_ref[...], hidden(v_ref, 1)).astype(BF16)


def _compress(k_cmp, v_cmp, pos, w1, w2k, w2vt, layer, batch, seq):
    b = batch
    n_blk = seq // CMP_STRIDE
    return pl.pallas_call(
        _compress_kernel,
        grid=(b,),
        in_specs=[
            pl.BlockSpec((seq, LANES), lambda i: (i, 0)),
            pl.BlockSpec((seq, LANES), lambda i: (i, 0)),
            pl.BlockSpec(pos.shape, lambda i: (0, 0, 0)),
            _layer_spec(w1, layer),
            pl.BlockSpec(w2k.shape, lambda i: (0, 0)),
            pl.BlockSpec(w2vt.shape, lambda i: (0, 0)),
        ],
        out_specs=(pl.BlockSpec((1, n_blk, LANES), lambda i: (i, 0, 0)),
                   pl.BlockSpec((1, LANES, n_blk), lambda i: (i, 0, 0))),
        out_shape=(jax.ShapeDtypeStruct((b, n_blk, LANES), BF16),
                   jax.ShapeDtypeStruct((b, LANES, n_blk), BF16)),
        compiler_params=pltpu.CompilerParams(
            dimension_semantics=("parallel",), vmem_limit_bytes=VMEM_LIMIT),
        name="nsa_compress",
    )(k_cmp, v_cmp, pos, w1, w2k, w2vt)


SCORE_LOOKAHEAD = 2
ONES_ROWS = 16


def _tri_scores(k, q, dead_upper, mask, bias=None):
    half = k.shape[0] // 2
    dead = jnp.full((half, half), NEG_INF, F32)
    if dead_upper:
        top = _nt(k[:half], q)
        bottom = jnp.concatenate([dead, _nt(k[half:], q[half:])], axis=1)
    else:
        top = jnp.concatenate([_nt(k[:half], q[:half]), dead], axis=1)
        bottom = _nt(k[half:], q)
    s_t = jnp.concatenate([top, bottom], axis=0)
    return jnp.where(mask, s_t if bias is None else s_t + bias, NEG_INF)


def _tri_pv(dead_upper, v_ext, p):
    half = p.shape[0] // 2
    if dead_upper:
        full = _nn(v_ext[:, :half], p[:half])
        part = _nn(v_ext[:, half:], p[half:, half:])
        return jnp.concatenate([full[:, :half], full[:, half:] + part], axis=1)
    part = _nn(v_ext[:, :half], p[:half, :half])
    full = _nn(v_ext[:, half:], p[half:])
    return jnp.concatenate([full[:, :half] + part, full[:, half:]], axis=1)


def _softmax_steps(score_fns, v_ts, states, pv_fn=_nn):
    n = len(score_fns)
    s_ts = [score_fns[h]() if h < SCORE_LOOKAHEAD else None for h in range(n)]
    out = []
    for h in range(n):
        if h + SCORE_LOOKAHEAD < n:
            s_ts[h + SCORE_LOOKAHEAD] = score_fns[h + SCORE_LOOKAHEAD]()
        m_old, acc_old = states[h]
        m = jnp.maximum(m_old, jnp.max(s_ts[h], axis=0, keepdims=True))
        p = jnp.exp2(s_ts[h] - m).astype(BF16)
        s_ts[h] = None
        v_ext = jnp.concatenate([v_ts[h], jnp.ones((ONES_ROWS, v_ts[h].shape[1]), BF16)], axis=0)
        out.append((m, jnp.exp2(m_old - m) * acc_old + pv_fn(v_ext, p)))
    return tuple(out)


def _softmax_init(cols):
    return jnp.full((1, cols), NEG_INF, F32), jnp.zeros((HEAD_DIM + ONES_ROWS, cols), F32)


def _softmax_finish(state):
    acc = state[1]
    return acc[0:HEAD_DIM] * (1.0 / jnp.maximum(acc[HEAD_DIM:HEAD_DIM + 1], 1e-30))


def _nsa_kernel(q_ref, gatet_ref, kc_ref, vct_ref, ks_ref, kw_ref, vst_ref, vwt_ref, ovl_ref,
                o_ref, out_t_ref, selb_ref):
    tq = ATT_T
    qi = pl.program_id(1)
    q0 = pl.multiple_of(qi * tq, tq)
    lo1 = _lane_lo((1, LANES))
    group_lanes = (lo1, jnp.logical_not(lo1))
    gate = jax.nn.sigmoid(gatet_ref[...])
    heads = range(NSA_HEADS)

    def q_head(h):
        j, g = h % NSA_REP, h // NSA_REP
        tile = q_ref[:, j * LANES:(j + 1) * LANES]
        return jnp.where(group_lanes[g], tile, jnp.zeros_like(tile))

    def v_rows(ref, h, k0, n):
        g = h // NSA_REP
        return ref[g * HEAD_DIM:(g + 1) * HEAD_DIM, pl.ds(k0, n)]

    def emit(h, branch, qs, nq, o_t):
        j, g = h % NSA_REP, h // NSA_REP
        r0 = j * LANES + g * HEAD_DIM
        row = h * N_BRANCH + branch
        val = gate[row:row + 1, qs:qs + nq] * o_t
        if branch == 0:
            out_t_ref[r0:r0 + HEAD_DIM, qs:qs + nq] = val
        else:
            out_t_ref[r0:r0 + HEAD_DIM, qs:qs + nq] += val

    qm = [q_head(h) for h in heads]

    n_cmp_pad = kc_ref.shape[1]
    kc = kc_ref[0]
    vct = vct_ref[0]
    n_i = lax.broadcasted_iota(jnp.int32, (n_cmp_pad, tq), 0)
    t_l = q0 + lax.broadcasted_iota(jnp.int32, (n_cmp_pad, tq), 1)
    cmask = (n_i * CMP_STRIDE + (CMP_LEN - 1)) <= t_l
    cmask_f = cmask.astype(F32)
    s_cs = [jnp.where(cmask, _nt(kc, qm[h]), NEG_INF) for h in heads]
    p_cs = [jnp.exp2(s_t - jnp.max(s_t, axis=0, keepdims=True)) * cmask_f for s_t in s_cs]
    p_cs = [p * (1.0 / jnp.maximum(jnp.sum(p, axis=0, keepdims=True), 1e-30)) for p in p_cs]
    for h in heads:
        g = h // NSA_REP
        emit(h, 0, 0, tq, _nn(vct[g * HEAD_DIM:(g + 1) * HEAD_DIM, :], p_cs[h].astype(BF16)))

    n_sel = ovl_ref.shape[0]
    needs_rank = q0 + tq > SEL_TOP_N * SEL_BLOCK

    @pl.when(jnp.logical_not(needs_rank))
    def _():
        selb_ref[...] = jnp.zeros_like(selb_ref)

    @pl.when(needs_rank)
    def _():
        m_i = lax.broadcasted_iota(jnp.int32, (n_sel, tq), 0)
        cur = jnp.right_shift(q0 + lax.broadcasted_iota(jnp.int32, (n_sel, tq), 1),
                              SEL_BLOCK.bit_length() - 1)
        valid = m_i <= cur
        forced = (m_i == 0) | (m_i == cur) | (m_i == cur - 1)
        ovl = ovl_ref[...]
        sub = SUBLANES
        m_loc = lax.broadcasted_iota(jnp.int32, (sub, tq), 0)
        for g in range(NSA_KV_GROUPS):
            psum = p_cs[g * NSA_REP]
            for h in range(g * NSA_REP + 1, (g + 1) * NSA_REP):
                psum = psum + p_cs[h]
            p_hi = psum.astype(BF16)
            p_lo = (psum - p_hi.astype(F32)).astype(BF16)
            imp = _nn(ovl, p_hi) + _nn(ovl, p_lo)
            imp = jnp.where(valid & forced, FORCE_SCORE, imp)
            imp = jnp.where(valid, imp, NEG_INF)
            parts = [imp[i:i + sub] for i in range(0, n_sel, sub)]
            ranks = [jnp.zeros((sub, tq), jnp.int32) for _ in parts]
            for mp in range(n_sel):
                row = imp[mp:mp + 1, :]
                for i, part in enumerate(parts):
                    if i * sub + sub - 1 <= mp:
                        beats = row > part
                    elif i * sub > mp:
                        beats = row >= part
                    else:
                        beats = (row > part) | ((row == part) & (m_loc + i * sub > mp))
                    ranks[i] = ranks[i] + beats.astype(jnp.int32)
            rank = jnp.concatenate(ranks, axis=0)
            selb_ref[g] = jnp.where(rank < SEL_TOP_N, 0.0, NEG_INF)

    blocks_per_chunk = tq // SEL_BLOCK

    def sel_bias(g, c, n_keys, qs, nq):
        rows = [jnp.broadcast_to(selb_ref[g, pl.ds(c * blocks_per_chunk + i, 1), qs:qs + nq], (SEL_BLOCK, nq))
                for i in range(n_keys // SEL_BLOCK)]
        return jnp.concatenate(rows, axis=0)

    key_i = lax.broadcasted_iota(jnp.int32, (tq, tq), 0)
    qry_i = lax.broadcasted_iota(jnp.int32, (tq, tq), 1)
    causal_mask = key_i <= qry_i

    def sel_chunk(c, k0, states, causal):
        k = ks_ref[pl.ds(k0, tq), :]
        bias = [sel_bias(g, c, tq, 0, tq) for g in range(NSA_KV_GROUPS)]

        def score_fn(h):
            if causal:
                return _tri_scores(k, qm[h], True, causal_mask, bias[h // NSA_REP])
            return _nt(k, qm[h]) + bias[h // NSA_REP]

        return _softmax_steps([functools.partial(score_fn, h) for h in heads],
                              [v_rows(vst_ref, h, k0, tq) for h in heads], states,
                              functools.partial(_tri_pv, True) if causal else _nn)

    states = sel_chunk(qi, q0, tuple(_softmax_init(tq) for _ in heads), True)
    states = lax.fori_loop(
        0, qi, lambda c, st: sel_chunk(c, pl.multiple_of(c * tq, tq), st, False), states)
    for h in heads:
        emit(h, 1, 0, tq, _softmax_finish(states[h]))

    def win_chunk(k0, states, mask, dead_upper):
        k = kw_ref[pl.ds(k0, tq), :]
        if dead_upper is None:
            score_fns = [lambda h=h: jnp.where(mask, _nt(k, qm[h]), NEG_INF) for h in heads]
            pv_fn = _nn
        else:
            score_fns = [lambda h=h: _tri_scores(k, qm[h], dead_upper, mask) for h in heads]
            pv_fn = functools.partial(_tri_pv, dead_upper)
        return _softmax_steps(score_fns, [v_rows(vwt_ref, h, k0, tq) for h in heads], states, pv_fn)

    states = win_chunk(q0, tuple(_softmax_init(tq) for _ in heads), causal_mask, True)
    for d in range(1, WINDOW // tq + 1):
        k0 = pl.multiple_of(jnp.maximum(qi - d, 0) * tq, tq)
        in_band = (key_i - d * tq > qry_i - WINDOW) & (qi >= d)
        states = win_chunk(k0, states, in_band, False if d * tq == WINDOW else None)
    for h in heads:
        emit(h, 2, 0, tq, _softmax_finish(states[h]))

    for j in range(NSA_REP):
        o_ref[:, j * LANES:(j + 1) * LANES] = out_t_ref[j * LANES:(j + 1) * LANES, :].T.astype(BF16)


def _nsa_attention(nsa, nsa_t, gate_t, kc, vct, ovl_t, batch, seq):
    tq = ATT_T
    nq = seq // tq
    n_blk = kc.shape[1]
    qrow = lambda b, i: (b * nq + i, 0)
    return pl.pallas_call(
        _nsa_kernel,
        grid=(batch, nq),
        in_specs=[
            pl.BlockSpec((tq, NSA_REP * LANES), qrow),
            pl.BlockSpec((GATE_ROWS, tq), lambda b, i: (0, b * nq + i)),
            pl.BlockSpec((1, n_blk, LANES), lambda b, i: (b, 0, 0)),
            pl.BlockSpec((1, LANES, n_blk), lambda b, i: (b, 0, 0)),
            pl.BlockSpec((seq, LANES), lambda b, i: (b, 3)),
            pl.BlockSpec((seq, LANES), lambda b, i: (b, 4)),
            pl.BlockSpec((LANES, seq), lambda b, i: (0, b)),
            pl.BlockSpec((LANES, seq), lambda b, i: (1, b)),
            pl.BlockSpec(ovl_t.shape, lambda b, i: (0, 0)),
        ],
        out_specs=pl.BlockSpec((tq, NSA_REP * LANES), qrow),
        out_shape=jax.ShapeDtypeStruct((batch * seq, NSA_REP * LANES), BF16),
        scratch_shapes=[pltpu.VMEM((NSA_REP * LANES, tq), F32),
                        pltpu.VMEM((NSA_KV_GROUPS, seq // SEL_BLOCK, tq), F32)],
        compiler_params=pltpu.CompilerParams(
            dimension_semantics=("parallel", "arbitrary"), vmem_limit_bytes=VMEM_LIMIT),
        name="nsa_attention",
    )(nsa, gate_t, kc, vct, nsa, nsa, nsa_t, nsa_t, ovl_t)


def _mla_attn_kernel(q_ref, k_ref, vt_ref, o_ref, out_t_ref):
    tq = ATT_T
    qi = pl.program_id(1)
    q0 = pl.multiple_of(qi * tq, tq)
    heads = range(MLA_HEADS)
    qh = [q_ref[:, h * LANES:(h + 1) * LANES] for h in heads]

    def chunk(k0, states, causal):
        if causal:
            mask = (lax.broadcasted_iota(jnp.int32, (tq, tq), 0)
                    <= lax.broadcasted_iota(jnp.int32, (tq, tq), 1))

        def score_fn(h):
            k = k_ref[pl.ds(k0, tq), h * LANES:(h + 1) * LANES]
            return _tri_scores(k, qh[h], True, mask) if causal else _nt(k, qh[h])

        v_ts = [vt_ref[h * MLA_V_DIM:(h + 1) * MLA_V_DIM, pl.ds(k0, tq)] for h in heads]
        return _softmax_steps([functools.partial(score_fn, h) for h in heads], v_ts, states,
                              functools.partial(_tri_pv, True) if causal else _nn)

    states = chunk(q0, tuple(_softmax_init(tq) for _ in heads), True)
    states = lax.fori_loop(0, qi, lambda c, st: chunk(pl.multiple_of(c * tq, tq), st, False), states)
    for h in heads:
        out_t_ref[h * MLA_V_DIM:(h + 1) * MLA_V_DIM, :] = _softmax_finish(states[h])
    out_t_ref[MLA_V_ROWS:, :] = jnp.zeros((RET_PAIRS * LANES - MLA_V_ROWS, tq), F32)
    for j in range(RET_PAIRS):
        o_ref[:, j * LANES:(j + 1) * LANES] = out_t_ref[j * LANES:(j + 1) * LANES, :].T.astype(BF16)


def _mla_attention(q, k, vt, batch, seq):
    tq = ATT_T
    nq = seq // tq
    return pl.pallas_call(
        _mla_attn_kernel,
        grid=(batch, nq),
        in_specs=[
            pl.BlockSpec((tq, MLA_HEADS * LANES), lambda b, i: (b * nq + i, 0)),
            pl.BlockSpec((seq, MLA_HEADS * LANES), lambda b, i: (b, 0)),
            pl.BlockSpec((MLA_V_ROWS, seq), lambda b, i: (0, b)),
        ],
        out_specs=pl.BlockSpec((tq, RET_PAIRS * LANES), lambda b, i: (b * nq + i, 0)),
        out_shape=jax.ShapeDtypeStruct((batch * seq, RET_PAIRS * LANES), BF16),
        scratch_shapes=[pltpu.VMEM((RET_PAIRS * LANES, tq), F32)],
        compiler_params=pltpu.CompilerParams(
            dimension_semantics=("parallel", "arbitrary"), vmem_limit_bytes=VMEM_LIMIT),
        name="mla_attention",
    )(q, k, vt)


def _tn(a, b):
    return lax.dot_general(a, b, (((0,), (0,)), ((), ())), preferred_element_type=F32)


RET_UNROLL = 16


def _retention_kernel(q_ref, k_ref, v_ref, gate_ref, gn_ref, intra_ref, rd_ref, wd_ref, cd_ref, o_ref,
                      kv_ref, st_ref):
    c_len = RET_CHUNK
    n_chunks = q_ref.shape[0] // c_len
    lo = _lane_lo((1, LANES))
    hi = jnp.logical_not(lo)
    blockdiag = (lax.broadcasted_iota(jnp.int32, (LANES, LANES), 0) < HALF) == _lane_lo((LANES, LANES))
    intra_a = intra_ref[0, 0]
    intra_b = intra_ref[0, 1]
    read_decay = rd_ref[0]
    write_decay = wd_ref[0]
    chunk_decay = cd_ref[0]
    gn = gn_ref[0]

    averager = jnp.where(blockdiag, 1.0 / HEAD_DIM, 0.0).astype(BF16)

    def half_mean(x):
        x_hi = x.astype(BF16)
        x_lo = (x - x_hi.astype(F32)).astype(BF16)
        return _nn(x_hi, averager) + _nn(x_lo, averager)

    def kv_body(c, carry):
        r0 = pl.multiple_of(c * c_len, c_len)
        kc = k_ref[pl.ds(r0, c_len), :]
        kv_ref[c] = _tn((kc.astype(F32) * write_decay).astype(BF16), v_ref[pl.ds(r0, c_len), :])
        return carry

    lax.fori_loop(0, n_chunks, kv_body, 0, unroll=RET_UNROLL)

    state = jnp.zeros((LANES, LANES), F32)
    for c in range(n_chunks):
        st_ref[c] = jnp.where(blockdiag, state, 0.0).astype(BF16)
        state = state * chunk_decay + kv_ref[c]

    def out_body(step, carry):
        cs = [step * RET_UNROLL + u for u in range(RET_UNROLL)]
        rows = [pl.ds(pl.multiple_of(c * c_len, c_len), c_len) for c in cs]
        qs = [q_ref[r, :] for r in rows]
        ks = [k_ref[r, :] for r in rows]
        vs = [v_ref[r, :] for r in rows]
        zero = jnp.zeros_like(qs[0])
        sa = [(_nt(jnp.where(lo, q, zero), k) * intra_a).astype(BF16) for q, k in zip(qs, ks)]
        sb = [(_nt(jnp.where(hi, q, zero), k) * intra_b).astype(BF16) for q, k in zip(qs, ks)]
        cross = [_nn(q, st_ref[c]) * read_decay for q, c in zip(qs, cs)]
        os_ = [jnp.where(lo, _nn(a, v), _nn(b, v)) + x for a, b, v, x in zip(sa, sb, vs, cross)]
        ds_ = [o - half_mean(o) for o in os_]
        ys = [d * lax.rsqrt(half_mean(d * d) + NORM_EPS) * gn for d in ds_]
        for r, y in zip(rows, ys):
            o_ref[r, :] = (jax.nn.silu(gate_ref[r, :]) * y).astype(BF16)
        return carry

    lax.fori_loop(0, n_chunks // RET_UNROLL, out_body, 0)


def _retention(ret, rgate, gn, intra, rd, wd, cd, batch, seq):
    pair_const3 = lambda b, j: (j, 0, 0)
    return pl.pallas_call(
        _retention_kernel,
        grid=(batch, RET_PAIRS),
        in_specs=[
            pl.BlockSpec((seq, LANES), lambda b, j: (b, j)),
            pl.BlockSpec((seq, LANES), lambda b, j: (b, RET_PAIRS + j)),
            pl.BlockSpec((seq, LANES), lambda b, j: (b, 2 * RET_PAIRS + j)),
            pl.BlockSpec((seq, LANES), lambda b, j: (b, j)),
            pl.BlockSpec((1, 1, LANES), pair_const3),
            pl.BlockSpec((1, 2, RET_CHUNK, RET_CHUNK), lambda b, j: (j, 0, 0, 0)),
            pl.BlockSpec((1, RET_CHUNK, LANES), pair_const3),
            pl.BlockSpec((1, RET_CHUNK, LANES), pair_const3),
            pl.BlockSpec((1, 1, LANES), pair_const3),
        ],
        out_specs=pl.BlockSpec((seq, LANES), lambda b, j: (b, j)),
        out_shape=jax.ShapeDtypeStruct((batch * seq, RET_PAIRS * LANES), BF16),
        scratch_shapes=[pltpu.VMEM((seq // RET_CHUNK, LANES, LANES), F32),
                        pltpu.VMEM((seq // RET_CHUNK, LANES, LANES), BF16)],
        compiler_params=pltpu.CompilerParams(
            dimension_semantics=("parallel", "arbitrary"), vmem_limit_bytes=VMEM_LIMIT),
        name="retention",
    )(ret, ret, ret, rgate, gn, intra, rd, wd, cd)


def _out_mlp_kernel(x_ref, nsa_ref, mla_ref, ret_ref, wo_ref, g2_ref, wu_ref, wd_ref, gf_ref, o_ref,
                    *, final_norm):
    mixed = jnp.concatenate([nsa_ref[...], mla_ref[...], ret_ref[...]], axis=1)
    x = x_ref[...] + _nn(mixed, wo_ref[...])
    h = _rms(x, g2_ref[...]).astype(BF16)
    y = x
    for c in range(D_FF // MLP_FF_CHUNK):
        sl = slice(c * MLP_FF_CHUNK, (c + 1) * MLP_FF_CHUNK)
        u = jnp.maximum(_nn(h, wu_ref[:, sl]), 0.0)
        y = y + _nn((u * u).astype(BF16), wd_ref[sl, :])
    if final_norm:
        y = _rms(y, gf_ref[...])
    o_ref[...] = y


def _out_mlp(x2, o_nsa, o_mla, o_ret, wo, g2, wu, wd, layer, gf, final_norm):
    t = x2.shape[0]
    w = RET_PAIRS * LANES
    row = lambda i: (i, 0)
    const = lambda i: (0, 0)
    resident = dict(pipeline_mode=pl.Buffered(1))
    return pl.pallas_call(
        functools.partial(_out_mlp_kernel, final_norm=final_norm),
        grid=(t // MLP_TM,),
        in_specs=[
            pl.BlockSpec((MLP_TM, D_MODEL), row),
            pl.BlockSpec((MLP_TM, w), row),
            pl.BlockSpec((MLP_TM, w), row),
            pl.BlockSpec((MLP_TM, w), row),
            _layer_spec(wo, layer, **resident),
            pl.BlockSpec((1, D_MODEL), const),
            _layer_spec(wu, layer, **resident),
            _layer_spec(wd, layer, **resident),
            pl.BlockSpec((1, D_MODEL), const),
        ],
        out_specs=pl.BlockSpec((MLP_TM, D_MODEL), row),
        out_shape=jax.ShapeDtypeStruct((t, D_MODEL), F32),
        compiler_params=pltpu.CompilerParams(
            dimension_semantics=("parallel",), vmem_limit_bytes=VMEM_LIMIT),
        name="out_mlp",
    )(x2, o_nsa, o_mla, o_ret, wo, g2, wu, wd, gf)


ROPE_KINDS = (
    (PARTIAL_ROPE_DIM, ROPE_THETA, HEAD_DIM, 0),
    (MLA_ROPE_DIM, ROPE_THETA, LANES, HALF),
    (HEAD_DIM, RET_THETA, HEAD_DIM, 0),
)
ROPE_TM = 1024


def _rope_placement():
    n_angles = sum(dim // 2 for dim, _, _, _ in ROPE_KINDS)
    assert 2 * n_angles <= LANES
    place = np.zeros((LANES, 3 * LANES * len(ROPE_KINDS)), np.float32)
    fill = np.zeros((1, 3 * LANES * len(ROPE_KINDS)), np.float32)
    row0 = 0
    for kind, (dim, _, period, base) in enumerate(ROPE_KINDS):
        half = dim // 2
        col0 = kind * 3 * LANES
        for lane in range(LANES):
            rel = (lane - base) % period
            first = lane >= base and rel < half
            second = lane >= base and half <= rel < dim
            if first or second:
                angle = rel if first else rel - half
                place[row0 + angle, col0 + lane] = 1.0
                place[n_angles + row0 + angle, col0 + (1 if first else 2) * LANES + lane] = -1.0 if first else 1.0
            else:
                fill[0, col0 + lane] = 1.0
        row0 += half
    return jnp.asarray(place, BF16), jnp.asarray(fill)


def _rope_kernel(cs_ref, place_ref, fill_ref, *out_refs):
    x = cs_ref[...]
    x1 = x.astype(BF16)
    r1 = x - x1.astype(F32)
    x2 = r1.astype(BF16)
    x3 = (r1 - x2.astype(F32)).astype(BF16)
    place = place_ref[...]
    tab = _tn(x1, place) + _tn(x2, place) + _tn(x3, place) + fill_ref[...]
    for i, ref in enumerate(out_refs):
        ref[...] = tab[:, i * 3 * LANES:(i + 1) * 3 * LANES]


def _rope_tables(positions):
    inv = jnp.concatenate([1.0 / (theta ** (jnp.arange(0, dim, 2, dtype=F32) / dim))
                           for dim, theta, _, _ in ROPE_KINDS])
    ang = inv[:, None] * positions.reshape(-1).astype(F32)[None, :]
    compact = jnp.concatenate([jnp.cos(ang), jnp.sin(ang)], axis=0)
    compact = jnp.pad(compact, ((0, LANES - compact.shape[0]), (0, 0)))
    place, fill = _rope_placement()
    t = compact.shape[1]
    tab_shape = jax.ShapeDtypeStruct((t, 3 * LANES), F32)
    return pl.pallas_call(
        _rope_kernel,
        grid=(t // ROPE_TM,),
        in_specs=[pl.BlockSpec((LANES, ROPE_TM), lambda i: (0, i)),
                  pl.BlockSpec(place.shape, lambda i: (0, 0)),
                  pl.BlockSpec(fill.shape, lambda i: (0, 0))],
        out_specs=tuple(pl.BlockSpec((ROPE_TM, 3 * LANES), lambda i: (i, 0)) for _ in ROPE_KINDS),
        out_shape=tuple(tab_shape for _ in ROPE_KINDS),
        compiler_params=pltpu.CompilerParams(
            dimension_semantics=("parallel",), vmem_limit_bytes=VMEM_LIMIT),
        name="rope_tables",
    )(compact, place, fill)


def _pad_cols(w, n):
    return jnp.pad(w, ((0, 0), (0, 0), (0, n - w.shape[-1])))


def _in_weight(w_in):
    offs = np.cumsum((0,) + IN_SIZES)
    seg = [w_in[:, :, offs[i]:offs[i + 1]] for i in range(len(IN_SIZES))]
    (nsa_q, k_cmp, v_cmp, k_slc, v_slc, k_win, v_win, gate,
     cq, ckv, kpe, ret_q, ret_k, ret_v, ret_g) = seg
    scale = HEAD_DIM ** -0.5
    qh = [nsa_q[:, :, h * HEAD_DIM:(h + 1) * HEAD_DIM] * scale for h in range(NSA_HEADS)]
    tiles = [jnp.concatenate([qh[j], qh[j + NSA_REP]], axis=-1) for j in range(NSA_REP)]
    tiles += [k_slc, k_win, k_cmp, v_cmp]
    tiles += [cq, ckv, jnp.pad(kpe, ((0, 0), (0, 0), (HALF, LANES - HALF - MLA_ROPE_DIM)))]
    ret_w = RET_PAIRS * LANES
    tiles += [_pad_cols(ret_q, ret_w), _pad_cols(ret_k * scale, ret_w), _pad_cols(ret_v, ret_w),
              _pad_cols(ret_g, ret_w)]
    w_t = lax.optimization_barrier(
        jnp.concatenate([v_slc, v_win, _pad_cols(gate, GATE_ROWS)], axis=-1).astype(BF16))
    return jnp.concatenate(tiles, axis=-1).astype(BF16), w_t.transpose(0, 2, 1)


def _compress_weights(pos, w1, w2):
    nl = pos.shape[0]
    g, dh, hid = NSA_KV_GROUPS, HEAD_DIM, CMP_HIDDEN
    assert g == 2
    p = jnp.tile(pos, (1, 1, g))

    def block_diag2(w):
        lead = ((0, 0),) * (w.ndim - 1)
        return jnp.concatenate([jnp.pad(w, lead + ((0, w.shape[-1]),)),
                                jnp.pad(w, lead + ((w.shape[-1], 0),))], axis=-2)

    w = block_diag2(w1.astype(BF16).reshape(nl, CMP_LEN, dh, hid))
    return p, w, block_diag2(w2.astype(BF16))


def _mla_weights(w_uq, w_ukv):
    nl = w_uq.shape[0]
    dq = MLA_NOPE_DIM + MLA_ROPE_DIM
    wq = w_uq.reshape(nl, MLA_Q_RANK, MLA_HEADS, dq)
    pe = wq[..., MLA_NOPE_DIM:]
    half = MLA_ROPE_DIM // 2
    rot = jnp.concatenate([jnp.zeros_like(wq[..., :MLA_NOPE_DIM]), -pe[..., half:], pe[..., :half]], axis=-1)
    pad = lambda w: jnp.pad(w, ((0, 0), (0, 0), (0, 0), (0, LANES - dq))).reshape(
        nl, MLA_Q_RANK, MLA_HEADS * LANES)
    wq = jnp.concatenate([pad(wq), pad(rot)], axis=-1)
    wkv = w_ukv.reshape(nl, MLA_KV_RANK, MLA_HEADS, MLA_NOPE_DIM + MLA_V_DIM)
    wk = jnp.pad(wkv[..., :MLA_NOPE_DIM], ((0, 0), (0, 0), (0, 0), (0, LANES - MLA_NOPE_DIM)))
    wk = wk.reshape(nl, MLA_KV_RANK, MLA_HEADS * LANES)
    wvt = wkv[..., MLA_NOPE_DIM:].reshape(nl, MLA_KV_RANK, MLA_V_ROWS).transpose(0, 2, 1)
    return wq.astype(BF16), wk.astype(BF16), wvt.astype(BF16)


def _out_weight(w_out):
    nl = w_out.shape[0]
    pad_rows = lambda w: jnp.pad(w, ((0, 0), (0, RET_PAIRS * LANES - w.shape[1]), (0, 0)))
    nsa = w_out[:, :NSA_Q_W].reshape(nl, NSA_HEADS, HEAD_DIM, D_MODEL)
    order = [h for j in range(NSA_REP) for h in (j, j + NSA_REP)]
    nsa = nsa[:, order].reshape(nl, NSA_Q_W, D_MODEL)
    mla = pad_rows(w_out[:, NSA_Q_W:NSA_Q_W + MLA_HEADS * MLA_V_DIM])
    ret = pad_rows(w_out[:, NSA_Q_W + MLA_HEADS * MLA_V_DIM:])
    return jnp.concatenate([nsa, mla, ret], axis=1).astype(BF16)


def _retention_tables(gn_gain):
    nh = 2 * RET_PAIRS
    log_g = jnp.log(1.0 - 2.0 ** (-5.0 - jnp.arange(nh, dtype=F32)))
    i = jnp.arange(RET_CHUNK, dtype=F32)
    diff = i[:, None] - i[None, :]
    intra = jnp.where(diff >= 0, jnp.exp(jnp.maximum(diff, 0.0)[None] * log_g[:, None, None]), 0.0)
    read_decay = jnp.exp((i + 1.0)[None, :] * log_g[:, None])
    write_decay = jnp.exp((RET_CHUNK - 1.0 - i)[None, :] * log_g[:, None])
    chunk_decay = jnp.exp(RET_CHUNK * log_g)

    def lanes(t):
        t = t.reshape(RET_PAIRS, 2, -1)
        return jnp.repeat(t.transpose(0, 2, 1), HALF, axis=-1)

    gn = jnp.pad(gn_gain, ((0, 0), (0, nh - RET_HEADS), (0, 0)))
    gn = gn.reshape(gn.shape[0], RET_PAIRS, 1, LANES)
    return (gn, intra.reshape(RET_PAIRS, 2, RET_CHUNK, RET_CHUNK), lanes(read_decay), lanes(write_decay),
            lanes(chunk_decay[:, None]))


def _selection_overlap(seq):
    n_cmp = (seq - CMP_LEN) // CMP_STRIDE + 1
    n_sel = seq // SEL_BLOCK
    cs = np.arange(n_cmp) * CMP_STRIDE
    ss = np.arange(n_sel) * SEL_BLOCK
    ov = np.clip(np.minimum(cs[:, None] + CMP_LEN, ss[None, :] + SEL_BLOCK)
                 - np.maximum(cs[:, None], ss[None, :]), 0, None) / CMP_LEN
    ovl_t = np.zeros((n_sel, seq // CMP_STRIDE), np.float32)
    ovl_t[:, :n_cmp] = ov.T
    return jnp.asarray(ovl_t, BF16)


def kernel(x, positions, ln1_gain, w_in, cmp_pos_k, cmp_w1_k, cmp_w2_k, cmp_pos_v, cmp_w1_v, cmp_w2_v,
           mla_q_norm, mla_w_uq, mla_kv_norm, mla_w_ukv, ret_gn_gain, w_out, ln2_gain, w_up, w_down,
           final_gain):
    batch, seq, _ = x.shape
    depth = w_in.shape[0]
    t = batch * seq

    tab_n, tab_m, tab_r = _rope_tables(positions)

    w_in_p, w_in_t = _in_weight(w_in)
    pos_k, w1_k, w2_k = _compress_weights(cmp_pos_k, cmp_w1_k, cmp_w2_k)
    pos_v, w1_v, w2_v = _compress_weights(cmp_pos_v, cmp_w1_v, cmp_w2_v)
    cmp_pos = jnp.stack([pos_k, pos_v], axis=1)
    cmp_w1 = jnp.stack([w1_k, w1_v], axis=1)
    w2_vt = w2_v.transpose(0, 2, 1)
    wq, wk, wvt = _mla_weights(mla_w_uq, mla_w_ukv)
    wo = _out_weight(w_out)
    wu = w_up.astype(BF16)
    wd = w_down.astype(BF16)
    gn, intra, rd, wdec, cd = _retention_tables(ret_gn_gain)
    ovl_t = _selection_overlap(seq)
    gf = final_gain.reshape(1, D_MODEL)

    x2 = x.reshape(t, D_MODEL)
    for l in range(depth):
        nsa, nsa_t, gate_t, k_cmp, v_cmp, ret, rgate, q_m, k_m, vt_m = _in_proj(
            x2, ln1_gain[l].reshape(1, D_MODEL), w_in_p, w_in_t, mla_q_norm[l].reshape(1, -1),
            mla_kv_norm[l].reshape(1, -1), wq, wk, wvt, l, tab_n, tab_m, tab_r)
        kc, vct = _compress(k_cmp, v_cmp, cmp_pos[l], cmp_w1, w2_k[l], w2_vt[l], l, batch, seq)
        o_nsa = _nsa_attention(nsa, nsa_t, gate_t, kc, vct, ovl_t, batch, seq)
        o_mla = _mla_attention(q_m, k_m, vt_m, batch, seq)
        o_ret = _retention(ret, rgate, gn[l], intra, rd, wdec, cd, batch, seq)
        x2 = _out_mlp(x2, o_nsa, o_mla, o_ret, wo, ln2_gain[l].reshape(1, D_MODEL), wu, wd, l, gf,
                      final_norm=(l == depth - 1))
    return x2.reshape(batch, seq, D_MODEL)
```

```python
import functools
import math

import numpy as np
import jax
import jax.numpy as jnp
from jax import lax
from jax.experimental import pallas as pl
from jax.experimental.pallas import tpu as pltpu

F32 = jnp.float32
BF16 = jnp.bfloat16

D_MODEL = 1024
HEAD_DIM = 64
NSA_HEADS = 6
NSA_KV_GROUPS = 2
NSA_REP = NSA_HEADS // NSA_KV_GROUPS
N_BRANCH = 3
CMP_LEN = 32
CMP_STRIDE = 16
CMP_HIDDEN = 2 * HEAD_DIM
SEL_BLOCK = 64
SEL_TOP_N = 16
WINDOW = 512
MLA_HEADS = 5
MLA_Q_RANK = 256
MLA_KV_RANK = 128
MLA_NOPE_DIM = 64
MLA_ROPE_DIM = 32
MLA_V_DIM = 64
RET_HEADS = 5
RET_CHUNK = 128
ROPE_THETA = 500000.0
PARTIAL_ROPE_DIM = HEAD_DIM // 4
RET_THETA = 10000.0
D_FF = 4 * D_MODEL
NORM_EPS = 1e-6
NEG_INF = -1e30
FORCE_SCORE = 1e9
LOG2E = math.log2(math.e)

NSA_Q_W = NSA_HEADS * HEAD_DIM
NSA_KV_W = NSA_KV_GROUPS * HEAD_DIM
NSA_GATE_W = NSA_HEADS * N_BRANCH
RET_W = RET_HEADS * HEAD_DIM
IN_SIZES = (NSA_Q_W, NSA_KV_W, NSA_KV_W, NSA_KV_W, NSA_KV_W, NSA_KV_W, NSA_KV_W, NSA_GATE_W,
            MLA_Q_RANK, MLA_KV_RANK, MLA_ROPE_DIM, RET_W, RET_W, RET_W, RET_W)

LANES = 128
HALF = LANES // 2
SUBLANES = 8
VMEM_LIMIT = 56 * 1024 * 1024

NSA_TILES = 5
GATE_ROWS = 32
NSA_T_ROWS = 2 * LANES + GATE_ROWS
MLA_IN_TILES = 4
RET_PAIRS = 3
RET_SLOTS = (0, 1, 2, 3, None, 4)
RET_K_TILES = RET_PAIRS - 1
RET_ROPE_TILES = RET_PAIRS + RET_K_TILES
RET_MM_TILES = RET_ROPE_TILES + RET_PAIRS
RET_TILES = 3 * RET_PAIRS
IN_TILES = NSA_TILES + 2 + MLA_IN_TILES + RET_MM_TILES + RET_PAIRS
N_PAD = IN_TILES * LANES
MLA_V_ROWS = MLA_HEADS * MLA_V_DIM

IN_TM = 512
IN_CHUNK_TILES = 8
MLP_TM = 512
ATT_T = 512
MLP_FF_CHUNK = 1024


def _nn(a, b):
    return jnp.dot(a, b, preferred_element_type=F32)


def _nt(a, b):
    return lax.dot_general(a, b, (((1,), (1,)), ((), ())), preferred_element_type=F32)


def _rms(x, gain):
    return x * lax.rsqrt(jnp.mean(x * x, axis=-1, keepdims=True) + NORM_EPS) * gain


def _rope(val, tab, half):
    cos = tab[:, 0:LANES]
    sin_a = tab[:, LANES:2 * LANES]
    sin_b = tab[:, 2 * LANES:3 * LANES]
    return (val * cos + pltpu.roll(val, LANES - half, 1) * sin_a
            + pltpu.roll(val, half, 1) * sin_b)


def _lane_lo(shape):
    return lax.broadcasted_iota(jnp.int32, shape, len(shape) - 1) < HALF


def _mla_up(c, qn_ref, kvn_ref, wq_ref, wk_ref, wvt_ref, tm_ref, q_ref, k_ref, vt_ref):
    scale = (MLA_NOPE_DIM + MLA_ROPE_DIM) ** -0.5 * LOG2E
    cq = _rms(c[:, 0:MLA_Q_RANK], qn_ref[...]).astype(BF16)
    ckv = _rms(c[:, MLA_Q_RANK:MLA_Q_RANK + MLA_KV_RANK], kvn_ref[...]).astype(BF16)
    k_pe = _rope(c[:, 3 * LANES:4 * LANES], tm_ref[...], MLA_ROPE_DIM // 2)
    cos = tm_ref[:, 0:LANES] * scale
    sin = (tm_ref[:, 2 * LANES:3 * LANES] - tm_ref[:, LANES:2 * LANES]) * scale
    q = _nn(cq, wq_ref[...])
    k = _nn(ckv, wk_ref[...])
    w = MLA_HEADS * LANES
    for hd in range(MLA_HEADS):
        sl = slice(hd * LANES, (hd + 1) * LANES)
        rot = slice(w + hd * LANES, w + (hd + 1) * LANES)
        q_ref[:, sl] = (q[:, sl] * cos + q[:, rot] * sin).astype(BF16)
        k_ref[:, sl] = (k[:, sl] + k_pe).astype(BF16)
    vt_ref[...] = _nt(wvt_ref[...], ckv).astype(BF16)


def _inproj_kernel(x_ref, g_ref, w_ref, wt_ref, tn_ref, tm_ref, tr_ref, qn_ref, kvn_ref, wq_ref, wk_ref,
                   wvt_ref, nsa_ref, nsat_ref, gatet_ref, kcmp_ref, vcmp_ref, ret_ref, rgate_ref,
                   mq_ref, mk_ref, mvt_ref):
    h = _rms(x_ref[...], g_ref[...]).astype(BF16)
    tab_n = tn_ref[...]
    tab_r = tr_ref[...]

    chunks = [_nn(h, w_ref[:, c:min(c + IN_CHUNK_TILES * LANES, N_PAD)])
              for c in range(0, N_PAD, IN_CHUNK_TILES * LANES)]

    def tiles(first, n):
        cols = [chunks[i // IN_CHUNK_TILES][:, (i % IN_CHUNK_TILES) * LANES:(i % IN_CHUNK_TILES + 1) * LANES]
                for i in range(first, first + n)]
        return cols[0] if n == 1 else jnp.concatenate(cols, axis=1)

    t0 = 0
    for i in range(NSA_TILES):
        v = _rope(tiles(t0 + i, 1), tab_n, PARTIAL_ROPE_DIM // 2)
        if i < NSA_REP:
            v = v * LOG2E
        nsa_ref[:, i * LANES:(i + 1) * LANES] = v.astype(BF16)
    t0 += NSA_TILES
    kcmp_ref[...] = _rope(tiles(t0, 1), tab_n, PARTIAL_ROPE_DIM // 2)
    vcmp_ref[...] = tiles(t0 + 1, 1)
    t0 += 2
    _mla_up(tiles(t0, MLA_IN_TILES), qn_ref, kvn_ref, wq_ref, wk_ref, wvt_ref, tm_ref, mq_ref, mk_ref, mvt_ref)
    t0 += MLA_IN_TILES
    stored = []
    for i in range(RET_MM_TILES):
        v = tiles(t0 + i, 1)
        if i < RET_ROPE_TILES:
            v = _rope(v, tab_r, HEAD_DIM // 2)
        stored.append(v)
        if i == RET_ROPE_TILES - 1:
            stored.append(pltpu.roll(stored[RET_PAIRS - 1], HALF, 1))
    for i, v in enumerate(stored):
        ret_ref[:, i * LANES:(i + 1) * LANES] = v.astype(BF16)
    t0 += RET_MM_TILES
    rgate_ref[...] = tiles(t0, RET_PAIRS)
    at = _nt(wt_ref[...], h)
    nsat_ref[...] = at[0:2 * LANES, :].astype(BF16)
    gatet_ref[...] = at[2 * LANES:NSA_T_ROWS, :]


def _layer_spec(w, layer, **kwargs):
    zeros = (0,) * (w.ndim - 1)
    return pl.BlockSpec((None,) + w.shape[1:], lambda *_: (layer,) + zeros, **kwargs)


def _in_proj(x2, gain, w, wt, q_norm, kv_norm, wq, wk, wvt, layer, tab_n, tab_m, tab_r):
    t = x2.shape[0]
    row = lambda i: (i, 0)
    col = lambda i: (0, i)
    const = lambda i: (0, 0)
    out_shapes = (
        jax.ShapeDtypeStruct((t, NSA_TILES * LANES), BF16),
        jax.ShapeDtypeStruct((2 * LANES, t), BF16),
        jax.ShapeDtypeStruct((GATE_ROWS, t), F32),
        jax.ShapeDtypeStruct((t, LANES), F32),
        jax.ShapeDtypeStruct((t, LANES), F32),
        jax.ShapeDtypeStruct((t, RET_TILES * LANES), BF16),
        jax.ShapeDtypeStruct((t, RET_PAIRS * LANES), F32),
        jax.ShapeDtypeStruct((t, MLA_HEADS * LANES), BF16),
        jax.ShapeDtypeStruct((t, MLA_HEADS * LANES), BF16),
        jax.ShapeDtypeStruct((MLA_V_ROWS, t), BF16),
    )
    out_specs = tuple(
        pl.BlockSpec((s.shape[0], IN_TM), col) if s.shape[1] == t else pl.BlockSpec((IN_TM, s.shape[1]), row)
        for s in out_shapes)
    return pl.pallas_call(
        _inproj_kernel,
        grid=(t // IN_TM,),
        in_specs=[
            pl.BlockSpec((IN_TM, D_MODEL), row),
            pl.BlockSpec((1, D_MODEL), const),
            _layer_spec(w, layer),
            _layer_spec(wt, layer),
            pl.BlockSpec((IN_TM, 3 * LANES), row),
            pl.BlockSpec((IN_TM, 3 * LANES), row),
            pl.BlockSpec((IN_TM, 3 * LANES), row),
            pl.BlockSpec((1, MLA_Q_RANK), const),
            pl.BlockSpec((1, MLA_KV_RANK), const),
            _layer_spec(wq, layer),
            _layer_spec(wk, layer),
            _layer_spec(wvt, layer),
        ],
        out_specs=out_specs,
        out_shape=out_shapes,
        compiler_params=pltpu.CompilerParams(
            dimension_semantics=("parallel",), vmem_limit_bytes=VMEM_LIMIT),
        name="in_proj",
    )(x2, gain, w, wt, tab_n, tab_m, tab_r, q_norm, kv_norm, wq, wk, wvt)


def _compress_kernel(k_ref, v_ref, pos_ref, w1_ref, w2k_ref, w2vt_ref, kc_ref, vct_ref):
    n_blk = k_ref.shape[0] // CMP_STRIDE

    def hidden(src, i):
        lo = jnp.zeros((n_blk, NSA_KV_GROUPS * CMP_HIDDEN), F32)
        hi = jnp.zeros((n_blk, NSA_KV_GROUPS * CMP_HIDDEN), F32)
        for r in range(CMP_STRIDE):
            tok = src[pl.ds(r, n_blk, stride=CMP_STRIDE), :]
            lo = lo + _nn((tok + pos_ref[i, r:r + 1, :]).astype(BF16), w1_ref[i, r])
            r2 = CMP_STRIDE + r
            hi = hi + _nn((tok + pos_ref[i, r2:r2 + 1, :]).astype(BF16), w1_ref[i, r2])
        return jax.nn.gelu(lo + pltpu.roll(hi, n_blk - 1, 0)).astype(BF16)

    kc_ref[0] = _nn(hidden(k_ref, 0), w2k_ref[...]).astype(BF16)
    vct_ref[0] = _nt(w2vt_ref[...], hidden(v_ref, 1)).astype(BF16)


def _compress(k_cmp, v_cmp, pos, w1, w2k, w2vt, layer, batch, seq):
    b = batch
    n_blk = seq // CMP_STRIDE
    return pl.pallas_call(
        _compress_kernel,
        grid=(b,),
        in_specs=[
            pl.BlockSpec((seq, LANES), lambda i: (i, 0)),
            pl.BlockSpec((seq, LANES), lambda i: (i, 0)),
            pl.BlockSpec(pos.shape, lambda i: (0, 0, 0)),
            _layer_spec(w1, layer),
            pl.BlockSpec(w2k.shape, lambda i: (0, 0)),
            pl.BlockSpec(w2vt.shape, lambda i: (0, 0)),
        ],
        out_specs=(pl.BlockSpec((1, n_blk, LANES), lambda i: (i, 0, 0)),
                   pl.BlockSpec((1, LANES, n_blk), lambda i: (i, 0, 0))),
        out_shape=(jax.ShapeDtypeStruct((b, n_blk, LANES), BF16),
                   jax.ShapeDtypeStruct((b, LANES, n_blk), BF16)),
        compiler_params=pltpu.CompilerParams(
            dimension_semantics=("parallel",), vmem_limit_bytes=VMEM_LIMIT),
        name="nsa_compress",
    )(k_cmp, v_cmp, pos, w1, w2k, w2vt)


SCORE_LOOKAHEAD = 2
ONES_ROWS = 16


def _tri_scores(k, q, dead_upper, mask, bias=None):
    half = k.shape[0] // 2
    dead = jnp.full((half, half), NEG_INF, F32)
    if dead_upper:
        top = _nt(k[:half], q)
        bottom = jnp.concatenate([dead, _nt(k[half:], q[half:])], axis=1)
    else:
        top = jnp.concatenate([_nt(k[:half], q[:half]), dead], axis=1)
        bottom = _nt(k[half:], q)
    s_t = jnp.concatenate([top, bottom], axis=0)
    return jnp.where(mask, s_t if bias is None else s_t + bias, NEG_INF)


def _tri_pv(dead_upper, v_ext, p):
    half = p.shape[0] // 2
    if dead_upper:
        full = _nn(v_ext[:, :half], p[:half])
        part = _nn(v_ext[:, half:], p[half:, half:])
        return jnp.concatenate([full[:, :half], full[:, half:] + part], axis=1)
    part = _nn(v_ext[:, :half], p[:half, :half])
    full = _nn(v_ext[:, half:], p[half:])
    return jnp.concatenate([full[:, :half] + part, full[:, half:]], axis=1)


def _softmax_steps(score_fns, v_ts, states, pv_fn=_nn):
    n = len(score_fns)
    s_ts = [score_fns[h]() if h < SCORE_LOOKAHEAD else None for h in range(n)]
    out = []
    for h in range(n):
        if h + SCORE_LOOKAHEAD < n:
            s_ts[h + SCORE_LOOKAHEAD] = score_fns[h + SCORE_LOOKAHEAD]()
        m_old, acc_old = states[h]
        m = jnp.maximum(m_old, jnp.max(s_ts[h], axis=0, keepdims=True))
        p = jnp.exp2(s_ts[h] - m).astype(BF16)
        s_ts[h] = None
        v_ext = jnp.concatenate([v_ts[h], jnp.ones((ONES_ROWS, v_ts[h].shape[1]), BF16)], axis=0)
        out.append((m, jnp.exp2(m_old - m) * acc_old + pv_fn(v_ext, p)))
    return tuple(out)


def _softmax_init(cols):
    return jnp.full((1, cols), NEG_INF, F32), jnp.zeros((HEAD_DIM + ONES_ROWS, cols), F32)


def _softmax_finish(state):
    acc = state[1]
    return acc[0:HEAD_DIM] * (1.0 / jnp.maximum(acc[HEAD_DIM:HEAD_DIM + 1], 1e-30))


def _nsa_kernel(q_ref, gatet_ref, kc_ref, vct_ref, ks_ref, kw_ref, vst_ref, vwt_ref, ovl_ref,
                o_ref, out_t_ref, selb_ref):
    tq = ATT_T
    qi = pl.program_id(1)
    q0 = pl.multiple_of(qi * tq, tq)
    lo1 = _lane_lo((1, LANES))
    group_lanes = (lo1, jnp.logical_not(lo1))
    gate = jax.nn.sigmoid(gatet_ref[...])
    heads = range(NSA_HEADS)

    def q_head(h):
        j, g = h % NSA_REP, h // NSA_REP
        tile = q_ref[:, j * LANES:(j + 1) * LANES]
        return jnp.where(group_lanes[g], tile, jnp.zeros_like(tile))

    def v_rows(ref, h, k0, n):
        g = h // NSA_REP
        return ref[g * HEAD_DIM:(g + 1) * HEAD_DIM, pl.ds(k0, n)]

    def emit(h, branch, qs, nq, o_t):
        j, g = h % NSA_REP, h // NSA_REP
        r0 = j * LANES + g * HEAD_DIM
        row = h * N_BRANCH + branch
        val = gate[row:row + 1, qs:qs + nq] * o_t
        if branch == 0:
            out_t_ref[r0:r0 + HEAD_DIM, qs:qs + nq] = val
        else:
            out_t_ref[r0:r0 + HEAD_DIM, qs:qs + nq] += val

    qm = [q_head(h) for h in heads]

    n_cmp_pad = kc_ref.shape[1]
    kc = kc_ref[0]
    vct = vct_ref[0]
    n_i = lax.broadcasted_iota(jnp.int32, (n_cmp_pad, tq), 0)
    t_l = q0 + lax.broadcasted_iota(jnp.int32, (n_cmp_pad, tq), 1)
    cmask = (n_i * CMP_STRIDE + (CMP_LEN - 1)) <= t_l
    cmask_f = cmask.astype(F32)
    s_cs = [jnp.where(cmask, _nt(kc, qm[h]), NEG_INF) for h in heads]
    p_cs = [jnp.exp2(s_t - jnp.max(s_t, axis=0, keepdims=True)) * cmask_f for s_t in s_cs]
    p_cs = [p * (1.0 / jnp.maximum(jnp.sum(p, axis=0, keepdims=True), 1e-30)) for p in p_cs]
    for h in heads:
        g = h // NSA_REP
        emit(h, 0, 0, tq, _nn(vct[g * HEAD_DIM:(g + 1) * HEAD_DIM, :], p_cs[h].astype(BF16)))

    n_sel = ovl_ref.shape[0]
    needs_rank = q0 + tq > SEL_TOP_N * SEL_BLOCK

    @pl.when(jnp.logical_not(needs_rank))
    def _():
        selb_ref[...] = jnp.zeros_like(selb_ref)

    @pl.when(needs_rank)
    def _():
        m_i = lax.broadcasted_iota(jnp.int32, (n_sel, tq), 0)
        cur = jnp.right_shift(q0 + lax.broadcasted_iota(jnp.int32, (n_sel, tq), 1),
                              SEL_BLOCK.bit_length() - 1)
        valid = m_i <= cur
        forced = (m_i == 0) | (m_i == cur) | (m_i == cur - 1)
        ovl = ovl_ref[...]
        sub = SUBLANES
        m_loc = lax.broadcasted_iota(jnp.int32, (sub, tq), 0)
        for g in range(NSA_KV_GROUPS):
            psum = p_cs[g * NSA_REP]
            for h in range(g * NSA_REP + 1, (g + 1) * NSA_REP):
                psum = psum + p_cs[h]
            p_hi = psum.astype(BF16)
            p_lo = (psum - p_hi.astype(F32)).astype(BF16)
            imp = _nn(ovl, p_hi) + _nn(ovl, p_lo)
            imp = jnp.where(valid & forced, FORCE_SCORE, imp)
            imp = jnp.where(valid, imp, NEG_INF)
            parts = [imp[i:i + sub] for i in range(0, n_sel, sub)]
            ranks = [jnp.zeros((sub, tq), jnp.int32) for _ in parts]
            for mp in range(n_sel):
                row = imp[mp:mp + 1, :]
                for i, part in enumerate(parts):
                    if i * sub + sub - 1 <= mp:
                        beats = row > part
                    elif i * sub > mp:
                        beats = row >= part
                    else:
                        beats = (row > part) | ((row == part) & (m_loc + i * sub > mp))
                    ranks[i] = ranks[i] + beats.astype(jnp.int32)
            rank = jnp.concatenate(ranks, axis=0)
            selb_ref[g] = jnp.where(rank < SEL_TOP_N, 0.0, NEG_INF)

    blocks_per_chunk = tq // SEL_BLOCK

    def sel_bias(g, c, n_keys, qs, nq):
        rows = [jnp.broadcast_to(selb_ref[g, pl.ds(c * blocks_per_chunk + i, 1), qs:qs + nq], (SEL_BLOCK, nq))
                for i in range(n_keys // SEL_BLOCK)]
        return jnp.concatenate(rows, axis=0)

    key_i = lax.broadcasted_iota(jnp.int32, (tq, tq), 0)
    qry_i = lax.broadcasted_iota(jnp.int32, (tq, tq), 1)
    causal_mask = key_i <= qry_i

    def sel_chunk(c, k0, states, causal):
        k = ks_ref[pl.ds(k0, tq), :]
        bias = [sel_bias(g, c, tq, 0, tq) for g in range(NSA_KV_GROUPS)]

        def score_fn(h):
            if causal:
                return _tri_scores(k, qm[h], True, causal_mask, bias[h // NSA_REP])
            return _nt(k, qm[h]) + bias[h // NSA_REP]

        return _softmax_steps([functools.partial(score_fn, h) for h in heads],
                              [v_rows(vst_ref, h, k0, tq) for h in heads], states,
                              functools.partial(_tri_pv, True) if causal else _nn)

    states = sel_chunk(qi, q0, tuple(_softmax_init(tq) for _ in heads), True)
    states = lax.fori_loop(
        0, qi, lambda c, st: sel_chunk(c, pl.multiple_of(c * tq, tq), st, False), states)
    for h in heads:
        emit(h, 1, 0, tq, _softmax_finish(states[h]))

    def win_chunk(k0, states, mask, dead_upper):
        k = kw_ref[pl.ds(k0, tq), :]
        if dead_upper is None:
            score_fns = [lambda h=h: jnp.where(mask, _nt(k, qm[h]), NEG_INF) for h in heads]
            pv_fn = _nn
        else:
            score_fns = [lambda h=h: _tri_scores(k, qm[h], dead_upper, mask) for h in heads]
            pv_fn = functools.partial(_tri_pv, dead_upper)
        return _softmax_steps(score_fns, [v_rows(vwt_ref, h, k0, tq) for h in heads], states, pv_fn)

    states = win_chunk(q0, tuple(_softmax_init(tq) for _ in heads), causal_mask, True)
    for d in range(1, WINDOW // tq + 1):
        k0 = pl.multiple_of(jnp.maximum(qi - d, 0) * tq, tq)
        in_band = (key_i - d * tq > qry_i - WINDOW) & (qi >= d)
        states = win_chunk(k0, states, in_band, False if d * tq == WINDOW else None)
    for h in heads:
        emit(h, 2, 0, tq, _softmax_finish(states[h]))

    for j in range(NSA_REP):
        o_ref[:, j * LANES:(j + 1) * LANES] = out_t_ref[j * LANES:(j + 1) * LANES, :].T.astype(BF16)


def _nsa_attention(nsa, nsa_t, gate_t, kc, vct, ovl_t, batch, seq):
    tq = ATT_T
    nq = seq // tq
    n_blk = kc.shape[1]
    qrow = lambda b, i: (b * nq + i, 0)
    return pl.pallas_call(
        _nsa_kernel,
        grid=(batch, nq),
        in_specs=[
            pl.BlockSpec((tq, NSA_REP * LANES), qrow),
            pl.BlockSpec((GATE_ROWS, tq), lambda b, i: (0, b * nq + i)),
            pl.BlockSpec((1, n_blk, LANES), lambda b, i: (b, 0, 0)),
            pl.BlockSpec((1, LANES, n_blk), lambda b, i: (b, 0, 0)),
            pl.BlockSpec((seq, LANES), lambda b, i: (b, 3)),
            pl.BlockSpec((seq, LANES), lambda b, i: (b, 4)),
            pl.BlockSpec((LANES, seq), lambda b, i: (0, b)),
            pl.BlockSpec((LANES, seq), lambda b, i: (1, b)),
            pl.BlockSpec(ovl_t.shape, lambda b, i: (0, 0)),
        ],
        out_specs=pl.BlockSpec((tq, NSA_REP * LANES), qrow),
        out_shape=jax.ShapeDtypeStruct((batch * seq, NSA_REP * LANES), BF16),
        scratch_shapes=[pltpu.VMEM((NSA_REP * LANES, tq), F32),
                        pltpu.VMEM((NSA_KV_GROUPS, seq // SEL_BLOCK, tq), F32)],
        compiler_params=pltpu.CompilerParams(
            dimension_semantics=("parallel", "arbitrary"), vmem_limit_bytes=VMEM_LIMIT),
        name="nsa_attention",
    )(nsa, gate_t, kc, vct, nsa, nsa, nsa_t, nsa_t, ovl_t)


def _mla_attn_kernel(q_ref, k_ref, vt_ref, o_ref, out_t_ref):
    tq = ATT_T
    qi = pl.program_id(1)
    q0 = pl.multiple_of(qi * tq, tq)
    heads = range(MLA_HEADS)
    qh = [q_ref[:, h * LANES:(h + 1) * LANES] for h in heads]

    def chunk(k0, states, causal):
        if causal:
            mask = (lax.broadcasted_iota(jnp.int32, (tq, tq), 0)
                    <= lax.broadcasted_iota(jnp.int32, (tq, tq), 1))

        def score_fn(h):
            k = k_ref[pl.ds(k0, tq), h * LANES:(h + 1) * LANES]
            return _tri_scores(k, qh[h], True, mask) if causal else _nt(k, qh[h])

        v_ts = [vt_ref[h * MLA_V_DIM:(h + 1) * MLA_V_DIM, pl.ds(k0, tq)] for h in heads]
        return _softmax_steps([functools.partial(score_fn, h) for h in heads], v_ts, states,
                              functools.partial(_tri_pv, True) if causal else _nn)

    states = chunk(q0, tuple(_softmax_init(tq) for _ in heads), True)
    states = lax.fori_loop(0, qi, lambda c, st: chunk(pl.multiple_of(c * tq, tq), st, False), states)
    for h in heads:
        out_t_ref[h * MLA_V_DIM:(h + 1) * MLA_V_DIM, :] = _softmax_finish(states[h])
    out_t_ref[MLA_V_ROWS:, :] = jnp.zeros((RET_PAIRS * LANES - MLA_V_ROWS, tq), F32)
    for j in range(RET_PAIRS):
        o_ref[:, j * LANES:(j + 1) * LANES] = out_t_ref[j * LANES:(j + 1) * LANES, :].T.astype(BF16)


def _mla_attention(q, k, vt, batch, seq):
    tq = ATT_T
    nq = seq // tq
    return pl.pallas_call(
        _mla_attn_kernel,
        grid=(batch, nq),
        in_specs=[
            pl.BlockSpec((tq, MLA_HEADS * LANES), lambda b, i: (b * nq + i, 0)),
            pl.BlockSpec((seq, MLA_HEADS * LANES), lambda b, i: (b, 0)),
            pl.BlockSpec((MLA_V_ROWS, seq), lambda b, i: (0, b)),
        ],
        out_specs=pl.BlockSpec((tq, RET_PAIRS * LANES), lambda b, i: (b * nq + i, 0)),
        out_shape=jax.ShapeDtypeStruct((batch * seq, RET_PAIRS * LANES), BF16),
        scratch_shapes=[pltpu.VMEM((RET_PAIRS * LANES, tq), F32)],
        compiler_params=pltpu.CompilerParams(
            dimension_semantics=("parallel", "arbitrary"), vmem_limit_bytes=VMEM_LIMIT),
        name="mla_attention",
    )(q, k, vt)


def _tn(a, b):
    return lax.dot_general(a, b, (((0,), (0,)), ((), ())), preferred_element_type=F32)


RET_UNROLL = 16


def _retention_kernel(q_ref, k_ref, v_ref, gate_ref, gn_ref, intra_ref, rd_ref, wd_ref, cd_ref, o_ref,
                      kv_ref, st_ref):
    c_len = RET_CHUNK
    n_chunks = q_ref.shape[0] // c_len
    lo = _lane_lo((1, LANES))
    hi = jnp.logical_not(lo)
    blockdiag = (lax.broadcasted_iota(jnp.int32, (LANES, LANES), 0) < HALF) == _lane_lo((LANES, LANES))
    intra_a = intra_ref[0, 0]
    intra_b = intra_ref[0, 1]
    read_decay = rd_ref[0]
    write_decay = wd_ref[0]
    chunk_decay = cd_ref[0]
    gn = gn_ref[0]

    averager = jnp.where(blockdiag, 1.0 / HEAD_DIM, 0.0).astype(BF16)

    def half_mean(x):
        x_hi = x.astype(BF16)
        x_lo = (x - x_hi.astype(F32)).astype(BF16)
        return _nn(x_hi, averager) + _nn(x_lo, averager)

    def kv_body(c, carry):
        r0 = pl.multiple_of(c * c_len, c_len)
        kc = k_ref[pl.ds(r0, c_len), :]
        kv_ref[c] = _tn((kc.astype(F32) * write_decay).astype(BF16), v_ref[pl.ds(r0, c_len), :])
        return carry

    lax.fori_loop(0, n_chunks, kv_body, 0, unroll=RET_UNROLL)

    state = jnp.zeros((LANES, LANES), F32)
    for c in range(n_chunks):
        st_ref[c] = jnp.where(blockdiag, state, 0.0).astype(BF16)
        state = state * chunk_decay + kv_ref[c]

    def out_body(step, carry):
        cs = [step * RET_UNROLL + u for u in range(RET_UNROLL)]
        rows = [pl.ds(pl.multiple_of(c * c_len, c_len), c_len) for c in cs]
        qs = [q_ref[r, :] for r in rows]
        ks = [k_ref[r, :] for r in rows]
        vs = [v_ref[r, :] for r in rows]
        zero = jnp.zeros_like(qs[0])
        sa = [(_nt(jnp.where(lo, q, zero), k) * intra_a).astype(BF16) for q, k in zip(qs, ks)]
        sb = [(_nt(jnp.where(hi, q, zero), k) * intra_b).astype(BF16) for q, k in zip(qs, ks)]
        cross = [_nn(q, st_ref[c]) * read_decay for q, c in zip(qs, cs)]
        os_ = [jnp.where(lo, _nn(a, v), _nn(b, v)) + x for a, b, v, x in zip(sa, sb, vs, cross)]
        ds_ = [o - half_mean(o) for o in os_]
        ys = [d * lax.rsqrt(half_mean(d * d) + NORM_EPS) * gn for d in ds_]
        for r, y in zip(rows, ys):
            o_ref[r, :] = (jax.nn.silu(gate_ref[r, :]) * y).astype(BF16)
        return carry

    lax.fori_loop(0, n_chunks // RET_UNROLL, out_body, 0)


def _retention(ret, rgate, gn, intra, rd, wd, cd, batch, seq):
    pair_const3 = lambda b, j: (j, 0, 0)
    return pl.pallas_call(
        _retention_kernel,
        grid=(batch, RET_PAIRS),
        in_specs=[
            pl.BlockSpec((seq, LANES), lambda b, j: (b, j)),
            pl.BlockSpec((seq, LANES), lambda b, j: (b, RET_PAIRS + j)),
            pl.BlockSpec((seq, LANES), lambda b, j: (b, 2 * RET_PAIRS + j)),
            pl.BlockSpec((seq, LANES), lambda b, j: (b, j)),
            pl.BlockSpec((1, 1, LANES), pair_const3),
            pl.BlockSpec((1, 2, RET_CHUNK, RET_CHUNK), lambda b, j: (j, 0, 0, 0)),
            pl.BlockSpec((1, RET_CHUNK, LANES), pair_const3),
            pl.BlockSpec((1, RET_CHUNK, LANES), pair_const3),
            pl.BlockSpec((1, 1, LANES), pair_const3),
        ],
        out_specs=pl.BlockSpec((seq, LANES), lambda b, j: (b, j)),
        out_shape=jax.ShapeDtypeStruct((batch * seq, RET_PAIRS * LANES), BF16),
        scratch_shapes=[pltpu.VMEM((seq // RET_CHUNK, LANES, LANES), F32),
                        pltpu.VMEM((seq // RET_CHUNK, LANES, LANES), BF16)],
        compiler_params=pltpu.CompilerParams(
            dimension_semantics=("parallel", "arbitrary"), vmem_limit_bytes=VMEM_LIMIT),
        name="retention",
    )(ret, ret, ret, rgate, gn, intra, rd, wd, cd)


def _out_mlp_kernel(x_ref, nsa_ref, mla_ref, ret_ref, wo_ref, g2_ref, wu_ref, wd_ref, gf_ref, o_ref,
                    *, final_norm):
    full = 2 * LANES
    mixed = jnp.concatenate([nsa_ref[...], mla_ref[:, :full], mla_ref[:, full:] + ret_ref[:, full:],
                             ret_ref[:, :full]], axis=1)
    x = x_ref[...] + _nn(mixed, wo_ref[...])
    h = _rms(x, g2_ref[...]).astype(BF16)
    y = x
    for c in range(D_FF // MLP_FF_CHUNK):
        sl = slice(c * MLP_FF_CHUNK, (c + 1) * MLP_FF_CHUNK)
        u = jnp.maximum(_nn(h, wu_ref[:, sl]), 0.0)
        y = y + _nn((u * u).astype(BF16), wd_ref[sl, :])
    if final_norm:
        y = _rms(y, gf_ref[...])
    o_ref[...] = y


def _out_mlp(x2, o_nsa, o_mla, o_ret, wo, g2, wu, wd, layer, gf, final_norm):
    t = x2.shape[0]
    w = RET_PAIRS * LANES
    row = lambda i: (i, 0)
    const = lambda i: (0, 0)
    resident = dict(pipeline_mode=pl.Buffered(1))
    return pl.pallas_call(
        functools.partial(_out_mlp_kernel, final_norm=final_norm),
        grid=(t // MLP_TM,),
        in_specs=[
            pl.BlockSpec((MLP_TM, D_MODEL), row),
            pl.BlockSpec((MLP_TM, w), row),
            pl.BlockSpec((MLP_TM, w), row),
            pl.BlockSpec((MLP_TM, w), row),
            _layer_spec(wo, layer, **resident),
            pl.BlockSpec((1, D_MODEL), const),
            _layer_spec(wu, layer, **resident),
            _layer_spec(wd, layer, **resident),
            pl.BlockSpec((1, D_MODEL), const),
        ],
        out_specs=pl.BlockSpec((MLP_TM, D_MODEL), row),
        out_shape=jax.ShapeDtypeStruct((t, D_MODEL), F32),
        compiler_params=pltpu.CompilerParams(
            dimension_semantics=("parallel",), vmem_limit_bytes=VMEM_LIMIT),
        name="out_mlp",
    )(x2, o_nsa, o_mla, o_ret, wo, g2, wu, wd, gf)


ROPE_KINDS = (
    (PARTIAL_ROPE_DIM, ROPE_THETA, HEAD_DIM, 0),
    (MLA_ROPE_DIM, ROPE_THETA, LANES, HALF),
    (HEAD_DIM, RET_THETA, HEAD_DIM, 0),
)
ROPE_TM = 1024


def _rope_placement():
    n_angles = sum(dim // 2 for dim, _, _, _ in ROPE_KINDS)
    assert 2 * n_angles <= LANES
    place = np.zeros((LANES, 3 * LANES * len(ROPE_KINDS)), np.float32)
    fill = np.zeros((1, 3 * LANES * len(ROPE_KINDS)), np.float32)
    row0 = 0
    for kind, (dim, _, period, base) in enumerate(ROPE_KINDS):
        half = dim // 2
        col0 = kind * 3 * LANES
        for lane in range(LANES):
            rel = (lane - base) % period
            first = lane >= base and rel < half
            second = lane >= base and half <= rel < dim
            if first or second:
                angle = rel if first else rel - half
                place[row0 + angle, col0 + lane] = 1.0
                place[n_angles + row0 + angle, col0 + (1 if first else 2) * LANES + lane] = -1.0 if first else 1.0
            else:
                fill[0, col0 + lane] = 1.0
        row0 += half
    return jnp.asarray(place, BF16), jnp.asarray(fill)


def _rope_kernel(cs_ref, place_ref, fill_ref, *out_refs):
    x = cs_ref[...]
    x1 = x.astype(BF16)
    r1 = x - x1.astype(F32)
    x2 = r1.astype(BF16)
    x3 = (r1 - x2.astype(F32)).astype(BF16)
    place = place_ref[...]
    tab = _tn(x1, place) + _tn(x2, place) + _tn(x3, place) + fill_ref[...]
    for i, ref in enumerate(out_refs):
        ref[...] = tab[:, i * 3 * LANES:(i + 1) * 3 * LANES]


def _rope_tables(positions):
    inv = jnp.concatenate([1.0 / (theta ** (jnp.arange(0, dim, 2, dtype=F32) / dim))
                           for dim, theta, _, _ in ROPE_KINDS])
    ang = inv[:, None] * positions.reshape(-1).astype(F32)[None, :]
    compact = jnp.concatenate([jnp.cos(ang), jnp.sin(ang)], axis=0)
    compact = jnp.pad(compact, ((0, LANES - compact.shape[0]), (0, 0)))
    place, fill = _rope_placement()
    t = compact.shape[1]
    tab_shape = jax.ShapeDtypeStruct((t, 3 * LANES), F32)
    return pl.pallas_call(
        _rope_kernel,
        grid=(t // ROPE_TM,),
        in_specs=[pl.BlockSpec((LANES, ROPE_TM), lambda i: (0, i)),
                  pl.BlockSpec(place.shape, lambda i: (0, 0)),
                  pl.BlockSpec(fill.shape, lambda i: (0, 0))],
        out_specs=tuple(pl.BlockSpec((ROPE_TM, 3 * LANES), lambda i: (i, 0)) for _ in ROPE_KINDS),
        out_shape=tuple(tab_shape for _ in ROPE_KINDS),
        compiler_params=pltpu.CompilerParams(
            dimension_semantics=("parallel",), vmem_limit_bytes=VMEM_LIMIT),
        name="rope_tables",
    )(compact, place, fill)


def _pad_cols(w, n):
    return jnp.pad(w, ((0, 0), (0, 0), (0, n - w.shape[-1])))


def _in_weight(w_in):
    offs = np.cumsum((0,) + IN_SIZES)
    seg = [w_in[:, :, offs[i]:offs[i + 1]] for i in range(len(IN_SIZES))]
    (nsa_q, k_cmp, v_cmp, k_slc, v_slc, k_win, v_win, gate,
     cq, ckv, kpe, ret_q, ret_k, ret_v, ret_g) = seg
    scale = HEAD_DIM ** -0.5
    qh = [nsa_q[:, :, h * HEAD_DIM:(h + 1) * HEAD_DIM] * scale for h in range(NSA_HEADS)]
    tiles = [jnp.concatenate([qh[j], qh[j + NSA_REP]], axis=-1) for j in range(NSA_REP)]
    tiles += [k_slc, k_win, k_cmp, v_cmp]
    tiles += [cq, ckv, jnp.pad(kpe, ((0, 0), (0, 0), (HALF, LANES - HALF - MLA_ROPE_DIM)))]
    head = lambda w, h: w[:, :, h * HEAD_DIM:(h + 1) * HEAD_DIM]
    zero = jnp.zeros_like(head(ret_q, 0))
    slots = lambda w, pad: [pad if h is None else head(w, h) for h in RET_SLOTS]
    ret_k = ret_k * scale
    tiles += slots(ret_q, head(ret_k, RET_SLOTS[-1]))
    tiles += slots(ret_k, zero)[:2 * RET_K_TILES]
    tiles += slots(ret_v, zero) + slots(ret_g, zero)
    w_t = lax.optimization_barrier(
        jnp.concatenate([v_slc, v_win, _pad_cols(gate, GATE_ROWS)], axis=-1).astype(BF16))
    return jnp.concatenate(tiles, axis=-1).astype(BF16), w_t.transpose(0, 2, 1)


def _compress_weights(pos, w1, w2):
    nl = pos.shape[0]
    g, dh, hid = NSA_KV_GROUPS, HEAD_DIM, CMP_HIDDEN
    assert g == 2
    p = jnp.tile(pos, (1, 1, g))

    def block_diag2(w):
        lead = ((0, 0),) * (w.ndim - 1)
        return jnp.concatenate([jnp.pad(w, lead + ((0, w.shape[-1]),)),
                                jnp.pad(w, lead + ((w.shape[-1], 0),))], axis=-2)

    w = block_diag2(w1.astype(BF16).reshape(nl, CMP_LEN, dh, hid))
    return p, w, block_diag2(w2.astype(BF16))


def _mla_weights(w_uq, w_ukv):
    nl = w_uq.shape[0]
    dq = MLA_NOPE_DIM + MLA_ROPE_DIM
    wq = w_uq.reshape(nl, MLA_Q_RANK, MLA_HEADS, dq)
    pe = wq[..., MLA_NOPE_DIM:]
    half = MLA_ROPE_DIM // 2
    rot = jnp.concatenate([jnp.zeros_like(wq[..., :MLA_NOPE_DIM]), -pe[..., half:], pe[..., :half]], axis=-1)
    pad = lambda w: jnp.pad(w, ((0, 0), (0, 0), (0, 0), (0, LANES - dq))).reshape(
        nl, MLA_Q_RANK, MLA_HEADS * LANES)
    wq = jnp.concatenate([pad(wq), pad(rot)], axis=-1)
    wkv = w_ukv.reshape(nl, MLA_KV_RANK, MLA_HEADS, MLA_NOPE_DIM + MLA_V_DIM)
    wk = jnp.pad(wkv[..., :MLA_NOPE_DIM], ((0, 0), (0, 0), (0, 0), (0, LANES - MLA_NOPE_DIM)))
    wk = wk.reshape(nl, MLA_KV_RANK, MLA_HEADS * LANES)
    wvt = wkv[..., MLA_NOPE_DIM:].reshape(nl, MLA_KV_RANK, MLA_V_ROWS).transpose(0, 2, 1)
    return wq.astype(BF16), wk.astype(BF16), wvt.astype(BF16)


def _out_weight(w_out):
    nl = w_out.shape[0]
    nsa = w_out[:, :NSA_Q_W].reshape(nl, NSA_HEADS, HEAD_DIM, D_MODEL)
    order = [h for j in range(NSA_REP) for h in (j, j + NSA_REP)]
    nsa = nsa[:, order].reshape(nl, NSA_Q_W, D_MODEL)
    mla = w_out[:, NSA_Q_W:NSA_Q_W + MLA_HEADS * MLA_V_DIM]
    ret = w_out[:, NSA_Q_W + MLA_HEADS * MLA_V_DIM:]
    full = 2 * LANES
    return jnp.concatenate([nsa, mla[:, :full], mla[:, full:], ret[:, full:], ret[:, :full]],
                           axis=1).astype(BF16)


def _retention_tables(gn_gain):
    nh = 2 * RET_PAIRS
    slot_head = np.array([RET_HEADS if h is None else h for h in RET_SLOTS])
    log_g = jnp.log(1.0 - 2.0 ** (-5.0 - jnp.asarray(slot_head, F32)))
    i = jnp.arange(RET_CHUNK, dtype=F32)
    diff = i[:, None] - i[None, :]
    intra = jnp.where(diff >= 0, jnp.exp(jnp.maximum(diff, 0.0)[None] * log_g[:, None, None]), 0.0)
    read_decay = jnp.exp((i + 1.0)[None, :] * log_g[:, None])
    write_decay = jnp.exp((RET_CHUNK - 1.0 - i)[None, :] * log_g[:, None])
    chunk_decay = jnp.exp(RET_CHUNK * log_g)

    def lanes(t):
        t = t.reshape(RET_PAIRS, 2, -1)
        return jnp.repeat(t.transpose(0, 2, 1), HALF, axis=-1)

    gn = jnp.pad(gn_gain, ((0, 0), (0, nh - RET_HEADS), (0, 0)))[:, slot_head]
    gn = gn.reshape(gn.shape[0], RET_PAIRS, 1, LANES)
    return (gn, intra.reshape(RET_PAIRS, 2, RET_CHUNK, RET_CHUNK), lanes(read_decay), lanes(write_decay),
            lanes(chunk_decay[:, None]))


def _selection_overlap(seq):
    n_cmp = (seq - CMP_LEN) // CMP_STRIDE + 1
    n_sel = seq // SEL_BLOCK
    cs = np.arange(n_cmp) * CMP_STRIDE
    ss = np.arange(n_sel) * SEL_BLOCK
    ov = np.clip(np.minimum(cs[:, None] + CMP_LEN, ss[None, :] + SEL_BLOCK)
                 - np.maximum(cs[:, None], ss[None, :]), 0, None) / CMP_LEN
    ovl_t = np.zeros((n_sel, seq // CMP_STRIDE), np.float32)
    ovl_t[:, :n_cmp] = ov.T
    return jnp.asarray(ovl_t, BF16)


def kernel(x, positions, ln1_gain, w_in, cmp_pos_k, cmp_w1_k, cmp_w2_k, cmp_pos_v, cmp_w1_v, cmp_w2_v,
           mla_q_norm, mla_w_uq, mla_kv_norm, mla_w_ukv, ret_gn_gain, w_out, ln2_gain, w_up, w_down,
           final_gain):
    batch, seq, _ = x.shape
    depth = w_in.shape[0]
    t = batch * seq

    tab_n, tab_m, tab_r = _rope_tables(positions)

    w_in_p, w_in_t = _in_weight(w_in)
    pos_k, w1_k, w2_k = _compress_weights(cmp_pos_k, cmp_w1_k, cmp_w2_k)
    pos_v, w1_v, w2_v = _compress_weights(cmp_pos_v, cmp_w1_v, cmp_w2_v)
    cmp_pos = jnp.stack([pos_k, pos_v], axis=1)
    cmp_w1 = jnp.stack([w1_k, w1_v], axis=1)
    w2_vt = w2_v.transpose(0, 2, 1)
    wq, wk, wvt = _mla_weights(mla_w_uq, mla_w_ukv)
    wo = _out_weight(w_out)
    wu = w_up.astype(BF16)
    wd = w_down.astype(BF16)
    gn, intra, rd, wdec, cd = _retention_tables(ret_gn_gain)
    ovl_t = _selection_overlap(seq)
    gf = final_gain.reshape(1, D_MODEL)

    x2 = x.reshape(t, D_MODEL)
    for l in range(depth):
        nsa, nsa_t, gate_t, k_cmp, v_cmp, ret, rgate, q_m, k_m, vt_m = _in_proj(
            x2, ln1_gain[l].reshape(1, D_MODEL), w_in_p, w_in_t, mla_q_norm[l].reshape(1, -1),
            mla_kv_norm[l].reshape(1, -1), wq, wk, wvt, l, tab_n, tab_m, tab_r)
        kc, vct = _compress(k_cmp, v_cmp, cmp_pos[l], cmp_w1, w2_k[l], w2_vt[l], l, batch, seq)
        o_nsa = _nsa_attention(nsa, nsa_t, gate_t, kc, vct, ovl_t, batch, seq)
        o_mla = _mla_attention(q_m, k_m, vt_m, batch, seq)
        o_ret = _retention(ret, rgate, gn[l], intra, rd, wdec, cd, batch, seq)
        x2 = _out_mlp(x2, o_nsa, o_mla, o_ret, wo, ln2_gain[l].reshape(1, D_MODEL), wu, wd, l, gf,
                      final_norm=(l == depth - 1))
    return x2.reshape(batch, seq, D_MODEL)
```

```python
import functools
import math

import numpy as np
import jax
import jax.numpy as jnp
from jax import lax
from jax.experimental import pallas as pl
from jax.experimental.pallas import tpu as pltpu

F32 = jnp.float32
BF16 = jnp.bfloat16

D_MODEL = 1024
HEAD_DIM = 64
NSA_HEADS = 6
NSA_KV_GROUPS = 2
NSA_REP = NSA_HEADS // NSA_KV_GROUPS
N_BRANCH = 3
CMP_LEN = 32
CMP_STRIDE = 16
CMP_HIDDEN = 2 * HEAD_DIM
SEL_BLOCK = 64
SEL_TOP_N = 16
WINDOW = 512
MLA_HEADS = 5
MLA_Q_RANK = 256
MLA_KV_RANK = 128
MLA_NOPE_DIM = 64
MLA_ROPE_DIM = 32
MLA_V_DIM = 64
RET_HEADS = 5
RET_CHUNK = 128
ROPE_THETA = 500000.0
PARTIAL_ROPE_DIM = HEAD_DIM // 4
RET_THETA = 10000.0
D_FF = 4 * D_MODEL
NORM_EPS = 1e-6
NEG_INF = -1e30
FORCE_SCORE = 1e9
LOG2E = math.log2(math.e)

NSA_Q_W = NSA_HEADS * HEAD_DIM
NSA_KV_W = NSA_KV_GROUPS * HEAD_DIM
NSA_GATE_W = NSA_HEADS * N_BRANCH
RET_W = RET_HEADS * HEAD_DIM
IN_SIZES = (NSA_Q_W, NSA_KV_W, NSA_KV_W, NSA_KV_W, NSA_KV_W, NSA_KV_W, NSA_KV_W, NSA_GATE_W,
            MLA_Q_RANK, MLA_KV_RANK, MLA_ROPE_DIM, RET_W, RET_W, RET_W, RET_W)

LANES = 128
HALF = LANES // 2
SUBLANES = 8
VMEM_LIMIT = 56 * 1024 * 1024

NSA_TILES = 5
GATE_ROWS = 32
NSA_T_ROWS = 2 * LANES + GATE_ROWS
MLA_IN_TILES = 4
RET_PAIRS = 3
RET_SLOTS = (0, 1, 2, 3, None, 4)
RET_K_TILES = RET_PAIRS - 1
RET_ROPE_TILES = RET_PAIRS + RET_K_TILES
RET_MM_TILES = RET_ROPE_TILES + RET_PAIRS
RET_TILES = 3 * RET_PAIRS
IN_TILES = NSA_TILES + 2 + MLA_IN_TILES + RET_MM_TILES + RET_PAIRS
N_PAD = IN_TILES * LANES
MLA_V_ROWS = MLA_HEADS * MLA_V_DIM

IN_TM = 512
IN_CHUNK_TILES = 8
MLP_TM = 512
ATT_T = 512
MLP_FF_CHUNK = 1024


def _nn(a, b):
    return jnp.dot(a, b, preferred_element_type=F32)


def _nt(a, b):
    return lax.dot_general(a, b, (((1,), (1,)), ((), ())), preferred_element_type=F32)


def _rms(x, gain):
    return x * lax.rsqrt(jnp.mean(x * x, axis=-1, keepdims=True) + NORM_EPS) * gain


def _rope(val, tab, half):
    cos = tab[:, 0:LANES]
    sin_a = tab[:, LANES:2 * LANES]
    sin_b = tab[:, 2 * LANES:3 * LANES]
    return (val * cos + pltpu.roll(val, LANES - half, 1) * sin_a
            + pltpu.roll(val, half, 1) * sin_b)


def _lane_lo(shape):
    return lax.broadcasted_iota(jnp.int32, shape, len(shape) - 1) < HALF


def _mla_up(c, qn_ref, kvn_ref, wq_ref, wk_ref, wvt_ref, tm_ref, q_ref, k_ref, vt_ref):
    scale = (MLA_NOPE_DIM + MLA_ROPE_DIM) ** -0.5 * LOG2E
    cq = _rms(c[:, 0:MLA_Q_RANK], qn_ref[...]).astype(BF16)
    ckv = _rms(c[:, MLA_Q_RANK:MLA_Q_RANK + MLA_KV_RANK], kvn_ref[...]).astype(BF16)
    k_pe = _rope(c[:, 3 * LANES:4 * LANES], tm_ref[...], MLA_ROPE_DIM // 2)
    cos = tm_ref[:, 0:LANES] * scale
    sin = (tm_ref[:, 2 * LANES:3 * LANES] - tm_ref[:, LANES:2 * LANES]) * scale
    q = _nn(cq, wq_ref[...])
    k = _nn(ckv, wk_ref[...])
    w = MLA_HEADS * LANES
    for hd in range(MLA_HEADS):
        sl = slice(hd * LANES, (hd + 1) * LANES)
        rot = slice(w + hd * LANES, w + (hd + 1) * LANES)
        q_ref[:, sl] = (q[:, sl] * cos + q[:, rot] * sin).astype(BF16)
        k_ref[:, sl] = (k[:, sl] + k_pe).astype(BF16)
    vt_ref[...] = _nt(wvt_ref[...], ckv).astype(BF16)


def _inproj_kernel(x_ref, g_ref, w_ref, wt_ref, tn_ref, tm_ref, tr_ref, qn_ref, kvn_ref, wq_ref, wk_ref,
                   wvt_ref, nsa_ref, nsat_ref, gatet_ref, kcmp_ref, vcmp_ref, ret_ref, rgate_ref,
                   mq_ref, mk_ref, mvt_ref):
    h = _rms(x_ref[...], g_ref[...]).astype(BF16)
    tab_n = tn_ref[...]
    tab_r = tr_ref[...]

    chunks = [_nn(h, w_ref[:, c:min(c + IN_CHUNK_TILES * LANES, N_PAD)])
              for c in range(0, N_PAD, IN_CHUNK_TILES * LANES)]

    def tiles(first, n):
        cols = [chunks[i // IN_CHUNK_TILES][:, (i % IN_CHUNK_TILES) * LANES:(i % IN_CHUNK_TILES + 1) * LANES]
                for i in range(first, first + n)]
        return cols[0] if n == 1 else jnp.concatenate(cols, axis=1)

    t0 = 0
    for i in range(NSA_TILES):
        v = _rope(tiles(t0 + i, 1), tab_n, PARTIAL_ROPE_DIM // 2)
        if i < NSA_REP:
            v = v * LOG2E
        nsa_ref[:, i * LANES:(i + 1) * LANES] = v.astype(BF16)
    t0 += NSA_TILES
    kcmp_ref[...] = _rope(tiles(t0, 1), tab_n, PARTIAL_ROPE_DIM // 2)
    vcmp_ref[...] = tiles(t0 + 1, 1)
    t0 += 2
    _mla_up(tiles(t0, MLA_IN_TILES), qn_ref, kvn_ref, wq_ref, wk_ref, wvt_ref, tm_ref, mq_ref, mk_ref, mvt_ref)
    t0 += MLA_IN_TILES
    stored = []
    for i in range(RET_MM_TILES):
        v = tiles(t0 + i, 1)
        if i < RET_ROPE_TILES:
            v = _rope(v, tab_r, HEAD_DIM // 2)
        stored.append(v)
        if i == RET_ROPE_TILES - 1:
            stored.append(pltpu.roll(stored[RET_PAIRS - 1], HALF, 1))
    for i, v in enumerate(stored):
        ret_ref[:, i * LANES:(i + 1) * LANES] = v.astype(BF16)
    t0 += RET_MM_TILES
    rgate_ref[...] = tiles(t0, RET_PAIRS)
    at = _nt(wt_ref[...], h)
    nsat_ref[...] = at[0:2 * LANES, :].astype(BF16)
    gatet_ref[...] = at[2 * LANES:NSA_T_ROWS, :]


def _layer_spec(w, layer, **kwargs):
    zeros = (0,) * (w.ndim - 1)
    return pl.BlockSpec((None,) + w.shape[1:], lambda *_: (layer,) + zeros, **kwargs)


def _in_proj(x2, gain, w, wt, q_norm, kv_norm, wq, wk, wvt, layer, tab_n, tab_m, tab_r):
    t = x2.shape[0]
    row = lambda i: (i, 0)
    col = lambda i: (0, i)
    const = lambda i: (0, 0)
    out_shapes = (
        jax.ShapeDtypeStruct((t, NSA_TILES * LANES), BF16),
        jax.ShapeDtypeStruct((2 * LANES, t), BF16),
        jax.ShapeDtypeStruct((GATE_ROWS, t), F32),
        jax.ShapeDtypeStruct((t, LANES), F32),
        jax.ShapeDtypeStruct((t, LANES), F32),
        jax.ShapeDtypeStruct((t, RET_TILES * LANES), BF16),
        jax.ShapeDtypeStruct((t, RET_PAIRS * LANES), F32),
        jax.ShapeDtypeStruct((t, MLA_HEADS * LANES), BF16),
        jax.ShapeDtypeStruct((t, MLA_HEADS * LANES), BF16),
        jax.ShapeDtypeStruct((MLA_V_ROWS, t), BF16),
    )
    out_specs = tuple(
        pl.BlockSpec((s.shape[0], IN_TM), col) if s.shape[1] == t else pl.BlockSpec((IN_TM, s.shape[1]), row)
        for s in out_shapes)
    return pl.pallas_call(
        _inproj_kernel,
        grid=(t // IN_TM,),
        in_specs=[
            pl.BlockSpec((IN_TM, D_MODEL), row),
            pl.BlockSpec((1, D_MODEL), const),
            _layer_spec(w, layer),
            _layer_spec(wt, layer),
            pl.BlockSpec((IN_TM, 3 * LANES), row),
            pl.BlockSpec((IN_TM, 3 * LANES), row),
            pl.BlockSpec((IN_TM, 3 * LANES), row),
            pl.BlockSpec((1, MLA_Q_RANK), const),
            pl.BlockSpec((1, MLA_KV_RANK), const),
            _layer_spec(wq, layer),
            _layer_spec(wk, layer),
            _layer_spec(wvt, layer),
        ],
        out_specs=out_specs,
        out_shape=out_shapes,
        compiler_params=pltpu.CompilerParams(
            dimension_semantics=("parallel",), vmem_limit_bytes=VMEM_LIMIT),
        name="in_proj",
    )(x2, gain, w, wt, tab_n, tab_m, tab_r, q_norm, kv_norm, wq, wk, wvt)


def _compress_kernel(k_ref, v_ref, pos_ref, w1_ref, w2k_ref, w2vt_ref, kc_ref, vct_ref):
    n_blk = k_ref.shape[0] // CMP_STRIDE

    def hidden(src, i):
        toks = [src[pl.ds(r, n_blk, stride=CMP_STRIDE), :] for r in range(CMP_STRIDE)]
        halves = []
        for half in range(CMP_LEN // CMP_STRIDE):
            off = half * CMP_STRIDE
            x = jnp.concatenate([(toks[r] + pos_ref[i, off + r:off + r + 1, :]).astype(BF16)
                                 for r in range(CMP_STRIDE)], axis=1)
            w = w1_ref[i, off:off + CMP_STRIDE].reshape(CMP_STRIDE * LANES, NSA_KV_GROUPS * CMP_HIDDEN)
            halves.append(_nn(x, w))
        return jax.nn.gelu(halves[0] + pltpu.roll(halves[1], n_blk - 1, 0)).astype(BF16)

    kc_ref[0] = _nn(hidden(k_ref, 0), w2k_ref[...]).astype(BF16)
    vct_ref[0] = _nt(w2vt_ref[...], hidden(v_ref, 1)).astype(BF16)


def _compress(k_cmp, v_cmp, pos, w1, w2k, w2vt, layer, batch, seq):
    b = batch
    n_blk = seq // CMP_STRIDE
    return pl.pallas_call(
        _compress_kernel,
        grid=(b,),
        in_specs=[
            pl.BlockSpec((seq, LANES), lambda i: (i, 0)),
            pl.BlockSpec((seq, LANES), lambda i: (i, 0)),
            pl.BlockSpec(pos.shape, lambda i: (0, 0, 0)),
            _layer_spec(w1, layer),
            pl.BlockSpec(w2k.shape, lambda i: (0, 0)),
            pl.BlockSpec(w2vt.shape, lambda i: (0, 0)),
        ],
        out_specs=(pl.BlockSpec((1, n_blk, LANES), lambda i: (i, 0, 0)),
                   pl.BlockSpec((1, LANES, n_blk), lambda i: (i, 0, 0))),
        out_shape=(jax.ShapeDtypeStruct((b, n_blk, LANES), BF16),
                   jax.ShapeDtypeStruct((b, LANES, n_blk), BF16)),
        compiler_params=pltpu.CompilerParams(
            dimension_semantics=("parallel",), vmem_limit_bytes=VMEM_LIMIT),
        name="nsa_compress",
    )(k_cmp, v_cmp, pos, w1, w2k, w2vt)


SCORE_LOOKAHEAD = 2
ONES_ROWS = 16


def _tri_scores(k, q, dead_upper, mask, bias=None):
    half = k.shape[0] // 2
    dead = jnp.full((half, half), NEG_INF, F32)
    if dead_upper:
        top = _nt(k[:half], q)
        bottom = jnp.concatenate([dead, _nt(k[half:], q[half:])], axis=1)
    else:
        top = jnp.concatenate([_nt(k[:half], q[:half]), dead], axis=1)
        bottom = _nt(k[half:], q)
    s_t = jnp.concatenate([top, bottom], axis=0)
    return jnp.where(mask, s_t if bias is None else s_t + bias, NEG_INF)


def _tri_pv(dead_upper, v_ext, p):
    half = p.shape[0] // 2
    if dead_upper:
        full = _nn(v_ext[:, :half], p[:half])
        part = _nn(v_ext[:, half:], p[half:, half:])
        return jnp.concatenate([full[:, :half], full[:, half:] + part], axis=1)
    part = _nn(v_ext[:, :half], p[:half, :half])
    full = _nn(v_ext[:, half:], p[half:])
    return jnp.concatenate([full[:, :half] + part, full[:, half:]], axis=1)


def _softmax_steps(score_fns, v_ts, states, pv_fn=_nn):
    n = len(score_fns)
    s_ts = [score_fns[h]() if h < SCORE_LOOKAHEAD else None for h in range(n)]
    out = []
    for h in range(n):
        if h + SCORE_LOOKAHEAD < n:
            s_ts[h + SCORE_LOOKAHEAD] = score_fns[h + SCORE_LOOKAHEAD]()
        m_old, acc_old = states[h]
        m = jnp.maximum(m_old, jnp.max(s_ts[h], axis=0, keepdims=True))
        p = jnp.exp2(s_ts[h] - m).astype(BF16)
        s_ts[h] = None
        v_ext = jnp.concatenate([v_ts[h], jnp.ones((ONES_ROWS, v_ts[h].shape[1]), BF16)], axis=0)
        out.append((m, jnp.exp2(m_old - m) * acc_old + pv_fn(v_ext, p)))
    return tuple(out)


def _softmax_init(cols):
    return jnp.full((1, cols), NEG_INF, F32), jnp.zeros((HEAD_DIM + ONES_ROWS, cols), F32)


def _softmax_finish(state):
    acc = state[1]
    return acc[0:HEAD_DIM] * (1.0 / jnp.maximum(acc[HEAD_DIM:HEAD_DIM + 1], 1e-30))


def _nsa_kernel(q_ref, gatet_ref, kc_ref, vct_ref, ks_ref, kw_ref, vst_ref, vwt_ref, ovl_ref,
                o_ref, out_t_ref, selb_ref):
    tq = ATT_T
    qi = pl.program_id(1)
    q0 = pl.multiple_of(qi * tq, tq)
    lo1 = _lane_lo((1, LANES))
    group_lanes = (lo1, jnp.logical_not(lo1))
    gate = jax.nn.sigmoid(gatet_ref[...])
    heads = range(NSA_HEADS)

    def q_head(h):
        j, g = h % NSA_REP, h // NSA_REP
        tile = q_ref[:, j * LANES:(j + 1) * LANES]
        return jnp.where(group_lanes[g], tile, jnp.zeros_like(tile))

    def v_rows(ref, h, k0, n):
        g = h // NSA_REP
        return ref[g * HEAD_DIM:(g + 1) * HEAD_DIM, pl.ds(k0, n)]

    def emit(h, branch, qs, nq, o_t):
        j, g = h % NSA_REP, h // NSA_REP
        r0 = j * LANES + g * HEAD_DIM
        row = h * N_BRANCH + branch
        val = gate[row:row + 1, qs:qs + nq] * o_t
        if branch == 0:
            out_t_ref[r0:r0 + HEAD_DIM, qs:qs + nq] = val
        else:
            out_t_ref[r0:r0 + HEAD_DIM, qs:qs + nq] += val

    qm = [q_head(h) for h in heads]

    n_cmp_pad = kc_ref.shape[1]
    kc = kc_ref[0]
    vct = vct_ref[0]
    n_i = lax.broadcasted_iota(jnp.int32, (n_cmp_pad, tq), 0)
    t_l = q0 + lax.broadcasted_iota(jnp.int32, (n_cmp_pad, tq), 1)
    cmask = (n_i * CMP_STRIDE + (CMP_LEN - 1)) <= t_l
    cmask_f = cmask.astype(F32)
    s_cs = [jnp.where(cmask, _nt(kc, qm[h]), NEG_INF) for h in heads]
    p_cs = [jnp.exp2(s_t - jnp.max(s_t, axis=0, keepdims=True)) * cmask_f for s_t in s_cs]
    p_cs = [p * (1.0 / jnp.maximum(jnp.sum(p, axis=0, keepdims=True), 1e-30)) for p in p_cs]
    for h in heads:
        g = h // NSA_REP
        emit(h, 0, 0, tq, _nn(vct[g * HEAD_DIM:(g + 1) * HEAD_DIM, :], p_cs[h].astype(BF16)))

    n_sel = ovl_ref.shape[0]
    needs_rank = q0 + tq > SEL_TOP_N * SEL_BLOCK

    @pl.when(jnp.logical_not(needs_rank))
    def _():
        selb_ref[...] = jnp.zeros_like(selb_ref)

    @pl.when(needs_rank)
    def _():
        m_i = lax.broadcasted_iota(jnp.int32, (n_sel, tq), 0)
        cur = jnp.right_shift(q0 + lax.broadcasted_iota(jnp.int32, (n_sel, tq), 1),
                              SEL_BLOCK.bit_length() - 1)
        valid = m_i <= cur
        forced = (m_i == 0) | (m_i == cur) | (m_i == cur - 1)
        ovl = ovl_ref[...]
        sub = SUBLANES
        m_loc = lax.broadcasted_iota(jnp.int32, (sub, tq), 0)
        for g in range(NSA_KV_GROUPS):
            psum = p_cs[g * NSA_REP]
            for h in range(g * NSA_REP + 1, (g + 1) * NSA_REP):
                psum = psum + p_cs[h]
            p_hi = psum.astype(BF16)
            p_lo = (psum - p_hi.astype(F32)).astype(BF16)
            imp = _nn(ovl, p_hi) + _nn(ovl, p_lo)
            imp = jnp.where(valid & forced, FORCE_SCORE, imp)
            imp = jnp.where(valid, imp, NEG_INF)
            parts = [imp[i:i + sub] for i in range(0, n_sel, sub)]
            ranks = [jnp.zeros((sub, tq), jnp.int32) for _ in parts]
            for mp in range(n_sel):
                row = imp[mp:mp + 1, :]
                for i, part in enumerate(parts):
                    if i * sub + sub - 1 <= mp:
                        beats = row > part
                    elif i * sub > mp:
                        beats = row >= part
                    else:
                        beats = (row > part) | ((row == part) & (m_loc + i * sub > mp))
                    ranks[i] = ranks[i] + beats.astype(jnp.int32)
            rank = jnp.concatenate(ranks, axis=0)
            selb_ref[g] = jnp.where(rank < SEL_TOP_N, 0.0, NEG_INF)

    blocks_per_chunk = tq // SEL_BLOCK

    def sel_bias(g, c, n_keys, qs, nq):
        rows = [jnp.broadcast_to(selb_ref[g, pl.ds(c * blocks_per_chunk + i, 1), qs:qs + nq], (SEL_BLOCK, nq))
                for i in range(n_keys // SEL_BLOCK)]
        return jnp.concatenate(rows, axis=0)

    key_i = lax.broadcasted_iota(jnp.int32, (tq, tq), 0)
    qry_i = lax.broadcasted_iota(jnp.int32, (tq, tq), 1)
    causal_mask = key_i <= qry_i

    def sel_chunk(c, k0, states, causal):
        k = ks_ref[pl.ds(k0, tq), :]
        bias = [sel_bias(g, c, tq, 0, tq) for g in range(NSA_KV_GROUPS)]

        def score_fn(h):
            if causal:
                return _tri_scores(k, qm[h], True, causal_mask, bias[h // NSA_REP])
            return _nt(k, qm[h]) + bias[h // NSA_REP]

        return _softmax_steps([functools.partial(score_fn, h) for h in heads],
                              [v_rows(vst_ref, h, k0, tq) for h in heads], states,
                              functools.partial(_tri_pv, True) if causal else _nn)

    states = sel_chunk(qi, q0, tuple(_softmax_init(tq) for _ in heads), True)
    states = lax.fori_loop(
        0, qi, lambda c, st: sel_chunk(c, pl.multiple_of(c * tq, tq), st, False), states)
    for h in heads:
        emit(h, 1, 0, tq, _softmax_finish(states[h]))

    def win_chunk(k0, states, mask, dead_upper):
        k = kw_ref[pl.ds(k0, tq), :]
        if dead_upper is None:
            score_fns = [lambda h=h: jnp.where(mask, _nt(k, qm[h]), NEG_INF) for h in heads]
            pv_fn = _nn
        else:
            score_fns = [lambda h=h: _tri_scores(k, qm[h], dead_upper, mask) for h in heads]
            pv_fn = functools.partial(_tri_pv, dead_upper)
        return _softmax_steps(score_fns, [v_rows(vwt_ref, h, k0, tq) for h in heads], states, pv_fn)

    states = win_chunk(q0, tuple(_softmax_init(tq) for _ in heads), causal_mask, True)
    for d in range(1, WINDOW // tq + 1):
        k0 = pl.multiple_of(jnp.maximum(qi - d, 0) * tq, tq)
        in_band = (key_i - d * tq > qry_i - WINDOW) & (qi >= d)
        states = win_chunk(k0, states, in_band, False if d * tq == WINDOW else None)
    for h in heads:
        emit(h, 2, 0, tq, _softmax_finish(states[h]))

    for j in range(NSA_REP):
        o_ref[:, j * LANES:(j + 1) * LANES] = out_t_ref[j * LANES:(j + 1) * LANES, :].T.astype(BF16)


def _nsa_attention(nsa, nsa_t, gate_t, kc, vct, ovl_t, batch, seq):
    tq = ATT_T
    nq = seq // tq
    n_blk = kc.shape[1]
    qrow = lambda b, i: (b * nq + i, 0)
    return pl.pallas_call(
        _nsa_kernel,
        grid=(batch, nq),
        in_specs=[
            pl.BlockSpec((tq, NSA_REP * LANES), qrow),
            pl.BlockSpec((GATE_ROWS, tq), lambda b, i: (0, b * nq + i)),
            pl.BlockSpec((1, n_blk, LANES), lambda b, i: (b, 0, 0)),
            pl.BlockSpec((1, LANES, n_blk), lambda b, i: (b, 0, 0)),
            pl.BlockSpec((seq, LANES), lambda b, i: (b, 3)),
            pl.BlockSpec((seq, LANES), lambda b, i: (b, 4)),
            pl.BlockSpec((LANES, seq), lambda b, i: (0, b)),
            pl.BlockSpec((LANES, seq), lambda b, i: (1, b)),
            pl.BlockSpec(ovl_t.shape, lambda b, i: (0, 0)),
        ],
        out_specs=pl.BlockSpec((tq, NSA_REP * LANES), qrow),
        out_shape=jax.ShapeDtypeStruct((batch * seq, NSA_REP * LANES), BF16),
        scratch_shapes=[pltpu.VMEM((NSA_REP * LANES, tq), F32),
                        pltpu.VMEM((NSA_KV_GROUPS, seq // SEL_BLOCK, tq), F32)],
        compiler_params=pltpu.CompilerParams(
            dimension_semantics=("parallel", "arbitrary"), vmem_limit_bytes=VMEM_LIMIT),
        name="nsa_attention",
    )(nsa, gate_t, kc, vct, nsa, nsa, nsa_t, nsa_t, ovl_t)


def _mla_attn_kernel(q_ref, k_ref, vt_ref, o_ref, out_t_ref):
    tq = ATT_T
    qi = pl.program_id(1)
    q0 = pl.multiple_of(qi * tq, tq)
    heads = range(MLA_HEADS)
    qh = [q_ref[:, h * LANES:(h + 1) * LANES] for h in heads]

    def chunk(k0, states, causal):
        if causal:
            mask = (lax.broadcasted_iota(jnp.int32, (tq, tq), 0)
                    <= lax.broadcasted_iota(jnp.int32, (tq, tq), 1))

        def score_fn(h):
            k = k_ref[pl.ds(k0, tq), h * LANES:(h + 1) * LANES]
            return _tri_scores(k, qh[h], True, mask) if causal else _nt(k, qh[h])

        v_ts = [vt_ref[h * MLA_V_DIM:(h + 1) * MLA_V_DIM, pl.ds(k0, tq)] for h in heads]
        return _softmax_steps([functools.partial(score_fn, h) for h in heads], v_ts, states,
                              functools.partial(_tri_pv, True) if causal else _nn)

    states = chunk(q0, tuple(_softmax_init(tq) for _ in heads), True)
    states = lax.fori_loop(0, qi, lambda c, st: chunk(pl.multiple_of(c * tq, tq), st, False), states)
    for h in heads:
        out_t_ref[h * MLA_V_DIM:(h + 1) * MLA_V_DIM, :] = _softmax_finish(states[h])
    out_t_ref[MLA_V_ROWS:, :] = jnp.zeros((RET_PAIRS * LANES - MLA_V_ROWS, tq), F32)
    for j in range(RET_PAIRS):
        o_ref[:, j * LANES:(j + 1) * LANES] = out_t_ref[j * LANES:(j + 1) * LANES, :].T.astype(BF16)


def _mla_attention(q, k, vt, batch, seq):
    tq = ATT_T
    nq = seq // tq
    return pl.pallas_call(
        _mla_attn_kernel,
        grid=(batch, nq),
        in_specs=[
            pl.BlockSpec((tq, MLA_HEADS * LANES), lambda b, i: (b * nq + i, 0)),
            pl.BlockSpec((seq, MLA_HEADS * LANES), lambda b, i: (b, 0)),
            pl.BlockSpec((MLA_V_ROWS, seq), lambda b, i: (0, b)),
        ],
        out_specs=pl.BlockSpec((tq, RET_PAIRS * LANES), lambda b, i: (b * nq + i, 0)),
        out_shape=jax.ShapeDtypeStruct((batch * seq, RET_PAIRS * LANES), BF16),
        scratch_shapes=[pltpu.VMEM((RET_PAIRS * LANES, tq), F32)],
        compiler_params=pltpu.CompilerParams(
            dimension_semantics=("parallel", "arbitrary"), vmem_limit_bytes=VMEM_LIMIT),
        name="mla_attention",
    )(q, k, vt)


def _tn(a, b):
    return lax.dot_general(a, b, (((0,), (0,)), ((), ())), preferred_element_type=F32)


RET_UNROLL = 16


def _retention_kernel(q_ref, k_ref, v_ref, gate_ref, gn_ref, intra_ref, rd_ref, wd_ref, cd_ref, o_ref,
                      kv_ref, st_ref):
    c_len = RET_CHUNK
    n_chunks = q_ref.shape[0] // c_len
    lo = _lane_lo((1, LANES))
    hi = jnp.logical_not(lo)
    blockdiag = (lax.broadcasted_iota(jnp.int32, (LANES, LANES), 0) < HALF) == _lane_lo((LANES, LANES))
    intra_a = intra_ref[0, 0]
    intra_b = intra_ref[0, 1]
    read_decay = rd_ref[0]
    write_decay = wd_ref[0]
    chunk_decay = cd_ref[0]
    gn = gn_ref[0]

    averager = jnp.where(blockdiag, 1.0 / HEAD_DIM, 0.0).astype(BF16)

    def half_mean(x):
        x_hi = x.astype(BF16)
        x_lo = (x - x_hi.astype(F32)).astype(BF16)
        return _nn(x_hi, averager) + _nn(x_lo, averager)

    def kv_body(c, carry):
        r0 = pl.multiple_of(c * c_len, c_len)
        kc = k_ref[pl.ds(r0, c_len), :]
        kv_ref[c] = _tn((kc.astype(F32) * write_decay).astype(BF16), v_ref[pl.ds(r0, c_len), :])
        return carry

    lax.fori_loop(0, n_chunks, kv_body, 0, unroll=RET_UNROLL)

    state = jnp.zeros((LANES, LANES), F32)
    for c in range(n_chunks):
        st_ref[c] = jnp.where(blockdiag, state, 0.0).astype(BF16)
        state = state * chunk_decay + kv_ref[c]

    def out_body(step, carry):
        cs = [step * RET_UNROLL + u for u in range(RET_UNROLL)]
        rows = [pl.ds(pl.multiple_of(c * c_len, c_len), c_len) for c in cs]
        qs = [q_ref[r, :] for r in rows]
        ks = [k_ref[r, :] for r in rows]
        vs = [v_ref[r, :] for r in rows]
        zero = jnp.zeros_like(qs[0])
        sa = [(_nt(jnp.where(lo, q, zero), k) * intra_a).astype(BF16) for q, k in zip(qs, ks)]
        sb = [(_nt(jnp.where(hi, q, zero), k) * intra_b).astype(BF16) for q, k in zip(qs, ks)]
        cross = [_nn(q, st_ref[c]) * read_decay for q, c in zip(qs, cs)]
        os_ = [jnp.where(lo, _nn(a, v), _nn(b, v)) + x for a, b, v, x in zip(sa, sb, vs, cross)]
        ds_ = [o - half_mean(o) for o in os_]
        ys = [d * lax.rsqrt(half_mean(d * d) + NORM_EPS) * gn for d in ds_]
        for r, y in zip(rows, ys):
            o_ref[r, :] = (jax.nn.silu(gate_ref[r, :]) * y).astype(BF16)
        return carry

    lax.fori_loop(0, n_chunks // RET_UNROLL, out_body, 0)


def _retention(ret, rgate, gn, intra, rd, wd, cd, batch, seq):
    pair_const3 = lambda b, j: (j, 0, 0)
    return pl.pallas_call(
        _retention_kernel,
        grid=(batch, RET_PAIRS),
        in_specs=[
            pl.BlockSpec((seq, LANES), lambda b, j: (b, j)),
            pl.BlockSpec((seq, LANES), lambda b, j: (b, RET_PAIRS + j)),
            pl.BlockSpec((seq, LANES), lambda b, j: (b, 2 * RET_PAIRS + j)),
            pl.BlockSpec((seq, LANES), lambda b, j: (b, j)),
            pl.BlockSpec((1, 1, LANES), pair_const3),
            pl.BlockSpec((1, 2, RET_CHUNK, RET_CHUNK), lambda b, j: (j, 0, 0, 0)),
            pl.BlockSpec((1, RET_CHUNK, LANES), pair_const3),
            pl.BlockSpec((1, RET_CHUNK, LANES), pair_const3),
            pl.BlockSpec((1, 1, LANES), pair_const3),
        ],
        out_specs=pl.BlockSpec((seq, LANES), lambda b, j: (b, j)),
        out_shape=jax.ShapeDtypeStruct((batch * seq, RET_PAIRS * LANES), BF16),
        scratch_shapes=[pltpu.VMEM((seq // RET_CHUNK, LANES, LANES), F32),
                        pltpu.VMEM((seq // RET_CHUNK, LANES, LANES), BF16)],
        compiler_params=pltpu.CompilerParams(
            dimension_semantics=("parallel", "arbitrary"), vmem_limit_bytes=VMEM_LIMIT),
        name="retention",
    )(ret, ret, ret, rgate, gn, intra, rd, wd, cd)


def _out_mlp_kernel(x_ref, nsa_ref, mla_ref, ret_ref, wo_ref, g2_ref, wu_ref, wd_ref, gf_ref, o_ref,
                    *, final_norm):
    full = 2 * LANES
    mixed = jnp.concatenate([nsa_ref[...], mla_ref[:, :full], mla_ref[:, full:] + ret_ref[:, full:],
                             ret_ref[:, :full]], axis=1)
    x = x_ref[...] + _nn(mixed, wo_ref[...])
    h = _rms(x, g2_ref[...]).astype(BF16)
    y = x
    for c in range(D_FF // MLP_FF_CHUNK):
        sl = slice(c * MLP_FF_CHUNK, (c + 1) * MLP_FF_CHUNK)
        u = jnp.maximum(_nn(h, wu_ref[:, sl]), 0.0)
        y = y + _nn((u * u).astype(BF16), wd_ref[sl, :])
    if final_norm:
        y = _rms(y, gf_ref[...])
    o_ref[...] = y


def _out_mlp(x2, o_nsa, o_mla, o_ret, wo, g2, wu, wd, layer, gf, final_norm):
    t = x2.shape[0]
    w = RET_PAIRS * LANES
    row = lambda i: (i, 0)
    const = lambda i: (0, 0)
    resident = dict(pipeline_mode=pl.Buffered(1))
    return pl.pallas_call(
        functools.partial(_out_mlp_kernel, final_norm=final_norm),
        grid=(t // MLP_TM,),
        in_specs=[
            pl.BlockSpec((MLP_TM, D_MODEL), row),
            pl.BlockSpec((MLP_TM, w), row),
            pl.BlockSpec((MLP_TM, w), row),
            pl.BlockSpec((MLP_TM, w), row),
            _layer_spec(wo, layer, **resident),
            pl.BlockSpec((1, D_MODEL), const),
            _layer_spec(wu, layer, **resident),
            _layer_spec(wd, layer, **resident),
            pl.BlockSpec((1, D_MODEL), const),
        ],
        out_specs=pl.BlockSpec((MLP_TM, D_MODEL), row),
        out_shape=jax.ShapeDtypeStruct((t, D_MODEL), F32),
        compiler_params=pltpu.CompilerParams(
            dimension_semantics=("parallel",), vmem_limit_bytes=VMEM_LIMIT),
        name="out_mlp",
    )(x2, o_nsa, o_mla, o_ret, wo, g2, wu, wd, gf)


ROPE_KINDS = (
    (PARTIAL_ROPE_DIM, ROPE_THETA, HEAD_DIM, 0),
    (MLA_ROPE_DIM, ROPE_THETA, LANES, HALF),
    (HEAD_DIM, RET_THETA, HEAD_DIM, 0),
)
ROPE_TM = 1024


def _rope_placement():
    n_angles = sum(dim // 2 for dim, _, _, _ in ROPE_KINDS)
    assert 2 * n_angles <= LANES
    place = np.zeros((LANES, 3 * LANES * len(ROPE_KINDS)), np.float32)
    fill = np.zeros((1, 3 * LANES * len(ROPE_KINDS)), np.float32)
    row0 = 0
    for kind, (dim, _, period, base) in enumerate(ROPE_KINDS):
        half = dim // 2
        col0 = kind * 3 * LANES
        for lane in range(LANES):
            rel = (lane - base) % period
            first = lane >= base and rel < half
            second = lane >= base and half <= rel < dim
            if first or second:
                angle = rel if first else rel - half
                place[row0 + angle, col0 + lane] = 1.0
                place[n_angles + row0 + angle, col0 + (1 if first else 2) * LANES + lane] = -1.0 if first else 1.0
            else:
                fill[0, col0 + lane] = 1.0
        row0 += half
    return jnp.asarray(place, BF16), jnp.asarray(fill)


def _rope_kernel(cs_ref, place_ref, fill_ref, *out_refs):
    x = cs_ref[...]
    x1 = x.astype(BF16)
    r1 = x - x1.astype(F32)
    x2 = r1.astype(BF16)
    x3 = (r1 - x2.astype(F32)).astype(BF16)
    place = place_ref[...]
    tab = _tn(x1, place) + _tn(x2, place) + _tn(x3, place) + fill_ref[...]
    for i, ref in enumerate(out_refs):
        ref[...] = tab[:, i * 3 * LANES:(i + 1) * 3 * LANES]


def _rope_tables(positions):
    inv = jnp.concatenate([1.0 / (theta ** (jnp.arange(0, dim, 2, dtype=F32) / dim))
                           for dim, theta, _, _ in ROPE_KINDS])
    ang = inv[:, None] * positions.reshape(-1).astype(F32)[None, :]
    compact = jnp.concatenate([jnp.cos(ang), jnp.sin(ang)], axis=0)
    compact = jnp.pad(compact, ((0, LANES - compact.shape[0]), (0, 0)))
    place, fill = _rope_placement()
    t = compact.shape[1]
    tab_shape = jax.ShapeDtypeStruct((t, 3 * LANES), F32)
    return pl.pallas_call(
        _rope_kernel,
        grid=(t // ROPE_TM,),
        in_specs=[pl.BlockSpec((LANES, ROPE_TM), lambda i: (0, i)),
                  pl.BlockSpec(place.shape, lambda i: (0, 0)),
                  pl.BlockSpec(fill.shape, lambda i: (0, 0))],
        out_specs=tuple(pl.BlockSpec((ROPE_TM, 3 * LANES), lambda i: (i, 0)) for _ in ROPE_KINDS),
        out_shape=tuple(tab_shape for _ in ROPE_KINDS),
        compiler_params=pltpu.CompilerParams(
            dimension_semantics=("parallel",), vmem_limit_bytes=VMEM_LIMIT),
        name="rope_tables",
    )(compact, place, fill)


def _pad_cols(w, n):
    return jnp.pad(w, ((0, 0), (0, 0), (0, n - w.shape[-1])))


def _in_weight(w_in):
    offs = np.cumsum((0,) + IN_SIZES)
    (nsa_q, k_cmp, v_cmp, k_slc, v_slc, k_win, v_win, gate,
     cq, ckv, kpe, ret_q, ret_k, ret_v, ret_g) = [np.arange(offs[i], offs[i + 1]) for i in range(len(IN_SIZES))]
    scale = HEAD_DIM ** -0.5
    head = lambda cols, h: cols[h * HEAD_DIM:(h + 1) * HEAD_DIM]
    pad = lambda n: None if n == 0 else -np.ones(n, np.int64)
    pieces = []
    for j in range(NSA_REP):
        pieces += [(head(nsa_q, j), scale), (head(nsa_q, j + NSA_REP), scale)]
    pieces += [(c, 1.0) for c in (k_slc, k_win, k_cmp, v_cmp, cq, ckv)]
    pieces += [(pad(HALF), 0.0), (kpe, 1.0), (pad(LANES - HALF - MLA_ROPE_DIM), 0.0)]
    slots = lambda cols, f, fill: [fill if h is None else (head(cols, h), f) for h in RET_SLOTS]
    zero = (pad(HEAD_DIM), 0.0)
    pieces += slots(ret_q, 1.0, (head(ret_k, RET_SLOTS[-1]), scale))
    pieces += slots(ret_k, scale, zero)[:2 * RET_K_TILES]
    pieces += slots(ret_v, 1.0, zero) + slots(ret_g, 1.0, zero)
    t_pieces = [(v_slc, 1.0), (v_win, 1.0), (gate, 1.0), (pad(GATE_ROWS - NSA_GATE_W), 0.0)]

    def gather(parts):
        src = np.concatenate([c for c, _ in parts])
        factor = np.concatenate([np.full(len(c), f if (c >= 0).all() else 0.0, np.float32) for c, f in parts])
        return (jnp.take(w_in, jnp.asarray(np.maximum(src, 0), jnp.int32), axis=2) * factor).astype(BF16)

    w_main = gather(pieces)
    assert w_main.shape[-1] == N_PAD
    return w_main, gather(t_pieces).transpose(0, 2, 1)


def _compress_weights(pos, w1, w2):
    nl = pos.shape[0]
    g, dh, hid = NSA_KV_GROUPS, HEAD_DIM, CMP_HIDDEN
    assert g == 2
    p = jnp.tile(pos, (1, 1, g))

    def block_diag2(w):
        lead = ((0, 0),) * (w.ndim - 1)
        return jnp.concatenate([jnp.pad(w, lead + ((0, w.shape[-1]),)),
                                jnp.pad(w, lead + ((w.shape[-1], 0),))], axis=-2)

    w = block_diag2(w1.astype(BF16).reshape(nl, CMP_LEN, dh, hid))
    return p, w, block_diag2(w2.astype(BF16))


def _mla_weights(w_uq, w_ukv):
    nl = w_uq.shape[0]
    dq = MLA_NOPE_DIM + MLA_ROPE_DIM
    wq = w_uq.reshape(nl, MLA_Q_RANK, MLA_HEADS, dq)
    pe = wq[..., MLA_NOPE_DIM:]
    half = MLA_ROPE_DIM // 2
    rot = jnp.concatenate([jnp.zeros_like(wq[..., :MLA_NOPE_DIM]), -pe[..., half:], pe[..., :half]], axis=-1)
    pad = lambda w: jnp.pad(w, ((0, 0), (0, 0), (0, 0), (0, LANES - dq))).reshape(
        nl, MLA_Q_RANK, MLA_HEADS * LANES)
    wq = jnp.concatenate([pad(wq), pad(rot)], axis=-1)
    wkv = w_ukv.reshape(nl, MLA_KV_RANK, MLA_HEADS, MLA_NOPE_DIM + MLA_V_DIM)
    wk = jnp.pad(wkv[..., :MLA_NOPE_DIM], ((0, 0), (0, 0), (0, 0), (0, LANES - MLA_NOPE_DIM)))
    wk = wk.reshape(nl, MLA_KV_RANK, MLA_HEADS * LANES)
    wvt = wkv[..., MLA_NOPE_DIM:].reshape(nl, MLA_KV_RANK, MLA_V_ROWS).transpose(0, 2, 1)
    return wq.astype(BF16), wk.astype(BF16), wvt.astype(BF16)


def _out_weight(w_out):
    nl = w_out.shape[0]
    nsa = w_out[:, :NSA_Q_W].reshape(nl, NSA_HEADS, HEAD_DIM, D_MODEL)
    order = [h for j in range(NSA_REP) for h in (j, j + NSA_REP)]
    nsa = nsa[:, order].reshape(nl, NSA_Q_W, D_MODEL)
    mla = w_out[:, NSA_Q_W:NSA_Q_W + MLA_HEADS * MLA_V_DIM]
    ret = w_out[:, NSA_Q_W + MLA_HEADS * MLA_V_DIM:]
    full = 2 * LANES
    return jnp.concatenate([nsa, mla[:, :full], mla[:, full:], ret[:, full:], ret[:, :full]],
                           axis=1).astype(BF16)


def _retention_tables(gn_gain):
    nh = 2 * RET_PAIRS
    slot_head = np.array([RET_HEADS if h is None else h for h in RET_SLOTS])
    log_g = jnp.log(1.0 - 2.0 ** (-5.0 - jnp.asarray(slot_head, F32)))
    i = jnp.arange(RET_CHUNK, dtype=F32)
    diff = i[:, None] - i[None, :]
    intra = jnp.where(diff >= 0, jnp.exp(jnp.maximum(diff, 0.0)[None] * log_g[:, None, None]), 0.0)
    read_decay = jnp.exp((i + 1.0)[None, :] * log_g[:, None])
    write_decay = jnp.exp((RET_CHUNK - 1.0 - i)[None, :] * log_g[:, None])
    chunk_decay = jnp.exp(RET_CHUNK * log_g)

    def lanes(t):
        t = t.reshape(RET_PAIRS, 2, -1)
        return jnp.repeat(t.transpose(0, 2, 1), HALF, axis=-1)

    gn = jnp.pad(gn_gain, ((0, 0), (0, nh - RET_HEADS), (0, 0)))[:, slot_head]
    gn = gn.reshape(gn.shape[0], RET_PAIRS, 1, LANES)
    return (gn, intra.reshape(RET_PAIRS, 2, RET_CHUNK, RET_CHUNK), lanes(read_decay), lanes(write_decay),
            lanes(chunk_decay[:, None]))


def _selection_overlap(seq):
    n_cmp = (seq - CMP_LEN) // CMP_STRIDE + 1
    n_sel = seq // SEL_BLOCK
    cs = np.arange(n_cmp) * CMP_STRIDE
    ss = np.arange(n_sel) * SEL_BLOCK
    ov = np.clip(np.minimum(cs[:, None] + CMP_LEN, ss[None, :] + SEL_BLOCK)
                 - np.maximum(cs[:, None], ss[None, :]), 0, None) / CMP_LEN
    ovl_t = np.zeros((n_sel, seq // CMP_STRIDE), np.float32)
    ovl_t[:, :n_cmp] = ov.T
    return jnp.asarray(ovl_t, BF16)


def kernel(x, positions, ln1_gain, w_in, cmp_pos_k, cmp_w1_k, cmp_w2_k, cmp_pos_v, cmp_w1_v, cmp_w2_v,
           mla_q_norm, mla_w_uq, mla_kv_norm, mla_w_ukv, ret_gn_gain, w_out, ln2_gain, w_up, w_down,
           final_gain):
    batch, seq, _ = x.shape
    depth = w_in.shape[0]
    t = batch * seq

    tab_n, tab_m, tab_r = _rope_tables(positions)

    w_in_p, w_in_t = _in_weight(w_in)
    pos_k, w1_k, w2_k = _compress_weights(cmp_pos_k, cmp_w1_k, cmp_w2_k)
    pos_v, w1_v, w2_v = _compress_weights(cmp_pos_v, cmp_w1_v, cmp_w2_v)
    cmp_pos = jnp.stack([pos_k, pos_v], axis=1)
    cmp_w1 = jnp.stack([w1_k, w1_v], axis=1)
    w2_vt = w2_v.transpose(0, 2, 1)
    wq, wk, wvt = _mla_weights(mla_w_uq, mla_w_ukv)
    wo = _out_weight(w_out)
    wu = w_up.astype(BF16)
    wd = w_down.astype(BF16)
    gn, intra, rd, wdec, cd = _retention_tables(ret_gn_gain)
    ovl_t = _selection_overlap(seq)
    gf = final_gain.reshape(1, D_MODEL)

    x2 = x.reshape(t, D_MODEL)
    for l in range(depth):
        nsa, nsa_t, gate_t, k_cmp, v_cmp, ret, rgate, q_m, k_m, vt_m = _in_proj(
            x2, ln1_gain[l].reshape(1, D_MODEL), w_in_p, w_in_t, mla_q_norm[l].reshape(1, -1),
            mla_kv_norm[l].reshape(1, -1), wq, wk, wvt, l, tab_n, tab_m, tab_r)
        kc, vct = _compress(k_cmp, v_cmp, cmp_pos[l], cmp_w1, w2_k[l], w2_vt[l], l, batch, seq)
        o_nsa = _nsa_attention(nsa, nsa_t, gate_t, kc, vct, ovl_t, batch, seq)
        o_mla = _mla_attention(q_m, k_m, vt_m, batch, seq)
        o_ret = _retention(ret, rgate, gn[l], intra, rd, wdec, cd, batch, seq)
        x2 = _out_mlp(x2, o_nsa, o_mla, o_ret, wo, ln2_gain[l].reshape(1, D_MODEL), wu, wd, l, gf,
                      final_norm=(l == depth - 1))
    return x2.reshape(batch, seq, D_MODEL)
```

```python
import functools
import math

import numpy as np
import jax
import jax.numpy as jnp
from jax import lax
from jax.experimental import pallas as pl
from jax.experimental.pallas import tpu as pltpu

F32 = jnp.float32
BF16 = jnp.bfloat16

D_MODEL = 1024
HEAD_DIM = 64
NSA_HEADS = 6
NSA_KV_GROUPS = 2
NSA_REP = NSA_HEADS // NSA_KV_GROUPS
N_BRANCH = 3
CMP_LEN = 32
CMP_STRIDE = 16
CMP_HIDDEN = 2 * HEAD_DIM
SEL_BLOCK = 64
SEL_TOP_N = 16
WINDOW = 512
MLA_HEADS = 5
MLA_Q_RANK = 256
MLA_KV_RANK = 128
MLA_NOPE_DIM = 64
MLA_ROPE_DIM = 32
MLA_V_DIM = 64
RET_HEADS = 5
RET_CHUNK = 128
ROPE_THETA = 500000.0
PARTIAL_ROPE_DIM = HEAD_DIM // 4
RET_THETA = 10000.0
D_FF = 4 * D_MODEL
NORM_EPS = 1e-6
NEG_INF = -1e30
FORCE_SCORE = 1e9
LOG2E = math.log2(math.e)

NSA_Q_W = NSA_HEADS * HEAD_DIM
NSA_KV_W = NSA_KV_GROUPS * HEAD_DIM
NSA_GATE_W = NSA_HEADS * N_BRANCH
RET_W = RET_HEADS * HEAD_DIM
IN_SIZES = (NSA_Q_W, NSA_KV_W, NSA_KV_W, NSA_KV_W, NSA_KV_W, NSA_KV_W, NSA_KV_W, NSA_GATE_W,
            MLA_Q_RANK, MLA_KV_RANK, MLA_ROPE_DIM, RET_W, RET_W, RET_W, RET_W)

LANES = 128
HALF = LANES // 2
SUBLANES = 8
VMEM_LIMIT = 56 * 1024 * 1024

NSA_TILES = 5
GATE_ROWS = 32
NSA_T_ROWS = 2 * LANES + GATE_ROWS
MLA_IN_TILES = 4
RET_PAIRS = 3
RET_SLOTS = (0, 1, 2, 3, None, 4)
RET_K_TILES = RET_PAIRS - 1
RET_ROPE_TILES = RET_PAIRS + RET_K_TILES
RET_MM_TILES = RET_ROPE_TILES + RET_PAIRS
RET_TILES = 3 * RET_PAIRS
IN_TILES = NSA_TILES + 2 + MLA_IN_TILES + RET_MM_TILES + RET_PAIRS
N_PAD = IN_TILES * LANES
MLA_V_ROWS = MLA_HEADS * MLA_V_DIM

IN_TM = 512
IN_CHUNK_TILES = 8
MLP_TM = 1024
ATT_T = 512
MLP_FF_CHUNK = 512


def _nn(a, b):
    return jnp.dot(a, b, preferred_element_type=F32)


def _nt(a, b):
    return lax.dot_general(a, b, (((1,), (1,)), ((), ())), preferred_element_type=F32)


def _rms(x, gain):
    return x * lax.rsqrt(jnp.mean(x * x, axis=-1, keepdims=True) + NORM_EPS) * gain


def _rope(val, tab, half):
    cos = tab[:, 0:LANES]
    sin_a = tab[:, LANES:2 * LANES]
    sin_b = tab[:, 2 * LANES:3 * LANES]
    return (val * cos + pltpu.roll(val, LANES - half, 1) * sin_a
            + pltpu.roll(val, half, 1) * sin_b)


def _lane_lo(shape):
    return lax.broadcasted_iota(jnp.int32, shape, len(shape) - 1) < HALF


def _mla_up(c, qn_ref, kvn_ref, wq_ref, wk_ref, wvt_ref, tm_ref, q_ref, k_ref, vt_ref):
    scale = (MLA_NOPE_DIM + MLA_ROPE_DIM) ** -0.5 * LOG2E
    cq = _rms(c[:, 0:MLA_Q_RANK], qn_ref[...]).astype(BF16)
    ckv = _rms(c[:, MLA_Q_RANK:MLA_Q_RANK + MLA_KV_RANK], kvn_ref[...]).astype(BF16)
    k_pe = _rope(c[:, 3 * LANES:4 * LANES], tm_ref[...], MLA_ROPE_DIM // 2)
    cos = tm_ref[:, 0:LANES] * scale
    sin = (tm_ref[:, 2 * LANES:3 * LANES] - tm_ref[:, LANES:2 * LANES]) * scale
    q = _nn(cq, wq_ref[...])
    k = _nn(ckv, wk_ref[...])
    w = MLA_HEADS * LANES
    for hd in range(MLA_HEADS):
        sl = slice(hd * LANES, (hd + 1) * LANES)
        rot = slice(w + hd * LANES, w + (hd + 1) * LANES)
        q_ref[:, sl] = (q[:, sl] * cos + q[:, rot] * sin).astype(BF16)
        k_ref[:, sl] = (k[:, sl] + k_pe).astype(BF16)
    vt_ref[...] = _nt(wvt_ref[...], ckv).astype(BF16)


def _inproj_kernel(x_ref, g_ref, w_ref, wt_ref, tn_ref, tm_ref, tr_ref, qn_ref, kvn_ref, wq_ref, wk_ref,
                   wvt_ref, nsa_ref, nsat_ref, gatet_ref, kcmp_ref, vcmp_ref, ret_ref, rgate_ref,
                   mq_ref, mk_ref, mvt_ref):
    h = _rms(x_ref[...], g_ref[...]).astype(BF16)
    tab_n = tn_ref[...]
    tab_r = tr_ref[...]

    chunks = [_nn(h, w_ref[:, c:min(c + IN_CHUNK_TILES * LANES, N_PAD)])
              for c in range(0, N_PAD, IN_CHUNK_TILES * LANES)]

    def tiles(first, n):
        cols = [chunks[i // IN_CHUNK_TILES][:, (i % IN_CHUNK_TILES) * LANES:(i % IN_CHUNK_TILES + 1) * LANES]
                for i in range(first, first + n)]
        return cols[0] if n == 1 else jnp.concatenate(cols, axis=1)

    t0 = 0
    for i in range(NSA_TILES):
        v = _rope(tiles(t0 + i, 1), tab_n, PARTIAL_ROPE_DIM // 2)
        if i < NSA_REP:
            v = v * LOG2E
        nsa_ref[:, i * LANES:(i + 1) * LANES] = v.astype(BF16)
    t0 += NSA_TILES
    kcmp_ref[...] = _rope(tiles(t0, 1), tab_n, PARTIAL_ROPE_DIM // 2)
    vcmp_ref[...] = tiles(t0 + 1, 1)
    t0 += 2
    _mla_up(tiles(t0, MLA_IN_TILES), qn_ref, kvn_ref, wq_ref, wk_ref, wvt_ref, tm_ref, mq_ref, mk_ref, mvt_ref)
    t0 += MLA_IN_TILES
    stored = []
    for i in range(RET_MM_TILES):
        v = tiles(t0 + i, 1)
        if i < RET_ROPE_TILES:
            v = _rope(v, tab_r, HEAD_DIM // 2)
        stored.append(v)
        if i == RET_ROPE_TILES - 1:
            stored.append(pltpu.roll(stored[RET_PAIRS - 1], HALF, 1))
    for i, v in enumerate(stored):
        ret_ref[:, i * LANES:(i + 1) * LANES] = v.astype(BF16)
    t0 += RET_MM_TILES
    rgate_ref[...] = tiles(t0, RET_PAIRS)
    at = _nt(wt_ref[...], h)
    nsat_ref[...] = at[0:2 * LANES, :].astype(BF16)
    gatet_ref[...] = at[2 * LANES:NSA_T_ROWS, :]


def _layer_spec(w, layer, **kwargs):
    zeros = (0,) * (w.ndim - 1)
    return pl.BlockSpec((None,) + w.shape[1:], lambda *_: (layer,) + zeros, **kwargs)


def _in_proj(x2, gain, w, wt, q_norm, kv_norm, wq, wk, wvt, layer, tab_n, tab_m, tab_r):
    t = x2.shape[0]
    row = lambda i: (i, 0)
    col = lambda i: (0, i)
    const = lambda i: (0, 0)
    out_shapes = (
        jax.ShapeDtypeStruct((t, NSA_TILES * LANES), BF16),
        jax.ShapeDtypeStruct((2 * LANES, t), BF16),
        jax.ShapeDtypeStruct((GATE_ROWS, t), F32),
        jax.ShapeDtypeStruct((t, LANES), F32),
        jax.ShapeDtypeStruct((t, LANES), F32),
        jax.ShapeDtypeStruct((t, RET_TILES * LANES), BF16),
        jax.ShapeDtypeStruct((t, RET_PAIRS * LANES), F32),
        jax.ShapeDtypeStruct((t, MLA_HEADS * LANES), BF16),
        jax.ShapeDtypeStruct((t, MLA_HEADS * LANES), BF16),
        jax.ShapeDtypeStruct((MLA_V_ROWS, t), BF16),
    )
    out_specs = tuple(
        pl.BlockSpec((s.shape[0], IN_TM), col) if s.shape[1] == t else pl.BlockSpec((IN_TM, s.shape[1]), row)
        for s in out_shapes)
    return pl.pallas_call(
        _inproj_kernel,
        grid=(t // IN_TM,),
        in_specs=[
            pl.BlockSpec((IN_TM, D_MODEL), row),
            pl.BlockSpec((1, D_MODEL), const),
            _layer_spec(w, layer),
            _layer_spec(wt, layer),
            pl.BlockSpec((IN_TM, 3 * LANES), row),
            pl.BlockSpec((IN_TM, 3 * LANES), row),
            pl.BlockSpec((IN_TM, 3 * LANES), row),
            pl.BlockSpec((1, MLA_Q_RANK), const),
            pl.BlockSpec((1, MLA_KV_RANK), const),
            _layer_spec(wq, layer),
            _layer_spec(wk, layer),
            _layer_spec(wvt, layer),
        ],
        out_specs=out_specs,
        out_shape=out_shapes,
        compiler_params=pltpu.CompilerParams(
            dimension_semantics=("parallel",), vmem_limit_bytes=VMEM_LIMIT),
        name="in_proj",
    )(x2, gain, w, wt, tab_n, tab_m, tab_r, q_norm, kv_norm, wq, wk, wvt)


def _compress_kernel(k_ref, v_ref, pos_ref, w1_ref, w2k_ref, w2vt_ref, kc_ref, vct_ref):
    n_blk = k_ref.shape[0] // CMP_STRIDE

    def hidden(src, i):
        toks = [src[pl.ds(r, n_blk, stride=CMP_STRIDE), :] for r in range(CMP_STRIDE)]
        halves = []
        for half in range(CMP_LEN // CMP_STRIDE):
            off = half * CMP_STRIDE
            x = jnp.concatenate([(toks[r] + pos_ref[i, off + r:off + r + 1, :]).astype(BF16)
                                 for r in range(CMP_STRIDE)], axis=1)
            w = w1_ref[i, off:off + CMP_STRIDE].reshape(CMP_STRIDE * LANES, NSA_KV_GROUPS * CMP_HIDDEN)
            halves.append(_nn(x, w))
        return jax.nn.gelu(halves[0] + pltpu.roll(halves[1], n_blk - 1, 0)).astype(BF16)

    kc_ref[0] = _nn(hidden(k_ref, 0), w2k_ref[...]).astype(BF16)
    vct_ref[0] = _nt(w2vt_ref[...], hidden(v_ref, 1)).astype(BF16)


def _compress(k_cmp, v_cmp, pos, w1, w2k, w2vt, layer, batch, seq):
    b = batch
    n_blk = seq // CMP_STRIDE
    return pl.pallas_call(
        _compress_kernel,
        grid=(b,),
        in_specs=[
            pl.BlockSpec((seq, LANES), lambda i: (i, 0)),
            pl.BlockSpec((seq, LANES), lambda i: (i, 0)),
            pl.BlockSpec(pos.shape, lambda i: (0, 0, 0)),
            _layer_spec(w1, layer),
            pl.BlockSpec(w2k.shape, lambda i: (0, 0)),
            pl.BlockSpec(w2vt.shape, lambda i: (0, 0)),
        ],
        out_specs=(pl.BlockSpec((1, n_blk, LANES), lambda i: (i, 0, 0)),
                   pl.BlockSpec((1, LANES, n_blk), lambda i: (i, 0, 0))),
        out_shape=(jax.ShapeDtypeStruct((b, n_blk, LANES), BF16),
                   jax.ShapeDtypeStruct((b, LANES, n_blk), BF16)),
        compiler_params=pltpu.CompilerParams(
            dimension_semantics=("parallel",), vmem_limit_bytes=VMEM_LIMIT),
        name="nsa_compress",
    )(k_cmp, v_cmp, pos, w1, w2k, w2vt)


SCORE_LOOKAHEAD = 2
ONES_ROWS = 16


def _tri_scores(k, q, dead_upper, mask, bias=None):
    half = k.shape[0] // 2
    dead = jnp.full((half, half), NEG_INF, F32)
    if dead_upper:
        top = _nt(k[:half], q)
        bottom = jnp.concatenate([dead, _nt(k[half:], q[half:])], axis=1)
    else:
        top = jnp.concatenate([_nt(k[:half], q[:half]), dead], axis=1)
        bottom = _nt(k[half:], q)
    s_t = jnp.concatenate([top, bottom], axis=0)
    return jnp.where(mask, s_t if bias is None else s_t + bias, NEG_INF)


def _tri_pv(dead_upper, v_ext, p):
    half = p.shape[0] // 2
    if dead_upper:
        full = _nn(v_ext[:, :half], p[:half])
        part = _nn(v_ext[:, half:], p[half:, half:])
        return jnp.concatenate([full[:, :half], full[:, half:] + part], axis=1)
    part = _nn(v_ext[:, :half], p[:half, :half])
    full = _nn(v_ext[:, half:], p[half:])
    return jnp.concatenate([full[:, :half] + part, full[:, half:]], axis=1)


def _softmax_steps(score_fns, v_ts, states, pv_fn=_nn):
    n = len(score_fns)
    s_ts = [score_fns[h]() if h < SCORE_LOOKAHEAD else None for h in range(n)]
    out = []
    for h in range(n):
        if h + SCORE_LOOKAHEAD < n:
            s_ts[h + SCORE_LOOKAHEAD] = score_fns[h + SCORE_LOOKAHEAD]()
        m_old, acc_old = states[h]
        m = jnp.maximum(m_old, jnp.max(s_ts[h], axis=0, keepdims=True))
        p = jnp.exp2(s_ts[h] - m).astype(BF16)
        s_ts[h] = None
        v_ext = jnp.concatenate([v_ts[h], jnp.ones((ONES_ROWS, v_ts[h].shape[1]), BF16)], axis=0)
        out.append((m, jnp.exp2(m_old - m) * acc_old + pv_fn(v_ext, p)))
    return tuple(out)


def _softmax_init(cols):
    return jnp.full((1, cols), NEG_INF, F32), jnp.zeros((HEAD_DIM + ONES_ROWS, cols), F32)


def _softmax_finish(state):
    acc = state[1]
    return acc[0:HEAD_DIM] * (1.0 / jnp.maximum(acc[HEAD_DIM:HEAD_DIM + 1], 1e-30))


def _nsa_kernel(q_ref, gatet_ref, kc_ref, vct_ref, ks_ref, kw_ref, vst_ref, vwt_ref, ovl_ref,
                o_ref, out_t_ref, selb_ref):
    tq = ATT_T
    qi = pl.program_id(1)
    q0 = pl.multiple_of(qi * tq, tq)
    lo1 = _lane_lo((1, LANES))
    group_lanes = (lo1, jnp.logical_not(lo1))
    gate = jax.nn.sigmoid(gatet_ref[...])
    heads = range(NSA_HEADS)

    def q_head(h):
        j, g = h % NSA_REP, h // NSA_REP
        tile = q_ref[:, j * LANES:(j + 1) * LANES]
        return jnp.where(group_lanes[g], tile, jnp.zeros_like(tile))

    def v_rows(ref, h, k0, n):
        g = h // NSA_REP
        return ref[g * HEAD_DIM:(g + 1) * HEAD_DIM, pl.ds(k0, n)]

    def emit(h, branch, qs, nq, o_t):
        j, g = h % NSA_REP, h // NSA_REP
        r0 = j * LANES + g * HEAD_DIM
        row = h * N_BRANCH + branch
        val = gate[row:row + 1, qs:qs + nq] * o_t
        if branch == 0:
            out_t_ref[r0:r0 + HEAD_DIM, qs:qs + nq] = val
        else:
            out_t_ref[r0:r0 + HEAD_DIM, qs:qs + nq] += val

    qm = [q_head(h) for h in heads]

    n_cmp_pad = kc_ref.shape[1]
    kc = kc_ref[0]
    vct = vct_ref[0]
    n_i = lax.broadcasted_iota(jnp.int32, (n_cmp_pad, tq), 0)
    t_l = q0 + lax.broadcasted_iota(jnp.int32, (n_cmp_pad, tq), 1)
    cmask = (n_i * CMP_STRIDE + (CMP_LEN - 1)) <= t_l
    cmask_f = cmask.astype(F32)
    s_cs = [jnp.where(cmask, _nt(kc, qm[h]), NEG_INF) for h in heads]
    p_cs = [jnp.exp2(s_t - jnp.max(s_t, axis=0, keepdims=True)) * cmask_f for s_t in s_cs]
    p_cs = [p * (1.0 / jnp.maximum(jnp.sum(p, axis=0, keepdims=True), 1e-30)) for p in p_cs]
    for h in heads:
        g = h // NSA_REP
        emit(h, 0, 0, tq, _nn(vct[g * HEAD_DIM:(g + 1) * HEAD_DIM, :], p_cs[h].astype(BF16)))

    n_sel = ovl_ref.shape[0]
    needs_rank = q0 + tq > SEL_TOP_N * SEL_BLOCK

    @pl.when(jnp.logical_not(needs_rank))
    def _():
        selb_ref[...] = jnp.zeros_like(selb_ref)

    @pl.when(needs_rank)
    def _():
        m_i = lax.broadcasted_iota(jnp.int32, (n_sel, tq), 0)
        cur = jnp.right_shift(q0 + lax.broadcasted_iota(jnp.int32, (n_sel, tq), 1),
                              SEL_BLOCK.bit_length() - 1)
        valid = m_i <= cur
        forced = (m_i == 0) | (m_i == cur) | (m_i == cur - 1)
        ovl = ovl_ref[...]
        sub = SUBLANES
        m_loc = lax.broadcasted_iota(jnp.int32, (sub, tq), 0)
        for g in range(NSA_KV_GROUPS):
            psum = p_cs[g * NSA_REP]
            for h in range(g * NSA_REP + 1, (g + 1) * NSA_REP):
                psum = psum + p_cs[h]
            p_hi = psum.astype(BF16)
            p_lo = (psum - p_hi.astype(F32)).astype(BF16)
            imp = _nn(ovl, p_hi) + _nn(ovl, p_lo)
            imp = jnp.where(valid & forced, FORCE_SCORE, imp)
            imp = jnp.where(valid, imp, NEG_INF)
            parts = [imp[i:i + sub] for i in range(0, n_sel, sub)]
            ranks = [jnp.zeros((sub, tq), jnp.int32) for _ in parts]
            for mp in range(n_sel):
                row = imp[mp:mp + 1, :]
                for i, part in enumerate(parts):
                    if i * sub + sub - 1 <= mp:
                        beats = row > part
                    elif i * sub > mp:
                        beats = row >= part
                    else:
                        beats = (row > part) | ((row == part) & (m_loc + i * sub > mp))
                    ranks[i] = ranks[i] + beats.astype(jnp.int32)
            rank = jnp.concatenate(ranks, axis=0)
            selb_ref[g] = jnp.where(rank < SEL_TOP_N, 0.0, NEG_INF)

    blocks_per_chunk = tq // SEL_BLOCK

    def sel_bias(g, c, n_keys, qs, nq):
        rows = [jnp.broadcast_to(selb_ref[g, pl.ds(c * blocks_per_chunk + i, 1), qs:qs + nq], (SEL_BLOCK, nq))
                for i in range(n_keys // SEL_BLOCK)]
        return jnp.concatenate(rows, axis=0)

    key_i = lax.broadcasted_iota(jnp.int32, (tq, tq), 0)
    qry_i = lax.broadcasted_iota(jnp.int32, (tq, tq), 1)
    causal_mask = key_i <= qry_i

    def sel_chunk(c, k0, states, causal):
        k = ks_ref[pl.ds(k0, tq), :]
        bias = [sel_bias(g, c, tq, 0, tq) for g in range(NSA_KV_GROUPS)]

        def score_fn(h):
            if causal:
                return _tri_scores(k, qm[h], True, causal_mask, bias[h // NSA_REP])
            return _nt(k, qm[h]) + bias[h // NSA_REP]

        return _softmax_steps([functools.partial(score_fn, h) for h in heads],
                              [v_rows(vst_ref, h, k0, tq) for h in heads], states,
                              functools.partial(_tri_pv, True) if causal else _nn)

    states = sel_chunk(qi, q0, tuple(_softmax_init(tq) for _ in heads), True)
    states = lax.fori_loop(
        0, qi, lambda c, st: sel_chunk(c, pl.multiple_of(c * tq, tq), st, False), states)
    for h in heads:
        emit(h, 1, 0, tq, _softmax_finish(states[h]))

    def win_chunk(k0, states, mask, dead_upper):
        k = kw_ref[pl.ds(k0, tq), :]
        if dead_upper is None:
            score_fns = [lambda h=h: jnp.where(mask, _nt(k, qm[h]), NEG_INF) for h in heads]
            pv_fn = _nn
        else:
            score_fns = [lambda h=h: _tri_scores(k, qm[h], dead_upper, mask) for h in heads]
            pv_fn = functools.partial(_tri_pv, dead_upper)
        return _softmax_steps(score_fns, [v_rows(vwt_ref, h, k0, tq) for h in heads], states, pv_fn)

    states = win_chunk(q0, tuple(_softmax_init(tq) for _ in heads), causal_mask, True)
    for d in range(1, WINDOW // tq + 1):
        k0 = pl.multiple_of(jnp.maximum(qi - d, 0) * tq, tq)
        in_band = (key_i - d * tq > qry_i - WINDOW) & (qi >= d)
        states = win_chunk(k0, states, in_band, False if d * tq == WINDOW else None)
    for h in heads:
        emit(h, 2, 0, tq, _softmax_finish(states[h]))

    for j in range(NSA_REP):
        o_ref[:, j * LANES:(j + 1) * LANES] = out_t_ref[j * LANES:(j + 1) * LANES, :].T.astype(BF16)


def _nsa_attention(nsa, nsa_t, gate_t, kc, vct, ovl_t, batch, seq):
    tq = ATT_T
    nq = seq // tq
    n_blk = kc.shape[1]
    qrow = lambda b, i: (b * nq + i, 0)
    return pl.pallas_call(
        _nsa_kernel,
        grid=(batch, nq),
        in_specs=[
            pl.BlockSpec((tq, NSA_REP * LANES), qrow),
            pl.BlockSpec((GATE_ROWS, tq), lambda b, i: (0, b * nq + i)),
            pl.BlockSpec((1, n_blk, LANES), lambda b, i: (b, 0, 0)),
            pl.BlockSpec((1, LANES, n_blk), lambda b, i: (b, 0, 0)),
            pl.BlockSpec((seq, LANES), lambda b, i: (b, 3)),
            pl.BlockSpec((seq, LANES), lambda b, i: (b, 4)),
            pl.BlockSpec((LANES, seq), lambda b, i: (0, b)),
            pl.BlockSpec((LANES, seq), lambda b, i: (1, b)),
            pl.BlockSpec(ovl_t.shape, lambda b, i: (0, 0)),
        ],
        out_specs=pl.BlockSpec((tq, NSA_REP * LANES), qrow),
        out_shape=jax.ShapeDtypeStruct((batch * seq, NSA_REP * LANES), BF16),
        scratch_shapes=[pltpu.VMEM((NSA_REP * LANES, tq), F32),
                        pltpu.VMEM((NSA_KV_GROUPS, seq // SEL_BLOCK, tq), F32)],
        compiler_params=pltpu.CompilerParams(
            dimension_semantics=("parallel", "arbitrary"), vmem_limit_bytes=VMEM_LIMIT),
        name="nsa_attention",
    )(nsa, gate_t, kc, vct, nsa, nsa, nsa_t, nsa_t, ovl_t)


def _mla_attn_kernel(q_ref, k_ref, vt_ref, o_ref, out_t_ref):
    tq = ATT_T
    qi = pl.program_id(1)
    q0 = pl.multiple_of(qi * tq, tq)
    heads = range(MLA_HEADS)
    qh = [q_ref[:, h * LANES:(h + 1) * LANES] for h in heads]

    def chunk(k0, states, causal):
        if causal:
            mask = (lax.broadcasted_iota(jnp.int32, (tq, tq), 0)
                    <= lax.broadcasted_iota(jnp.int32, (tq, tq), 1))

        def score_fn(h):
            k = k_ref[pl.ds(k0, tq), h * LANES:(h + 1) * LANES]
            return _tri_scores(k, qh[h], True, mask) if causal else _nt(k, qh[h])

        v_ts = [vt_ref[h * MLA_V_DIM:(h + 1) * MLA_V_DIM, pl.ds(k0, tq)] for h in heads]
        return _softmax_steps([functools.partial(score_fn, h) for h in heads], v_ts, states,
                              functools.partial(_tri_pv, True) if causal else _nn)

    states = chunk(q0, tuple(_softmax_init(tq) for _ in heads), True)
    states = lax.fori_loop(0, qi, lambda c, st: chunk(pl.multiple_of(c * tq, tq), st, False), states)
    for h in heads:
        out_t_ref[h * MLA_V_DIM:(h + 1) * MLA_V_DIM, :] = _softmax_finish(states[h])
    out_t_ref[MLA_V_ROWS:, :] = jnp.zeros((RET_PAIRS * LANES - MLA_V_ROWS, tq), F32)
    for j in range(RET_PAIRS):
        o_ref[:, j * LANES:(j + 1) * LANES] = out_t_ref[j * LANES:(j + 1) * LANES, :].T.astype(BF16)


def _mla_attention(q, k, vt, batch, seq):
    tq = ATT_T
    nq = seq // tq
    return pl.pallas_call(
        _mla_attn_kernel,
        grid=(batch, nq),
        in_specs=[
            pl.BlockSpec((tq, MLA_HEADS * LANES), lambda b, i: (b * nq + i, 0)),
            pl.BlockSpec((seq, MLA_HEADS * LANES), lambda b, i: (b, 0)),
            pl.BlockSpec((MLA_V_ROWS, seq), lambda b, i: (0, b)),
        ],
        out_specs=pl.BlockSpec((tq, RET_PAIRS * LANES), lambda b, i: (b * nq + i, 0)),
        out_shape=jax.ShapeDtypeStruct((batch * seq, RET_PAIRS * LANES), BF16),
        scratch_shapes=[pltpu.VMEM((RET_PAIRS * LANES, tq), F32)],
        compiler_params=pltpu.CompilerParams(
            dimension_semantics=("parallel", "arbitrary"), vmem_limit_bytes=VMEM_LIMIT),
        name="mla_attention",
    )(q, k, vt)


def _tn(a, b):
    return lax.dot_general(a, b, (((0,), (0,)), ((), ())), preferred_element_type=F32)


RET_UNROLL = 16


def _retention_kernel(q_ref, k_ref, v_ref, gate_ref, gn_ref, intra_ref, rd_ref, wd_ref, cd_ref, o_ref,
                      kv_ref, st_ref):
    c_len = RET_CHUNK
    n_chunks = q_ref.shape[0] // c_len
    lo = _lane_lo((1, LANES))
    hi = jnp.logical_not(lo)
    blockdiag = (lax.broadcasted_iota(jnp.int32, (LANES, LANES), 0) < HALF) == _lane_lo((LANES, LANES))
    intra_a = intra_ref[0, 0]
    intra_b = intra_ref[0, 1]
    read_decay = rd_ref[0]
    write_decay = wd_ref[0]
    chunk_decay = cd_ref[0]
    gn = gn_ref[0]

    averager = jnp.where(blockdiag, 1.0 / HEAD_DIM, 0.0).astype(BF16)

    def half_mean(x):
        x_hi = x.astype(BF16)
        x_lo = (x - x_hi.astype(F32)).astype(BF16)
        return _nn(x_hi, averager) + _nn(x_lo, averager)

    def kv_body(c, carry):
        r0 = pl.multiple_of(c * c_len, c_len)
        kc = k_ref[pl.ds(r0, c_len), :]
        kv_ref[c] = _tn((kc.astype(F32) * write_decay).astype(BF16), v_ref[pl.ds(r0, c_len), :])
        return carry

    lax.fori_loop(0, n_chunks, kv_body, 0, unroll=RET_UNROLL)

    state = jnp.zeros((LANES, LANES), F32)
    for c in range(n_chunks):
        st_ref[c] = jnp.where(blockdiag, state, 0.0).astype(BF16)
        state = state * chunk_decay + kv_ref[c]

    def out_body(step, carry):
        cs = [step * RET_UNROLL + u for u in range(RET_UNROLL)]
        rows = [pl.ds(pl.multiple_of(c * c_len, c_len), c_len) for c in cs]
        qs = [q_ref[r, :] for r in rows]
        ks = [k_ref[r, :] for r in rows]
        vs = [v_ref[r, :] for r in rows]
        zero = jnp.zeros_like(qs[0])
        sa = [(_nt(jnp.where(lo, q, zero), k) * intra_a).astype(BF16) for q, k in zip(qs, ks)]
        sb = [(_nt(jnp.where(hi, q, zero), k) * intra_b).astype(BF16) for q, k in zip(qs, ks)]
        cross = [_nn(q, st_ref[c]) * read_decay for q, c in zip(qs, cs)]
        os_ = [jnp.where(lo, _nn(a, v), _nn(b, v)) + x for a, b, v, x in zip(sa, sb, vs, cross)]
        ds_ = [o - half_mean(o) for o in os_]
        ys = [d * lax.rsqrt(half_mean(d * d) + NORM_EPS) * gn for d in ds_]
        for r, y in zip(rows, ys):
            o_ref[r, :] = (jax.nn.silu(gate_ref[r, :]) * y).astype(BF16)
        return carry

    lax.fori_loop(0, n_chunks // RET_UNROLL, out_body, 0)


def _retention(ret, rgate, gn, intra, rd, wd, cd, batch, seq):
    pair_const3 = lambda b, j: (j, 0, 0)
    return pl.pallas_call(
        _retention_kernel,
        grid=(batch, RET_PAIRS),
        in_specs=[
            pl.BlockSpec((seq, LANES), lambda b, j: (b, j)),
            pl.BlockSpec((seq, LANES), lambda b, j: (b, RET_PAIRS + j)),
            pl.BlockSpec((seq, LANES), lambda b, j: (b, 2 * RET_PAIRS + j)),
            pl.BlockSpec((seq, LANES), lambda b, j: (b, j)),
            pl.BlockSpec((1, 1, LANES), pair_const3),
            pl.BlockSpec((1, 2, RET_CHUNK, RET_CHUNK), lambda b, j: (j, 0, 0, 0)),
            pl.BlockSpec((1, RET_CHUNK, LANES), pair_const3),
            pl.BlockSpec((1, RET_CHUNK, LANES), pair_const3),
            pl.BlockSpec((1, 1, LANES), pair_const3),
        ],
        out_specs=pl.BlockSpec((seq, LANES), lambda b, j: (b, j)),
        out_shape=jax.ShapeDtypeStruct((batch * seq, RET_PAIRS * LANES), BF16),
        scratch_shapes=[pltpu.VMEM((seq // RET_CHUNK, LANES, LANES), F32),
                        pltpu.VMEM((seq // RET_CHUNK, LANES, LANES), BF16)],
        compiler_params=pltpu.CompilerParams(
            dimension_semantics=("parallel", "arbitrary"), vmem_limit_bytes=VMEM_LIMIT),
        name="retention",
    )(ret, ret, ret, rgate, gn, intra, rd, wd, cd)


def _out_mlp_kernel(x_ref, nsa_ref, mla_ref, ret_ref, wo_ref, g2_ref, wu_ref, wd_ref, gf_ref, o_ref,
                    *, final_norm):
    full = 2 * LANES
    mixed = jnp.concatenate([nsa_ref[...], mla_ref[:, :full], mla_ref[:, full:] + ret_ref[:, full:],
                             ret_ref[:, :full]], axis=1)
    x = x_ref[...] + _nn(mixed, wo_ref[...])
    h = _rms(x, g2_ref[...]).astype(BF16)
    y = x
    for c in range(D_FF // MLP_FF_CHUNK):
        sl = slice(c * MLP_FF_CHUNK, (c + 1) * MLP_FF_CHUNK)
        u = jnp.maximum(_nn(h, wu_ref[:, sl]), 0.0)
        y = y + _nn((u * u).astype(BF16), wd_ref[sl, :])
    if final_norm:
        y = _rms(y, gf_ref[...])
    o_ref[...] = y


def _out_mlp(x2, o_nsa, o_mla, o_ret, wo, g2, wu, wd, layer, gf, final_norm):
    t = x2.shape[0]
    w = RET_PAIRS * LANES
    row = lambda i: (i, 0)
    const = lambda i: (0, 0)
    resident = dict(pipeline_mode=pl.Buffered(1))
    return pl.pallas_call(
        functools.partial(_out_mlp_kernel, final_norm=final_norm),
        grid=(t // MLP_TM,),
        in_specs=[
            pl.BlockSpec((MLP_TM, D_MODEL), row),
            pl.BlockSpec((MLP_TM, w), row),
            pl.BlockSpec((MLP_TM, w), row),
            pl.BlockSpec((MLP_TM, w), row),
            _layer_spec(wo, layer, **resident),
            pl.BlockSpec((1, D_MODEL), const),
            _layer_spec(wu, layer, **resident),
            _layer_spec(wd, layer, **resident),
            pl.BlockSpec((1, D_MODEL), const),
        ],
        out_specs=pl.BlockSpec((MLP_TM, D_MODEL), row),
        out_shape=jax.ShapeDtypeStruct((t, D_MODEL), F32),
        compiler_params=pltpu.CompilerParams(
            dimension_semantics=("parallel",), vmem_limit_bytes=VMEM_LIMIT),
        name="out_mlp",
    )(x2, o_nsa, o_mla, o_ret, wo, g2, wu, wd, gf)


ROPE_KINDS = (
    (PARTIAL_ROPE_DIM, ROPE_THETA, HEAD_DIM, 0),
    (MLA_ROPE_DIM, ROPE_THETA, LANES, HALF),
    (HEAD_DIM, RET_THETA, HEAD_DIM, 0),
)
ROPE_TM = 1024


def _rope_placement():
    n_angles = sum(dim // 2 for dim, _, _, _ in ROPE_KINDS)
    assert 2 * n_angles <= LANES
    place = np.zeros((LANES, 3 * LANES * len(ROPE_KINDS)), np.float32)
    fill = np.zeros((1, 3 * LANES * len(ROPE_KINDS)), np.float32)
    row0 = 0
    for kind, (dim, _, period, base) in enumerate(ROPE_KINDS):
        half = dim // 2
        col0 = kind * 3 * LANES
        for lane in range(LANES):
            rel = (lane - base) % period
            first = lane >= base and rel < half
            second = lane >= base and half <= rel < dim
            if first or second:
                angle = rel if first else rel - half
                place[row0 + angle, col0 + lane] = 1.0
                place[n_angles + row0 + angle, col0 + (1 if first else 2) * LANES + lane] = -1.0 if first else 1.0
            else:
                fill[0, col0 + lane] = 1.0
        row0 += half
    return jnp.asarray(place, BF16), jnp.asarray(fill)


def _rope_kernel(cs_ref, place_ref, fill_ref, *out_refs):
    x = cs_ref[...]
    x1 = x.astype(BF16)
    r1 = x - x1.astype(F32)
    x2 = r1.astype(BF16)
    x3 = (r1 - x2.astype(F32)).astype(BF16)
    place = place_ref[...]
    tab = _tn(x1, place) + _tn(x2, place) + _tn(x3, place) + fill_ref[...]
    for i, ref in enumerate(out_refs):
        ref[...] = tab[:, i * 3 * LANES:(i + 1) * 3 * LANES]


def _rope_tables(positions):
    inv = jnp.concatenate([1.0 / (theta ** (jnp.arange(0, dim, 2, dtype=F32) / dim))
                           for dim, theta, _, _ in ROPE_KINDS])
    ang = inv[:, None] * positions.reshape(-1).astype(F32)[None, :]
    compact = jnp.concatenate([jnp.cos(ang), jnp.sin(ang)], axis=0)
    compact = jnp.pad(compact, ((0, LANES - compact.shape[0]), (0, 0)))
    place, fill = _rope_placement()
    t = compact.shape[1]
    tab_shape = jax.ShapeDtypeStruct((t, 3 * LANES), F32)
    return pl.pallas_call(
        _rope_kernel,
        grid=(t // ROPE_TM,),
        in_specs=[pl.BlockSpec((LANES, ROPE_TM), lambda i: (0, i)),
                  pl.BlockSpec(place.shape, lambda i: (0, 0)),
                  pl.BlockSpec(fill.shape, lambda i: (0, 0))],
        out_specs=tuple(pl.BlockSpec((ROPE_TM, 3 * LANES), lambda i: (i, 0)) for _ in ROPE_KINDS),
        out_shape=tuple(tab_shape for _ in ROPE_KINDS),
        compiler_params=pltpu.CompilerParams(
            dimension_semantics=("parallel",), vmem_limit_bytes=VMEM_LIMIT),
        name="rope_tables",
    )(compact, place, fill)


def _pad_cols(w, n):
    return jnp.pad(w, ((0, 0), (0, 0), (0, n - w.shape[-1])))


def _in_weight(w_in):
    offs = np.cumsum((0,) + IN_SIZES)
    (nsa_q, k_cmp, v_cmp, k_slc, v_slc, k_win, v_win, gate,
     cq, ckv, kpe, ret_q, ret_k, ret_v, ret_g) = [np.arange(offs[i], offs[i + 1]) for i in range(len(IN_SIZES))]
    scale = HEAD_DIM ** -0.5
    head = lambda cols, h: cols[h * HEAD_DIM:(h + 1) * HEAD_DIM]
    pad = lambda n: None if n == 0 else -np.ones(n, np.int64)
    pieces = []
    for j in range(NSA_REP):
        pieces += [(head(nsa_q, j), scale), (head(nsa_q, j + NSA_REP), scale)]
    pieces += [(c, 1.0) for c in (k_slc, k_win, k_cmp, v_cmp, cq, ckv)]
    pieces += [(pad(HALF), 0.0), (kpe, 1.0), (pad(LANES - HALF - MLA_ROPE_DIM), 0.0)]
    slots = lambda cols, f, fill: [fill if h is None else (head(cols, h), f) for h in RET_SLOTS]
    zero = (pad(HEAD_DIM), 0.0)
    pieces += slots(ret_q, 1.0, (head(ret_k, RET_SLOTS[-1]), scale))
    pieces += slots(ret_k, scale, zero)[:2 * RET_K_TILES]
    pieces += slots(ret_v, 1.0, zero) + slots(ret_g, 1.0, zero)
    t_pieces = [(v_slc, 1.0), (v_win, 1.0), (gate, 1.0), (pad(GATE_ROWS - NSA_GATE_W), 0.0)]

    def gather(parts):
        src = np.concatenate([c for c, _ in parts])
        factor = np.concatenate([np.full(len(c), f if (c >= 0).all() else 0.0, np.float32) for c, f in parts])
        return (jnp.take(w_in, jnp.asarray(np.maximum(src, 0), jnp.int32), axis=2) * factor).astype(BF16)

    w_main = gather(pieces)
    assert w_main.shape[-1] == N_PAD
    return w_main, gather(t_pieces).transpose(0, 2, 1)


def _compress_weights(pos, w1, w2):
    nl = pos.shape[0]
    g, dh, hid = NSA_KV_GROUPS, HEAD_DIM, CMP_HIDDEN
    assert g == 2
    p = jnp.tile(pos, (1, 1, g))

    def block_diag2(w):
        lead = ((0, 0),) * (w.ndim - 1)
        return jnp.concatenate([jnp.pad(w, lead + ((0, w.shape[-1]),)),
                                jnp.pad(w, lead + ((w.shape[-1], 0),))], axis=-2)

    w = block_diag2(w1.astype(BF16).reshape(nl, CMP_LEN, dh, hid))
    return p, w, block_diag2(w2.astype(BF16))


def _mla_weights(w_uq, w_ukv):
    nl = w_uq.shape[0]
    dq = MLA_NOPE_DIM + MLA_ROPE_DIM
    wq = w_uq.reshape(nl, MLA_Q_RANK, MLA_HEADS, dq)
    pe = wq[..., MLA_NOPE_DIM:]
    half = MLA_ROPE_DIM // 2
    rot = jnp.concatenate([jnp.zeros_like(wq[..., :MLA_NOPE_DIM]), -pe[..., half:], pe[..., :half]], axis=-1)
    pad = lambda w: jnp.pad(w, ((0, 0), (0, 0), (0, 0), (0, LANES - dq))).reshape(
        nl, MLA_Q_RANK, MLA_HEADS * LANES)
    wq = jnp.concatenate([pad(wq), pad(rot)], axis=-1)
    wkv = w_ukv.reshape(nl, MLA_KV_RANK, MLA_HEADS, MLA_NOPE_DIM + MLA_V_DIM)
    wk = jnp.pad(wkv[..., :MLA_NOPE_DIM], ((0, 0), (0, 0), (0, 0), (0, LANES - MLA_NOPE_DIM)))
    wk = wk.reshape(nl, MLA_KV_RANK, MLA_HEADS * LANES)
    wvt = wkv[..., MLA_NOPE_DIM:].reshape(nl, MLA_KV_RANK, MLA_V_ROWS).transpose(0, 2, 1)
    return wq.astype(BF16), wk.astype(BF16), wvt.astype(BF16)


def _out_weight(w_out):
    nl = w_out.shape[0]
    nsa = w_out[:, :NSA_Q_W].reshape(nl, NSA_HEADS, HEAD_DIM, D_MODEL)
    order = [h for j in range(NSA_REP) for h in (j, j + NSA_REP)]
    nsa = nsa[:, order].reshape(nl, NSA_Q_W, D_MODEL)
    mla = w_out[:, NSA_Q_W:NSA_Q_W + MLA_HEADS * MLA_V_DIM]
    ret = w_out[:, NSA_Q_W + MLA_HEADS * MLA_V_DIM:]
    full = 2 * LANES
    return jnp.concatenate([nsa, mla[:, :full], mla[:, full:], ret[:, full:], ret[:, :full]],
                           axis=1).astype(BF16)


def _retention_tables(gn_gain):
    nh = 2 * RET_PAIRS
    slot_head = np.array([RET_HEADS if h is None else h for h in RET_SLOTS])
    log_g = jnp.log(1.0 - 2.0 ** (-5.0 - jnp.asarray(slot_head, F32)))
    i = jnp.arange(RET_CHUNK, dtype=F32)
    diff = i[:, None] - i[None, :]
    intra = jnp.where(diff >= 0, jnp.exp(jnp.maximum(diff, 0.0)[None] * log_g[:, None, None]), 0.0)
    read_decay = jnp.exp((i + 1.0)[None, :] * log_g[:, None])
    write_decay = jnp.exp((RET_CHUNK - 1.0 - i)[None, :] * log_g[:, None])
    chunk_decay = jnp.exp(RET_CHUNK * log_g)

    def lanes(t):
        t = t.reshape(RET_PAIRS, 2, -1)
        return jnp.repeat(t.transpose(0, 2, 1), HALF, axis=-1)

    gn = jnp.pad(gn_gain, ((0, 0), (0, nh - RET_HEADS), (0, 0)))[:, slot_head]
    gn = gn.reshape(gn.shape[0], RET_PAIRS, 1, LANES)
    return (gn, intra.reshape(RET_PAIRS, 2, RET_CHUNK, RET_CHUNK), lanes(read_decay), lanes(write_decay),
            lanes(chunk_decay[:, None]))


def _selection_overlap(seq):
    n_cmp = (seq - CMP_LEN) // CMP_STRIDE + 1
    n_sel = seq // SEL_BLOCK
    cs = np.arange(n_cmp) * CMP_STRIDE
    ss = np.arange(n_sel) * SEL_BLOCK
    ov = np.clip(np.minimum(cs[:, None] + CMP_LEN, ss[None, :] + SEL_BLOCK)
                 - np.maximum(cs[:, None], ss[None, :]), 0, None) / CMP_LEN
    ovl_t = np.zeros((n_sel, seq // CMP_STRIDE), np.float32)
    ovl_t[:, :n_cmp] = ov.T
    return jnp.asarray(ovl_t, BF16)


def kernel(x, positions, ln1_gain, w_in, cmp_pos_k, cmp_w1_k, cmp_w2_k, cmp_pos_v, cmp_w1_v, cmp_w2_v,
           mla_q_norm, mla_w_uq, mla_kv_norm, mla_w_ukv, ret_gn_gain, w_out, ln2_gain, w_up, w_down,
           final_gain):
    batch, seq, _ = x.shape
    depth = w_in.shape[0]
    t = batch * seq

    tab_n, tab_m, tab_r = _rope_tables(positions)

    w_in_p, w_in_t = _in_weight(w_in)
    pos_k, w1_k, w2_k = _compress_weights(cmp_pos_k, cmp_w1_k, cmp_w2_k)
    pos_v, w1_v, w2_v = _compress_weights(cmp_pos_v, cmp_w1_v, cmp_w2_v)
    cmp_pos = jnp.stack([pos_k, pos_v], axis=1)
    cmp_w1 = jnp.stack([w1_k, w1_v], axis=1)
    w2_vt = w2_v.transpose(0, 2, 1)
    wq, wk, wvt = _mla_weights(mla_w_uq, mla_w_ukv)
    wo = _out_weight(w_out)
    wu = w_up.astype(BF16)
    wd = w_down.astype(BF16)
    gn, intra, rd, wdec, cd = _retention_tables(ret_gn_gain)
    ovl_t = _selection_overlap(seq)
    gf = final_gain.reshape(1, D_MODEL)

    x2 = x.reshape(t, D_MODEL)
    for l in range(depth):
        nsa, nsa_t, gate_t, k_cmp, v_cmp, ret, rgate, q_m, k_m, vt_m = _in_proj(
            x2, ln1_gain[l].reshape(1, D_MODEL), w_in_p, w_in_t, mla_q_norm[l].reshape(1, -1),
            mla_kv_norm[l].reshape(1, -1), wq, wk, wvt, l, tab_n, tab_m, tab_r)
        kc, vct = _compress(k_cmp, v_cmp, cmp_pos[l], cmp_w1, w2_k[l], w2_vt[l], l, batch, seq)
        o_nsa = _nsa_attention(nsa, nsa_t, gate_t, kc, vct, ovl_t, batch, seq)
        o_mla = _mla_attention(q_m, k_m, vt_m, batch, seq)
        o_ret = _retention(ret, rgate, gn[l], intra, rd, wdec, cd, batch, seq)
        x2 = _out_mlp(x2, o_nsa, o_mla, o_ret, wo, ln2_gain[l].reshape(1, D_MODEL), wu, wd, l, gf,
                      final_norm=(l == depth - 1))
    return x2.reshape(batch, seq, D_MODEL)
```

```python
import functools
import math

import numpy as np
import jax
import jax.numpy as jnp
from jax import lax
from jax.experimental import pallas as pl
from jax.experimental.pallas import tpu as pltpu

F32 = jnp.float32
BF16 = jnp.bfloat16

D_MODEL = 1024
HEAD_DIM = 64
NSA_HEADS = 6
NSA_KV_GROUPS = 2
NSA_REP = NSA_HEADS // NSA_KV_GROUPS
N_BRANCH = 3
CMP_LEN = 32
CMP_STRIDE = 16
CMP_HIDDEN = 2 * HEAD_DIM
SEL_BLOCK = 64
SEL_TOP_N = 16
WINDOW = 512
MLA_HEADS = 5
MLA_Q_RANK = 256
MLA_KV_RANK = 128
MLA_NOPE_DIM = 64
MLA_ROPE_DIM = 32
MLA_V_DIM = 64
RET_HEADS = 5
RET_CHUNK = 128
ROPE_THETA = 500000.0
PARTIAL_ROPE_DIM = HEAD_DIM // 4
RET_THETA = 10000.0
D_FF = 4 * D_MODEL
NORM_EPS = 1e-6
NEG_INF = -1e30
FORCE_SCORE = 1e9
LOG2E = math.log2(math.e)

NSA_Q_W = NSA_HEADS * HEAD_DIM
NSA_KV_W = NSA_KV_GROUPS * HEAD_DIM
NSA_GATE_W = NSA_HEADS * N_BRANCH
RET_W = RET_HEADS * HEAD_DIM
IN_SIZES = (NSA_Q_W, NSA_KV_W, NSA_KV_W, NSA_KV_W, NSA_KV_W, NSA_KV_W, NSA_KV_W, NSA_GATE_W,
            MLA_Q_RANK, MLA_KV_RANK, MLA_ROPE_DIM, RET_W, RET_W, RET_W, RET_W)

LANES = 128
HALF = LANES // 2
SUBLANES = 8
MXU_WIDTH = 256
VMEM_LIMIT = 56 * 1024 * 1024

NSA_TILES = 5
GATE_ROWS = 32
NSA_T_ROWS = 2 * LANES + GATE_ROWS
MLA_IN_TILES = 4
RET_PAIRS = 3
RET_SLOTS = (0, 1, 2, 3, None, 4)
RET_K_TILES = RET_PAIRS - 1
RET_ROPE_TILES = RET_PAIRS + RET_K_TILES
RET_MM_TILES = RET_ROPE_TILES + RET_PAIRS
RET_TILES = 3 * RET_PAIRS
IN_TILES = NSA_TILES + 2 + MLA_IN_TILES + RET_MM_TILES + RET_PAIRS
N_PAD = IN_TILES * LANES
MLA_V_ROWS = MLA_HEADS * MLA_V_DIM

IN_TM = 512
IN_CHUNK_TILES = 4 * MXU_WIDTH // LANES
MLP_TM = 1024
ATT_T = 512
MLP_FF_CHUNK = 512


def _nn(a, b):
    return jnp.dot(a, b, preferred_element_type=F32)


def _nt(a, b):
    return lax.dot_general(a, b, (((1,), (1,)), ((), ())), preferred_element_type=F32)


def _tn(a, b):
    return lax.dot_general(a, b, (((0,), (0,)), ((), ())), preferred_element_type=F32)


def _rms(x, gain):
    return x * lax.rsqrt(jnp.mean(x * x, axis=-1, keepdims=True) + NORM_EPS) * gain


def _rope(val, tab, half):
    cos = tab[:, 0:LANES]
    sin_a = tab[:, LANES:2 * LANES]
    sin_b = tab[:, 2 * LANES:3 * LANES]
    return (val * cos + pltpu.roll(val, LANES - half, 1) * sin_a
            + pltpu.roll(val, half, 1) * sin_b)


def _lane_lo(shape):
    return lax.broadcasted_iota(jnp.int32, shape, len(shape) - 1) < HALF


def _mla_up(c, qn_ref, kvn_ref, wq_ref, wk_ref, wvt_ref, tm_ref, q_ref, k_ref, vt_ref):
    scale = (MLA_NOPE_DIM + MLA_ROPE_DIM) ** -0.5 * LOG2E
    cq = _rms(c[:, 0:MLA_Q_RANK], qn_ref[...]).astype(BF16)
    ckv = _rms(c[:, MLA_Q_RANK:MLA_Q_RANK + MLA_KV_RANK], kvn_ref[...]).astype(BF16)
    k_pe = _rope(c[:, 3 * LANES:4 * LANES], tm_ref[...], MLA_ROPE_DIM // 2)
    cos = tm_ref[:, 0:LANES] * scale
    sin = (tm_ref[:, 2 * LANES:3 * LANES] - tm_ref[:, LANES:2 * LANES]) * scale
    q = _nn(cq, wq_ref[...])
    k = _nn(ckv, wk_ref[...])
    w = MLA_HEADS * LANES
    for hd in range(MLA_HEADS):
        sl = slice(hd * LANES, (hd + 1) * LANES)
        rot = slice(w + hd * LANES, w + (hd + 1) * LANES)
        q_ref[:, sl] = (q[:, sl] * cos + q[:, rot] * sin).astype(BF16)
        k_ref[:, sl] = (k[:, sl] + k_pe).astype(BF16)
    vt_ref[...] = _nt(wvt_ref[...], ckv).astype(BF16)


def _inproj_kernel(x_ref, g_ref, w_ref, wt_ref, tn_ref, tm_ref, tr_ref, qn_ref, kvn_ref, wq_ref, wk_ref,
                   wvt_ref, nsa_ref, nsat_ref, gatet_ref, kcmp_ref, vcmp_ref, ret_ref, rgate_ref,
                   mq_ref, mk_ref, mvt_ref):
    h = _rms(x_ref[...], g_ref[...]).astype(BF16)
    tab_n = tn_ref[...]
    tab_r = tr_ref[...]

    chunks = [_nn(h, w_ref[:, c:min(c + IN_CHUNK_TILES * LANES, N_PAD)])
              for c in range(0, N_PAD, IN_CHUNK_TILES * LANES)]

    def tiles(first, n):
        cols = [chunks[i // IN_CHUNK_TILES][:, (i % IN_CHUNK_TILES) * LANES:(i % IN_CHUNK_TILES + 1) * LANES]
                for i in range(first, first + n)]
        return cols[0] if n == 1 else jnp.concatenate(cols, axis=1)

    t0 = 0
    for i in range(NSA_TILES):
        v = _rope(tiles(t0 + i, 1), tab_n, PARTIAL_ROPE_DIM // 2)
        if i < NSA_REP:
            v = v * LOG2E
        nsa_ref[:, i * LANES:(i + 1) * LANES] = v.astype(BF16)
    t0 += NSA_TILES
    kcmp_ref[...] = _rope(tiles(t0, 1), tab_n, PARTIAL_ROPE_DIM // 2)
    vcmp_ref[...] = tiles(t0 + 1, 1)
    t0 += 2
    _mla_up(tiles(t0, MLA_IN_TILES), qn_ref, kvn_ref, wq_ref, wk_ref, wvt_ref, tm_ref, mq_ref, mk_ref, mvt_ref)
    t0 += MLA_IN_TILES
    stored = []
    for i in range(RET_MM_TILES):
        v = tiles(t0 + i, 1)
        if i < RET_ROPE_TILES:
            v = _rope(v, tab_r, HEAD_DIM // 2)
        stored.append(v)
        if i == RET_ROPE_TILES - 1:
            stored.append(pltpu.roll(stored[RET_PAIRS - 1], HALF, 1))
    for i, v in enumerate(stored):
        ret_ref[:, i * LANES:(i + 1) * LANES] = v.astype(BF16)
    t0 += RET_MM_TILES
    rgate_ref[...] = tiles(t0, RET_PAIRS)
    at = _nt(wt_ref[...], h)
    nsat_ref[...] = at[0:2 * LANES, :].astype(BF16)
    gatet_ref[...] = at[2 * LANES:NSA_T_ROWS, :]


def _layer_spec(w, layer, **kwargs):
    zeros = (0,) * (w.ndim - 1)
    return pl.BlockSpec((None,) + w.shape[1:], lambda *_: (layer,) + zeros, **kwargs)


def _in_proj(x2, gain, w, wt, q_norm, kv_norm, wq, wk, wvt, layer, tab_n, tab_m, tab_r):
    t = x2.shape[0]
    row = lambda i: (i, 0)
    col = lambda i: (0, i)
    const = lambda i: (0, 0)
    out_shapes = (
        jax.ShapeDtypeStruct((t, NSA_TILES * LANES), BF16),
        jax.ShapeDtypeStruct((2 * LANES, t), BF16),
        jax.ShapeDtypeStruct((GATE_ROWS, t), F32),
        jax.ShapeDtypeStruct((t, LANES), F32),
        jax.ShapeDtypeStruct((t, LANES), F32),
        jax.ShapeDtypeStruct((t, RET_TILES * LANES), BF16),
        jax.ShapeDtypeStruct((t, RET_PAIRS * LANES), F32),
        jax.ShapeDtypeStruct((t, MLA_HEADS * LANES), BF16),
        jax.ShapeDtypeStruct((t, MLA_HEADS * LANES), BF16),
        jax.ShapeDtypeStruct((MLA_V_ROWS, t), BF16),
    )
    out_specs = tuple(
        pl.BlockSpec((s.shape[0], IN_TM), col) if s.shape[1] == t else pl.BlockSpec((IN_TM, s.shape[1]), row)
        for s in out_shapes)
    return pl.pallas_call(
        _inproj_kernel,
        grid=(t // IN_TM,),
        in_specs=[
            pl.BlockSpec((IN_TM, D_MODEL), row),
            pl.BlockSpec((1, D_MODEL), const),
            _layer_spec(w, layer),
            _layer_spec(wt, layer),
            pl.BlockSpec((IN_TM, 3 * LANES), row),
            pl.BlockSpec((IN_TM, 3 * LANES), row),
            pl.BlockSpec((IN_TM, 3 * LANES), row),
            pl.BlockSpec((1, MLA_Q_RANK), const),
            pl.BlockSpec((1, MLA_KV_RANK), const),
            _layer_spec(wq, layer),
            _layer_spec(wk, layer),
            _layer_spec(wvt, layer),
        ],
        out_specs=out_specs,
        out_shape=out_shapes,
        compiler_params=pltpu.CompilerParams(
            dimension_semantics=("parallel",), vmem_limit_bytes=VMEM_LIMIT),
        name="in_proj",
    )(x2, gain, w, wt, tab_n, tab_m, tab_r, q_norm, kv_norm, wq, wk, wvt)


def _compress_kernel(k_ref, v_ref, pos_ref, w1_ref, w2k_ref, w2vt_ref, kc_ref, vct_ref):
    n_blk = k_ref.shape[0] // CMP_STRIDE

    def hidden(src, i):
        toks = [src[pl.ds(r, n_blk, stride=CMP_STRIDE), :] for r in range(CMP_STRIDE)]
        halves = []
        for half in range(CMP_LEN // CMP_STRIDE):
            off = half * CMP_STRIDE
            x = jnp.concatenate([(toks[r] + pos_ref[i, off + r:off + r + 1, :]).astype(BF16)
                                 for r in range(CMP_STRIDE)], axis=1)
            w = w1_ref[i, off:off + CMP_STRIDE].reshape(CMP_STRIDE * LANES, NSA_KV_GROUPS * CMP_HIDDEN)
            halves.append(_nn(x, w))
        return jax.nn.gelu(halves[0] + pltpu.roll(halves[1], n_blk - 1, 0)).astype(BF16)

    kc_ref[0] = _nn(hidden(k_ref, 0), w2k_ref[...]).astype(BF16)
    vct_ref[0] = _nt(w2vt_ref[...], hidden(v_ref, 1)).astype(BF16)


def _compress(k_cmp, v_cmp, pos, w1, w2k, w2vt, layer, batch, seq):
    b = batch
    n_blk = seq // CMP_STRIDE
    return pl.pallas_call(
        _compress_kernel,
        grid=(b,),
        in_specs=[
            pl.BlockSpec((seq, LANES), lambda i: (i, 0)),
            pl.BlockSpec((seq, LANES), lambda i: (i, 0)),
            pl.BlockSpec(pos.shape, lambda i: (0, 0, 0)),
            _layer_spec(w1, layer),
            pl.BlockSpec(w2k.shape, lambda i: (0, 0)),
            pl.BlockSpec(w2vt.shape, lambda i: (0, 0)),
        ],
        out_specs=(pl.BlockSpec((1, n_blk, LANES), lambda i: (i, 0, 0)),
                   pl.BlockSpec((1, LANES, n_blk), lambda i: (i, 0, 0))),
        out_shape=(jax.ShapeDtypeStruct((b, n_blk, LANES), BF16),
                   jax.ShapeDtypeStruct((b, LANES, n_blk), BF16)),
        compiler_params=pltpu.CompilerParams(
            dimension_semantics=("parallel",), vmem_limit_bytes=VMEM_LIMIT),
        name="nsa_compress",
    )(k_cmp, v_cmp, pos, w1, w2k, w2vt)


SCORE_LOOKAHEAD = 2
ONES_ROWS = 16


def _tri_scores(k, q, dead_upper, mask, bias=None):
    half = k.shape[0] // 2
    dead = jnp.full((half, half), NEG_INF, F32)
    if dead_upper:
        top = _nt(k[:half], q)
        bottom = jnp.concatenate([dead, _nt(k[half:], q[half:])], axis=1)
    else:
        top = jnp.concatenate([_nt(k[:half], q[:half]), dead], axis=1)
        bottom = _nt(k[half:], q)
    s_t = jnp.concatenate([top, bottom], axis=0)
    return jnp.where(mask, s_t if bias is None else s_t + bias, NEG_INF)


def _tri_pv(dead_upper, v_ext, p):
    half = p.shape[0] // 2
    if dead_upper:
        full = _nn(v_ext[:, :half], p[:half])
        part = _nn(v_ext[:, half:], p[half:, half:])
        return jnp.concatenate([full[:, :half], full[:, half:] + part], axis=1)
    part = _nn(v_ext[:, :half], p[:half, :half])
    full = _nn(v_ext[:, half:], p[half:])
    return jnp.concatenate([full[:, :half] + part, full[:, half:]], axis=1)


def _softmax_steps(score_fns, v_ts, states, pv_fn=_nn):
    n = len(score_fns)
    s_ts = [score_fns[h]() if h < SCORE_LOOKAHEAD else None for h in range(n)]
    out = []
    for h in range(n):
        if h + SCORE_LOOKAHEAD < n:
            s_ts[h + SCORE_LOOKAHEAD] = score_fns[h + SCORE_LOOKAHEAD]()
        m_old, acc_old = states[h]
        m = jnp.maximum(m_old, jnp.max(s_ts[h], axis=0, keepdims=True))
        p = jnp.exp2(s_ts[h] - m).astype(BF16)
        s_ts[h] = None
        v_ext = jnp.concatenate([v_ts[h], jnp.ones((ONES_ROWS, v_ts[h].shape[1]), BF16)], axis=0)
        out.append((m, jnp.exp2(m_old - m) * acc_old + pv_fn(v_ext, p)))
    return tuple(out)


def _softmax_init(cols):
    return jnp.full((1, cols), NEG_INF, F32), jnp.zeros((HEAD_DIM + ONES_ROWS, cols), F32)


def _softmax_finish(state):
    acc = state[1]
    return acc[0:HEAD_DIM] * (1.0 / jnp.maximum(acc[HEAD_DIM:HEAD_DIM + 1], 1e-30))


def _nsa_kernel(q_ref, gatet_ref, kc_ref, vct_ref, ks_ref, kw_ref, vst_ref, vwt_ref, ovl_ref,
                o_ref, out_t_ref, selb_ref):
    tq = ATT_T
    qi = pl.program_id(1)
    q0 = pl.multiple_of(qi * tq, tq)
    lo1 = _lane_lo((1, LANES))
    group_lanes = (lo1, jnp.logical_not(lo1))
    gate = jax.nn.sigmoid(gatet_ref[...])
    heads = range(NSA_HEADS)

    def q_head(h):
        j, g = h % NSA_REP, h // NSA_REP
        tile = q_ref[:, j * LANES:(j + 1) * LANES]
        return jnp.where(group_lanes[g], tile, jnp.zeros_like(tile))

    def v_rows(ref, h, k0, n):
        g = h // NSA_REP
        return ref[g * HEAD_DIM:(g + 1) * HEAD_DIM, pl.ds(k0, n)]

    def emit(h, branch, qs, nq, o_t):
        j, g = h % NSA_REP, h // NSA_REP
        r0 = j * LANES + g * HEAD_DIM
        row = h * N_BRANCH + branch
        val = gate[row:row + 1, qs:qs + nq] * o_t
        if branch == 0:
            out_t_ref[r0:r0 + HEAD_DIM, qs:qs + nq] = val
        else:
            out_t_ref[r0:r0 + HEAD_DIM, qs:qs + nq] += val

    qm = [q_head(h) for h in heads]

    n_cmp_pad = kc_ref.shape[1]
    kc = kc_ref[0]
    vct = vct_ref[0]
    n_i = lax.broadcasted_iota(jnp.int32, (n_cmp_pad, tq), 0)
    t_l = q0 + lax.broadcasted_iota(jnp.int32, (n_cmp_pad, tq), 1)
    cmask = (n_i * CMP_STRIDE + (CMP_LEN - 1)) <= t_l
    cmask_f = cmask.astype(F32)
    s_cs = [jnp.where(cmask, _nt(kc, qm[h]), NEG_INF) for h in heads]
    p_cs = [jnp.exp2(s_t - jnp.max(s_t, axis=0, keepdims=True)) * cmask_f for s_t in s_cs]
    p_cs = [p * (1.0 / jnp.maximum(jnp.sum(p, axis=0, keepdims=True), 1e-30)) for p in p_cs]
    for h in heads:
        g = h // NSA_REP
        emit(h, 0, 0, tq, _nn(vct[g * HEAD_DIM:(g + 1) * HEAD_DIM, :], p_cs[h].astype(BF16)))

    n_sel = ovl_ref.shape[0]
    needs_rank = q0 + tq > SEL_TOP_N * SEL_BLOCK

    @pl.when(jnp.logical_not(needs_rank))
    def _():
        selb_ref[...] = jnp.zeros_like(selb_ref)

    @pl.when(needs_rank)
    def _():
        m_i = lax.broadcasted_iota(jnp.int32, (n_sel, tq), 0)
        cur = jnp.right_shift(q0 + lax.broadcasted_iota(jnp.int32, (n_sel, tq), 1),
                              SEL_BLOCK.bit_length() - 1)
        valid = m_i <= cur
        forced = (m_i == 0) | (m_i == cur) | (m_i == cur - 1)
        ovl = ovl_ref[...]
        sub = SUBLANES
        m_loc = lax.broadcasted_iota(jnp.int32, (sub, tq), 0)
        for g in range(NSA_KV_GROUPS):
            psum = p_cs[g * NSA_REP]
            for h in range(g * NSA_REP + 1, (g + 1) * NSA_REP):
                psum = psum + p_cs[h]
            p_hi = psum.astype(BF16)
            p_lo = (psum - p_hi.astype(F32)).astype(BF16)
            imp = _nn(ovl, p_hi) + _nn(ovl, p_lo)
            imp = jnp.where(valid & forced, FORCE_SCORE, imp)
            imp = jnp.where(valid, imp, NEG_INF)
            parts = [imp[i:i + sub] for i in range(0, n_sel, sub)]
            ranks = [jnp.zeros((sub, tq), jnp.int32) for _ in parts]
            for mp in range(n_sel):
                row = imp[mp:mp + 1, :]
                for i, part in enumerate(parts):
                    if i * sub + sub - 1 <= mp:
                        beats = row > part
                    elif i * sub > mp:
                        beats = row >= part
                    else:
                        beats = (row > part) | ((row == part) & (m_loc + i * sub > mp))
                    ranks[i] = ranks[i] + beats.astype(jnp.int32)
            rank = jnp.concatenate(ranks, axis=0)
            selb_ref[g] = jnp.where(rank < SEL_TOP_N, 0.0, NEG_INF)

    blocks_per_chunk = tq // SEL_BLOCK

    def sel_bias(g, c, n_keys, qs, nq):
        rows = [jnp.broadcast_to(selb_ref[g, pl.ds(c * blocks_per_chunk + i, 1), qs:qs + nq], (SEL_BLOCK, nq))
                for i in range(n_keys // SEL_BLOCK)]
        return jnp.concatenate(rows, axis=0)

    key_i = lax.broadcasted_iota(jnp.int32, (tq, tq), 0)
    qry_i = lax.broadcasted_iota(jnp.int32, (tq, tq), 1)
    causal_mask = key_i <= qry_i

    def sel_chunk(c, k0, states, causal):
        k = ks_ref[pl.ds(k0, tq), :]
        bias = [sel_bias(g, c, tq, 0, tq) for g in range(NSA_KV_GROUPS)]

        def score_fn(h):
            if causal:
                return _tri_scores(k, qm[h], True, causal_mask, bias[h // NSA_REP])
            return _nt(k, qm[h]) + bias[h // NSA_REP]

        return _softmax_steps([functools.partial(score_fn, h) for h in heads],
                              [v_rows(vst_ref, h, k0, tq) for h in heads], states,
                              functools.partial(_tri_pv, True) if causal else _nn)

    states = sel_chunk(qi, q0, tuple(_softmax_init(tq) for _ in heads), True)
    states = lax.fori_loop(
        0, qi, lambda c, st: sel_chunk(c, pl.multiple_of(c * tq, tq), st, False), states)
    for h in heads:
        emit(h, 1, 0, tq, _softmax_finish(states[h]))

    def win_chunk(k0, states, mask, dead_upper):
        k = kw_ref[pl.ds(k0, tq), :]
        if dead_upper is None:
            score_fns = [lambda h=h: jnp.where(mask, _nt(k, qm[h]), NEG_INF) for h in heads]
            pv_fn = _nn
        else:
            score_fns = [lambda h=h: _tri_scores(k, qm[h], dead_upper, mask) for h in heads]
            pv_fn = functools.partial(_tri_pv, dead_upper)
        return _softmax_steps(score_fns, [v_rows(vwt_ref, h, k0, tq) for h in heads], states, pv_fn)

    states = win_chunk(q0, tuple(_softmax_init(tq) for _ in heads), causal_mask, True)
    for d in range(1, WINDOW // tq + 1):
        k0 = pl.multiple_of(jnp.maximum(qi - d, 0) * tq, tq)
        in_band = (key_i - d * tq > qry_i - WINDOW) & (qi >= d)
        states = win_chunk(k0, states, in_band, False if d * tq == WINDOW else None)
    for h in heads:
        emit(h, 2, 0, tq, _softmax_finish(states[h]))

    for j in range(NSA_REP):
        o_ref[:, j * LANES:(j + 1) * LANES] = out_t_ref[j * LANES:(j + 1) * LANES, :].T.astype(BF16)


def _nsa_attention(nsa, nsa_t, gate_t, kc, vct, ovl_t, batch, seq):
    tq = ATT_T
    nq = seq // tq
    n_blk = kc.shape[1]
    qrow = lambda b, i: (b * nq + i, 0)
    return pl.pallas_call(
        _nsa_kernel,
        grid=(batch, nq),
        in_specs=[
            pl.BlockSpec((tq, NSA_REP * LANES), qrow),
            pl.BlockSpec((GATE_ROWS, tq), lambda b, i: (0, b * nq + i)),
            pl.BlockSpec((1, n_blk, LANES), lambda b, i: (b, 0, 0)),
            pl.BlockSpec((1, LANES, n_blk), lambda b, i: (b, 0, 0)),
            pl.BlockSpec((seq, LANES), lambda b, i: (b, 3)),
            pl.BlockSpec((seq, LANES), lambda b, i: (b, 4)),
            pl.BlockSpec((LANES, seq), lambda b, i: (0, b)),
            pl.BlockSpec((LANES, seq), lambda b, i: (1, b)),
            pl.BlockSpec(ovl_t.shape, lambda b, i: (0, 0)),
        ],
        out_specs=pl.BlockSpec((tq, NSA_REP * LANES), qrow),
        out_shape=jax.ShapeDtypeStruct((batch * seq, NSA_REP * LANES), BF16),
        scratch_shapes=[pltpu.VMEM((NSA_REP * LANES, tq), F32),
                        pltpu.VMEM((NSA_KV_GROUPS, seq // SEL_BLOCK, tq), F32)],
        compiler_params=pltpu.CompilerParams(
            dimension_semantics=("parallel", "arbitrary"), vmem_limit_bytes=VMEM_LIMIT),
        name="nsa_attention",
    )(nsa, gate_t, kc, vct, nsa, nsa, nsa_t, nsa_t, ovl_t)


def _mla_attn_kernel(q_ref, k_ref, vt_ref, o_ref, out_t_ref):
    tq = ATT_T
    qi = pl.program_id(1)
    q0 = pl.multiple_of(qi * tq, tq)
    heads = range(MLA_HEADS)
    qh = [q_ref[:, h * LANES:(h + 1) * LANES] for h in heads]

    def chunk(k0, states, causal):
        if causal:
            mask = (lax.broadcasted_iota(jnp.int32, (tq, tq), 0)
                    <= lax.broadcasted_iota(jnp.int32, (tq, tq), 1))

        def score_fn(h):
            k = k_ref[pl.ds(k0, tq), h * LANES:(h + 1) * LANES]
            return _tri_scores(k, qh[h], True, mask) if causal else _nt(k, qh[h])

        v_ts = [vt_ref[h * MLA_V_DIM:(h + 1) * MLA_V_DIM, pl.ds(k0, tq)] for h in heads]
        return _softmax_steps([functools.partial(score_fn, h) for h in heads], v_ts, states,
                              functools.partial(_tri_pv, True) if causal else _nn)

    states = chunk(q0, tuple(_softmax_init(tq) for _ in heads), True)
    states = lax.fori_loop(0, qi, lambda c, st: chunk(pl.multiple_of(c * tq, tq), st, False), states)
    for h in heads:
        out_t_ref[h * MLA_V_DIM:(h + 1) * MLA_V_DIM, :] = _softmax_finish(states[h])
    out_t_ref[MLA_V_ROWS:, :] = jnp.zeros((RET_PAIRS * LANES - MLA_V_ROWS, tq), F32)
    for j in range(RET_PAIRS):
        o_ref[:, j * LANES:(j + 1) * LANES] = out_t_ref[j * LANES:(j + 1) * LANES, :].T.astype(BF16)


def _mla_attention(q, k, vt, batch, seq):
    tq = ATT_T
    nq = seq // tq
    return pl.pallas_call(
        _mla_attn_kernel,
        grid=(batch, nq),
        in_specs=[
            pl.BlockSpec((tq, MLA_HEADS * LANES), lambda b, i: (b * nq + i, 0)),
            pl.BlockSpec((seq, MLA_HEADS * LANES), lambda b, i: (b, 0)),
            pl.BlockSpec((MLA_V_ROWS, seq), lambda b, i: (0, b)),
        ],
        out_specs=pl.BlockSpec((tq, RET_PAIRS * LANES), lambda b, i: (b * nq + i, 0)),
        out_shape=jax.ShapeDtypeStruct((batch * seq, RET_PAIRS * LANES), BF16),
        scratch_shapes=[pltpu.VMEM((RET_PAIRS * LANES, tq), F32)],
        compiler_params=pltpu.CompilerParams(
            dimension_semantics=("parallel", "arbitrary"), vmem_limit_bytes=VMEM_LIMIT),
        name="mla_attention",
    )(q, k, vt)


RET_UNROLL = 16


def _retention_kernel(q_ref, k_ref, v_ref, gate_ref, gn_ref, intra_ref, rd_ref, wd_ref, cd_ref, o_ref,
                      kv_ref, st_ref):
    c_len = RET_CHUNK
    n_chunks = q_ref.shape[0] // c_len
    lo = _lane_lo((1, LANES))
    hi = jnp.logical_not(lo)
    blockdiag = (lax.broadcasted_iota(jnp.int32, (LANES, LANES), 0) < HALF) == _lane_lo((LANES, LANES))
    intra_a = intra_ref[0, 0]
    intra_b = intra_ref[0, 1]
    read_decay = rd_ref[0]
    write_decay = wd_ref[0]
    chunk_decay = cd_ref[0]
    gn = gn_ref[0]

    averager = jnp.where(blockdiag, 1.0 / HEAD_DIM, 0.0).astype(BF16)

    def half_mean(x):
        x_hi = x.astype(BF16)
        x_lo = (x - x_hi.astype(F32)).astype(BF16)
        return _nn(x_hi, averager) + _nn(x_lo, averager)

    def kv_body(c, carry):
        r0 = pl.multiple_of(c * c_len, c_len)
        kc = k_ref[pl.ds(r0, c_len), :]
        kv_ref[c] = _tn((kc.astype(F32) * write_decay).astype(BF16), v_ref[pl.ds(r0, c_len), :])
        return carry

    lax.fori_loop(0, n_chunks, kv_body, 0, unroll=RET_UNROLL)

    state = jnp.zeros((LANES, LANES), F32)
    for c in range(n_chunks):
        st_ref[c] = jnp.where(blockdiag, state, 0.0).astype(BF16)
        state = state * chunk_decay + kv_ref[c]

    def out_body(step, carry):
        cs = [step * RET_UNROLL + u for u in range(RET_UNROLL)]
        rows = [pl.ds(pl.multiple_of(c * c_len, c_len), c_len) for c in cs]
        qs = [q_ref[r, :] for r in rows]
        ks = [k_ref[r, :] for r in rows]
        vs = [v_ref[r, :] for r in rows]
        zero = jnp.zeros_like(qs[0])
        sa = [(_nt(jnp.where(lo, q, zero), k) * intra_a).astype(BF16) for q, k in zip(qs, ks)]
        sb = [(_nt(jnp.where(hi, q, zero), k) * intra_b).astype(BF16) for q, k in zip(qs, ks)]
        cross = [_nn(q, st_ref[c]) * read_decay for q, c in zip(qs, cs)]
        os_ = [jnp.where(lo, _nn(a, v), _nn(b, v)) + x for a, b, v, x in zip(sa, sb, vs, cross)]
        ds_ = [o - half_mean(o) for o in os_]
        ys = [d * lax.rsqrt(half_mean(d * d) + NORM_EPS) * gn for d in ds_]
        for r, y in zip(rows, ys):
            o_ref[r, :] = (jax.nn.silu(gate_ref[r, :]) * y).astype(BF16)
        return carry

    lax.fori_loop(0, n_chunks // RET_UNROLL, out_body, 0)


def _retention(ret, rgate, gn, intra, rd, wd, cd, batch, seq):
    pair_const3 = lambda b, j: (j, 0, 0)
    return pl.pallas_call(
        _retention_kernel,
        grid=(batch, RET_PAIRS),
        in_specs=[
            pl.BlockSpec((seq, LANES), lambda b, j: (b, j)),
            pl.BlockSpec((seq, LANES), lambda b, j: (b, RET_PAIRS + j)),
            pl.BlockSpec((seq, LANES), lambda b, j: (b, 2 * RET_PAIRS + j)),
            pl.BlockSpec((seq, LANES), lambda b, j: (b, j)),
            pl.BlockSpec((1, 1, LANES), pair_const3),
            pl.BlockSpec((1, 2, RET_CHUNK, RET_CHUNK), lambda b, j: (j, 0, 0, 0)),
            pl.BlockSpec((1, RET_CHUNK, LANES), pair_const3),
            pl.BlockSpec((1, RET_CHUNK, LANES), pair_const3),
            pl.BlockSpec((1, 1, LANES), pair_const3),
        ],
        out_specs=pl.BlockSpec((seq, LANES), lambda b, j: (b, j)),
        out_shape=jax.ShapeDtypeStruct((batch * seq, RET_PAIRS * LANES), BF16),
        scratch_shapes=[pltpu.VMEM((seq // RET_CHUNK, LANES, LANES), F32),
                        pltpu.VMEM((seq // RET_CHUNK, LANES, LANES), BF16)],
        compiler_params=pltpu.CompilerParams(
            dimension_semantics=("parallel", "arbitrary"), vmem_limit_bytes=VMEM_LIMIT),
        name="retention",
    )(ret, ret, ret, rgate, gn, intra, rd, wd, cd)


def _out_mlp_kernel(x_ref, nsa_ref, mla_ref, ret_ref, wo_ref, g2_ref, wu_ref, wd_ref, gf_ref, o_ref,
                    *, final_norm):
    full = 2 * LANES
    mixed = jnp.concatenate([nsa_ref[...], mla_ref[:, :full], mla_ref[:, full:] + ret_ref[:, full:],
                             ret_ref[:, :full]], axis=1)
    x = x_ref[...] + _nn(mixed, wo_ref[...])
    h = _rms(x, g2_ref[...]).astype(BF16)
    y = x
    for c in range(D_FF // MLP_FF_CHUNK):
        sl = slice(c * MLP_FF_CHUNK, (c + 1) * MLP_FF_CHUNK)
        u = jnp.maximum(_nn(h, wu_ref[:, sl]), 0.0)
        y = y + _nn((u * u).astype(BF16), wd_ref[sl, :])
    if final_norm:
        y = _rms(y, gf_ref[...])
    o_ref[...] = y


def _out_mlp(x2, o_nsa, o_mla, o_ret, wo, g2, wu, wd, layer, gf, final_norm):
    t = x2.shape[0]
    w = RET_PAIRS * LANES
    row = lambda i: (i, 0)
    const = lambda i: (0, 0)
    resident = dict(pipeline_mode=pl.Buffered(1))
    return pl.pallas_call(
        functools.partial(_out_mlp_kernel, final_norm=final_norm),
        grid=(t // MLP_TM,),
        in_specs=[
            pl.BlockSpec((MLP_TM, D_MODEL), row),
            pl.BlockSpec((MLP_TM, w), row),
            pl.BlockSpec((MLP_TM, w), row),
            pl.BlockSpec((MLP_TM, w), row),
            _layer_spec(wo, layer, **resident),
            pl.BlockSpec((1, D_MODEL), const),
            _layer_spec(wu, layer, **resident),
            _layer_spec(wd, layer, **resident),
            pl.BlockSpec((1, D_MODEL), const),
        ],
        out_specs=pl.BlockSpec((MLP_TM, D_MODEL), row),
        out_shape=jax.ShapeDtypeStruct((t, D_MODEL), F32),
        compiler_params=pltpu.CompilerParams(
            dimension_semantics=("parallel",), vmem_limit_bytes=VMEM_LIMIT),
        name="out_mlp",
    )(x2, o_nsa, o_mla, o_ret, wo, g2, wu, wd, gf)


ROPE_KINDS = (
    (PARTIAL_ROPE_DIM, ROPE_THETA, HEAD_DIM, 0),
    (MLA_ROPE_DIM, ROPE_THETA, LANES, HALF),
    (HEAD_DIM, RET_THETA, HEAD_DIM, 0),
)
ROPE_TM = 1024


def _rope_placement():
    n_angles = sum(dim // 2 for dim, _, _, _ in ROPE_KINDS)
    assert 2 * n_angles <= LANES
    place = np.zeros((LANES, 3 * LANES * len(ROPE_KINDS)), np.float32)
    fill = np.zeros((1, 3 * LANES * len(ROPE_KINDS)), np.float32)
    row0 = 0
    for kind, (dim, _, period, base) in enumerate(ROPE_KINDS):
        half = dim // 2
        col0 = kind * 3 * LANES
        for lane in range(LANES):
            rel = (lane - base) % period
            first = lane >= base and rel < half
            second = lane >= base and half <= rel < dim
            if first or second:
                angle = rel if first else rel - half
                place[row0 + angle, col0 + lane] = 1.0
                place[n_angles + row0 + angle, col0 + (1 if first else 2) * LANES + lane] = -1.0 if first else 1.0
            else:
                fill[0, col0 + lane] = 1.0
        row0 += half
    return jnp.asarray(place, BF16), jnp.asarray(fill)


def _rope_kernel(cs_ref, place_ref, fill_ref, *out_refs):
    x = cs_ref[...]
    x1 = x.astype(BF16)
    r1 = x - x1.astype(F32)
    x2 = r1.astype(BF16)
    x3 = (r1 - x2.astype(F32)).astype(BF16)
    place = place_ref[...]
    tab = _tn(x1, place) + _tn(x2, place) + _tn(x3, place) + fill_ref[...]
    for i, ref in enumerate(out_refs):
        ref[...] = tab[:, i * 3 * LANES:(i + 1) * 3 * LANES]


def _rope_tables(positions):
    inv = jnp.concatenate([1.0 / (theta ** (jnp.arange(0, dim, 2, dtype=F32) / dim))
                           for dim, theta, _, _ in ROPE_KINDS])
    ang = inv[:, None] * positions.reshape(-1).astype(F32)[None, :]
    compact = jnp.concatenate([jnp.cos(ang), jnp.sin(ang)], axis=0)
    compact = jnp.pad(compact, ((0, LANES - compact.shape[0]), (0, 0)))
    place, fill = _rope_placement()
    t = compact.shape[1]
    tab_shape = jax.ShapeDtypeStruct((t, 3 * LANES), F32)
    return pl.pallas_call(
        _rope_kernel,
        grid=(t // ROPE_TM,),
        in_specs=[pl.BlockSpec((LANES, ROPE_TM), lambda i: (0, i)),
                  pl.BlockSpec(place.shape, lambda i: (0, 0)),
                  pl.BlockSpec(fill.shape, lambda i: (0, 0))],
        out_specs=tuple(pl.BlockSpec((ROPE_TM, 3 * LANES), lambda i: (i, 0)) for _ in ROPE_KINDS),
        out_shape=tuple(tab_shape for _ in ROPE_KINDS),
        compiler_params=pltpu.CompilerParams(
            dimension_semantics=("parallel",), vmem_limit_bytes=VMEM_LIMIT),
        name="rope_tables",
    )(compact, place, fill)


def _in_weight(w_in):
    offs = np.cumsum((0,) + IN_SIZES)
    (nsa_q, k_cmp, v_cmp, k_slc, v_slc, k_win, v_win, gate,
     cq, ckv, kpe, ret_q, ret_k, ret_v, ret_g) = [np.arange(offs[i], offs[i + 1]) for i in range(len(IN_SIZES))]
    scale = HEAD_DIM ** -0.5
    head = lambda cols, h: cols[h * HEAD_DIM:(h + 1) * HEAD_DIM]
    pad = lambda n: None if n == 0 else -np.ones(n, np.int64)
    pieces = []
    for j in range(NSA_REP):
        pieces += [(head(nsa_q, j), scale), (head(nsa_q, j + NSA_REP), scale)]
    pieces += [(c, 1.0) for c in (k_slc, k_win, k_cmp, v_cmp, cq, ckv)]
    pieces += [(pad(HALF), 0.0), (kpe, 1.0), (pad(LANES - HALF - MLA_ROPE_DIM), 0.0)]
    slots = lambda cols, f, fill: [fill if h is None else (head(cols, h), f) for h in RET_SLOTS]
    zero = (pad(HEAD_DIM), 0.0)
    pieces += slots(ret_q, 1.0, (head(ret_k, RET_SLOTS[-1]), scale))
    pieces += slots(ret_k, scale, zero)[:2 * RET_K_TILES]
    pieces += slots(ret_v, 1.0, zero) + slots(ret_g, 1.0, zero)
    t_pieces = [(v_slc, 1.0), (v_win, 1.0), (gate, 1.0), (pad(GATE_ROWS - NSA_GATE_W), 0.0)]

    def gather(parts):
        src = np.concatenate([c for c, _ in parts])
        factor = np.concatenate([np.full(len(c), f if (c >= 0).all() else 0.0, np.float32) for c, f in parts])
        return (jnp.take(w_in, jnp.asarray(np.maximum(src, 0), jnp.int32), axis=2) * factor).astype(BF16)

    w_main = gather(pieces)
    assert w_main.shape[-1] == N_PAD
    return w_main, gather(t_pieces).transpose(0, 2, 1)


def _compress_weights(pos, w1, w2):
    nl = pos.shape[0]
    g, dh, hid = NSA_KV_GROUPS, HEAD_DIM, CMP_HIDDEN
    assert g == 2
    p = jnp.tile(pos, (1, 1, g))

    def block_diag2(w):
        lead = ((0, 0),) * (w.ndim - 1)
        return jnp.concatenate([jnp.pad(w, lead + ((0, w.shape[-1]),)),
                                jnp.pad(w, lead + ((w.shape[-1], 0),))], axis=-2)

    w = block_diag2(w1.astype(BF16).reshape(nl, CMP_LEN, dh, hid))
    return p, w, block_diag2(w2.astype(BF16))


def _mla_weights(w_uq, w_ukv):
    nl = w_uq.shape[0]
    dq = MLA_NOPE_DIM + MLA_ROPE_DIM
    wq = w_uq.reshape(nl, MLA_Q_RANK, MLA_HEADS, dq)
    pe = wq[..., MLA_NOPE_DIM:]
    half = MLA_ROPE_DIM // 2
    rot = jnp.concatenate([jnp.zeros_like(wq[..., :MLA_NOPE_DIM]), -pe[..., half:], pe[..., :half]], axis=-1)
    pad = lambda w: jnp.pad(w, ((0, 0), (0, 0), (0, 0), (0, LANES - dq))).reshape(
        nl, MLA_Q_RANK, MLA_HEADS * LANES)
    wq = jnp.concatenate([pad(wq), pad(rot)], axis=-1)
    wkv = w_ukv.reshape(nl, MLA_KV_RANK, MLA_HEADS, MLA_NOPE_DIM + MLA_V_DIM)
    wk = jnp.pad(wkv[..., :MLA_NOPE_DIM], ((0, 0), (0, 0), (0, 0), (0, LANES - MLA_NOPE_DIM)))
    wk = wk.reshape(nl, MLA_KV_RANK, MLA_HEADS * LANES)
    wvt = wkv[..., MLA_NOPE_DIM:].reshape(nl, MLA_KV_RANK, MLA_V_ROWS).transpose(0, 2, 1)
    return wq.astype(BF16), wk.astype(BF16), wvt.astype(BF16)


def _out_weight(w_out):
    nl = w_out.shape[0]
    nsa = w_out[:, :NSA_Q_W].reshape(nl, NSA_HEADS, HEAD_DIM, D_MODEL)
    order = [h for j in range(NSA_REP) for h in (j, j + NSA_REP)]
    nsa = nsa[:, order].reshape(nl, NSA_Q_W, D_MODEL)
    mla = w_out[:, NSA_Q_W:NSA_Q_W + MLA_HEADS * MLA_V_DIM]
    ret = w_out[:, NSA_Q_W + MLA_HEADS * MLA_V_DIM:]
    full = 2 * LANES
    return jnp.concatenate([nsa, mla[:, :full], mla[:, full:], ret[:, full:], ret[:, :full]],
                           axis=1).astype(BF16)


def _retention_tables(gn_gain):
    nh = 2 * RET_PAIRS
    slot_head = np.array([RET_HEADS if h is None else h for h in RET_SLOTS])
    log_g = jnp.log(1.0 - 2.0 ** (-5.0 - jnp.asarray(slot_head, F32)))
    i = jnp.arange(RET_CHUNK, dtype=F32)
    diff = i[:, None] - i[None, :]
    intra = jnp.where(diff >= 0, jnp.exp(jnp.maximum(diff, 0.0)[None] * log_g[:, None, None]), 0.0)
    read_decay = jnp.exp((i + 1.0)[None, :] * log_g[:, None])
    write_decay = jnp.exp((RET_CHUNK - 1.0 - i)[None, :] * log_g[:, None])
    chunk_decay = jnp.exp(RET_CHUNK * log_g)

    def lanes(t):
        t = t.reshape(RET_PAIRS, 2, -1)
        return jnp.repeat(t.transpose(0, 2, 1), HALF, axis=-1)

    gn = jnp.pad(gn_gain, ((0, 0), (0, nh - RET_HEADS), (0, 0)))[:, slot_head]
    gn = gn.reshape(gn.shape[0], RET_PAIRS, 1, LANES)
    return (gn, intra.reshape(RET_PAIRS, 2, RET_CHUNK, RET_CHUNK), lanes(read_decay), lanes(write_decay),
            lanes(chunk_decay[:, None]))


def _selection_overlap(seq):
    n_cmp = (seq - CMP_LEN) // CMP_STRIDE + 1
    n_sel = seq // SEL_BLOCK
    cs = np.arange(n_cmp) * CMP_STRIDE
    ss = np.arange(n_sel) * SEL_BLOCK
    ov = np.clip(np.minimum(cs[:, None] + CMP_LEN, ss[None, :] + SEL_BLOCK)
                 - np.maximum(cs[:, None], ss[None, :]), 0, None) / CMP_LEN
    ovl_t = np.zeros((n_sel, seq // CMP_STRIDE), np.float32)
    ovl_t[:, :n_cmp] = ov.T
    return jnp.asarray(ovl_t, BF16)


def kernel(x, positions, ln1_gain, w_in, cmp_pos_k, cmp_w1_k, cmp_w2_k, cmp_pos_v, cmp_w1_v, cmp_w2_v,
           mla_q_norm, mla_w_uq, mla_kv_norm, mla_w_ukv, ret_gn_gain, w_out, ln2_gain, w_up, w_down,
           final_gain):
    batch, seq, _ = x.shape
    depth = w_in.shape[0]
    t = batch * seq

    tab_n, tab_m, tab_r = _rope_tables(positions)

    w_in_p, w_in_t = _in_weight(w_in)
    pos_k, w1_k, w2_k = _compress_weights(cmp_pos_k, cmp_w1_k, cmp_w2_k)
    pos_v, w1_v, w2_v = _compress_weights(cmp_pos_v, cmp_w1_v, cmp_w2_v)
    cmp_pos = jnp.stack([pos_k, pos_v], axis=1)
    cmp_w1 = jnp.stack([w1_k, w1_v], axis=1)
    w2_vt = w2_v.transpose(0, 2, 1)
    wq, wk, wvt = _mla_weights(mla_w_uq, mla_w_ukv)
    wo = _out_weight(w_out)
    wu = w_up.astype(BF16)
    wd = w_down.astype(BF16)
    gn, intra, rd, wdec, cd = _retention_tables(ret_gn_gain)
    ovl_t = _selection_overlap(seq)
    gf = final_gain.reshape(1, D_MODEL)

    x2 = x.reshape(t, D_MODEL)
    for l in range(depth):
        nsa, nsa_t, gate_t, k_cmp, v_cmp, ret, rgate, q_m, k_m, vt_m = _in_proj(
            x2, ln1_gain[l].reshape(1, D_MODEL), w_in_p, w_in_t, mla_q_norm[l].reshape(1, -1),
            mla_kv_norm[l].reshape(1, -1), wq, wk, wvt, l, tab_n, tab_m, tab_r)
        kc, vct = _compress(k_cmp, v_cmp, cmp_pos[l], cmp_w1, w2_k[l], w2_vt[l], l, batch, seq)
        o_nsa = _nsa_attention(nsa, nsa_t, gate_t, kc, vct, ovl_t, batch, seq)
        o_mla = _mla_attention(q_m, k_m, vt_m, batch, seq)
        o_ret = _retention(ret, rgate, gn[l], intra, rd, wdec, cd, batch, seq)
        x2 = _out_mlp(x2, o_nsa, o_mla, o_ret, wo, ln2_gain[l].reshape(1, D_MODEL), wu, wd, l, gf,
                      final_norm=(l == depth - 1))
    return x2.reshape(batch, seq, D_MODEL)
```

```python
import functools
import math

import numpy as np
import jax
import jax.numpy as jnp
from jax import lax
from jax.experimental import pallas as pl
from jax.experimental.pallas import tpu as pltpu

F32 = jnp.float32
BF16 = jnp.bfloat16

D_MODEL = 1024
HEAD_DIM = 64
NSA_HEADS = 6
NSA_KV_GROUPS = 2
NSA_REP = NSA_HEADS // NSA_KV_GROUPS
N_BRANCH = 3
CMP_LEN = 32
CMP_STRIDE = 16
CMP_HIDDEN = 2 * HEAD_DIM
SEL_BLOCK = 64
SEL_TOP_N = 16
WINDOW = 512
MLA_HEADS = 5
MLA_Q_RANK = 256
MLA_KV_RANK = 128
MLA_NOPE_DIM = 64
MLA_ROPE_DIM = 32
MLA_V_DIM = 64
RET_HEADS = 5
RET_CHUNK = 128
ROPE_THETA = 500000.0
PARTIAL_ROPE_DIM = HEAD_DIM // 4
RET_THETA = 10000.0
D_FF = 4 * D_MODEL
NORM_EPS = 1e-6
NEG_INF = -1e30
FORCE_SCORE = 1e9
LOG2E = math.log2(math.e)

NSA_Q_W = NSA_HEADS * HEAD_DIM
NSA_KV_W = NSA_KV_GROUPS * HEAD_DIM
NSA_GATE_W = NSA_HEADS * N_BRANCH
RET_W = RET_HEADS * HEAD_DIM
IN_SIZES = (NSA_Q_W, NSA_KV_W, NSA_KV_W, NSA_KV_W, NSA_KV_W, NSA_KV_W, NSA_KV_W, NSA_GATE_W,
            MLA_Q_RANK, MLA_KV_RANK, MLA_ROPE_DIM, RET_W, RET_W, RET_W, RET_W)

LANES = 128
HALF = LANES // 2
SUBLANES = 8
MXU_WIDTH = 256
VMEM_LIMIT = 56 * 1024 * 1024

NSA_TILES = 5
GATE_ROWS = 32
NSA_T_ROWS = 2 * LANES + GATE_ROWS
MLA_IN_TILES = 4
RET_PAIRS = 3
RET_SLOTS = (0, 1, 2, 3, None, 4)
RET_K_TILES = RET_PAIRS - 1
RET_ROPE_TILES = RET_PAIRS + RET_K_TILES
RET_MM_TILES = RET_ROPE_TILES + RET_PAIRS
RET_TILES = 3 * RET_PAIRS
IN_TILES = NSA_TILES + 2 + MLA_IN_TILES + RET_MM_TILES + RET_PAIRS
N_PAD = IN_TILES * LANES
MLA_V_ROWS = MLA_HEADS * MLA_V_DIM

IN_TM = 512
IN_CHUNK_TILES = 4 * MXU_WIDTH // LANES
MLP_TM = 1024
ATT_T = 512
MLP_FF_CHUNK = 512


def _nn(a, b):
    return jnp.dot(a, b, preferred_element_type=F32)


def _nt(a, b):
    return lax.dot_general(a, b, (((1,), (1,)), ((), ())), preferred_element_type=F32)


def _tn(a, b):
    return lax.dot_general(a, b, (((0,), (0,)), ((), ())), preferred_element_type=F32)


def _rms(x, gain):
    return x * lax.rsqrt(jnp.mean(x * x, axis=-1, keepdims=True) + NORM_EPS) * gain


def _rope(val, tab, half):
    cos = tab[:, 0:LANES]
    sin_a = tab[:, LANES:2 * LANES]
    sin_b = tab[:, 2 * LANES:3 * LANES]
    return (val * cos + pltpu.roll(val, LANES - half, 1) * sin_a
            + pltpu.roll(val, half, 1) * sin_b)


def _lane_lo(shape):
    return lax.broadcasted_iota(jnp.int32, shape, len(shape) - 1) < HALF


def _mla_up(c, qn_ref, kvn_ref, wq_ref, wk_ref, wvt_ref, tm_ref, q_ref, k_ref, vt_ref):
    scale = (MLA_NOPE_DIM + MLA_ROPE_DIM) ** -0.5 * LOG2E
    cq = _rms(c[:, 0:MLA_Q_RANK], qn_ref[...]).astype(BF16)
    ckv = _rms(c[:, MLA_Q_RANK:MLA_Q_RANK + MLA_KV_RANK], kvn_ref[...]).astype(BF16)
    k_pe = _rope(c[:, 3 * LANES:4 * LANES], tm_ref[...], MLA_ROPE_DIM // 2)
    cos = tm_ref[:, 0:LANES] * scale
    sin = (tm_ref[:, 2 * LANES:3 * LANES] - tm_ref[:, LANES:2 * LANES]) * scale
    q = _nn(cq, wq_ref[...])
    k = _nn(ckv, wk_ref[...])
    w = MLA_HEADS * LANES
    for hd in range(MLA_HEADS):
        sl = slice(hd * LANES, (hd + 1) * LANES)
        rot = slice(w + hd * LANES, w + (hd + 1) * LANES)
        q_ref[:, sl] = (q[:, sl] * cos + q[:, rot] * sin).astype(BF16)
        k_ref[:, sl] = (k[:, sl] + k_pe).astype(BF16)
    vt_ref[...] = _nt(wvt_ref[...], ckv).astype(BF16)


def _inproj_kernel(x_ref, g_ref, w_ref, wt_ref, tn_ref, tm_ref, tr_ref, qn_ref, kvn_ref, wq_ref, wk_ref,
                   wvt_ref, nsa_ref, nsat_ref, gatet_ref, kcmp_ref, vcmp_ref, ret_ref, rgate_ref,
                   mq_ref, mk_ref, mvt_ref):
    h = _rms(x_ref[...], g_ref[...]).astype(BF16)
    tab_n = tn_ref[...]
    tab_r = tr_ref[...]

    chunks = [_nn(h, w_ref[:, c:min(c + IN_CHUNK_TILES * LANES, N_PAD)])
              for c in range(0, N_PAD, IN_CHUNK_TILES * LANES)]

    def tiles(first, n):
        cols = [chunks[i // IN_CHUNK_TILES][:, (i % IN_CHUNK_TILES) * LANES:(i % IN_CHUNK_TILES + 1) * LANES]
                for i in range(first, first + n)]
        return cols[0] if n == 1 else jnp.concatenate(cols, axis=1)

    t0 = 0
    for i in range(NSA_TILES):
        v = _rope(tiles(t0 + i, 1), tab_n, PARTIAL_ROPE_DIM // 2)
        if i < NSA_REP:
            v = v * LOG2E
        nsa_ref[:, i * LANES:(i + 1) * LANES] = v.astype(BF16)
    t0 += NSA_TILES
    kcmp_ref[...] = _rope(tiles(t0, 1), tab_n, PARTIAL_ROPE_DIM // 2)
    vcmp_ref[...] = tiles(t0 + 1, 1)
    t0 += 2
    _mla_up(tiles(t0, MLA_IN_TILES), qn_ref, kvn_ref, wq_ref, wk_ref, wvt_ref, tm_ref, mq_ref, mk_ref, mvt_ref)
    t0 += MLA_IN_TILES
    stored = []
    for i in range(RET_MM_TILES):
        v = tiles(t0 + i, 1)
        if i < RET_ROPE_TILES:
            v = _rope(v, tab_r, HEAD_DIM // 2)
        stored.append(v)
        if i == RET_ROPE_TILES - 1:
            stored.append(pltpu.roll(stored[RET_PAIRS - 1], HALF, 1))
    for i, v in enumerate(stored):
        ret_ref[:, i * LANES:(i + 1) * LANES] = v.astype(BF16)
    t0 += RET_MM_TILES
    rgate_ref[...] = tiles(t0, RET_PAIRS)
    at = _nt(wt_ref[...], h)
    nsat_ref[...] = at[0:2 * LANES, :].astype(BF16)
    gatet_ref[...] = at[2 * LANES:NSA_T_ROWS, :]


def _layer_spec(w, layer, **kwargs):
    zeros = (0,) * (w.ndim - 1)
    return pl.BlockSpec((None,) + w.shape[1:], lambda *_: (layer,) + zeros, **kwargs)


def _in_proj(x2, gain, w, wt, q_norm, kv_norm, wq, wk, wvt, layer, tab_n, tab_m, tab_r):
    t = x2.shape[0]
    row = lambda i: (i, 0)
    col = lambda i: (0, i)
    const = lambda i: (0, 0)
    out_shapes = (
        jax.ShapeDtypeStruct((t, NSA_TILES * LANES), BF16),
        jax.ShapeDtypeStruct((2 * LANES, t), BF16),
        jax.ShapeDtypeStruct((GATE_ROWS, t), F32),
        jax.ShapeDtypeStruct((t, LANES), F32),
        jax.ShapeDtypeStruct((t, LANES), F32),
        jax.ShapeDtypeStruct((t, RET_TILES * LANES), BF16),
        jax.ShapeDtypeStruct((t, RET_PAIRS * LANES), F32),
        jax.ShapeDtypeStruct((t, MLA_HEADS * LANES), BF16),
        jax.ShapeDtypeStruct((t, MLA_HEADS * LANES), BF16),
        jax.ShapeDtypeStruct((MLA_V_ROWS, t), BF16),
    )
    out_specs = tuple(
        pl.BlockSpec((s.shape[0], IN_TM), col) if s.shape[1] == t else pl.BlockSpec((IN_TM, s.shape[1]), row)
        for s in out_shapes)
    return pl.pallas_call(
        _inproj_kernel,
        grid=(t // IN_TM,),
        in_specs=[
            pl.BlockSpec((IN_TM, D_MODEL), row),
            pl.BlockSpec((1, D_MODEL), const),
            _layer_spec(w, layer),
            _layer_spec(wt, layer),
            pl.BlockSpec((IN_TM, 3 * LANES), row),
            pl.BlockSpec((IN_TM, 3 * LANES), row),
            pl.BlockSpec((IN_TM, 3 * LANES), row),
            pl.BlockSpec((1, MLA_Q_RANK), const),
            pl.BlockSpec((1, MLA_KV_RANK), const),
            _layer_spec(wq, layer),
            _layer_spec(wk, layer),
            _layer_spec(wvt, layer),
        ],
        out_specs=out_specs,
        out_shape=out_shapes,
        compiler_params=pltpu.CompilerParams(
            dimension_semantics=("parallel",), vmem_limit_bytes=VMEM_LIMIT),
        name="in_proj",
    )(x2, gain, w, wt, tab_n, tab_m, tab_r, q_norm, kv_norm, wq, wk, wvt)


def _compress_kernel(k_ref, v_ref, pos_ref, w1_ref, w2k_ref, w2vt_ref, kc_ref, vct_ref):
    n_blk = k_ref.shape[0] // CMP_STRIDE

    def hidden(src, i):
        toks = [src[pl.ds(r, n_blk, stride=CMP_STRIDE), :] for r in range(CMP_STRIDE)]
        halves = []
        for half in range(CMP_LEN // CMP_STRIDE):
            off = half * CMP_STRIDE
            x = jnp.concatenate([(toks[r] + pos_ref[i, off + r:off + r + 1, :]).astype(BF16)
                                 for r in range(CMP_STRIDE)], axis=1)
            w = w1_ref[i, off:off + CMP_STRIDE].reshape(CMP_STRIDE * LANES, NSA_KV_GROUPS * CMP_HIDDEN)
            halves.append(_nn(x, w))
        return jax.nn.gelu(halves[0] + pltpu.roll(halves[1], n_blk - 1, 0)).astype(BF16)

    kc_ref[0] = _nn(hidden(k_ref, 0), w2k_ref[...]).astype(BF16)
    vct_ref[0] = _nt(w2vt_ref[...], hidden(v_ref, 1)).astype(BF16)


def _compress(k_cmp, v_cmp, pos, w1, w2k, w2vt, layer, batch, seq):
    b = batch
    n_blk = seq // CMP_STRIDE
    return pl.pallas_call(
        _compress_kernel,
        grid=(b,),
        in_specs=[
            pl.BlockSpec((seq, LANES), lambda i: (i, 0)),
            pl.BlockSpec((seq, LANES), lambda i: (i, 0)),
            pl.BlockSpec(pos.shape, lambda i: (0, 0, 0)),
            _layer_spec(w1, layer),
            pl.BlockSpec(w2k.shape, lambda i: (0, 0)),
            pl.BlockSpec(w2vt.shape, lambda i: (0, 0)),
        ],
        out_specs=(pl.BlockSpec((1, n_blk, LANES), lambda i: (i, 0, 0)),
                   pl.BlockSpec((1, LANES, n_blk), lambda i: (i, 0, 0))),
        out_shape=(jax.ShapeDtypeStruct((b, n_blk, LANES), BF16),
                   jax.ShapeDtypeStruct((b, LANES, n_blk), BF16)),
        compiler_params=pltpu.CompilerParams(
            dimension_semantics=("parallel",), vmem_limit_bytes=VMEM_LIMIT),
        name="nsa_compress",
    )(k_cmp, v_cmp, pos, w1, w2k, w2vt)


SCORE_LOOKAHEAD = 2
ONES_ROWS = 16


def _tri_scores(k, q, dead_upper, mask, bias=None):
    half = k.shape[0] // 2
    dead = jnp.full((half, half), NEG_INF, F32)
    if dead_upper:
        top = _nt(k[:half], q)
        bottom = jnp.concatenate([dead, _nt(k[half:], q[half:])], axis=1)
    else:
        top = jnp.concatenate([_nt(k[:half], q[:half]), dead], axis=1)
        bottom = _nt(k[half:], q)
    s_t = jnp.concatenate([top, bottom], axis=0)
    return jnp.where(mask, s_t if bias is None else s_t + bias, NEG_INF)


def _tri_pv(dead_upper, v_ext, p):
    half = p.shape[0] // 2
    if dead_upper:
        full = _nn(v_ext[:, :half], p[:half])
        part = _nn(v_ext[:, half:], p[half:, half:])
        return jnp.concatenate([full[:, :half], full[:, half:] + part], axis=1)
    part = _nn(v_ext[:, :half], p[:half, :half])
    full = _nn(v_ext[:, half:], p[half:])
    return jnp.concatenate([full[:, :half] + part, full[:, half:]], axis=1)


def _softmax_steps(score_fns, v_ts, states, pv_fn=_nn):
    n = len(score_fns)
    s_ts = [score_fns[h]() if h < SCORE_LOOKAHEAD else None for h in range(n)]
    out = []
    for h in range(n):
        if h + SCORE_LOOKAHEAD < n:
            s_ts[h + SCORE_LOOKAHEAD] = score_fns[h + SCORE_LOOKAHEAD]()
        m_old, acc_old = states[h]
        m = jnp.maximum(m_old, jnp.max(s_ts[h], axis=0, keepdims=True))
        p = jnp.exp2(s_ts[h] - m).astype(BF16)
        s_ts[h] = None
        v_ext = jnp.concatenate([v_ts[h], jnp.ones((ONES_ROWS, v_ts[h].shape[1]), BF16)], axis=0)
        out.append((m, jnp.exp2(m_old - m) * acc_old + pv_fn(v_ext, p)))
    return tuple(out)


def _softmax_init(cols):
    return jnp.full((1, cols), NEG_INF, F32), jnp.zeros((HEAD_DIM + ONES_ROWS, cols), F32)


def _softmax_finish(state):
    acc = state[1]
    return acc[0:HEAD_DIM] * (1.0 / jnp.maximum(acc[HEAD_DIM:HEAD_DIM + 1], 1e-30))


def _nsa_kernel(q_ref, gatet_ref, kc_ref, vct_ref, ks_ref, kw_ref, vst_ref, vwt_ref, ovl_ref,
                o_ref, out_t_ref, selb_ref):
    tq = ATT_T
    qi = pl.program_id(1)
    q0 = pl.multiple_of(qi * tq, tq)
    lo1 = _lane_lo((1, LANES))
    group_lanes = (lo1, jnp.logical_not(lo1))
    gate = jax.nn.sigmoid(gatet_ref[...])
    heads = range(NSA_HEADS)

    def q_head(h):
        j, g = h % NSA_REP, h // NSA_REP
        tile = q_ref[:, j * LANES:(j + 1) * LANES]
        return jnp.where(group_lanes[g], tile, jnp.zeros_like(tile))

    def v_rows(ref, h, k0, n):
        g = h // NSA_REP
        return ref[g * HEAD_DIM:(g + 1) * HEAD_DIM, pl.ds(k0, n)]

    def emit(h, branch, qs, nq, o_t):
        j, g = h % NSA_REP, h // NSA_REP
        r0 = j * LANES + g * HEAD_DIM
        row = h * N_BRANCH + branch
        val = gate[row:row + 1, qs:qs + nq] * o_t
        if branch == 0:
            out_t_ref[r0:r0 + HEAD_DIM, qs:qs + nq] = val
        else:
            out_t_ref[r0:r0 + HEAD_DIM, qs:qs + nq] += val

    qm = [q_head(h) for h in heads]

    n_cmp_pad = kc_ref.shape[1]
    kc = kc_ref[0]
    vct = vct_ref[0]
    n_i = lax.broadcasted_iota(jnp.int32, (n_cmp_pad, tq), 0)
    t_l = q0 + lax.broadcasted_iota(jnp.int32, (n_cmp_pad, tq), 1)
    cmask = (n_i * CMP_STRIDE + (CMP_LEN - 1)) <= t_l
    cmask_f = cmask.astype(F32)
    s_cs = [jnp.where(cmask, _nt(kc, qm[h]), NEG_INF) for h in heads]
    p_cs = [jnp.exp2(s_t - jnp.max(s_t, axis=0, keepdims=True)) * cmask_f for s_t in s_cs]
    p_cs = [p * (1.0 / jnp.maximum(jnp.sum(p, axis=0, keepdims=True), 1e-30)) for p in p_cs]
    for h in heads:
        g = h // NSA_REP
        emit(h, 0, 0, tq, _nn(vct[g * HEAD_DIM:(g + 1) * HEAD_DIM, :], p_cs[h].astype(BF16)))

    n_sel = ovl_ref.shape[0]
    needs_rank = q0 + tq > SEL_TOP_N * SEL_BLOCK

    @pl.when(jnp.logical_not(needs_rank))
    def _():
        selb_ref[...] = jnp.zeros_like(selb_ref)

    @pl.when(needs_rank)
    def _():
        m_i = lax.broadcasted_iota(jnp.int32, (n_sel, tq), 0)
        cur = jnp.right_shift(q0 + lax.broadcasted_iota(jnp.int32, (n_sel, tq), 1),
                              SEL_BLOCK.bit_length() - 1)
        valid = m_i <= cur
        forced = (m_i == 0) | (m_i == cur) | (m_i == cur - 1)
        ovl = ovl_ref[...]
        sub = SUBLANES
        m_loc = lax.broadcasted_iota(jnp.int32, (sub, tq), 0)
        for g in range(NSA_KV_GROUPS):
            psum = p_cs[g * NSA_REP]
            for h in range(g * NSA_REP + 1, (g + 1) * NSA_REP):
                psum = psum + p_cs[h]
            p_hi = psum.astype(BF16)
            p_lo = (psum - p_hi.astype(F32)).astype(BF16)
            imp = _nn(ovl, p_hi) + _nn(ovl, p_lo)
            imp = jnp.where(valid & forced, FORCE_SCORE, imp)
            imp = jnp.where(valid, imp, NEG_INF)
            parts = [imp[i:i + sub] for i in range(0, n_sel, sub)]
            ranks = [jnp.zeros((sub, tq), jnp.int32) for _ in parts]
            for mp in range(n_sel):
                row = imp[mp:mp + 1, :]
                for i, part in enumerate(parts):
                    if i * sub + sub - 1 <= mp:
                        beats = row > part
                    elif i * sub > mp:
                        beats = row >= part
                    else:
                        beats = (row > part) | ((row == part) & (m_loc + i * sub > mp))
                    ranks[i] = ranks[i] + beats.astype(jnp.int32)
            rank = jnp.concatenate(ranks, axis=0)
            selb_ref[g] = jnp.where(rank < SEL_TOP_N, 0.0, NEG_INF)

    blocks_per_chunk = tq // SEL_BLOCK

    def sel_bias(g, c, n_keys, qs, nq):
        rows = [jnp.broadcast_to(selb_ref[g, pl.ds(c * blocks_per_chunk + i, 1), qs:qs + nq], (SEL_BLOCK, nq))
                for i in range(n_keys // SEL_BLOCK)]
        return jnp.concatenate(rows, axis=0)

    key_i = lax.broadcasted_iota(jnp.int32, (tq, tq), 0)
    qry_i = lax.broadcasted_iota(jnp.int32, (tq, tq), 1)
    causal_mask = key_i <= qry_i

    def sel_chunk(c, k0, states, causal):
        k = ks_ref[pl.ds(k0, tq), :]
        bias = [sel_bias(g, c, tq, 0, tq) for g in range(NSA_KV_GROUPS)]

        def score_fn(h):
            if causal:
                return _tri_scores(k, qm[h], True, causal_mask, bias[h // NSA_REP])
            return _nt(k, qm[h]) + bias[h // NSA_REP]

        return _softmax_steps([functools.partial(score_fn, h) for h in heads],
                              [v_rows(vst_ref, h, k0, tq) for h in heads], states,
                              functools.partial(_tri_pv, True) if causal else _nn)

    states = sel_chunk(qi, q0, tuple(_softmax_init(tq) for _ in heads), True)
    states = lax.fori_loop(
        0, qi, lambda c, st: sel_chunk(c, pl.multiple_of(c * tq, tq), st, False), states)
    for h in heads:
        emit(h, 1, 0, tq, _softmax_finish(states[h]))

    def win_chunk(k0, states, mask, dead_upper):
        k = kw_ref[pl.ds(k0, tq), :]
        if dead_upper is None:
            score_fns = [lambda h=h: jnp.where(mask, _nt(k, qm[h]), NEG_INF) for h in heads]
            pv_fn = _nn
        else:
            score_fns = [lambda h=h: _tri_scores(k, qm[h], dead_upper, mask) for h in heads]
            pv_fn = functools.partial(_tri_pv, dead_upper)
        return _softmax_steps(score_fns, [v_rows(vwt_ref, h, k0, tq) for h in heads], states, pv_fn)

    states = win_chunk(q0, tuple(_softmax_init(tq) for _ in heads), causal_mask, True)
    for d in range(1, WINDOW // tq + 1):
        k0 = pl.multiple_of(jnp.maximum(qi - d, 0) * tq, tq)
        in_band = (key_i - d * tq > qry_i - WINDOW) & (qi >= d)
        states = win_chunk(k0, states, in_band, False if d * tq == WINDOW else None)
    for h in heads:
        emit(h, 2, 0, tq, _softmax_finish(states[h]))

    for j in range(NSA_REP):
        o_ref[:, j * LANES:(j + 1) * LANES] = out_t_ref[j * LANES:(j + 1) * LANES, :].T.astype(BF16)


def _nsa_attention(nsa, nsa_t, gate_t, kc, vct, ovl_t, batch, seq):
    tq = ATT_T
    nq = seq // tq
    n_blk = kc.shape[1]
    qrow = lambda b, i: (b * nq + i, 0)
    return pl.pallas_call(
        _nsa_kernel,
        grid=(batch, nq),
        in_specs=[
            pl.BlockSpec((tq, NSA_REP * LANES), qrow),
            pl.BlockSpec((GATE_ROWS, tq), lambda b, i: (0, b * nq + i)),
            pl.BlockSpec((1, n_blk, LANES), lambda b, i: (b, 0, 0)),
            pl.BlockSpec((1, LANES, n_blk), lambda b, i: (b, 0, 0)),
            pl.BlockSpec((seq, LANES), lambda b, i: (b, 3)),
            pl.BlockSpec((seq, LANES), lambda b, i: (b, 4)),
            pl.BlockSpec((LANES, seq), lambda b, i: (0, b)),
            pl.BlockSpec((LANES, seq), lambda b, i: (1, b)),
            pl.BlockSpec(ovl_t.shape, lambda b, i: (0, 0)),
        ],
        out_specs=pl.BlockSpec((tq, NSA_REP * LANES), qrow),
        out_shape=jax.ShapeDtypeStruct((batch * seq, NSA_REP * LANES), BF16),
        scratch_shapes=[pltpu.VMEM((NSA_REP * LANES, tq), F32),
                        pltpu.VMEM((NSA_KV_GROUPS, seq // SEL_BLOCK, tq), F32)],
        compiler_params=pltpu.CompilerParams(
            dimension_semantics=("parallel", "arbitrary"), vmem_limit_bytes=VMEM_LIMIT),
        name="nsa_attention",
    )(nsa, gate_t, kc, vct, nsa, nsa, nsa_t, nsa_t, ovl_t)


def _mla_attn_kernel(q_ref, k_ref, vt_ref, o_ref, out_t_ref):
    tq = ATT_T
    qi = pl.program_id(1)
    q0 = pl.multiple_of(qi * tq, tq)
    heads = range(MLA_HEADS)
    qh = [q_ref[:, h * LANES:(h + 1) * LANES] for h in heads]

    def chunk(k0, states, causal):
        if causal:
            mask = (lax.broadcasted_iota(jnp.int32, (tq, tq), 0)
                    <= lax.broadcasted_iota(jnp.int32, (tq, tq), 1))

        def score_fn(h):
            k = k_ref[pl.ds(k0, tq), h * LANES:(h + 1) * LANES]
            return _tri_scores(k, qh[h], True, mask) if causal else _nt(k, qh[h])

        v_ts = [vt_ref[h * MLA_V_DIM:(h + 1) * MLA_V_DIM, pl.ds(k0, tq)] for h in heads]
        return _softmax_steps([functools.partial(score_fn, h) for h in heads], v_ts, states,
                              functools.partial(_tri_pv, True) if causal else _nn)

    states = chunk(q0, tuple(_softmax_init(tq) for _ in heads), True)
    states = lax.fori_loop(0, qi, lambda c, st: chunk(pl.multiple_of(c * tq, tq), st, False), states)
    for h in heads:
        out_t_ref[h * MLA_V_DIM:(h + 1) * MLA_V_DIM, :] = _softmax_finish(states[h])
    out_t_ref[MLA_V_ROWS:, :] = jnp.zeros((RET_PAIRS * LANES - MLA_V_ROWS, tq), F32)
    for j in range(RET_PAIRS):
        o_ref[:, j * LANES:(j + 1) * LANES] = out_t_ref[j * LANES:(j + 1) * LANES, :].T.astype(BF16)


def _mla_attention(q, k, vt, batch, seq):
    tq = ATT_T
    nq = seq // tq
    return pl.pallas_call(
        _mla_attn_kernel,
        grid=(batch, nq),
        in_specs=[
            pl.BlockSpec((tq, MLA_HEADS * LANES), lambda b, i: (b * nq + i, 0)),
            pl.BlockSpec((seq, MLA_HEADS * LANES), lambda b, i: (b, 0)),
            pl.BlockSpec((MLA_V_ROWS, seq), lambda b, i: (0, b)),
        ],
        out_specs=pl.BlockSpec((tq, RET_PAIRS * LANES), lambda b, i: (b * nq + i, 0)),
        out_shape=jax.ShapeDtypeStruct((batch * seq, RET_PAIRS * LANES), BF16),
        scratch_shapes=[pltpu.VMEM((RET_PAIRS * LANES, tq), F32)],
        compiler_params=pltpu.CompilerParams(
            dimension_semantics=("parallel", "arbitrary"), vmem_limit_bytes=VMEM_LIMIT),
        name="mla_attention",
    )(q, k, vt)


RET_UNROLL = 16


def _retention_kernel(q_ref, k_ref, v_ref, gate_ref, gn_ref, intra_ref, rd_ref, wd_ref, cd_ref, o_ref,
                      kv_ref, st_ref):
    c_len = RET_CHUNK
    n_chunks = q_ref.shape[0] // c_len
    lo = _lane_lo((1, LANES))
    hi = jnp.logical_not(lo)
    blockdiag = (lax.broadcasted_iota(jnp.int32, (LANES, LANES), 0) < HALF) == _lane_lo((LANES, LANES))
    intra_a = intra_ref[0, 0]
    intra_b = intra_ref[0, 1]
    read_decay = rd_ref[0]
    write_decay = wd_ref[0]
    chunk_decay = cd_ref[0]
    gn = gn_ref[0]

    averager = jnp.where(blockdiag, 1.0 / HEAD_DIM, 0.0).astype(BF16)

    def half_mean(x):
        x_hi = x.astype(BF16)
        x_lo = (x - x_hi.astype(F32)).astype(BF16)
        return _nn(x_hi, averager) + _nn(x_lo, averager)

    def kv_body(c, carry):
        r0 = pl.multiple_of(c * c_len, c_len)
        kc = k_ref[pl.ds(r0, c_len), :]
        kv_ref[c] = _tn((kc.astype(F32) * write_decay).astype(BF16), v_ref[pl.ds(r0, c_len), :])
        return carry

    lax.fori_loop(0, n_chunks, kv_body, 0, unroll=RET_UNROLL)

    state = jnp.zeros((LANES, LANES), F32)
    for c in range(n_chunks):
        st_ref[c] = jnp.where(blockdiag, state, 0.0).astype(BF16)
        state = state * chunk_decay + kv_ref[c]

    def out_body(step, carry):
        cs = [step * RET_UNROLL + u for u in range(RET_UNROLL)]
        rows = [pl.ds(pl.multiple_of(c * c_len, c_len), c_len) for c in cs]
        qs = [q_ref[r, :] for r in rows]
        ks = [k_ref[r, :] for r in rows]
        vs = [v_ref[r, :] for r in rows]
        zero = jnp.zeros_like(qs[0])
        sa = [(_nt(jnp.where(lo, q, zero), k) * intra_a).astype(BF16) for q, k in zip(qs, ks)]
        sb = [(_nt(jnp.where(hi, q, zero), k) * intra_b).astype(BF16) for q, k in zip(qs, ks)]
        cross = [_nn(q, st_ref[c]) * read_decay for q, c in zip(qs, cs)]
        os_ = [jnp.where(lo, _nn(a, v), _nn(b, v)) + x for a, b, v, x in zip(sa, sb, vs, cross)]
        ds_ = [o - half_mean(o) for o in os_]
        ys = [d * lax.rsqrt(half_mean(d * d) + NORM_EPS) * gn for d in ds_]
        for r, y in zip(rows, ys):
            o_ref[r, :] = (jax.nn.silu(gate_ref[r, :]) * y).astype(BF16)
        return carry

    lax.fori_loop(0, n_chunks // RET_UNROLL, out_body, 0)


def _retention(ret, rgate, gn, intra, rd, wd, cd, batch, seq):
    pair_const3 = lambda b, j: (j, 0, 0)
    return pl.pallas_call(
        _retention_kernel,
        grid=(batch, RET_PAIRS),
        in_specs=[
            pl.BlockSpec((seq, LANES), lambda b, j: (b, j)),
            pl.BlockSpec((seq, LANES), lambda b, j: (b, RET_PAIRS + j)),
            pl.BlockSpec((seq, LANES), lambda b, j: (b, 2 * RET_PAIRS + j)),
            pl.BlockSpec((seq, LANES), lambda b, j: (b, j)),
            pl.BlockSpec((1, 1, LANES), pair_const3),
            pl.BlockSpec((1, 2, RET_CHUNK, RET_CHUNK), lambda b, j: (j, 0, 0, 0)),
            pl.BlockSpec((1, RET_CHUNK, LANES), pair_const3),
            pl.BlockSpec((1, RET_CHUNK, LANES), pair_const3),
            pl.BlockSpec((1, 1, LANES), pair_const3),
        ],
        out_specs=pl.BlockSpec((seq, LANES), lambda b, j: (b, j)),
        out_shape=jax.ShapeDtypeStruct((batch * seq, RET_PAIRS * LANES), BF16),
        scratch_shapes=[pltpu.VMEM((seq // RET_CHUNK, LANES, LANES), F32),
                        pltpu.VMEM((seq // RET_CHUNK, LANES, LANES), BF16)],
        compiler_params=pltpu.CompilerParams(
            dimension_semantics=("parallel", "arbitrary"), vmem_limit_bytes=VMEM_LIMIT),
        name="retention",
    )(ret, ret, ret, rgate, gn, intra, rd, wd, cd)


def _out_mlp_kernel(x_ref, nsa_ref, mla_ref, ret_ref, wo_ref, g2_ref, wu_ref, wd_ref, gf_ref, o_ref,
                    *, final_norm):
    full = 2 * LANES
    mixed = jnp.concatenate([nsa_ref[...], mla_ref[:, :full], mla_ref[:, full:] + ret_ref[:, full:],
                             ret_ref[:, :full]], axis=1)
    x = x_ref[...] + _nn(mixed, wo_ref[...])
    h = _rms(x, g2_ref[...]).astype(BF16)
    y = x
    for c in range(D_FF // MLP_FF_CHUNK):
        sl = slice(c * MLP_FF_CHUNK, (c + 1) * MLP_FF_CHUNK)
        u = jnp.maximum(_nn(h, wu_ref[:, sl]), 0.0)
        y = y + _nn((u * u).astype(BF16), wd_ref[sl, :])
    if final_norm:
        y = _rms(y, gf_ref[...])
    o_ref[...] = y


def _out_mlp(x2, o_nsa, o_mla, o_ret, wo, g2, wu, wd, layer, gf, final_norm):
    t = x2.shape[0]
    w = RET_PAIRS * LANES
    row = lambda i: (i, 0)
    const = lambda i: (0, 0)
    resident = dict(pipeline_mode=pl.Buffered(1))
    return pl.pallas_call(
        functools.partial(_out_mlp_kernel, final_norm=final_norm),
        grid=(t // MLP_TM,),
        in_specs=[
            pl.BlockSpec((MLP_TM, D_MODEL), row),
            pl.BlockSpec((MLP_TM, w), row),
            pl.BlockSpec((MLP_TM, w), row),
            pl.BlockSpec((MLP_TM, w), row),
            _layer_spec(wo, layer, **resident),
            pl.BlockSpec((1, D_MODEL), const),
            _layer_spec(wu, layer, **resident),
            _layer_spec(wd, layer, **resident),
            pl.BlockSpec((1, D_MODEL), const),
        ],
        out_specs=pl.BlockSpec((MLP_TM, D_MODEL), row),
        out_shape=jax.ShapeDtypeStruct((t, D_MODEL), F32),
        compiler_params=pltpu.CompilerParams(
            dimension_semantics=("parallel",), vmem_limit_bytes=VMEM_LIMIT),
        name="out_mlp",
    )(x2, o_nsa, o_mla, o_ret, wo, g2, wu, wd, gf)


ROPE_KINDS = (
    (PARTIAL_ROPE_DIM, ROPE_THETA, HEAD_DIM, 0),
    (MLA_ROPE_DIM, ROPE_THETA, LANES, HALF),
    (HEAD_DIM, RET_THETA, HEAD_DIM, 0),
)
ROPE_TM = 2048


def _rope_placement():
    n_angles = sum(dim // 2 for dim, _, _, _ in ROPE_KINDS)
    assert 2 * n_angles <= LANES
    place = np.zeros((LANES, 3 * LANES * len(ROPE_KINDS)), np.float32)
    fill = np.zeros((1, 3 * LANES * len(ROPE_KINDS)), np.float32)
    row0 = 0
    for kind, (dim, _, period, base) in enumerate(ROPE_KINDS):
        half = dim // 2
        col0 = kind * 3 * LANES
        for lane in range(LANES):
            rel = (lane - base) % period
            first = lane >= base and rel < half
            second = lane >= base and half <= rel < dim
            if first or second:
                angle = rel if first else rel - half
                place[row0 + angle, col0 + lane] = 1.0
                place[n_angles + row0 + angle, col0 + (1 if first else 2) * LANES + lane] = -1.0 if first else 1.0
            else:
                fill[0, col0 + lane] = 1.0
        row0 += half
    return jnp.asarray(place, BF16), jnp.asarray(fill)


def _rope_kernel(cs_ref, place_ref, fill_ref, *out_refs):
    x = cs_ref[...]
    x1 = x.astype(BF16)
    r1 = x - x1.astype(F32)
    x2 = r1.astype(BF16)
    x3 = (r1 - x2.astype(F32)).astype(BF16)
    place = place_ref[...]
    tab = _tn(x1, place) + _tn(x2, place) + _tn(x3, place) + fill_ref[...]
    for i, ref in enumerate(out_refs):
        ref[...] = tab[:, i * 3 * LANES:(i + 1) * 3 * LANES]


def _rope_tables(positions):
    inv = jnp.concatenate([1.0 / (theta ** (jnp.arange(0, dim, 2, dtype=F32) / dim))
                           for dim, theta, _, _ in ROPE_KINDS])
    ang = inv[:, None] * positions.reshape(-1).astype(F32)[None, :]
    compact = jnp.concatenate([jnp.cos(ang), jnp.sin(ang)], axis=0)
    compact = jnp.pad(compact, ((0, LANES - compact.shape[0]), (0, 0)))
    place, fill = _rope_placement()
    t = compact.shape[1]
    tab_shape = jax.ShapeDtypeStruct((t, 3 * LANES), F32)
    return pl.pallas_call(
        _rope_kernel,
        grid=(t // ROPE_TM,),
        in_specs=[pl.BlockSpec((LANES, ROPE_TM), lambda i: (0, i)),
                  pl.BlockSpec(place.shape, lambda i: (0, 0)),
                  pl.BlockSpec(fill.shape, lambda i: (0, 0))],
        out_specs=tuple(pl.BlockSpec((ROPE_TM, 3 * LANES), lambda i: (i, 0)) for _ in ROPE_KINDS),
        out_shape=tuple(tab_shape for _ in ROPE_KINDS),
        compiler_params=pltpu.CompilerParams(
            dimension_semantics=("parallel",), vmem_limit_bytes=VMEM_LIMIT),
        name="rope_tables",
    )(compact, place, fill)


def _in_weight(w_in):
    offs = np.cumsum((0,) + IN_SIZES)
    (nsa_q, k_cmp, v_cmp, k_slc, v_slc, k_win, v_win, gate,
     cq, ckv, kpe, ret_q, ret_k, ret_v, ret_g) = [np.arange(offs[i], offs[i + 1]) for i in range(len(IN_SIZES))]
    scale = HEAD_DIM ** -0.5
    head = lambda cols, h: cols[h * HEAD_DIM:(h + 1) * HEAD_DIM]
    pad = lambda n: None if n == 0 else -np.ones(n, np.int64)
    pieces = []
    for j in range(NSA_REP):
        pieces += [(head(nsa_q, j), scale), (head(nsa_q, j + NSA_REP), scale)]
    pieces += [(c, 1.0) for c in (k_slc, k_win, k_cmp, v_cmp, cq, ckv)]
    pieces += [(pad(HALF), 0.0), (kpe, 1.0), (pad(LANES - HALF - MLA_ROPE_DIM), 0.0)]
    slots = lambda cols, f, fill: [fill if h is None else (head(cols, h), f) for h in RET_SLOTS]
    zero = (pad(HEAD_DIM), 0.0)
    pieces += slots(ret_q, 1.0, (head(ret_k, RET_SLOTS[-1]), scale))
    pieces += slots(ret_k, scale, zero)[:2 * RET_K_TILES]
    pieces += slots(ret_v, 1.0, zero) + slots(ret_g, 1.0, zero)
    t_pieces = [(v_slc, 1.0), (v_win, 1.0), (gate, 1.0), (pad(GATE_ROWS - NSA_GATE_W), 0.0)]

    def gather(parts):
        src = np.concatenate([c for c, _ in parts])
        factor = np.concatenate([np.full(len(c), f if (c >= 0).all() else 0.0, np.float32) for c, f in parts])
        return (jnp.take(w_in, jnp.asarray(np.maximum(src, 0), jnp.int32), axis=2) * factor).astype(BF16)

    w_main = gather(pieces)
    assert w_main.shape[-1] == N_PAD
    return w_main, gather(t_pieces).transpose(0, 2, 1)


def _compress_weights(pos, w1, w2):
    nl = pos.shape[0]
    g, dh, hid = NSA_KV_GROUPS, HEAD_DIM, CMP_HIDDEN
    assert g == 2
    p = jnp.tile(pos, (1, 1, g))

    def block_diag2(w):
        lead = ((0, 0),) * (w.ndim - 1)
        return jnp.concatenate([jnp.pad(w, lead + ((0, w.shape[-1]),)),
                                jnp.pad(w, lead + ((w.shape[-1], 0),))], axis=-2)

    w = block_diag2(w1.astype(BF16).reshape(nl, CMP_LEN, dh, hid))
    return p, w, block_diag2(w2.astype(BF16))


def _mla_weights(w_uq, w_ukv):
    nl = w_uq.shape[0]
    dq = MLA_NOPE_DIM + MLA_ROPE_DIM
    wq = w_uq.reshape(nl, MLA_Q_RANK, MLA_HEADS, dq)
    pe = wq[..., MLA_NOPE_DIM:]
    half = MLA_ROPE_DIM // 2
    rot = jnp.concatenate([jnp.zeros_like(wq[..., :MLA_NOPE_DIM]), -pe[..., half:], pe[..., :half]], axis=-1)
    pad = lambda w: jnp.pad(w, ((0, 0), (0, 0), (0, 0), (0, LANES - dq))).reshape(
        nl, MLA_Q_RANK, MLA_HEADS * LANES)
    wq = jnp.concatenate([pad(wq), pad(rot)], axis=-1)
    wkv = w_ukv.reshape(nl, MLA_KV_RANK, MLA_HEADS, MLA_NOPE_DIM + MLA_V_DIM)
    wk = jnp.pad(wkv[..., :MLA_NOPE_DIM], ((0, 0), (0, 0), (0, 0), (0, LANES - MLA_NOPE_DIM)))
    wk = wk.reshape(nl, MLA_KV_RANK, MLA_HEADS * LANES)
    wvt = wkv[..., MLA_NOPE_DIM:].reshape(nl, MLA_KV_RANK, MLA_V_ROWS).transpose(0, 2, 1)
    return wq.astype(BF16), wk.astype(BF16), wvt.astype(BF16)


def _out_weight(w_out):
    nl = w_out.shape[0]
    nsa = w_out[:, :NSA_Q_W].reshape(nl, NSA_HEADS, HEAD_DIM, D_MODEL)
    order = [h for j in range(NSA_REP) for h in (j, j + NSA_REP)]
    nsa = nsa[:, order].reshape(nl, NSA_Q_W, D_MODEL)
    mla = w_out[:, NSA_Q_W:NSA_Q_W + MLA_HEADS * MLA_V_DIM]
    ret = w_out[:, NSA_Q_W + MLA_HEADS * MLA_V_DIM:]
    full = 2 * LANES
    return jnp.concatenate([nsa, mla[:, :full], mla[:, full:], ret[:, full:], ret[:, :full]],
                           axis=1).astype(BF16)


def _retention_tables(gn_gain):
    nh = 2 * RET_PAIRS
    slot_head = np.array([RET_HEADS if h is None else h for h in RET_SLOTS])
    log_g = jnp.log(1.0 - 2.0 ** (-5.0 - jnp.asarray(slot_head, F32)))
    i = jnp.arange(RET_CHUNK, dtype=F32)
    diff = i[:, None] - i[None, :]
    intra = jnp.where(diff >= 0, jnp.exp(jnp.maximum(diff, 0.0)[None] * log_g[:, None, None]), 0.0)
    read_decay = jnp.exp((i + 1.0)[None, :] * log_g[:, None])
    write_decay = jnp.exp((RET_CHUNK - 1.0 - i)[None, :] * log_g[:, None])
    chunk_decay = jnp.exp(RET_CHUNK * log_g)

    def lanes(t):
        t = t.reshape(RET_PAIRS, 2, -1)
        return jnp.repeat(t.transpose(0, 2, 1), HALF, axis=-1)

    gn = jnp.pad(gn_gain, ((0, 0), (0, nh - RET_HEADS), (0, 0)))[:, slot_head]
    gn = gn.reshape(gn.shape[0], RET_PAIRS, 1, LANES)
    return (gn, intra.reshape(RET_PAIRS, 2, RET_CHUNK, RET_CHUNK), lanes(read_decay), lanes(write_decay),
            lanes(chunk_decay[:, None]))


def _selection_overlap(seq):
    n_cmp = (seq - CMP_LEN) // CMP_STRIDE + 1
    n_sel = seq // SEL_BLOCK
    cs = np.arange(n_cmp) * CMP_STRIDE
    ss = np.arange(n_sel) * SEL_BLOCK
    ov = np.clip(np.minimum(cs[:, None] + CMP_LEN, ss[None, :] + SEL_BLOCK)
                 - np.maximum(cs[:, None], ss[None, :]), 0, None) / CMP_LEN
    ovl_t = np.zeros((n_sel, seq // CMP_STRIDE), np.float32)
    ovl_t[:, :n_cmp] = ov.T
    return jnp.asarray(ovl_t, BF16)


def kernel(x, positions, ln1_gain, w_in, cmp_pos_k, cmp_w1_k, cmp_w2_k, cmp_pos_v, cmp_w1_v, cmp_w2_v,
           mla_q_norm, mla_w_uq, mla_kv_norm, mla_w_ukv, ret_gn_gain, w_out, ln2_gain, w_up, w_down,
           final_gain):
    batch, seq, _ = x.shape
    depth = w_in.shape[0]
    t = batch * seq

    tab_n, tab_m, tab_r = _rope_tables(positions)

    w_in_p, w_in_t = _in_weight(w_in)
    pos_k, w1_k, w2_k = _compress_weights(cmp_pos_k, cmp_w1_k, cmp_w2_k)
    pos_v, w1_v, w2_v = _compress_weights(cmp_pos_v, cmp_w1_v, cmp_w2_v)
    cmp_pos = jnp.stack([pos_k, pos_v], axis=1)
    cmp_w1 = jnp.stack([w1_k, w1_v], axis=1)
    w2_vt = w2_v.transpose(0, 2, 1)
    wq, wk, wvt = _mla_weights(mla_w_uq, mla_w_ukv)
    wo = _out_weight(w_out)
    wu = w_up.astype(BF16)
    wd = w_down.astype(BF16)
    gn, intra, rd, wdec, cd = _retention_tables(ret_gn_gain)
    ovl_t = _selection_overlap(seq)
    gf = final_gain.reshape(1, D_MODEL)

    x2 = x.reshape(t, D_MODEL)
    for l in range(depth):
        nsa, nsa_t, gate_t, k_cmp, v_cmp, ret, rgate, q_m, k_m, vt_m = _in_proj(
            x2, ln1_gain[l].reshape(1, D_MODEL), w_in_p, w_in_t, mla_q_norm[l].reshape(1, -1),
            mla_kv_norm[l].reshape(1, -1), wq, wk, wvt, l, tab_n, tab_m, tab_r)
        kc, vct = _compress(k_cmp, v_cmp, cmp_pos[l], cmp_w1, w2_k[l], w2_vt[l], l, batch, seq)
        o_nsa = _nsa_attention(nsa, nsa_t, gate_t, kc, vct, ovl_t, batch, seq)
        o_mla = _mla_attention(q_m, k_m, vt_m, batch, seq)
        o_ret = _retention(ret, rgate, gn[l], intra, rd, wdec, cd, batch, seq)
        x2 = _out_mlp(x2, o_nsa, o_mla, o_ret, wo, ln2_gain[l].reshape(1, D_MODEL), wu, wd, l, gf,
                      final_norm=(l == depth - 1))
    return x2.reshape(batch, seq, D_MODEL)
```

```python
import functools
import math

import numpy as np
import jax
import jax.numpy as jnp
from jax import lax
from jax.experimental import pallas as pl
from jax.experimental.pallas import tpu as pltpu

F32 = jnp.float32
BF16 = jnp.bfloat16

D_MODEL = 1024
HEAD_DIM = 64
NSA_HEADS = 6
NSA_KV_GROUPS = 2
NSA_REP = NSA_HEADS // NSA_KV_GROUPS
N_BRANCH = 3
CMP_LEN = 32
CMP_STRIDE = 16
CMP_HIDDEN = 2 * HEAD_DIM
SEL_BLOCK = 64
SEL_TOP_N = 16
WINDOW = 512
MLA_HEADS = 5
MLA_Q_RANK = 256
MLA_KV_RANK = 128
MLA_NOPE_DIM = 64
MLA_ROPE_DIM = 32
MLA_V_DIM = 64
RET_HEADS = 5
RET_CHUNK = 128
ROPE_THETA = 500000.0
PARTIAL_ROPE_DIM = HEAD_DIM // 4
RET_THETA = 10000.0
D_FF = 4 * D_MODEL
NORM_EPS = 1e-6
NEG_INF = -1e30
FORCE_SCORE = 1e9
LOG2E = math.log2(math.e)

NSA_Q_W = NSA_HEADS * HEAD_DIM
NSA_KV_W = NSA_KV_GROUPS * HEAD_DIM
NSA_GATE_W = NSA_HEADS * N_BRANCH
RET_W = RET_HEADS * HEAD_DIM
IN_SIZES = (NSA_Q_W, NSA_KV_W, NSA_KV_W, NSA_KV_W, NSA_KV_W, NSA_KV_W, NSA_KV_W, NSA_GATE_W,
            MLA_Q_RANK, MLA_KV_RANK, MLA_ROPE_DIM, RET_W, RET_W, RET_W, RET_W)

LANES = 128
HALF = LANES // 2
SUBLANES = 8
MXU_WIDTH = 256
VMEM_LIMIT = 56 * 1024 * 1024

NSA_TILES = 5
GATE_ROWS = 32
NSA_T_ROWS = 2 * LANES + GATE_ROWS
MLA_IN_TILES = 4
RET_PAIRS = 3
RET_SLOTS = (0, 1, 2, 3, None, 4)
RET_K_TILES = RET_PAIRS - 1
RET_ROPE_TILES = RET_PAIRS + RET_K_TILES
RET_MM_TILES = RET_ROPE_TILES + RET_PAIRS
RET_TILES = 3 * RET_PAIRS
IN_TILES = NSA_TILES + 2 + MLA_IN_TILES + RET_MM_TILES + RET_PAIRS
N_PAD = IN_TILES * LANES
MLA_V_ROWS = MLA_HEADS * MLA_V_DIM

IN_TM = 512
IN_CHUNK_TILES = 4 * MXU_WIDTH // LANES
MLP_TM = 1024
ATT_T = 512
MLP_FF_CHUNK = 512


def _nn(a, b):
    return jnp.dot(a, b, preferred_element_type=F32)


def _nt(a, b):
    return lax.dot_general(a, b, (((1,), (1,)), ((), ())), preferred_element_type=F32)


def _tn(a, b):
    return lax.dot_general(a, b, (((0,), (0,)), ((), ())), preferred_element_type=F32)


def _rms(x, gain):
    return x * lax.rsqrt(jnp.mean(x * x, axis=-1, keepdims=True) + NORM_EPS) * gain


def _rope(val, tab, half):
    cos = tab[:, 0:LANES]
    sin_a = tab[:, LANES:2 * LANES]
    sin_b = tab[:, 2 * LANES:3 * LANES]
    return (val * cos + pltpu.roll(val, LANES - half, 1) * sin_a
            + pltpu.roll(val, half, 1) * sin_b)


def _lane_lo(shape):
    return lax.broadcasted_iota(jnp.int32, shape, len(shape) - 1) < HALF


def _mla_up(c, qn_ref, kvn_ref, wq_ref, wk_ref, wvt_ref, tm_ref, q_ref, k_ref, vt_ref):
    scale = (MLA_NOPE_DIM + MLA_ROPE_DIM) ** -0.5 * LOG2E
    cq = _rms(c[:, 0:MLA_Q_RANK], qn_ref[...]).astype(BF16)
    ckv = _rms(c[:, MLA_Q_RANK:MLA_Q_RANK + MLA_KV_RANK], kvn_ref[...]).astype(BF16)
    k_pe = _rope(c[:, 3 * LANES:4 * LANES], tm_ref[...], MLA_ROPE_DIM // 2)
    cos = tm_ref[:, 0:LANES] * scale
    sin = (tm_ref[:, 2 * LANES:3 * LANES] - tm_ref[:, LANES:2 * LANES]) * scale
    q = _nn(cq, wq_ref[...])
    k = _nn(ckv, wk_ref[...])
    w = MLA_HEADS * LANES
    for hd in range(MLA_HEADS):
        sl = slice(hd * LANES, (hd + 1) * LANES)
        rot = slice(w + hd * LANES, w + (hd + 1) * LANES)
        q_ref[:, sl] = (q[:, sl] * cos + q[:, rot] * sin).astype(BF16)
        k_ref[:, sl] = (k[:, sl] + k_pe).astype(BF16)
    vt_ref[...] = _nt(wvt_ref[...], ckv).astype(BF16)


def _inproj_kernel(x_ref, g_ref, w_ref, wt_ref, tn_ref, tm_ref, tr_ref, qn_ref, kvn_ref, wq_ref, wk_ref,
                   wvt_ref, nsa_ref, nsat_ref, gatet_ref, kcmp_ref, vcmp_ref, ret_ref, rgate_ref,
                   mq_ref, mk_ref, mvt_ref):
    h = _rms(x_ref[...], g_ref[...]).astype(BF16)
    tab_n = tn_ref[...]
    tab_r = tr_ref[...]

    chunks = [_nn(h, w_ref[:, c:min(c + IN_CHUNK_TILES * LANES, N_PAD)])
              for c in range(0, N_PAD, IN_CHUNK_TILES * LANES)]

    def tiles(first, n):
        cols = [chunks[i // IN_CHUNK_TILES][:, (i % IN_CHUNK_TILES) * LANES:(i % IN_CHUNK_TILES + 1) * LANES]
                for i in range(first, first + n)]
        return cols[0] if n == 1 else jnp.concatenate(cols, axis=1)

    t0 = 0
    for i in range(NSA_TILES):
        v = _rope(tiles(t0 + i, 1), tab_n, PARTIAL_ROPE_DIM // 2)
        if i < NSA_REP:
            v = v * LOG2E
        nsa_ref[:, i * LANES:(i + 1) * LANES] = v.astype(BF16)
    t0 += NSA_TILES
    kcmp_ref[...] = _rope(tiles(t0, 1), tab_n, PARTIAL_ROPE_DIM // 2)
    vcmp_ref[...] = tiles(t0 + 1, 1)
    t0 += 2
    _mla_up(tiles(t0, MLA_IN_TILES), qn_ref, kvn_ref, wq_ref, wk_ref, wvt_ref, tm_ref, mq_ref, mk_ref, mvt_ref)
    t0 += MLA_IN_TILES
    stored = []
    for i in range(RET_MM_TILES):
        v = tiles(t0 + i, 1)
        if i < RET_ROPE_TILES:
            v = _rope(v, tab_r, HEAD_DIM // 2)
        stored.append(v)
        if i == RET_ROPE_TILES - 1:
            stored.append(pltpu.roll(stored[RET_PAIRS - 1], HALF, 1))
    for i, v in enumerate(stored):
        ret_ref[:, i * LANES:(i + 1) * LANES] = v.astype(BF16)
    t0 += RET_MM_TILES
    rgate_ref[...] = tiles(t0, RET_PAIRS)
    at = _nt(wt_ref[...], h)
    nsat_ref[...] = at[0:2 * LANES, :].astype(BF16)
    gatet_ref[...] = at[2 * LANES:NSA_T_ROWS, :]


def _layer_spec(w, layer, **kwargs):
    zeros = (0,) * (w.ndim - 1)
    return pl.BlockSpec((None,) + w.shape[1:], lambda *_: (layer,) + zeros, **kwargs)


def _in_proj(x2, gain, w, wt, q_norm, kv_norm, wq, wk, wvt, layer, tab_n, tab_m, tab_r):
    t = x2.shape[0]
    row = lambda i: (i, 0)
    col = lambda i: (0, i)
    const = lambda i: (0, 0)
    out_shapes = (
        jax.ShapeDtypeStruct((t, NSA_TILES * LANES), BF16),
        jax.ShapeDtypeStruct((2 * LANES, t), BF16),
        jax.ShapeDtypeStruct((GATE_ROWS, t), F32),
        jax.ShapeDtypeStruct((t, LANES), F32),
        jax.ShapeDtypeStruct((t, LANES), F32),
        jax.ShapeDtypeStruct((t, RET_TILES * LANES), BF16),
        jax.ShapeDtypeStruct((t, RET_PAIRS * LANES), F32),
        jax.ShapeDtypeStruct((t, MLA_HEADS * LANES), BF16),
        jax.ShapeDtypeStruct((t, MLA_HEADS * LANES), BF16),
        jax.ShapeDtypeStruct((MLA_V_ROWS, t), BF16),
    )
    out_specs = tuple(
        pl.BlockSpec((s.shape[0], IN_TM), col) if s.shape[1] == t else pl.BlockSpec((IN_TM, s.shape[1]), row)
        for s in out_shapes)
    return pl.pallas_call(
        _inproj_kernel,
        grid=(t // IN_TM,),
        in_specs=[
            pl.BlockSpec((IN_TM, D_MODEL), row),
            pl.BlockSpec((1, D_MODEL), const),
            _layer_spec(w, layer),
            _layer_spec(wt, layer),
            pl.BlockSpec((IN_TM, 3 * LANES), row),
            pl.BlockSpec((IN_TM, 3 * LANES), row),
            pl.BlockSpec((IN_TM, 3 * LANES), row),
            pl.BlockSpec((1, MLA_Q_RANK), const),
            pl.BlockSpec((1, MLA_KV_RANK), const),
            _layer_spec(wq, layer),
            _layer_spec(wk, layer),
            _layer_spec(wvt, layer),
        ],
        out_specs=out_specs,
        out_shape=out_shapes,
        compiler_params=pltpu.CompilerParams(
            dimension_semantics=("parallel",), vmem_limit_bytes=VMEM_LIMIT),
        name="in_proj",
    )(x2, gain, w, wt, tab_n, tab_m, tab_r, q_norm, kv_norm, wq, wk, wvt)


def _compress_kernel(k_ref, v_ref, pos_ref, w1_ref, w2k_ref, w2vt_ref, kc_ref, vct_ref):
    n_blk = k_ref.shape[0] // CMP_STRIDE

    def hidden(src, i):
        toks = [src[pl.ds(r, n_blk, stride=CMP_STRIDE), :] for r in range(CMP_STRIDE)]
        halves = []
        for half in range(CMP_LEN // CMP_STRIDE):
            off = half * CMP_STRIDE
            x = jnp.concatenate([(toks[r] + pos_ref[i, off + r:off + r + 1, :]).astype(BF16)
                                 for r in range(CMP_STRIDE)], axis=1)
            w = w1_ref[i, off:off + CMP_STRIDE].reshape(CMP_STRIDE * LANES, NSA_KV_GROUPS * CMP_HIDDEN)
            halves.append(_nn(x, w))
        return jax.nn.gelu(halves[0] + pltpu.roll(halves[1], n_blk - 1, 0)).astype(BF16)

    kc_ref[0] = _nn(hidden(k_ref, 0), w2k_ref[...]).astype(BF16)
    vct_ref[0] = _nt(w2vt_ref[...], hidden(v_ref, 1)).astype(BF16)


def _compress(k_cmp, v_cmp, pos, w1, w2k, w2vt, layer, batch, seq):
    b = batch
    n_blk = seq // CMP_STRIDE
    return pl.pallas_call(
        _compress_kernel,
        grid=(b,),
        in_specs=[
            pl.BlockSpec((seq, LANES), lambda i: (i, 0)),
            pl.BlockSpec((seq, LANES), lambda i: (i, 0)),
            pl.BlockSpec(pos.shape, lambda i: (0, 0, 0)),
            _layer_spec(w1, layer),
            pl.BlockSpec(w2k.shape, lambda i: (0, 0)),
            pl.BlockSpec(w2vt.shape, lambda i: (0, 0)),
        ],
        out_specs=(pl.BlockSpec((1, n_blk, LANES), lambda i: (i, 0, 0)),
                   pl.BlockSpec((1, LANES, n_blk), lambda i: (i, 0, 0))),
        out_shape=(jax.ShapeDtypeStruct((b, n_blk, LANES), BF16),
                   jax.ShapeDtypeStruct((b, LANES, n_blk), BF16)),
        compiler_params=pltpu.CompilerParams(
            dimension_semantics=("parallel",), vmem_limit_bytes=VMEM_LIMIT),
        name="nsa_compress",
    )(k_cmp, v_cmp, pos, w1, w2k, w2vt)


SCORE_LOOKAHEAD = 2
ONES_ROWS = 16


def _tri_scores(k, q, dead_upper, mask, bias=None):
    half = k.shape[0] // 2
    dead = jnp.full((half, half), NEG_INF, F32)
    if dead_upper:
        top = _nt(k[:half], q)
        bottom = jnp.concatenate([dead, _nt(k[half:], q[half:])], axis=1)
    else:
        top = jnp.concatenate([_nt(k[:half], q[:half]), dead], axis=1)
        bottom = _nt(k[half:], q)
    s_t = jnp.concatenate([top, bottom], axis=0)
    return jnp.where(mask, s_t if bias is None else s_t + bias, NEG_INF)


def _tri_pv(dead_upper, v_ext, p):
    half = p.shape[0] // 2
    if dead_upper:
        full = _nn(v_ext[:, :half], p[:half])
        part = _nn(v_ext[:, half:], p[half:, half:])
        return jnp.concatenate([full[:, :half], full[:, half:] + part], axis=1)
    part = _nn(v_ext[:, :half], p[:half, :half])
    full = _nn(v_ext[:, half:], p[half:])
    return jnp.concatenate([full[:, :half] + part, full[:, half:]], axis=1)


def _softmax_steps(score_fns, v_ts, states, pv_fn=_nn):
    n = len(score_fns)
    s_ts = [score_fns[h]() if h < SCORE_LOOKAHEAD else None for h in range(n)]
    out = []
    for h in range(n):
        if h + SCORE_LOOKAHEAD < n:
            s_ts[h + SCORE_LOOKAHEAD] = score_fns[h + SCORE_LOOKAHEAD]()
        m_old, acc_old = states[h]
        m = jnp.maximum(m_old, jnp.max(s_ts[h], axis=0, keepdims=True))
        p = jnp.exp2(s_ts[h] - m).astype(BF16)
        s_ts[h] = None
        v_ext = jnp.concatenate([v_ts[h], jnp.ones((ONES_ROWS, v_ts[h].shape[1]), BF16)], axis=0)
        out.append((m, jnp.exp2(m_old - m) * acc_old + pv_fn(v_ext, p)))
    return tuple(out)


def _softmax_init(cols):
    return jnp.full((1, cols), NEG_INF, F32), jnp.zeros((HEAD_DIM + ONES_ROWS, cols), F32)


def _softmax_finish(state):
    acc = state[1]
    return acc[0:HEAD_DIM] * (1.0 / jnp.maximum(acc[HEAD_DIM:HEAD_DIM + 1], 1e-30))


def _nsa_kernel(q_ref, gatet_ref, kc_ref, vct_ref, ks_ref, kw_ref, vst_ref, vwt_ref, ovl_ref,
                o_ref, out_t_ref, selb_ref):
    tq = ATT_T
    qi = pl.program_id(1)
    q0 = pl.multiple_of(qi * tq, tq)
    lo1 = _lane_lo((1, LANES))
    group_lanes = (lo1, jnp.logical_not(lo1))
    gate = jax.nn.sigmoid(gatet_ref[...])
    heads = range(NSA_HEADS)

    def q_head(h):
        j, g = h % NSA_REP, h // NSA_REP
        tile = q_ref[:, j * LANES:(j + 1) * LANES]
        return jnp.where(group_lanes[g], tile, jnp.zeros_like(tile))

    def v_rows(ref, h, k0, n):
        g = h // NSA_REP
        return ref[g * HEAD_DIM:(g + 1) * HEAD_DIM, pl.ds(k0, n)]

    def emit(h, branch, qs, nq, o_t):
        j, g = h % NSA_REP, h // NSA_REP
        r0 = j * LANES + g * HEAD_DIM
        row = h * N_BRANCH + branch
        val = gate[row:row + 1, qs:qs + nq] * o_t
        if branch == 0:
            out_t_ref[r0:r0 + HEAD_DIM, qs:qs + nq] = val
        else:
            out_t_ref[r0:r0 + HEAD_DIM, qs:qs + nq] += val

    qm = [q_head(h) for h in heads]

    n_cmp_pad = kc_ref.shape[1]
    kc = kc_ref[0]
    vct = vct_ref[0]
    n_i = lax.broadcasted_iota(jnp.int32, (n_cmp_pad, tq), 0)
    t_l = q0 + lax.broadcasted_iota(jnp.int32, (n_cmp_pad, tq), 1)
    cmask = (n_i * CMP_STRIDE + (CMP_LEN - 1)) <= t_l
    cmask_f = cmask.astype(F32)
    s_cs = [jnp.where(cmask, _nt(kc, qm[h]), NEG_INF) for h in heads]
    p_cs = [jnp.exp2(s_t - jnp.max(s_t, axis=0, keepdims=True)) * cmask_f for s_t in s_cs]
    p_cs = [p * (1.0 / jnp.maximum(jnp.sum(p, axis=0, keepdims=True), 1e-30)) for p in p_cs]
    for h in heads:
        g = h // NSA_REP
        emit(h, 0, 0, tq, _nn(vct[g * HEAD_DIM:(g + 1) * HEAD_DIM, :], p_cs[h].astype(BF16)))

    n_sel = ovl_ref.shape[0]
    needs_rank = q0 + tq > SEL_TOP_N * SEL_BLOCK

    @pl.when(jnp.logical_not(needs_rank))
    def _():
        selb_ref[...] = jnp.zeros_like(selb_ref)

    @pl.when(needs_rank)
    def _():
        m_i = lax.broadcasted_iota(jnp.int32, (n_sel, tq), 0)
        cur = jnp.right_shift(q0 + lax.broadcasted_iota(jnp.int32, (n_sel, tq), 1),
                              SEL_BLOCK.bit_length() - 1)
        valid = m_i <= cur
        forced = (m_i == 0) | (m_i == cur) | (m_i == cur - 1)
        ovl = ovl_ref[...]
        sub = SUBLANES
        m_loc = lax.broadcasted_iota(jnp.int32, (sub, tq), 0)
        for g in range(NSA_KV_GROUPS):
            psum = p_cs[g * NSA_REP]
            for h in range(g * NSA_REP + 1, (g + 1) * NSA_REP):
                psum = psum + p_cs[h]
            p_hi = psum.astype(BF16)
            p_lo = (psum - p_hi.astype(F32)).astype(BF16)
            imp = _nn(ovl, p_hi) + _nn(ovl, p_lo)
            imp = jnp.where(valid & forced, FORCE_SCORE, imp)
            imp = jnp.where(valid, imp, NEG_INF)
            parts = [imp[i:i + sub] for i in range(0, n_sel, sub)]
            ranks = [jnp.zeros((sub, tq), jnp.int32) for _ in parts]
            for mp in range(n_sel):
                row = imp[mp:mp + 1, :]
                for i, part in enumerate(parts):
                    if i * sub + sub - 1 <= mp:
                        beats = row > part
                    elif i * sub > mp:
                        beats = row >= part
                    else:
                        beats = (row > part) | ((row == part) & (m_loc + i * sub > mp))
                    ranks[i] = ranks[i] + beats.astype(jnp.int32)
            rank = jnp.concatenate(ranks, axis=0)
            selb_ref[g] = jnp.where(rank < SEL_TOP_N, 0.0, NEG_INF)

    blocks_per_chunk = tq // SEL_BLOCK

    def sel_bias(g, c, n_keys, qs, nq):
        rows = [jnp.broadcast_to(selb_ref[g, pl.ds(c * blocks_per_chunk + i, 1), qs:qs + nq], (SEL_BLOCK, nq))
                for i in range(n_keys // SEL_BLOCK)]
        return jnp.concatenate(rows, axis=0)

    key_i = lax.broadcasted_iota(jnp.int32, (tq, tq), 0)
    qry_i = lax.broadcasted_iota(jnp.int32, (tq, tq), 1)
    causal_mask = key_i <= qry_i

    def sel_chunk(c, k0, states, causal):
        k = ks_ref[pl.ds(k0, tq), :]
        bias = [sel_bias(g, c, tq, 0, tq) for g in range(NSA_KV_GROUPS)]

        def score_fn(h):
            if causal:
                return _tri_scores(k, qm[h], True, causal_mask, bias[h // NSA_REP])
            return _nt(k, qm[h]) + bias[h // NSA_REP]

        return _softmax_steps([functools.partial(score_fn, h) for h in heads],
                              [v_rows(vst_ref, h, k0, tq) for h in heads], states,
                              functools.partial(_tri_pv, True) if causal else _nn)

    states = sel_chunk(qi, q0, tuple(_softmax_init(tq) for _ in heads), True)
    states = lax.fori_loop(
        0, qi, lambda c, st: sel_chunk(c, pl.multiple_of(c * tq, tq), st, False), states)
    for h in heads:
        emit(h, 1, 0, tq, _softmax_finish(states[h]))

    def win_chunk(k0, states, mask, dead_upper):
        k = kw_ref[pl.ds(k0, tq), :]
        if dead_upper is None:
            score_fns = [lambda h=h: jnp.where(mask, _nt(k, qm[h]), NEG_INF) for h in heads]
            pv_fn = _nn
        else:
            score_fns = [lambda h=h: _tri_scores(k, qm[h], dead_upper, mask) for h in heads]
            pv_fn = functools.partial(_tri_pv, dead_upper)
        return _softmax_steps(score_fns, [v_rows(vwt_ref, h, k0, tq) for h in heads], states, pv_fn)

    states = win_chunk(q0, tuple(_softmax_init(tq) for _ in heads), causal_mask, True)
    for d in range(1, WINDOW // tq + 1):
        k0 = pl.multiple_of(jnp.maximum(qi - d, 0) * tq, tq)
        in_band = (key_i - d * tq > qry_i - WINDOW) & (qi >= d)
        states = win_chunk(k0, states, in_band, False if d * tq == WINDOW else None)
    for h in heads:
        emit(h, 2, 0, tq, _softmax_finish(states[h]))

    for j in range(NSA_REP):
        o_ref[:, j * LANES:(j + 1) * LANES] = out_t_ref[j * LANES:(j + 1) * LANES, :].T.astype(BF16)


def _nsa_attention(nsa, nsa_t, gate_t, kc, vct, ovl_t, batch, seq):
    tq = ATT_T
    nq = seq // tq
    n_blk = kc.shape[1]
    qrow = lambda b, i: (b * nq + i, 0)
    return pl.pallas_call(
        _nsa_kernel,
        grid=(batch, nq),
        in_specs=[
            pl.BlockSpec((tq, NSA_REP * LANES), qrow),
            pl.BlockSpec((GATE_ROWS, tq), lambda b, i: (0, b * nq + i)),
            pl.BlockSpec((1, n_blk, LANES), lambda b, i: (b, 0, 0)),
            pl.BlockSpec((1, LANES, n_blk), lambda b, i: (b, 0, 0)),
            pl.BlockSpec((seq, LANES), lambda b, i: (b, 3)),
            pl.BlockSpec((seq, LANES), lambda b, i: (b, 4)),
            pl.BlockSpec((LANES, seq), lambda b, i: (0, b)),
            pl.BlockSpec((LANES, seq), lambda b, i: (1, b)),
            pl.BlockSpec(ovl_t.shape, lambda b, i: (0, 0)),
        ],
        out_specs=pl.BlockSpec((tq, NSA_REP * LANES), qrow),
        out_shape=jax.ShapeDtypeStruct((batch * seq, NSA_REP * LANES), BF16),
        scratch_shapes=[pltpu.VMEM((NSA_REP * LANES, tq), F32),
                        pltpu.VMEM((NSA_KV_GROUPS, seq // SEL_BLOCK, tq), F32)],
        compiler_params=pltpu.CompilerParams(
            dimension_semantics=("parallel", "arbitrary"), vmem_limit_bytes=VMEM_LIMIT),
        name="nsa_attention",
    )(nsa, gate_t, kc, vct, nsa, nsa, nsa_t, nsa_t, ovl_t)


def _mla_attn_kernel(q_ref, k_ref, vt_ref, o_ref, out_t_ref):
    tq = ATT_T
    qi = pl.program_id(1)
    q0 = pl.multiple_of(qi * tq, tq)
    heads = range(MLA_HEADS)
    qh = [q_ref[:, h * LANES:(h + 1) * LANES] for h in heads]

    def chunk(k0, states, causal):
        if causal:
            mask = (lax.broadcasted_iota(jnp.int32, (tq, tq), 0)
                    <= lax.broadcasted_iota(jnp.int32, (tq, tq), 1))

        def score_fn(h):
            k = k_ref[pl.ds(k0, tq), h * LANES:(h + 1) * LANES]
            return _tri_scores(k, qh[h], True, mask) if causal else _nt(k, qh[h])

        v_ts = [vt_ref[h * MLA_V_DIM:(h + 1) * MLA_V_DIM, pl.ds(k0, tq)] for h in heads]
        return _softmax_steps([functools.partial(score_fn, h) for h in heads], v_ts, states,
                              functools.partial(_tri_pv, True) if causal else _nn)

    states = chunk(q0, tuple(_softmax_init(tq) for _ in heads), True)
    states = lax.fori_loop(0, qi, lambda c, st: chunk(pl.multiple_of(c * tq, tq), st, False), states)
    for h in heads:
        out_t_ref[h * MLA_V_DIM:(h + 1) * MLA_V_DIM, :] = _softmax_finish(states[h])
    out_t_ref[MLA_V_ROWS:, :] = jnp.zeros((RET_PAIRS * LANES - MLA_V_ROWS, tq), F32)
    for j in range(RET_PAIRS):
        o_ref[:, j * LANES:(j + 1) * LANES] = out_t_ref[j * LANES:(j + 1) * LANES, :].T.astype(BF16)


def _mla_attention(q, k, vt, batch, seq):
    tq = ATT_T
    nq = seq // tq
    return pl.pallas_call(
        _mla_attn_kernel,
        grid=(batch, nq),
        in_specs=[
            pl.BlockSpec((tq, MLA_HEADS * LANES), lambda b, i: (b * nq + i, 0)),
            pl.BlockSpec((seq, MLA_HEADS * LANES), lambda b, i: (b, 0)),
            pl.BlockSpec((MLA_V_ROWS, seq), lambda b, i: (0, b)),
        ],
        out_specs=pl.BlockSpec((tq, RET_PAIRS * LANES), lambda b, i: (b * nq + i, 0)),
        out_shape=jax.ShapeDtypeStruct((batch * seq, RET_PAIRS * LANES), BF16),
        scratch_shapes=[pltpu.VMEM((RET_PAIRS * LANES, tq), F32)],
        compiler_params=pltpu.CompilerParams(
            dimension_semantics=("parallel", "arbitrary"), vmem_limit_bytes=VMEM_LIMIT),
        name="mla_attention",
    )(q, k, vt)


RET_UNROLL = 16


def _retention_kernel(q_ref, k_ref, v_ref, gate_ref, gn_ref, intra_ref, rd_ref, wd_ref, cd_ref, o_ref,
                      kv_ref, st_ref):
    c_len = RET_CHUNK
    n_chunks = q_ref.shape[0] // c_len
    lo = _lane_lo((1, LANES))
    hi = jnp.logical_not(lo)
    blockdiag = (lax.broadcasted_iota(jnp.int32, (LANES, LANES), 0) < HALF) == _lane_lo((LANES, LANES))
    intra_a = intra_ref[0, 0]
    intra_b = intra_ref[0, 1]
    read_decay = rd_ref[0]
    write_decay = wd_ref[0]
    chunk_decay = cd_ref[0]
    gn = gn_ref[0]

    averager = jnp.where(blockdiag, 1.0 / HEAD_DIM, 0.0).astype(BF16)

    def half_mean(x):
        x_hi = x.astype(BF16)
        x_lo = (x - x_hi.astype(F32)).astype(BF16)
        return _nn(x_hi, averager) + _nn(x_lo, averager)

    def kv_body(c, carry):
        r0 = pl.multiple_of(c * c_len, c_len)
        kc = k_ref[pl.ds(r0, c_len), :]
        kv_ref[c] = _tn((kc.astype(F32) * write_decay).astype(BF16), v_ref[pl.ds(r0, c_len), :])
        return carry

    lax.fori_loop(0, n_chunks, kv_body, 0, unroll=RET_UNROLL)

    state = jnp.zeros((LANES, LANES), F32)
    for c in range(n_chunks):
        st_ref[c] = jnp.where(blockdiag, state, 0.0).astype(BF16)
        state = state * chunk_decay + kv_ref[c]

    def out_body(step, carry):
        cs = [step * RET_UNROLL + u for u in range(RET_UNROLL)]
        rows = [pl.ds(pl.multiple_of(c * c_len, c_len), c_len) for c in cs]
        qs = [q_ref[r, :] for r in rows]
        ks = [k_ref[r, :] for r in rows]
        vs = [v_ref[r, :] for r in rows]
        zero = jnp.zeros_like(qs[0])
        sa = [(_nt(jnp.where(lo, q, zero), k) * intra_a).astype(BF16) for q, k in zip(qs, ks)]
        sb = [(_nt(jnp.where(hi, q, zero), k) * intra_b).astype(BF16) for q, k in zip(qs, ks)]
        cross = [_nn(q, st_ref[c]) * read_decay for q, c in zip(qs, cs)]
        os_ = [jnp.where(lo, _nn(a, v), _nn(b, v)) + x for a, b, v, x in zip(sa, sb, vs, cross)]
        ds_ = [o - half_mean(o) for o in os_]
        ys = [d * lax.rsqrt(half_mean(d * d) + NORM_EPS) * gn for d in ds_]
        for r, y in zip(rows, ys):
            o_ref[r, :] = (jax.nn.silu(gate_ref[r, :]) * y).astype(BF16)
        return carry

    lax.fori_loop(0, n_chunks // RET_UNROLL, out_body, 0)


def _retention(ret, rgate, gn, intra, rd, wd, cd, batch, seq):
    pair_const3 = lambda b, j: (j, 0, 0)
    return pl.pallas_call(
        _retention_kernel,
        grid=(batch, RET_PAIRS),
        in_specs=[
            pl.BlockSpec((seq, LANES), lambda b, j: (b, j)),
            pl.BlockSpec((seq, LANES), lambda b, j: (b, RET_PAIRS + j)),
            pl.BlockSpec((seq, LANES), lambda b, j: (b, 2 * RET_PAIRS + j)),
            pl.BlockSpec((seq, LANES), lambda b, j: (b, j)),
            pl.BlockSpec((1, 1, LANES), pair_const3),
            pl.BlockSpec((1, 2, RET_CHUNK, RET_CHUNK), lambda b, j: (j, 0, 0, 0)),
            pl.BlockSpec((1, RET_CHUNK, LANES), pair_const3),
            pl.BlockSpec((1, RET_CHUNK, LANES), pair_const3),
            pl.BlockSpec((1, 1, LANES), pair_const3),
        ],
        out_specs=pl.BlockSpec((seq, LANES), lambda b, j: (b, j)),
        out_shape=jax.ShapeDtypeStruct((batch * seq, RET_PAIRS * LANES), BF16),
        scratch_shapes=[pltpu.VMEM((seq // RET_CHUNK, LANES, LANES), F32),
                        pltpu.VMEM((seq // RET_CHUNK, LANES, LANES), BF16)],
        compiler_params=pltpu.CompilerParams(
            dimension_semantics=("parallel", "arbitrary"), vmem_limit_bytes=VMEM_LIMIT),
        name="retention",
    )(ret, ret, ret, rgate, gn, intra, rd, wd, cd)


def _out_mlp_kernel(x_ref, nsa_ref, mla_ref, ret_ref, wo_ref, g2_ref, wu_ref, wd_ref, gf_ref, o_ref,
                    *, final_norm):
    full = 2 * LANES
    mixed = jnp.concatenate([nsa_ref[...], mla_ref[:, :full], mla_ref[:, full:] + ret_ref[:, full:],
                             ret_ref[:, :full]], axis=1)
    x = x_ref[...] + _nn(mixed, wo_ref[...])
    h = _rms(x, g2_ref[...]).astype(BF16)
    y = x
    for c in range(D_FF // MLP_FF_CHUNK):
        sl = slice(c * MLP_FF_CHUNK, (c + 1) * MLP_FF_CHUNK)
        u = jnp.maximum(_nn(h, wu_ref[:, sl]), 0.0)
        y = y + _nn((u * u).astype(BF16), wd_ref[sl, :])
    if final_norm:
        y = _rms(y, gf_ref[...])
    o_ref[...] = y


def _out_mlp(x2, o_nsa, o_mla, o_ret, wo, g2, wu, wd, layer, gf, final_norm):
    t = x2.shape[0]
    w = RET_PAIRS * LANES
    row = lambda i: (i, 0)
    const = lambda i: (0, 0)
    resident = dict(pipeline_mode=pl.Buffered(1))
    return pl.pallas_call(
        functools.partial(_out_mlp_kernel, final_norm=final_norm),
        grid=(t // MLP_TM,),
        in_specs=[
            pl.BlockSpec((MLP_TM, D_MODEL), row),
            pl.BlockSpec((MLP_TM, w), row),
            pl.BlockSpec((MLP_TM, w), row),
            pl.BlockSpec((MLP_TM, w), row),
            _layer_spec(wo, layer, **resident),
            pl.BlockSpec((1, D_MODEL), const),
            _layer_spec(wu, layer, **resident),
            _layer_spec(wd, layer, **resident),
            pl.BlockSpec((1, D_MODEL), const),
        ],
        out_specs=pl.BlockSpec((MLP_TM, D_MODEL), row),
        out_shape=jax.ShapeDtypeStruct((t, D_MODEL), F32),
        compiler_params=pltpu.CompilerParams(
            dimension_semantics=("parallel",), vmem_limit_bytes=VMEM_LIMIT),
        name="out_mlp",
    )(x2, o_nsa, o_mla, o_ret, wo, g2, wu, wd, gf)


ROPE_KINDS = (
    (PARTIAL_ROPE_DIM, ROPE_THETA, HEAD_DIM, 0),
    (MLA_ROPE_DIM, ROPE_THETA, LANES, HALF),
    (HEAD_DIM, RET_THETA, HEAD_DIM, 0),
)
ROPE_TM = 2048


def _rope_placement():
    n_angles = sum(dim // 2 for dim, _, _, _ in ROPE_KINDS)
    assert 2 * n_angles <= LANES
    place = np.zeros((LANES, 3 * LANES * len(ROPE_KINDS)), np.float32)
    fill = np.zeros((1, 3 * LANES * len(ROPE_KINDS)), np.float32)
    row0 = 0
    for kind, (dim, _, period, base) in enumerate(ROPE_KINDS):
        half = dim // 2
        col0 = kind * 3 * LANES
        for lane in range(LANES):
            rel = (lane - base) % period
            first = lane >= base and rel < half
            second = lane >= base and half <= rel < dim
            if first or second:
                angle = rel if first else rel - half
                place[row0 + angle, col0 + lane] = 1.0
                place[n_angles + row0 + angle, col0 + (1 if first else 2) * LANES + lane] = -1.0 if first else 1.0
            else:
                fill[0, col0 + lane] = 1.0
        row0 += half
    return jnp.asarray(place, BF16), jnp.asarray(fill)


def _rope_kernel(cs_ref, place_ref, fill_ref, *out_refs):
    x = cs_ref[...]
    x1 = x.astype(BF16)
    r1 = x - x1.astype(F32)
    x2 = r1.astype(BF16)
    x3 = (r1 - x2.astype(F32)).astype(BF16)
    place = place_ref[...]
    tab = _tn(x1, place) + _tn(x2, place) + _tn(x3, place) + fill_ref[...]
    for i, ref in enumerate(out_refs):
        ref[...] = tab[:, i * 3 * LANES:(i + 1) * 3 * LANES]


def _rope_tables(positions):
    inv = jnp.concatenate([1.0 / (theta ** (jnp.arange(0, dim, 2, dtype=F32) / dim))
                           for dim, theta, _, _ in ROPE_KINDS])
    ang = inv[:, None] * positions.reshape(-1).astype(F32)[None, :]
    compact = jnp.concatenate([jnp.cos(ang), jnp.sin(ang)], axis=0)
    compact = jnp.pad(compact, ((0, LANES - compact.shape[0]), (0, 0)))
    place, fill = _rope_placement()
    t = compact.shape[1]
    tab_shape = jax.ShapeDtypeStruct((t, 3 * LANES), F32)
    return pl.pallas_call(
        _rope_kernel,
        grid=(t // ROPE_TM,),
        in_specs=[pl.BlockSpec((LANES, ROPE_TM), lambda i: (0, i)),
                  pl.BlockSpec(place.shape, lambda i: (0, 0)),
                  pl.BlockSpec(fill.shape, lambda i: (0, 0))],
        out_specs=tuple(pl.BlockSpec((ROPE_TM, 3 * LANES), lambda i: (i, 0)) for _ in ROPE_KINDS),
        out_shape=tuple(tab_shape for _ in ROPE_KINDS),
        compiler_params=pltpu.CompilerParams(
            dimension_semantics=("parallel",), vmem_limit_bytes=VMEM_LIMIT),
        name="rope_tables",
    )(compact, place, fill)


def _in_weight(w_in):
    offs = np.cumsum((0,) + IN_SIZES)
    (nsa_q, k_cmp, v_cmp, k_slc, v_slc, k_win, v_win, gate,
     cq, ckv, kpe, ret_q, ret_k, ret_v, ret_g) = [np.arange(offs[i], offs[i + 1]) for i in range(len(IN_SIZES))]
    scale = HEAD_DIM ** -0.5
    head = lambda cols, h: cols[h * HEAD_DIM:(h + 1) * HEAD_DIM]
    pad = lambda n: None if n == 0 else -np.ones(n, np.int64)
    pieces = []
    for j in range(NSA_REP):
        pieces += [(head(nsa_q, j), scale), (head(nsa_q, j + NSA_REP), scale)]
    pieces += [(c, 1.0) for c in (k_slc, k_win, k_cmp, v_cmp, cq, ckv)]
    pieces += [(pad(HALF), 0.0), (kpe, 1.0), (pad(LANES - HALF - MLA_ROPE_DIM), 0.0)]
    slots = lambda cols, f, fill: [fill if h is None else (head(cols, h), f) for h in RET_SLOTS]
    zero = (pad(HEAD_DIM), 0.0)
    pieces += slots(ret_q, 1.0, (head(ret_k, RET_SLOTS[-1]), scale))
    pieces += slots(ret_k, scale, zero)[:2 * RET_K_TILES]
    pieces += slots(ret_v, 1.0, zero) + slots(ret_g, 1.0, zero)
    t_pieces = [(v_slc, 1.0), (v_win, 1.0), (gate, 1.0), (pad(GATE_ROWS - NSA_GATE_W), 0.0)]

    t_pieces += [(pad(RET_PAIRS * LANES - NSA_T_ROWS), 0.0)]
    spans = lambda parts: tuple((None if c[0] < 0 else int(c[0]), len(c), f) for c, f in parts)
    assert sum(n for _, n, _ in spans(pieces)) == N_PAD
    nl, d_model, n_in = w_in.shape
    return pl.pallas_call(
        functools.partial(_relayout_kernel, pieces=spans(pieces), t_pieces=spans(t_pieces)),
        grid=(nl, d_model // RELAYOUT_TM),
        in_specs=[pl.BlockSpec((None, RELAYOUT_TM, n_in), lambda l, i: (l, i, 0))],
        out_specs=(pl.BlockSpec((None, RELAYOUT_TM, N_PAD), lambda l, i: (l, i, 0)),
                   pl.BlockSpec((None, NSA_T_ROWS, RELAYOUT_TM), lambda l, i: (l, 0, i))),
        out_shape=(jax.ShapeDtypeStruct((nl, d_model, N_PAD), BF16),
                   jax.ShapeDtypeStruct((nl, NSA_T_ROWS, d_model), BF16)),
        compiler_params=pltpu.CompilerParams(
            dimension_semantics=("parallel", "parallel"), vmem_limit_bytes=VMEM_LIMIT),
        name="in_weight_layout",
    )(w_in)


RELAYOUT_TM = 256


def _relayout_kernel(w_ref, o_ref, ot_ref, *, pieces, t_pieces):
    w = w_ref[...]

    def build(parts):
        cols = []
        for src, n, factor in parts:
            if src is None:
                cols.append(jnp.zeros((w.shape[0], n), F32))
            else:
                cols.append(w[:, src:src + n] if factor == 1.0 else w[:, src:src + n] * factor)
        return jnp.concatenate(cols, axis=1)

    o_ref[...] = build(pieces).astype(BF16)
    ot_ref[...] = build(t_pieces).T[0:NSA_T_ROWS].astype(BF16)


def _compress_weights(pos, w1, w2):
    nl = pos.shape[0]
    g, dh, hid = NSA_KV_GROUPS, HEAD_DIM, CMP_HIDDEN
    assert g == 2
    p = jnp.tile(pos, (1, 1, g))

    def block_diag2(w):
        lead = ((0, 0),) * (w.ndim - 1)
        return jnp.concatenate([jnp.pad(w, lead + ((0, w.shape[-1]),)),
                                jnp.pad(w, lead + ((w.shape[-1], 0),))], axis=-2)

    w = block_diag2(w1.astype(BF16).reshape(nl, CMP_LEN, dh, hid))
    return p, w, block_diag2(w2.astype(BF16))


def _mla_weights(w_uq, w_ukv):
    nl = w_uq.shape[0]
    dq = MLA_NOPE_DIM + MLA_ROPE_DIM
    wq = w_uq.reshape(nl, MLA_Q_RANK, MLA_HEADS, dq)
    pe = wq[..., MLA_NOPE_DIM:]
    half = MLA_ROPE_DIM // 2
    rot = jnp.concatenate([jnp.zeros_like(wq[..., :MLA_NOPE_DIM]), -pe[..., half:], pe[..., :half]], axis=-1)
    pad = lambda w: jnp.pad(w, ((0, 0), (0, 0), (0, 0), (0, LANES - dq))).reshape(
        nl, MLA_Q_RANK, MLA_HEADS * LANES)
    wq = jnp.concatenate([pad(wq), pad(rot)], axis=-1)
    wkv = w_ukv.reshape(nl, MLA_KV_RANK, MLA_HEADS, MLA_NOPE_DIM + MLA_V_DIM)
    wk = jnp.pad(wkv[..., :MLA_NOPE_DIM], ((0, 0), (0, 0), (0, 0), (0, LANES - MLA_NOPE_DIM)))
    wk = wk.reshape(nl, MLA_KV_RANK, MLA_HEADS * LANES)
    wvt = wkv[..., MLA_NOPE_DIM:].reshape(nl, MLA_KV_RANK, MLA_V_ROWS).transpose(0, 2, 1)
    return wq.astype(BF16), wk.astype(BF16), wvt.astype(BF16)


def _out_weight(w_out):
    nl = w_out.shape[0]
    nsa = w_out[:, :NSA_Q_W].reshape(nl, NSA_HEADS, HEAD_DIM, D_MODEL)
    order = [h for j in range(NSA_REP) for h in (j, j + NSA_REP)]
    nsa = nsa[:, order].reshape(nl, NSA_Q_W, D_MODEL)
    mla = w_out[:, NSA_Q_W:NSA_Q_W + MLA_HEADS * MLA_V_DIM]
    ret = w_out[:, NSA_Q_W + MLA_HEADS * MLA_V_DIM:]
    full = 2 * LANES
    return jnp.concatenate([nsa, mla[:, :full], mla[:, full:], ret[:, full:], ret[:, :full]],
                           axis=1).astype(BF16)


def _retention_tables(gn_gain):
    nh = 2 * RET_PAIRS
    slot_head = np.array([RET_HEADS if h is None else h for h in RET_SLOTS])
    log_g = jnp.log(1.0 - 2.0 ** (-5.0 - jnp.asarray(slot_head, F32)))
    i = jnp.arange(RET_CHUNK, dtype=F32)
    diff = i[:, None] - i[None, :]
    intra = jnp.where(diff >= 0, jnp.exp(jnp.maximum(diff, 0.0)[None] * log_g[:, None, None]), 0.0)
    read_decay = jnp.exp((i + 1.0)[None, :] * log_g[:, None])
    write_decay = jnp.exp((RET_CHUNK - 1.0 - i)[None, :] * log_g[:, None])
    chunk_decay = jnp.exp(RET_CHUNK * log_g)

    def lanes(t):
        t = t.reshape(RET_PAIRS, 2, -1)
        return jnp.repeat(t.transpose(0, 2, 1), HALF, axis=-1)

    gn = jnp.pad(gn_gain, ((0, 0), (0, nh - RET_HEADS), (0, 0)))[:, slot_head]
    gn = gn.reshape(gn.shape[0], RET_PAIRS, 1, LANES)
    return (gn, intra.reshape(RET_PAIRS, 2, RET_CHUNK, RET_CHUNK), lanes(read_decay), lanes(write_decay),
            lanes(chunk_decay[:, None]))


def _selection_overlap(seq):
    n_cmp = (seq - CMP_LEN) // CMP_STRIDE + 1
    n_sel = seq // SEL_BLOCK
    cs = np.arange(n_cmp) * CMP_STRIDE
    ss = np.arange(n_sel) * SEL_BLOCK
    ov = np.clip(np.minimum(cs[:, None] + CMP_LEN, ss[None, :] + SEL_BLOCK)
                 - np.maximum(cs[:, None], ss[None, :]), 0, None) / CMP_LEN
    ovl_t = np.zeros((n_sel, seq // CMP_STRIDE), np.float32)
    ovl_t[:, :n_cmp] = ov.T
    return jnp.asarray(ovl_t, BF16)


def kernel(x, positions, ln1_gain, w_in, cmp_pos_k, cmp_w1_k, cmp_w2_k, cmp_pos_v, cmp_w1_v, cmp_w2_v,
           mla_q_norm, mla_w_uq, mla_kv_norm, mla_w_ukv, ret_gn_gain, w_out, ln2_gain, w_up, w_down,
           final_gain):
    batch, seq, _ = x.shape
    depth = w_in.shape[0]
    t = batch * seq

    tab_n, tab_m, tab_r = _rope_tables(positions)

    w_in_p, w_in_t = _in_weight(w_in)
    pos_k, w1_k, w2_k = _compress_weights(cmp_pos_k, cmp_w1_k, cmp_w2_k)
    pos_v, w1_v, w2_v = _compress_weights(cmp_pos_v, cmp_w1_v, cmp_w2_v)
    cmp_pos = jnp.stack([pos_k, pos_v], axis=1)
    cmp_w1 = jnp.stack([w1_k, w1_v], axis=1)
    w2_vt = w2_v.transpose(0, 2, 1)
    wq, wk, wvt = _mla_weights(mla_w_uq, mla_w_ukv)
    wo = _out_weight(w_out)
    wu = w_up.astype(BF16)
    wd = w_down.astype(BF16)
    gn, intra, rd, wdec, cd = _retention_tables(ret_gn_gain)
    ovl_t = _selection_overlap(seq)
    gf = final_gain.reshape(1, D_MODEL)

    x2 = x.reshape(t, D_MODEL)
    for l in range(depth):
        nsa, nsa_t, gate_t, k_cmp, v_cmp, ret, rgate, q_m, k_m, vt_m = _in_proj(
            x2, ln1_gain[l].reshape(1, D_MODEL), w_in_p, w_in_t, mla_q_norm[l].reshape(1, -1),
            mla_kv_norm[l].reshape(1, -1), wq, wk, wvt, l, tab_n, tab_m, tab_r)
        kc, vct = _compress(k_cmp, v_cmp, cmp_pos[l], cmp_w1, w2_k[l], w2_vt[l], l, batch, seq)
        o_nsa = _nsa_attention(nsa, nsa_t, gate_t, kc, vct, ovl_t, batch, seq)
        o_mla = _mla_attention(q_m, k_m, vt_m, batch, seq)
        o_ret = _retention(ret, rgate, gn[l], intra, rd, wdec, cd, batch, seq)
        x2 = _out_mlp(x2, o_nsa, o_mla, o_ret, wo, ln2_gain[l].reshape(1, D_MODEL), wu, wd, l, gf,
                      final_norm=(l == depth - 1))
    return x2.reshape(batch, seq, D_MODEL)
```

```python
import functools
import math

import numpy as np
import jax
import jax.numpy as jnp
from jax import lax
from jax.experimental import pallas as pl
from jax.experimental.pallas import tpu as pltpu

F32 = jnp.float32
BF16 = jnp.bfloat16

D_MODEL = 1024
HEAD_DIM = 64
NSA_HEADS = 6
NSA_KV_GROUPS = 2
NSA_REP = NSA_HEADS // NSA_KV_GROUPS
N_BRANCH = 3
CMP_LEN = 32
CMP_STRIDE = 16
CMP_HIDDEN = 2 * HEAD_DIM
SEL_BLOCK = 64
SEL_TOP_N = 16
WINDOW = 512
MLA_HEADS = 5
MLA_Q_RANK = 256
MLA_KV_RANK = 128
MLA_NOPE_DIM = 64
MLA_ROPE_DIM = 32
MLA_V_DIM = 64
RET_HEADS = 5
RET_CHUNK = 128
ROPE_THETA = 500000.0
PARTIAL_ROPE_DIM = HEAD_DIM // 4
RET_THETA = 10000.0
D_FF = 4 * D_MODEL
NORM_EPS = 1e-6
NEG_INF = -1e30
FORCE_SCORE = 1e9
LOG2E = math.log2(math.e)

NSA_Q_W = NSA_HEADS * HEAD_DIM
NSA_KV_W = NSA_KV_GROUPS * HEAD_DIM
NSA_GATE_W = NSA_HEADS * N_BRANCH
RET_W = RET_HEADS * HEAD_DIM
IN_SIZES = (NSA_Q_W, NSA_KV_W, NSA_KV_W, NSA_KV_W, NSA_KV_W, NSA_KV_W, NSA_KV_W, NSA_GATE_W,
            MLA_Q_RANK, MLA_KV_RANK, MLA_ROPE_DIM, RET_W, RET_W, RET_W, RET_W)

LANES = 128
HALF = LANES // 2
SUBLANES = 8
MXU_WIDTH = 256
VMEM_LIMIT = 56 * 1024 * 1024

NSA_TILES = 5
GATE_ROWS = 32
NSA_T_ROWS = 2 * LANES + GATE_ROWS
MLA_IN_TILES = 4
RET_PAIRS = 3
RET_SLOTS = (0, 1, 2, 3, None, 4)
RET_K_TILES = RET_PAIRS - 1
RET_ROPE_TILES = RET_PAIRS + RET_K_TILES
RET_MM_TILES = RET_ROPE_TILES + RET_PAIRS
RET_TILES = 3 * RET_PAIRS
IN_TILES = NSA_TILES + 2 + MLA_IN_TILES + RET_MM_TILES + RET_PAIRS
N_PAD = IN_TILES * LANES
MLA_V_ROWS = MLA_HEADS * MLA_V_DIM

IN_TM = 1024
IN_CHUNK_TILES = 4 * MXU_WIDTH // LANES
MLP_TM = 1024
ATT_T = 512
MLP_FF_CHUNK = 512


def _nn(a, b):
    return jnp.dot(a, b, preferred_element_type=F32)


def _nt(a, b):
    return lax.dot_general(a, b, (((1,), (1,)), ((), ())), preferred_element_type=F32)


def _tn(a, b):
    return lax.dot_general(a, b, (((0,), (0,)), ((), ())), preferred_element_type=F32)


def _rms(x, gain):
    return x * lax.rsqrt(jnp.mean(x * x, axis=-1, keepdims=True) + NORM_EPS) * gain


def _rope(val, tab, half):
    cos = tab[:, 0:LANES]
    sin_a = tab[:, LANES:2 * LANES]
    sin_b = tab[:, 2 * LANES:3 * LANES]
    return (val * cos + pltpu.roll(val, LANES - half, 1) * sin_a
            + pltpu.roll(val, half, 1) * sin_b)


def _lane_lo(shape):
    return lax.broadcasted_iota(jnp.int32, shape, len(shape) - 1) < HALF


def _mla_up(c, qn_ref, kvn_ref, wq_ref, wk_ref, wvt_ref, tm_ref, q_ref, k_ref, vt_ref):
    scale = (MLA_NOPE_DIM + MLA_ROPE_DIM) ** -0.5 * LOG2E
    cq = _rms(c[:, 0:MLA_Q_RANK], qn_ref[...]).astype(BF16)
    ckv = _rms(c[:, MLA_Q_RANK:MLA_Q_RANK + MLA_KV_RANK], kvn_ref[...]).astype(BF16)
    k_pe = _rope(c[:, 3 * LANES:4 * LANES], tm_ref[...], MLA_ROPE_DIM // 2)
    cos = tm_ref[:, 0:LANES] * scale
    sin = (tm_ref[:, 2 * LANES:3 * LANES] - tm_ref[:, LANES:2 * LANES]) * scale
    q = _nn(cq, wq_ref[...])
    k = _nn(ckv, wk_ref[...])
    w = MLA_HEADS * LANES
    for hd in range(MLA_HEADS):
        sl = slice(hd * LANES, (hd + 1) * LANES)
        rot = slice(w + hd * LANES, w + (hd + 1) * LANES)
        q_ref[:, sl] = (q[:, sl] * cos + q[:, rot] * sin).astype(BF16)
        k_ref[:, sl] = (k[:, sl] + k_pe).astype(BF16)
    vt_ref[...] = _nt(wvt_ref[...], ckv).astype(BF16)


def _inproj_kernel(x_ref, g_ref, w_ref, wt_ref, tn_ref, tm_ref, tr_ref, qn_ref, kvn_ref, wq_ref, wk_ref,
                   wvt_ref, nsa_ref, nsat_ref, gatet_ref, kcmp_ref, vcmp_ref, ret_ref, rgate_ref,
                   mq_ref, mk_ref, mvt_ref):
    h = _rms(x_ref[...], g_ref[...]).astype(BF16)
    tab_n = tn_ref[...]
    tab_r = tr_ref[...]

    chunks = [_nn(h, w_ref[:, c:min(c + IN_CHUNK_TILES * LANES, N_PAD)])
              for c in range(0, N_PAD, IN_CHUNK_TILES * LANES)]

    def tiles(first, n):
        cols = [chunks[i // IN_CHUNK_TILES][:, (i % IN_CHUNK_TILES) * LANES:(i % IN_CHUNK_TILES + 1) * LANES]
                for i in range(first, first + n)]
        return cols[0] if n == 1 else jnp.concatenate(cols, axis=1)

    t0 = 0
    for i in range(NSA_TILES):
        v = _rope(tiles(t0 + i, 1), tab_n, PARTIAL_ROPE_DIM // 2)
        if i < NSA_REP:
            v = v * LOG2E
        nsa_ref[:, i * LANES:(i + 1) * LANES] = v.astype(BF16)
    t0 += NSA_TILES
    kcmp_ref[...] = _rope(tiles(t0, 1), tab_n, PARTIAL_ROPE_DIM // 2)
    vcmp_ref[...] = tiles(t0 + 1, 1)
    t0 += 2
    _mla_up(tiles(t0, MLA_IN_TILES), qn_ref, kvn_ref, wq_ref, wk_ref, wvt_ref, tm_ref, mq_ref, mk_ref, mvt_ref)
    t0 += MLA_IN_TILES
    stored = []
    for i in range(RET_MM_TILES):
        v = tiles(t0 + i, 1)
        if i < RET_ROPE_TILES:
            v = _rope(v, tab_r, HEAD_DIM // 2)
        stored.append(v)
        if i == RET_ROPE_TILES - 1:
            stored.append(pltpu.roll(stored[RET_PAIRS - 1], HALF, 1))
    for i, v in enumerate(stored):
        ret_ref[:, i * LANES:(i + 1) * LANES] = v.astype(BF16)
    t0 += RET_MM_TILES
    rgate_ref[...] = tiles(t0, RET_PAIRS)
    at = _nt(wt_ref[...], h)
    nsat_ref[...] = at[0:2 * LANES, :].astype(BF16)
    gatet_ref[...] = at[2 * LANES:NSA_T_ROWS, :]


def _layer_spec(w, layer, **kwargs):
    zeros = (0,) * (w.ndim - 1)
    return pl.BlockSpec((None,) + w.shape[1:], lambda *_: (layer,) + zeros, **kwargs)


def _in_proj(x2, gain, w, wt, q_norm, kv_norm, wq, wk, wvt, layer, tab_n, tab_m, tab_r):
    t = x2.shape[0]
    row = lambda i: (i, 0)
    col = lambda i: (0, i)
    const = lambda i: (0, 0)
    out_shapes = (
        jax.ShapeDtypeStruct((t, NSA_TILES * LANES), BF16),
        jax.ShapeDtypeStruct((2 * LANES, t), BF16),
        jax.ShapeDtypeStruct((GATE_ROWS, t), F32),
        jax.ShapeDtypeStruct((t, LANES), F32),
        jax.ShapeDtypeStruct((t, LANES), F32),
        jax.ShapeDtypeStruct((t, RET_TILES * LANES), BF16),
        jax.ShapeDtypeStruct((t, RET_PAIRS * LANES), F32),
        jax.ShapeDtypeStruct((t, MLA_HEADS * LANES), BF16),
        jax.ShapeDtypeStruct((t, MLA_HEADS * LANES), BF16),
        jax.ShapeDtypeStruct((MLA_V_ROWS, t), BF16),
    )
    out_specs = tuple(
        pl.BlockSpec((s.shape[0], IN_TM), col) if s.shape[1] == t else pl.BlockSpec((IN_TM, s.shape[1]), row)
        for s in out_shapes)
    resident = dict(pipeline_mode=pl.Buffered(1))
    return pl.pallas_call(
        _inproj_kernel,
        grid=(t // IN_TM,),
        in_specs=[
            pl.BlockSpec((IN_TM, D_MODEL), row),
            pl.BlockSpec((1, D_MODEL), const),
            _layer_spec(w, layer, **resident),
            _layer_spec(wt, layer, **resident),
            pl.BlockSpec((IN_TM, 3 * LANES), row),
            pl.BlockSpec((IN_TM, 3 * LANES), row),
            pl.BlockSpec((IN_TM, 3 * LANES), row),
            pl.BlockSpec((1, MLA_Q_RANK), const),
            pl.BlockSpec((1, MLA_KV_RANK), const),
            _layer_spec(wq, layer, **resident),
            _layer_spec(wk, layer, **resident),
            _layer_spec(wvt, layer, **resident),
        ],
        out_specs=out_specs,
        out_shape=out_shapes,
        compiler_params=pltpu.CompilerParams(
            dimension_semantics=("parallel",), vmem_limit_bytes=VMEM_LIMIT),
        name="in_proj",
    )(x2, gain, w, wt, tab_n, tab_m, tab_r, q_norm, kv_norm, wq, wk, wvt)


def _compress_kernel(k_ref, v_ref, pos_ref, w1_ref, w2k_ref, w2vt_ref, kc_ref, vct_ref):
    n_blk = k_ref.shape[0] // CMP_STRIDE

    def hidden(src, i):
        toks = [src[pl.ds(r, n_blk, stride=CMP_STRIDE), :] for r in range(CMP_STRIDE)]
        halves = []
        for half in range(CMP_LEN // CMP_STRIDE):
            off = half * CMP_STRIDE
            x = jnp.concatenate([(toks[r] + pos_ref[i, off + r:off + r + 1, :]).astype(BF16)
                                 for r in range(CMP_STRIDE)], axis=1)
            w = w1_ref[i, off:off + CMP_STRIDE].reshape(CMP_STRIDE * LANES, NSA_KV_GROUPS * CMP_HIDDEN)
            halves.append(_nn(x, w))
        return jax.nn.gelu(halves[0] + pltpu.roll(halves[1], n_blk - 1, 0)).astype(BF16)

    kc_ref[0] = _nn(hidden(k_ref, 0), w2k_ref[...]).astype(BF16)
    vct_ref[0] = _nt(w2vt_ref[...], hidden(v_ref, 1)).astype(BF16)


def _compress(k_cmp, v_cmp, pos, w1, w2k, w2vt, layer, batch, seq):
    b = batch
    n_blk = seq // CMP_STRIDE
    return pl.pallas_call(
        _compress_kernel,
        grid=(b,),
        in_specs=[
            pl.BlockSpec((seq, LANES), lambda i: (i, 0)),
            pl.BlockSpec((seq, LANES), lambda i: (i, 0)),
            pl.BlockSpec(pos.shape, lambda i: (0, 0, 0)),
            _layer_spec(w1, layer),
            pl.BlockSpec(w2k.shape, lambda i: (0, 0)),
            pl.BlockSpec(w2vt.shape, lambda i: (0, 0)),
        ],
        out_specs=(pl.BlockSpec((1, n_blk, LANES), lambda i: (i, 0, 0)),
                   pl.BlockSpec((1, LANES, n_blk), lambda i: (i, 0, 0))),
        out_shape=(jax.ShapeDtypeStruct((b, n_blk, LANES), BF16),
                   jax.ShapeDtypeStruct((b, LANES, n_blk), BF16)),
        compiler_params=pltpu.CompilerParams(
            dimension_semantics=("parallel",), vmem_limit_bytes=VMEM_LIMIT),
        name="nsa_compress",
    )(k_cmp, v_cmp, pos, w1, w2k, w2vt)


SCORE_LOOKAHEAD = 2
ONES_ROWS = 16


def _tri_scores(k, q, dead_upper, mask, bias=None):
    half = k.shape[0] // 2
    dead = jnp.full((half, half), NEG_INF, F32)
    if dead_upper:
        top = _nt(k[:half], q)
        bottom = jnp.concatenate([dead, _nt(k[half:], q[half:])], axis=1)
    else:
        top = jnp.concatenate([_nt(k[:half], q[:half]), dead], axis=1)
        bottom = _nt(k[half:], q)
    s_t = jnp.concatenate([top, bottom], axis=0)
    return jnp.where(mask, s_t if bias is None else s_t + bias, NEG_INF)


def _tri_pv(dead_upper, v_ext, p):
    half = p.shape[0] // 2
    if dead_upper:
        full = _nn(v_ext[:, :half], p[:half])
        part = _nn(v_ext[:, half:], p[half:, half:])
        return jnp.concatenate([full[:, :half], full[:, half:] + part], axis=1)
    part = _nn(v_ext[:, :half], p[:half, :half])
    full = _nn(v_ext[:, half:], p[half:])
    return jnp.concatenate([full[:, :half] + part, full[:, half:]], axis=1)


def _softmax_steps(score_fns, v_ts, states, pv_fn=_nn):
    n = len(score_fns)
    s_ts = [score_fns[h]() if h < SCORE_LOOKAHEAD else None for h in range(n)]
    out = []
    for h in range(n):
        if h + SCORE_LOOKAHEAD < n:
            s_ts[h + SCORE_LOOKAHEAD] = score_fns[h + SCORE_LOOKAHEAD]()
        m_old, acc_old = states[h]
        m = jnp.maximum(m_old, jnp.max(s_ts[h], axis=0, keepdims=True))
        p = jnp.exp2(s_ts[h] - m).astype(BF16)
        s_ts[h] = None
        v_ext = jnp.concatenate([v_ts[h], jnp.ones((ONES_ROWS, v_ts[h].shape[1]), BF16)], axis=0)
        out.append((m, jnp.exp2(m_old - m) * acc_old + pv_fn(v_ext, p)))
    return tuple(out)


def _softmax_init(cols):
    return jnp.full((1, cols), NEG_INF, F32), jnp.zeros((HEAD_DIM + ONES_ROWS, cols), F32)


def _softmax_finish(state):
    acc = state[1]
    return acc[0:HEAD_DIM] * (1.0 / jnp.maximum(acc[HEAD_DIM:HEAD_DIM + 1], 1e-30))


def _nsa_kernel(q_ref, gatet_ref, kc_ref, vct_ref, ks_ref, kw_ref, vst_ref, vwt_ref, ovl_ref,
                o_ref, out_t_ref, selb_ref):
    tq = ATT_T
    qi = pl.program_id(1)
    q0 = pl.multiple_of(qi * tq, tq)
    lo1 = _lane_lo((1, LANES))
    group_lanes = (lo1, jnp.logical_not(lo1))
    gate = jax.nn.sigmoid(gatet_ref[...])
    heads = range(NSA_HEADS)

    def q_head(h):
        j, g = h % NSA_REP, h // NSA_REP
        tile = q_ref[:, j * LANES:(j + 1) * LANES]
        return jnp.where(group_lanes[g], tile, jnp.zeros_like(tile))

    def v_rows(ref, h, k0, n):
        g = h // NSA_REP
        return ref[g * HEAD_DIM:(g + 1) * HEAD_DIM, pl.ds(k0, n)]

    def emit(h, branch, qs, nq, o_t):
        j, g = h % NSA_REP, h // NSA_REP
        r0 = j * LANES + g * HEAD_DIM
        row = h * N_BRANCH + branch
        val = gate[row:row + 1, qs:qs + nq] * o_t
        if branch == 0:
            out_t_ref[r0:r0 + HEAD_DIM, qs:qs + nq] = val
        else:
            out_t_ref[r0:r0 + HEAD_DIM, qs:qs + nq] += val

    qm = [q_head(h) for h in heads]

    n_cmp_pad = kc_ref.shape[1]
    kc = kc_ref[0]
    vct = vct_ref[0]
    n_i = lax.broadcasted_iota(jnp.int32, (n_cmp_pad, tq), 0)
    t_l = q0 + lax.broadcasted_iota(jnp.int32, (n_cmp_pad, tq), 1)
    cmask = (n_i * CMP_STRIDE + (CMP_LEN - 1)) <= t_l
    cmask_f = cmask.astype(F32)
    s_cs = [jnp.where(cmask, _nt(kc, qm[h]), NEG_INF) for h in heads]
    p_cs = [jnp.exp2(s_t - jnp.max(s_t, axis=0, keepdims=True)) * cmask_f for s_t in s_cs]
    p_cs = [p * (1.0 / jnp.maximum(jnp.sum(p, axis=0, keepdims=True), 1e-30)) for p in p_cs]
    for h in heads:
        g = h // NSA_REP
        emit(h, 0, 0, tq, _nn(vct[g * HEAD_DIM:(g + 1) * HEAD_DIM, :], p_cs[h].astype(BF16)))

    n_sel = ovl_ref.shape[0]
    needs_rank = q0 + tq > SEL_TOP_N * SEL_BLOCK

    @pl.when(jnp.logical_not(needs_rank))
    def _():
        selb_ref[...] = jnp.zeros_like(selb_ref)

    @pl.when(needs_rank)
    def _():
        m_i = lax.broadcasted_iota(jnp.int32, (n_sel, tq), 0)
        cur = jnp.right_shift(q0 + lax.broadcasted_iota(jnp.int32, (n_sel, tq), 1),
                              SEL_BLOCK.bit_length() - 1)
        valid = m_i <= cur
        forced = (m_i == 0) | (m_i == cur) | (m_i == cur - 1)
        ovl = ovl_ref[...]
        sub = SUBLANES
        m_loc = lax.broadcasted_iota(jnp.int32, (sub, tq), 0)
        for g in range(NSA_KV_GROUPS):
            psum = p_cs[g * NSA_REP]
            for h in range(g * NSA_REP + 1, (g + 1) * NSA_REP):
                psum = psum + p_cs[h]
            p_hi = psum.astype(BF16)
            p_lo = (psum - p_hi.astype(F32)).astype(BF16)
            imp = _nn(ovl, p_hi) + _nn(ovl, p_lo)
            imp = jnp.where(valid & forced, FORCE_SCORE, imp)
            imp = jnp.where(valid, imp, NEG_INF)
            parts = [imp[i:i + sub] for i in range(0, n_sel, sub)]
            ranks = [jnp.zeros((sub, tq), jnp.int32) for _ in parts]
            for mp in range(n_sel):
                row = imp[mp:mp + 1, :]
                for i, part in enumerate(parts):
                    if i * sub + sub - 1 <= mp:
                        beats = row > part
                    elif i * sub > mp:
                        beats = row >= part
                    else:
                        beats = (row > part) | ((row == part) & (m_loc + i * sub > mp))
                    ranks[i] = ranks[i] + beats.astype(jnp.int32)
            rank = jnp.concatenate(ranks, axis=0)
            selb_ref[g] = jnp.where(rank < SEL_TOP_N, 0.0, NEG_INF)

    blocks_per_chunk = tq // SEL_BLOCK

    def sel_bias(g, c, n_keys, qs, nq):
        rows = [jnp.broadcast_to(selb_ref[g, pl.ds(c * blocks_per_chunk + i, 1), qs:qs + nq], (SEL_BLOCK, nq))
                for i in range(n_keys // SEL_BLOCK)]
        return jnp.concatenate(rows, axis=0)

    key_i = lax.broadcasted_iota(jnp.int32, (tq, tq), 0)
    qry_i = lax.broadcasted_iota(jnp.int32, (tq, tq), 1)
    causal_mask = key_i <= qry_i

    def sel_chunk(c, k0, states, causal):
        k = ks_ref[pl.ds(k0, tq), :]
        bias = [sel_bias(g, c, tq, 0, tq) for g in range(NSA_KV_GROUPS)]

        def score_fn(h):
            if causal:
                return _tri_scores(k, qm[h], True, causal_mask, bias[h // NSA_REP])
            return _nt(k, qm[h]) + bias[h // NSA_REP]

        return _softmax_steps([functools.partial(score_fn, h) for h in heads],
                              [v_rows(vst_ref, h, k0, tq) for h in heads], states,
                              functools.partial(_tri_pv, True) if causal else _nn)

    states = sel_chunk(qi, q0, tuple(_softmax_init(tq) for _ in heads), True)
    states = lax.fori_loop(
        0, qi, lambda c, st: sel_chunk(c, pl.multiple_of(c * tq, tq), st, False), states)
    for h in heads:
        emit(h, 1, 0, tq, _softmax_finish(states[h]))

    def win_chunk(k0, states, mask, dead_upper):
        k = kw_ref[pl.ds(k0, tq), :]
        if dead_upper is None:
            score_fns = [lambda h=h: jnp.where(mask, _nt(k, qm[h]), NEG_INF) for h in heads]
            pv_fn = _nn
        else:
            score_fns = [lambda h=h: _tri_scores(k, qm[h], dead_upper, mask) for h in heads]
            pv_fn = functools.partial(_tri_pv, dead_upper)
        return _softmax_steps(score_fns, [v_rows(vwt_ref, h, k0, tq) for h in heads], states, pv_fn)

    states = win_chunk(q0, tuple(_softmax_init(tq) for _ in heads), causal_mask, True)
    for d in range(1, WINDOW // tq + 1):
        k0 = pl.multiple_of(jnp.maximum(qi - d, 0) * tq, tq)
        in_band = (key_i - d * tq > qry_i - WINDOW) & (qi >= d)
        states = win_chunk(k0, states, in_band, False if d * tq == WINDOW else None)
    for h in heads:
        emit(h, 2, 0, tq, _softmax_finish(states[h]))

    for j in range(NSA_REP):
        o_ref[:, j * LANES:(j + 1) * LANES] = out_t_ref[j * LANES:(j + 1) * LANES, :].T.astype(BF16)


def _nsa_attention(nsa, nsa_t, gate_t, kc, vct, ovl_t, batch, seq):
    tq = ATT_T
    nq = seq // tq
    n_blk = kc.shape[1]
    qrow = lambda b, i: (b * nq + i, 0)
    return pl.pallas_call(
        _nsa_kernel,
        grid=(batch, nq),
        in_specs=[
            pl.BlockSpec((tq, NSA_REP * LANES), qrow),
            pl.BlockSpec((GATE_ROWS, tq), lambda b, i: (0, b * nq + i)),
            pl.BlockSpec((1, n_blk, LANES), lambda b, i: (b, 0, 0)),
            pl.BlockSpec((1, LANES, n_blk), lambda b, i: (b, 0, 0)),
            pl.BlockSpec((seq, LANES), lambda b, i: (b, 3)),
            pl.BlockSpec((seq, LANES), lambda b, i: (b, 4)),
            pl.BlockSpec((LANES, seq), lambda b, i: (0, b)),
            pl.BlockSpec((LANES, seq), lambda b, i: (1, b)),
            pl.BlockSpec(ovl_t.shape, lambda b, i: (0, 0)),
        ],
        out_specs=pl.BlockSpec((tq, NSA_REP * LANES), qrow),
        out_shape=jax.ShapeDtypeStruct((batch * seq, NSA_REP * LANES), BF16),
        scratch_shapes=[pltpu.VMEM((NSA_REP * LANES, tq), F32),
                        pltpu.VMEM((NSA_KV_GROUPS, seq // SEL_BLOCK, tq), F32)],
        compiler_params=pltpu.CompilerParams(
            dimension_semantics=("parallel", "arbitrary"), vmem_limit_bytes=VMEM_LIMIT),
        name="nsa_attention",
    )(nsa, gate_t, kc, vct, nsa, nsa, nsa_t, nsa_t, ovl_t)


def _mla_attn_kernel(q_ref, k_ref, vt_ref, o_ref, out_t_ref):
    tq = ATT_T
    qi = pl.program_id(1)
    q0 = pl.multiple_of(qi * tq, tq)
    heads = range(MLA_HEADS)
    qh = [q_ref[:, h * LANES:(h + 1) * LANES] for h in heads]

    def chunk(k0, states, causal):
        if causal:
            mask = (lax.broadcasted_iota(jnp.int32, (tq, tq), 0)
                    <= lax.broadcasted_iota(jnp.int32, (tq, tq), 1))

        def score_fn(h):
            k = k_ref[pl.ds(k0, tq), h * LANES:(h + 1) * LANES]
            return _tri_scores(k, qh[h], True, mask) if causal else _nt(k, qh[h])

        v_ts = [vt_ref[h * MLA_V_DIM:(h + 1) * MLA_V_DIM, pl.ds(k0, tq)] for h in heads]
        return _softmax_steps([functools.partial(score_fn, h) for h in heads], v_ts, states,
                              functools.partial(_tri_pv, True) if causal else _nn)

    states = chunk(q0, tuple(_softmax_init(tq) for _ in heads), True)
    states = lax.fori_loop(0, qi, lambda c, st: chunk(pl.multiple_of(c * tq, tq), st, False), states)
    for h in heads:
        out_t_ref[h * MLA_V_DIM:(h + 1) * MLA_V_DIM, :] = _softmax_finish(states[h])
    out_t_ref[MLA_V_ROWS:, :] = jnp.zeros((RET_PAIRS * LANES - MLA_V_ROWS, tq), F32)
    for j in range(RET_PAIRS):
        o_ref[:, j * LANES:(j + 1) * LANES] = out_t_ref[j * LANES:(j + 1) * LANES, :].T.astype(BF16)


def _mla_attention(q, k, vt, batch, seq):
    tq = ATT_T
    nq = seq // tq
    return pl.pallas_call(
        _mla_attn_kernel,
        grid=(batch, nq),
        in_specs=[
            pl.BlockSpec((tq, MLA_HEADS * LANES), lambda b, i: (b * nq + i, 0)),
            pl.BlockSpec((seq, MLA_HEADS * LANES), lambda b, i: (b, 0)),
            pl.BlockSpec((MLA_V_ROWS, seq), lambda b, i: (0, b)),
        ],
        out_specs=pl.BlockSpec((tq, RET_PAIRS * LANES), lambda b, i: (b * nq + i, 0)),
        out_shape=jax.ShapeDtypeStruct((batch * seq, RET_PAIRS * LANES), BF16),
        scratch_shapes=[pltpu.VMEM((RET_PAIRS * LANES, tq), F32)],
        compiler_params=pltpu.CompilerParams(
            dimension_semantics=("parallel", "arbitrary"), vmem_limit_bytes=VMEM_LIMIT),
        name="mla_attention",
    )(q, k, vt)


RET_UNROLL = 16


def _retention_kernel(q_ref, k_ref, v_ref, gate_ref, gn_ref, intra_ref, rd_ref, wd_ref, cd_ref, o_ref,
                      kv_ref, st_ref):
    c_len = RET_CHUNK
    n_chunks = q_ref.shape[0] // c_len
    lo = _lane_lo((1, LANES))
    hi = jnp.logical_not(lo)
    blockdiag = (lax.broadcasted_iota(jnp.int32, (LANES, LANES), 0) < HALF) == _lane_lo((LANES, LANES))
    intra_a = intra_ref[0, 0]
    intra_b = intra_ref[0, 1]
    read_decay = rd_ref[0]
    write_decay = wd_ref[0]
    chunk_decay = cd_ref[0]
    gn = gn_ref[0]

    averager = jnp.where(blockdiag, 1.0 / HEAD_DIM, 0.0).astype(BF16)

    def half_mean(x):
        x_hi = x.astype(BF16)
        x_lo = (x - x_hi.astype(F32)).astype(BF16)
        return _nn(x_hi, averager) + _nn(x_lo, averager)

    def kv_body(c, carry):
        r0 = pl.multiple_of(c * c_len, c_len)
        kc = k_ref[pl.ds(r0, c_len), :]
        kv_ref[c] = _tn((kc.astype(F32) * write_decay).astype(BF16), v_ref[pl.ds(r0, c_len), :])
        return carry

    lax.fori_loop(0, n_chunks, kv_body, 0, unroll=RET_UNROLL)

    state = jnp.zeros((LANES, LANES), F32)
    for c in range(n_chunks):
        st_ref[c] = jnp.where(blockdiag, state, 0.0).astype(BF16)
        state = state * chunk_decay + kv_ref[c]

    def out_body(step, carry):
        cs = [step * RET_UNROLL + u for u in range(RET_UNROLL)]
        rows = [pl.ds(pl.multiple_of(c * c_len, c_len), c_len) for c in cs]
        qs = [q_ref[r, :] for r in rows]
        ks = [k_ref[r, :] for r in rows]
        vs = [v_ref[r, :] for r in rows]
        zero = jnp.zeros_like(qs[0])
        sa = [(_nt(jnp.where(lo, q, zero), k) * intra_a).astype(BF16) for q, k in zip(qs, ks)]
        sb = [(_nt(jnp.where(hi, q, zero), k) * intra_b).astype(BF16) for q, k in zip(qs, ks)]
        cross = [_nn(q, st_ref[c]) * read_decay for q, c in zip(qs, cs)]
        os_ = [jnp.where(lo, _nn(a, v), _nn(b, v)) + x for a, b, v, x in zip(sa, sb, vs, cross)]
        ds_ = [o - half_mean(o) for o in os_]
        ys = [d * lax.rsqrt(half_mean(d * d) + NORM_EPS) * gn for d in ds_]
        for r, y in zip(rows, ys):
            o_ref[r, :] = (jax.nn.silu(gate_ref[r, :]) * y).astype(BF16)
        return carry

    lax.fori_loop(0, n_chunks // RET_UNROLL, out_body, 0)


def _retention(ret, rgate, gn, intra, rd, wd, cd, batch, seq):
    pair_const3 = lambda b, j: (j, 0, 0)
    return pl.pallas_call(
        _retention_kernel,
        grid=(batch, RET_PAIRS),
        in_specs=[
            pl.BlockSpec((seq, LANES), lambda b, j: (b, j)),
            pl.BlockSpec((seq, LANES), lambda b, j: (b, RET_PAIRS + j)),
            pl.BlockSpec((seq, LANES), lambda b, j: (b, 2 * RET_PAIRS + j)),
            pl.BlockSpec((seq, LANES), lambda b, j: (b, j)),
            pl.BlockSpec((1, 1, LANES), pair_const3),
            pl.BlockSpec((1, 2, RET_CHUNK, RET_CHUNK), lambda b, j: (j, 0, 0, 0)),
            pl.BlockSpec((1, RET_CHUNK, LANES), pair_const3),
            pl.BlockSpec((1, RET_CHUNK, LANES), pair_const3),
            pl.BlockSpec((1, 1, LANES), pair_const3),
        ],
        out_specs=pl.BlockSpec((seq, LANES), lambda b, j: (b, j)),
        out_shape=jax.ShapeDtypeStruct((batch * seq, RET_PAIRS * LANES), BF16),
        scratch_shapes=[pltpu.VMEM((seq // RET_CHUNK, LANES, LANES), F32),
                        pltpu.VMEM((seq // RET_CHUNK, LANES, LANES), BF16)],
        compiler_params=pltpu.CompilerParams(
            dimension_semantics=("parallel", "arbitrary"), vmem_limit_bytes=VMEM_LIMIT),
        name="retention",
    )(ret, ret, ret, rgate, gn, intra, rd, wd, cd)


def _out_mlp_kernel(x_ref, nsa_ref, mla_ref, ret_ref, wo_ref, g2_ref, wu_ref, wd_ref, gf_ref, o_ref,
                    *, final_norm):
    full = 2 * LANES
    mixed = jnp.concatenate([nsa_ref[...], mla_ref[:, :full], mla_ref[:, full:] + ret_ref[:, full:],
                             ret_ref[:, :full]], axis=1)
    x = x_ref[...] + _nn(mixed, wo_ref[...])
    h = _rms(x, g2_ref[...]).astype(BF16)
    y = x
    for c in range(D_FF // MLP_FF_CHUNK):
        sl = slice(c * MLP_FF_CHUNK, (c + 1) * MLP_FF_CHUNK)
        u = jnp.maximum(_nn(h, wu_ref[:, sl]), 0.0)
        y = y + _nn((u * u).astype(BF16), wd_ref[sl, :])
    if final_norm:
        y = _rms(y, gf_ref[...])
    o_ref[...] = y


def _out_mlp(x2, o_nsa, o_mla, o_ret, wo, g2, wu, wd, layer, gf, final_norm):
    t = x2.shape[0]
    w = RET_PAIRS * LANES
    row = lambda i: (i, 0)
    const = lambda i: (0, 0)
    resident = dict(pipeline_mode=pl.Buffered(1))
    return pl.pallas_call(
        functools.partial(_out_mlp_kernel, final_norm=final_norm),
        grid=(t // MLP_TM,),
        in_specs=[
            pl.BlockSpec((MLP_TM, D_MODEL), row),
            pl.BlockSpec((MLP_TM, w), row),
            pl.BlockSpec((MLP_TM, w), row),
            pl.BlockSpec((MLP_TM, w), row),
            _layer_spec(wo, layer, **resident),
            pl.BlockSpec((1, D_MODEL), const),
            _layer_spec(wu, layer, **resident),
            _layer_spec(wd, layer, **resident),
            pl.BlockSpec((1, D_MODEL), const),
        ],
        out_specs=pl.BlockSpec((MLP_TM, D_MODEL), row),
        out_shape=jax.ShapeDtypeStruct((t, D_MODEL), F32),
        compiler_params=pltpu.CompilerParams(
            dimension_semantics=("parallel",), vmem_limit_bytes=VMEM_LIMIT),
        name="out_mlp",
    )(x2, o_nsa, o_mla, o_ret, wo, g2, wu, wd, gf)


ROPE_KINDS = (
    (PARTIAL_ROPE_DIM, ROPE_THETA, HEAD_DIM, 0),
    (MLA_ROPE_DIM, ROPE_THETA, LANES, HALF),
    (HEAD_DIM, RET_THETA, HEAD_DIM, 0),
)
ROPE_TM = 2048


def _rope_placement():
    n_angles = sum(dim // 2 for dim, _, _, _ in ROPE_KINDS)
    assert 2 * n_angles <= LANES
    place = np.zeros((LANES, 3 * LANES * len(ROPE_KINDS)), np.float32)
    fill = np.zeros((1, 3 * LANES * len(ROPE_KINDS)), np.float32)
    row0 = 0
    for kind, (dim, _, period, base) in enumerate(ROPE_KINDS):
        half = dim // 2
        col0 = kind * 3 * LANES
        for lane in range(LANES):
            rel = (lane - base) % period
            first = lane >= base and rel < half
            second = lane >= base and half <= rel < dim
            if first or second:
                angle = rel if first else rel - half
                place[row0 + angle, col0 + lane] = 1.0
                place[n_angles + row0 + angle, col0 + (1 if first else 2) * LANES + lane] = -1.0 if first else 1.0
            else:
                fill[0, col0 + lane] = 1.0
        row0 += half
    return jnp.asarray(place, BF16), jnp.asarray(fill)


def _rope_kernel(cs_ref, place_ref, fill_ref, *out_refs):
    x = cs_ref[...]
    x1 = x.astype(BF16)
    r1 = x - x1.astype(F32)
    x2 = r1.astype(BF16)
    x3 = (r1 - x2.astype(F32)).astype(BF16)
    place = place_ref[...]
    tab = _tn(x1, place) + _tn(x2, place) + _tn(x3, place) + fill_ref[...]
    for i, ref in enumerate(out_refs):
        ref[...] = tab[:, i * 3 * LANES:(i + 1) * 3 * LANES]


def _rope_tables(positions):
    inv = jnp.concatenate([1.0 / (theta ** (jnp.arange(0, dim, 2, dtype=F32) / dim))
                           for dim, theta, _, _ in ROPE_KINDS])
    ang = inv[:, None] * positions.reshape(-1).astype(F32)[None, :]
    compact = jnp.concatenate([jnp.cos(ang), jnp.sin(ang)], axis=0)
    compact = jnp.pad(compact, ((0, LANES - compact.shape[0]), (0, 0)))
    place, fill = _rope_placement()
    t = compact.shape[1]
    tab_shape = jax.ShapeDtypeStruct((t, 3 * LANES), F32)
    return pl.pallas_call(
        _rope_kernel,
        grid=(t // ROPE_TM,),
        in_specs=[pl.BlockSpec((LANES, ROPE_TM), lambda i: (0, i)),
                  pl.BlockSpec(place.shape, lambda i: (0, 0)),
                  pl.BlockSpec(fill.shape, lambda i: (0, 0))],
        out_specs=tuple(pl.BlockSpec((ROPE_TM, 3 * LANES), lambda i: (i, 0)) for _ in ROPE_KINDS),
        out_shape=tuple(tab_shape for _ in ROPE_KINDS),
        compiler_params=pltpu.CompilerParams(
            dimension_semantics=("parallel",), vmem_limit_bytes=VMEM_LIMIT),
        name="rope_tables",
    )(compact, place, fill)


def _in_weight(w_in):
    offs = np.cumsum((0,) + IN_SIZES)
    (nsa_q, k_cmp, v_cmp, k_slc, v_slc, k_win, v_win, gate,
     cq, ckv, kpe, ret_q, ret_k, ret_v, ret_g) = [np.arange(offs[i], offs[i + 1]) for i in range(len(IN_SIZES))]
    scale = HEAD_DIM ** -0.5
    head = lambda cols, h: cols[h * HEAD_DIM:(h + 1) * HEAD_DIM]
    pad = lambda n: None if n == 0 else -np.ones(n, np.int64)
    pieces = []
    for j in range(NSA_REP):
        pieces += [(head(nsa_q, j), scale), (head(nsa_q, j + NSA_REP), scale)]
    pieces += [(c, 1.0) for c in (k_slc, k_win, k_cmp, v_cmp, cq, ckv)]
    pieces += [(pad(HALF), 0.0), (kpe, 1.0), (pad(LANES - HALF - MLA_ROPE_DIM), 0.0)]
    slots = lambda cols, f, fill: [fill if h is None else (head(cols, h), f) for h in RET_SLOTS]
    zero = (pad(HEAD_DIM), 0.0)
    pieces += slots(ret_q, 1.0, (head(ret_k, RET_SLOTS[-1]), scale))
    pieces += slots(ret_k, scale, zero)[:2 * RET_K_TILES]
    pieces += slots(ret_v, 1.0, zero) + slots(ret_g, 1.0, zero)
    t_pieces = [(v_slc, 1.0), (v_win, 1.0), (gate, 1.0), (pad(GATE_ROWS - NSA_GATE_W), 0.0)]

    t_pieces += [(pad(RET_PAIRS * LANES - NSA_T_ROWS), 0.0)]
    spans = lambda parts: tuple((None if c[0] < 0 else int(c[0]), len(c), f) for c, f in parts)
    assert sum(n for _, n, _ in spans(pieces)) == N_PAD
    nl, d_model, n_in = w_in.shape
    return pl.pallas_call(
        functools.partial(_relayout_kernel, pieces=spans(pieces), t_pieces=spans(t_pieces)),
        grid=(nl, d_model // RELAYOUT_TM),
        in_specs=[pl.BlockSpec((None, RELAYOUT_TM, n_in), lambda l, i: (l, i, 0))],
        out_specs=(pl.BlockSpec((None, RELAYOUT_TM, N_PAD), lambda l, i: (l, i, 0)),
                   pl.BlockSpec((None, NSA_T_ROWS, RELAYOUT_TM), lambda l, i: (l, 0, i))),
        out_shape=(jax.ShapeDtypeStruct((nl, d_model, N_PAD), BF16),
                   jax.ShapeDtypeStruct((nl, NSA_T_ROWS, d_model), BF16)),
        compiler_params=pltpu.CompilerParams(
            dimension_semantics=("parallel", "parallel"), vmem_limit_bytes=VMEM_LIMIT),
        name="in_weight_layout",
    )(w_in)


RELAYOUT_TM = 256


def _relayout_kernel(w_ref, o_ref, ot_ref, *, pieces, t_pieces):
    w = w_ref[...]

    def build(parts):
        cols = []
        for src, n, factor in parts:
            if src is None:
                cols.append(jnp.zeros((w.shape[0], n), F32))
            else:
                cols.append(w[:, src:src + n] if factor == 1.0 else w[:, src:src + n] * factor)
        return jnp.concatenate(cols, axis=1)

    o_ref[...] = build(pieces).astype(BF16)
    ot_ref[...] = build(t_pieces).T[0:NSA_T_ROWS].astype(BF16)


def _compress_weights(pos, w1, w2):
    nl = pos.shape[0]
    g, dh, hid = NSA_KV_GROUPS, HEAD_DIM, CMP_HIDDEN
    assert g == 2
    p = jnp.tile(pos, (1, 1, g))

    def block_diag2(w):
        lead = ((0, 0),) * (w.ndim - 1)
        return jnp.concatenate([jnp.pad(w, lead + ((0, w.shape[-1]),)),
                                jnp.pad(w, lead + ((w.shape[-1], 0),))], axis=-2)

    w = block_diag2(w1.astype(BF16).reshape(nl, CMP_LEN, dh, hid))
    return p, w, block_diag2(w2.astype(BF16))


def _mla_weights(w_uq, w_ukv):
    nl = w_uq.shape[0]
    dq = MLA_NOPE_DIM + MLA_ROPE_DIM
    wq = w_uq.reshape(nl, MLA_Q_RANK, MLA_HEADS, dq)
    pe = wq[..., MLA_NOPE_DIM:]
    half = MLA_ROPE_DIM // 2
    rot = jnp.concatenate([jnp.zeros_like(wq[..., :MLA_NOPE_DIM]), -pe[..., half:], pe[..., :half]], axis=-1)
    pad = lambda w: jnp.pad(w, ((0, 0), (0, 0), (0, 0), (0, LANES - dq))).reshape(
        nl, MLA_Q_RANK, MLA_HEADS * LANES)
    wq = jnp.concatenate([pad(wq), pad(rot)], axis=-1)
    wkv = w_ukv.reshape(nl, MLA_KV_RANK, MLA_HEADS, MLA_NOPE_DIM + MLA_V_DIM)
    wk = jnp.pad(wkv[..., :MLA_NOPE_DIM], ((0, 0), (0, 0), (0, 0), (0, LANES - MLA_NOPE_DIM)))
    wk = wk.reshape(nl, MLA_KV_RANK, MLA_HEADS * LANES)
    wvt = wkv[..., MLA_NOPE_DIM:].reshape(nl, MLA_KV_RANK, MLA_V_ROWS).transpose(0, 2, 1)
    return wq.astype(BF16), wk.astype(BF16), wvt.astype(BF16)


def _out_weight(w_out):
    nl = w_out.shape[0]
    nsa = w_out[:, :NSA_Q_W].reshape(nl, NSA_HEADS, HEAD_DIM, D_MODEL)
    order = [h for j in range(NSA_REP) for h in (j, j + NSA_REP)]
    nsa = nsa[:, order].reshape(nl, NSA_Q_W, D_MODEL)
    mla = w_out[:, NSA_Q_W:NSA_Q_W + MLA_HEADS * MLA_V_DIM]
    ret = w_out[:, NSA_Q_W + MLA_HEADS * MLA_V_DIM:]
    full = 2 * LANES
    return jnp.concatenate([nsa, mla[:, :full], mla[:, full:], ret[:, full:], ret[:, :full]],
                           axis=1).astype(BF16)


def _retention_tables(gn_gain):
    nh = 2 * RET_PAIRS
    slot_head = np.array([RET_HEADS if h is None else h for h in RET_SLOTS])
    log_g = jnp.log(1.0 - 2.0 ** (-5.0 - jnp.asarray(slot_head, F32)))
    i = jnp.arange(RET_CHUNK, dtype=F32)
    diff = i[:, None] - i[None, :]
    intra = jnp.where(diff >= 0, jnp.exp(jnp.maximum(diff, 0.0)[None] * log_g[:, None, None]), 0.0)
    read_decay = jnp.exp((i + 1.0)[None, :] * log_g[:, None])
    write_decay = jnp.exp((RET_CHUNK - 1.0 - i)[None, :] * log_g[:, None])
    chunk_decay = jnp.exp(RET_CHUNK * log_g)

    def lanes(t):
        t = t.reshape(RET_PAIRS, 2, -1)
        return jnp.repeat(t.transpose(0, 2, 1), HALF, axis=-1)

    gn = jnp.pad(gn_gain, ((0, 0), (0, nh - RET_HEADS), (0, 0)))[:, slot_head]
    gn = gn.reshape(gn.shape[0], RET_PAIRS, 1, LANES)
    return (gn, intra.reshape(RET_PAIRS, 2, RET_CHUNK, RET_CHUNK), lanes(read_decay), lanes(write_decay),
            lanes(chunk_decay[:, None]))


def _selection_overlap(seq):
    n_cmp = (seq - CMP_LEN) // CMP_STRIDE + 1
    n_sel = seq // SEL_BLOCK
    cs = np.arange(n_cmp) * CMP_STRIDE
    ss = np.arange(n_sel) * SEL_BLOCK
    ov = np.clip(np.minimum(cs[:, None] + CMP_LEN, ss[None, :] + SEL_BLOCK)
                 - np.maximum(cs[:, None], ss[None, :]), 0, None) / CMP_LEN
    ovl_t = np.zeros((n_sel, seq // CMP_STRIDE), np.float32)
    ovl_t[:, :n_cmp] = ov.T
    return jnp.asarray(ovl_t, BF16)


def kernel(x, positions, ln1_gain, w_in, cmp_pos_k, cmp_w1_k, cmp_w2_k, cmp_pos_v, cmp_w1_v, cmp_w2_v,
           mla_q_norm, mla_w_uq, mla_kv_norm, mla_w_ukv, ret_gn_gain, w_out, ln2_gain, w_up, w_down,
           final_gain):
    batch, seq, _ = x.shape
    depth = w_in.shape[0]
    t = batch * seq

    tab_n, tab_m, tab_r = _rope_tables(positions)

    w_in_p, w_in_t = _in_weight(w_in)
    pos_k, w1_k, w2_k = _compress_weights(cmp_pos_k, cmp_w1_k, cmp_w2_k)
    pos_v, w1_v, w2_v = _compress_weights(cmp_pos_v, cmp_w1_v, cmp_w2_v)
    cmp_pos = jnp.stack([pos_k, pos_v], axis=1)
    cmp_w1 = jnp.stack([w1_k, w1_v], axis=1)
    w2_vt = w2_v.transpose(0, 2, 1)
    wq, wk, wvt = _mla_weights(mla_w_uq, mla_w_ukv)
    wo = _out_weight(w_out)
    wu = w_up.astype(BF16)
    wd = w_down.astype(BF16)
    gn, intra, rd, wdec, cd = _retention_tables(ret_gn_gain)
    ovl_t = _selection_overlap(seq)
    gf = final_gain.reshape(1, D_MODEL)

    x2 = x.reshape(t, D_MODEL)
    for l in range(depth):
        nsa, nsa_t, gate_t, k_cmp, v_cmp, ret, rgate, q_m, k_m, vt_m = _in_proj(
            x2, ln1_gain[l].reshape(1, D_MODEL), w_in_p, w_in_t, mla_q_norm[l].reshape(1, -1),
            mla_kv_norm[l].reshape(1, -1), wq, wk, wvt, l, tab_n, tab_m, tab_r)
        kc, vct = _compress(k_cmp, v_cmp, cmp_pos[l], cmp_w1, w2_k[l], w2_vt[l], l, batch, seq)
        o_nsa = _nsa_attention(nsa, nsa_t, gate_t, kc, vct, ovl_t, batch, seq)
        o_mla = _mla_attention(q_m, k_m, vt_m, batch, seq)
        o_ret = _retention(ret, rgate, gn[l], intra, rd, wdec, cd, batch, seq)
        x2 = _out_mlp(x2, o_nsa, o_mla, o_ret, wo, ln2_gain[l].reshape(1, D_MODEL), wu, wd, l, gf,
                      final_norm=(l == depth - 1))
    return x2.reshape(batch, seq, D_MODEL)
```

```python
import functools
import math

import numpy as np
import jax
import jax.numpy as jnp
from jax import lax
from jax.experimental import pallas as pl
from jax.experimental.pallas import tpu as pltpu

F32 = jnp.float32
BF16 = jnp.bfloat16

D_MODEL = 1024
HEAD_DIM = 64
NSA_HEADS = 6
NSA_KV_GROUPS = 2
NSA_REP = NSA_HEADS // NSA_KV_GROUPS
N_BRANCH = 3
CMP_LEN = 32
CMP_STRIDE = 16
CMP_HIDDEN = 2 * HEAD_DIM
SEL_BLOCK = 64
SEL_TOP_N = 16
WINDOW = 512
MLA_HEADS = 5
MLA_Q_RANK = 256
MLA_KV_RANK = 128
MLA_NOPE_DIM = 64
MLA_ROPE_DIM = 32
MLA_V_DIM = 64
RET_HEADS = 5
RET_CHUNK = 128
ROPE_THETA = 500000.0
PARTIAL_ROPE_DIM = HEAD_DIM // 4
RET_THETA = 10000.0
D_FF = 4 * D_MODEL
NORM_EPS = 1e-6
NEG_INF = -1e30
FORCE_SCORE = 1e9
LOG2E = math.log2(math.e)

NSA_Q_W = NSA_HEADS * HEAD_DIM
NSA_KV_W = NSA_KV_GROUPS * HEAD_DIM
NSA_GATE_W = NSA_HEADS * N_BRANCH
RET_W = RET_HEADS * HEAD_DIM
IN_SIZES = (NSA_Q_W, NSA_KV_W, NSA_KV_W, NSA_KV_W, NSA_KV_W, NSA_KV_W, NSA_KV_W, NSA_GATE_W,
            MLA_Q_RANK, MLA_KV_RANK, MLA_ROPE_DIM, RET_W, RET_W, RET_W, RET_W)

LANES = 128
HALF = LANES // 2
SUBLANES = 8
MXU_WIDTH = 256
VMEM_LIMIT = 56 * 1024 * 1024

NSA_TILES = 5
GATE_ROWS = 32
NSA_T_ROWS = 2 * LANES + GATE_ROWS
MLA_IN_TILES = 4
RET_PAIRS = 3
RET_SLOTS = (0, 1, 2, 3, None, 4)
RET_K_TILES = RET_PAIRS - 1
RET_ROPE_TILES = RET_PAIRS + RET_K_TILES
RET_MM_TILES = RET_ROPE_TILES + RET_PAIRS
RET_TILES = 3 * RET_PAIRS
IN_TILES = NSA_TILES + 2 + MLA_IN_TILES + RET_MM_TILES + RET_PAIRS
N_PAD = IN_TILES * LANES
MLA_V_ROWS = MLA_HEADS * MLA_V_DIM

IN_TM = 1024
IN_CHUNK_TILES = 4 * MXU_WIDTH // LANES
MLP_TM = 1024
ATT_T = 512
MLP_FF_CHUNK = 512


def _nn(a, b):
    return jnp.dot(a, b, preferred_element_type=F32)


def _nt(a, b):
    return lax.dot_general(a, b, (((1,), (1,)), ((), ())), preferred_element_type=F32)


def _tn(a, b):
    return lax.dot_general(a, b, (((0,), (0,)), ((), ())), preferred_element_type=F32)


def _rms(x, gain):
    return x * lax.rsqrt(jnp.mean(x * x, axis=-1, keepdims=True) + NORM_EPS) * gain


def _rope(val, tab, half):
    cos = tab[:, 0:LANES]
    sin_a = tab[:, LANES:2 * LANES]
    sin_b = tab[:, 2 * LANES:3 * LANES]
    return (val * cos + pltpu.roll(val, LANES - half, 1) * sin_a
            + pltpu.roll(val, half, 1) * sin_b)


def _lane_lo(shape):
    return lax.broadcasted_iota(jnp.int32, shape, len(shape) - 1) < HALF


def _mla_up(c, qn_ref, kvn_ref, wq_ref, wk_ref, wvt_ref, tm_ref, q_ref, k_ref, vt_ref):
    scale = (MLA_NOPE_DIM + MLA_ROPE_DIM) ** -0.5 * LOG2E
    cq = _rms(c[:, 0:MLA_Q_RANK], qn_ref[...]).astype(BF16)
    ckv = _rms(c[:, MLA_Q_RANK:MLA_Q_RANK + MLA_KV_RANK], kvn_ref[...]).astype(BF16)
    k_pe = _rope(c[:, 3 * LANES:4 * LANES], tm_ref[...], MLA_ROPE_DIM // 2)
    cos = tm_ref[:, 0:LANES] * scale
    sin = (tm_ref[:, 2 * LANES:3 * LANES] - tm_ref[:, LANES:2 * LANES]) * scale
    q = _nn(cq, wq_ref[...])
    k = _nn(ckv, wk_ref[...])
    w = MLA_HEADS * LANES
    for hd in range(MLA_HEADS):
        sl = slice(hd * LANES, (hd + 1) * LANES)
        rot = slice(w + hd * LANES, w + (hd + 1) * LANES)
        q_ref[:, sl] = (q[:, sl] * cos + q[:, rot] * sin).astype(BF16)
        k_ref[:, sl] = (k[:, sl] + k_pe).astype(BF16)
    vt_ref[...] = _nt(wvt_ref[...], ckv).astype(BF16)


def _inproj_kernel(x_ref, g_ref, w_ref, wt_ref, tn_ref, tm_ref, tr_ref, qn_ref, kvn_ref, wq_ref, wk_ref,
                   wvt_ref, nsa_ref, nsat_ref, gatet_ref, kcmp_ref, vcmp_ref, ret_ref, rgate_ref,
                   mq_ref, mk_ref, mvt_ref):
    h = _rms(x_ref[...], g_ref[...]).astype(BF16)
    tab_n = tn_ref[...]
    tab_r = tr_ref[...]

    chunks = [_nn(h, w_ref[:, c:min(c + IN_CHUNK_TILES * LANES, N_PAD)])
              for c in range(0, N_PAD, IN_CHUNK_TILES * LANES)]

    def tiles(first, n):
        cols = [chunks[i // IN_CHUNK_TILES][:, (i % IN_CHUNK_TILES) * LANES:(i % IN_CHUNK_TILES + 1) * LANES]
                for i in range(first, first + n)]
        return cols[0] if n == 1 else jnp.concatenate(cols, axis=1)

    t0 = 0
    for i in range(NSA_TILES):
        v = _rope(tiles(t0 + i, 1), tab_n, PARTIAL_ROPE_DIM // 2)
        if i < NSA_REP:
            v = v * LOG2E
        nsa_ref[:, i * LANES:(i + 1) * LANES] = v.astype(BF16)
    t0 += NSA_TILES
    kcmp_ref[...] = _rope(tiles(t0, 1), tab_n, PARTIAL_ROPE_DIM // 2)
    vcmp_ref[...] = tiles(t0 + 1, 1)
    t0 += 2
    _mla_up(tiles(t0, MLA_IN_TILES), qn_ref, kvn_ref, wq_ref, wk_ref, wvt_ref, tm_ref, mq_ref, mk_ref, mvt_ref)
    t0 += MLA_IN_TILES
    stored = []
    for i in range(RET_MM_TILES):
        v = tiles(t0 + i, 1)
        if i < RET_ROPE_TILES:
            v = _rope(v, tab_r, HEAD_DIM // 2)
        stored.append(v)
        if i == RET_ROPE_TILES - 1:
            stored.append(pltpu.roll(stored[RET_PAIRS - 1], HALF, 1))
    for i, v in enumerate(stored):
        ret_ref[:, i * LANES:(i + 1) * LANES] = v.astype(BF16)
    t0 += RET_MM_TILES
    rgate_ref[...] = tiles(t0, RET_PAIRS)
    at = _nt(wt_ref[...], h)
    nsat_ref[...] = at[0:2 * LANES, :].astype(BF16)
    gatet_ref[...] = at[2 * LANES:NSA_T_ROWS, :]


def _layer_spec(w, layer, **kwargs):
    zeros = (0,) * (w.ndim - 1)
    return pl.BlockSpec((None,) + w.shape[1:], lambda *_: (layer,) + zeros, **kwargs)


def _in_proj(x2, gain, w, wt, q_norm, kv_norm, wq, wk, wvt, layer, tab_n, tab_m, tab_r):
    t = x2.shape[0]
    row = lambda i: (i, 0)
    col = lambda i: (0, i)
    const = lambda i: (0, 0)
    out_shapes = (
        jax.ShapeDtypeStruct((t, NSA_TILES * LANES), BF16),
        jax.ShapeDtypeStruct((2 * LANES, t), BF16),
        jax.ShapeDtypeStruct((GATE_ROWS, t), F32),
        jax.ShapeDtypeStruct((t, LANES), F32),
        jax.ShapeDtypeStruct((t, LANES), F32),
        jax.ShapeDtypeStruct((t, RET_TILES * LANES), BF16),
        jax.ShapeDtypeStruct((t, RET_PAIRS * LANES), F32),
        jax.ShapeDtypeStruct((t, MLA_HEADS * LANES), BF16),
        jax.ShapeDtypeStruct((t, MLA_HEADS * LANES), BF16),
        jax.ShapeDtypeStruct((MLA_V_ROWS, t), BF16),
    )
    out_specs = tuple(
        pl.BlockSpec((s.shape[0], IN_TM), col) if s.shape[1] == t else pl.BlockSpec((IN_TM, s.shape[1]), row)
        for s in out_shapes)
    resident = dict(pipeline_mode=pl.Buffered(1))
    return pl.pallas_call(
        _inproj_kernel,
        grid=(t // IN_TM,),
        in_specs=[
            pl.BlockSpec((IN_TM, D_MODEL), row),
            pl.BlockSpec((1, D_MODEL), const),
            _layer_spec(w, layer, **resident),
            _layer_spec(wt, layer, **resident),
            pl.BlockSpec((IN_TM, 3 * LANES), row),
            pl.BlockSpec((IN_TM, 3 * LANES), row),
            pl.BlockSpec((IN_TM, 3 * LANES), row),
            pl.BlockSpec((1, MLA_Q_RANK), const),
            pl.BlockSpec((1, MLA_KV_RANK), const),
            _layer_spec(wq, layer, **resident),
            _layer_spec(wk, layer, **resident),
            _layer_spec(wvt, layer, **resident),
        ],
        out_specs=out_specs,
        out_shape=out_shapes,
        compiler_params=pltpu.CompilerParams(
            dimension_semantics=("parallel",), vmem_limit_bytes=VMEM_LIMIT),
        name="in_proj",
    )(x2, gain, w, wt, tab_n, tab_m, tab_r, q_norm, kv_norm, wq, wk, wvt)


def _compress_kernel(k_ref, v_ref, pos_ref, w1_ref, w2k_ref, w2vt_ref, kc_ref, vct_ref):
    n_blk = k_ref.shape[0] // CMP_STRIDE

    def hidden(src, i):
        toks = [src[pl.ds(r, n_blk, stride=CMP_STRIDE), :] for r in range(CMP_STRIDE)]
        halves = []
        for half in range(CMP_LEN // CMP_STRIDE):
            off = half * CMP_STRIDE
            x = jnp.concatenate([(toks[r] + pos_ref[i, off + r:off + r + 1, :]).astype(BF16)
                                 for r in range(CMP_STRIDE)], axis=1)
            w = w1_ref[i, off:off + CMP_STRIDE].reshape(CMP_STRIDE * LANES, NSA_KV_GROUPS * CMP_HIDDEN)
            halves.append(_nn(x, w))
        return jax.nn.gelu(halves[0] + pltpu.roll(halves[1], n_blk - 1, 0)).astype(BF16)

    kc_ref[0] = _nn(hidden(k_ref, 0), w2k_ref[...]).astype(BF16)
    vct_ref[0] = _nt(w2vt_ref[...], hidden(v_ref, 1)).astype(BF16)


def _compress(k_cmp, v_cmp, pos, w1, w2k, w2vt, layer, batch, seq):
    b = batch
    n_blk = seq // CMP_STRIDE
    return pl.pallas_call(
        _compress_kernel,
        grid=(b,),
        in_specs=[
            pl.BlockSpec((seq, LANES), lambda i: (i, 0)),
            pl.BlockSpec((seq, LANES), lambda i: (i, 0)),
            pl.BlockSpec(pos.shape, lambda i: (0, 0, 0)),
            _layer_spec(w1, layer),
            pl.BlockSpec(w2k.shape, lambda i: (0, 0)),
            pl.BlockSpec(w2vt.shape, lambda i: (0, 0)),
        ],
        out_specs=(pl.BlockSpec((1, n_blk, LANES), lambda i: (i, 0, 0)),
                   pl.BlockSpec((1, LANES, n_blk), lambda i: (i, 0, 0))),
        out_shape=(jax.ShapeDtypeStruct((b, n_blk, LANES), BF16),
                   jax.ShapeDtypeStruct((b, LANES, n_blk), BF16)),
        compiler_params=pltpu.CompilerParams(
            dimension_semantics=("parallel",), vmem_limit_bytes=VMEM_LIMIT),
        name="nsa_compress",
    )(k_cmp, v_cmp, pos, w1, w2k, w2vt)


SCORE_LOOKAHEAD = 2
ONES_ROWS = 16


def _tri_scores(k, q, dead_upper, mask, bias=None):
    half = k.shape[0] // 2
    dead = jnp.full((half, half), NEG_INF, F32)
    if dead_upper:
        top = _nt(k[:half], q)
        bottom = jnp.concatenate([dead, _nt(k[half:], q[half:])], axis=1)
    else:
        top = jnp.concatenate([_nt(k[:half], q[:half]), dead], axis=1)
        bottom = _nt(k[half:], q)
    s_t = jnp.concatenate([top, bottom], axis=0)
    return jnp.where(mask, s_t if bias is None else s_t + bias, NEG_INF)


def _tri_pv(dead_upper, v_ext, p):
    half = p.shape[0] // 2
    if dead_upper:
        full = _nn(v_ext[:, :half], p[:half])
        part = _nn(v_ext[:, half:], p[half:, half:])
        return jnp.concatenate([full[:, :half], full[:, half:] + part], axis=1)
    part = _nn(v_ext[:, :half], p[:half, :half])
    full = _nn(v_ext[:, half:], p[half:])
    return jnp.concatenate([full[:, :half] + part, full[:, half:]], axis=1)


def _softmax_steps(score_fns, v_ts, states, pv_fn=_nn):
    n = len(score_fns)
    s_ts = [score_fns[h]() if h < SCORE_LOOKAHEAD else None for h in range(n)]
    out = []
    for h in range(n):
        if h + SCORE_LOOKAHEAD < n:
            s_ts[h + SCORE_LOOKAHEAD] = score_fns[h + SCORE_LOOKAHEAD]()
        m_old, acc_old = states[h]
        m = jnp.maximum(m_old, jnp.max(s_ts[h], axis=0, keepdims=True))
        p = jnp.exp2(s_ts[h] - m).astype(BF16)
        s_ts[h] = None
        v_ext = jnp.concatenate([v_ts[h], jnp.ones((ONES_ROWS, v_ts[h].shape[1]), BF16)], axis=0)
        out.append((m, jnp.exp2(m_old - m) * acc_old + pv_fn(v_ext, p)))
    return tuple(out)


def _softmax_init(cols):
    return jnp.full((1, cols), NEG_INF, F32), jnp.zeros((HEAD_DIM + ONES_ROWS, cols), F32)


def _softmax_finish(state):
    acc = state[1]
    return acc[0:HEAD_DIM] * (1.0 / jnp.maximum(acc[HEAD_DIM:HEAD_DIM + 1], 1e-30))


def _nsa_kernel(q_ref, gatet_ref, kc_ref, vct_ref, ks_ref, kw_ref, vst_ref, vwt_ref, ovl_ref,
                o_ref, out_t_ref, selb_ref):
    tq = ATT_T
    qi = pl.program_id(1)
    q0 = pl.multiple_of(qi * tq, tq)
    lo1 = _lane_lo((1, LANES))
    group_lanes = (lo1, jnp.logical_not(lo1))
    gate = jax.nn.sigmoid(gatet_ref[...])
    heads = range(NSA_HEADS)

    def q_head(h):
        j, g = h % NSA_REP, h // NSA_REP
        tile = q_ref[:, j * LANES:(j + 1) * LANES]
        return jnp.where(group_lanes[g], tile, jnp.zeros_like(tile))

    def v_rows(ref, h, k0, n):
        g = h // NSA_REP
        return ref[g * HEAD_DIM:(g + 1) * HEAD_DIM, pl.ds(k0, n)]

    def emit(h, branch, qs, nq, o_t):
        j, g = h % NSA_REP, h // NSA_REP
        r0 = j * LANES + g * HEAD_DIM
        row = h * N_BRANCH + branch
        val = gate[row:row + 1, qs:qs + nq] * o_t
        if branch == 0:
            out_t_ref[r0:r0 + HEAD_DIM, qs:qs + nq] = val
        else:
            out_t_ref[r0:r0 + HEAD_DIM, qs:qs + nq] += val

    qm = [q_head(h) for h in heads]

    n_cmp_pad = kc_ref.shape[1]
    kc = kc_ref[0]
    vct = vct_ref[0]
    n_i = lax.broadcasted_iota(jnp.int32, (n_cmp_pad, tq), 0)
    t_l = q0 + lax.broadcasted_iota(jnp.int32, (n_cmp_pad, tq), 1)
    cmask = (n_i * CMP_STRIDE + (CMP_LEN - 1)) <= t_l
    cmask_f = cmask.astype(F32)
    s_cs = [jnp.where(cmask, _nt(kc, qm[h]), NEG_INF) for h in heads]
    p_cs = [jnp.exp2(s_t - jnp.max(s_t, axis=0, keepdims=True)) * cmask_f for s_t in s_cs]
    p_cs = [p * (1.0 / jnp.maximum(jnp.sum(p, axis=0, keepdims=True), 1e-30)) for p in p_cs]
    for h in heads:
        g = h // NSA_REP
        emit(h, 0, 0, tq, _nn(vct[g * HEAD_DIM:(g + 1) * HEAD_DIM, :], p_cs[h].astype(BF16)))

    n_sel = ovl_ref.shape[0]
    needs_rank = q0 + tq > SEL_TOP_N * SEL_BLOCK

    @pl.when(jnp.logical_not(needs_rank))
    def _():
        selb_ref[...] = jnp.zeros_like(selb_ref)

    @pl.when(needs_rank)
    def _():
        m_i = lax.broadcasted_iota(jnp.int32, (n_sel, tq), 0)
        cur = jnp.right_shift(q0 + lax.broadcasted_iota(jnp.int32, (n_sel, tq), 1),
                              SEL_BLOCK.bit_length() - 1)
        valid = m_i <= cur
        forced = (m_i == 0) | (m_i == cur) | (m_i == cur - 1)
        ovl = ovl_ref[...]
        sub = SUBLANES
        m_loc = lax.broadcasted_iota(jnp.int32, (sub, tq), 0)
        for g in range(NSA_KV_GROUPS):
            psum = p_cs[g * NSA_REP]
            for h in range(g * NSA_REP + 1, (g + 1) * NSA_REP):
                psum = psum + p_cs[h]
            p_hi = psum.astype(BF16)
            p_lo = (psum - p_hi.astype(F32)).astype(BF16)
            imp = _nn(ovl, p_hi) + _nn(ovl, p_lo)
            imp = jnp.where(valid & forced, FORCE_SCORE, imp)
            imp = jnp.where(valid, imp, NEG_INF)
            parts = [imp[i:i + sub] for i in range(0, n_sel, sub)]
            ranks = [jnp.zeros((sub, tq), jnp.int32) for _ in parts]
            for mp in range(n_sel):
                row = imp[mp:mp + 1, :]
                for i, part in enumerate(parts):
                    if i * sub + sub - 1 <= mp:
                        beats = row > part
                    elif i * sub > mp:
                        beats = row >= part
                    else:
                        beats = (row > part) | ((row == part) & (m_loc + i * sub > mp))
                    ranks[i] = ranks[i] + beats.astype(jnp.int32)
            rank = jnp.concatenate(ranks, axis=0)
            selb_ref[g] = jnp.where(rank < SEL_TOP_N, 0.0, NEG_INF)

    blocks_per_chunk = tq // SEL_BLOCK

    def sel_bias(g, c, n_keys, qs, nq):
        rows = [jnp.broadcast_to(selb_ref[g, pl.ds(c * blocks_per_chunk + i, 1), qs:qs + nq], (SEL_BLOCK, nq))
                for i in range(n_keys // SEL_BLOCK)]
        return jnp.concatenate(rows, axis=0)

    key_i = lax.broadcasted_iota(jnp.int32, (tq, tq), 0)
    qry_i = lax.broadcasted_iota(jnp.int32, (tq, tq), 1)
    causal_mask = key_i <= qry_i

    def sel_chunk(c, k0, states, causal):
        k = ks_ref[pl.ds(k0, tq), :]
        bias = [sel_bias(g, c, tq, 0, tq) for g in range(NSA_KV_GROUPS)]

        def score_fn(h):
            if causal:
                return _tri_scores(k, qm[h], True, causal_mask, bias[h // NSA_REP])
            return _nt(k, qm[h]) + bias[h // NSA_REP]

        return _softmax_steps([functools.partial(score_fn, h) for h in heads],
                              [v_rows(vst_ref, h, k0, tq) for h in heads], states,
                              functools.partial(_tri_pv, True) if causal else _nn)

    states = sel_chunk(qi, q0, tuple(_softmax_init(tq) for _ in heads), True)
    states = lax.fori_loop(
        0, qi, lambda c, st: sel_chunk(c, pl.multiple_of(c * tq, tq), st, False), states)
    for h in heads:
        emit(h, 1, 0, tq, _softmax_finish(states[h]))

    def win_chunk(k0, states, mask, dead_upper):
        k = kw_ref[pl.ds(k0, tq), :]
        if dead_upper is None:
            score_fns = [lambda h=h: jnp.where(mask, _nt(k, qm[h]), NEG_INF) for h in heads]
            pv_fn = _nn
        else:
            score_fns = [lambda h=h: _tri_scores(k, qm[h], dead_upper, mask) for h in heads]
            pv_fn = functools.partial(_tri_pv, dead_upper)
        return _softmax_steps(score_fns, [v_rows(vwt_ref, h, k0, tq) for h in heads], states, pv_fn)

    states = win_chunk(q0, tuple(_softmax_init(tq) for _ in heads), causal_mask, True)
    for d in range(1, WINDOW // tq + 1):
        k0 = pl.multiple_of(jnp.maximum(qi - d, 0) * tq, tq)
        in_band = (key_i - d * tq > qry_i - WINDOW) & (qi >= d)
        states = win_chunk(k0, states, in_band, False if d * tq == WINDOW else None)
    for h in heads:
        emit(h, 2, 0, tq, _softmax_finish(states[h]))

    for j in range(NSA_REP):
        o_ref[:, j * LANES:(j + 1) * LANES] = out_t_ref[j * LANES:(j + 1) * LANES, :].T.astype(BF16)


def _nsa_attention(nsa, nsa_t, gate_t, kc, vct, ovl_t, batch, seq):
    tq = ATT_T
    nq = seq // tq
    n_blk = kc.shape[1]
    qrow = lambda b, i: (b * nq + i, 0)
    return pl.pallas_call(
        _nsa_kernel,
        grid=(batch, nq),
        in_specs=[
            pl.BlockSpec((tq, NSA_REP * LANES), qrow),
            pl.BlockSpec((GATE_ROWS, tq), lambda b, i: (0, b * nq + i)),
            pl.BlockSpec((1, n_blk, LANES), lambda b, i: (b, 0, 0)),
            pl.BlockSpec((1, LANES, n_blk), lambda b, i: (b, 0, 0)),
            pl.BlockSpec((seq, LANES), lambda b, i: (b, 3)),
            pl.BlockSpec((seq, LANES), lambda b, i: (b, 4)),
            pl.BlockSpec((LANES, seq), lambda b, i: (0, b)),
            pl.BlockSpec((LANES, seq), lambda b, i: (1, b)),
            pl.BlockSpec(ovl_t.shape, lambda b, i: (0, 0)),
        ],
        out_specs=pl.BlockSpec((tq, NSA_REP * LANES), qrow),
        out_shape=jax.ShapeDtypeStruct((batch * seq, NSA_REP * LANES), BF16),
        scratch_shapes=[pltpu.VMEM((NSA_REP * LANES, tq), F32),
                        pltpu.VMEM((NSA_KV_GROUPS, seq // SEL_BLOCK, tq), F32)],
        compiler_params=pltpu.CompilerParams(
            dimension_semantics=("parallel", "arbitrary"), vmem_limit_bytes=VMEM_LIMIT),
        name="nsa_attention",
    )(nsa, gate_t, kc, vct, nsa, nsa, nsa_t, nsa_t, ovl_t)


def _mla_attn_kernel(q_ref, k_ref, vt_ref, o_ref, out_t_ref):
    tq = ATT_T
    qi = pl.program_id(1)
    q0 = pl.multiple_of(qi * tq, tq)
    heads = range(MLA_HEADS)
    qh = [q_ref[:, h * LANES:(h + 1) * LANES] for h in heads]

    def chunk(k0, states, causal):
        if causal:
            mask = (lax.broadcasted_iota(jnp.int32, (tq, tq), 0)
                    <= lax.broadcasted_iota(jnp.int32, (tq, tq), 1))

        def score_fn(h):
            k = k_ref[pl.ds(k0, tq), h * LANES:(h + 1) * LANES]
            return _tri_scores(k, qh[h], True, mask) if causal else _nt(k, qh[h])

        v_ts = [vt_ref[h * MLA_V_DIM:(h + 1) * MLA_V_DIM, pl.ds(k0, tq)] for h in heads]
        return _softmax_steps([functools.partial(score_fn, h) for h in heads], v_ts, states,
                              functools.partial(_tri_pv, True) if causal else _nn)

    states = chunk(q0, tuple(_softmax_init(tq) for _ in heads), True)
    states = lax.fori_loop(0, qi, lambda c, st: chunk(pl.multiple_of(c * tq, tq), st, False), states)
    for h in heads:
        out_t_ref[h * MLA_V_DIM:(h + 1) * MLA_V_DIM, :] = _softmax_finish(states[h])
    out_t_ref[MLA_V_ROWS:, :] = jnp.zeros((RET_PAIRS * LANES - MLA_V_ROWS, tq), F32)
    for j in range(RET_PAIRS):
        o_ref[:, j * LANES:(j + 1) * LANES] = out_t_ref[j * LANES:(j + 1) * LANES, :].T.astype(BF16)


def _mla_attention(q, k, vt, batch, seq):
    tq = ATT_T
    nq = seq // tq
    return pl.pallas_call(
        _mla_attn_kernel,
        grid=(batch, nq),
        in_specs=[
            pl.BlockSpec((tq, MLA_HEADS * LANES), lambda b, i: (b * nq + i, 0)),
            pl.BlockSpec((seq, MLA_HEADS * LANES), lambda b, i: (b, 0)),
            pl.BlockSpec((MLA_V_ROWS, seq), lambda b, i: (0, b)),
        ],
        out_specs=pl.BlockSpec((tq, RET_PAIRS * LANES), lambda b, i: (b * nq + i, 0)),
        out_shape=jax.ShapeDtypeStruct((batch * seq, RET_PAIRS * LANES), BF16),
        scratch_shapes=[pltpu.VMEM((RET_PAIRS * LANES, tq), F32)],
        compiler_params=pltpu.CompilerParams(
            dimension_semantics=("parallel", "arbitrary"), vmem_limit_bytes=VMEM_LIMIT),
        name="mla_attention",
    )(q, k, vt)


RET_UNROLL = 16


def _retention_kernel(q_ref, k_ref, v_ref, gate_ref, gn_ref, intra_ref, rd_ref, wd_ref, cd_ref, o_ref,
                      kv_ref, st_ref):
    c_len = RET_CHUNK
    n_chunks = q_ref.shape[0] // c_len
    lo = _lane_lo((1, LANES))
    hi = jnp.logical_not(lo)
    blockdiag = (lax.broadcasted_iota(jnp.int32, (LANES, LANES), 0) < HALF) == _lane_lo((LANES, LANES))
    intra_a = intra_ref[0, 0]
    intra_b = intra_ref[0, 1]
    read_decay = rd_ref[0]
    write_decay = wd_ref[0]
    chunk_decay = cd_ref[0]
    gn = gn_ref[0]

    averager = jnp.where(blockdiag, 1.0 / HEAD_DIM, 0.0).astype(BF16)

    def half_mean(x):
        x_hi = x.astype(BF16)
        x_lo = (x - x_hi.astype(F32)).astype(BF16)
        return _nn(x_hi, averager) + _nn(x_lo, averager)

    def kv_body(c, carry):
        r0 = pl.multiple_of(c * c_len, c_len)
        kc = k_ref[pl.ds(r0, c_len), :]
        kv_ref[c] = _tn((kc.astype(F32) * write_decay).astype(BF16), v_ref[pl.ds(r0, c_len), :])
        return carry

    lax.fori_loop(0, n_chunks, kv_body, 0, unroll=RET_UNROLL)

    state = jnp.zeros((LANES, LANES), F32)
    for c in range(n_chunks):
        st_ref[c] = jnp.where(blockdiag, state, 0.0).astype(BF16)
        state = state * chunk_decay + kv_ref[c]

    def out_body(step, carry):
        cs = [step * RET_UNROLL + u for u in range(RET_UNROLL)]
        rows = [pl.ds(pl.multiple_of(c * c_len, c_len), c_len) for c in cs]
        qs = [q_ref[r, :] for r in rows]
        ks = [k_ref[r, :] for r in rows]
        vs = [v_ref[r, :] for r in rows]
        zero = jnp.zeros_like(qs[0])
        sa = [(_nt(jnp.where(lo, q, zero), k) * intra_a).astype(BF16) for q, k in zip(qs, ks)]
        sb = [(_nt(jnp.where(hi, q, zero), k) * intra_b).astype(BF16) for q, k in zip(qs, ks)]
        cross = [_nn(q, st_ref[c]) * read_decay for q, c in zip(qs, cs)]
        os_ = [jnp.where(lo, _nn(a, v), _nn(b, v)) + x for a, b, v, x in zip(sa, sb, vs, cross)]
        ds_ = [o - half_mean(o) for o in os_]
        ys = [d * lax.rsqrt(half_mean(d * d) + NORM_EPS) * gn for d in ds_]
        for r, y in zip(rows, ys):
            o_ref[r, :] = (jax.nn.silu(gate_ref[r, :]) * y).astype(BF16)
        return carry

    lax.fori_loop(0, n_chunks // RET_UNROLL, out_body, 0)


def _retention_all_kernel(q_ref, k_ref, v_ref, gate_ref, gn_ref, intra_ref, rd_ref, wd_ref, cd_ref, o_ref,
                          kv_ref, st_ref):
    for j in range(RET_PAIRS):
        cols = slice(j * LANES, (j + 1) * LANES)
        one = slice(j, j + 1)
        _retention_kernel(q_ref.at[:, cols], k_ref.at[:, cols], v_ref.at[:, cols], gate_ref.at[:, cols],
                          gn_ref.at[one], intra_ref.at[one], rd_ref.at[one], wd_ref.at[one], cd_ref.at[one],
                          o_ref.at[:, cols], kv_ref.at[j], st_ref.at[j])


def _retention(ret, rgate, gn, intra, rd, wd, cd, batch, seq):
    whole = lambda a: pl.BlockSpec(a.shape, lambda b: (0,) * a.ndim)
    width = RET_PAIRS * LANES
    return pl.pallas_call(
        _retention_all_kernel,
        grid=(batch,),
        in_specs=[
            pl.BlockSpec((seq, width), lambda b: (b, 0)),
            pl.BlockSpec((seq, width), lambda b: (b, 1)),
            pl.BlockSpec((seq, width), lambda b: (b, 2)),
            pl.BlockSpec((seq, width), lambda b: (b, 0)),
            whole(gn), whole(intra), whole(rd), whole(wd), whole(cd),
        ],
        out_specs=pl.BlockSpec((seq, width), lambda b: (b, 0)),
        out_shape=jax.ShapeDtypeStruct((batch * seq, width), BF16),
        scratch_shapes=[pltpu.VMEM((RET_PAIRS, seq // RET_CHUNK, LANES, LANES), F32),
                        pltpu.VMEM((RET_PAIRS, seq // RET_CHUNK, LANES, LANES), BF16)],
        compiler_params=pltpu.CompilerParams(
            dimension_semantics=("parallel",), vmem_limit_bytes=VMEM_LIMIT),
        name="retention",
    )(ret, ret, ret, rgate, gn, intra, rd, wd, cd)


def _out_mlp_kernel(x_ref, nsa_ref, mla_ref, ret_ref, wo_ref, g2_ref, wu_ref, wd_ref, gf_ref, o_ref,
                    *, final_norm):
    full = 2 * LANES
    mixed = jnp.concatenate([nsa_ref[...], mla_ref[:, :full], mla_ref[:, full:] + ret_ref[:, full:],
                             ret_ref[:, :full]], axis=1)
    x = x_ref[...] + _nn(mixed, wo_ref[...])
    h = _rms(x, g2_ref[...]).astype(BF16)
    y = x
    for c in range(D_FF // MLP_FF_CHUNK):
        sl = slice(c * MLP_FF_CHUNK, (c + 1) * MLP_FF_CHUNK)
        u = jnp.maximum(_nn(h, wu_ref[:, sl]), 0.0)
        y = y + _nn((u * u).astype(BF16), wd_ref[sl, :])
    if final_norm:
        y = _rms(y, gf_ref[...])
    o_ref[...] = y


def _out_mlp(x2, o_nsa, o_mla, o_ret, wo, g2, wu, wd, layer, gf, final_norm):
    t = x2.shape[0]
    w = RET_PAIRS * LANES
    row = lambda i: (i, 0)
    const = lambda i: (0, 0)
    resident = dict(pipeline_mode=pl.Buffered(1))
    return pl.pallas_call(
        functools.partial(_out_mlp_kernel, final_norm=final_norm),
        grid=(t // MLP_TM,),
        in_specs=[
            pl.BlockSpec((MLP_TM, D_MODEL), row),
            pl.BlockSpec((MLP_TM, w), row),
            pl.BlockSpec((MLP_TM, w), row),
            pl.BlockSpec((MLP_TM, w), row),
            _layer_spec(wo, layer, **resident),
            pl.BlockSpec((1, D_MODEL), const),
            _layer_spec(wu, layer, **resident),
            _layer_spec(wd, layer, **resident),
            pl.BlockSpec((1, D_MODEL), const),
        ],
        out_specs=pl.BlockSpec((MLP_TM, D_MODEL), row),
        out_shape=jax.ShapeDtypeStruct((t, D_MODEL), F32),
        compiler_params=pltpu.CompilerParams(
            dimension_semantics=("parallel",), vmem_limit_bytes=VMEM_LIMIT),
        name="out_mlp",
    )(x2, o_nsa, o_mla, o_ret, wo, g2, wu, wd, gf)


ROPE_KINDS = (
    (PARTIAL_ROPE_DIM, ROPE_THETA, HEAD_DIM, 0),
    (MLA_ROPE_DIM, ROPE_THETA, LANES, HALF),
    (HEAD_DIM, RET_THETA, HEAD_DIM, 0),
)
ROPE_TM = 2048


def _rope_placement():
    n_angles = sum(dim // 2 for dim, _, _, _ in ROPE_KINDS)
    assert 2 * n_angles <= LANES
    place = np.zeros((LANES, 3 * LANES * len(ROPE_KINDS)), np.float32)
    fill = np.zeros((1, 3 * LANES * len(ROPE_KINDS)), np.float32)
    row0 = 0
    for kind, (dim, _, period, base) in enumerate(ROPE_KINDS):
        half = dim // 2
        col0 = kind * 3 * LANES
        for lane in range(LANES):
            rel = (lane - base) % period
            first = lane >= base and rel < half
            second = lane >= base and half <= rel < dim
            if first or second:
                angle = rel if first else rel - half
                place[row0 + angle, col0 + lane] = 1.0
                place[n_angles + row0 + angle, col0 + (1 if first else 2) * LANES + lane] = -1.0 if first else 1.0
            else:
                fill[0, col0 + lane] = 1.0
        row0 += half
    return jnp.asarray(place, BF16), jnp.asarray(fill)


def _rope_kernel(cs_ref, place_ref, fill_ref, *out_refs):
    x = cs_ref[...]
    x1 = x.astype(BF16)
    r1 = x - x1.astype(F32)
    x2 = r1.astype(BF16)
    x3 = (r1 - x2.astype(F32)).astype(BF16)
    place = place_ref[...]
    tab = _tn(x1, place) + _tn(x2, place) + _tn(x3, place) + fill_ref[...]
    for i, ref in enumerate(out_refs):
        ref[...] = tab[:, i * 3 * LANES:(i + 1) * 3 * LANES]


def _rope_tables(positions):
    inv = jnp.concatenate([1.0 / (theta ** (jnp.arange(0, dim, 2, dtype=F32) / dim))
                           for dim, theta, _, _ in ROPE_KINDS])
    ang = inv[:, None] * positions.reshape(-1).astype(F32)[None, :]
    compact = jnp.concatenate([jnp.cos(ang), jnp.sin(ang)], axis=0)
    compact = jnp.pad(compact, ((0, LANES - compact.shape[0]), (0, 0)))
    place, fill = _rope_placement()
    t = compact.shape[1]
    tab_shape = jax.ShapeDtypeStruct((t, 3 * LANES), F32)
    return pl.pallas_call(
        _rope_kernel,
        grid=(t // ROPE_TM,),
        in_specs=[pl.BlockSpec((LANES, ROPE_TM), lambda i: (0, i)),
                  pl.BlockSpec(place.shape, lambda i: (0, 0)),
                  pl.BlockSpec(fill.shape, lambda i: (0, 0))],
        out_specs=tuple(pl.BlockSpec((ROPE_TM, 3 * LANES), lambda i: (i, 0)) for _ in ROPE_KINDS),
        out_shape=tuple(tab_shape for _ in ROPE_KINDS),
        compiler_params=pltpu.CompilerParams(
            dimension_semantics=("parallel",), vmem_limit_bytes=VMEM_LIMIT),
        name="rope_tables",
    )(compact, place, fill)


def _in_weight(w_in):
    offs = np.cumsum((0,) + IN_SIZES)
    (nsa_q, k_cmp, v_cmp, k_slc, v_slc, k_win, v_win, gate,
     cq, ckv, kpe, ret_q, ret_k, ret_v, ret_g) = [np.arange(offs[i], offs[i + 1]) for i in range(len(IN_SIZES))]
    scale = HEAD_DIM ** -0.5
    head = lambda cols, h: cols[h * HEAD_DIM:(h + 1) * HEAD_DIM]
    pad = lambda n: None if n == 0 else -np.ones(n, np.int64)
    pieces = []
    for j in range(NSA_REP):
        pieces += [(head(nsa_q, j), scale), (head(nsa_q, j + NSA_REP), scale)]
    pieces += [(c, 1.0) for c in (k_slc, k_win, k_cmp, v_cmp, cq, ckv)]
    pieces += [(pad(HALF), 0.0), (kpe, 1.0), (pad(LANES - HALF - MLA_ROPE_DIM), 0.0)]
    slots = lambda cols, f, fill: [fill if h is None else (head(cols, h), f) for h in RET_SLOTS]
    zero = (pad(HEAD_DIM), 0.0)
    pieces += slots(ret_q, 1.0, (head(ret_k, RET_SLOTS[-1]), scale))
    pieces += slots(ret_k, scale, zero)[:2 * RET_K_TILES]
    pieces += slots(ret_v, 1.0, zero) + slots(ret_g, 1.0, zero)
    t_pieces = [(v_slc, 1.0), (v_win, 1.0), (gate, 1.0), (pad(GATE_ROWS - NSA_GATE_W), 0.0)]

    t_pieces += [(pad(RET_PAIRS * LANES - NSA_T_ROWS), 0.0)]
    spans = lambda parts: tuple((None if c[0] < 0 else int(c[0]), len(c), f) for c, f in parts)
    assert sum(n for _, n, _ in spans(pieces)) == N_PAD
    nl, d_model, n_in = w_in.shape
    return pl.pallas_call(
        functools.partial(_relayout_kernel, pieces=spans(pieces), t_pieces=spans(t_pieces)),
        grid=(nl, d_model // RELAYOUT_TM),
        in_specs=[pl.BlockSpec((None, RELAYOUT_TM, n_in), lambda l, i: (l, i, 0))],
        out_specs=(pl.BlockSpec((None, RELAYOUT_TM, N_PAD), lambda l, i: (l, i, 0)),
                   pl.BlockSpec((None, NSA_T_ROWS, RELAYOUT_TM), lambda l, i: (l, 0, i))),
        out_shape=(jax.ShapeDtypeStruct((nl, d_model, N_PAD), BF16),
                   jax.ShapeDtypeStruct((nl, NSA_T_ROWS, d_model), BF16)),
        compiler_params=pltpu.CompilerParams(
            dimension_semantics=("parallel", "parallel"), vmem_limit_bytes=VMEM_LIMIT),
        name="in_weight_layout",
    )(w_in)


RELAYOUT_TM = 256


def _relayout_kernel(w_ref, o_ref, ot_ref, *, pieces, t_pieces):
    w = w_ref[...]

    def build(parts):
        cols = []
        for src, n, factor in parts:
            if src is None:
                cols.append(jnp.zeros((w.shape[0], n), F32))
            else:
                cols.append(w[:, src:src + n] if factor == 1.0 else w[:, src:src + n] * factor)
        return jnp.concatenate(cols, axis=1)

    o_ref[...] = build(pieces).astype(BF16)
    ot_ref[...] = build(t_pieces).T[0:NSA_T_ROWS].astype(BF16)


def _compress_weights(pos, w1, w2):
    nl = pos.shape[0]
    g, dh, hid = NSA_KV_GROUPS, HEAD_DIM, CMP_HIDDEN
    assert g == 2
    p = jnp.tile(pos, (1, 1, g))

    def block_diag2(w):
        lead = ((0, 0),) * (w.ndim - 1)
        return jnp.concatenate([jnp.pad(w, lead + ((0, w.shape[-1]),)),
                                jnp.pad(w, lead + ((w.shape[-1], 0),))], axis=-2)

    w = block_diag2(w1.astype(BF16).reshape(nl, CMP_LEN, dh, hid))
    return p, w, block_diag2(w2.astype(BF16))


def _mla_weights(w_uq, w_ukv):
    nl = w_uq.shape[0]
    dq = MLA_NOPE_DIM + MLA_ROPE_DIM
    wq = w_uq.reshape(nl, MLA_Q_RANK, MLA_HEADS, dq)
    pe = wq[..., MLA_NOPE_DIM:]
    half = MLA_ROPE_DIM // 2
    rot = jnp.concatenate([jnp.zeros_like(wq[..., :MLA_NOPE_DIM]), -pe[..., half:], pe[..., :half]], axis=-1)
    pad = lambda w: jnp.pad(w, ((0, 0), (0, 0), (0, 0), (0, LANES - dq))).reshape(
        nl, MLA_Q_RANK, MLA_HEADS * LANES)
    wq = jnp.concatenate([pad(wq), pad(rot)], axis=-1)
    wkv = w_ukv.reshape(nl, MLA_KV_RANK, MLA_HEADS, MLA_NOPE_DIM + MLA_V_DIM)
    wk = jnp.pad(wkv[..., :MLA_NOPE_DIM], ((0, 0), (0, 0), (0, 0), (0, LANES - MLA_NOPE_DIM)))
    wk = wk.reshape(nl, MLA_KV_RANK, MLA_HEADS * LANES)
    wvt = wkv[..., MLA_NOPE_DIM:].reshape(nl, MLA_KV_RANK, MLA_V_ROWS).transpose(0, 2, 1)
    return wq.astype(BF16), wk.astype(BF16), wvt.astype(BF16)


def _out_weight(w_out):
    nl = w_out.shape[0]
    nsa = w_out[:, :NSA_Q_W].reshape(nl, NSA_HEADS, HEAD_DIM, D_MODEL)
    order = [h for j in range(NSA_REP) for h in (j, j + NSA_REP)]
    nsa = nsa[:, order].reshape(nl, NSA_Q_W, D_MODEL)
    mla = w_out[:, NSA_Q_W:NSA_Q_W + MLA_HEADS * MLA_V_DIM]
    ret = w_out[:, NSA_Q_W + MLA_HEADS * MLA_V_DIM:]
    full = 2 * LANES
    return jnp.concatenate([nsa, mla[:, :full], mla[:, full:], ret[:, full:], ret[:, :full]],
                           axis=1).astype(BF16)


def _retention_tables(gn_gain):
    nh = 2 * RET_PAIRS
    slot_head = np.array([RET_HEADS if h is None else h for h in RET_SLOTS])
    log_g = jnp.log(1.0 - 2.0 ** (-5.0 - jnp.asarray(slot_head, F32)))
    i = jnp.arange(RET_CHUNK, dtype=F32)
    diff = i[:, None] - i[None, :]
    intra = jnp.where(diff >= 0, jnp.exp(jnp.maximum(diff, 0.0)[None] * log_g[:, None, None]), 0.0)
    read_decay = jnp.exp((i + 1.0)[None, :] * log_g[:, None])
    write_decay = jnp.exp((RET_CHUNK - 1.0 - i)[None, :] * log_g[:, None])
    chunk_decay = jnp.exp(RET_CHUNK * log_g)

    def lanes(t):
        t = t.reshape(RET_PAIRS, 2, -1)
        return jnp.repeat(t.transpose(0, 2, 1), HALF, axis=-1)

    gn = jnp.pad(gn_gain, ((0, 0), (0, nh - RET_HEADS), (0, 0)))[:, slot_head]
    gn = gn.reshape(gn.shape[0], RET_PAIRS, 1, LANES)
    return (gn, intra.reshape(RET_PAIRS, 2, RET_CHUNK, RET_CHUNK), lanes(read_decay), lanes(write_decay),
            lanes(chunk_decay[:, None]))


def _selection_overlap(seq):
    n_cmp = (seq - CMP_LEN) // CMP_STRIDE + 1
    n_sel = seq // SEL_BLOCK
    cs = np.arange(n_cmp) * CMP_STRIDE
    ss = np.arange(n_sel) * SEL_BLOCK
    ov = np.clip(np.minimum(cs[:, None] + CMP_LEN, ss[None, :] + SEL_BLOCK)
                 - np.maximum(cs[:, None], ss[None, :]), 0, None) / CMP_LEN
    ovl_t = np.zeros((n_sel, seq // CMP_STRIDE), np.float32)
    ovl_t[:, :n_cmp] = ov.T
    return jnp.asarray(ovl_t, BF16)


def kernel(x, positions, ln1_gain, w_in, cmp_pos_k, cmp_w1_k, cmp_w2_k, cmp_pos_v, cmp_w1_v, cmp_w2_v,
           mla_q_norm, mla_w_uq, mla_kv_norm, mla_w_ukv, ret_gn_gain, w_out, ln2_gain, w_up, w_down,
           final_gain):
    batch, seq, _ = x.shape
    depth = w_in.shape[0]
    t = batch * seq

    tab_n, tab_m, tab_r = _rope_tables(positions)

    w_in_p, w_in_t = _in_weight(w_in)
    pos_k, w1_k, w2_k = _compress_weights(cmp_pos_k, cmp_w1_k, cmp_w2_k)
    pos_v, w1_v, w2_v = _compress_weights(cmp_pos_v, cmp_w1_v, cmp_w2_v)
    cmp_pos = jnp.stack([pos_k, pos_v], axis=1)
    cmp_w1 = jnp.stack([w1_k, w1_v], axis=1)
    w2_vt = w2_v.transpose(0, 2, 1)
    wq, wk, wvt = _mla_weights(mla_w_uq, mla_w_ukv)
    wo = _out_weight(w_out)
    wu = w_up.astype(BF16)
    wd = w_down.astype(BF16)
    gn, intra, rd, wdec, cd = _retention_tables(ret_gn_gain)
    ovl_t = _selection_overlap(seq)
    gf = final_gain.reshape(1, D_MODEL)

    x2 = x.reshape(t, D_MODEL)
    for l in range(depth):
        nsa, nsa_t, gate_t, k_cmp, v_cmp, ret, rgate, q_m, k_m, vt_m = _in_proj(
            x2, ln1_gain[l].reshape(1, D_MODEL), w_in_p, w_in_t, mla_q_norm[l].reshape(1, -1),
            mla_kv_norm[l].reshape(1, -1), wq, wk, wvt, l, tab_n, tab_m, tab_r)
        kc, vct = _compress(k_cmp, v_cmp, cmp_pos[l], cmp_w1, w2_k[l], w2_vt[l], l, batch, seq)
        o_nsa = _nsa_attention(nsa, nsa_t, gate_t, kc, vct, ovl_t, batch, seq)
        o_mla = _mla_attention(q_m, k_m, vt_m, batch, seq)
        o_ret = _retention(ret, rgate, gn[l], intra, rd, wdec, cd, batch, seq)
        x2 = _out_mlp(x2, o_nsa, o_mla, o_ret, wo, ln2_gain[l].reshape(1, D_MODEL), wu, wd, l, gf,
                      final_norm=(l == depth - 1))
    return x2.reshape(batch, seq, D_MODEL)
```

```python
import functools
import math

import numpy as np
import jax
import jax.numpy as jnp
from jax import lax
from jax.experimental import pallas as pl
from jax.experimental.pallas import tpu as pltpu

F32 = jnp.float32
BF16 = jnp.bfloat16

D_MODEL = 1024
HEAD_DIM = 64
NSA_HEADS = 6
NSA_KV_GROUPS = 2
NSA_REP = NSA_HEADS // NSA_KV_GROUPS
N_BRANCH = 3
CMP_LEN = 32
CMP_STRIDE = 16
CMP_HIDDEN = 2 * HEAD_DIM
SEL_BLOCK = 64
SEL_TOP_N = 16
WINDOW = 512
MLA_HEADS = 5
MLA_Q_RANK = 256
MLA_KV_RANK = 128
MLA_NOPE_DIM = 64
MLA_ROPE_DIM = 32
MLA_V_DIM = 64
RET_HEADS = 5
RET_CHUNK = 128
ROPE_THETA = 500000.0
PARTIAL_ROPE_DIM = HEAD_DIM // 4
RET_THETA = 10000.0
D_FF = 4 * D_MODEL
NORM_EPS = 1e-6
NEG_INF = -1e30
FORCE_SCORE = 1e9
LOG2E = math.log2(math.e)

NSA_Q_W = NSA_HEADS * HEAD_DIM
NSA_KV_W = NSA_KV_GROUPS * HEAD_DIM
NSA_GATE_W = NSA_HEADS * N_BRANCH
RET_W = RET_HEADS * HEAD_DIM
IN_SIZES = (NSA_Q_W, NSA_KV_W, NSA_KV_W, NSA_KV_W, NSA_KV_W, NSA_KV_W, NSA_KV_W, NSA_GATE_W,
            MLA_Q_RANK, MLA_KV_RANK, MLA_ROPE_DIM, RET_W, RET_W, RET_W, RET_W)

LANES = 128
HALF = LANES // 2
SUBLANES = 8
MXU_WIDTH = 256
VMEM_LIMIT = 56 * 1024 * 1024

NSA_TILES = 5
GATE_ROWS = 32
NSA_T_ROWS = 2 * LANES + GATE_ROWS
MLA_IN_TILES = 4
RET_PAIRS = 3
RET_SLOTS = (0, 1, 2, 3, None, 4)
RET_K_TILES = RET_PAIRS - 1
RET_ROPE_TILES = RET_PAIRS + RET_K_TILES
RET_MM_TILES = RET_ROPE_TILES + RET_PAIRS
RET_TILES = 3 * RET_PAIRS
IN_TILES = NSA_TILES + 2 + MLA_IN_TILES + RET_MM_TILES + RET_PAIRS
N_PAD = IN_TILES * LANES
MLA_V_ROWS = MLA_HEADS * MLA_V_DIM

IN_TM = 1024
IN_CHUNK_TILES = 4 * MXU_WIDTH // LANES
MLP_TM = 1024
ATT_T = 512
MLP_FF_CHUNK = 512


def _nn(a, b):
    return jnp.dot(a, b, preferred_element_type=F32)


def _nt(a, b):
    return lax.dot_general(a, b, (((1,), (1,)), ((), ())), preferred_element_type=F32)


def _tn(a, b):
    return lax.dot_general(a, b, (((0,), (0,)), ((), ())), preferred_element_type=F32)


def _rms(x, gain):
    return x * lax.rsqrt(jnp.mean(x * x, axis=-1, keepdims=True) + NORM_EPS) * gain


def _rope(val, tab, half):
    cos = tab[:, 0:LANES]
    sin_a = tab[:, LANES:2 * LANES]
    sin_b = tab[:, 2 * LANES:3 * LANES]
    return (val * cos + pltpu.roll(val, LANES - half, 1) * sin_a
            + pltpu.roll(val, half, 1) * sin_b)


def _lane_lo(shape):
    return lax.broadcasted_iota(jnp.int32, shape, len(shape) - 1) < HALF


def _mla_up(c, qn_ref, kvn_ref, wq_ref, wk_ref, wvt_ref, tm_ref, q_ref, k_ref, vt_ref):
    scale = (MLA_NOPE_DIM + MLA_ROPE_DIM) ** -0.5 * LOG2E
    cq = _rms(c[:, 0:MLA_Q_RANK], qn_ref[...]).astype(BF16)
    ckv = _rms(c[:, MLA_Q_RANK:MLA_Q_RANK + MLA_KV_RANK], kvn_ref[...]).astype(BF16)
    k_pe = _rope(c[:, 3 * LANES:4 * LANES], tm_ref[...], MLA_ROPE_DIM // 2)
    cos = tm_ref[:, 0:LANES] * scale
    sin = (tm_ref[:, 2 * LANES:3 * LANES] - tm_ref[:, LANES:2 * LANES]) * scale
    q = _nn(cq, wq_ref[...])
    k = _nn(ckv, wk_ref[...])
    w = MLA_HEADS * LANES
    for hd in range(MLA_HEADS):
        sl = slice(hd * LANES, (hd + 1) * LANES)
        rot = slice(w + hd * LANES, w + (hd + 1) * LANES)
        q_ref[:, sl] = (q[:, sl] * cos + q[:, rot] * sin).astype(BF16)
        k_ref[:, sl] = (k[:, sl] + k_pe).astype(BF16)
    vt_ref[...] = _nt(wvt_ref[...], ckv).astype(BF16)


def _inproj_kernel(x_ref, g_ref, w_ref, wt_ref, tn_ref, tm_ref, tr_ref, qn_ref, kvn_ref, wq_ref, wk_ref,
                   wvt_ref, nsa_ref, nsat_ref, gatet_ref, kcmp_ref, vcmp_ref, ret_ref, rgate_ref,
                   mq_ref, mk_ref, mvt_ref):
    h = _rms(x_ref[...], g_ref[...]).astype(BF16)
    tab_n = tn_ref[...]
    tab_r = tr_ref[...]

    chunks = [_nn(h, w_ref[:, c:min(c + IN_CHUNK_TILES * LANES, N_PAD)])
              for c in range(0, N_PAD, IN_CHUNK_TILES * LANES)]

    def tiles(first, n):
        cols = [chunks[i // IN_CHUNK_TILES][:, (i % IN_CHUNK_TILES) * LANES:(i % IN_CHUNK_TILES + 1) * LANES]
                for i in range(first, first + n)]
        return cols[0] if n == 1 else jnp.concatenate(cols, axis=1)

    t0 = 0
    for i in range(NSA_TILES):
        v = _rope(tiles(t0 + i, 1), tab_n, PARTIAL_ROPE_DIM // 2)
        if i < NSA_REP:
            v = v * LOG2E
        nsa_ref[:, i * LANES:(i + 1) * LANES] = v.astype(BF16)
    t0 += NSA_TILES
    kcmp_ref[...] = _rope(tiles(t0, 1), tab_n, PARTIAL_ROPE_DIM // 2)
    vcmp_ref[...] = tiles(t0 + 1, 1)
    t0 += 2
    _mla_up(tiles(t0, MLA_IN_TILES), qn_ref, kvn_ref, wq_ref, wk_ref, wvt_ref, tm_ref, mq_ref, mk_ref, mvt_ref)
    t0 += MLA_IN_TILES
    stored = []
    for i in range(RET_MM_TILES):
        v = tiles(t0 + i, 1)
        if i < RET_ROPE_TILES:
            v = _rope(v, tab_r, HEAD_DIM // 2)
        stored.append(v)
        if i == RET_ROPE_TILES - 1:
            stored.append(pltpu.roll(stored[RET_PAIRS - 1], HALF, 1))
    for i, v in enumerate(stored):
        ret_ref[:, i * LANES:(i + 1) * LANES] = v.astype(BF16)
    t0 += RET_MM_TILES
    rgate_ref[...] = tiles(t0, RET_PAIRS)
    at = _nt(wt_ref[...], h)
    nsat_ref[...] = at[0:2 * LANES, :].astype(BF16)
    gatet_ref[...] = at[2 * LANES:NSA_T_ROWS, :]


def _layer_spec(w, layer, **kwargs):
    zeros = (0,) * (w.ndim - 1)
    return pl.BlockSpec((None,) + w.shape[1:], lambda *_: (layer,) + zeros, **kwargs)


def _in_proj(x2, gain, w, wt, q_norm, kv_norm, wq, wk, wvt, layer, tab_n, tab_m, tab_r):
    t = x2.shape[0]
    row = lambda i: (i, 0)
    col = lambda i: (0, i)
    const = lambda i: (0, 0)
    out_shapes = (
        jax.ShapeDtypeStruct((t, NSA_TILES * LANES), BF16),
        jax.ShapeDtypeStruct((2 * LANES, t), BF16),
        jax.ShapeDtypeStruct((GATE_ROWS, t), F32),
        jax.ShapeDtypeStruct((t, LANES), F32),
        jax.ShapeDtypeStruct((t, LANES), F32),
        jax.ShapeDtypeStruct((t, RET_TILES * LANES), BF16),
        jax.ShapeDtypeStruct((t, RET_PAIRS * LANES), F32),
        jax.ShapeDtypeStruct((t, MLA_HEADS * LANES), BF16),
        jax.ShapeDtypeStruct((t, MLA_HEADS * LANES), BF16),
        jax.ShapeDtypeStruct((MLA_V_ROWS, t), BF16),
    )
    out_specs = tuple(
        pl.BlockSpec((s.shape[0], IN_TM), col) if s.shape[1] == t else pl.BlockSpec((IN_TM, s.shape[1]), row)
        for s in out_shapes)
    resident = dict(pipeline_mode=pl.Buffered(1))
    return pl.pallas_call(
        _inproj_kernel,
        grid=(t // IN_TM,),
        in_specs=[
            pl.BlockSpec((IN_TM, D_MODEL), row),
            pl.BlockSpec((1, D_MODEL), const),
            _layer_spec(w, layer, **resident),
            _layer_spec(wt, layer, **resident),
            pl.BlockSpec((IN_TM, 3 * LANES), row),
            pl.BlockSpec((IN_TM, 3 * LANES), row),
            pl.BlockSpec((IN_TM, 3 * LANES), row),
            pl.BlockSpec((1, MLA_Q_RANK), const),
            pl.BlockSpec((1, MLA_KV_RANK), const),
            _layer_spec(wq, layer, **resident),
            _layer_spec(wk, layer, **resident),
            _layer_spec(wvt, layer, **resident),
        ],
        out_specs=out_specs,
        out_shape=out_shapes,
        compiler_params=pltpu.CompilerParams(
            dimension_semantics=("parallel",), vmem_limit_bytes=VMEM_LIMIT),
        name="in_proj",
    )(x2, gain, w, wt, tab_n, tab_m, tab_r, q_norm, kv_norm, wq, wk, wvt)


def _compress_kernel(k_ref, v_ref, pos_ref, w1_ref, w2k_ref, w2vt_ref, kc_ref, vct_ref):
    n_blk = k_ref.shape[0] // CMP_STRIDE

    def hidden(src, i):
        toks = [src[pl.ds(r, n_blk, stride=CMP_STRIDE), :] for r in range(CMP_STRIDE)]
        halves = []
        for half in range(CMP_LEN // CMP_STRIDE):
            off = half * CMP_STRIDE
            x = jnp.concatenate([(toks[r] + pos_ref[i, off + r:off + r + 1, :]).astype(BF16)
                                 for r in range(CMP_STRIDE)], axis=1)
            w = w1_ref[i, off:off + CMP_STRIDE].reshape(CMP_STRIDE * LANES, NSA_KV_GROUPS * CMP_HIDDEN)
            halves.append(_nn(x, w))
        return jax.nn.gelu(halves[0] + pltpu.roll(halves[1], n_blk - 1, 0)).astype(BF16)

    kc_ref[0] = _nn(hidden(k_ref, 0), w2k_ref[...]).astype(BF16)
    vct_ref[0] = _nt(w2vt_ref[...], hidden(v_ref, 1)).astype(BF16)


def _compress(k_cmp, v_cmp, pos, w1, w2k, w2vt, layer, batch, seq):
    b = batch
    n_blk = seq // CMP_STRIDE
    return pl.pallas_call(
        _compress_kernel,
        grid=(b,),
        in_specs=[
            pl.BlockSpec((seq, LANES), lambda i: (i, 0)),
            pl.BlockSpec((seq, LANES), lambda i: (i, 0)),
            pl.BlockSpec(pos.shape, lambda i: (0, 0, 0)),
            _layer_spec(w1, layer),
            pl.BlockSpec(w2k.shape, lambda i: (0, 0)),
            pl.BlockSpec(w2vt.shape, lambda i: (0, 0)),
        ],
        out_specs=(pl.BlockSpec((1, n_blk, LANES), lambda i: (i, 0, 0)),
                   pl.BlockSpec((1, LANES, n_blk), lambda i: (i, 0, 0))),
        out_shape=(jax.ShapeDtypeStruct((b, n_blk, LANES), BF16),
                   jax.ShapeDtypeStruct((b, LANES, n_blk), BF16)),
        compiler_params=pltpu.CompilerParams(
            dimension_semantics=("parallel",), vmem_limit_bytes=VMEM_LIMIT),
        name="nsa_compress",
    )(k_cmp, v_cmp, pos, w1, w2k, w2vt)


SCORE_LOOKAHEAD = 1
ONES_ROWS = 16


def _tri_scores(k, q, dead_upper, mask, bias=None):
    half = k.shape[0] // 2
    dead = jnp.full((half, half), NEG_INF, F32)
    if dead_upper:
        top = _nt(k[:half], q)
        bottom = jnp.concatenate([dead, _nt(k[half:], q[half:])], axis=1)
    else:
        top = jnp.concatenate([_nt(k[:half], q[:half]), dead], axis=1)
        bottom = _nt(k[half:], q)
    s_t = jnp.concatenate([top, bottom], axis=0)
    return jnp.where(mask, s_t if bias is None else s_t + bias, NEG_INF)


def _tri_pv(dead_upper, v_ext, p):
    half = p.shape[0] // 2
    if dead_upper:
        full = _nn(v_ext[:, :half], p[:half])
        part = _nn(v_ext[:, half:], p[half:, half:])
        return jnp.concatenate([full[:, :half], full[:, half:] + part], axis=1)
    part = _nn(v_ext[:, :half], p[:half, :half])
    full = _nn(v_ext[:, half:], p[half:])
    return jnp.concatenate([full[:, :half] + part, full[:, half:]], axis=1)


def _softmax_steps(score_fns, v_ts, states, pv_fn=_nn):
    n = len(score_fns)
    s_ts = [score_fns[h]() if h < SCORE_LOOKAHEAD else None for h in range(n)]
    out = []
    for h in range(n):
        if h + SCORE_LOOKAHEAD < n:
            s_ts[h + SCORE_LOOKAHEAD] = score_fns[h + SCORE_LOOKAHEAD]()
        m_old, acc_old = states[h]
        m = jnp.maximum(m_old, jnp.max(s_ts[h], axis=0, keepdims=True))
        p = jnp.exp2(s_ts[h] - m).astype(BF16)
        s_ts[h] = None
        v_ext = jnp.concatenate([v_ts[h], jnp.ones((ONES_ROWS, v_ts[h].shape[1]), BF16)], axis=0)
        out.append((m, jnp.exp2(m_old - m) * acc_old + pv_fn(v_ext, p)))
    return tuple(out)


def _softmax_init(cols):
    return jnp.full((1, cols), NEG_INF, F32), jnp.zeros((HEAD_DIM + ONES_ROWS, cols), F32)


def _softmax_finish(state):
    acc = state[1]
    return acc[0:HEAD_DIM] * (1.0 / jnp.maximum(acc[HEAD_DIM:HEAD_DIM + 1], 1e-30))


def _nsa_kernel(q_ref, gatet_ref, kc_ref, vct_ref, ks_ref, kw_ref, vst_ref, vwt_ref, ovl_ref,
                o_ref, out_t_ref, selb_ref):
    tq = ATT_T
    qi = pl.program_id(1)
    q0 = pl.multiple_of(qi * tq, tq)
    lo1 = _lane_lo((1, LANES))
    group_lanes = (lo1, jnp.logical_not(lo1))
    gate = jax.nn.sigmoid(gatet_ref[...])
    heads = range(NSA_HEADS)

    def q_head(h):
        j, g = h % NSA_REP, h // NSA_REP
        tile = q_ref[:, j * LANES:(j + 1) * LANES]
        return jnp.where(group_lanes[g], tile, jnp.zeros_like(tile))

    def v_rows(ref, h, k0, n):
        g = h // NSA_REP
        return ref[g * HEAD_DIM:(g + 1) * HEAD_DIM, pl.ds(k0, n)]

    def emit(h, branch, qs, nq, o_t):
        j, g = h % NSA_REP, h // NSA_REP
        r0 = j * LANES + g * HEAD_DIM
        row = h * N_BRANCH + branch
        val = gate[row:row + 1, qs:qs + nq] * o_t
        if branch == 0:
            out_t_ref[r0:r0 + HEAD_DIM, qs:qs + nq] = val
        else:
            out_t_ref[r0:r0 + HEAD_DIM, qs:qs + nq] += val

    qm = [q_head(h) for h in heads]

    n_cmp_pad = kc_ref.shape[1]
    kc = kc_ref[0]
    vct = vct_ref[0]
    n_i = lax.broadcasted_iota(jnp.int32, (n_cmp_pad, tq), 0)
    t_l = q0 + lax.broadcasted_iota(jnp.int32, (n_cmp_pad, tq), 1)
    cmask = (n_i * CMP_STRIDE + (CMP_LEN - 1)) <= t_l
    cmask_f = cmask.astype(F32)
    s_cs = [jnp.where(cmask, _nt(kc, qm[h]), NEG_INF) for h in heads]
    p_cs = [jnp.exp2(s_t - jnp.max(s_t, axis=0, keepdims=True)) * cmask_f for s_t in s_cs]
    p_cs = [p * (1.0 / jnp.maximum(jnp.sum(p, axis=0, keepdims=True), 1e-30)) for p in p_cs]
    for h in heads:
        g = h // NSA_REP
        emit(h, 0, 0, tq, _nn(vct[g * HEAD_DIM:(g + 1) * HEAD_DIM, :], p_cs[h].astype(BF16)))

    n_sel = ovl_ref.shape[0]
    needs_rank = q0 + tq > SEL_TOP_N * SEL_BLOCK

    @pl.when(jnp.logical_not(needs_rank))
    def _():
        selb_ref[...] = jnp.zeros_like(selb_ref)

    @pl.when(needs_rank)
    def _():
        m_i = lax.broadcasted_iota(jnp.int32, (n_sel, tq), 0)
        cur = jnp.right_shift(q0 + lax.broadcasted_iota(jnp.int32, (n_sel, tq), 1),
                              SEL_BLOCK.bit_length() - 1)
        valid = m_i <= cur
        forced = (m_i == 0) | (m_i == cur) | (m_i == cur - 1)
        ovl = ovl_ref[...]
        sub = SUBLANES
        m_loc = lax.broadcasted_iota(jnp.int32, (sub, tq), 0)
        for g in range(NSA_KV_GROUPS):
            psum = p_cs[g * NSA_REP]
            for h in range(g * NSA_REP + 1, (g + 1) * NSA_REP):
                psum = psum + p_cs[h]
            p_hi = psum.astype(BF16)
            p_lo = (psum - p_hi.astype(F32)).astype(BF16)
            imp = _nn(ovl, p_hi) + _nn(ovl, p_lo)
            imp = jnp.where(valid & forced, FORCE_SCORE, imp)
            imp = jnp.where(valid, imp, NEG_INF)
            parts = [imp[i:i + sub] for i in range(0, n_sel, sub)]
            ranks = [jnp.zeros((sub, tq), jnp.int32) for _ in parts]
            for mp in range(n_sel):
                row = imp[mp:mp + 1, :]
                for i, part in enumerate(parts):
                    if i * sub + sub - 1 <= mp:
                        beats = row > part
                    elif i * sub > mp:
                        beats = row >= part
                    else:
                        beats = (row > part) | ((row == part) & (m_loc + i * sub > mp))
                    ranks[i] = ranks[i] + beats.astype(jnp.int32)
            rank = jnp.concatenate(ranks, axis=0)
            selb_ref[g] = jnp.where(rank < SEL_TOP_N, 0.0, NEG_INF)

    blocks_per_chunk = tq // SEL_BLOCK

    def sel_bias(g, c, n_keys, qs, nq):
        rows = [jnp.broadcast_to(selb_ref[g, pl.ds(c * blocks_per_chunk + i, 1), qs:qs + nq], (SEL_BLOCK, nq))
                for i in range(n_keys // SEL_BLOCK)]
        return jnp.concatenate(rows, axis=0)

    key_i = lax.broadcasted_iota(jnp.int32, (tq, tq), 0)
    qry_i = lax.broadcasted_iota(jnp.int32, (tq, tq), 1)
    causal_mask = key_i <= qry_i

    def sel_chunk(c, k0, states, causal):
        k = ks_ref[pl.ds(k0, tq), :]
        bias = [sel_bias(g, c, tq, 0, tq) for g in range(NSA_KV_GROUPS)]

        def score_fn(h):
            if causal:
                return _tri_scores(k, qm[h], True, causal_mask, bias[h // NSA_REP])
            return _nt(k, qm[h]) + bias[h // NSA_REP]

        return _softmax_steps([functools.partial(score_fn, h) for h in heads],
                              [v_rows(vst_ref, h, k0, tq) for h in heads], states,
                              functools.partial(_tri_pv, True) if causal else _nn)

    states = sel_chunk(qi, q0, tuple(_softmax_init(tq) for _ in heads), True)
    states = lax.fori_loop(
        0, qi, lambda c, st: sel_chunk(c, pl.multiple_of(c * tq, tq), st, False), states)
    for h in heads:
        emit(h, 1, 0, tq, _softmax_finish(states[h]))

    def win_chunk(k0, states, mask, dead_upper):
        k = kw_ref[pl.ds(k0, tq), :]
        if dead_upper is None:
            score_fns = [lambda h=h: jnp.where(mask, _nt(k, qm[h]), NEG_INF) for h in heads]
            pv_fn = _nn
        else:
            score_fns = [lambda h=h: _tri_scores(k, qm[h], dead_upper, mask) for h in heads]
            pv_fn = functools.partial(_tri_pv, dead_upper)
        return _softmax_steps(score_fns, [v_rows(vwt_ref, h, k0, tq) for h in heads], states, pv_fn)

    states = win_chunk(q0, tuple(_softmax_init(tq) for _ in heads), causal_mask, True)
    for d in range(1, WINDOW // tq + 1):
        k0 = pl.multiple_of(jnp.maximum(qi - d, 0) * tq, tq)
        in_band = (key_i - d * tq > qry_i - WINDOW) & (qi >= d)
        states = win_chunk(k0, states, in_band, False if d * tq == WINDOW else None)
    for h in heads:
        emit(h, 2, 0, tq, _softmax_finish(states[h]))

    for j in range(NSA_REP):
        o_ref[:, j * LANES:(j + 1) * LANES] = out_t_ref[j * LANES:(j + 1) * LANES, :].T.astype(BF16)


def _nsa_attention(nsa, nsa_t, gate_t, kc, vct, ovl_t, batch, seq):
    tq = ATT_T
    nq = seq // tq
    n_blk = kc.shape[1]
    qrow = lambda b, i: (b * nq + i, 0)
    return pl.pallas_call(
        _nsa_kernel,
        grid=(batch, nq),
        in_specs=[
            pl.BlockSpec((tq, NSA_REP * LANES), qrow),
            pl.BlockSpec((GATE_ROWS, tq), lambda b, i: (0, b * nq + i)),
            pl.BlockSpec((1, n_blk, LANES), lambda b, i: (b, 0, 0)),
            pl.BlockSpec((1, LANES, n_blk), lambda b, i: (b, 0, 0)),
            pl.BlockSpec((seq, LANES), lambda b, i: (b, 3)),
            pl.BlockSpec((seq, LANES), lambda b, i: (b, 4)),
            pl.BlockSpec((LANES, seq), lambda b, i: (0, b)),
            pl.BlockSpec((LANES, seq), lambda b, i: (1, b)),
            pl.BlockSpec(ovl_t.shape, lambda b, i: (0, 0)),
        ],
        out_specs=pl.BlockSpec((tq, NSA_REP * LANES), qrow),
        out_shape=jax.ShapeDtypeStruct((batch * seq, NSA_REP * LANES), BF16),
        scratch_shapes=[pltpu.VMEM((NSA_REP * LANES, tq), F32),
                        pltpu.VMEM((NSA_KV_GROUPS, seq // SEL_BLOCK, tq), F32)],
        compiler_params=pltpu.CompilerParams(
            dimension_semantics=("parallel", "arbitrary"), vmem_limit_bytes=VMEM_LIMIT),
        name="nsa_attention",
    )(nsa, gate_t, kc, vct, nsa, nsa, nsa_t, nsa_t, ovl_t)


def _mla_attn_kernel(q_ref, k_ref, vt_ref, o_ref, out_t_ref):
    tq = ATT_T
    qi = pl.program_id(1)
    q0 = pl.multiple_of(qi * tq, tq)
    heads = range(MLA_HEADS)
    qh = [q_ref[:, h * LANES:(h + 1) * LANES] for h in heads]

    def chunk(k0, states, causal):
        if causal:
            mask = (lax.broadcasted_iota(jnp.int32, (tq, tq), 0)
                    <= lax.broadcasted_iota(jnp.int32, (tq, tq), 1))

        def score_fn(h):
            k = k_ref[pl.ds(k0, tq), h * LANES:(h + 1) * LANES]
            return _tri_scores(k, qh[h], True, mask) if causal else _nt(k, qh[h])

        v_ts = [vt_ref[h * MLA_V_DIM:(h + 1) * MLA_V_DIM, pl.ds(k0, tq)] for h in heads]
        return _softmax_steps([functools.partial(score_fn, h) for h in heads], v_ts, states,
                              functools.partial(_tri_pv, True) if causal else _nn)

    states = chunk(q0, tuple(_softmax_init(tq) for _ in heads), True)
    states = lax.fori_loop(0, qi, lambda c, st: chunk(pl.multiple_of(c * tq, tq), st, False), states)
    for h in heads:
        out_t_ref[h * MLA_V_DIM:(h + 1) * MLA_V_DIM, :] = _softmax_finish(states[h])
    out_t_ref[MLA_V_ROWS:, :] = jnp.zeros((RET_PAIRS * LANES - MLA_V_ROWS, tq), F32)
    for j in range(RET_PAIRS):
        o_ref[:, j * LANES:(j + 1) * LANES] = out_t_ref[j * LANES:(j + 1) * LANES, :].T.astype(BF16)


def _mla_attention(q, k, vt, batch, seq):
    tq = ATT_T
    nq = seq // tq
    return pl.pallas_call(
        _mla_attn_kernel,
        grid=(batch, nq),
        in_specs=[
            pl.BlockSpec((tq, MLA_HEADS * LANES), lambda b, i: (b * nq + i, 0)),
            pl.BlockSpec((seq, MLA_HEADS * LANES), lambda b, i: (b, 0)),
            pl.BlockSpec((MLA_V_ROWS, seq), lambda b, i: (0, b)),
        ],
        out_specs=pl.BlockSpec((tq, RET_PAIRS * LANES), lambda b, i: (b * nq + i, 0)),
        out_shape=jax.ShapeDtypeStruct((batch * seq, RET_PAIRS * LANES), BF16),
        scratch_shapes=[pltpu.VMEM((RET_PAIRS * LANES, tq), F32)],
        compiler_params=pltpu.CompilerParams(
            dimension_semantics=("parallel", "arbitrary"), vmem_limit_bytes=VMEM_LIMIT),
        name="mla_attention",
    )(q, k, vt)


RET_UNROLL = 16


def _retention_kernel(q_ref, k_ref, v_ref, gate_ref, gn_ref, intra_ref, rd_ref, wd_ref, cd_ref, o_ref,
                      kv_ref, st_ref):
    c_len = RET_CHUNK
    n_chunks = q_ref.shape[0] // c_len
    lo = _lane_lo((1, LANES))
    hi = jnp.logical_not(lo)
    blockdiag = (lax.broadcasted_iota(jnp.int32, (LANES, LANES), 0) < HALF) == _lane_lo((LANES, LANES))
    intra_a = intra_ref[0, 0]
    intra_b = intra_ref[0, 1]
    read_decay = rd_ref[0]
    write_decay = wd_ref[0]
    chunk_decay = cd_ref[0]
    gn = gn_ref[0]

    averager = jnp.where(blockdiag, 1.0 / HEAD_DIM, 0.0).astype(BF16)

    def half_mean(x):
        x_hi = x.astype(BF16)
        x_lo = (x - x_hi.astype(F32)).astype(BF16)
        return _nn(x_hi, averager) + _nn(x_lo, averager)

    def kv_body(c, carry):
        r0 = pl.multiple_of(c * c_len, c_len)
        kc = k_ref[pl.ds(r0, c_len), :]
        kv_ref[c] = _tn((kc.astype(F32) * write_decay).astype(BF16), v_ref[pl.ds(r0, c_len), :])
        return carry

    lax.fori_loop(0, n_chunks, kv_body, 0, unroll=RET_UNROLL)

    state = jnp.zeros((LANES, LANES), F32)
    for c in range(n_chunks):
        st_ref[c] = jnp.where(blockdiag, state, 0.0).astype(BF16)
        state = state * chunk_decay + kv_ref[c]

    def out_body(step, carry):
        cs = [step * RET_UNROLL + u for u in range(RET_UNROLL)]
        rows = [pl.ds(pl.multiple_of(c * c_len, c_len), c_len) for c in cs]
        qs = [q_ref[r, :] for r in rows]
        ks = [k_ref[r, :] for r in rows]
        vs = [v_ref[r, :] for r in rows]
        zero = jnp.zeros_like(qs[0])
        sa = [(_nt(jnp.where(lo, q, zero), k) * intra_a).astype(BF16) for q, k in zip(qs, ks)]
        sb = [(_nt(jnp.where(hi, q, zero), k) * intra_b).astype(BF16) for q, k in zip(qs, ks)]
        cross = [_nn(q, st_ref[c]) * read_decay for q, c in zip(qs, cs)]
        os_ = [jnp.where(lo, _nn(a, v), _nn(b, v)) + x for a, b, v, x in zip(sa, sb, vs, cross)]
        ds_ = [o - half_mean(o) for o in os_]
        ys = [d * lax.rsqrt(half_mean(d * d) + NORM_EPS) * gn for d in ds_]
        for r, y in zip(rows, ys):
            o_ref[r, :] = (jax.nn.silu(gate_ref[r, :]) * y).astype(BF16)
        return carry

    lax.fori_loop(0, n_chunks // RET_UNROLL, out_body, 0)


def _retention_all_kernel(q_ref, k_ref, v_ref, gate_ref, gn_ref, intra_ref, rd_ref, wd_ref, cd_ref, o_ref,
                          kv_ref, st_ref):
    for j in range(RET_PAIRS):
        cols = slice(j * LANES, (j + 1) * LANES)
        one = slice(j, j + 1)
        _retention_kernel(q_ref.at[:, cols], k_ref.at[:, cols], v_ref.at[:, cols], gate_ref.at[:, cols],
                          gn_ref.at[one], intra_ref.at[one], rd_ref.at[one], wd_ref.at[one], cd_ref.at[one],
                          o_ref.at[:, cols], kv_ref.at[j], st_ref.at[j])


def _retention(ret, rgate, gn, intra, rd, wd, cd, batch, seq):
    whole = lambda a: pl.BlockSpec(a.shape, lambda b: (0,) * a.ndim)
    width = RET_PAIRS * LANES
    return pl.pallas_call(
        _retention_all_kernel,
        grid=(batch,),
        in_specs=[
            pl.BlockSpec((seq, width), lambda b: (b, 0)),
            pl.BlockSpec((seq, width), lambda b: (b, 1)),
            pl.BlockSpec((seq, width), lambda b: (b, 2)),
            pl.BlockSpec((seq, width), lambda b: (b, 0)),
            whole(gn), whole(intra), whole(rd), whole(wd), whole(cd),
        ],
        out_specs=pl.BlockSpec((seq, width), lambda b: (b, 0)),
        out_shape=jax.ShapeDtypeStruct((batch * seq, width), BF16),
        scratch_shapes=[pltpu.VMEM((RET_PAIRS, seq // RET_CHUNK, LANES, LANES), F32),
                        pltpu.VMEM((RET_PAIRS, seq // RET_CHUNK, LANES, LANES), BF16)],
        compiler_params=pltpu.CompilerParams(
            dimension_semantics=("parallel",), vmem_limit_bytes=VMEM_LIMIT),
        name="retention",
    )(ret, ret, ret, rgate, gn, intra, rd, wd, cd)


def _out_mlp_kernel(x_ref, nsa_ref, mla_ref, ret_ref, wo_ref, g2_ref, wu_ref, wd_ref, gf_ref, o_ref,
                    *, final_norm):
    full = 2 * LANES
    mixed = jnp.concatenate([nsa_ref[...], mla_ref[:, :full], mla_ref[:, full:] + ret_ref[:, full:],
                             ret_ref[:, :full]], axis=1)
    x = x_ref[...] + _nn(mixed, wo_ref[...])
    h = _rms(x, g2_ref[...]).astype(BF16)
    y = x
    for c in range(D_FF // MLP_FF_CHUNK):
        sl = slice(c * MLP_FF_CHUNK, (c + 1) * MLP_FF_CHUNK)
        u = jnp.maximum(_nn(h, wu_ref[:, sl]), 0.0)
        y = y + _nn((u * u).astype(BF16), wd_ref[sl, :])
    if final_norm:
        y = _rms(y, gf_ref[...])
    o_ref[...] = y


def _out_mlp(x2, o_nsa, o_mla, o_ret, wo, g2, wu, wd, layer, gf, final_norm):
    t = x2.shape[0]
    w = RET_PAIRS * LANES
    row = lambda i: (i, 0)
    const = lambda i: (0, 0)
    resident = dict(pipeline_mode=pl.Buffered(1))
    return pl.pallas_call(
        functools.partial(_out_mlp_kernel, final_norm=final_norm),
        grid=(t // MLP_TM,),
        in_specs=[
            pl.BlockSpec((MLP_TM, D_MODEL), row),
            pl.BlockSpec((MLP_TM, w), row),
            pl.BlockSpec((MLP_TM, w), row),
            pl.BlockSpec((MLP_TM, w), row),
            _layer_spec(wo, layer, **resident),
            pl.BlockSpec((1, D_MODEL), const),
            _layer_spec(wu, layer, **resident),
            _layer_spec(wd, layer, **resident),
            pl.BlockSpec((1, D_MODEL), const),
        ],
        out_specs=pl.BlockSpec((MLP_TM, D_MODEL), row),
        out_shape=jax.ShapeDtypeStruct((t, D_MODEL), F32),
        compiler_params=pltpu.CompilerParams(
            dimension_semantics=("parallel",), vmem_limit_bytes=VMEM_LIMIT),
        name="out_mlp",
    )(x2, o_nsa, o_mla, o_ret, wo, g2, wu, wd, gf)


ROPE_KINDS = (
    (PARTIAL_ROPE_DIM, ROPE_THETA, HEAD_DIM, 0),
    (MLA_ROPE_DIM, ROPE_THETA, LANES, HALF),
    (HEAD_DIM, RET_THETA, HEAD_DIM, 0),
)
ROPE_TM = 2048


def _rope_placement():
    n_angles = sum(dim // 2 for dim, _, _, _ in ROPE_KINDS)
    assert 2 * n_angles <= LANES
    place = np.zeros((LANES, 3 * LANES * len(ROPE_KINDS)), np.float32)
    fill = np.zeros((1, 3 * LANES * len(ROPE_KINDS)), np.float32)
    row0 = 0
    for kind, (dim, _, period, base) in enumerate(ROPE_KINDS):
        half = dim // 2
        col0 = kind * 3 * LANES
        for lane in range(LANES):
            rel = (lane - base) % period
            first = lane >= base and rel < half
            second = lane >= base and half <= rel < dim
            if first or second:
                angle = rel if first else rel - half
                place[row0 + angle, col0 + lane] = 1.0
                place[n_angles + row0 + angle, col0 + (1 if first else 2) * LANES + lane] = -1.0 if first else 1.0
            else:
                fill[0, col0 + lane] = 1.0
        row0 += half
    return jnp.asarray(place, BF16), jnp.asarray(fill)


def _rope_kernel(cs_ref, place_ref, fill_ref, *out_refs):
    x = cs_ref[...]
    x1 = x.astype(BF16)
    r1 = x - x1.astype(F32)
    x2 = r1.astype(BF16)
    x3 = (r1 - x2.astype(F32)).astype(BF16)
    place = place_ref[...]
    tab = _tn(x1, place) + _tn(x2, place) + _tn(x3, place) + fill_ref[...]
    for i, ref in enumerate(out_refs):
        ref[...] = tab[:, i * 3 * LANES:(i + 1) * 3 * LANES]


def _rope_tables(positions):
    inv = jnp.concatenate([1.0 / (theta ** (jnp.arange(0, dim, 2, dtype=F32) / dim))
                           for dim, theta, _, _ in ROPE_KINDS])
    ang = inv[:, None] * positions.reshape(-1).astype(F32)[None, :]
    compact = jnp.concatenate([jnp.cos(ang), jnp.sin(ang)], axis=0)
    compact = jnp.pad(compact, ((0, LANES - compact.shape[0]), (0, 0)))
    place, fill = _rope_placement()
    t = compact.shape[1]
    tab_shape = jax.ShapeDtypeStruct((t, 3 * LANES), F32)
    return pl.pallas_call(
        _rope_kernel,
        grid=(t // ROPE_TM,),
        in_specs=[pl.BlockSpec((LANES, ROPE_TM), lambda i: (0, i)),
                  pl.BlockSpec(place.shape, lambda i: (0, 0)),
                  pl.BlockSpec(fill.shape, lambda i: (0, 0))],
        out_specs=tuple(pl.BlockSpec((ROPE_TM, 3 * LANES), lambda i: (i, 0)) for _ in ROPE_KINDS),
        out_shape=tuple(tab_shape for _ in ROPE_KINDS),
        compiler_params=pltpu.CompilerParams(
            dimension_semantics=("parallel",), vmem_limit_bytes=VMEM_LIMIT),
        name="rope_tables",
    )(compact, place, fill)


def _in_weight(w_in):
    offs = np.cumsum((0,) + IN_SIZES)
    (nsa_q, k_cmp, v_cmp, k_slc, v_slc, k_win, v_win, gate,
     cq, ckv, kpe, ret_q, ret_k, ret_v, ret_g) = [np.arange(offs[i], offs[i + 1]) for i in range(len(IN_SIZES))]
    scale = HEAD_DIM ** -0.5
    head = lambda cols, h: cols[h * HEAD_DIM:(h + 1) * HEAD_DIM]
    pad = lambda n: None if n == 0 else -np.ones(n, np.int64)
    pieces = []
    for j in range(NSA_REP):
        pieces += [(head(nsa_q, j), scale), (head(nsa_q, j + NSA_REP), scale)]
    pieces += [(c, 1.0) for c in (k_slc, k_win, k_cmp, v_cmp, cq, ckv)]
    pieces += [(pad(HALF), 0.0), (kpe, 1.0), (pad(LANES - HALF - MLA_ROPE_DIM), 0.0)]
    slots = lambda cols, f, fill: [fill if h is None else (head(cols, h), f) for h in RET_SLOTS]
    zero = (pad(HEAD_DIM), 0.0)
    pieces += slots(ret_q, 1.0, (head(ret_k, RET_SLOTS[-1]), scale))
    pieces += slots(ret_k, scale, zero)[:2 * RET_K_TILES]
    pieces += slots(ret_v, 1.0, zero) + slots(ret_g, 1.0, zero)
    t_pieces = [(v_slc, 1.0), (v_win, 1.0), (gate, 1.0), (pad(GATE_ROWS - NSA_GATE_W), 0.0)]

    t_pieces += [(pad(RET_PAIRS * LANES - NSA_T_ROWS), 0.0)]
    spans = lambda parts: tuple((None if c[0] < 0 else int(c[0]), len(c), f) for c, f in parts)
    assert sum(n for _, n, _ in spans(pieces)) == N_PAD
    nl, d_model, n_in = w_in.shape
    return pl.pallas_call(
        functools.partial(_relayout_kernel, pieces=spans(pieces), t_pieces=spans(t_pieces)),
        grid=(nl, d_model // RELAYOUT_TM),
        in_specs=[pl.BlockSpec((None, RELAYOUT_TM, n_in), lambda l, i: (l, i, 0))],
        out_specs=(pl.BlockSpec((None, RELAYOUT_TM, N_PAD), lambda l, i: (l, i, 0)),
                   pl.BlockSpec((None, NSA_T_ROWS, RELAYOUT_TM), lambda l, i: (l, 0, i))),
        out_shape=(jax.ShapeDtypeStruct((nl, d_model, N_PAD), BF16),
                   jax.ShapeDtypeStruct((nl, NSA_T_ROWS, d_model), BF16)),
        compiler_params=pltpu.CompilerParams(
            dimension_semantics=("parallel", "parallel"), vmem_limit_bytes=VMEM_LIMIT),
        name="in_weight_layout",
    )(w_in)


RELAYOUT_TM = 256


def _relayout_kernel(w_ref, o_ref, ot_ref, *, pieces, t_pieces):
    w = w_ref[...]

    def build(parts):
        cols = []
        for src, n, factor in parts:
            if src is None:
                cols.append(jnp.zeros((w.shape[0], n), F32))
            else:
                cols.append(w[:, src:src + n] if factor == 1.0 else w[:, src:src + n] * factor)
        return jnp.concatenate(cols, axis=1)

    o_ref[...] = build(pieces).astype(BF16)
    ot_ref[...] = build(t_pieces).T[0:NSA_T_ROWS].astype(BF16)


def _compress_weights(pos, w1, w2):
    nl = pos.shape[0]
    g, dh, hid = NSA_KV_GROUPS, HEAD_DIM, CMP_HIDDEN
    assert g == 2
    p = jnp.tile(pos, (1, 1, g))

    def block_diag2(w):
        lead = ((0, 0),) * (w.ndim - 1)
        return jnp.concatenate([jnp.pad(w, lead + ((0, w.shape[-1]),)),
                                jnp.pad(w, lead + ((w.shape[-1], 0),))], axis=-2)

    w = block_diag2(w1.astype(BF16).reshape(nl, CMP_LEN, dh, hid))
    return p, w, block_diag2(w2.astype(BF16))


def _mla_weights(w_uq, w_ukv):
    nl = w_uq.shape[0]
    dq = MLA_NOPE_DIM + MLA_ROPE_DIM
    wq = w_uq.reshape(nl, MLA_Q_RANK, MLA_HEADS, dq)
    pe = wq[..., MLA_NOPE_DIM:]
    half = MLA_ROPE_DIM // 2
    rot = jnp.concatenate([jnp.zeros_like(wq[..., :MLA_NOPE_DIM]), -pe[..., half:], pe[..., :half]], axis=-1)
    pad = lambda w: jnp.pad(w, ((0, 0), (0, 0), (0, 0), (0, LANES - dq))).reshape(
        nl, MLA_Q_RANK, MLA_HEADS * LANES)
    wq = jnp.concatenate([pad(wq), pad(rot)], axis=-1)
    wkv = w_ukv.reshape(nl, MLA_KV_RANK, MLA_HEADS, MLA_NOPE_DIM + MLA_V_DIM)
    wk = jnp.pad(wkv[..., :MLA_NOPE_DIM], ((0, 0), (0, 0), (0, 0), (0, LANES - MLA_NOPE_DIM)))
    wk = wk.reshape(nl, MLA_KV_RANK, MLA_HEADS * LANES)
    wvt = wkv[..., MLA_NOPE_DIM:].reshape(nl, MLA_KV_RANK, MLA_V_ROWS).transpose(0, 2, 1)
    return wq.astype(BF16), wk.astype(BF16), wvt.astype(BF16)


def _out_weight(w_out):
    nl = w_out.shape[0]
    nsa = w_out[:, :NSA_Q_W].reshape(nl, NSA_HEADS, HEAD_DIM, D_MODEL)
    order = [h for j in range(NSA_REP) for h in (j, j + NSA_REP)]
    nsa = nsa[:, order].reshape(nl, NSA_Q_W, D_MODEL)
    mla = w_out[:, NSA_Q_W:NSA_Q_W + MLA_HEADS * MLA_V_DIM]
    ret = w_out[:, NSA_Q_W + MLA_HEADS * MLA_V_DIM:]
    full = 2 * LANES
    return jnp.concatenate([nsa, mla[:, :full], mla[:, full:], ret[:, full:], ret[:, :full]],
                           axis=1).astype(BF16)


def _retention_tables(gn_gain):
    nh = 2 * RET_PAIRS
    slot_head = np.array([RET_HEADS if h is None else h for h in RET_SLOTS])
    log_g = jnp.log(1.0 - 2.0 ** (-5.0 - jnp.asarray(slot_head, F32)))
    i = jnp.arange(RET_CHUNK, dtype=F32)
    diff = i[:, None] - i[None, :]
    intra = jnp.where(diff >= 0, jnp.exp(jnp.maximum(diff, 0.0)[None] * log_g[:, None, None]), 0.0)
    read_decay = jnp.exp((i + 1.0)[None, :] * log_g[:, None])
    write_decay = jnp.exp((RET_CHUNK - 1.0 - i)[None, :] * log_g[:, None])
    chunk_decay = jnp.exp(RET_CHUNK * log_g)

    def lanes(t):
        t = t.reshape(RET_PAIRS, 2, -1)
        return jnp.repeat(t.transpose(0, 2, 1), HALF, axis=-1)

    gn = jnp.pad(gn_gain, ((0, 0), (0, nh - RET_HEADS), (0, 0)))[:, slot_head]
    gn = gn.reshape(gn.shape[0], RET_PAIRS, 1, LANES)
    return (gn, intra.reshape(RET_PAIRS, 2, RET_CHUNK, RET_CHUNK), lanes(read_decay), lanes(write_decay),
            lanes(chunk_decay[:, None]))


def _selection_overlap(seq):
    n_cmp = (seq - CMP_LEN) // CMP_STRIDE + 1
    n_sel = seq // SEL_BLOCK
    cs = np.arange(n_cmp) * CMP_STRIDE
    ss = np.arange(n_sel) * SEL_BLOCK
    ov = np.clip(np.minimum(cs[:, None] + CMP_LEN, ss[None, :] + SEL_BLOCK)
                 - np.maximum(cs[:, None], ss[None, :]), 0, None) / CMP_LEN
    ovl_t = np.zeros((n_sel, seq // CMP_STRIDE), np.float32)
    ovl_t[:, :n_cmp] = ov.T
    return jnp.asarray(ovl_t, BF16)


def kernel(x, positions, ln1_gain, w_in, cmp_pos_k, cmp_w1_k, cmp_w2_k, cmp_pos_v, cmp_w1_v, cmp_w2_v,
           mla_q_norm, mla_w_uq, mla_kv_norm, mla_w_ukv, ret_gn_gain, w_out, ln2_gain, w_up, w_down,
           final_gain):
    batch, seq, _ = x.shape
    depth = w_in.shape[0]
    t = batch * seq

    tab_n, tab_m, tab_r = _rope_tables(positions)

    w_in_p, w_in_t = _in_weight(w_in)
    pos_k, w1_k, w2_k = _compress_weights(cmp_pos_k, cmp_w1_k, cmp_w2_k)
    pos_v, w1_v, w2_v = _compress_weights(cmp_pos_v, cmp_w1_v, cmp_w2_v)
    cmp_pos = jnp.stack([pos_k, pos_v], axis=1)
    cmp_w1 = jnp.stack([w1_k, w1_v], axis=1)
    w2_vt = w2_v.transpose(0, 2, 1)
    wq, wk, wvt = _mla_weights(mla_w_uq, mla_w_ukv)
    wo = _out_weight(w_out)
    wu = w_up.astype(BF16)
    wd = w_down.astype(BF16)
    gn, intra, rd, wdec, cd = _retention_tables(ret_gn_gain)
    ovl_t = _selection_overlap(seq)
    gf = final_gain.reshape(1, D_MODEL)

    x2 = x.reshape(t, D_MODEL)
    for l in range(depth):
        nsa, nsa_t, gate_t, k_cmp, v_cmp, ret, rgate, q_m, k_m, vt_m = _in_proj(
            x2, ln1_gain[l].reshape(1, D_MODEL), w_in_p, w_in_t, mla_q_norm[l].reshape(1, -1),
            mla_kv_norm[l].reshape(1, -1), wq, wk, wvt, l, tab_n, tab_m, tab_r)
        kc, vct = _compress(k_cmp, v_cmp, cmp_pos[l], cmp_w1, w2_k[l], w2_vt[l], l, batch, seq)
        o_nsa = _nsa_attention(nsa, nsa_t, gate_t, kc, vct, ovl_t, batch, seq)
        o_mla = _mla_attention(q_m, k_m, vt_m, batch, seq)
        o_ret = _retention(ret, rgate, gn[l], intra, rd, wdec, cd, batch, seq)
        x2 = _out_mlp(x2, o_nsa, o_mla, o_ret, wo, ln2_gain[l].reshape(1, D_MODEL), wu, wd, l, gf,
                      final_norm=(l == depth - 1))
    return x2.reshape(batch, seq, D_MODEL)
```

```python
import functools
import math

import numpy as np
import jax
import jax.numpy as jnp
from jax import lax
from jax.experimental import pallas as pl
from jax.experimental.pallas import tpu as pltpu

F32 = jnp.float32
BF16 = jnp.bfloat16

D_MODEL = 1024
HEAD_DIM = 64
NSA_HEADS = 6
NSA_KV_GROUPS = 2
NSA_REP = NSA_HEADS // NSA_KV_GROUPS
N_BRANCH = 3
CMP_LEN = 32
CMP_STRIDE = 16
CMP_HIDDEN = 2 * HEAD_DIM
SEL_BLOCK = 64
SEL_TOP_N = 16
WINDOW = 512
MLA_HEADS = 5
MLA_Q_RANK = 256
MLA_KV_RANK = 128
MLA_NOPE_DIM = 64
MLA_ROPE_DIM = 32
MLA_V_DIM = 64
RET_HEADS = 5
RET_CHUNK = 128
ROPE_THETA = 500000.0
PARTIAL_ROPE_DIM = HEAD_DIM // 4
RET_THETA = 10000.0
D_FF = 4 * D_MODEL
NORM_EPS = 1e-6
NEG_INF = -1e30
FORCE_SCORE = 1e9
LOG2E = math.log2(math.e)

NSA_Q_W = NSA_HEADS * HEAD_DIM
NSA_KV_W = NSA_KV_GROUPS * HEAD_DIM
NSA_GATE_W = NSA_HEADS * N_BRANCH
RET_W = RET_HEADS * HEAD_DIM
IN_SIZES = (NSA_Q_W, NSA_KV_W, NSA_KV_W, NSA_KV_W, NSA_KV_W, NSA_KV_W, NSA_KV_W, NSA_GATE_W,
            MLA_Q_RANK, MLA_KV_RANK, MLA_ROPE_DIM, RET_W, RET_W, RET_W, RET_W)

LANES = 128
HALF = LANES // 2
SUBLANES = 8
MXU_WIDTH = 256
VMEM_LIMIT = 56 * 1024 * 1024

NSA_TILES = 5
GATE_ROWS = 32
NSA_T_ROWS = 2 * LANES + GATE_ROWS
MLA_IN_TILES = 4
RET_PAIRS = 3
RET_SLOTS = (0, 1, 2, 3, None, 4)
RET_K_TILES = RET_PAIRS - 1
RET_ROPE_TILES = RET_PAIRS + RET_K_TILES
RET_MM_TILES = RET_ROPE_TILES + RET_PAIRS
RET_TILES = 3 * RET_PAIRS
IN_TILES = NSA_TILES + 2 + MLA_IN_TILES + RET_MM_TILES + RET_PAIRS
N_PAD = IN_TILES * LANES
MLA_V_ROWS = MLA_HEADS * MLA_V_DIM

IN_TM = 1024
IN_CHUNK_TILES = 4 * MXU_WIDTH // LANES
MLP_TM = 1024
ATT_T = 512
MLP_FF_CHUNK = 512


def _nn(a, b):
    return jnp.dot(a, b, preferred_element_type=F32)


def _nt(a, b):
    return lax.dot_general(a, b, (((1,), (1,)), ((), ())), preferred_element_type=F32)


def _tn(a, b):
    return lax.dot_general(a, b, (((0,), (0,)), ((), ())), preferred_element_type=F32)


def _rms(x, gain):
    return x * lax.rsqrt(jnp.mean(x * x, axis=-1, keepdims=True) + NORM_EPS) * gain


def _rope(val, tab, half):
    cos = tab[:, 0:LANES]
    sin_a = tab[:, LANES:2 * LANES]
    sin_b = tab[:, 2 * LANES:3 * LANES]
    return (val * cos + pltpu.roll(val, LANES - half, 1) * sin_a
            + pltpu.roll(val, half, 1) * sin_b)


def _lane_lo(shape):
    return lax.broadcasted_iota(jnp.int32, shape, len(shape) - 1) < HALF


def _mla_up(c, qn_ref, kvn_ref, wq_ref, wk_ref, wvt_ref, tm_ref, q_ref, k_ref, vt_ref):
    scale = (MLA_NOPE_DIM + MLA_ROPE_DIM) ** -0.5 * LOG2E
    cq = _rms(c[:, 0:MLA_Q_RANK], qn_ref[...]).astype(BF16)
    ckv = _rms(c[:, MLA_Q_RANK:MLA_Q_RANK + MLA_KV_RANK], kvn_ref[...]).astype(BF16)
    k_pe = _rope(c[:, 3 * LANES:4 * LANES], tm_ref[...], MLA_ROPE_DIM // 2)
    cos = tm_ref[:, 0:LANES] * scale
    sin = (tm_ref[:, 2 * LANES:3 * LANES] - tm_ref[:, LANES:2 * LANES]) * scale
    q = _nn(cq, wq_ref[...])
    k = _nn(ckv, wk_ref[...])
    w = MLA_HEADS * LANES
    for hd in range(MLA_HEADS):
        sl = slice(hd * LANES, (hd + 1) * LANES)
        rot = slice(w + hd * LANES, w + (hd + 1) * LANES)
        q_ref[:, sl] = (q[:, sl] * cos + q[:, rot] * sin).astype(BF16)
        k_ref[:, sl] = (k[:, sl] + k_pe).astype(BF16)
    vt_ref[...] = _nt(wvt_ref[...], ckv).astype(BF16)


def _inproj_kernel(x_ref, g_ref, w_ref, wt_ref, tn_ref, tm_ref, tr_ref, qn_ref, kvn_ref, wq_ref, wk_ref,
                   wvt_ref, nsa_ref, nsat_ref, gatet_ref, kcmp_ref, vcmp_ref, ret_ref, rgate_ref,
                   mq_ref, mk_ref, mvt_ref):
    h = _rms(x_ref[...], g_ref[...]).astype(BF16)
    tab_n = tn_ref[...]
    tab_r = tr_ref[...]

    chunks = [_nn(h, w_ref[:, c:min(c + IN_CHUNK_TILES * LANES, N_PAD)])
              for c in range(0, N_PAD, IN_CHUNK_TILES * LANES)]

    def tiles(first, n):
        cols = [chunks[i // IN_CHUNK_TILES][:, (i % IN_CHUNK_TILES) * LANES:(i % IN_CHUNK_TILES + 1) * LANES]
                for i in range(first, first + n)]
        return cols[0] if n == 1 else jnp.concatenate(cols, axis=1)

    t0 = 0
    for i in range(NSA_TILES):
        v = _rope(tiles(t0 + i, 1), tab_n, PARTIAL_ROPE_DIM // 2)
        if i < NSA_REP:
            v = v * LOG2E
        nsa_ref[:, i * LANES:(i + 1) * LANES] = v.astype(BF16)
    t0 += NSA_TILES
    kcmp_ref[...] = _rope(tiles(t0, 1), tab_n, PARTIAL_ROPE_DIM // 2)
    vcmp_ref[...] = tiles(t0 + 1, 1)
    t0 += 2
    _mla_up(tiles(t0, MLA_IN_TILES), qn_ref, kvn_ref, wq_ref, wk_ref, wvt_ref, tm_ref, mq_ref, mk_ref, mvt_ref)
    t0 += MLA_IN_TILES
    stored = []
    for i in range(RET_MM_TILES):
        v = tiles(t0 + i, 1)
        if i < RET_ROPE_TILES:
            v = _rope(v, tab_r, HEAD_DIM // 2)
        stored.append(v)
        if i == RET_ROPE_TILES - 1:
            stored.append(pltpu.roll(stored[RET_PAIRS - 1], HALF, 1))
    for i, v in enumerate(stored):
        ret_ref[:, i * LANES:(i + 1) * LANES] = v.astype(BF16)
    t0 += RET_MM_TILES
    rgate_ref[...] = tiles(t0, RET_PAIRS)
    at = _nt(wt_ref[...], h)
    nsat_ref[...] = at[0:2 * LANES, :].astype(BF16)
    gatet_ref[...] = at[2 * LANES:NSA_T_ROWS, :]


def _layer_spec(w, layer, **kwargs):
    zeros = (0,) * (w.ndim - 1)
    return pl.BlockSpec((None,) + w.shape[1:], lambda *_: (layer,) + zeros, **kwargs)


def _in_proj(x2, gain, w, wt, q_norm, kv_norm, wq, wk, wvt, layer, tab_n, tab_m, tab_r):
    t = x2.shape[0]
    row = lambda i: (i, 0)
    col = lambda i: (0, i)
    const = lambda i: (0, 0)
    out_shapes = (
        jax.ShapeDtypeStruct((t, NSA_TILES * LANES), BF16),
        jax.ShapeDtypeStruct((2 * LANES, t), BF16),
        jax.ShapeDtypeStruct((GATE_ROWS, t), F32),
        jax.ShapeDtypeStruct((t, LANES), F32),
        jax.ShapeDtypeStruct((t, LANES), F32),
        jax.ShapeDtypeStruct((t, RET_TILES * LANES), BF16),
        jax.ShapeDtypeStruct((t, RET_PAIRS * LANES), F32),
        jax.ShapeDtypeStruct((t, MLA_HEADS * LANES), BF16),
        jax.ShapeDtypeStruct((t, MLA_HEADS * LANES), BF16),
        jax.ShapeDtypeStruct((MLA_V_ROWS, t), BF16),
    )
    out_specs = tuple(
        pl.BlockSpec((s.shape[0], IN_TM), col) if s.shape[1] == t else pl.BlockSpec((IN_TM, s.shape[1]), row)
        for s in out_shapes)
    resident = dict(pipeline_mode=pl.Buffered(1))
    return pl.pallas_call(
        _inproj_kernel,
        grid=(t // IN_TM,),
        in_specs=[
            pl.BlockSpec((IN_TM, D_MODEL), row),
            pl.BlockSpec((1, D_MODEL), const),
            _layer_spec(w, layer, **resident),
            _layer_spec(wt, layer, **resident),
            pl.BlockSpec((IN_TM, 3 * LANES), row),
            pl.BlockSpec((IN_TM, 3 * LANES), row),
            pl.BlockSpec((IN_TM, 3 * LANES), row),
            pl.BlockSpec((1, MLA_Q_RANK), const),
            pl.BlockSpec((1, MLA_KV_RANK), const),
            _layer_spec(wq, layer, **resident),
            _layer_spec(wk, layer, **resident),
            _layer_spec(wvt, layer, **resident),
        ],
        out_specs=out_specs,
        out_shape=out_shapes,
        compiler_params=pltpu.CompilerParams(
            dimension_semantics=("parallel",), vmem_limit_bytes=VMEM_LIMIT),
        name="in_proj",
    )(x2, gain, w, wt, tab_n, tab_m, tab_r, q_norm, kv_norm, wq, wk, wvt)


def _compress_kernel(k_ref, v_ref, pos_ref, w1_ref, w2k_ref, w2vt_ref, kc_ref, vct_ref):
    n_blk = k_ref.shape[0] // CMP_STRIDE

    def hidden(src, i):
        toks = [src[pl.ds(r, n_blk, stride=CMP_STRIDE), :] for r in range(CMP_STRIDE)]
        halves = []
        for half in range(CMP_LEN // CMP_STRIDE):
            off = half * CMP_STRIDE
            x = jnp.concatenate([(toks[r] + pos_ref[i, off + r:off + r + 1, :]).astype(BF16)
                                 for r in range(CMP_STRIDE)], axis=1)
            w = w1_ref[i, off:off + CMP_STRIDE].reshape(CMP_STRIDE * LANES, NSA_KV_GROUPS * CMP_HIDDEN)
            halves.append(_nn(x, w))
        return jax.nn.gelu(halves[0] + pltpu.roll(halves[1], n_blk - 1, 0)).astype(BF16)

    kc_ref[0] = _nn(hidden(k_ref, 0), w2k_ref[...]).astype(BF16)
    vct_ref[0] = _nt(w2vt_ref[...], hidden(v_ref, 1)).astype(BF16)


def _compress(k_cmp, v_cmp, pos, w1, w2k, w2vt, layer, batch, seq):
    b = batch
    n_blk = seq // CMP_STRIDE
    return pl.pallas_call(
        _compress_kernel,
        grid=(b,),
        in_specs=[
            pl.BlockSpec((seq, LANES), lambda i: (i, 0)),
            pl.BlockSpec((seq, LANES), lambda i: (i, 0)),
            pl.BlockSpec(pos.shape, lambda i: (0, 0, 0)),
            _layer_spec(w1, layer),
            pl.BlockSpec(w2k.shape, lambda i: (0, 0)),
            pl.BlockSpec(w2vt.shape, lambda i: (0, 0)),
        ],
        out_specs=(pl.BlockSpec((1, n_blk, LANES), lambda i: (i, 0, 0)),
                   pl.BlockSpec((1, LANES, n_blk), lambda i: (i, 0, 0))),
        out_shape=(jax.ShapeDtypeStruct((b, n_blk, LANES), BF16),
                   jax.ShapeDtypeStruct((b, LANES, n_blk), BF16)),
        compiler_params=pltpu.CompilerParams(
            dimension_semantics=("parallel",), vmem_limit_bytes=VMEM_LIMIT),
        name="nsa_compress",
    )(k_cmp, v_cmp, pos, w1, w2k, w2vt)


SCORE_LOOKAHEAD = 3
ONES_ROWS = 16


def _tri_scores(k, q, dead_upper, mask, bias=None):
    half = k.shape[0] // 2
    dead = jnp.full((half, half), NEG_INF, F32)
    if dead_upper:
        top = _nt(k[:half], q)
        bottom = jnp.concatenate([dead, _nt(k[half:], q[half:])], axis=1)
    else:
        top = jnp.concatenate([_nt(k[:half], q[:half]), dead], axis=1)
        bottom = _nt(k[half:], q)
    s_t = jnp.concatenate([top, bottom], axis=0)
    return jnp.where(mask, s_t if bias is None else s_t + bias, NEG_INF)


def _tri_pv(dead_upper, v_ext, p):
    half = p.shape[0] // 2
    if dead_upper:
        full = _nn(v_ext[:, :half], p[:half])
        part = _nn(v_ext[:, half:], p[half:, half:])
        return jnp.concatenate([full[:, :half], full[:, half:] + part], axis=1)
    part = _nn(v_ext[:, :half], p[:half, :half])
    full = _nn(v_ext[:, half:], p[half:])
    return jnp.concatenate([full[:, :half] + part, full[:, half:]], axis=1)


def _softmax_steps(score_fns, v_ts, states, pv_fn=_nn):
    n = len(score_fns)
    s_ts = [score_fns[h]() if h < SCORE_LOOKAHEAD else None for h in range(n)]
    out = []
    for h in range(n):
        if h + SCORE_LOOKAHEAD < n:
            s_ts[h + SCORE_LOOKAHEAD] = score_fns[h + SCORE_LOOKAHEAD]()
        m_old, acc_old = states[h]
        m = jnp.maximum(m_old, jnp.max(s_ts[h], axis=0, keepdims=True))
        p = jnp.exp2(s_ts[h] - m).astype(BF16)
        s_ts[h] = None
        v_ext = jnp.concatenate([v_ts[h], jnp.ones((ONES_ROWS, v_ts[h].shape[1]), BF16)], axis=0)
        out.append((m, jnp.exp2(m_old - m) * acc_old + pv_fn(v_ext, p)))
    return tuple(out)


def _softmax_init(cols):
    return jnp.full((1, cols), NEG_INF, F32), jnp.zeros((HEAD_DIM + ONES_ROWS, cols), F32)


def _softmax_finish(state):
    acc = state[1]
    return acc[0:HEAD_DIM] * (1.0 / jnp.maximum(acc[HEAD_DIM:HEAD_DIM + 1], 1e-30))


def _nsa_kernel(q_ref, gatet_ref, kc_ref, vct_ref, ks_ref, kw_ref, vst_ref, vwt_ref, ovl_ref,
                o_ref, out_t_ref, selb_ref):
    tq = ATT_T
    qi = pl.program_id(1)
    q0 = pl.multiple_of(qi * tq, tq)
    lo1 = _lane_lo((1, LANES))
    group_lanes = (lo1, jnp.logical_not(lo1))
    gate = jax.nn.sigmoid(gatet_ref[...])
    heads = range(NSA_HEADS)

    def q_head(h):
        j, g = h % NSA_REP, h // NSA_REP
        tile = q_ref[:, j * LANES:(j + 1) * LANES]
        return jnp.where(group_lanes[g], tile, jnp.zeros_like(tile))

    def v_rows(ref, h, k0, n):
        g = h // NSA_REP
        return ref[g * HEAD_DIM:(g + 1) * HEAD_DIM, pl.ds(k0, n)]

    def emit(h, branch, qs, nq, o_t):
        j, g = h % NSA_REP, h // NSA_REP
        r0 = j * LANES + g * HEAD_DIM
        row = h * N_BRANCH + branch
        val = gate[row:row + 1, qs:qs + nq] * o_t
        if branch == 0:
            out_t_ref[r0:r0 + HEAD_DIM, qs:qs + nq] = val
        else:
            out_t_ref[r0:r0 + HEAD_DIM, qs:qs + nq] += val

    qm = [q_head(h) for h in heads]

    n_cmp_pad = kc_ref.shape[1]
    kc = kc_ref[0]
    vct = vct_ref[0]
    n_i = lax.broadcasted_iota(jnp.int32, (n_cmp_pad, tq), 0)
    t_l = q0 + lax.broadcasted_iota(jnp.int32, (n_cmp_pad, tq), 1)
    cmask = (n_i * CMP_STRIDE + (CMP_LEN - 1)) <= t_l
    cmask_f = cmask.astype(F32)
    s_cs = [jnp.where(cmask, _nt(kc, qm[h]), NEG_INF) for h in heads]
    p_cs = [jnp.exp2(s_t - jnp.max(s_t, axis=0, keepdims=True)) * cmask_f for s_t in s_cs]
    p_cs = [p * (1.0 / jnp.maximum(jnp.sum(p, axis=0, keepdims=True), 1e-30)) for p in p_cs]
    for h in heads:
        g = h // NSA_REP
        emit(h, 0, 0, tq, _nn(vct[g * HEAD_DIM:(g + 1) * HEAD_DIM, :], p_cs[h].astype(BF16)))

    n_sel = ovl_ref.shape[0]
    needs_rank = q0 + tq > SEL_TOP_N * SEL_BLOCK

    @pl.when(jnp.logical_not(needs_rank))
    def _():
        selb_ref[...] = jnp.zeros_like(selb_ref)

    @pl.when(needs_rank)
    def _():
        m_i = lax.broadcasted_iota(jnp.int32, (n_sel, tq), 0)
        cur = jnp.right_shift(q0 + lax.broadcasted_iota(jnp.int32, (n_sel, tq), 1),
                              SEL_BLOCK.bit_length() - 1)
        valid = m_i <= cur
        forced = (m_i == 0) | (m_i == cur) | (m_i == cur - 1)
        ovl = ovl_ref[...]
        sub = SUBLANES
        m_loc = lax.broadcasted_iota(jnp.int32, (sub, tq), 0)
        for g in range(NSA_KV_GROUPS):
            psum = p_cs[g * NSA_REP]
            for h in range(g * NSA_REP + 1, (g + 1) * NSA_REP):
                psum = psum + p_cs[h]
            p_hi = psum.astype(BF16)
            p_lo = (psum - p_hi.astype(F32)).astype(BF16)
            imp = _nn(ovl, p_hi) + _nn(ovl, p_lo)
            imp = jnp.where(valid & forced, FORCE_SCORE, imp)
            imp = jnp.where(valid, imp, NEG_INF)
            parts = [imp[i:i + sub] for i in range(0, n_sel, sub)]
            ranks = [jnp.zeros((sub, tq), jnp.int32) for _ in parts]
            for mp in range(n_sel):
                row = imp[mp:mp + 1, :]
                for i, part in enumerate(parts):
                    if i * sub + sub - 1 <= mp:
                        beats = row > part
                    elif i * sub > mp:
                        beats = row >= part
                    else:
                        beats = (row > part) | ((row == part) & (m_loc + i * sub > mp))
                    ranks[i] = ranks[i] + beats.astype(jnp.int32)
            rank = jnp.concatenate(ranks, axis=0)
            selb_ref[g] = jnp.where(rank < SEL_TOP_N, 0.0, NEG_INF)

    blocks_per_chunk = tq // SEL_BLOCK

    def sel_bias(g, c, n_keys, qs, nq):
        rows = [jnp.broadcast_to(selb_ref[g, pl.ds(c * blocks_per_chunk + i, 1), qs:qs + nq], (SEL_BLOCK, nq))
                for i in range(n_keys // SEL_BLOCK)]
        return jnp.concatenate(rows, axis=0)

    key_i = lax.broadcasted_iota(jnp.int32, (tq, tq), 0)
    qry_i = lax.broadcasted_iota(jnp.int32, (tq, tq), 1)
    causal_mask = key_i <= qry_i

    def sel_chunk(c, k0, states, causal):
        k = ks_ref[pl.ds(k0, tq), :]
        bias = [sel_bias(g, c, tq, 0, tq) for g in range(NSA_KV_GROUPS)]

        def score_fn(h):
            if causal:
                return _tri_scores(k, qm[h], True, causal_mask, bias[h // NSA_REP])
            return _nt(k, qm[h]) + bias[h // NSA_REP]

        return _softmax_steps([functools.partial(score_fn, h) for h in heads],
                              [v_rows(vst_ref, h, k0, tq) for h in heads], states,
                              functools.partial(_tri_pv, True) if causal else _nn)

    states = sel_chunk(qi, q0, tuple(_softmax_init(tq) for _ in heads), True)
    states = lax.fori_loop(
        0, qi, lambda c, st: sel_chunk(c, pl.multiple_of(c * tq, tq), st, False), states)
    for h in heads:
        emit(h, 1, 0, tq, _softmax_finish(states[h]))

    def win_chunk(k0, states, mask, dead_upper):
        k = kw_ref[pl.ds(k0, tq), :]
        if dead_upper is None:
            score_fns = [lambda h=h: jnp.where(mask, _nt(k, qm[h]), NEG_INF) for h in heads]
            pv_fn = _nn
        else:
            score_fns = [lambda h=h: _tri_scores(k, qm[h], dead_upper, mask) for h in heads]
            pv_fn = functools.partial(_tri_pv, dead_upper)
        return _softmax_steps(score_fns, [v_rows(vwt_ref, h, k0, tq) for h in heads], states, pv_fn)

    states = win_chunk(q0, tuple(_softmax_init(tq) for _ in heads), causal_mask, True)
    for d in range(1, WINDOW // tq + 1):
        k0 = pl.multiple_of(jnp.maximum(qi - d, 0) * tq, tq)
        in_band = (key_i - d * tq > qry_i - WINDOW) & (qi >= d)
        states = win_chunk(k0, states, in_band, False if d * tq == WINDOW else None)
    for h in heads:
        emit(h, 2, 0, tq, _softmax_finish(states[h]))

    for j in range(NSA_REP):
        o_ref[:, j * LANES:(j + 1) * LANES] = out_t_ref[j * LANES:(j + 1) * LANES, :].T.astype(BF16)


def _nsa_attention(nsa, nsa_t, gate_t, kc, vct, ovl_t, batch, seq):
    tq = ATT_T
    nq = seq // tq
    n_blk = kc.shape[1]
    qrow = lambda b, i: (b * nq + i, 0)
    return pl.pallas_call(
        _nsa_kernel,
        grid=(batch, nq),
        in_specs=[
            pl.BlockSpec((tq, NSA_REP * LANES), qrow),
            pl.BlockSpec((GATE_ROWS, tq), lambda b, i: (0, b * nq + i)),
            pl.BlockSpec((1, n_blk, LANES), lambda b, i: (b, 0, 0)),
            pl.BlockSpec((1, LANES, n_blk), lambda b, i: (b, 0, 0)),
            pl.BlockSpec((seq, LANES), lambda b, i: (b, 3)),
            pl.BlockSpec((seq, LANES), lambda b, i: (b, 4)),
            pl.BlockSpec((LANES, seq), lambda b, i: (0, b)),
            pl.BlockSpec((LANES, seq), lambda b, i: (1, b)),
            pl.BlockSpec(ovl_t.shape, lambda b, i: (0, 0)),
        ],
        out_specs=pl.BlockSpec((tq, NSA_REP * LANES), qrow),
        out_shape=jax.ShapeDtypeStruct((batch * seq, NSA_REP * LANES), BF16),
        scratch_shapes=[pltpu.VMEM((NSA_REP * LANES, tq), F32),
                        pltpu.VMEM((NSA_KV_GROUPS, seq // SEL_BLOCK, tq), F32)],
        compiler_params=pltpu.CompilerParams(
            dimension_semantics=("parallel", "arbitrary"), vmem_limit_bytes=VMEM_LIMIT),
        name="nsa_attention",
    )(nsa, gate_t, kc, vct, nsa, nsa, nsa_t, nsa_t, ovl_t)


def _mla_attn_kernel(q_ref, k_ref, vt_ref, o_ref, out_t_ref):
    tq = ATT_T
    qi = pl.program_id(1)
    q0 = pl.multiple_of(qi * tq, tq)
    heads = range(MLA_HEADS)
    qh = [q_ref[:, h * LANES:(h + 1) * LANES] for h in heads]

    def chunk(k0, states, causal):
        if causal:
            mask = (lax.broadcasted_iota(jnp.int32, (tq, tq), 0)
                    <= lax.broadcasted_iota(jnp.int32, (tq, tq), 1))

        def score_fn(h):
            k = k_ref[pl.ds(k0, tq), h * LANES:(h + 1) * LANES]
            return _tri_scores(k, qh[h], True, mask) if causal else _nt(k, qh[h])

        v_ts = [vt_ref[h * MLA_V_DIM:(h + 1) * MLA_V_DIM, pl.ds(k0, tq)] for h in heads]
        return _softmax_steps([functools.partial(score_fn, h) for h in heads], v_ts, states,
                              functools.partial(_tri_pv, True) if causal else _nn)

    states = chunk(q0, tuple(_softmax_init(tq) for _ in heads), True)
    states = lax.fori_loop(0, qi, lambda c, st: chunk(pl.multiple_of(c * tq, tq), st, False), states)
    for h in heads:
        out_t_ref[h * MLA_V_DIM:(h + 1) * MLA_V_DIM, :] = _softmax_finish(states[h])
    out_t_ref[MLA_V_ROWS:, :] = jnp.zeros((RET_PAIRS * LANES - MLA_V_ROWS, tq), F32)
    for j in range(RET_PAIRS):
        o_ref[:, j * LANES:(j + 1) * LANES] = out_t_ref[j * LANES:(j + 1) * LANES, :].T.astype(BF16)


def _mla_attention(q, k, vt, batch, seq):
    tq = ATT_T
    nq = seq // tq
    return pl.pallas_call(
        _mla_attn_kernel,
        grid=(batch, nq),
        in_specs=[
            pl.BlockSpec((tq, MLA_HEADS * LANES), lambda b, i: (b * nq + i, 0)),
            pl.BlockSpec((seq, MLA_HEADS * LANES), lambda b, i: (b, 0)),
            pl.BlockSpec((MLA_V_ROWS, seq), lambda b, i: (0, b)),
        ],
        out_specs=pl.BlockSpec((tq, RET_PAIRS * LANES), lambda b, i: (b * nq + i, 0)),
        out_shape=jax.ShapeDtypeStruct((batch * seq, RET_PAIRS * LANES), BF16),
        scratch_shapes=[pltpu.VMEM((RET_PAIRS * LANES, tq), F32)],
        compiler_params=pltpu.CompilerParams(
            dimension_semantics=("parallel", "arbitrary"), vmem_limit_bytes=VMEM_LIMIT),
        name="mla_attention",
    )(q, k, vt)


RET_UNROLL = 16


def _retention_kernel(q_ref, k_ref, v_ref, gate_ref, gn_ref, intra_ref, rd_ref, wd_ref, cd_ref, o_ref,
                      kv_ref, st_ref):
    c_len = RET_CHUNK
    n_chunks = q_ref.shape[0] // c_len
    lo = _lane_lo((1, LANES))
    hi = jnp.logical_not(lo)
    blockdiag = (lax.broadcasted_iota(jnp.int32, (LANES, LANES), 0) < HALF) == _lane_lo((LANES, LANES))
    intra_a = intra_ref[0, 0]
    intra_b = intra_ref[0, 1]
    read_decay = rd_ref[0]
    write_decay = wd_ref[0]
    chunk_decay = cd_ref[0]
    gn = gn_ref[0]

    averager = jnp.where(blockdiag, 1.0 / HEAD_DIM, 0.0).astype(BF16)

    def half_mean(x):
        x_hi = x.astype(BF16)
        x_lo = (x - x_hi.astype(F32)).astype(BF16)
        return _nn(x_hi, averager) + _nn(x_lo, averager)

    def kv_body(c, carry):
        r0 = pl.multiple_of(c * c_len, c_len)
        kc = k_ref[pl.ds(r0, c_len), :]
        kv_ref[c] = _tn((kc.astype(F32) * write_decay).astype(BF16), v_ref[pl.ds(r0, c_len), :])
        return carry

    lax.fori_loop(0, n_chunks, kv_body, 0, unroll=RET_UNROLL)

    state = jnp.zeros((LANES, LANES), F32)
    for c in range(n_chunks):
        st_ref[c] = jnp.where(blockdiag, state, 0.0).astype(BF16)
        state = state * chunk_decay + kv_ref[c]

    def out_body(step, carry):
        cs = [step * RET_UNROLL + u for u in range(RET_UNROLL)]
        rows = [pl.ds(pl.multiple_of(c * c_len, c_len), c_len) for c in cs]
        qs = [q_ref[r, :] for r in rows]
        ks = [k_ref[r, :] for r in rows]
        vs = [v_ref[r, :] for r in rows]
        zero = jnp.zeros_like(qs[0])
        sa = [(_nt(jnp.where(lo, q, zero), k) * intra_a).astype(BF16) for q, k in zip(qs, ks)]
        sb = [(_nt(jnp.where(hi, q, zero), k) * intra_b).astype(BF16) for q, k in zip(qs, ks)]
        cross = [_nn(q, st_ref[c]) * read_decay for q, c in zip(qs, cs)]
        os_ = [jnp.where(lo, _nn(a, v), _nn(b, v)) + x for a, b, v, x in zip(sa, sb, vs, cross)]
        ds_ = [o - half_mean(o) for o in os_]
        ys = [d * lax.rsqrt(half_mean(d * d) + NORM_EPS) * gn for d in ds_]
        for r, y in zip(rows, ys):
            o_ref[r, :] = (jax.nn.silu(gate_ref[r, :]) * y).astype(BF16)
        return carry

    lax.fori_loop(0, n_chunks // RET_UNROLL, out_body, 0)


def _retention_all_kernel(q_ref, k_ref, v_ref, gate_ref, gn_ref, intra_ref, rd_ref, wd_ref, cd_ref, o_ref,
                          kv_ref, st_ref):
    for j in range(RET_PAIRS):
        cols = slice(j * LANES, (j + 1) * LANES)
        one = slice(j, j + 1)
        _retention_kernel(q_ref.at[:, cols], k_ref.at[:, cols], v_ref.at[:, cols], gate_ref.at[:, cols],
                          gn_ref.at[one], intra_ref.at[one], rd_ref.at[one], wd_ref.at[one], cd_ref.at[one],
                          o_ref.at[:, cols], kv_ref.at[j], st_ref.at[j])


def _retention(ret, rgate, gn, intra, rd, wd, cd, batch, seq):
    whole = lambda a: pl.BlockSpec(a.shape, lambda b: (0,) * a.ndim)
    width = RET_PAIRS * LANES
    return pl.pallas_call(
        _retention_all_kernel,
        grid=(batch,),
        in_specs=[
            pl.BlockSpec((seq, width), lambda b: (b, 0)),
            pl.BlockSpec((seq, width), lambda b: (b, 1)),
            pl.BlockSpec((seq, width), lambda b: (b, 2)),
            pl.BlockSpec((seq, width), lambda b: (b, 0)),
            whole(gn), whole(intra), whole(rd), whole(wd), whole(cd),
        ],
        out_specs=pl.BlockSpec((seq, width), lambda b: (b, 0)),
        out_shape=jax.ShapeDtypeStruct((batch * seq, width), BF16),
        scratch_shapes=[pltpu.VMEM((RET_PAIRS, seq // RET_CHUNK, LANES, LANES), F32),
                        pltpu.VMEM((RET_PAIRS, seq // RET_CHUNK, LANES, LANES), BF16)],
        compiler_params=pltpu.CompilerParams(
            dimension_semantics=("parallel",), vmem_limit_bytes=VMEM_LIMIT),
        name="retention",
    )(ret, ret, ret, rgate, gn, intra, rd, wd, cd)


def _out_mlp_kernel(x_ref, nsa_ref, mla_ref, ret_ref, wo_ref, g2_ref, wu_ref, wd_ref, gf_ref, o_ref,
                    *, final_norm):
    full = 2 * LANES
    mixed = jnp.concatenate([nsa_ref[...], mla_ref[:, :full], mla_ref[:, full:] + ret_ref[:, full:],
                             ret_ref[:, :full]], axis=1)
    x = x_ref[...] + _nn(mixed, wo_ref[...])
    h = _rms(x, g2_ref[...]).astype(BF16)
    y = x
    for c in range(D_FF // MLP_FF_CHUNK):
        sl = slice(c * MLP_FF_CHUNK, (c + 1) * MLP_FF_CHUNK)
        u = jnp.maximum(_nn(h, wu_ref[:, sl]), 0.0)
        y = y + _nn((u * u).astype(BF16), wd_ref[sl, :])
    if final_norm:
        y = _rms(y, gf_ref[...])
    o_ref[...] = y


def _out_mlp(x2, o_nsa, o_mla, o_ret, wo, g2, wu, wd, layer, gf, final_norm):
    t = x2.shape[0]
    w = RET_PAIRS * LANES
    row = lambda i: (i, 0)
    const = lambda i: (0, 0)
    resident = dict(pipeline_mode=pl.Buffered(1))
    return pl.pallas_call(
        functools.partial(_out_mlp_kernel, final_norm=final_norm),
        grid=(t // MLP_TM,),
        in_specs=[
            pl.BlockSpec((MLP_TM, D_MODEL), row),
            pl.BlockSpec((MLP_TM, w), row),
            pl.BlockSpec((MLP_TM, w), row),
            pl.BlockSpec((MLP_TM, w), row),
            _layer_spec(wo, layer, **resident),
            pl.BlockSpec((1, D_MODEL), const),
            _layer_spec(wu, layer, **resident),
            _layer_spec(wd, layer, **resident),
            pl.BlockSpec((1, D_MODEL), const),
        ],
        out_specs=pl.BlockSpec((MLP_TM, D_MODEL), row),
        out_shape=jax.ShapeDtypeStruct((t, D_MODEL), F32),
        compiler_params=pltpu.CompilerParams(
            dimension_semantics=("parallel",), vmem_limit_bytes=VMEM_LIMIT),
        name="out_mlp",
    )(x2, o_nsa, o_mla, o_ret, wo, g2, wu, wd, gf)


ROPE_KINDS = (
    (PARTIAL_ROPE_DIM, ROPE_THETA, HEAD_DIM, 0),
    (MLA_ROPE_DIM, ROPE_THETA, LANES, HALF),
    (HEAD_DIM, RET_THETA, HEAD_DIM, 0),
)
ROPE_TM = 2048


def _rope_placement():
    n_angles = sum(dim // 2 for dim, _, _, _ in ROPE_KINDS)
    assert 2 * n_angles <= LANES
    place = np.zeros((LANES, 3 * LANES * len(ROPE_KINDS)), np.float32)
    fill = np.zeros((1, 3 * LANES * len(ROPE_KINDS)), np.float32)
    row0 = 0
    for kind, (dim, _, period, base) in enumerate(ROPE_KINDS):
        half = dim // 2
        col0 = kind * 3 * LANES
        for lane in range(LANES):
            rel = (lane - base) % period
            first = lane >= base and rel < half
            second = lane >= base and half <= rel < dim
            if first or second:
                angle = rel if first else rel - half
                place[row0 + angle, col0 + lane] = 1.0
                place[n_angles + row0 + angle, col0 + (1 if first else 2) * LANES + lane] = -1.0 if first else 1.0
            else:
                fill[0, col0 + lane] = 1.0
        row0 += half
    return jnp.asarray(place, BF16), jnp.asarray(fill)


def _rope_kernel(cs_ref, place_ref, fill_ref, *out_refs):
    x = cs_ref[...]
    x1 = x.astype(BF16)
    r1 = x - x1.astype(F32)
    x2 = r1.astype(BF16)
    x3 = (r1 - x2.astype(F32)).astype(BF16)
    place = place_ref[...]
    tab = _tn(x1, place) + _tn(x2, place) + _tn(x3, place) + fill_ref[...]
    for i, ref in enumerate(out_refs):
        ref[...] = tab[:, i * 3 * LANES:(i + 1) * 3 * LANES]


def _rope_tables(positions):
    inv = jnp.concatenate([1.0 / (theta ** (jnp.arange(0, dim, 2, dtype=F32) / dim))
                           for dim, theta, _, _ in ROPE_KINDS])
    ang = inv[:, None] * positions.reshape(-1).astype(F32)[None, :]
    compact = jnp.concatenate([jnp.cos(ang), jnp.sin(ang)], axis=0)
    compact = jnp.pad(compact, ((0, LANES - compact.shape[0]), (0, 0)))
    place, fill = _rope_placement()
    t = compact.shape[1]
    tab_shape = jax.ShapeDtypeStruct((t, 3 * LANES), F32)
    return pl.pallas_call(
        _rope_kernel,
        grid=(t // ROPE_TM,),
        in_specs=[pl.BlockSpec((LANES, ROPE_TM), lambda i: (0, i)),
                  pl.BlockSpec(place.shape, lambda i: (0, 0)),
                  pl.BlockSpec(fill.shape, lambda i: (0, 0))],
        out_specs=tuple(pl.BlockSpec((ROPE_TM, 3 * LANES), lambda i: (i, 0)) for _ in ROPE_KINDS),
        out_shape=tuple(tab_shape for _ in ROPE_KINDS),
        compiler_params=pltpu.CompilerParams(
            dimension_semantics=("parallel",), vmem_limit_bytes=VMEM_LIMIT),
        name="rope_tables",
    )(compact, place, fill)


def _in_weight(w_in):
    offs = np.cumsum((0,) + IN_SIZES)
    (nsa_q, k_cmp, v_cmp, k_slc, v_slc, k_win, v_win, gate,
     cq, ckv, kpe, ret_q, ret_k, ret_v, ret_g) = [np.arange(offs[i], offs[i + 1]) for i in range(len(IN_SIZES))]
    scale = HEAD_DIM ** -0.5
    head = lambda cols, h: cols[h * HEAD_DIM:(h + 1) * HEAD_DIM]
    pad = lambda n: None if n == 0 else -np.ones(n, np.int64)
    pieces = []
    for j in range(NSA_REP):
        pieces += [(head(nsa_q, j), scale), (head(nsa_q, j + NSA_REP), scale)]
    pieces += [(c, 1.0) for c in (k_slc, k_win, k_cmp, v_cmp, cq, ckv)]
    pieces += [(pad(HALF), 0.0), (kpe, 1.0), (pad(LANES - HALF - MLA_ROPE_DIM), 0.0)]
    slots = lambda cols, f, fill: [fill if h is None else (head(cols, h), f) for h in RET_SLOTS]
    zero = (pad(HEAD_DIM), 0.0)
    pieces += slots(ret_q, 1.0, (head(ret_k, RET_SLOTS[-1]), scale))
    pieces += slots(ret_k, scale, zero)[:2 * RET_K_TILES]
    pieces += slots(ret_v, 1.0, zero) + slots(ret_g, 1.0, zero)
    t_pieces = [(v_slc, 1.0), (v_win, 1.0), (gate, 1.0), (pad(GATE_ROWS - NSA_GATE_W), 0.0)]

    t_pieces += [(pad(RET_PAIRS * LANES - NSA_T_ROWS), 0.0)]
    spans = lambda parts: tuple((None if c[0] < 0 else int(c[0]), len(c), f) for c, f in parts)
    assert sum(n for _, n, _ in spans(pieces)) == N_PAD
    nl, d_model, n_in = w_in.shape
    return pl.pallas_call(
        functools.partial(_relayout_kernel, pieces=spans(pieces), t_pieces=spans(t_pieces)),
        grid=(nl, d_model // RELAYOUT_TM),
        in_specs=[pl.BlockSpec((None, RELAYOUT_TM, n_in), lambda l, i: (l, i, 0))],
        out_specs=(pl.BlockSpec((None, RELAYOUT_TM, N_PAD), lambda l, i: (l, i, 0)),
                   pl.BlockSpec((None, NSA_T_ROWS, RELAYOUT_TM), lambda l, i: (l, 0, i))),
        out_shape=(jax.ShapeDtypeStruct((nl, d_model, N_PAD), BF16),
                   jax.ShapeDtypeStruct((nl, NSA_T_ROWS, d_model), BF16)),
        compiler_params=pltpu.CompilerParams(
            dimension_semantics=("parallel", "parallel"), vmem_limit_bytes=VMEM_LIMIT),
        name="in_weight_layout",
    )(w_in)


RELAYOUT_TM = 256


def _relayout_kernel(w_ref, o_ref, ot_ref, *, pieces, t_pieces):
    w = w_ref[...]

    def build(parts):
        cols = []
        for src, n, factor in parts:
            if src is None:
                cols.append(jnp.zeros((w.shape[0], n), F32))
            else:
                cols.append(w[:, src:src + n] if factor == 1.0 else w[:, src:src + n] * factor)
        return jnp.concatenate(cols, axis=1)

    o_ref[...] = build(pieces).astype(BF16)
    ot_ref[...] = build(t_pieces).T[0:NSA_T_ROWS].astype(BF16)


def _compress_weights(pos, w1, w2):
    nl = pos.shape[0]
    g, dh, hid = NSA_KV_GROUPS, HEAD_DIM, CMP_HIDDEN
    assert g == 2
    p = jnp.tile(pos, (1, 1, g))

    def block_diag2(w):
        lead = ((0, 0),) * (w.ndim - 1)
        return jnp.concatenate([jnp.pad(w, lead + ((0, w.shape[-1]),)),
                                jnp.pad(w, lead + ((w.shape[-1], 0),))], axis=-2)

    w = block_diag2(w1.astype(BF16).reshape(nl, CMP_LEN, dh, hid))
    return p, w, block_diag2(w2.astype(BF16))


def _mla_weights(w_uq, w_ukv):
    nl = w_uq.shape[0]
    dq = MLA_NOPE_DIM + MLA_ROPE_DIM
    wq = w_uq.reshape(nl, MLA_Q_RANK, MLA_HEADS, dq)
    pe = wq[..., MLA_NOPE_DIM:]
    half = MLA_ROPE_DIM // 2
    rot = jnp.concatenate([jnp.zeros_like(wq[..., :MLA_NOPE_DIM]), -pe[..., half:], pe[..., :half]], axis=-1)
    pad = lambda w: jnp.pad(w, ((0, 0), (0, 0), (0, 0), (0, LANES - dq))).reshape(
        nl, MLA_Q_RANK, MLA_HEADS * LANES)
    wq = jnp.concatenate([pad(wq), pad(rot)], axis=-1)
    wkv = w_ukv.reshape(nl, MLA_KV_RANK, MLA_HEADS, MLA_NOPE_DIM + MLA_V_DIM)
    wk = jnp.pad(wkv[..., :MLA_NOPE_DIM], ((0, 0), (0, 0), (0, 0), (0, LANES - MLA_NOPE_DIM)))
    wk = wk.reshape(nl, MLA_KV_RANK, MLA_HEADS * LANES)
    wvt = wkv[..., MLA_NOPE_DIM:].reshape(nl, MLA_KV_RANK, MLA_V_ROWS).transpose(0, 2, 1)
    return wq.astype(BF16), wk.astype(BF16), wvt.astype(BF16)


def _out_weight(w_out):
    nl = w_out.shape[0]
    nsa = w_out[:, :NSA_Q_W].reshape(nl, NSA_HEADS, HEAD_DIM, D_MODEL)
    order = [h for j in range(NSA_REP) for h in (j, j + NSA_REP)]
    nsa = nsa[:, order].reshape(nl, NSA_Q_W, D_MODEL)
    mla = w_out[:, NSA_Q_W:NSA_Q_W + MLA_HEADS * MLA_V_DIM]
    ret = w_out[:, NSA_Q_W + MLA_HEADS * MLA_V_DIM:]
    full = 2 * LANES
    return jnp.concatenate([nsa, mla[:, :full], mla[:, full:], ret[:, full:], ret[:, :full]],
                           axis=1).astype(BF16)


def _retention_tables(gn_gain):
    nh = 2 * RET_PAIRS
    slot_head = np.array([RET_HEADS if h is None else h for h in RET_SLOTS])
    log_g = jnp.log(1.0 - 2.0 ** (-5.0 - jnp.asarray(slot_head, F32)))
    i = jnp.arange(RET_CHUNK, dtype=F32)
    diff = i[:, None] - i[None, :]
    intra = jnp.where(diff >= 0, jnp.exp(jnp.maximum(diff, 0.0)[None] * log_g[:, None, None]), 0.0)
    read_decay = jnp.exp((i + 1.0)[None, :] * log_g[:, None])
    write_decay = jnp.exp((RET_CHUNK - 1.0 - i)[None, :] * log_g[:, None])
    chunk_decay = jnp.exp(RET_CHUNK * log_g)

    def lanes(t):
        t = t.reshape(RET_PAIRS, 2, -1)
        return jnp.repeat(t.transpose(0, 2, 1), HALF, axis=-1)

    gn = jnp.pad(gn_gain, ((0, 0), (0, nh - RET_HEADS), (0, 0)))[:, slot_head]
    gn = gn.reshape(gn.shape[0], RET_PAIRS, 1, LANES)
    return (gn, intra.reshape(RET_PAIRS, 2, RET_CHUNK, RET_CHUNK), lanes(read_decay), lanes(write_decay),
            lanes(chunk_decay[:, None]))


def _selection_overlap(seq):
    n_cmp = (seq - CMP_LEN) // CMP_STRIDE + 1
    n_sel = seq // SEL_BLOCK
    cs = np.arange(n_cmp) * CMP_STRIDE
    ss = np.arange(n_sel) * SEL_BLOCK
    ov = np.clip(np.minimum(cs[:, None] + CMP_LEN, ss[None, :] + SEL_BLOCK)
                 - np.maximum(cs[:, None], ss[None, :]), 0, None) / CMP_LEN
    ovl_t = np.zeros((n_sel, seq // CMP_STRIDE), np.float32)
    ovl_t[:, :n_cmp] = ov.T
    return jnp.asarray(ovl_t, BF16)


def kernel(x, positions, ln1_gain, w_in, cmp_pos_k, cmp_w1_k, cmp_w2_k, cmp_pos_v, cmp_w1_v, cmp_w2_v,
           mla_q_norm, mla_w_uq, mla_kv_norm, mla_w_ukv, ret_gn_gain, w_out, ln2_gain, w_up, w_down,
           final_gain):
    batch, seq, _ = x.shape
    depth = w_in.shape[0]
    t = batch * seq

    tab_n, tab_m, tab_r = _rope_tables(positions)

    w_in_p, w_in_t = _in_weight(w_in)
    pos_k, w1_k, w2_k = _compress_weights(cmp_pos_k, cmp_w1_k, cmp_w2_k)
    pos_v, w1_v, w2_v = _compress_weights(cmp_pos_v, cmp_w1_v, cmp_w2_v)
    cmp_pos = jnp.stack([pos_k, pos_v], axis=1)
    cmp_w1 = jnp.stack([w1_k, w1_v], axis=1)
    w2_vt = w2_v.transpose(0, 2, 1)
    wq, wk, wvt = _mla_weights(mla_w_uq, mla_w_ukv)
    wo = _out_weight(w_out)
    wu = w_up.astype(BF16)
    wd = w_down.astype(BF16)
    gn, intra, rd, wdec, cd = _retention_tables(ret_gn_gain)
    ovl_t = _selection_overlap(seq)
    gf = final_gain.reshape(1, D_MODEL)

    x2 = x.reshape(t, D_MODEL)
    for l in range(depth):
        nsa, nsa_t, gate_t, k_cmp, v_cmp, ret, rgate, q_m, k_m, vt_m = _in_proj(
            x2, ln1_gain[l].reshape(1, D_MODEL), w_in_p, w_in_t, mla_q_norm[l].reshape(1, -1),
            mla_kv_norm[l].reshape(1, -1), wq, wk, wvt, l, tab_n, tab_m, tab_r)
        kc, vct = _compress(k_cmp, v_cmp, cmp_pos[l], cmp_w1, w2_k[l], w2_vt[l], l, batch, seq)
        o_nsa = _nsa_attention(nsa, nsa_t, gate_t, kc, vct, ovl_t, batch, seq)
        o_mla = _mla_attention(q_m, k_m, vt_m, batch, seq)
        o_ret = _retention(ret, rgate, gn[l], intra, rd, wdec, cd, batch, seq)
        x2 = _out_mlp(x2, o_nsa, o_mla, o_ret, wo, ln2_gain[l].reshape(1, D_MODEL), wu, wd, l, gf,
                      final_norm=(l == depth - 1))
    return x2.reshape(batch, seq, D_MODEL)
```
